```python
import jax
import jax.numpy as jnp
from jax import lax
import numpy as np


D_MODEL = 1024
BATCH = 8
SEQ = 4096
DEPTH = 2

HEAD_DIM = 64
A_WIDTH = D_MODEL // 2
A_HEADS = A_WIDTH // HEAD_DIM
CHUNK = 128
B_WIDTH = D_MODEL // 2
CONV_WIDTH = 31
IN_WIDTH = 2 * A_WIDTH + 2 * B_WIDTH
ATTN_HEADS = D_MODEL // HEAD_DIM
DILATED_PATTERNS = ((128, 1), (512, 4), (2048, 16))
BLOCK = 128
ROPE_THETA = 10000.0
D_FF = 2816
FFN_CONV_WIDTH = 3
EPS = 1e-6
NEG = -1e30

kernel_name = 'hybrid_gmlp_conformer_dilated_attn_block'


def rms_norm(x, g):
    xf = x.astype(jnp.float32)
    y = xf * lax.rsqrt(jnp.mean(xf * xf, axis=-1, keepdims=True) + EPS)
    return (y * g.astype(jnp.float32)).astype(x.dtype)


def layer_norm(x, g, b):
    xf = x.astype(jnp.float32)
    mu = jnp.mean(xf, axis=-1, keepdims=True)
    var = jnp.mean(jnp.square(xf - mu), axis=-1, keepdims=True)
    y = (xf - mu) * lax.rsqrt(var + EPS)
    return (y * g.astype(jnp.float32) + b.astype(jnp.float32)).astype(x.dtype)


def causal_dwconv(x, w, b):
    k = w.shape[0]
    y = lax.conv_general_dilated(
        x, w[:, None, :].astype(x.dtype), window_strides=(1,),
        padding=[(k - 1, 0)], dimension_numbers=('NWC', 'WIO', 'NWC'),
        feature_group_count=x.shape[-1])
    return y + b.astype(x.dtype)


def rotary(x, pos):
    half = x.shape[-1] // 2
    inv = ROPE_THETA ** (-jnp.arange(half, dtype=jnp.float32) / half)
    ang = pos.astype(jnp.float32)[:, None] * inv[None, :]
    cos = jnp.cos(ang)[None, :, None, :]
    sin = jnp.sin(ang)[None, :, None, :]
    xf = x.astype(jnp.float32)
    x1, x2 = xf[..., :half], xf[..., half:]
    out = jnp.concatenate([x1 * cos - x2 * sin, x2 * cos + x1 * sin], axis=-1)
    return out.astype(x.dtype)


def chunked_spatial_gating(z, ln_g, ln_b, w_s, b_s):
    u, v = jnp.split(z, 2, axis=-1)
    v = layer_norm(v, ln_g, ln_b)
    bn, s, _ = v.shape
    v = v.reshape(bn, s // CHUNK, CHUNK, A_HEADS, HEAD_DIM)
    causal = jnp.tril(jnp.ones((CHUNK, CHUNK), dtype=bool))
    w = jnp.where(causal, w_s, jnp.zeros_like(w_s))
    mixed = jnp.einsum('hts,bcshd->bcthd', w, v) + b_s.T[None, None, :, :, None]
    return u * mixed.reshape(bn, s, A_WIDTH)


def conformer_conv(z, conv_w, conv_b, ln_g, ln_b):
    a, g = jnp.split(z, 2, axis=-1)
    h = a * jax.nn.sigmoid(g)
    h = causal_dwconv(h, conv_w, conv_b)
    h = layer_norm(h, ln_g, ln_b)
    return jax.nn.silu(h)


def dilated_branch(q, k, v, window, dilation):
    bn, s, h, dh = q.shape
    span = dilation * BLOCK
    s_pad = -(-s // span) * span
    seq_len = s_pad // dilation
    nb = seq_len // BLOCK
    reach = window // dilation

    def strided(t):
        t = jnp.pad(t, [(0, 0), (0, s_pad - s), (0, 0), (0, 0)])
        t = t.reshape(bn, seq_len, dilation, h, dh).transpose(0, 2, 1, 3, 4)
        return t.reshape(bn, dilation, nb, BLOCK, h, dh)

    def with_prev(t):
        prev = jnp.pad(t, [(0, 0), (0, 0), (1, 0), (0, 0), (0, 0), (0, 0)])[:, :, :-1]
        return jnp.concatenate([prev, t], axis=3)

    qs = strided(q)
    kw = with_prev(strided(k))
    vw = with_prev(strided(v))
    qi = jnp.arange(BLOCK)[:, None]
    kj = jnp.arange(2 * BLOCK)[None, :]
    dist = BLOCK + qi - kj
    band = (dist >= 0) & (dist <= reach)
    blk = jnp.arange(nb)[:, None, None]
    valid = band[None] & ((blk > 0) | (kj[None] >= BLOCK))

    scores = jnp.einsum('brnqhd,brnkhd->brnhqk', qs, kw).astype(jnp.float32)
    scores = scores * (HEAD_DIM ** -0.5)
    scores = jnp.where(valid[None, None, :, None], scores, NEG)
    m = jnp.max(scores, axis=-1, keepdims=True)
    p = jnp.exp(scores - m)
    l = jnp.sum(p, axis=-1, keepdims=True)
    o = jnp.einsum('brnhqk,brnkhd->brnqhd', p / l, vw.astype(jnp.float32))
    lse = (m + jnp.log(l))[..., 0]

    o = o.reshape(bn, dilation, seq_len, h, dh).transpose(0, 2, 1, 3, 4)
    o = o.reshape(bn, s_pad, h, dh)[:, :s]
    lse = lse.transpose(0, 1, 2, 4, 3).reshape(bn, dilation, seq_len, h)
    lse = lse.transpose(0, 2, 1, 3).reshape(bn, s_pad, h)[:, :s]
    return o, lse


def dilated_attention(q, k, v):
    outs, lses = [], []
    for window, dilation in DILATED_PATTERNS:
        o, lse = dilated_branch(q, k, v, window, dilation)
        outs.append(o)
        lses.append(lse)
    wts = jax.nn.softmax(jnp.stack(lses, axis=0), axis=0)
    return jnp.einsum('pbsh,pbshd->bshd', wts, jnp.stack(outs, axis=0))


def conv_ffn(x, norm_g, w_up, conv_w, conv_b, w_down):
    h = rms_norm(x, norm_g)
    u = causal_dwconv(h @ w_up, conv_w, conv_b)
    gate, val = jnp.split(u, 2, axis=-1)
    return x + (jax.nn.silu(gate) * val) @ w_down


def _fwd_setup_inputs(seed: int = 0) -> dict:
    key = jax.random.key(seed)
    keys = iter(jax.random.split(key, 32))
    n_even = (DEPTH + 1) // 2
    n_odd = DEPTH // 2
    f32 = jnp.float32

    def nrm(shape, scale):
        return jax.random.normal(next(keys), shape, f32) * scale

    def gain(shape):
        return 1.0 + nrm(shape, 0.02)

    return {
        'x': nrm((BATCH, SEQ, D_MODEL), 1.0),
        'even_norm_g': gain((n_even, D_MODEL)),
        'even_w_in': nrm((n_even, D_MODEL, IN_WIDTH), D_MODEL ** -0.5),
        'even_b_in': nrm((n_even, IN_WIDTH), 0.02),
        'even_v_ln_g': gain((n_even, A_WIDTH)),
        'even_v_ln_b': nrm((n_even, A_WIDTH), 0.02),
        'even_w_s': nrm((n_even, A_HEADS, CHUNK, CHUNK), CHUNK ** -0.5),
        'even_b_s': gain((n_even, A_HEADS, CHUNK)),
        'even_conv_w': nrm((n_even, CONV_WIDTH, B_WIDTH), CONV_WIDTH ** -0.5),
        'even_conv_b': nrm((n_even, B_WIDTH), 0.02),
        'even_conv_ln_g': gain((n_even, B_WIDTH)),
        'even_conv_ln_b': nrm((n_even, B_WIDTH), 0.02),
        'even_w_out': nrm((n_even, A_WIDTH + B_WIDTH, D_MODEL), (A_WIDTH + B_WIDTH) ** -0.5),
        'odd_norm_g': gain((n_odd, D_MODEL)),
        'odd_w_qkv': nrm((n_odd, D_MODEL, 3 * D_MODEL), D_MODEL ** -0.5),
        'odd_w_o': nrm((n_odd, D_MODEL, D_MODEL), D_MODEL ** -0.5),
        'ffn_norm_g': gain((DEPTH, D_MODEL)),
        'ffn_w_up': nrm((DEPTH, D_MODEL, 2 * D_FF), D_MODEL ** -0.5),
        'ffn_conv_w': nrm((DEPTH, FFN_CONV_WIDTH, 2 * D_FF), FFN_CONV_WIDTH ** -0.5),
        'ffn_conv_b': nrm((DEPTH, 2 * D_FF), 0.02),
        'ffn_w_down': nrm((DEPTH, D_FF, D_MODEL), D_FF ** -0.5),
        'final_norm_g': gain((D_MODEL,)),
    }


def _fwd_reference(x, even_norm_g, even_w_in, even_b_in, even_v_ln_g, even_v_ln_b,
              even_w_s, even_b_s, even_conv_w, even_conv_b, even_conv_ln_g,
              even_conv_ln_b, even_w_out, odd_norm_g, odd_w_qkv, odd_w_o,
              ffn_norm_g, ffn_w_up, ffn_conv_w, ffn_conv_b, ffn_w_down,
              final_norm_g):
    bn, s, _ = x.shape
    pos = jnp.arange(s)
    for i in range(DEPTH):
        j = i // 2
        if i % 2 == 0:
            h = rms_norm(x, even_norm_g[j])
            z = h @ even_w_in[j] + even_b_in[j]
            za = jax.nn.gelu(z[..., :2 * A_WIDTH])
            zb = z[..., 2 * A_WIDTH:]
            ya = chunked_spatial_gating(za, even_v_ln_g[j], even_v_ln_b[j],
                                        even_w_s[j], even_b_s[j])
            yb = conformer_conv(zb, even_conv_w[j], even_conv_b[j],
                                even_conv_ln_g[j], even_conv_ln_b[j])
            x = x + jnp.concatenate([ya, yb], axis=-1) @ even_w_out[j]
        else:
            h = rms_norm(x, odd_norm_g[j])
            qkv = (h @ odd_w_qkv[j]).reshape(bn, s, 3, ATTN_HEADS, HEAD_DIM)
            q = rotary(qkv[:, :, 0], pos)
            k = rotary(qkv[:, :, 1], pos)
            v = qkv[:, :, 2]
            o = dilated_attention(q, k, v).astype(x.dtype).reshape(bn, s, D_MODEL)
            x = x + o @ odd_w_o[j]
        x = conv_ffn(x, ffn_norm_g[i], ffn_w_up[i], ffn_conv_w[i], ffn_conv_b[i],
                     ffn_w_down[i])
    return rms_norm(x, final_norm_g)


import jax as _jax
import jax.numpy as _jnp

TWIN_FORMAT = 'train_step'
FWD_PARAMS = ['x', 'even_norm_g', 'even_w_in', 'even_b_in', 'even_v_ln_g', 'even_v_ln_b', 'even_w_s', 'even_b_s', 'even_conv_w', 'even_conv_b', 'even_conv_ln_g', 'even_conv_ln_b', 'even_w_out', 'odd_norm_g', 'odd_w_qkv', 'odd_w_o', 'ffn_norm_g', 'ffn_w_up', 'ffn_conv_w', 'ffn_conv_b', 'ffn_w_down', 'final_norm_g']
TWIN_WEIGHTS = ['even_norm_g', 'even_w_in', 'even_b_in', 'even_v_ln_g', 'even_v_ln_b', 'even_w_s', 'even_b_s', 'even_conv_w', 'even_conv_b', 'even_conv_ln_g', 'even_conv_ln_b', 'even_w_out', 'odd_norm_g', 'odd_w_qkv', 'odd_w_o', 'ffn_norm_g', 'ffn_w_up', 'ffn_conv_w', 'ffn_conv_b', 'ffn_w_down', 'final_norm_g']
TWIN_DIFF_INPUT = 'x'
TWIN_INPUTS = ['x', 'even_norm_g', 'even_w_in', 'even_b_in', 'even_v_ln_g', 'even_v_ln_b', 'even_w_s', 'even_b_s', 'even_conv_w', 'even_conv_b', 'even_conv_ln_g', 'even_conv_ln_b', 'even_w_out', 'odd_norm_g', 'odd_w_qkv', 'odd_w_o', 'ffn_norm_g', 'ffn_w_up', 'ffn_conv_w', 'ffn_conv_b', 'ffn_w_down', 'final_norm_g', 'loss_target', 'm_even_norm_g', 'm_even_w_in', 'm_even_b_in', 'm_even_v_ln_g', 'm_even_v_ln_b', 'm_even_w_s', 'm_even_b_s', 'm_even_conv_w', 'm_even_conv_b', 'm_even_conv_ln_g', 'm_even_conv_ln_b', 'm_even_w_out', 'm_odd_norm_g', 'm_odd_w_qkv', 'm_odd_w_o', 'm_ffn_norm_g', 'm_ffn_w_up', 'm_ffn_conv_w', 'm_ffn_conv_b', 'm_ffn_w_down', 'm_final_norm_g', 'v_even_norm_g', 'v_even_w_in', 'v_even_b_in', 'v_even_v_ln_g', 'v_even_v_ln_b', 'v_even_w_s', 'v_even_b_s', 'v_even_conv_w', 'v_even_conv_b', 'v_even_conv_ln_g', 'v_even_conv_ln_b', 'v_even_w_out', 'v_odd_norm_g', 'v_odd_w_qkv', 'v_odd_w_o', 'v_ffn_norm_g', 'v_ffn_w_up', 'v_ffn_conv_w', 'v_ffn_conv_b', 'v_ffn_w_down', 'v_final_norm_g']
TWIN_OUTPUTS = ['loss', 'grad_x', 'grad_even_norm_g', 'grad_even_w_in', 'grad_even_b_in', 'grad_even_v_ln_g', 'grad_even_v_ln_b', 'grad_even_w_s', 'grad_even_b_s', 'grad_even_conv_w', 'grad_even_conv_b', 'grad_even_conv_ln_g', 'grad_even_conv_ln_b', 'grad_even_w_out', 'grad_odd_norm_g', 'grad_odd_w_qkv', 'grad_odd_w_o', 'grad_ffn_norm_g', 'grad_ffn_w_up', 'grad_ffn_conv_w', 'grad_ffn_conv_b', 'grad_ffn_w_down', 'grad_final_norm_g', 'delta_even_norm_g', 'delta_even_w_in', 'delta_even_b_in', 'delta_even_v_ln_g', 'delta_even_v_ln_b', 'delta_even_w_s', 'delta_even_b_s', 'delta_even_conv_w', 'delta_even_conv_b', 'delta_even_conv_ln_g', 'delta_even_conv_ln_b', 'delta_even_w_out', 'delta_odd_norm_g', 'delta_odd_w_qkv', 'delta_odd_w_o', 'delta_ffn_norm_g', 'delta_ffn_w_up', 'delta_ffn_conv_w', 'delta_ffn_conv_b', 'delta_ffn_w_down', 'delta_final_norm_g', 'new_m_even_norm_g', 'new_m_even_w_in', 'new_m_even_b_in', 'new_m_even_v_ln_g', 'new_m_even_v_ln_b', 'new_m_even_w_s', 'new_m_even_b_s', 'new_m_even_conv_w', 'new_m_even_conv_b', 'new_m_even_conv_ln_g', 'new_m_even_conv_ln_b', 'new_m_even_w_out', 'new_m_odd_norm_g', 'new_m_odd_w_qkv', 'new_m_odd_w_o', 'new_m_ffn_norm_g', 'new_m_ffn_w_up', 'new_m_ffn_conv_w', 'new_m_ffn_conv_b', 'new_m_ffn_w_down', 'new_m_final_norm_g', 'new_v_even_norm_g', 'new_v_even_w_in', 'new_v_even_b_in', 'new_v_even_v_ln_g', 'new_v_even_v_ln_b', 'new_v_even_w_s', 'new_v_even_b_s', 'new_v_even_conv_w', 'new_v_even_conv_b', 'new_v_even_conv_ln_g', 'new_v_even_conv_ln_b', 'new_v_even_w_out', 'new_v_odd_norm_g', 'new_v_odd_w_qkv', 'new_v_odd_w_o', 'new_v_ffn_norm_g', 'new_v_ffn_w_up', 'new_v_ffn_conv_w', 'new_v_ffn_conv_b', 'new_v_ffn_w_down', 'new_v_final_norm_g']
TWIN_LEAF_KINDS = {'loss': 'loss', 'grad_x': 'grad_x', 'grad_even_norm_g': 'grad_w', 'grad_even_w_in': 'grad_w', 'grad_even_b_in': 'grad_w', 'grad_even_v_ln_g': 'grad_w', 'grad_even_v_ln_b': 'grad_w', 'grad_even_w_s': 'grad_w', 'grad_even_b_s': 'grad_w', 'grad_even_conv_w': 'grad_w', 'grad_even_conv_b': 'grad_w', 'grad_even_conv_ln_g': 'grad_w', 'grad_even_conv_ln_b': 'grad_w', 'grad_even_w_out': 'grad_w', 'grad_odd_norm_g': 'grad_w', 'grad_odd_w_qkv': 'grad_w', 'grad_odd_w_o': 'grad_w', 'grad_ffn_norm_g': 'grad_w', 'grad_ffn_w_up': 'grad_w', 'grad_ffn_conv_w': 'grad_w', 'grad_ffn_conv_b': 'grad_w', 'grad_ffn_w_down': 'grad_w', 'grad_final_norm_g': 'grad_w', 'delta_even_norm_g': 'delta_w', 'delta_even_w_in': 'delta_w', 'delta_even_b_in': 'delta_w', 'delta_even_v_ln_g': 'delta_w', 'delta_even_v_ln_b': 'delta_w', 'delta_even_w_s': 'delta_w', 'delta_even_b_s': 'delta_w', 'delta_even_conv_w': 'delta_w', 'delta_even_conv_b': 'delta_w', 'delta_even_conv_ln_g': 'delta_w', 'delta_even_conv_ln_b': 'delta_w', 'delta_even_w_out': 'delta_w', 'delta_odd_norm_g': 'delta_w', 'delta_odd_w_qkv': 'delta_w', 'delta_odd_w_o': 'delta_w', 'delta_ffn_norm_g': 'delta_w', 'delta_ffn_w_up': 'delta_w', 'delta_ffn_conv_w': 'delta_w', 'delta_ffn_conv_b': 'delta_w', 'delta_ffn_w_down': 'delta_w', 'delta_final_norm_g': 'delta_w', 'new_m_even_norm_g': 'new_m', 'new_m_even_w_in': 'new_m', 'new_m_even_b_in': 'new_m', 'new_m_even_v_ln_g': 'new_m', 'new_m_even_v_ln_b': 'new_m', 'new_m_even_w_s': 'new_m', 'new_m_even_b_s': 'new_m', 'new_m_even_conv_w': 'new_m', 'new_m_even_conv_b': 'new_m', 'new_m_even_conv_ln_g': 'new_m', 'new_m_even_conv_ln_b': 'new_m', 'new_m_even_w_out': 'new_m', 'new_m_odd_norm_g': 'new_m', 'new_m_odd_w_qkv': 'new_m', 'new_m_odd_w_o': 'new_m', 'new_m_ffn_norm_g': 'new_m', 'new_m_ffn_w_up': 'new_m', 'new_m_ffn_conv_w': 'new_m', 'new_m_ffn_conv_b': 'new_m', 'new_m_ffn_w_down': 'new_m', 'new_m_final_norm_g': 'new_m', 'new_v_even_norm_g': 'new_v', 'new_v_even_w_in': 'new_v', 'new_v_even_b_in': 'new_v', 'new_v_even_v_ln_g': 'new_v', 'new_v_even_v_ln_b': 'new_v', 'new_v_even_w_s': 'new_v', 'new_v_even_b_s': 'new_v', 'new_v_even_conv_w': 'new_v', 'new_v_even_conv_b': 'new_v', 'new_v_even_conv_ln_g': 'new_v', 'new_v_even_conv_ln_b': 'new_v', 'new_v_even_w_out': 'new_v', 'new_v_odd_norm_g': 'new_v', 'new_v_odd_w_qkv': 'new_v', 'new_v_odd_w_o': 'new_v', 'new_v_ffn_norm_g': 'new_v', 'new_v_ffn_w_up': 'new_v', 'new_v_ffn_conv_w': 'new_v', 'new_v_ffn_conv_b': 'new_v', 'new_v_ffn_w_down': 'new_v', 'new_v_final_norm_g': 'new_v'}


def _forward(args):
    return _fwd_reference(*[args[k] for k in FWD_PARAMS])


def _output_shape():
    def fwd():
        inp = _fwd_setup_inputs(0)
        return _fwd_reference(*[inp[k] for k in FWD_PARAMS])
    out = _jax.eval_shape(fwd)
    return out.shape, out.dtype

N_MICROBATCH = 1
ADAM_LR = 0.001
ADAM_B1 = 0.9
ADAM_B2 = 0.999
ADAM_EPS = 1e-08
ADAM_WD = 0.01
ADAM_STEP = 10
PER_EXAMPLE_BATCH_AXIS = {'x': 0, 'loss_target': 0}
SHARED_INPUTS = []
_WEIGHT_DTYPES = {'even_norm_g': _jnp.float32, 'even_w_in': _jnp.float32, 'even_b_in': _jnp.float32, 'even_v_ln_g': _jnp.float32, 'even_v_ln_b': _jnp.float32, 'even_w_s': _jnp.float32, 'even_b_s': _jnp.float32, 'even_conv_w': _jnp.float32, 'even_conv_b': _jnp.float32, 'even_conv_ln_g': _jnp.float32, 'even_conv_ln_b': _jnp.float32, 'even_w_out': _jnp.float32, 'odd_norm_g': _jnp.float32, 'odd_w_qkv': _jnp.float32, 'odd_w_o': _jnp.float32, 'ffn_norm_g': _jnp.float32, 'ffn_w_up': _jnp.float32, 'ffn_conv_w': _jnp.float32, 'ffn_conv_b': _jnp.float32, 'ffn_w_down': _jnp.float32, 'final_norm_g': _jnp.float32}
MOMENT_SCALE = {'even_norm_g': 1.501154e-01, 'even_w_in': 1.019271e-01, 'even_b_in': 1.268954e-01, 'even_v_ln_g': 7.870959e-02, 'even_v_ln_b': 8.478819e-02, 'even_w_s': 5.542300e-02, 'even_b_s': 7.876933e-02, 'even_conv_w': 1.087503e-01, 'even_conv_b': 2.732468e-01, 'even_conv_ln_g': 1.465613e-01, 'even_conv_ln_b': 1.471592e-01, 'even_w_out': 1.296307e-01, 'odd_norm_g': 5.314750e-02, 'odd_w_qkv': 3.108837e-02, 'odd_w_o': 3.992700e-02, 'ffn_norm_g': 1.090843e-01, 'ffn_w_up': 4.781339e-02, 'ffn_conv_w': 4.887275e-02, 'ffn_conv_b': 5.059463e-02, 'ffn_w_down': 7.874049e-02, 'final_norm_g': 3.210415e+01}


def _to_microbatches(a, axis):
    t = _jnp.moveaxis(a, axis, 0)
    t = t.reshape((N_MICROBATCH, t.shape[0] // N_MICROBATCH) + t.shape[1:])
    return _jnp.moveaxis(t, 1, axis + 1)


def setup_inputs(seed: int = 0) -> dict:
    inp = _fwd_setup_inputs(seed)
    key = _jax.random.fold_in(_jax.random.key(seed), 7919)
    shape, _ = _output_shape()
    out = dict(inp)
    out["loss_target"] = _jax.random.normal(_jax.random.fold_in(key, 0), shape, _jnp.float32)
    for i, name in enumerate(TWIN_WEIGHTS):
        w = inp[name].astype(_jnp.float32)
        if MOMENT_SCALE is None:
            s = _jnp.sqrt(_jnp.mean(_jnp.square(w)) + 1e-30)
        else:
            s = MOMENT_SCALE[name]
        km, kv = _jax.random.split(_jax.random.fold_in(key, i + 1))
        out[name] = w
        out["m_" + name] = s * _jax.random.normal(km, w.shape, _jnp.float32)
        out["v_" + name] = (s * s) * _jax.random.uniform(kv, w.shape, _jnp.float32, 0.5, 1.5)
    if N_MICROBATCH > 1:
        for name, axis in PER_EXAMPLE_BATCH_AXIS.items():
            out[name] = _to_microbatches(out[name], axis)
    return {'x': out['x'], 'even_norm_g': out['even_norm_g'], 'even_w_in': out['even_w_in'], 'even_b_in': out['even_b_in'], 'even_v_ln_g': out['even_v_ln_g'], 'even_v_ln_b': out['even_v_ln_b'], 'even_w_s': out['even_w_s'], 'even_b_s': out['even_b_s'], 'even_conv_w': out['even_conv_w'], 'even_conv_b': out['even_conv_b'], 'even_conv_ln_g': out['even_conv_ln_g'], 'even_conv_ln_b': out['even_conv_ln_b'], 'even_w_out': out['even_w_out'], 'odd_norm_g': out['odd_norm_g'], 'odd_w_qkv': out['odd_w_qkv'], 'odd_w_o': out['odd_w_o'], 'ffn_norm_g': out['ffn_norm_g'], 'ffn_w_up': out['ffn_w_up'], 'ffn_conv_w': out['ffn_conv_w'], 'ffn_conv_b': out['ffn_conv_b'], 'ffn_w_down': out['ffn_w_down'], 'final_norm_g': out['final_norm_g'], 'loss_target': out['loss_target'], 'm_even_norm_g': out['m_even_norm_g'], 'm_even_w_in': out['m_even_w_in'], 'm_even_b_in': out['m_even_b_in'], 'm_even_v_ln_g': out['m_even_v_ln_g'], 'm_even_v_ln_b': out['m_even_v_ln_b'], 'm_even_w_s': out['m_even_w_s'], 'm_even_b_s': out['m_even_b_s'], 'm_even_conv_w': out['m_even_conv_w'], 'm_even_conv_b': out['m_even_conv_b'], 'm_even_conv_ln_g': out['m_even_conv_ln_g'], 'm_even_conv_ln_b': out['m_even_conv_ln_b'], 'm_even_w_out': out['m_even_w_out'], 'm_odd_norm_g': out['m_odd_norm_g'], 'm_odd_w_qkv': out['m_odd_w_qkv'], 'm_odd_w_o': out['m_odd_w_o'], 'm_ffn_norm_g': out['m_ffn_norm_g'], 'm_ffn_w_up': out['m_ffn_w_up'], 'm_ffn_conv_w': out['m_ffn_conv_w'], 'm_ffn_conv_b': out['m_ffn_conv_b'], 'm_ffn_w_down': out['m_ffn_w_down'], 'm_final_norm_g': out['m_final_norm_g'], 'v_even_norm_g': out['v_even_norm_g'], 'v_even_w_in': out['v_even_w_in'], 'v_even_b_in': out['v_even_b_in'], 'v_even_v_ln_g': out['v_even_v_ln_g'], 'v_even_v_ln_b': out['v_even_v_ln_b'], 'v_even_w_s': out['v_even_w_s'], 'v_even_b_s': out['v_even_b_s'], 'v_even_conv_w': out['v_even_conv_w'], 'v_even_conv_b': out['v_even_conv_b'], 'v_even_conv_ln_g': out['v_even_conv_ln_g'], 'v_even_conv_ln_b': out['v_even_conv_ln_b'], 'v_even_w_out': out['v_even_w_out'], 'v_odd_norm_g': out['v_odd_norm_g'], 'v_odd_w_qkv': out['v_odd_w_qkv'], 'v_odd_w_o': out['v_odd_w_o'], 'v_ffn_norm_g': out['v_ffn_norm_g'], 'v_ffn_w_up': out['v_ffn_w_up'], 'v_ffn_conv_w': out['v_ffn_conv_w'], 'v_ffn_conv_b': out['v_ffn_conv_b'], 'v_ffn_w_down': out['v_ffn_w_down'], 'v_final_norm_g': out['v_final_norm_g']}


def _loss(weights, diff, rest, loss_target):
    with _jax.named_scope("forward"):
        args = {**rest, TWIN_DIFF_INPUT: diff, **{k: w.astype(_WEIGHT_DTYPES[k]) for k, w in weights.items()}}
        y = _forward(args)
    with _jax.named_scope("loss_head"):
        err = _jnp.square(y.astype(_jnp.float32) - loss_target)
        return 0.5 * _jnp.sum(_jnp.mean(err, axis=-1)) if err.ndim else 0.5 * err


def _adamw(w, g, m, v):
    m = ADAM_B1 * m + (1.0 - ADAM_B1) * g
    v = ADAM_B2 * v + (1.0 - ADAM_B2) * _jnp.square(g)
    m_hat = m / (1.0 - ADAM_B1 ** ADAM_STEP)
    v_hat = v / (1.0 - ADAM_B2 ** ADAM_STEP)
    delta = -ADAM_LR * (m_hat / (_jnp.sqrt(v_hat) + ADAM_EPS) + ADAM_WD * w)
    return delta, m, v


def reference(x, even_norm_g, even_w_in, even_b_in, even_v_ln_g, even_v_ln_b, even_w_s, even_b_s, even_conv_w, even_conv_b, even_conv_ln_g, even_conv_ln_b, even_w_out, odd_norm_g, odd_w_qkv, odd_w_o, ffn_norm_g, ffn_w_up, ffn_conv_w, ffn_conv_b, ffn_w_down, final_norm_g, loss_target, m_even_norm_g, m_even_w_in, m_even_b_in, m_even_v_ln_g, m_even_v_ln_b, m_even_w_s, m_even_b_s, m_even_conv_w, m_even_conv_b, m_even_conv_ln_g, m_even_conv_ln_b, m_even_w_out, m_odd_norm_g, m_odd_w_qkv, m_odd_w_o, m_ffn_norm_g, m_ffn_w_up, m_ffn_conv_w, m_ffn_conv_b, m_ffn_w_down, m_final_norm_g, v_even_norm_g, v_even_w_in, v_even_b_in, v_even_v_ln_g, v_even_v_ln_b, v_even_w_s, v_even_b_s, v_even_conv_w, v_even_conv_b, v_even_conv_ln_g, v_even_conv_ln_b, v_even_w_out, v_odd_norm_g, v_odd_w_qkv, v_odd_w_o, v_ffn_norm_g, v_ffn_w_up, v_ffn_conv_w, v_ffn_conv_b, v_ffn_w_down, v_final_norm_g):
    given = dict(x=x, even_norm_g=even_norm_g, even_w_in=even_w_in, even_b_in=even_b_in, even_v_ln_g=even_v_ln_g, even_v_ln_b=even_v_ln_b, even_w_s=even_w_s, even_b_s=even_b_s, even_conv_w=even_conv_w, even_conv_b=even_conv_b, even_conv_ln_g=even_conv_ln_g, even_conv_ln_b=even_conv_ln_b, even_w_out=even_w_out, odd_norm_g=odd_norm_g, odd_w_qkv=odd_w_qkv, odd_w_o=odd_w_o, ffn_norm_g=ffn_norm_g, ffn_w_up=ffn_w_up, ffn_conv_w=ffn_conv_w, ffn_conv_b=ffn_conv_b, ffn_w_down=ffn_w_down, final_norm_g=final_norm_g, loss_target=loss_target, m_even_norm_g=m_even_norm_g, m_even_w_in=m_even_w_in, m_even_b_in=m_even_b_in, m_even_v_ln_g=m_even_v_ln_g, m_even_v_ln_b=m_even_v_ln_b, m_even_w_s=m_even_w_s, m_even_b_s=m_even_b_s, m_even_conv_w=m_even_conv_w, m_even_conv_b=m_even_conv_b, m_even_conv_ln_g=m_even_conv_ln_g, m_even_conv_ln_b=m_even_conv_ln_b, m_even_w_out=m_even_w_out, m_odd_norm_g=m_odd_norm_g, m_odd_w_qkv=m_odd_w_qkv, m_odd_w_o=m_odd_w_o, m_ffn_norm_g=m_ffn_norm_g, m_ffn_w_up=m_ffn_w_up, m_ffn_conv_w=m_ffn_conv_w, m_ffn_conv_b=m_ffn_conv_b, m_ffn_w_down=m_ffn_w_down, m_final_norm_g=m_final_norm_g, v_even_norm_g=v_even_norm_g, v_even_w_in=v_even_w_in, v_even_b_in=v_even_b_in, v_even_v_ln_g=v_even_v_ln_g, v_even_v_ln_b=v_even_v_ln_b, v_even_w_s=v_even_w_s, v_even_b_s=v_even_b_s, v_even_conv_w=v_even_conv_w, v_even_conv_b=v_even_conv_b, v_even_conv_ln_g=v_even_conv_ln_g, v_even_conv_ln_b=v_even_conv_ln_b, v_even_w_out=v_even_w_out, v_odd_norm_g=v_odd_norm_g, v_odd_w_qkv=v_odd_w_qkv, v_odd_w_o=v_odd_w_o, v_ffn_norm_g=v_ffn_norm_g, v_ffn_w_up=v_ffn_w_up, v_ffn_conv_w=v_ffn_conv_w, v_ffn_conv_b=v_ffn_conv_b, v_ffn_w_down=v_ffn_w_down, v_final_norm_g=v_final_norm_g)
    weights = {n: given[n] for n in TWIN_WEIGHTS}
    shared = {n: given[n] for n in SHARED_INPUTS}
    per_example = {n: given[n] for n in ['x']}
    grad_fn = _jax.value_and_grad(_loss, argnums=(0, 1))

    def one_microbatch(ex, loss_target):
        ex = dict(ex)
        diff = ex.pop(TWIN_DIFF_INPUT)
        return grad_fn(weights, diff, {**shared, **ex}, loss_target)

    if N_MICROBATCH == 1:
        loss, (grad_w, grad_x) = one_microbatch(per_example, given["loss_target"])
    else:
        def body(carry, xs):
            loss_sum, grad_sum = carry
            l_k, (gw_k, gx_k) = one_microbatch(xs[0], xs[1])
            with _jax.named_scope("update"):
                return (loss_sum + l_k, _jax.tree.map(_jnp.add, grad_sum, gw_k)), gx_k

        init = (_jnp.zeros((), _jnp.float32), _jax.tree.map(_jnp.zeros_like, weights))
        (loss, grad_w), grad_x = _jax.lax.scan(body, init, (per_example, given["loss_target"]))
    with _jax.named_scope("update"):
        delta_w, new_m, new_v = {}, {}, {}
        for n in TWIN_WEIGHTS:
            delta_w[n], new_m[n], new_v[n] = _adamw(weights[n], grad_w[n], given["m_" + n], given["v_" + n])
    return (loss, grad_x, *[grad_w[n] for n in TWIN_WEIGHTS], *[delta_w[n] for n in TWIN_WEIGHTS],
            *[new_m[n] for n in TWIN_WEIGHTS], *[new_v[n] for n in TWIN_WEIGHTS])
```

```python
import functools
import math

import jax
import jax.numpy as jnp
from jax import lax
from jax.experimental import pallas as pl
from jax.experimental.pallas import tpu as pltpu

F32 = jnp.float32
BF16 = jnp.bfloat16

N_DEV = 8
EPS = 1e-6
NEG = -1e30
HEAD = 64
CHUNK = 128
BLOCK = 128
CONV_K = 31
FFN_K = 3
DILATIONS = (1, 4, 16)
ROPE_THETA = 10000.0
LR, B1, B2, ADAM_EPS, WD, STEP = 0.001, 0.9, 0.999, 1e-08, 0.01, 10

VMEM_LIMIT = 56 * 1024 * 1024
VMEM_BUDGET = 32 * 1024 * 1024
ROWS = 512
HALO = 32
FHALO = 8

NAMES = ['x', 'even_norm_g', 'even_w_in', 'even_b_in', 'even_v_ln_g', 'even_v_ln_b', 'even_w_s', 'even_b_s',
         'even_conv_w', 'even_conv_b', 'even_conv_ln_g', 'even_conv_ln_b', 'even_w_out', 'odd_norm_g',
         'odd_w_qkv', 'odd_w_o', 'ffn_norm_g', 'ffn_w_up', 'ffn_conv_w', 'ffn_conv_b', 'ffn_w_down',
         'final_norm_g']
WEIGHTS = NAMES[1:]


def _params(sem=None):
    return pltpu.CompilerParams(dimension_semantics=sem, vmem_limit_bytes=VMEM_LIMIT)


def _sigmoid(x):
    return 1.0 / (1.0 + jnp.exp(-x))


def _gelu(x):
    c = math.sqrt(2.0 / math.pi)
    return 0.5 * x * (1.0 + jnp.tanh(c * (x + 0.044715 * x * x * x)))


def _gelu_grad(x):
    c = math.sqrt(2.0 / math.pi)
    t = jnp.tanh(c * (x + 0.044715 * x * x * x))
    return 0.5 * (1.0 + t) + 0.5 * x * (1.0 - t * t) * c * (1.0 + 3.0 * 0.044715 * x * x)


def _ln_stats(x):
    mu = jnp.mean(x, axis=-1, keepdims=True)
    xc = x - mu
    rstd = lax.rsqrt(jnp.mean(xc * xc, axis=-1, keepdims=True) + EPS)
    return xc * rstd, rstd


def _ln_bwd(dy, xhat, rstd, g):
    dxh = dy * g
    return rstd * (dxh - jnp.mean(dxh, axis=-1, keepdims=True) - xhat * jnp.mean(dxh * xhat, axis=-1, keepdims=True))


def _colsum(x):
    return jnp.sum(x, axis=0, keepdims=True)


def _split3(x):
    hi = x.astype(BF16)
    r = x - hi.astype(F32)
    mid = r.astype(BF16)
    lo = (r - mid.astype(F32)).astype(BF16)
    return hi, mid, lo


def _dot(a, b, dims):
    return lax.dot_general(a, b, (dims, ((), ())), preferred_element_type=F32)


NN = ((1,), (0,))
NT = ((1,), (1,))
TN = ((0,), (0,))


def _divisors(n, cands):
    return [c for c in cands if c <= n and n % c == 0]


def _pick_tiles(m, n, k, a_bytes, b_bytes, o_bytes, extra_bytes):
    best = None
    for tm in _divisors(m, (1024, 512, 256, 128)):
        for tn in _divisors(n, (1408, 1024, 768, 704, 512, 384, 256, 128)):
            if tn % 128:
                continue
            need = 2 * (tm * k * a_bytes + k * tn * b_bytes + tm * tn * (o_bytes + extra_bytes)) + tm * tn * 4
            if need <= VMEM_BUDGET and (best is None or tm * tn > best[0] * best[1]):
                best = (tm, tn)
    assert best is not None, (m, n, k)
    return best


def matmul(a, b, *, tb=False, bias=None, res=None, out_dtype=F32, name):
    m, k = a.shape
    n = b.shape[0] if tb else b.shape[1]
    assert (b.shape[1] if tb else b.shape[0]) == k
    tm, tn = _pick_tiles(m, n, k, a.dtype.itemsize, b.dtype.itemsize, jnp.dtype(out_dtype).itemsize,
                         4 if res is not None else 0)

    def body(*refs):
        a_ref, b_ref = refs[:2]
        o_ref = refs[-1]
        acc = _dot(a_ref[...].astype(BF16), b_ref[...].astype(BF16), NT if tb else NN)
        pos = 2
        if bias is not None:
            acc = acc + refs[pos][...]
            pos += 1
        if res is not None:
            acc = acc + refs[pos][...]
        o_ref[...] = acc.astype(out_dtype)

    in_specs = [pl.BlockSpec((tm, k), lambda i, j: (i, 0)),
                pl.BlockSpec((tn, k), lambda i, j: (j, 0)) if tb else pl.BlockSpec((k, tn), lambda i, j: (0, j))]
    args = [a, b]
    if bias is not None:
        in_specs.append(pl.BlockSpec((1, tn), lambda i, j: (0, j)))
        args.append(bias)
    if res is not None:
        in_specs.append(pl.BlockSpec((tm, tn), lambda i, j: (i, j)))
        args.append(res)
    return pl.pallas_call(
        body, name=name, grid=(m // tm, n // tn), in_specs=in_specs,
        out_specs=pl.BlockSpec((tm, tn), lambda i, j: (i, j)),
        out_shape=jax.ShapeDtypeStruct((m, n), out_dtype),
        compiler_params=_params(("parallel", "parallel")))(*args)


def matmul_ta(a, b, *, out_dtype=BF16, name):
    s, m = a.shape
    n = b.shape[1]
    assert b.shape[0] == s
    best = None
    for tm in _divisors(m, (512, 256, 128)):
        for tn in _divisors(n, (1024, 512, 384, 256, 128)):
            need = 2 * (s * tm * a.dtype.itemsize + s * tn * b.dtype.itemsize + tm * tn * 2) + tm * tn * 4
            if need <= VMEM_BUDGET and (best is None or tm * tn > best[0] * best[1]):
                best = (tm, tn)
    tm, tn = best

    def body(a_ref, b_ref, o_ref):
        o_ref[...] = _dot(a_ref[...].astype(BF16), b_ref[...].astype(BF16), TN).astype(out_dtype)

    return pl.pallas_call(
        body, name=name, grid=(m // tm, n // tn),
        in_specs=[pl.BlockSpec((s, tm), lambda i, j: (0, i)), pl.BlockSpec((s, tn), lambda i, j: (0, j))],
        out_specs=pl.BlockSpec((tm, tn), lambda i, j: (i, j)),
        out_shape=jax.ShapeDtypeStruct((m, n), out_dtype),
        compiler_params=_params(("parallel", "parallel")))(a, b)


def rms_fwd(x, g, *, name):
    s, d = x.shape

    def body(x_ref, g_ref, h_ref):
        xv = x_ref[...]
        r = lax.rsqrt(jnp.mean(xv * xv, axis=-1, keepdims=True) + EPS)
        h_ref[...] = (xv * r * g_ref[...]).astype(BF16)

    return pl.pallas_call(
        body, name=name, grid=(s // ROWS,),
        in_specs=[pl.BlockSpec((ROWS, d), lambda i: (i, 0)), pl.BlockSpec((1, d), lambda i: (0, 0))],
        out_specs=pl.BlockSpec((ROWS, d), lambda i: (i, 0)),
        out_shape=jax.ShapeDtypeStruct((s, d), BF16),
        compiler_params=_params(("parallel",)))(x, g)


def rms_bwd(dh, x, g, dres, *, name):
    s, d = x.shape

    def body(dh_ref, x_ref, g_ref, dres_ref, dx_ref, dxb_ref, dg_ref):
        xv = x_ref[...]
        r = lax.rsqrt(jnp.mean(xv * xv, axis=-1, keepdims=True) + EPS)
        xhat = xv * r
        dhv = dh_ref[...]
        dxh = dhv * g_ref[...]
        dx = dres_ref[...] + r * (dxh - xhat * jnp.mean(dxh * xhat, axis=-1, keepdims=True))
        dx_ref[...] = dx
        dxb_ref[...] = dx.astype(BF16)

        @pl.when(pl.program_id(0) == 0)
        def _():
            dg_ref[...] = jnp.zeros_like(dg_ref)
        dg_ref[...] += _colsum(dhv * xhat)

    row = pl.BlockSpec((ROWS, d), lambda i: (i, 0))
    vec = pl.BlockSpec((1, d), lambda i: (0, 0))
    return pl.pallas_call(
        body, name=name, grid=(s // ROWS,),
        in_specs=[row, row, vec, row], out_specs=[row, row, vec],
        out_shape=[jax.ShapeDtypeStruct((s, d), F32), jax.ShapeDtypeStruct((s, d), BF16),
                   jax.ShapeDtypeStruct((1, d), F32)],
        compiler_params=_params(("arbitrary",)))(dh, x, g, dres)


def final_loss_bwd(x, g, tgt, *, name):
    s, d = x.shape

    def body(x_ref, g_ref, t_ref, loss_ref, dx_ref, dxb_ref, dg_ref):
        xv = x_ref[...]
        gv = g_ref[...]
        r = lax.rsqrt(jnp.mean(xv * xv, axis=-1, keepdims=True) + EPS)
        xhat = xv * r
        e = xhat * gv - t_ref[...]
        dy = e * (1.0 / d)
        dxh = dy * gv
        dx = r * (dxh - xhat * jnp.mean(dxh * xhat, axis=-1, keepdims=True))
        dx_ref[...] = dx
        dxb_ref[...] = dx.astype(BF16)

        @pl.when(pl.program_id(0) == 0)
        def _():
            dg_ref[...] = jnp.zeros_like(dg_ref)
            loss_ref[...] = jnp.zeros_like(loss_ref)
        dg_ref[...] += _colsum(dy * xhat)
        loss_ref[...] += 0.5 * jnp.sum(jnp.mean(e * e, axis=-1, keepdims=True))

    row = pl.BlockSpec((ROWS, d), lambda i: (i, 0))
    vec = pl.BlockSpec((1, d), lambda i: (0, 0))
    one = pl.BlockSpec((8, 128), lambda i: (0, 0))
    return pl.pallas_call(
        body, name=name, grid=(s // ROWS,),
        in_specs=[row, vec, row], out_specs=[one, row, row, vec],
        out_shape=[jax.ShapeDtypeStruct((8, 128), F32), jax.ShapeDtypeStruct((s, d), F32),
                   jax.ShapeDtypeStruct((s, d), BF16), jax.ShapeDtypeStruct((1, d), F32)],
        compiler_params=_params(("arbitrary",)))(x, g, tgt)


def _pair_masks():
    lane = lax.broadcasted_iota(jnp.int32, (CHUNK, 128), 1)
    return lane < HEAD


def _head_keep(rows):
    lane = lax.broadcasted_iota(jnp.int32, (rows, 128), 1)
    first = jnp.where(lane < HEAD, 1.0, 0.0)
    return first.astype(BF16), (1.0 - first).astype(BF16)


def _gating_mixed(vn_b, wm_ref, lo):
    rows, aw = vn_b.shape
    out = []
    for c in range(rows // CHUNK):
        tiles = []
        for p in range(aw // 128):
            vp = vn_b[c * CHUNK:(c + 1) * CHUNK, p * 128:(p + 1) * 128]
            r0 = _dot(wm_ref[2 * p], vp, NN)
            r1 = _dot(wm_ref[2 * p + 1], vp, NN)
            tiles.append(jnp.where(lo, r0, r1))
        out.append(jnp.concatenate(tiles, axis=1))
    return jnp.concatenate(out, axis=0)


def even_mid_fwd(z, vg, vb, wm, bm, cw, cb, cg, cbeta, *, name):
    s, zw = z.shape
    aw = zw // 4
    nblk = s // ROWS

    def body(z_ref, zp_ref, vg_ref, vb_ref, wm_ref, bm_ref, cw_ref, cb_ref, cg_ref, cbeta_ref,
             y_ref, hc_ref, ext_ref):
        i = pl.program_id(0)
        lo = _pair_masks()
        u = _gelu(z_ref[:, 0:aw])
        v = _gelu(z_ref[:, aw:2 * aw])
        vhat, _ = _ln_stats(v)
        vn = (vhat * vg_ref[...] + vb_ref[...]).astype(BF16)
        mixed = _gating_mixed(vn, wm_ref, lo)
        bias = jnp.concatenate([bm_ref[...]] * (ROWS // CHUNK), axis=0)
        y_ref[:, 0:aw] = (u * (mixed + bias)).astype(BF16)

        hb = z_ref[:, 2 * aw:3 * aw] * _sigmoid(z_ref[:, 3 * aw:4 * aw])
        hbp = zp_ref[:, 0:aw] * _sigmoid(zp_ref[:, aw:2 * aw])
        ext_ref[0:HALO, :] = jnp.where(i > 0, hbp, 0.0)
        ext_ref[HALO:HALO + ROWS, :] = hb
        acc = jnp.zeros((ROWS, aw), F32) + cb_ref[...]
        for k in range(CONV_K):
            acc = acc + cw_ref[k:k + 1, :] * ext_ref[pl.ds(HALO - (CONV_K - 1) + k, ROWS), :]
        hc_ref[...] = acc
        hhat, _ = _ln_stats(acc)
        hn = hhat * cg_ref[...] + cbeta_ref[...]
        y_ref[:, aw:2 * aw] = (hn * _sigmoid(hn)).astype(BF16)

    hb_per = ROWS // HALO
    vec = pl.BlockSpec((1, aw), lambda i: (0, 0))
    return pl.pallas_call(
        body, name=name, grid=(nblk,),
        in_specs=[pl.BlockSpec((ROWS, zw), lambda i: (i, 0)),
                  pl.BlockSpec((HALO, 2 * aw), lambda i: (jnp.maximum(i * hb_per - 1, 0), 1)),
                  vec, vec,
                  pl.BlockSpec(wm.shape, lambda i: (0, 0, 0)),
                  pl.BlockSpec((CHUNK, aw), lambda i: (0, 0)),
                  pl.BlockSpec((CONV_K, aw), lambda i: (0, 0)), vec, vec, vec],
        out_specs=[pl.BlockSpec((ROWS, 2 * aw), lambda i: (i, 0)), pl.BlockSpec((ROWS, aw), lambda i: (i, 0))],
        out_shape=[jax.ShapeDtypeStruct((s, 2 * aw), BF16), jax.ShapeDtypeStruct((s, aw), F32)],
        scratch_shapes=[pltpu.VMEM((HALO + ROWS, aw), F32)],
        compiler_params=_params(("parallel",)))(z, z, vg, vb, wm, bm, cw, cb, cg, cbeta)


def even_mid_bwd_rows(dy, z, hc, vg, vb, wm, wmt, bm, sel, cg, cbeta, *, name):
    s, zw = z.shape
    aw = zw // 4
    nh = wm.shape[0]

    def body(dy_ref, z_ref, hc_ref, vg_ref, vb_ref, wm_ref, wmt_ref, bm_ref, sel_ref, cg_ref, cbeta_ref,
             dza_ref, dhc_ref, dba_ref, dvg_ref, dvb_ref, dwm_ref, dbs_ref, dcg_ref, dcbeta_ref, dcb_ref):
        @pl.when(pl.program_id(0) == 0)
        def _():
            for r in (dba_ref, dvg_ref, dvb_ref, dwm_ref, dbs_ref, dcg_ref, dcbeta_ref, dcb_ref):
                r[...] = jnp.zeros_like(r)

        lo = _pair_masks()
        keep = _head_keep(CHUNK)
        zu = z_ref[:, 0:aw]
        zv = z_ref[:, aw:2 * aw]
        u = _gelu(zu)
        v = _gelu(zv)
        vhat, vrstd = _ln_stats(v)
        vn = (vhat * vg_ref[...] + vb_ref[...]).astype(BF16)
        mixed = _gating_mixed(vn, wm_ref, lo)
        bias = jnp.concatenate([bm_ref[...]] * (ROWS // CHUNK), axis=0)
        dya = dy_ref[:, 0:aw]
        du = dya * (mixed + bias)
        dmix = dya * u
        dmix_b = dmix.astype(BF16)

        dvn_rows = []
        for c in range(ROWS // CHUNK):
            rs = slice(c * CHUNK, (c + 1) * CHUNK)
            tiles = []
            for p in range(aw // 128):
                cs = slice(p * 128, (p + 1) * 128)
                dm = dmix_b[rs, cs]
                dm0 = dm * keep[0]
                dm1 = dm * keep[1]
                vp = vn[rs, cs]
                tiles.append(_dot(wmt_ref[2 * p], dm0, NN) + _dot(wmt_ref[2 * p + 1], dm1, NN))
                dwm_ref[2 * p] += _dot(dm0, vp, NT)
                dwm_ref[2 * p + 1] += _dot(dm1, vp, NT)
            dvn_rows.append(jnp.concatenate(tiles, axis=1))
            acc = jnp.zeros((CHUNK, 128), F32)
            for part in _split3(dmix[rs, :]):
                acc = acc + _dot(part, sel_ref[...], NN)
            dbs_ref[...] += acc
        dvn = jnp.concatenate(dvn_rows, axis=0)
        dvg_ref[...] += _colsum(dvn * vhat)
        dvb_ref[...] += _colsum(dvn)
        dv = _ln_bwd(dvn, vhat, vrstd, vg_ref[...])
        dzu = du * _gelu_grad(zu)
        dzv = dv * _gelu_grad(zv)
        dza_ref[:, 0:aw] = dzu.astype(BF16)
        dza_ref[:, aw:2 * aw] = dzv.astype(BF16)
        dba_ref[:, 0:aw] += _colsum(dzu)
        dba_ref[:, aw:2 * aw] += _colsum(dzv)

        hcv = hc_ref[...]
        hhat, hrstd = _ln_stats(hcv)
        hn = hhat * cg_ref[...] + cbeta_ref[...]
        sg = _sigmoid(hn)
        dhn = dy_ref[:, aw:2 * aw] * (sg * (1.0 + hn * (1.0 - sg)))
        dcg_ref[...] += _colsum(dhn * hhat)
        dcbeta_ref[...] += _colsum(dhn)
        dhc = _ln_bwd(dhn, hhat, hrstd, cg_ref[...])
        dhc_ref[...] = dhc
        dcb_ref[...] += _colsum(dhc)

    vec = pl.BlockSpec((1, aw), lambda i: (0, 0))
    vec2 = pl.BlockSpec((1, 2 * aw), lambda i: (0, 0))
    w3 = pl.BlockSpec(wm.shape, lambda i: (0, 0, 0))
    sq = pl.BlockSpec((CHUNK, 128), lambda i: (0, 0))
    return pl.pallas_call(
        body, name=name, grid=(s // ROWS,),
        in_specs=[pl.BlockSpec((ROWS, 2 * aw), lambda i: (i, 0)), pl.BlockSpec((ROWS, 2 * aw), lambda i: (i, 0)),
                  pl.BlockSpec((ROWS, aw), lambda i: (i, 0)), vec, vec, w3, w3,
                  pl.BlockSpec((CHUNK, aw), lambda i: (0, 0)), pl.BlockSpec((aw, 128), lambda i: (0, 0)), vec, vec],
        out_specs=[pl.BlockSpec((ROWS, 2 * aw), lambda i: (i, 0)), pl.BlockSpec((ROWS, aw), lambda i: (i, 0)),
                   vec2, vec, vec, w3, sq, vec, vec, vec],
        out_shape=[jax.ShapeDtypeStruct((s, 2 * aw), BF16), jax.ShapeDtypeStruct((s, aw), F32),
                   jax.ShapeDtypeStruct((1, 2 * aw), F32), jax.ShapeDtypeStruct((1, aw), F32),
                   jax.ShapeDtypeStruct((1, aw), F32), jax.ShapeDtypeStruct(wm.shape, F32),
                   jax.ShapeDtypeStruct((CHUNK, 128), F32), jax.ShapeDtypeStruct((1, aw), F32),
                   jax.ShapeDtypeStruct((1, aw), F32), jax.ShapeDtypeStruct((1, aw), F32)],
        compiler_params=_params(("arbitrary",)))(dy, z, hc, vg, vb, wm, wmt, bm, sel, cg, cbeta)


def even_conv_bwd(dhc, z, cw, *, name):
    s, zw = z.shape
    aw = zw // 4
    nblk = s // ROWS
    hb_per = ROWS // HALO

    def body(dc_ref, dn_ref, z_ref, zp_ref, cw_ref, dzb_ref, dbb_ref, dcw_ref, exth_ref, extd_ref):
        i = pl.program_id(0)

        @pl.when(i == 0)
        def _():
            dbb_ref[...] = jnp.zeros_like(dbb_ref)
            dcw_ref[...] = jnp.zeros_like(dcw_ref)

        a = z_ref[:, 0:aw]
        sg = _sigmoid(z_ref[:, aw:2 * aw])
        exth_ref[0:HALO, :] = jnp.where(i > 0, zp_ref[:, 0:aw] * _sigmoid(zp_ref[:, aw:2 * aw]), 0.0)
        exth_ref[HALO:HALO + ROWS, :] = a * sg
        dcur = dc_ref[...]
        extd_ref[0:ROWS, :] = dcur
        extd_ref[ROWS:ROWS + HALO, :] = jnp.where(i < nblk - 1, dn_ref[...], 0.0)
        dhb = jnp.zeros((ROWS, aw), F32)
        for k in range(CONV_K):
            wk = cw_ref[k:k + 1, :]
            dhb = dhb + wk * extd_ref[pl.ds(CONV_K - 1 - k, ROWS), :]
            dcw_ref[k:k + 1, :] += _colsum(dcur * exth_ref[pl.ds(HALO - (CONV_K - 1) + k, ROWS), :])
        da = dhb * sg
        dg = dhb * a * sg * (1.0 - sg)
        dzb_ref[:, 0:aw] = da.astype(BF16)
        dzb_ref[:, aw:2 * aw] = dg.astype(BF16)
        dbb_ref[:, 0:aw] += _colsum(da)
        dbb_ref[:, aw:2 * aw] += _colsum(dg)

    return pl.pallas_call(
        body, name=name, grid=(nblk,),
        in_specs=[pl.BlockSpec((ROWS, aw), lambda i: (i, 0)),
                  pl.BlockSpec((HALO, aw), lambda i: (jnp.minimum((i + 1) * hb_per, nblk * hb_per - 1), 0)),
                  pl.BlockSpec((ROWS, 2 * aw), lambda i: (i, 1)),
                  pl.BlockSpec((HALO, 2 * aw), lambda i: (jnp.maximum(i * hb_per - 1, 0), 1)),
                  pl.BlockSpec((CONV_K, aw), lambda i: (0, 0))],
        out_specs=[pl.BlockSpec((ROWS, 2 * aw), lambda i: (i, 0)), pl.BlockSpec((1, 2 * aw), lambda i: (0, 0)),
                   pl.BlockSpec((CONV_K, aw), lambda i: (0, 0))],
        out_shape=[jax.ShapeDtypeStruct((s, 2 * aw), BF16), jax.ShapeDtypeStruct((1, 2 * aw), F32),
                   jax.ShapeDtypeStruct((CONV_K, aw), F32)],
        scratch_shapes=[pltpu.VMEM((HALO + ROWS, aw), F32), pltpu.VMEM((ROWS + HALO, aw), F32)],
        compiler_params=_params(("arbitrary",)))(dhc, dhc, z, z, cw)


def _ffn_tile(f):
    for t in (1408, 1024, 768, 704, 512, 384, 256, 128):
        if f % t == 0 and t % 128 == 0:
            return t
    raise ValueError(f)


def ffn_mid_fwd(up, cw, cb, *, name):
    s, f2 = up.shape
    f = f2 // 2
    tn = _ffn_tile(f)
    nj = f // tn
    per = ROWS // FHALO

    def body(ug_ref, uv_ref, pg_ref, pv_ref, wg_ref, wv_ref, bg_ref, bv_ref, act_ref, eg_ref, ev_ref):
        i = pl.program_id(0)

        def conv(cur_ref, prev_ref, w_ref, b_ref, ext_ref):
            ext_ref[0:FHALO, :] = jnp.where(i > 0, prev_ref[...], 0.0)
            ext_ref[FHALO:FHALO + ROWS, :] = cur_ref[...]
            acc = jnp.zeros((ROWS, tn), F32) + b_ref[...]
            for k in range(FFN_K):
                acc = acc + w_ref[k:k + 1, :] * ext_ref[pl.ds(FHALO - (FFN_K - 1) + k, ROWS), :]
            return acc

        gate = conv(ug_ref, pg_ref, wg_ref, bg_ref, eg_ref)
        val = conv(uv_ref, pv_ref, wv_ref, bv_ref, ev_ref)
        act_ref[...] = (gate * _sigmoid(gate) * val).astype(BF16)

    cur = lambda off: pl.BlockSpec((ROWS, tn), lambda i, j: (i, j + off))
    prev = lambda off: pl.BlockSpec((FHALO, tn), lambda i, j: (jnp.maximum(i * per - 1, 0), j + off))
    wsp = lambda off: pl.BlockSpec((FFN_K, tn), lambda i, j: (0, j + off))
    bsp = lambda off: pl.BlockSpec((1, tn), lambda i, j: (0, j + off))
    return pl.pallas_call(
        body, name=name, grid=(s // ROWS, nj),
        in_specs=[cur(0), cur(nj), prev(0), prev(nj), wsp(0), wsp(nj), bsp(0), bsp(nj)],
        out_specs=pl.BlockSpec((ROWS, tn), lambda i, j: (i, j)),
        out_shape=jax.ShapeDtypeStruct((s, f), BF16),
        scratch_shapes=[pltpu.VMEM((FHALO + ROWS, tn), F32), pltpu.VMEM((FHALO + ROWS, tn), F32)],
        compiler_params=_params(("parallel", "parallel")))(up, up, up, up, cw, cw, cb, cb)


def ffn_mid_bwd(dact, up, cw, cb, *, name):
    s, f2 = up.shape
    f = f2 // 2
    tn = _ffn_tile(f)
    nj = f // tn
    nblk = s // ROWS
    per = ROWS // FHALO
    ext = ROWS + FHALO

    def body(da_ref, dan_ref, ug_ref, uv_ref, pg_ref, pv_ref, ng_ref, nv_ref, wg_ref, wv_ref, bg_ref, bv_ref,
             dug_ref, duv_ref, dwg_ref, dwv_ref, dbg_ref, dbv_ref, eg_ref, ev_ref, dg_ref, dv_ref):
        i = pl.program_id(1)

        @pl.when(i == 0)
        def _():
            for r in (dwg_ref, dwv_ref, dbg_ref, dbv_ref):
                r[...] = jnp.zeros_like(r)

        def fill(cur_ref, prev_ref, next_ref, ext_ref):
            ext_ref[0:FHALO, :] = jnp.where(i > 0, prev_ref[...], 0.0)
            ext_ref[FHALO:FHALO + ROWS, :] = cur_ref[...]
            ext_ref[FHALO + ROWS:FHALO + ROWS + FHALO, :] = jnp.where(i < nblk - 1, next_ref[...], 0.0)

        def conv(w_ref, b_ref, ext_ref):
            acc = jnp.zeros((ext, tn), F32) + b_ref[...]
            for k in range(FFN_K):
                acc = acc + w_ref[k:k + 1, :] * ext_ref[pl.ds(FHALO - (FFN_K - 1) + k, ext), :]
            return acc

        fill(ug_ref, pg_ref, ng_ref, eg_ref)
        fill(uv_ref, pv_ref, nv_ref, ev_ref)
        gate = conv(wg_ref, bg_ref, eg_ref)
        val = conv(wv_ref, bv_ref, ev_ref)
        dact_ext = jnp.concatenate([da_ref[...], jnp.where(i < nblk - 1, dan_ref[...], 0.0)], axis=0)
        sg = _sigmoid(gate)
        dg_ref[...] = dact_ext * val * (sg * (1.0 + gate * (1.0 - sg)))
        dv_ref[...] = dact_ext * (gate * sg)

        def back(d_ref, w_ref, ext_ref, du_ref, dw_ref, db_ref):
            dcur = d_ref[0:ROWS, :]
            acc = jnp.zeros((ROWS, tn), F32)
            for k in range(FFN_K):
                acc = acc + w_ref[k:k + 1, :] * d_ref[pl.ds(FFN_K - 1 - k, ROWS), :]
                dw_ref[k:k + 1, :] += _colsum(dcur * ext_ref[pl.ds(FHALO - (FFN_K - 1) + k, ROWS), :])
            du_ref[...] = acc.astype(BF16)
            db_ref[...] += _colsum(dcur)

        back(dg_ref, wg_ref, eg_ref, dug_ref, dwg_ref, dbg_ref)
        back(dv_ref, wv_ref, ev_ref, duv_ref, dwv_ref, dbv_ref)

    cur = lambda off: pl.BlockSpec((ROWS, tn), lambda j, i: (i, j + off))
    prev = lambda off: pl.BlockSpec((FHALO, tn), lambda j, i: (jnp.maximum(i * per - 1, 0), j + off))
    nxt = lambda off: pl.BlockSpec((FHALO, tn), lambda j, i: (jnp.minimum((i + 1) * per, nblk * per - 1), j + off))
    wsp = lambda off: pl.BlockSpec((FFN_K, tn), lambda j, i: (0, j + off))
    bsp = lambda off: pl.BlockSpec((1, tn), lambda j, i: (0, j + off))
    outs = pl.pallas_call(
        body, name=name, grid=(nj, nblk),
        in_specs=[cur(0), nxt(0), cur(0), cur(nj), prev(0), prev(nj), nxt(0), nxt(nj),
                  wsp(0), wsp(nj), bsp(0), bsp(nj)],
        out_specs=[cur(0), cur(0), wsp(0), wsp(0), bsp(0), bsp(0)],
        out_shape=[jax.ShapeDtypeStruct((s, f), BF16), jax.ShapeDtypeStruct((s, f), BF16),
                   jax.ShapeDtypeStruct((FFN_K, f), F32), jax.ShapeDtypeStruct((FFN_K, f), F32),
                   jax.ShapeDtypeStruct((1, f), F32), jax.ShapeDtypeStruct((1, f), F32)],
        scratch_shapes=[pltpu.VMEM((ROWS + 2 * FHALO, tn), F32), pltpu.VMEM((ROWS + 2 * FHALO, tn), F32),
                        pltpu.VMEM((ext, tn), F32), pltpu.VMEM((ext, tn), F32)],
        compiler_params=_params(("parallel", "arbitrary")))(dact, dact, up, up, up, up, up, up, cw, cw, cb, cb)
    dug, duv, dwg, dwv, dbg, dbv = outs
    return dug, duv, jnp.concatenate([dwg, dwv], axis=1), jnp.concatenate([dbg, dbv], axis=1)


def rope_tables(s):
    half = HEAD // 2
    lane = jnp.arange(128)
    j = lane % HEAD
    inv = ROPE_THETA ** (-(j % half).astype(F32) / half)
    ang = jnp.arange(s, dtype=F32)[:, None] * inv[None, :]
    sign = jnp.where(j < half, -1.0, 1.0).astype(F32)
    return jnp.cos(ang), jnp.sin(ang) * sign[None, :]


def _swap_halves(x):
    lane = lax.broadcasted_iota(jnp.int32, x.shape, 1)
    return jnp.where((lane % HEAD) < HEAD // 2, pltpu.roll(x, 128 - HEAD // 2, 1), pltpu.roll(x, HEAD // 2, 1))


def rope_fwd(qkv, cos, sin, *, name):
    s, d3 = qkv.shape
    d = d3 // 3
    scale = HEAD ** -0.5

    def body(x_ref, c_ref, s_ref, q_ref, k_ref, v_ref):
        c = c_ref[...]
        sn = s_ref[...]
        for t in range(d // 128):
            cs = slice(t * 128, (t + 1) * 128)
            xq = x_ref[:, t * 128:(t + 1) * 128]
            xk = x_ref[:, d + t * 128:d + (t + 1) * 128]
            q_ref[:, cs] = ((xq * c + _swap_halves(xq) * sn) * scale).astype(BF16)
            k_ref[:, cs] = (xk * c + _swap_halves(xk) * sn).astype(BF16)
        v_ref[...] = x_ref[:, 2 * d:3 * d].astype(BF16)

    row = pl.BlockSpec((ROWS, d), lambda i: (i, 0))
    tab = pl.BlockSpec((ROWS, 128), lambda i: (i, 0))
    return pl.pallas_call(
        body, name=name, grid=(s // ROWS,),
        in_specs=[pl.BlockSpec((ROWS, d3), lambda i: (i, 0)), tab, tab],
        out_specs=[row, row, row],
        out_shape=[jax.ShapeDtypeStruct((s, d), BF16)] * 3,
        compiler_params=_params(("parallel",)))(qkv, cos, sin)


def rope_bwd(dq, dk, dv, cos, sin, *, name):
    s, d = dq.shape
    scale = HEAD ** -0.5

    def body(dq_ref, dk_ref, dv_ref, c_ref, s_ref, o_ref):
        c = c_ref[...]
        sn = s_ref[...]
        for t in range(d // 128):
            cs = slice(t * 128, (t + 1) * 128)
            gq = dq_ref[:, cs] * scale
            gk = dk_ref[:, cs]
            o_ref[:, t * 128:(t + 1) * 128] = (gq * c + _swap_halves(gq * sn)).astype(BF16)
            o_ref[:, d + t * 128:d + (t + 1) * 128] = (gk * c + _swap_halves(gk * sn)).astype(BF16)
        o_ref[:, 2 * d:3 * d] = dv_ref[...].astype(BF16)

    row = pl.BlockSpec((ROWS, d), lambda i: (i, 0))
    tab = pl.BlockSpec((ROWS, 128), lambda i: (i, 0))
    return pl.pallas_call(
        body, name=name, grid=(s // ROWS,),
        in_specs=[row, row, row, tab, tab],
        out_specs=pl.BlockSpec((ROWS, 3 * d), lambda i: (i, 0)),
        out_shape=jax.ShapeDtypeStruct((s, 3 * d), BF16),
        compiler_params=_params(("parallel",)))(dq, dk, dv, cos, sin)


def _band(n, shape_q, nblocks_q):
    qi = lax.broadcasted_iota(jnp.int32, (BLOCK, 2 * BLOCK), 0)
    kj = lax.broadcasted_iota(jnp.int32, (BLOCK, 2 * BLOCK), 1)
    dist = BLOCK + qi - kj
    return (dist >= 0) & (dist <= BLOCK) & ((n > 0) | (kj >= BLOCK))


def _col(tile, h):
    return tile[:, h * HEAD:h * HEAD + 1]


def attn_fwd_branch(q, k, v, state, dil, last, *, name):
    s, d = q.shape
    r = s // dil
    nb = r // BLOCK
    first = state is None

    def body(*refs):
        q_ref, kc_ref, kp_ref, vc_ref, vp_ref = refs[:5]
        if first:
            outs = refs[5:]
        else:
            acc_ref, m_ref, l_ref = refs[5:8]
            outs = refs[8:]
        n = pl.program_id(1)
        valid = _band(n, None, None)
        lo = _pair_masks()
        keep = _head_keep(BLOCK)
        for p in range(d // 128):
            cs = slice(p * 128, (p + 1) * 128)
            qp = q_ref[:, cs]
            kw = jnp.concatenate([kp_ref[:, cs], kc_ref[:, cs]], axis=0)
            vw = jnp.concatenate([vp_ref[:, cs], vc_ref[:, cs]], axis=0)
            new_m, new_l, alpha, pv = [], [], [], []
            for h in range(2):
                sc = jnp.where(valid, _dot(qp * keep[h], kw, NT), NEG)
                mx = jnp.max(sc, axis=-1, keepdims=True)
                if first:
                    m_new = mx
                else:
                    m_old = _col(m_ref[:, cs], h)
                    m_new = jnp.maximum(m_old, mx)
                    alpha.append(jnp.exp(m_old - m_new))
                pr = jnp.exp(sc - m_new)
                new_m.append(m_new)
                new_l.append(jnp.sum(pr, axis=-1, keepdims=True))
                pv.append(_dot(pr.astype(BF16), vw, NN))
            m_t = jnp.where(lo, new_m[0], new_m[1])
            l_t = jnp.where(lo, new_l[0], new_l[1])
            acc_t = jnp.where(lo, pv[0], pv[1])
            if not first:
                a_t = jnp.where(lo, alpha[0], alpha[1])
                l_t = a_t * l_ref[:, cs] + l_t
                acc_t = a_t * acc_ref[:, cs] + acc_t
            if last:
                outs[0][:, cs] = acc_t / l_t
                outs[1][:, cs] = m_t + jnp.log(l_t)
            else:
                outs[0][:, cs] = acc_t
                outs[1][:, cs] = m_t
                outs[2][:, cs] = l_t

    view = lambda a: a.reshape(r, dil * d)
    cur = pl.BlockSpec((BLOCK, d), lambda rr, n: (n, rr))
    prv = pl.BlockSpec((BLOCK, d), lambda rr, n: (jnp.maximum(n - 1, 0), rr))
    args = [view(q), view(k), view(k), view(v), view(v)]
    in_specs = [cur, cur, prv, cur, prv]
    if not first:
        args += [view(a) for a in state]
        in_specs += [cur, cur, cur]
    n_out = 2 if last else 3
    outs = pl.pallas_call(
        body, name=name, grid=(dil, nb), in_specs=in_specs, out_specs=[cur] * n_out,
        out_shape=[jax.ShapeDtypeStruct((r, dil * d), F32)] * n_out,
        compiler_params=_params(("parallel", "parallel")))(*args)
    return tuple(o.reshape(s, d) for o in outs)


def attn_delta(do, o, *, name):
    s, d = do.shape

    def body(do_ref, o_ref, dl_ref, dob_ref):
        lane = lax.broadcasted_iota(jnp.int32, (ROWS, 128), 1)
        lo = lane < HEAD
        for p in range(d // 128):
            cs = slice(p * 128, (p + 1) * 128)
            dv = do_ref[:, cs]
            pr = dv * o_ref[:, cs]
            s0 = jnp.sum(jnp.where(lo, pr, 0.0), axis=-1, keepdims=True)
            s1 = jnp.sum(jnp.where(lo, 0.0, pr), axis=-1, keepdims=True)
            dl_ref[:, cs] = jnp.where(lo, s0, s1)
            dob_ref[:, cs] = dv.astype(BF16)

    row = pl.BlockSpec((ROWS, d), lambda i: (i, 0))
    return pl.pallas_call(
        body, name=name, grid=(s // ROWS,), in_specs=[row, row], out_specs=[row, row],
        out_shape=[jax.ShapeDtypeStruct((s, d), F32), jax.ShapeDtypeStruct((s, d), BF16)],
        compiler_params=_params(("parallel",)))(do, o)


def attn_dq_branch(q, k, v, dob, lse, delta, dq_in, dil, *, name):
    s, d = q.shape
    r = s // dil
    nb = r // BLOCK
    first = dq_in is None

    def body(*refs):
        q_ref, kc_ref, kp_ref, vc_ref, vp_ref, do_ref, l_ref, dl_ref = refs[:8]
        o_ref = refs[-1]
        n = pl.program_id(1)
        valid = _band(n, None, None)
        lo = _pair_masks()
        keep = _head_keep(BLOCK)
        for p in range(d // 128):
            cs = slice(p * 128, (p + 1) * 128)
            qp = q_ref[:, cs]
            dop = do_ref[:, cs]
            kw = jnp.concatenate([kp_ref[:, cs], kc_ref[:, cs]], axis=0)
            vw = jnp.concatenate([vp_ref[:, cs], vc_ref[:, cs]], axis=0)
            lt = l_ref[:, cs]
            dt = dl_ref[:, cs]
            res = []
            for h in range(2):
                pr = jnp.where(valid, jnp.exp(_dot(qp * keep[h], kw, NT) - _col(lt, h)), 0.0)
                ds = pr * (_dot(dop * keep[h], vw, NT) - _col(dt, h))
                res.append(_dot(ds.astype(BF16), kw, NN))
            dq_t = jnp.where(lo, res[0], res[1])
            if not first:
                dq_t = dq_t + refs[8][:, cs]
            o_ref[:, cs] = dq_t

    view = lambda a: a.reshape(r, dil * d)
    cur = pl.BlockSpec((BLOCK, d), lambda rr, n: (n, rr))
    prv = pl.BlockSpec((BLOCK, d), lambda rr, n: (jnp.maximum(n - 1, 0), rr))
    args = [view(q), view(k), view(k), view(v), view(v), view(dob), view(lse), view(delta)]
    in_specs = [cur, cur, prv, cur, prv, cur, cur, cur]
    if not first:
        args.append(view(dq_in))
        in_specs.append(cur)
    out = pl.pallas_call(
        body, name=name, grid=(dil, nb), in_specs=in_specs, out_specs=cur,
        out_shape=jax.ShapeDtypeStruct((r, dil * d), F32),
        compiler_params=_params(("parallel", "parallel")))(*args)
    return out.reshape(s, d)


def attn_dkv_branch(q, k, v, dob, lse, delta, dkv_in, dil, *, name):
    s, d = q.shape
    r = s // dil
    nb = r // BLOCK
    first = dkv_in is None

    def body(*refs):
        (k_ref, v_ref, qc_ref, qn_ref, doc_ref, don_ref, lc_ref, ln_ref, dc_ref, dn_ref) = refs[:10]
        dk_ref, dv_ref = refs[-2:]
        n = pl.program_id(1)
        qi = lax.broadcasted_iota(jnp.int32, (2 * BLOCK, BLOCK), 0)
        kj = lax.broadcasted_iota(jnp.int32, (2 * BLOCK, BLOCK), 1)
        valid = ((qi < BLOCK) & (kj <= qi)) | ((qi >= BLOCK) & (kj >= qi - BLOCK) & (n < nb - 1))
        keep = _head_keep(2 * BLOCK)
        lo = _pair_masks()
        for p in range(d // 128):
            cs = slice(p * 128, (p + 1) * 128)
            kp = k_ref[:, cs]
            vp = v_ref[:, cs]
            qab = jnp.concatenate([qc_ref[:, cs], qn_ref[:, cs]], axis=0)
            doab = jnp.concatenate([doc_ref[:, cs], don_ref[:, cs]], axis=0)
            lt = jnp.concatenate([lc_ref[:, cs], ln_ref[:, cs]], axis=0)
            dt = jnp.concatenate([dc_ref[:, cs], dn_ref[:, cs]], axis=0)
            rk, rv = [], []
            for h in range(2):
                pr = jnp.where(valid, jnp.exp(_dot(qab * keep[h], kp, NT) - _col(lt, h)), 0.0)
                ds = pr * (_dot(doab * keep[h], vp, NT) - _col(dt, h))
                rv.append(_dot(pr.astype(BF16), doab, TN))
                rk.append(_dot(ds.astype(BF16), qab, TN))
            dk_t = jnp.where(lo, rk[0], rk[1])
            dv_t = jnp.where(lo, rv[0], rv[1])
            if not first:
                dk_t = dk_t + refs[10][:, cs]
                dv_t = dv_t + refs[11][:, cs]
            dk_ref[:, cs] = dk_t
            dv_ref[:, cs] = dv_t

    view = lambda a: a.reshape(r, dil * d)
    cur = pl.BlockSpec((BLOCK, d), lambda rr, n: (n, rr))
    nxt = pl.BlockSpec((BLOCK, d), lambda rr, n: (jnp.minimum(n + 1, nb - 1), rr))
    args = [view(k), view(v), view(q), view(q), view(dob), view(dob), view(lse), view(lse), view(delta), view(delta)]
    in_specs = [cur, cur, cur, nxt, cur, nxt, cur, nxt, cur, nxt]
    if not first:
        args += [view(a) for a in dkv_in]
        in_specs += [cur, cur]
    dk, dv = pl.pallas_call(
        body, name=name, grid=(dil, nb), in_specs=in_specs, out_specs=[cur, cur],
        out_shape=[jax.ShapeDtypeStruct((r, dil * d), F32)] * 2,
        compiler_params=_params(("parallel", "parallel")))(*args)
    return dk.reshape(s, d), dv.reshape(s, d)


def adamw(parts, w, m, v, *, name):
    npart, r, c = parts.shape
    tr = r
    for cand in (512, 256, 128, 64, 32, 16, 8):
        if r % cand == 0 and cand * c * 4 <= 2 * 1024 * 1024:
            tr = cand
            break

    def body(p_ref, w_ref, m_ref, v_ref, g_ref, d_ref, nm_ref, nv_ref):
        g = p_ref[0].astype(F32)
        for i in range(1, npart):
            g = g + p_ref[i].astype(F32)
        m2 = B1 * m_ref[...] + (1.0 - B1) * g
        v2 = B2 * v_ref[...] + (1.0 - B2) * (g * g)
        m_hat = m2 / (1.0 - B1 ** STEP)
        v_hat = v2 / (1.0 - B2 ** STEP)
        g_ref[...] = g
        d_ref[...] = -LR * (m_hat / (jnp.sqrt(v_hat) + ADAM_EPS) + WD * w_ref[...])
        nm_ref[...] = m2
        nv_ref[...] = v2

    blk = pl.BlockSpec((tr, c), lambda i: (i, 0))
    return pl.pallas_call(
        body, name=name, grid=(r // tr,),
        in_specs=[pl.BlockSpec((npart, tr, c), lambda i: (0, i, 0)), blk, blk, blk],
        out_specs=[blk] * 4, out_shape=[jax.ShapeDtypeStruct((r, c), F32)] * 4,
        compiler_params=_params(("parallel",)))(parts, w, m, v)


def _my_index():
    return 4 * lax.axis_index("x") + 2 * lax.axis_index("y") + lax.axis_index("c")


def exchange(arrays, scatter, *, name):
    nt = len(arrays)

    def body(*refs):
        ins = refs[:nt]
        outs = refs[nt:2 * nt]
        send_sems, recv_sems, local_sems = refs[2 * nt:]
        x, y, c = lax.axis_index("x"), lax.axis_index("y"), lax.axis_index("c")
        me = 4 * x + 2 * y + c
        copies = []
        for t in range(nt):
            src = ins[t].at[me] if scatter[t] else ins[t]
            cp = pltpu.make_async_copy(src, outs[t].at[me], local_sems.at[t])
            cp.start()
            copies.append(cp)
        remote = []
        for kk in range(1, N_DEV):
            px, py, pc = x ^ (kk >> 2), y ^ ((kk >> 1) & 1), c ^ (kk & 1)
            peer = 4 * px + 2 * py + pc
            for t in range(nt):
                src = ins[t].at[peer] if scatter[t] else ins[t]
                cp = pltpu.make_async_remote_copy(
                    src_ref=src, dst_ref=outs[t].at[me], send_sem=send_sems.at[t, kk], recv_sem=recv_sems.at[t, kk],
                    device_id=(px, py, pc), device_id_type=pl.DeviceIdType.MESH)
                cp.start()
                remote.append(cp)
        for cp in remote:
            cp.wait()
        for cp in copies:
            cp.wait()

    hbm = pl.BlockSpec(memory_space=pl.ANY)
    out_shape = [jax.ShapeDtypeStruct(a.shape if scatter[t] else (N_DEV,) + a.shape, a.dtype)
                 for t, a in enumerate(arrays)]
    return pl.pallas_call(
        body, name=name, in_specs=[hbm] * nt, out_specs=[hbm] * nt, out_shape=out_shape,
        scratch_shapes=[pltpu.SemaphoreType.DMA((nt, N_DEV)), pltpu.SemaphoreType.DMA((nt, N_DEV)),
                        pltpu.SemaphoreType.DMA((nt,))],
        compiler_params=pltpu.CompilerParams(has_side_effects=True))(*arrays)


def _cols_from_shards(g):
    g = jnp.moveaxis(g, 0, -2)
    return g.reshape(g.shape[:-2] + (g.shape[-2] * g.shape[-1],))


def _cols_to_shards(w):
    w = w.reshape(w.shape[:-1] + (N_DEV, w.shape[-1] // N_DEV))
    return jnp.moveaxis(w, -2, 0)


def _ffn_fwd(x, g, w_up, cw, cb, w_down, tag):
    h = rms_fwd(x, g, name=f"{tag}_norm")
    up = matmul(h, w_up, name=f"{tag}_up")
    act = ffn_mid_fwd(up, cw, cb, name=f"{tag}_mid")
    out = matmul(act, w_down, res=x, name=f"{tag}_down")
    return out, (h, up, act)


def _ffn_bwd(dx, dxb, x, saved, g, w_up, cw, cb, w_down, tag):
    h, up, act = saved
    dact = matmul(dxb, w_down, tb=True, name=f"{tag}_ddown")
    d_w_down = matmul_ta(act, dxb, name=f"{tag}_gdown")
    dug, duv, dcw, dcb = ffn_mid_bwd(dact, up, cw, cb, name=f"{tag}_dmid")
    dup = jnp.concatenate([dug, duv], axis=1)
    d_w_up = matmul_ta(h, dup, name=f"{tag}_gup")
    dh = matmul(dup, w_up, tb=True, name=f"{tag}_dup")
    dx2, dxb2, dg = rms_bwd(dh, x, g, dx, name=f"{tag}_dnorm")
    return dx2, dxb2, dict(norm_g=dg, w_up=d_w_up, conv_w=dcw, conv_b=dcb, w_down=d_w_down)


def local_step(x0, tgt, a, w_in, w_out, w_qkv, w_o, w_up, w_down, conv_w, odd_g, ffn_cw):
    s, d = x0.shape
    aw = a['even_v_ln_g'].shape[-1]
    causal = jnp.tril(jnp.ones((CHUNK, CHUNK), dtype=bool))
    wm = jnp.where(causal, a['even_w_s'][0], 0.0).astype(BF16)
    wmt = jnp.swapaxes(wm, 1, 2)
    bm = jnp.repeat(a['even_b_s'][0].T, HEAD, axis=1)
    sel = (jnp.arange(aw)[:, None] // HEAD == jnp.arange(128)[None, :]).astype(BF16)
    cos, sin = rope_tables(s)

    h0 = rms_fwd(x0, a['even_norm_g'], name="even_norm")
    z = matmul(h0, w_in, bias=a['even_b_in'], name="even_in")
    ycat, hc = even_mid_fwd(z, a['even_v_ln_g'], a['even_v_ln_b'], wm, bm, conv_w, a['even_conv_b'],
                            a['even_conv_ln_g'], a['even_conv_ln_b'], name="even_mid")
    x1 = matmul(ycat, w_out, res=x0, name="even_out")
    x2, ffn0 = _ffn_fwd(x1, a['ffn_norm_g'][0:1], w_up[0], ffn_cw[0], a['ffn_conv_b'][0:1], w_down[0], "ffn0")
    h2 = rms_fwd(x2, odd_g, name="odd_norm")
    qkv = matmul(h2, w_qkv, name="odd_qkv")
    q, k, v = rope_fwd(qkv, cos, sin, name="rope")
    state = None
    for bi, dil in enumerate(DILATIONS):
        state = attn_fwd_branch(q, k, v, state, dil, bi == len(DILATIONS) - 1, name=f"attn_fwd_d{dil}")
    o, lse = state
    x3 = matmul(o, w_o, res=x2, name="odd_out")
    x4, ffn1 = _ffn_fwd(x3, a['ffn_norm_g'][1:2], w_up[1], ffn_cw[1], a['ffn_conv_b'][1:2], w_down[1], "ffn1")
    loss_t, dx, dxb, d_final_g = final_loss_bwd(x4, a['final_norm_g'].reshape(1, -1), tgt, name="final_loss")

    dx, dxb, g1 = _ffn_bwd(dx, dxb, x3, ffn1, a['ffn_norm_g'][1:2], w_up[1], ffn_cw[1], a['ffn_conv_b'][1:2],
                           w_down[1], "ffn1")
    do = matmul(dxb, w_o, tb=True, name="odd_dout")
    d_w_o = matmul_ta(o, dxb, name="odd_gout")
    delta, dob = attn_delta(do, o, name="attn_delta")
    dq, dkv = None, None
    for dil in DILATIONS:
        dq = attn_dq_branch(q, k, v, dob, lse, delta, dq, dil, name=f"attn_dq_d{dil}")
        dkv = attn_dkv_branch(q, k, v, dob, lse, delta, dkv, dil, name=f"attn_dkv_d{dil}")
    dqkv = rope_bwd(dq, dkv[0], dkv[1], cos, sin, name="rope_bwd")
    d_w_qkv = matmul_ta(h2, dqkv, name="odd_gqkv")
    dh2 = matmul(dqkv, w_qkv, tb=True, name="odd_dqkv")
    dx, dxb, d_odd_g = rms_bwd(dh2, x2, odd_g, dx, name="odd_dnorm")
    dx, dxb, g0 = _ffn_bwd(dx, dxb, x1, ffn0, a['ffn_norm_g'][0:1], w_up[0], ffn_cw[0], a['ffn_conv_b'][0:1],
                           w_down[0], "ffn0")
    dycat = matmul(dxb, w_out, tb=True, name="even_dout")
    d_w_out = matmul_ta(ycat, dxb, name="even_gout")
    (dza, dhc, dba, dvg, dvb, dwm, dbs, dcg, dcbeta, dcb) = even_mid_bwd_rows(
        dycat, z, hc, a['even_v_ln_g'], a['even_v_ln_b'], wm, wmt, bm, sel, a['even_conv_ln_g'],
        a['even_conv_ln_b'], name="even_dmid_rows")
    dzb, dbb, dcw = even_conv_bwd(dhc, z, conv_w, name="even_dmid_conv")
    dz = jnp.concatenate([dza, dzb], axis=1)
    d_w_in = matmul_ta(h0, dz, name="even_gin")
    dh0 = matmul(dz, w_in, tb=True, name="even_din")
    grad_x, _, d_even_g = rms_bwd(dh0, x0, a['even_norm_g'], dx, name="even_dnorm")

    nh = a['even_w_s'].shape[1]
    small_grads = {
        'even_norm_g': d_even_g, 'even_b_in': jnp.concatenate([dba, dbb], axis=1), 'even_v_ln_g': dvg,
        'even_v_ln_b': dvb, 'even_w_s': jnp.where(causal, dwm, 0.0)[None], 'even_b_s': dbs[:, :nh].T[None],
        'even_conv_w': dcw[None], 'even_conv_b': dcb, 'even_conv_ln_g': dcg, 'even_conv_ln_b': dcbeta,
        'odd_norm_g': d_odd_g, 'ffn_norm_g': jnp.concatenate([g0['norm_g'], g1['norm_g']], axis=0),
        'ffn_conv_w': jnp.stack([g0['conv_w'], g1['conv_w']]),
        'ffn_conv_b': jnp.concatenate([g0['conv_b'], g1['conv_b']], axis=0),
        'final_norm_g': d_final_g.reshape(-1),
    }
    big_grads = dict(even_w_in=d_w_in, even_w_out=d_w_out, odd_w_qkv=d_w_qkv, odd_w_o=d_w_o,
                     ffn_w_up=jnp.stack([g0['w_up'], g1['w_up']]), ffn_w_down=jnp.stack([g0['w_down'], g1['w_down']]))
    return loss_t, grad_x, big_grads, small_grads


BIG = ['even_w_in', 'even_w_out', 'odd_w_qkv', 'odd_w_o', 'ffn_w_up', 'ffn_w_down']


def _as_tiles(flat):
    return jnp.pad(flat, (0, (-flat.size) % 1024)).reshape(-1, 128)


def kernel(*args):
    a = dict(zip(NAMES + ['loss_target'] + ['m_' + n for n in WEIGHTS] + ['v_' + n for n in WEIGHTS], args))
    x0 = a['x'][0]
    tgt = a['loss_target'][0]
    s, d = x0.shape
    me = _my_index()

    small_local = _as_tiles(jnp.concatenate([a['even_conv_w'].reshape(-1), a['odd_norm_g'].reshape(-1),
                                             a['ffn_conv_w'].reshape(-1)]))
    gathered = exchange([a[n].astype(BF16) for n in BIG] + [small_local], [False] * 7, name="gather_weights")
    gw = dict(zip(BIG, gathered[:6]))
    w_in = _cols_from_shards(gw['even_w_in'])[0]
    w_out = gw['even_w_out'].reshape(-1, d)
    w_qkv = _cols_from_shards(gw['odd_w_qkv'])[0]
    w_o = gw['odd_w_o'].reshape(-1, d)
    w_up = _cols_from_shards(gw['ffn_w_up'])
    w_down = jnp.moveaxis(gw['ffn_w_down'], 0, 1).reshape(2, -1, d)
    gs = gathered[6].reshape(N_DEV, -1)
    n_cw, n_og, n_fw = a['even_conv_w'].size, a['odd_norm_g'].size, a['ffn_conv_w'].size
    conv_w = _cols_from_shards(gs[:, :n_cw].reshape((N_DEV,) + a['even_conv_w'].shape))[0]
    odd_g = gs[:, n_cw:n_cw + n_og].reshape(1, -1)
    ffn_cw = _cols_from_shards(gs[:, n_cw + n_og:n_cw + n_og + n_fw].reshape((N_DEV,) + a['ffn_conv_w'].shape))

    loss_t, grad_x, big_grads, small_grads = local_step(x0, tgt, a, w_in, w_out, w_qkv, w_o, w_up, w_down,
                                                        conv_w, odd_g, ffn_cw)
    loss = lax.psum(loss_t[0, 0], ("x", "y", "c"))

    small_names = list(small_grads)
    n_small = sum(small_grads[n].size for n in small_names)
    small_flat = _as_tiles(jnp.concatenate([small_grads[n].reshape(-1) for n in small_names]))
    pieces = [
        _cols_to_shards(big_grads['even_w_in'][None]), big_grads['even_w_out'].reshape(N_DEV, 1, -1, d),
        _cols_to_shards(big_grads['odd_w_qkv'][None]), big_grads['odd_w_o'].reshape(N_DEV, 1, -1, d),
        _cols_to_shards(big_grads['ffn_w_up']),
        jnp.moveaxis(big_grads['ffn_w_down'].reshape(2, N_DEV, -1, d), 1, 0),
    ]
    received = exchange(pieces + [small_flat], [True] * 6 + [False], name="exchange_grads")

    results = {}
    for n, parts in zip(BIG, received[:6]):
        shp = a[n].shape
        flat = lambda t: t.reshape(-1, shp[-1])
        outs = adamw(parts.reshape(N_DEV, -1, shp[-1]), flat(a[n]), flat(a['m_' + n]), flat(a['v_' + n]),
                     name=f"adamw_{n}")
        results[n] = [t.reshape(shp) for t in outs]

    rs = received[6].reshape(N_DEV, -1)[:, :n_small]
    parts, offs = [], 0
    for n in small_names:
        full = small_grads[n].shape
        piece = rs[:, offs:offs + small_grads[n].size].reshape((N_DEV,) + full)
        offs += small_grads[n].size
        shp = a[n].shape
        if shp != full:
            width = shp[-1]
            piece = lax.dynamic_slice_in_dim(piece, me * width, width, axis=piece.ndim - 1)
        parts.append(piece.reshape(N_DEV, -1))
    parts = jnp.concatenate(parts, axis=1)
    pad = (-parts.shape[1]) % 1024
    cat = lambda pre: _as_tiles(jnp.concatenate([a[pre + n].reshape(-1) for n in small_names]))
    outs = adamw(jnp.pad(parts, ((0, 0), (0, pad))).reshape(N_DEV, -1, 128), cat(''), cat('m_'), cat('v_'),
                 name="adamw_small")
    offs = 0
    for n in small_names:
        size = a[n].size
        results[n] = [t.reshape(-1)[offs:offs + size].reshape(a[n].shape) for t in outs]
        offs += size

    out = [loss, grad_x[None]]
    for i in range(4):
        out += [results[n][i] for n in WEIGHTS]
    return tuple(out)
```

```python
import functools
import math

import jax
import jax.numpy as jnp
from jax import lax
from jax.experimental import pallas as pl
from jax.experimental.pallas import tpu as pltpu

F32 = jnp.float32
BF16 = jnp.bfloat16

N_DEV = 8
EPS = 1e-6
NEG = -1e30
HEAD = 64
CHUNK = 128
BLOCK = 128
CONV_K = 31
FFN_K = 3
DILATIONS = (1, 4, 16)
ROPE_THETA = 10000.0
LR, B1, B2, ADAM_EPS, WD, STEP = 0.001, 0.9, 0.999, 1e-08, 0.01, 10

VMEM_LIMIT = 56 * 1024 * 1024
VMEM_BUDGET = 32 * 1024 * 1024
ROWS = 512
HALO = 32
FHALO = 8

NAMES = ['x', 'even_norm_g', 'even_w_in', 'even_b_in', 'even_v_ln_g', 'even_v_ln_b', 'even_w_s', 'even_b_s',
         'even_conv_w', 'even_conv_b', 'even_conv_ln_g', 'even_conv_ln_b', 'even_w_out', 'odd_norm_g',
         'odd_w_qkv', 'odd_w_o', 'ffn_norm_g', 'ffn_w_up', 'ffn_conv_w', 'ffn_conv_b', 'ffn_w_down',
         'final_norm_g']
WEIGHTS = NAMES[1:]


def _params(sem=None):
    return pltpu.CompilerParams(dimension_semantics=sem, vmem_limit_bytes=VMEM_LIMIT)


def _sigmoid(x):
    return 1.0 / (1.0 + jnp.exp(-x))


def _gelu(x):
    c = math.sqrt(2.0 / math.pi)
    return 0.5 * x * (1.0 + jnp.tanh(c * (x + 0.044715 * x * x * x)))


def _gelu_grad(x):
    c = math.sqrt(2.0 / math.pi)
    t = jnp.tanh(c * (x + 0.044715 * x * x * x))
    return 0.5 * (1.0 + t) + 0.5 * x * (1.0 - t * t) * c * (1.0 + 3.0 * 0.044715 * x * x)


def _ln_stats(x):
    mu = jnp.mean(x, axis=-1, keepdims=True)
    xc = x - mu
    rstd = lax.rsqrt(jnp.mean(xc * xc, axis=-1, keepdims=True) + EPS)
    return xc * rstd, rstd


def _ln_bwd(dy, xhat, rstd, g):
    dxh = dy * g
    return rstd * (dxh - jnp.mean(dxh, axis=-1, keepdims=True) - xhat * jnp.mean(dxh * xhat, axis=-1, keepdims=True))


def _colsum(x):
    return jnp.sum(x, axis=0, keepdims=True)


def _split3(x):
    hi = x.astype(BF16)
    r = x - hi.astype(F32)
    mid = r.astype(BF16)
    lo = (r - mid.astype(F32)).astype(BF16)
    return hi, mid, lo


def _dot(a, b, dims):
    return lax.dot_general(a, b, (dims, ((), ())), preferred_element_type=F32)


NN = ((1,), (0,))
NT = ((1,), (1,))
TN = ((0,), (0,))


def _divisors(n, cands):
    return [c for c in cands if c <= n and n % c == 0]


def _pick_tiles(m, n, k, a_bytes, b_bytes, o_bytes, extra_bytes):
    best = None
    for tm in _divisors(m, (1024, 512, 256, 128)):
        for tn in _divisors(n, (1408, 1024, 768, 704, 512, 384, 256, 128)):
            if tn % 128:
                continue
            need = 2 * (tm * k * a_bytes + k * tn * b_bytes + tm * tn * (o_bytes + extra_bytes)) + tm * tn * 4
            if need <= VMEM_BUDGET and (best is None or tm * tn > best[0] * best[1]):
                best = (tm, tn)
    assert best is not None, (m, n, k)
    return best


def matmul(a, b, *, tb=False, bias=None, res=None, dep=None, out_dtype=F32, name):
    m, k = a.shape
    n = b.shape[0] if tb else b.shape[1]
    assert (b.shape[1] if tb else b.shape[0]) == k
    tm, tn = _pick_tiles(m, n, k, a.dtype.itemsize, b.dtype.itemsize, jnp.dtype(out_dtype).itemsize,
                         4 if res is not None else 0)

    def body(*refs):
        a_ref, b_ref = refs[:2]
        o_ref = refs[-1]
        acc = _dot(a_ref[...].astype(BF16), b_ref[...].astype(BF16), NT if tb else NN)
        pos = 2
        if bias is not None:
            acc = acc + refs[pos][...]
            pos += 1
        if res is not None:
            acc = acc + refs[pos][...]
        o_ref[...] = acc.astype(out_dtype)

    in_specs = [pl.BlockSpec((tm, k), lambda i, j: (i, 0)),
                pl.BlockSpec((tn, k), lambda i, j: (j, 0)) if tb else pl.BlockSpec((k, tn), lambda i, j: (0, j))]
    args = [a, b]
    if bias is not None:
        in_specs.append(pl.BlockSpec((1, tn), lambda i, j: (0, j)))
        args.append(bias)
    if res is not None:
        in_specs.append(pl.BlockSpec((tm, tn), lambda i, j: (i, j)))
        args.append(res)
    if dep is not None:
        in_specs.append(pl.BlockSpec(memory_space=pl.ANY))
        args.append(dep)
    return pl.pallas_call(
        body, name=name, grid=(m // tm, n // tn), in_specs=in_specs,
        out_specs=pl.BlockSpec((tm, tn), lambda i, j: (i, j)),
        out_shape=jax.ShapeDtypeStruct((m, n), out_dtype),
        compiler_params=_params(("parallel", "parallel")))(*args)


def matmul_ta(a, b, *, out_dtype=BF16, name):
    s, m = a.shape
    n = b.shape[1]
    assert b.shape[0] == s
    best = None
    for tm in _divisors(m, (512, 256, 128)):
        for tn in _divisors(n, (1024, 512, 384, 256, 128)):
            need = 2 * (s * tm * a.dtype.itemsize + s * tn * b.dtype.itemsize + tm * tn * 2) + tm * tn * 4
            if need <= VMEM_BUDGET and (best is None or tm * tn > best[0] * best[1]):
                best = (tm, tn)
    tm, tn = best

    def body(a_ref, b_ref, o_ref):
        o_ref[...] = _dot(a_ref[...].astype(BF16), b_ref[...].astype(BF16), TN).astype(out_dtype)

    return pl.pallas_call(
        body, name=name, grid=(m // tm, n // tn),
        in_specs=[pl.BlockSpec((s, tm), lambda i, j: (0, i)), pl.BlockSpec((s, tn), lambda i, j: (0, j))],
        out_specs=pl.BlockSpec((tm, tn), lambda i, j: (i, j)),
        out_shape=jax.ShapeDtypeStruct((m, n), out_dtype),
        compiler_params=_params(("parallel", "parallel")))(a, b)


def rms_fwd(x, g, *, name):
    s, d = x.shape

    def body(x_ref, g_ref, h_ref):
        xv = x_ref[...]
        r = lax.rsqrt(jnp.mean(xv * xv, axis=-1, keepdims=True) + EPS)
        h_ref[...] = (xv * r * g_ref[...]).astype(BF16)

    return pl.pallas_call(
        body, name=name, grid=(s // ROWS,),
        in_specs=[pl.BlockSpec((ROWS, d), lambda i: (i, 0)), pl.BlockSpec((1, d), lambda i: (0, 0))],
        out_specs=pl.BlockSpec((ROWS, d), lambda i: (i, 0)),
        out_shape=jax.ShapeDtypeStruct((s, d), BF16),
        compiler_params=_params(("parallel",)))(x, g)


def rms_bwd(dh, x, g, dres, *, name):
    s, d = x.shape

    def body(dh_ref, x_ref, g_ref, dres_ref, dx_ref, dxb_ref, dg_ref):
        xv = x_ref[...]
        r = lax.rsqrt(jnp.mean(xv * xv, axis=-1, keepdims=True) + EPS)
        xhat = xv * r
        dhv = dh_ref[...]
        dxh = dhv * g_ref[...]
        dx = dres_ref[...] + r * (dxh - xhat * jnp.mean(dxh * xhat, axis=-1, keepdims=True))
        dx_ref[...] = dx
        dxb_ref[...] = dx.astype(BF16)

        @pl.when(pl.program_id(0) == 0)
        def _():
            dg_ref[...] = jnp.zeros_like(dg_ref)
        dg_ref[...] += _colsum(dhv * xhat)

    row = pl.BlockSpec((ROWS, d), lambda i: (i, 0))
    vec = pl.BlockSpec((1, d), lambda i: (0, 0))
    return pl.pallas_call(
        body, name=name, grid=(s // ROWS,),
        in_specs=[row, row, vec, row], out_specs=[row, row, vec],
        out_shape=[jax.ShapeDtypeStruct((s, d), F32), jax.ShapeDtypeStruct((s, d), BF16),
                   jax.ShapeDtypeStruct((1, d), F32)],
        compiler_params=_params(("arbitrary",)))(dh, x, g, dres)


def final_loss_bwd(x, g, tgt, *, name):
    s, d = x.shape

    def body(x_ref, g_ref, t_ref, loss_ref, dx_ref, dxb_ref, dg_ref):
        xv = x_ref[...]
        gv = g_ref[...]
        r = lax.rsqrt(jnp.mean(xv * xv, axis=-1, keepdims=True) + EPS)
        xhat = xv * r
        e = xhat * gv - t_ref[...]
        dy = e * (1.0 / d)
        dxh = dy * gv
        dx = r * (dxh - xhat * jnp.mean(dxh * xhat, axis=-1, keepdims=True))
        dx_ref[...] = dx
        dxb_ref[...] = dx.astype(BF16)

        @pl.when(pl.program_id(0) == 0)
        def _():
            dg_ref[...] = jnp.zeros_like(dg_ref)
            loss_ref[...] = jnp.zeros_like(loss_ref)
        dg_ref[...] += _colsum(dy * xhat)
        loss_ref[...] += 0.5 * jnp.sum(jnp.mean(e * e, axis=-1, keepdims=True))

    row = pl.BlockSpec((ROWS, d), lambda i: (i, 0))
    vec = pl.BlockSpec((1, d), lambda i: (0, 0))
    one = pl.BlockSpec((8, 128), lambda i: (0, 0))
    return pl.pallas_call(
        body, name=name, grid=(s // ROWS,),
        in_specs=[row, vec, row], out_specs=[one, row, row, vec],
        out_shape=[jax.ShapeDtypeStruct((8, 128), F32), jax.ShapeDtypeStruct((s, d), F32),
                   jax.ShapeDtypeStruct((s, d), BF16), jax.ShapeDtypeStruct((1, d), F32)],
        compiler_params=_params(("arbitrary",)))(x, g, tgt)


def _pair_masks():
    lane = lax.broadcasted_iota(jnp.int32, (CHUNK, 128), 1)
    return lane < HEAD


def _head_keep(rows):
    lane = lax.broadcasted_iota(jnp.int32, (rows, 128), 1)
    first = jnp.where(lane < HEAD, 1.0, 0.0)
    return first.astype(BF16), (1.0 - first).astype(BF16)


def _gating_mixed(vn_b, wm_ref, lo):
    rows, aw = vn_b.shape
    out = []
    for c in range(rows // CHUNK):
        tiles = []
        for p in range(aw // 128):
            vp = vn_b[c * CHUNK:(c + 1) * CHUNK, p * 128:(p + 1) * 128]
            r0 = _dot(wm_ref[2 * p], vp, NN)
            r1 = _dot(wm_ref[2 * p + 1], vp, NN)
            tiles.append(jnp.where(lo, r0, r1))
        out.append(jnp.concatenate(tiles, axis=1))
    return jnp.concatenate(out, axis=0)


def even_mid_fwd(z, vg, vb, wm, bm, cw, cb, cg, cbeta, *, name):
    s, zw = z.shape
    aw = zw // 4
    nblk = s // ROWS

    def body(z_ref, zp_ref, vg_ref, vb_ref, wm_ref, bm_ref, cw_ref, cb_ref, cg_ref, cbeta_ref,
             y_ref, hc_ref, ext_ref):
        i = pl.program_id(0)
        lo = _pair_masks()
        u = _gelu(z_ref[:, 0:aw])
        v = _gelu(z_ref[:, aw:2 * aw])
        vhat, _ = _ln_stats(v)
        vn = (vhat * vg_ref[...] + vb_ref[...]).astype(BF16)
        mixed = _gating_mixed(vn, wm_ref, lo)
        bias = jnp.concatenate([bm_ref[...]] * (ROWS // CHUNK), axis=0)
        y_ref[:, 0:aw] = (u * (mixed + bias)).astype(BF16)

        hb = z_ref[:, 2 * aw:3 * aw] * _sigmoid(z_ref[:, 3 * aw:4 * aw])
        hbp = zp_ref[:, 0:aw] * _sigmoid(zp_ref[:, aw:2 * aw])
        ext_ref[0:HALO, :] = jnp.where(i > 0, hbp, 0.0)
        ext_ref[HALO:HALO + ROWS, :] = hb
        acc = jnp.zeros((ROWS, aw), F32) + cb_ref[...]
        for k in range(CONV_K):
            acc = acc + cw_ref[k:k + 1, :] * ext_ref[pl.ds(HALO - (CONV_K - 1) + k, ROWS), :]
        hc_ref[...] = acc
        hhat, _ = _ln_stats(acc)
        hn = hhat * cg_ref[...] + cbeta_ref[...]
        y_ref[:, aw:2 * aw] = (hn * _sigmoid(hn)).astype(BF16)

    hb_per = ROWS // HALO
    vec = pl.BlockSpec((1, aw), lambda i: (0, 0))
    return pl.pallas_call(
        body, name=name, grid=(nblk,),
        in_specs=[pl.BlockSpec((ROWS, zw), lambda i: (i, 0)),
                  pl.BlockSpec((HALO, 2 * aw), lambda i: (jnp.maximum(i * hb_per - 1, 0), 1)),
                  vec, vec,
                  pl.BlockSpec(wm.shape, lambda i: (0, 0, 0)),
                  pl.BlockSpec((CHUNK, aw), lambda i: (0, 0)),
                  pl.BlockSpec((CONV_K, aw), lambda i: (0, 0)), vec, vec, vec],
        out_specs=[pl.BlockSpec((ROWS, 2 * aw), lambda i: (i, 0)), pl.BlockSpec((ROWS, aw), lambda i: (i, 0))],
        out_shape=[jax.ShapeDtypeStruct((s, 2 * aw), BF16), jax.ShapeDtypeStruct((s, aw), F32)],
        scratch_shapes=[pltpu.VMEM((HALO + ROWS, aw), F32)],
        compiler_params=_params(("parallel",)))(z, z, vg, vb, wm, bm, cw, cb, cg, cbeta)


def even_mid_bwd_rows(dy, z, hc, vg, vb, wm, wmt, bm, sel, cg, cbeta, *, name):
    s, zw = z.shape
    aw = zw // 4
    nh = wm.shape[0]

    def body(dy_ref, z_ref, hc_ref, vg_ref, vb_ref, wm_ref, wmt_ref, bm_ref, sel_ref, cg_ref, cbeta_ref,
             dza_ref, dhc_ref, dba_ref, dvg_ref, dvb_ref, dwm_ref, dbs_ref, dcg_ref, dcbeta_ref, dcb_ref):
        @pl.when(pl.program_id(0) == 0)
        def _():
            for r in (dba_ref, dvg_ref, dvb_ref, dwm_ref, dbs_ref, dcg_ref, dcbeta_ref, dcb_ref):
                r[...] = jnp.zeros_like(r)

        lo = _pair_masks()
        keep = _head_keep(CHUNK)
        zu = z_ref[:, 0:aw]
        zv = z_ref[:, aw:2 * aw]
        u = _gelu(zu)
        v = _gelu(zv)
        vhat, vrstd = _ln_stats(v)
        vn = (vhat * vg_ref[...] + vb_ref[...]).astype(BF16)
        mixed = _gating_mixed(vn, wm_ref, lo)
        bias = jnp.concatenate([bm_ref[...]] * (ROWS // CHUNK), axis=0)
        dya = dy_ref[:, 0:aw]
        du = dya * (mixed + bias)
        dmix = dya * u
        dmix_b = dmix.astype(BF16)

        dvn_rows = []
        for c in range(ROWS // CHUNK):
            rs = slice(c * CHUNK, (c + 1) * CHUNK)
            tiles = []
            for p in range(aw // 128):
                cs = slice(p * 128, (p + 1) * 128)
                dm = dmix_b[rs, cs]
                dm0 = dm * keep[0]
                dm1 = dm * keep[1]
                vp = vn[rs, cs]
                tiles.append(_dot(wmt_ref[2 * p], dm0, NN) + _dot(wmt_ref[2 * p + 1], dm1, NN))
                dwm_ref[2 * p] += _dot(dm0, vp, NT)
                dwm_ref[2 * p + 1] += _dot(dm1, vp, NT)
            dvn_rows.append(jnp.concatenate(tiles, axis=1))
            acc = jnp.zeros((CHUNK, 128), F32)
            for part in _split3(dmix[rs, :]):
                acc = acc + _dot(part, sel_ref[...], NN)
            dbs_ref[...] += acc
        dvn = jnp.concatenate(dvn_rows, axis=0)
        dvg_ref[...] += _colsum(dvn * vhat)
        dvb_ref[...] += _colsum(dvn)
        dv = _ln_bwd(dvn, vhat, vrstd, vg_ref[...])
        dzu = du * _gelu_grad(zu)
        dzv = dv * _gelu_grad(zv)
        dza_ref[:, 0:aw] = dzu.astype(BF16)
        dza_ref[:, aw:2 * aw] = dzv.astype(BF16)
        dba_ref[:, 0:aw] += _colsum(dzu)
        dba_ref[:, aw:2 * aw] += _colsum(dzv)

        hcv = hc_ref[...]
        hhat, hrstd = _ln_stats(hcv)
        hn = hhat * cg_ref[...] + cbeta_ref[...]
        sg = _sigmoid(hn)
        dhn = dy_ref[:, aw:2 * aw] * (sg * (1.0 + hn * (1.0 - sg)))
        dcg_ref[...] += _colsum(dhn * hhat)
        dcbeta_ref[...] += _colsum(dhn)
        dhc = _ln_bwd(dhn, hhat, hrstd, cg_ref[...])
        dhc_ref[...] = dhc
        dcb_ref[...] += _colsum(dhc)

    vec = pl.BlockSpec((1, aw), lambda i: (0, 0))
    vec2 = pl.BlockSpec((1, 2 * aw), lambda i: (0, 0))
    w3 = pl.BlockSpec(wm.shape, lambda i: (0, 0, 0))
    sq = pl.BlockSpec((CHUNK, 128), lambda i: (0, 0))
    return pl.pallas_call(
        body, name=name, grid=(s // ROWS,),
        in_specs=[pl.BlockSpec((ROWS, 2 * aw), lambda i: (i, 0)), pl.BlockSpec((ROWS, 2 * aw), lambda i: (i, 0)),
                  pl.BlockSpec((ROWS, aw), lambda i: (i, 0)), vec, vec, w3, w3,
                  pl.BlockSpec((CHUNK, aw), lambda i: (0, 0)), pl.BlockSpec((aw, 128), lambda i: (0, 0)), vec, vec],
        out_specs=[pl.BlockSpec((ROWS, 2 * aw), lambda i: (i, 0)), pl.BlockSpec((ROWS, aw), lambda i: (i, 0)),
                   vec2, vec, vec, w3, sq, vec, vec, vec],
        out_shape=[jax.ShapeDtypeStruct((s, 2 * aw), BF16), jax.ShapeDtypeStruct((s, aw), F32),
                   jax.ShapeDtypeStruct((1, 2 * aw), F32), jax.ShapeDtypeStruct((1, aw), F32),
                   jax.ShapeDtypeStruct((1, aw), F32), jax.ShapeDtypeStruct(wm.shape, F32),
                   jax.ShapeDtypeStruct((CHUNK, 128), F32), jax.ShapeDtypeStruct((1, aw), F32),
                   jax.ShapeDtypeStruct((1, aw), F32), jax.ShapeDtypeStruct((1, aw), F32)],
        compiler_params=_params(("arbitrary",)))(dy, z, hc, vg, vb, wm, wmt, bm, sel, cg, cbeta)


def even_conv_bwd(dhc, z, cw, *, name):
    s, zw = z.shape
    aw = zw // 4
    nblk = s // ROWS
    hb_per = ROWS // HALO

    def body(dc_ref, dn_ref, z_ref, zp_ref, cw_ref, dzb_ref, dbb_ref, dcw_ref, exth_ref, extd_ref):
        i = pl.program_id(0)

        @pl.when(i == 0)
        def _():
            dbb_ref[...] = jnp.zeros_like(dbb_ref)
            dcw_ref[...] = jnp.zeros_like(dcw_ref)

        a = z_ref[:, 0:aw]
        sg = _sigmoid(z_ref[:, aw:2 * aw])
        exth_ref[0:HALO, :] = jnp.where(i > 0, zp_ref[:, 0:aw] * _sigmoid(zp_ref[:, aw:2 * aw]), 0.0)
        exth_ref[HALO:HALO + ROWS, :] = a * sg
        dcur = dc_ref[...]
        extd_ref[0:ROWS, :] = dcur
        extd_ref[ROWS:ROWS + HALO, :] = jnp.where(i < nblk - 1, dn_ref[...], 0.0)
        dhb = jnp.zeros((ROWS, aw), F32)
        for k in range(CONV_K):
            wk = cw_ref[k:k + 1, :]
            dhb = dhb + wk * extd_ref[pl.ds(CONV_K - 1 - k, ROWS), :]
            dcw_ref[k:k + 1, :] += _colsum(dcur * exth_ref[pl.ds(HALO - (CONV_K - 1) + k, ROWS), :])
        da = dhb * sg
        dg = dhb * a * sg * (1.0 - sg)
        dzb_ref[:, 0:aw] = da.astype(BF16)
        dzb_ref[:, aw:2 * aw] = dg.astype(BF16)
        dbb_ref[:, 0:aw] += _colsum(da)
        dbb_ref[:, aw:2 * aw] += _colsum(dg)

    return pl.pallas_call(
        body, name=name, grid=(nblk,),
        in_specs=[pl.BlockSpec((ROWS, aw), lambda i: (i, 0)),
                  pl.BlockSpec((HALO, aw), lambda i: (jnp.minimum((i + 1) * hb_per, nblk * hb_per - 1), 0)),
                  pl.BlockSpec((ROWS, 2 * aw), lambda i: (i, 1)),
                  pl.BlockSpec((HALO, 2 * aw), lambda i: (jnp.maximum(i * hb_per - 1, 0), 1)),
                  pl.BlockSpec((CONV_K, aw), lambda i: (0, 0))],
        out_specs=[pl.BlockSpec((ROWS, 2 * aw), lambda i: (i, 0)), pl.BlockSpec((1, 2 * aw), lambda i: (0, 0)),
                   pl.BlockSpec((CONV_K, aw), lambda i: (0, 0))],
        out_shape=[jax.ShapeDtypeStruct((s, 2 * aw), BF16), jax.ShapeDtypeStruct((1, 2 * aw), F32),
                   jax.ShapeDtypeStruct((CONV_K, aw), F32)],
        scratch_shapes=[pltpu.VMEM((HALO + ROWS, aw), F32), pltpu.VMEM((ROWS + HALO, aw), F32)],
        compiler_params=_params(("arbitrary",)))(dhc, dhc, z, z, cw)


def _ffn_tile(f):
    for t in (1408, 1024, 768, 704, 512, 384, 256, 128):
        if f % t == 0 and t % 128 == 0:
            return t
    raise ValueError(f)


def ffn_mid_fwd(up, cw, cb, *, name):
    s, f2 = up.shape
    f = f2 // 2
    tn = _ffn_tile(f)
    nj = f // tn
    per = ROWS // FHALO

    def body(ug_ref, uv_ref, pg_ref, pv_ref, wg_ref, wv_ref, bg_ref, bv_ref, act_ref, eg_ref, ev_ref):
        i = pl.program_id(0)

        def conv(cur_ref, prev_ref, w_ref, b_ref, ext_ref):
            ext_ref[0:FHALO, :] = jnp.where(i > 0, prev_ref[...], 0.0)
            ext_ref[FHALO:FHALO + ROWS, :] = cur_ref[...]
            acc = jnp.zeros((ROWS, tn), F32) + b_ref[...]
            for k in range(FFN_K):
                acc = acc + w_ref[k:k + 1, :] * ext_ref[pl.ds(FHALO - (FFN_K - 1) + k, ROWS), :]
            return acc

        gate = conv(ug_ref, pg_ref, wg_ref, bg_ref, eg_ref)
        val = conv(uv_ref, pv_ref, wv_ref, bv_ref, ev_ref)
        act_ref[...] = (gate * _sigmoid(gate) * val).astype(BF16)

    cur = lambda off: pl.BlockSpec((ROWS, tn), lambda i, j: (i, j + off))
    prev = lambda off: pl.BlockSpec((FHALO, tn), lambda i, j: (jnp.maximum(i * per - 1, 0), j + off))
    wsp = lambda off: pl.BlockSpec((FFN_K, tn), lambda i, j: (0, j + off))
    bsp = lambda off: pl.BlockSpec((1, tn), lambda i, j: (0, j + off))
    return pl.pallas_call(
        body, name=name, grid=(s // ROWS, nj),
        in_specs=[cur(0), cur(nj), prev(0), prev(nj), wsp(0), wsp(nj), bsp(0), bsp(nj)],
        out_specs=pl.BlockSpec((ROWS, tn), lambda i, j: (i, j)),
        out_shape=jax.ShapeDtypeStruct((s, f), BF16),
        scratch_shapes=[pltpu.VMEM((FHALO + ROWS, tn), F32), pltpu.VMEM((FHALO + ROWS, tn), F32)],
        compiler_params=_params(("parallel", "parallel")))(up, up, up, up, cw, cw, cb, cb)


def ffn_mid_bwd(dact, up, cw, cb, *, name):
    s, f2 = up.shape
    f = f2 // 2
    tn = _ffn_tile(f)
    nj = f // tn
    nblk = s // ROWS
    per = ROWS // FHALO
    ext = ROWS + FHALO

    def body(da_ref, dan_ref, ug_ref, uv_ref, pg_ref, pv_ref, ng_ref, nv_ref, wg_ref, wv_ref, bg_ref, bv_ref,
             dug_ref, duv_ref, dwg_ref, dwv_ref, dbg_ref, dbv_ref, eg_ref, ev_ref, dg_ref, dv_ref):
        i = pl.program_id(1)

        @pl.when(i == 0)
        def _():
            for r in (dwg_ref, dwv_ref, dbg_ref, dbv_ref):
                r[...] = jnp.zeros_like(r)

        def fill(cur_ref, prev_ref, next_ref, ext_ref):
            ext_ref[0:FHALO, :] = jnp.where(i > 0, prev_ref[...], 0.0)
            ext_ref[FHALO:FHALO + ROWS, :] = cur_ref[...]
            ext_ref[FHALO + ROWS:FHALO + ROWS + FHALO, :] = jnp.where(i < nblk - 1, next_ref[...], 0.0)

        def conv(w_ref, b_ref, ext_ref):
            acc = jnp.zeros((ext, tn), F32) + b_ref[...]
            for k in range(FFN_K):
                acc = acc + w_ref[k:k + 1, :] * ext_ref[pl.ds(FHALO - (FFN_K - 1) + k, ext), :]
            return acc

        fill(ug_ref, pg_ref, ng_ref, eg_ref)
        fill(uv_ref, pv_ref, nv_ref, ev_ref)
        gate = conv(wg_ref, bg_ref, eg_ref)
        val = conv(wv_ref, bv_ref, ev_ref)
        dact_ext = jnp.concatenate([da_ref[...], jnp.where(i < nblk - 1, dan_ref[...], 0.0)], axis=0)
        sg = _sigmoid(gate)
        dg_ref[...] = dact_ext * val * (sg * (1.0 + gate * (1.0 - sg)))
        dv_ref[...] = dact_ext * (gate * sg)

        def back(d_ref, w_ref, ext_ref, du_ref, dw_ref, db_ref):
            dcur = d_ref[0:ROWS, :]
            acc = jnp.zeros((ROWS, tn), F32)
            for k in range(FFN_K):
                acc = acc + w_ref[k:k + 1, :] * d_ref[pl.ds(FFN_K - 1 - k, ROWS), :]
                dw_ref[k:k + 1, :] += _colsum(dcur * ext_ref[pl.ds(FHALO - (FFN_K - 1) + k, ROWS), :])
            du_ref[...] = acc.astype(BF16)
            db_ref[...] += _colsum(dcur)

        back(dg_ref, wg_ref, eg_ref, dug_ref, dwg_ref, dbg_ref)
        back(dv_ref, wv_ref, ev_ref, duv_ref, dwv_ref, dbv_ref)

    cur = lambda off: pl.BlockSpec((ROWS, tn), lambda j, i: (i, j + off))
    prev = lambda off: pl.BlockSpec((FHALO, tn), lambda j, i: (jnp.maximum(i * per - 1, 0), j + off))
    nxt = lambda off: pl.BlockSpec((FHALO, tn), lambda j, i: (jnp.minimum((i + 1) * per, nblk * per - 1), j + off))
    wsp = lambda off: pl.BlockSpec((FFN_K, tn), lambda j, i: (0, j + off))
    bsp = lambda off: pl.BlockSpec((1, tn), lambda j, i: (0, j + off))
    outs = pl.pallas_call(
        body, name=name, grid=(nj, nblk),
        in_specs=[cur(0), nxt(0), cur(0), cur(nj), prev(0), prev(nj), nxt(0), nxt(nj),
                  wsp(0), wsp(nj), bsp(0), bsp(nj)],
        out_specs=[cur(0), cur(0), wsp(0), wsp(0), bsp(0), bsp(0)],
        out_shape=[jax.ShapeDtypeStruct((s, f), BF16), jax.ShapeDtypeStruct((s, f), BF16),
                   jax.ShapeDtypeStruct((FFN_K, f), F32), jax.ShapeDtypeStruct((FFN_K, f), F32),
                   jax.ShapeDtypeStruct((1, f), F32), jax.ShapeDtypeStruct((1, f), F32)],
        scratch_shapes=[pltpu.VMEM((ROWS + 2 * FHALO, tn), F32), pltpu.VMEM((ROWS + 2 * FHALO, tn), F32),
                        pltpu.VMEM((ext, tn), F32), pltpu.VMEM((ext, tn), F32)],
        compiler_params=_params(("parallel", "arbitrary")))(dact, dact, up, up, up, up, up, up, cw, cw, cb, cb)
    dug, duv, dwg, dwv, dbg, dbv = outs
    return dug, duv, jnp.concatenate([dwg, dwv], axis=1), jnp.concatenate([dbg, dbv], axis=1)


def rope_tables(s):
    half = HEAD // 2
    lane = jnp.arange(128)
    j = lane % HEAD
    inv = ROPE_THETA ** (-(j % half).astype(F32) / half)
    ang = jnp.arange(s, dtype=F32)[:, None] * inv[None, :]
    sign = jnp.where(j < half, -1.0, 1.0).astype(F32)
    return jnp.cos(ang), jnp.sin(ang) * sign[None, :]


def _swap_halves(x):
    lane = lax.broadcasted_iota(jnp.int32, x.shape, 1)
    return jnp.where((lane % HEAD) < HEAD // 2, pltpu.roll(x, 128 - HEAD // 2, 1), pltpu.roll(x, HEAD // 2, 1))


def rope_fwd(qkv, cos, sin, *, name):
    s, d3 = qkv.shape
    d = d3 // 3
    scale = HEAD ** -0.5

    def body(xq_ref, xk_ref, c_ref, s_ref, q_ref, k_ref):
        c = c_ref[...]
        sn = s_ref[...]
        for t in range(d // 128):
            cs = slice(t * 128, (t + 1) * 128)
            xq = xq_ref[:, cs]
            xk = xk_ref[:, cs]
            q_ref[:, cs] = (xq * c + _swap_halves(xq) * sn) * scale
            k_ref[:, cs] = xk * c + _swap_halves(xk) * sn

    row = pl.BlockSpec((ROWS, d), lambda i: (i, 0))
    tab = pl.BlockSpec((ROWS, 128), lambda i: (i, 0))
    return pl.pallas_call(
        body, name=name, grid=(s // ROWS,),
        in_specs=[row, pl.BlockSpec((ROWS, d), lambda i: (i, 1)), tab, tab],
        out_specs=[row, row],
        out_shape=[jax.ShapeDtypeStruct((s, d), F32)] * 2,
        compiler_params=_params(("parallel",)))(qkv, qkv, cos, sin)


def rope_bwd(dq, dk, dv, cos, sin, *, name):
    s, d = dq.shape
    scale = HEAD ** -0.5

    def body(dq_ref, dk_ref, dv_ref, c_ref, s_ref, o_ref):
        c = c_ref[...]
        sn = s_ref[...]
        for t in range(d // 128):
            cs = slice(t * 128, (t + 1) * 128)
            gq = dq_ref[:, cs] * scale
            gk = dk_ref[:, cs]
            o_ref[:, t * 128:(t + 1) * 128] = (gq * c + _swap_halves(gq * sn)).astype(BF16)
            o_ref[:, d + t * 128:d + (t + 1) * 128] = (gk * c + _swap_halves(gk * sn)).astype(BF16)
        o_ref[:, 2 * d:3 * d] = dv_ref[...].astype(BF16)

    row = pl.BlockSpec((ROWS, d), lambda i: (i, 0))
    tab = pl.BlockSpec((ROWS, 128), lambda i: (i, 0))
    return pl.pallas_call(
        body, name=name, grid=(s // ROWS,),
        in_specs=[row, row, row, tab, tab],
        out_specs=pl.BlockSpec((ROWS, 3 * d), lambda i: (i, 0)),
        out_shape=jax.ShapeDtypeStruct((s, 3 * d), BF16),
        compiler_params=_params(("parallel",)))(dq, dk, dv, cos, sin)


ATT_T = BLOCK * max(DILATIONS)


def _unit_rows(r, j, dil):
    start = r + dil * BLOCK * j
    return pl.ds(start, BLOCK) if dil == 1 else pl.ds(start, BLOCK, stride=dil)


def _units():
    for bi, dil in enumerate(DILATIONS):
        nsub = ATT_T // (BLOCK * dil)
        for r in range(dil):
            for j in range(nsub):
                yield bi, dil, nsub, r, j


def _band(first_block):
    qi = lax.broadcasted_iota(jnp.int32, (BLOCK, 2 * BLOCK), 0)
    kj = lax.broadcasted_iota(jnp.int32, (BLOCK, 2 * BLOCK), 1)
    dist = BLOCK + qi - kj
    band = (dist >= 0) & (dist <= BLOCK)
    return band, band & (jnp.logical_not(first_block) | (kj >= BLOCK))


def _col(tile, h):
    return tile[:, h * HEAD:h * HEAD + 1]


def _keys(cur_ref, prev_ref, r, j, dil, nsub):
    cur = cur_ref[_unit_rows(r, j, dil), :]
    prev = cur_ref[_unit_rows(r, j - 1, dil), :] if j > 0 else prev_ref[_unit_rows(r, nsub - 1, dil), :]
    return jnp.concatenate([prev, cur], axis=0).astype(BF16)


def _att_specs(d, col_off=0):
    nt_cols = d // 128
    cur = pl.BlockSpec((ATT_T, 128), lambda n, p: (n, p + col_off))
    prv = pl.BlockSpec((ATT_T, 128), lambda n, p: (jnp.maximum(n - 1, 0), p + col_off))
    return cur, prv


def attn_fwd(q, k, qkv, *, name):
    s, d = q.shape
    nt = s // ATT_T

    def body(q_ref, kc_ref, kp_ref, vc_ref, vp_ref, o_ref, lse_ref, acc_ref, m_ref, l_ref):
        n = pl.program_id(0)
        band, band0 = _band(n == 0)
        lo = _pair_masks()
        keep = _head_keep(BLOCK)
        nb = len(DILATIONS)
        for bi, dil, nsub, r, j in _units():
            rows = _unit_rows(r, j, dil)
            valid = band if j > 0 else band0
            qp = q_ref[rows, :].astype(BF16)
            kw = _keys(kc_ref, kp_ref, r, j, dil, nsub)
            vw = _keys(vc_ref, vp_ref, r, j, dil, nsub)
            if bi > 0:
                m_old_t = m_ref[rows, :]
            new_m, new_l, alpha, pv = [], [], [], []
            for h in range(2):
                sc = jnp.where(valid, _dot(qp * keep[h], kw, NT), NEG)
                mx = jnp.max(sc, axis=-1, keepdims=True)
                if bi == 0:
                    m_new = mx
                else:
                    m_old = _col(m_old_t, h)
                    m_new = jnp.maximum(m_old, mx)
                    alpha.append(jnp.exp(m_old - m_new))
                pr = jnp.exp(sc - m_new)
                new_m.append(m_new)
                new_l.append(jnp.sum(pr, axis=-1, keepdims=True))
                pv.append(_dot(pr.astype(BF16), vw, NN))
            m_t = jnp.where(lo, new_m[0], new_m[1])
            l_t = jnp.where(lo, new_l[0], new_l[1])
            acc_t = jnp.where(lo, pv[0], pv[1])
            if bi > 0:
                a_t = jnp.where(lo, alpha[0], alpha[1])
                l_t = a_t * l_ref[rows, :] + l_t
                acc_t = a_t * acc_ref[rows, :] + acc_t
            if bi == nb - 1:
                o_ref[rows, :] = acc_t / l_t
                lse_ref[rows, :] = m_t + jnp.log(l_t)
            else:
                acc_ref[rows, :] = acc_t
                m_ref[rows, :] = m_t
                l_ref[rows, :] = l_t

    cur, prv = _att_specs(d)
    vcur, vprv = _att_specs(d, 2 * (d // 128))
    return pl.pallas_call(
        body, name=name, grid=(nt, d // 128), in_specs=[cur, cur, prv, vcur, vprv], out_specs=[cur, cur],
        out_shape=[jax.ShapeDtypeStruct((s, d), F32)] * 2,
        scratch_shapes=[pltpu.VMEM((ATT_T, 128), F32)] * 3,
        compiler_params=_params(("parallel", "parallel")))(q, k, k, qkv, qkv)


def attn_delta(do, o, *, name):
    s, d = do.shape

    def body(do_ref, o_ref, dl_ref):
        lane = lax.broadcasted_iota(jnp.int32, (ROWS, 128), 1)
        lo = lane < HEAD
        for p in range(d // 128):
            cs = slice(p * 128, (p + 1) * 128)
            pr = do_ref[:, cs] * o_ref[:, cs]
            s0 = jnp.sum(jnp.where(lo, pr, 0.0), axis=-1, keepdims=True)
            s1 = jnp.sum(jnp.where(lo, 0.0, pr), axis=-1, keepdims=True)
            dl_ref[:, cs] = jnp.where(lo, s0, s1)

    row = pl.BlockSpec((ROWS, d), lambda i: (i, 0))
    return pl.pallas_call(
        body, name=name, grid=(s // ROWS,), in_specs=[row, row], out_specs=row,
        out_shape=jax.ShapeDtypeStruct((s, d), F32),
        compiler_params=_params(("parallel",)))(do, o)


def attn_dq(q, k, qkv, do, lse, delta, *, name):
    s, d = q.shape
    nt = s // ATT_T

    def body(q_ref, kc_ref, kp_ref, vc_ref, vp_ref, do_ref, l_ref, dl_ref, dq_ref):
        n = pl.program_id(0)
        band, band0 = _band(n == 0)
        lo = _pair_masks()
        keep = _head_keep(BLOCK)
        for bi, dil, nsub, r, j in _units():
            rows = _unit_rows(r, j, dil)
            valid = band if j > 0 else band0
            qp = q_ref[rows, :].astype(BF16)
            dop = do_ref[rows, :].astype(BF16)
            kw = _keys(kc_ref, kp_ref, r, j, dil, nsub)
            vw = _keys(vc_ref, vp_ref, r, j, dil, nsub)
            lt = l_ref[rows, :]
            dt = dl_ref[rows, :]
            res = []
            for h in range(2):
                pr = jnp.where(valid, jnp.exp(_dot(qp * keep[h], kw, NT) - _col(lt, h)), 0.0)
                ds = pr * (_dot(dop * keep[h], vw, NT) - _col(dt, h))
                res.append(_dot(ds.astype(BF16), kw, NN))
            dq_t = jnp.where(lo, res[0], res[1])
            if bi > 0:
                dq_t = dq_t + dq_ref[rows, :]
            dq_ref[rows, :] = dq_t

    cur, prv = _att_specs(d)
    vcur, vprv = _att_specs(d, 2 * (d // 128))
    return pl.pallas_call(
        body, name=name, grid=(nt, d // 128), in_specs=[cur, cur, prv, vcur, vprv, cur, cur, cur], out_specs=cur,
        out_shape=jax.ShapeDtypeStruct((s, d), F32),
        compiler_params=_params(("parallel", "parallel")))(q, k, k, qkv, qkv, do, lse, delta)


def attn_dkv(q, k, qkv, do, lse, delta, *, name):
    s, d = q.shape
    nt = s // ATT_T

    def body(k_ref, v_ref, qc_ref, qn_ref, doc_ref, don_ref, lc_ref, ln_ref, dc_ref, dn_ref, dk_ref, dv_ref):
        n = pl.program_id(0)
        qi = lax.broadcasted_iota(jnp.int32, (2 * BLOCK, BLOCK), 0)
        kj = lax.broadcasted_iota(jnp.int32, (2 * BLOCK, BLOCK), 1)
        own = (qi < BLOCK) & (kj <= qi)
        nxt = (qi >= BLOCK) & (kj >= qi - BLOCK)
        valid_in = own | nxt
        valid_edge = own | (nxt & (n < nt - 1))
        keep = _head_keep(2 * BLOCK)
        lo = _pair_masks()
        for bi, dil, nsub, r, j in _units():
            rows = _unit_rows(r, j, dil)
            inner = j + 1 < nsub
            nrows = _unit_rows(r, j + 1, dil) if inner else _unit_rows(r, 0, dil)
            pick = lambda c_ref, n_ref: jnp.concatenate(
                [c_ref[rows, :], (c_ref if inner else n_ref)[nrows, :]], axis=0)
            valid = valid_in if inner else valid_edge
            kp = k_ref[rows, :].astype(BF16)
            vp = v_ref[rows, :].astype(BF16)
            qab = pick(qc_ref, qn_ref).astype(BF16)
            doab = pick(doc_ref, don_ref).astype(BF16)
            lt = pick(lc_ref, ln_ref)
            dt = pick(dc_ref, dn_ref)
            rk, rv = [], []
            for h in range(2):
                pr = jnp.where(valid, jnp.exp(_dot(qab * keep[h], kp, NT) - _col(lt, h)), 0.0)
                ds = pr * (_dot(doab * keep[h], vp, NT) - _col(dt, h))
                rv.append(_dot(pr.astype(BF16), doab, TN))
                rk.append(_dot(ds.astype(BF16), qab, TN))
            dk_t = jnp.where(lo, rk[0], rk[1])
            dv_t = jnp.where(lo, rv[0], rv[1])
            if bi > 0:
                dk_t = dk_t + dk_ref[rows, :]
                dv_t = dv_t + dv_ref[rows, :]
            dk_ref[rows, :] = dk_t
            dv_ref[rows, :] = dv_t

    cur = pl.BlockSpec((ATT_T, 128), lambda n, p: (n, p))
    nxt_spec = pl.BlockSpec((ATT_T, 128), lambda n, p: (jnp.minimum(n + 1, nt - 1), p))
    vcur = pl.BlockSpec((ATT_T, 128), lambda n, p: (n, p + 2 * (d // 128)))
    return pl.pallas_call(
        body, name=name, grid=(nt, d // 128),
        in_specs=[cur, vcur, cur, nxt_spec, cur, nxt_spec, cur, nxt_spec, cur, nxt_spec], out_specs=[cur, cur],
        out_shape=[jax.ShapeDtypeStruct((s, d), F32)] * 2,
        compiler_params=_params(("parallel", "parallel")))(k, qkv, q, q, do, do, lse, lse, delta, delta)


def adamw(parts_list, w, m, v, *, name):
    nk = len(parts_list)
    npart, rk, c = parts_list[0].shape
    r = rk * nk
    assert w.shape == (r, c)
    tr = rk
    for cand in (512, 256, 128, 64, 32, 16, 8):
        if rk % cand == 0 and cand * c * 4 <= 2 * 1024 * 1024:
            tr = cand
            break
    nbk = rk // tr

    def body(*refs):
        p_refs = refs[:nk]
        w_ref, m_ref, v_ref, g_ref, d_ref, nm_ref, nv_ref = refs[nk:]
        i = pl.program_id(0)
        g = None
        for kk, p_ref in enumerate(p_refs):
            gk = p_ref[0].astype(F32)
            for j in range(1, npart):
                gk = gk + p_ref[j].astype(F32)
            g = gk if g is None else jnp.where(i >= kk * nbk, gk, g)
        m2 = B1 * m_ref[...] + (1.0 - B1) * g
        v2 = B2 * v_ref[...] + (1.0 - B2) * (g * g)
        m_hat = m2 / (1.0 - B1 ** STEP)
        v_hat = v2 / (1.0 - B2 ** STEP)
        g_ref[...] = g
        d_ref[...] = -LR * (m_hat / (jnp.sqrt(v_hat) + ADAM_EPS) + WD * w_ref[...])
        nm_ref[...] = m2
        nv_ref[...] = v2

    blk = pl.BlockSpec((tr, c), lambda i: (i, 0))
    pspec = lambda kk: pl.BlockSpec((npart, tr, c), lambda i: (0, jnp.clip(i - kk * nbk, 0, nbk - 1), 0))
    return pl.pallas_call(
        body, name=name, grid=(r // tr,),
        in_specs=[pspec(kk) for kk in range(nk)] + [blk, blk, blk],
        out_specs=[blk] * 4, out_shape=[jax.ShapeDtypeStruct((r, c), F32)] * 4,
        compiler_params=_params(("parallel",)))(*parts_list, w, m, v)


def _my_index():
    return 4 * lax.axis_index("x") + 2 * lax.axis_index("y") + lax.axis_index("c")


def exchange(arrays, scatter, *, name):
    nt = len(arrays)

    def body(*refs):
        ins = refs[:nt]
        outs = refs[nt:2 * nt]
        send_sems, recv_sems, local_sems = refs[2 * nt:]
        x, y, c = lax.axis_index("x"), lax.axis_index("y"), lax.axis_index("c")
        me = 4 * x + 2 * y + c
        copies = []
        for t in range(nt):
            src = ins[t].at[me] if scatter[t] else ins[t]
            cp = pltpu.make_async_copy(src, outs[t].at[me], local_sems.at[t])
            cp.start()
            copies.append(cp)
        remote = []
        for kk in range(1, N_DEV):
            px, py, pc = x ^ (kk >> 2), y ^ ((kk >> 1) & 1), c ^ (kk & 1)
            peer = 4 * px + 2 * py + pc
            for t in range(nt):
                src = ins[t].at[peer] if scatter[t] else ins[t]
                cp = pltpu.make_async_remote_copy(
                    src_ref=src, dst_ref=outs[t].at[me], send_sem=send_sems.at[t, kk], recv_sem=recv_sems.at[t, kk],
                    device_id=(px, py, pc), device_id_type=pl.DeviceIdType.MESH)
                cp.start()
                remote.append(cp)
        for cp in remote:
            cp.wait()
        for cp in copies:
            cp.wait()

    hbm = pl.BlockSpec(memory_space=pl.ANY)
    out_shape = [jax.ShapeDtypeStruct(a.shape if scatter[t] else (N_DEV,) + a.shape, a.dtype)
                 for t, a in enumerate(arrays)]
    return pl.pallas_call(
        body, name=name, in_specs=[hbm] * nt, out_specs=[hbm] * nt, out_shape=out_shape,
        scratch_shapes=[pltpu.SemaphoreType.DMA((nt, N_DEV)), pltpu.SemaphoreType.DMA((nt, N_DEV)),
                        pltpu.SemaphoreType.DMA((nt,))],
        compiler_params=pltpu.CompilerParams(has_side_effects=True))(*arrays)


def _peer_of(kk):
    x, y, c = lax.axis_index("x"), lax.axis_index("y"), lax.axis_index("c")
    return x ^ (kk >> 2), y ^ ((kk >> 1) & 1), c ^ (kk & 1)


def _peer_copy(t, kk, scatter, ins, lands, send_sems, recv_sems):
    px, py, pc = _peer_of(kk)
    me = _my_index()
    src = ins[t].at[4 * px + 2 * py + pc] if scatter[t] else ins[t]
    return pltpu.make_async_remote_copy(
        src_ref=src, dst_ref=lands[t].at[me], send_sem=send_sems.at[t * N_DEV + kk],
        recv_sem=recv_sems.at[t * N_DEV + kk], device_id=(px, py, pc), device_id_type=pl.DeviceIdType.MESH)


_HBM = pl.BlockSpec(memory_space=pltpu.HBM)
_SEM = pl.BlockSpec(memory_space=pltpu.SEMAPHORE)
_EFFECT = pltpu.SideEffectType.DATAFLOW_SIDE_EFFECTING


def exchange_start(arrays, scatter, *, name):
    nt = len(arrays)
    land_shapes = [a.shape if scatter[t] else (N_DEV,) + a.shape for t, a in enumerate(arrays)]

    def body(*refs):
        ins, lands = refs[:nt], refs[nt:2 * nt]
        send_sems, recv_sems = refs[2 * nt], refs[2 * nt + 1]
        token = refs[-1]
        for kk in range(1, N_DEV):
            for t in range(nt):
                _peer_copy(t, kk, scatter, ins, lands, send_sems, recv_sems).start()
        token[...] = jnp.zeros_like(token)

    sems = pltpu.SemaphoreType.DMA((nt * N_DEV,))
    outs = pl.pallas_call(
        body, name=name,
        out_shape=(sems, sems, *[pltpu.HBM(a.shape, a.dtype) for a in arrays],
                   *[pltpu.HBM(shp, a.dtype) for shp, a in zip(land_shapes, arrays)],
                   jax.ShapeDtypeStruct((8, 128), F32)),
        in_specs=[_HBM] * (2 * nt),
        out_specs=(_SEM, _SEM, *[_HBM] * (2 * nt), pl.BlockSpec(memory_space=pltpu.VMEM)),
        input_output_aliases={i: 2 + i for i in range(2 * nt)},
        compiler_params=pltpu.CompilerParams(has_side_effects=_EFFECT),
    )(*[pltpu.with_memory_space_constraint(a, pltpu.HBM) for a in arrays],
      *[pltpu.with_memory_space_constraint(lax.empty(shp, a.dtype), pltpu.HBM) for shp, a in zip(land_shapes, arrays)])
    return (outs[0], outs[1], outs[2:2 + nt], outs[2 + nt:2 + 2 * nt], scatter), outs[-1]


def exchange_wait(handle, after, *, name):
    send_sems, recv_sems, thru, lands, scatter = handle
    nt = len(thru)

    def body(*refs):
        ins, lnd = refs[:nt], refs[nt:2 * nt]
        s_sems, r_sems = refs[2 * nt], refs[2 * nt + 1]
        for kk in range(1, N_DEV):
            for t in range(nt):
                cp = _peer_copy(t, kk, scatter, ins, lnd, s_sems, r_sems)
                cp.wait_send()
                cp.wait_recv()

    outs = pl.pallas_call(
        body, name=name,
        out_shape=(*[pltpu.HBM(a.shape, a.dtype) for a in thru], *[pltpu.HBM(a.shape, a.dtype) for a in lands]),
        in_specs=[_HBM] * (2 * nt) + [_SEM, _SEM, pl.BlockSpec(memory_space=pl.ANY)],
        out_specs=tuple([_HBM] * (2 * nt)),
        input_output_aliases={i: i for i in range(2 * nt)},
        compiler_params=pltpu.CompilerParams(has_side_effects=_EFFECT),
    )(*thru, *lands, send_sems, recv_sems, after)
    return outs[nt:]


def _with_own(landed, own, me):
    return lax.dynamic_update_index_in_dim(landed, own.astype(landed.dtype), me, 0)


def _cols_from_shards(g):
    g = jnp.moveaxis(g, 0, -2)
    return g.reshape(g.shape[:-2] + (g.shape[-2] * g.shape[-1],))


def _cols_to_shards(w):
    w = w.reshape(w.shape[:-1] + (N_DEV, w.shape[-1] // N_DEV))
    return jnp.moveaxis(w, -2, 0)


def _ffn_fwd(x, g, w_up, cw, cb, w_down, tag):
    h = rms_fwd(x, g, name=f"{tag}_norm")
    up = matmul(h, w_up, name=f"{tag}_up")
    act = ffn_mid_fwd(up, cw, cb, name=f"{tag}_mid")
    out = matmul(act, w_down, res=x, name=f"{tag}_down")
    return out, (h, up, act)


def _ffn_bwd(dx, dxb, x, saved, g, w_up, cw, cb, w_down, tag):
    h, up, act = saved
    dact = matmul(dxb, w_down, tb=True, name=f"{tag}_ddown")
    d_w_down = matmul_ta(act, dxb, name=f"{tag}_gdown")
    dug, duv, dcw, dcb = ffn_mid_bwd(dact, up, cw, cb, name=f"{tag}_dmid")
    dup = jnp.concatenate([dug, duv], axis=1)
    d_w_up = matmul_ta(h, dup, name=f"{tag}_gup")
    dh = matmul(dup, w_up, tb=True, name=f"{tag}_dup")
    dx2, dxb2, dg = rms_bwd(dh, x, g, dx, name=f"{tag}_dnorm")
    return dx2, dxb2, dict(norm_g=dg, w_up=d_w_up, conv_w=dcw, conv_b=dcb, w_down=d_w_down)


def local_step(x0, tgt, a, weights, grads_out):
    s, d = x0.shape
    aw = a['even_v_ln_g'].shape[-1]
    causal = jnp.tril(jnp.ones((CHUNK, CHUNK), dtype=bool))
    wm = jnp.where(causal, a['even_w_s'][0], 0.0).astype(BF16)
    wmt = jnp.swapaxes(wm, 1, 2)
    bm = jnp.repeat(a['even_b_s'][0].T, HEAD, axis=1)
    sel = (jnp.arange(aw)[:, None] // HEAD == jnp.arange(128)[None, :]).astype(BF16)
    cos, sin = rope_tables(s)
    ffn_g, ffn_cb = a['ffn_norm_g'], a['ffn_conv_b']

    w0 = weights(0, None)
    w_in, conv_w, odd_g, ffn_cw = w0['w_in'], w0['conv_w'], w0['odd_g'], w0['ffn_cw']
    h0 = rms_fwd(x0, w0['even_g'], name="even_norm")
    z = matmul(h0, w_in, bias=a['even_b_in'], name="even_in")
    ycat, hc = even_mid_fwd(z, a['even_v_ln_g'], a['even_v_ln_b'], wm, bm, conv_w, a['even_conv_b'],
                            a['even_conv_ln_g'], a['even_conv_ln_b'], name="even_mid")
    w1 = weights(1, ycat)
    x1 = matmul(ycat, w1['w_out'], res=x0, name="even_out")
    x2, ffn0 = _ffn_fwd(x1, ffn_g[0:1], w1['w_up'], ffn_cw[0], ffn_cb[0:1], w1['w_down'], "ffn0")
    h2 = rms_fwd(x2, odd_g, name="odd_norm")
    w2 = weights(2, h2)
    qkv = matmul(h2, w2['w_qkv'], name="odd_qkv")
    q, k = rope_fwd(qkv, cos, sin, name="rope")
    o, lse = attn_fwd(q, k, qkv, name="attn_fwd")
    x3 = matmul(o, w2['w_o'], res=x2, name="odd_out")
    x4, ffn1 = _ffn_fwd(x3, ffn_g[1:2], w2['w_up'], ffn_cw[1], ffn_cb[1:2], w2['w_down'], "ffn1")
    loss_t, dx, dxb, d_final_g = final_loss_bwd(x4, a['final_norm_g'].reshape(1, -1), tgt, name="final_loss")

    dx, dxb, g1 = _ffn_bwd(dx, dxb, x3, ffn1, ffn_g[1:2], w2['w_up'], ffn_cw[1], ffn_cb[1:2], w2['w_down'], "ffn1")
    dep = grads_out(0, dict(w_up=g1['w_up'], w_down=g1['w_down']))
    do = matmul(dxb, w2['w_o'], tb=True, dep=dep, name="odd_dout")
    d_w_o = matmul_ta(o, dxb, name="odd_gout")
    delta = attn_delta(do, o, name="attn_delta")
    dq = attn_dq(q, k, qkv, do, lse, delta, name="attn_dq")
    dk, dv = attn_dkv(q, k, qkv, do, lse, delta, name="attn_dkv")
    dqkv = rope_bwd(dq, dk, dv, cos, sin, name="rope_bwd")
    d_w_qkv = matmul_ta(h2, dqkv, name="odd_gqkv")
    dep = grads_out(1, dict(w_qkv=d_w_qkv, w_o=d_w_o))
    dh2 = matmul(dqkv, w2['w_qkv'], tb=True, dep=dep, name="odd_dqkv")
    dx, dxb, d_odd_g = rms_bwd(dh2, x2, odd_g, dx, name="odd_dnorm")
    dx, dxb, g0 = _ffn_bwd(dx, dxb, x1, ffn0, ffn_g[0:1], w1['w_up'], ffn_cw[0], ffn_cb[0:1], w1['w_down'], "ffn0")
    dep = grads_out(2, dict(w_up=g0['w_up'], w_down=g0['w_down']))
    dycat = matmul(dxb, w1['w_out'], tb=True, dep=dep, name="even_dout")
    d_w_out = matmul_ta(ycat, dxb, name="even_gout")
    (dza, dhc, dba, dvg, dvb, dwm, dbs, dcg, dcbeta, dcb) = even_mid_bwd_rows(
        dycat, z, hc, a['even_v_ln_g'], a['even_v_ln_b'], wm, wmt, bm, sel, a['even_conv_ln_g'],
        a['even_conv_ln_b'], name="even_dmid_rows")
    dzb, dbb, dcw = even_conv_bwd(dhc, z, conv_w, name="even_dmid_conv")
    dz = jnp.concatenate([dza, dzb], axis=1)
    d_w_in = matmul_ta(h0, dz, name="even_gin")
    dh0 = matmul(dz, w_in, tb=True, name="even_din")
    grad_x, _, d_even_g = rms_bwd(dh0, x0, w0['even_g'], dx, name="even_dnorm")

    nh = a['even_w_s'].shape[1]
    small_grads = {
        'even_norm_g': d_even_g, 'even_b_in': jnp.concatenate([dba, dbb], axis=1), 'even_v_ln_g': dvg,
        'even_v_ln_b': dvb, 'even_w_s': jnp.where(causal, dwm, 0.0)[None], 'even_b_s': dbs[:, :nh].T[None],
        'even_conv_w': dcw[None], 'even_conv_b': dcb, 'even_conv_ln_g': dcg, 'even_conv_ln_b': dcbeta,
        'odd_norm_g': d_odd_g, 'ffn_norm_g': jnp.concatenate([g0['norm_g'], g1['norm_g']], axis=0),
        'ffn_conv_w': jnp.stack([g0['conv_w'], g1['conv_w']]),
        'ffn_conv_b': jnp.concatenate([g0['conv_b'], g1['conv_b']], axis=0),
        'final_norm_g': d_final_g.reshape(-1),
    }
    grads_out(3, dict(w_in=d_w_in, w_out=d_w_out, small=small_grads))
    return loss_t, grad_x, small_grads


BIG = ['even_w_in', 'even_w_out', 'odd_w_qkv', 'odd_w_o', 'ffn_w_up', 'ffn_w_down']


def _as_tiles(flat):
    return jnp.pad(flat, (0, (-flat.size) % 1024)).reshape(-1, 128)


def kernel(*args):
    a = dict(zip(NAMES + ['loss_target'] + ['m_' + n for n in WEIGHTS] + ['v_' + n for n in WEIGHTS], args))
    x0 = a['x'][0]
    tgt = a['loss_target'][0]
    s, d = x0.shape
    me = _my_index()
    bf = lambda t: t.astype(BF16)

    small_local = _as_tiles(jnp.concatenate([a['even_conv_w'].reshape(-1), a['odd_norm_g'].reshape(-1),
                                             a['ffn_conv_w'].reshape(-1)]))
    stage_arrays = [
        [bf(a['even_w_in']), small_local],
        [bf(a['even_w_out']), bf(a['ffn_w_up'][0:1]), bf(a['ffn_w_down'][0:1])],
        [bf(a['odd_w_qkv']), bf(a['odd_w_o']), bf(a['ffn_w_up'][1:2]), bf(a['ffn_w_down'][1:2])],
    ]
    started = [exchange_start(arrs, [False] * len(arrs), name=f"gather{i}_start") for i, arrs in enumerate(stage_arrays)]
    order = sum(tok[0, 0] for _, tok in started)

    def weights(stage, after):
        handle, tok = started[stage]
        landed = exchange_wait(handle, tok if after is None else after, name=f"gather{stage}_wait")
        full = [_with_own(l, own, me) for l, own in zip(landed, stage_arrays[stage])]
        rows = lambda g: jnp.moveaxis(g, 0, 1).reshape(-1, d)
        if stage == 0:
            gs = full[1].reshape(N_DEV, -1)
            n_cw, n_og, n_fw = a['even_conv_w'].size, a['odd_norm_g'].size, a['ffn_conv_w'].size
            return dict(
                w_in=_cols_from_shards(full[0])[0], even_g=a['even_norm_g'] + order,
                conv_w=_cols_from_shards(gs[:, :n_cw].reshape((N_DEV,) + a['even_conv_w'].shape))[0],
                odd_g=gs[:, n_cw:n_cw + n_og].reshape(1, -1),
                ffn_cw=_cols_from_shards(gs[:, n_cw + n_og:n_cw + n_og + n_fw].reshape((N_DEV,) + a['ffn_conv_w'].shape)))
        if stage == 1:
            return dict(w_out=rows(full[0]), w_up=_cols_from_shards(full[1])[0], w_down=rows(full[2]))
        return dict(w_qkv=_cols_from_shards(full[0])[0], w_o=rows(full[1]), w_up=_cols_from_shards(full[2])[0],
                    w_down=rows(full[3]))

    sent = {}

    def grads_out(stage, g):
        to_rows = lambda w: w.reshape(N_DEV, 1, -1, d)
        if stage in (0, 2):
            pieces = [_cols_to_shards(g['w_up'][None]), to_rows(g['w_down'])]
        elif stage == 1:
            pieces = [_cols_to_shards(g['w_qkv'][None]), to_rows(g['w_o'])]
        else:
            names = list(g['small'])
            pieces = [_cols_to_shards(g['w_in'][None]), to_rows(g['w_out']),
                      _as_tiles(jnp.concatenate([g['small'][n].reshape(-1) for n in names]))]
        scatter = [True, True] + [False] * (len(pieces) - 2)
        handle, tok = exchange_start(pieces, scatter, name=f"grads{stage}_start")
        sent[stage] = (handle, pieces, scatter)
        return tok

    loss_t, grad_x, small_grads = local_step(x0, tgt, a, weights, grads_out)
    loss = lax.psum(loss_t[0, 0], ("x", "y", "c"))

    received = {}
    for stage, (handle, pieces, scatter) in sent.items():
        landed = exchange_wait(handle, grad_x, name=f"grads{stage}_wait")
        received[stage] = [_with_own(l, p[me] if sc else p, me) for l, p, sc in zip(landed, pieces, scatter)]

    results = {}
    big_parts = {'even_w_in': [received[3][0]], 'even_w_out': [received[3][1]], 'odd_w_qkv': [received[1][0]],
                 'odd_w_o': [received[1][1]], 'ffn_w_up': [received[2][0], received[0][0]],
                 'ffn_w_down': [received[2][1], received[0][1]]}
    for n in BIG:
        shp = a[n].shape
        flat = lambda t: t.reshape(-1, shp[-1])
        outs = adamw([p.reshape(N_DEV, -1, shp[-1]) for p in big_parts[n]], flat(a[n]), flat(a['m_' + n]),
                     flat(a['v_' + n]), name=f"adamw_{n}")
        results[n] = [t.reshape(shp) for t in outs]

    small_names = list(small_grads)
    n_small = sum(small_grads[n].size for n in small_names)
    rs = received[3][2].reshape(N_DEV, -1)[:, :n_small]
    parts, offs = [], 0
    for n in small_names:
        full = small_grads[n].shape
        piece = rs[:, offs:offs + small_grads[n].size].reshape((N_DEV,) + full)
        offs += small_grads[n].size
        shp = a[n].shape
        if shp != full:
            width = shp[-1]
            piece = lax.dynamic_slice_in_dim(piece, me * width, width, axis=piece.ndim - 1)
        parts.append(piece.reshape(N_DEV, -1))
    parts = jnp.concatenate(parts, axis=1)
    pad = (-parts.shape[1]) % 1024
    cat = lambda pre: _as_tiles(jnp.concatenate([a[pre + n].reshape(-1) for n in small_names]))
    outs = adamw([jnp.pad(parts, ((0, 0), (0, pad))).reshape(N_DEV, -1, 128)], cat(''), cat('m_'), cat('v_'),
                 name="adamw_small")
    offs = 0
    for n in small_names:
        size = a[n].size
        results[n] = [t.reshape(-1)[offs:offs + size].reshape(a[n].shape) for t in outs]
        offs += size

    out = [loss, grad_x[None]]
    for i in range(4):
        out += [results[n][i] for n in WEIGHTS]
    return tuple(out)
```

```python
import functools
import math

import jax
import jax.numpy as jnp
from jax import lax
from jax.experimental import pallas as pl
from jax.experimental.pallas import tpu as pltpu

F32 = jnp.float32
BF16 = jnp.bfloat16

N_DEV = 8
EPS = 1e-6
NEG = -1e30
HEAD = 64
CHUNK = 128
BLOCK = 128
CONV_K = 31
FFN_K = 3
DILATIONS = (1, 4, 16)
ROPE_THETA = 10000.0
LR, B1, B2, ADAM_EPS, WD, STEP = 0.001, 0.9, 0.999, 1e-08, 0.01, 10

VMEM_LIMIT = 56 * 1024 * 1024
VMEM_BUDGET = 32 * 1024 * 1024
ROWS = 512
HALO = 32
FHALO = 8

NAMES = ['x', 'even_norm_g', 'even_w_in', 'even_b_in', 'even_v_ln_g', 'even_v_ln_b', 'even_w_s', 'even_b_s',
         'even_conv_w', 'even_conv_b', 'even_conv_ln_g', 'even_conv_ln_b', 'even_w_out', 'odd_norm_g',
         'odd_w_qkv', 'odd_w_o', 'ffn_norm_g', 'ffn_w_up', 'ffn_conv_w', 'ffn_conv_b', 'ffn_w_down',
         'final_norm_g']
WEIGHTS = NAMES[1:]


def _params(sem=None):
    return pltpu.CompilerParams(dimension_semantics=sem, vmem_limit_bytes=VMEM_LIMIT)


def _sigmoid(x):
    return 1.0 / (1.0 + jnp.exp(-x))


def _gelu(x):
    c = math.sqrt(2.0 / math.pi)
    return 0.5 * x * (1.0 + jnp.tanh(c * (x + 0.044715 * x * x * x)))


def _gelu_grad(x):
    c = math.sqrt(2.0 / math.pi)
    t = jnp.tanh(c * (x + 0.044715 * x * x * x))
    return 0.5 * (1.0 + t) + 0.5 * x * (1.0 - t * t) * c * (1.0 + 3.0 * 0.044715 * x * x)


def _ln_stats(x):
    mu = jnp.mean(x, axis=-1, keepdims=True)
    xc = x - mu
    rstd = lax.rsqrt(jnp.mean(xc * xc, axis=-1, keepdims=True) + EPS)
    return xc * rstd, rstd


def _ln_bwd(dy, xhat, rstd, g):
    dxh = dy * g
    return rstd * (dxh - jnp.mean(dxh, axis=-1, keepdims=True) - xhat * jnp.mean(dxh * xhat, axis=-1, keepdims=True))


def _colsum(x):
    return jnp.sum(x, axis=0, keepdims=True)


def _split3(x):
    hi = x.astype(BF16)
    r = x - hi.astype(F32)
    mid = r.astype(BF16)
    lo = (r - mid.astype(F32)).astype(BF16)
    return hi, mid, lo


def _dot(a, b, dims):
    return lax.dot_general(a, b, (dims, ((), ())), preferred_element_type=F32)


NN = ((1,), (0,))
NT = ((1,), (1,))
TN = ((0,), (0,))


def _divisors(n, cands):
    return [c for c in cands if c <= n and n % c == 0]


def _pick_tiles(m, n, k, a_bytes, b_bytes, o_bytes, extra_bytes):
    best = None
    for tm in _divisors(m, (1024, 512, 256, 128)):
        for tn in _divisors(n, (1408, 1024, 768, 704, 512, 384, 256, 128)):
            if tn % 128:
                continue
            need = 2 * (tm * k * a_bytes + k * tn * b_bytes + tm * tn * (o_bytes + extra_bytes)) + tm * tn * 4
            if need <= VMEM_BUDGET and (best is None or tm * tn > best[0] * best[1]):
                best = (tm, tn)
    assert best is not None, (m, n, k)
    return best


def matmul(a, b, *, tb=False, bias=None, res=None, dep=None, out_dtype=F32, name):
    m, k = a.shape
    n = b.shape[0] if tb else b.shape[1]
    assert (b.shape[1] if tb else b.shape[0]) == k
    tm, tn = _pick_tiles(m, n, k, a.dtype.itemsize, b.dtype.itemsize, jnp.dtype(out_dtype).itemsize,
                         4 if res is not None else 0)

    def body(*refs):
        a_ref, b_ref = refs[:2]
        o_ref = refs[-1]
        acc = _dot(a_ref[...].astype(BF16), b_ref[...].astype(BF16), NT if tb else NN)
        pos = 2
        if bias is not None:
            acc = acc + refs[pos][...]
            pos += 1
        if res is not None:
            acc = acc + refs[pos][...]
        o_ref[...] = acc.astype(out_dtype)

    in_specs = [pl.BlockSpec((tm, k), lambda i, j: (i, 0)),
                pl.BlockSpec((tn, k), lambda i, j: (j, 0)) if tb else pl.BlockSpec((k, tn), lambda i, j: (0, j))]
    args = [a, b]
    if bias is not None:
        in_specs.append(pl.BlockSpec((1, tn), lambda i, j: (0, j)))
        args.append(bias)
    if res is not None:
        in_specs.append(pl.BlockSpec((tm, tn), lambda i, j: (i, j)))
        args.append(res)
    if dep is not None:
        in_specs.append(pl.BlockSpec(memory_space=pl.ANY))
        args.append(dep)
    return pl.pallas_call(
        body, name=name, grid=(m // tm, n // tn), in_specs=in_specs,
        out_specs=pl.BlockSpec((tm, tn), lambda i, j: (i, j)),
        out_shape=jax.ShapeDtypeStruct((m, n), out_dtype),
        compiler_params=_params(("parallel", "parallel")))(*args)


def matmul_ta(a, b, *, out_dtype=BF16, name):
    s, m = a.shape
    n = b.shape[1]
    assert b.shape[0] == s
    best = None
    for tm in _divisors(m, (512, 256, 128)):
        for tn in _divisors(n, (1024, 512, 384, 256, 128)):
            need = 2 * (s * tm * a.dtype.itemsize + s * tn * b.dtype.itemsize + tm * tn * 2) + tm * tn * 4
            if need <= VMEM_BUDGET and (best is None or tm * tn > best[0] * best[1]):
                best = (tm, tn)
    tm, tn = best

    def body(a_ref, b_ref, o_ref):
        o_ref[...] = _dot(a_ref[...].astype(BF16), b_ref[...].astype(BF16), TN).astype(out_dtype)

    return pl.pallas_call(
        body, name=name, grid=(m // tm, n // tn),
        in_specs=[pl.BlockSpec((s, tm), lambda i, j: (0, i)), pl.BlockSpec((s, tn), lambda i, j: (0, j))],
        out_specs=pl.BlockSpec((tm, tn), lambda i, j: (i, j)),
        out_shape=jax.ShapeDtypeStruct((m, n), out_dtype),
        compiler_params=_params(("parallel", "parallel")))(a, b)


def rms_fwd(x, g, *, name):
    s, d = x.shape

    def body(x_ref, g_ref, h_ref):
        xv = x_ref[...]
        r = lax.rsqrt(jnp.mean(xv * xv, axis=-1, keepdims=True) + EPS)
        h_ref[...] = (xv * r * g_ref[...]).astype(BF16)

    return pl.pallas_call(
        body, name=name, grid=(s // ROWS,),
        in_specs=[pl.BlockSpec((ROWS, d), lambda i: (i, 0)), pl.BlockSpec((1, d), lambda i: (0, 0))],
        out_specs=pl.BlockSpec((ROWS, d), lambda i: (i, 0)),
        out_shape=jax.ShapeDtypeStruct((s, d), BF16),
        compiler_params=_params(("parallel",)))(x, g)


def rms_bwd(dh, x, g, dres, *, name):
    s, d = x.shape

    def body(dh_ref, x_ref, g_ref, dres_ref, dx_ref, dxb_ref, dg_ref):
        xv = x_ref[...]
        r = lax.rsqrt(jnp.mean(xv * xv, axis=-1, keepdims=True) + EPS)
        xhat = xv * r
        dhv = dh_ref[...]
        dxh = dhv * g_ref[...]
        dx = dres_ref[...] + r * (dxh - xhat * jnp.mean(dxh * xhat, axis=-1, keepdims=True))
        dx_ref[...] = dx
        dxb_ref[...] = dx.astype(BF16)

        @pl.when(pl.program_id(0) == 0)
        def _():
            dg_ref[...] = jnp.zeros_like(dg_ref)
        dg_ref[...] += _colsum(dhv * xhat)

    row = pl.BlockSpec((ROWS, d), lambda i: (i, 0))
    vec = pl.BlockSpec((1, d), lambda i: (0, 0))
    return pl.pallas_call(
        body, name=name, grid=(s // ROWS,),
        in_specs=[row, row, vec, row], out_specs=[row, row, vec],
        out_shape=[jax.ShapeDtypeStruct((s, d), F32), jax.ShapeDtypeStruct((s, d), BF16),
                   jax.ShapeDtypeStruct((1, d), F32)],
        compiler_params=_params(("arbitrary",)))(dh, x, g, dres)


def final_loss_bwd(x, g, tgt, *, name):
    s, d = x.shape

    def body(x_ref, g_ref, t_ref, loss_ref, dx_ref, dxb_ref, dg_ref):
        xv = x_ref[...]
        gv = g_ref[...]
        r = lax.rsqrt(jnp.mean(xv * xv, axis=-1, keepdims=True) + EPS)
        xhat = xv * r
        e = xhat * gv - t_ref[...]
        dy = e * (1.0 / d)
        dxh = dy * gv
        dx = r * (dxh - xhat * jnp.mean(dxh * xhat, axis=-1, keepdims=True))
        dx_ref[...] = dx
        dxb_ref[...] = dx.astype(BF16)

        @pl.when(pl.program_id(0) == 0)
        def _():
            dg_ref[...] = jnp.zeros_like(dg_ref)
            loss_ref[...] = jnp.zeros_like(loss_ref)
        dg_ref[...] += _colsum(dy * xhat)
        loss_ref[...] += 0.5 * jnp.sum(jnp.mean(e * e, axis=-1, keepdims=True))

    row = pl.BlockSpec((ROWS, d), lambda i: (i, 0))
    vec = pl.BlockSpec((1, d), lambda i: (0, 0))
    one = pl.BlockSpec((8, 128), lambda i: (0, 0))
    return pl.pallas_call(
        body, name=name, grid=(s // ROWS,),
        in_specs=[row, vec, row], out_specs=[one, row, row, vec],
        out_shape=[jax.ShapeDtypeStruct((8, 128), F32), jax.ShapeDtypeStruct((s, d), F32),
                   jax.ShapeDtypeStruct((s, d), BF16), jax.ShapeDtypeStruct((1, d), F32)],
        compiler_params=_params(("arbitrary",)))(x, g, tgt)


def _pair_masks():
    lane = lax.broadcasted_iota(jnp.int32, (CHUNK, 128), 1)
    return lane < HEAD


def _head_keep(rows):
    lane = lax.broadcasted_iota(jnp.int32, (rows, 128), 1)
    first = jnp.where(lane < HEAD, 1.0, 0.0)
    return first.astype(BF16), (1.0 - first).astype(BF16)


def _gating_mixed(vn_b, wm_ref, lo):
    rows, aw = vn_b.shape
    out = []
    for c in range(rows // CHUNK):
        tiles = []
        for p in range(aw // 128):
            vp = vn_b[c * CHUNK:(c + 1) * CHUNK, p * 128:(p + 1) * 128]
            r0 = _dot(wm_ref[2 * p], vp, NN)
            r1 = _dot(wm_ref[2 * p + 1], vp, NN)
            tiles.append(jnp.where(lo, r0, r1))
        out.append(jnp.concatenate(tiles, axis=1))
    return jnp.concatenate(out, axis=0)


def even_mid_fwd(z, vg, vb, wm, bm, cw, cb, cg, cbeta, *, name):
    s, zw = z.shape
    aw = zw // 4
    nblk = s // ROWS

    def body(z_ref, zp_ref, vg_ref, vb_ref, wm_ref, bm_ref, cw_ref, cb_ref, cg_ref, cbeta_ref,
             y_ref, hc_ref, ext_ref):
        i = pl.program_id(0)
        lo = _pair_masks()
        u = _gelu(z_ref[:, 0:aw])
        v = _gelu(z_ref[:, aw:2 * aw])
        vhat, _ = _ln_stats(v)
        vn = (vhat * vg_ref[...] + vb_ref[...]).astype(BF16)
        mixed = _gating_mixed(vn, wm_ref, lo)
        bias = jnp.concatenate([bm_ref[...]] * (ROWS // CHUNK), axis=0)
        y_ref[:, 0:aw] = (u * (mixed + bias)).astype(BF16)

        hb = z_ref[:, 2 * aw:3 * aw] * _sigmoid(z_ref[:, 3 * aw:4 * aw])
        hbp = zp_ref[:, 0:aw] * _sigmoid(zp_ref[:, aw:2 * aw])
        ext_ref[0:HALO, :] = jnp.where(i > 0, hbp, 0.0)
        ext_ref[HALO:HALO + ROWS, :] = hb
        acc = jnp.zeros((ROWS, aw), F32) + cb_ref[...]
        for k in range(CONV_K):
            acc = acc + cw_ref[k:k + 1, :] * ext_ref[pl.ds(HALO - (CONV_K - 1) + k, ROWS), :]
        hc_ref[...] = acc
        hhat, _ = _ln_stats(acc)
        hn = hhat * cg_ref[...] + cbeta_ref[...]
        y_ref[:, aw:2 * aw] = (hn * _sigmoid(hn)).astype(BF16)

    hb_per = ROWS // HALO
    vec = pl.BlockSpec((1, aw), lambda i: (0, 0))
    return pl.pallas_call(
        body, name=name, grid=(nblk,),
        in_specs=[pl.BlockSpec((ROWS, zw), lambda i: (i, 0)),
                  pl.BlockSpec((HALO, 2 * aw), lambda i: (jnp.maximum(i * hb_per - 1, 0), 1)),
                  vec, vec,
                  pl.BlockSpec(wm.shape, lambda i: (0, 0, 0)),
                  pl.BlockSpec((CHUNK, aw), lambda i: (0, 0)),
                  pl.BlockSpec((CONV_K, aw), lambda i: (0, 0)), vec, vec, vec],
        out_specs=[pl.BlockSpec((ROWS, 2 * aw), lambda i: (i, 0)), pl.BlockSpec((ROWS, aw), lambda i: (i, 0))],
        out_shape=[jax.ShapeDtypeStruct((s, 2 * aw), BF16), jax.ShapeDtypeStruct((s, aw), F32)],
        scratch_shapes=[pltpu.VMEM((HALO + ROWS, aw), F32)],
        compiler_params=_params(("parallel",)))(z, z, vg, vb, wm, bm, cw, cb, cg, cbeta)


def even_mid_bwd_rows(dy, z, hc, vg, vb, wm, wmt, bm, sel, cg, cbeta, *, name):
    s, zw = z.shape
    aw = zw // 4
    nh = wm.shape[0]

    def body(dy_ref, z_ref, hc_ref, vg_ref, vb_ref, wm_ref, wmt_ref, bm_ref, sel_ref, cg_ref, cbeta_ref,
             dza_ref, dhc_ref, dba_ref, dvg_ref, dvb_ref, dwm_ref, dbs_ref, dcg_ref, dcbeta_ref, dcb_ref):
        @pl.when(pl.program_id(0) == 0)
        def _():
            for r in (dba_ref, dvg_ref, dvb_ref, dwm_ref, dbs_ref, dcg_ref, dcbeta_ref, dcb_ref):
                r[...] = jnp.zeros_like(r)

        lo = _pair_masks()
        keep = _head_keep(CHUNK)
        zu = z_ref[:, 0:aw]
        zv = z_ref[:, aw:2 * aw]
        u = _gelu(zu)
        v = _gelu(zv)
        vhat, vrstd = _ln_stats(v)
        vn = (vhat * vg_ref[...] + vb_ref[...]).astype(BF16)
        mixed = _gating_mixed(vn, wm_ref, lo)
        bias = jnp.concatenate([bm_ref[...]] * (ROWS // CHUNK), axis=0)
        dya = dy_ref[:, 0:aw]
        du = dya * (mixed + bias)
        dmix = dya * u
        dmix_b = dmix.astype(BF16)

        dvn_rows = []
        for c in range(ROWS // CHUNK):
            rs = slice(c * CHUNK, (c + 1) * CHUNK)
            tiles = []
            for p in range(aw // 128):
                cs = slice(p * 128, (p + 1) * 128)
                dm = dmix_b[rs, cs]
                dm0 = dm * keep[0]
                dm1 = dm * keep[1]
                vp = vn[rs, cs]
                tiles.append(_dot(wmt_ref[2 * p], dm0, NN) + _dot(wmt_ref[2 * p + 1], dm1, NN))
                dwm_ref[2 * p] += _dot(dm0, vp, NT)
                dwm_ref[2 * p + 1] += _dot(dm1, vp, NT)
            dvn_rows.append(jnp.concatenate(tiles, axis=1))
            acc = jnp.zeros((CHUNK, 128), F32)
            for part in _split3(dmix[rs, :]):
                acc = acc + _dot(part, sel_ref[...], NN)
            dbs_ref[...] += acc
        dvn = jnp.concatenate(dvn_rows, axis=0)
        dvg_ref[...] += _colsum(dvn * vhat)
        dvb_ref[...] += _colsum(dvn)
        dv = _ln_bwd(dvn, vhat, vrstd, vg_ref[...])
        dzu = du * _gelu_grad(zu)
        dzv = dv * _gelu_grad(zv)
        dza_ref[:, 0:aw] = dzu.astype(BF16)
        dza_ref[:, aw:2 * aw] = dzv.astype(BF16)
        dba_ref[:, 0:aw] += _colsum(dzu)
        dba_ref[:, aw:2 * aw] += _colsum(dzv)

        hcv = hc_ref[...]
        hhat, hrstd = _ln_stats(hcv)
        hn = hhat * cg_ref[...] + cbeta_ref[...]
        sg = _sigmoid(hn)
        dhn = dy_ref[:, aw:2 * aw] * (sg * (1.0 + hn * (1.0 - sg)))
        dcg_ref[...] += _colsum(dhn * hhat)
        dcbeta_ref[...] += _colsum(dhn)
        dhc = _ln_bwd(dhn, hhat, hrstd, cg_ref[...])
        dhc_ref[...] = dhc
        dcb_ref[...] += _colsum(dhc)

    vec = pl.BlockSpec((1, aw), lambda i: (0, 0))
    vec2 = pl.BlockSpec((1, 2 * aw), lambda i: (0, 0))
    w3 = pl.BlockSpec(wm.shape, lambda i: (0, 0, 0))
    sq = pl.BlockSpec((CHUNK, 128), lambda i: (0, 0))
    return pl.pallas_call(
        body, name=name, grid=(s // ROWS,),
        in_specs=[pl.BlockSpec((ROWS, 2 * aw), lambda i: (i, 0)), pl.BlockSpec((ROWS, 2 * aw), lambda i: (i, 0)),
                  pl.BlockSpec((ROWS, aw), lambda i: (i, 0)), vec, vec, w3, w3,
                  pl.BlockSpec((CHUNK, aw), lambda i: (0, 0)), pl.BlockSpec((aw, 128), lambda i: (0, 0)), vec, vec],
        out_specs=[pl.BlockSpec((ROWS, 2 * aw), lambda i: (i, 0)), pl.BlockSpec((ROWS, aw), lambda i: (i, 0)),
                   vec2, vec, vec, w3, sq, vec, vec, vec],
        out_shape=[jax.ShapeDtypeStruct((s, 2 * aw), BF16), jax.ShapeDtypeStruct((s, aw), F32),
                   jax.ShapeDtypeStruct((1, 2 * aw), F32), jax.ShapeDtypeStruct((1, aw), F32),
                   jax.ShapeDtypeStruct((1, aw), F32), jax.ShapeDtypeStruct(wm.shape, F32),
                   jax.ShapeDtypeStruct((CHUNK, 128), F32), jax.ShapeDtypeStruct((1, aw), F32),
                   jax.ShapeDtypeStruct((1, aw), F32), jax.ShapeDtypeStruct((1, aw), F32)],
        compiler_params=_params(("arbitrary",)))(dy, z, hc, vg, vb, wm, wmt, bm, sel, cg, cbeta)


def even_conv_bwd(dhc, z, cw, *, name):
    s, zw = z.shape
    aw = zw // 4
    nblk = s // ROWS
    hb_per = ROWS // HALO

    def body(dc_ref, dn_ref, z_ref, zp_ref, cw_ref, dzb_ref, dbb_ref, dcw_ref, exth_ref, extd_ref):
        i = pl.program_id(0)

        @pl.when(i == 0)
        def _():
            dbb_ref[...] = jnp.zeros_like(dbb_ref)
            dcw_ref[...] = jnp.zeros_like(dcw_ref)

        a = z_ref[:, 0:aw]
        sg = _sigmoid(z_ref[:, aw:2 * aw])
        exth_ref[0:HALO, :] = jnp.where(i > 0, zp_ref[:, 0:aw] * _sigmoid(zp_ref[:, aw:2 * aw]), 0.0)
        exth_ref[HALO:HALO + ROWS, :] = a * sg
        dcur = dc_ref[...]
        extd_ref[0:ROWS, :] = dcur
        extd_ref[ROWS:ROWS + HALO, :] = jnp.where(i < nblk - 1, dn_ref[...], 0.0)
        dhb = jnp.zeros((ROWS, aw), F32)
        for k in range(CONV_K):
            wk = cw_ref[k:k + 1, :]
            dhb = dhb + wk * extd_ref[pl.ds(CONV_K - 1 - k, ROWS), :]
            dcw_ref[k:k + 1, :] += _colsum(dcur * exth_ref[pl.ds(HALO - (CONV_K - 1) + k, ROWS), :])
        da = dhb * sg
        dg = dhb * a * sg * (1.0 - sg)
        dzb_ref[:, 0:aw] = da.astype(BF16)
        dzb_ref[:, aw:2 * aw] = dg.astype(BF16)
        dbb_ref[:, 0:aw] += _colsum(da)
        dbb_ref[:, aw:2 * aw] += _colsum(dg)

    return pl.pallas_call(
        body, name=name, grid=(nblk,),
        in_specs=[pl.BlockSpec((ROWS, aw), lambda i: (i, 0)),
                  pl.BlockSpec((HALO, aw), lambda i: (jnp.minimum((i + 1) * hb_per, nblk * hb_per - 1), 0)),
                  pl.BlockSpec((ROWS, 2 * aw), lambda i: (i, 1)),
                  pl.BlockSpec((HALO, 2 * aw), lambda i: (jnp.maximum(i * hb_per - 1, 0), 1)),
                  pl.BlockSpec((CONV_K, aw), lambda i: (0, 0))],
        out_specs=[pl.BlockSpec((ROWS, 2 * aw), lambda i: (i, 0)), pl.BlockSpec((1, 2 * aw), lambda i: (0, 0)),
                   pl.BlockSpec((CONV_K, aw), lambda i: (0, 0))],
        out_shape=[jax.ShapeDtypeStruct((s, 2 * aw), BF16), jax.ShapeDtypeStruct((1, 2 * aw), F32),
                   jax.ShapeDtypeStruct((CONV_K, aw), F32)],
        scratch_shapes=[pltpu.VMEM((HALO + ROWS, aw), F32), pltpu.VMEM((ROWS + HALO, aw), F32)],
        compiler_params=_params(("arbitrary",)))(dhc, dhc, z, z, cw)


FFN_ROWS = 1024
FFN_CHUNK = 16


def _ffn_tile(f):
    for t in (256, 128):
        if f % t == 0:
            return t
    raise ValueError(f)


def _taps(ext_ref, w, b, r0, rows):
    acc = b
    for k in range(FFN_K):
        acc = acc + w[k] * ext_ref[pl.ds(FHALO - (FFN_K - 1) + k + r0, rows), :]
    return acc


def ffn_mid_fwd(up, cw, cb, *, name):
    s, f2 = up.shape
    f = f2 // 2
    tn = _ffn_tile(f)
    nj = f // tn
    ROWS = FFN_ROWS
    per = ROWS // FHALO

    def body(ug_ref, uv_ref, pg_ref, pv_ref, wg_ref, wv_ref, bg_ref, bv_ref, act_ref, eg_ref, ev_ref):
        i = pl.program_id(0)
        for cur_ref, prev_ref, ext_ref in ((ug_ref, pg_ref, eg_ref), (uv_ref, pv_ref, ev_ref)):
            ext_ref[0:FHALO, :] = jnp.where(i > 0, prev_ref[...], 0.0)
            ext_ref[FHALO:FHALO + ROWS, :] = cur_ref[...]
        wg = [wg_ref[k:k + 1, :] for k in range(FFN_K)]
        wv = [wv_ref[k:k + 1, :] for k in range(FFN_K)]
        bg, bv = bg_ref[...], bv_ref[...]
        for r0 in range(0, ROWS, FFN_CHUNK):
            gate = _taps(eg_ref, wg, bg, r0, FFN_CHUNK)
            val = _taps(ev_ref, wv, bv, r0, FFN_CHUNK)
            act_ref[pl.ds(r0, FFN_CHUNK), :] = (gate * _sigmoid(gate) * val).astype(BF16)

    cur = lambda off: pl.BlockSpec((ROWS, tn), lambda i, j: (i, j + off))
    prev = lambda off: pl.BlockSpec((FHALO, tn), lambda i, j: (jnp.maximum(i * per - 1, 0), j + off))
    wsp = lambda off: pl.BlockSpec((FFN_K, tn), lambda i, j: (0, j + off))
    bsp = lambda off: pl.BlockSpec((1, tn), lambda i, j: (0, j + off))
    return pl.pallas_call(
        body, name=name, grid=(s // ROWS, nj),
        in_specs=[cur(0), cur(nj), prev(0), prev(nj), wsp(0), wsp(nj), bsp(0), bsp(nj)],
        out_specs=pl.BlockSpec((ROWS, tn), lambda i, j: (i, j)),
        out_shape=jax.ShapeDtypeStruct((s, f), BF16),
        scratch_shapes=[pltpu.VMEM((FHALO + ROWS, tn), F32), pltpu.VMEM((FHALO + ROWS, tn), F32)],
        compiler_params=_params(("parallel", "parallel")))(up, up, up, up, cw, cw, cb, cb)


def ffn_mid_bwd(dact, up, cw, cb, *, name):
    s, f2 = up.shape
    f = f2 // 2
    tn = _ffn_tile(f)
    nj = f // tn
    ROWS = FFN_ROWS
    nblk = s // ROWS
    per = ROWS // FHALO
    ext = ROWS + FHALO

    def body(da_ref, dan_ref, ug_ref, uv_ref, pg_ref, pv_ref, ng_ref, nv_ref, wg_ref, wv_ref, bg_ref, bv_ref,
             dug_ref, duv_ref, dwg_ref, dwv_ref, dbg_ref, dbv_ref, eg_ref, ev_ref, dg_ref, dv_ref):
        i = pl.program_id(1)

        @pl.when(i == 0)
        def _():
            for r in (dwg_ref, dwv_ref, dbg_ref, dbv_ref):
                r[...] = jnp.zeros_like(r)

        for cur_ref, prev_ref, next_ref, ext_ref in ((ug_ref, pg_ref, ng_ref, eg_ref), (uv_ref, pv_ref, nv_ref, ev_ref)):
            ext_ref[0:FHALO, :] = jnp.where(i > 0, prev_ref[...], 0.0)
            ext_ref[FHALO:FHALO + ROWS, :] = cur_ref[...]
            ext_ref[FHALO + ROWS:FHALO + ROWS + FHALO, :] = jnp.where(i < nblk - 1, next_ref[...], 0.0)
        wg = [wg_ref[k:k + 1, :] for k in range(FFN_K)]
        wv = [wv_ref[k:k + 1, :] for k in range(FFN_K)]
        bg, bv = bg_ref[...], bv_ref[...]

        for r0, rows in [(r, FFN_CHUNK) for r in range(0, ROWS, FFN_CHUNK)] + [(ROWS, FHALO)]:
            gate = _taps(eg_ref, wg, bg, r0, rows)
            val = _taps(ev_ref, wv, bv, r0, rows)
            da = da_ref[pl.ds(r0, rows), :] if r0 < ROWS else jnp.where(i < nblk - 1, dan_ref[...], 0.0)
            sg = _sigmoid(gate)
            dg_ref[pl.ds(r0, rows), :] = da * val * (sg * (1.0 + gate * (1.0 - sg)))
            dv_ref[pl.ds(r0, rows), :] = da * (gate * sg)

        def back(d_ref, w, ext_ref, du_ref, dw_ref, db_ref):
            zero = jnp.zeros((FFN_CHUNK, tn), F32)
            acc = [zero] * FFN_K
            accb = zero
            for r0 in range(0, ROWS, FFN_CHUNK):
                d = [d_ref[pl.ds(r0 + FFN_K - 1 - k, FFN_CHUNK), :] for k in range(FFN_K)]
                u = ext_ref[pl.ds(FHALO + r0, FFN_CHUNK), :]
                du = w[0] * d[0]
                for k in range(1, FFN_K):
                    du = du + w[k] * d[k]
                du_ref[pl.ds(r0, FFN_CHUNK), :] = du.astype(BF16)
                acc = [acc[k] + u * d[k] for k in range(FFN_K)]
                accb = accb + d[FFN_K - 1]
            for k in range(FFN_K):
                dw_ref[k:k + 1, :] += _colsum(acc[k])
            db_ref[...] += _colsum(accb)

        back(dg_ref, wg, eg_ref, dug_ref, dwg_ref, dbg_ref)
        back(dv_ref, wv, ev_ref, duv_ref, dwv_ref, dbv_ref)

    cur = lambda off: pl.BlockSpec((ROWS, tn), lambda j, i: (i, j + off))
    prev = lambda off: pl.BlockSpec((FHALO, tn), lambda j, i: (jnp.maximum(i * per - 1, 0), j + off))
    nxt = lambda off: pl.BlockSpec((FHALO, tn), lambda j, i: (jnp.minimum((i + 1) * per, nblk * per - 1), j + off))
    wsp = lambda off: pl.BlockSpec((FFN_K, tn), lambda j, i: (0, j + off))
    bsp = lambda off: pl.BlockSpec((1, tn), lambda j, i: (0, j + off))
    outs = pl.pallas_call(
        body, name=name, grid=(nj, nblk),
        in_specs=[cur(0), nxt(0), cur(0), cur(nj), prev(0), prev(nj), nxt(0), nxt(nj),
                  wsp(0), wsp(nj), bsp(0), bsp(nj)],
        out_specs=[cur(0), cur(0), wsp(0), wsp(0), bsp(0), bsp(0)],
        out_shape=[jax.ShapeDtypeStruct((s, f), BF16), jax.ShapeDtypeStruct((s, f), BF16),
                   jax.ShapeDtypeStruct((FFN_K, f), F32), jax.ShapeDtypeStruct((FFN_K, f), F32),
                   jax.ShapeDtypeStruct((1, f), F32), jax.ShapeDtypeStruct((1, f), F32)],
        scratch_shapes=[pltpu.VMEM((ROWS + 2 * FHALO, tn), F32), pltpu.VMEM((ROWS + 2 * FHALO, tn), F32),
                        pltpu.VMEM((ext, tn), F32), pltpu.VMEM((ext, tn), F32)],
        compiler_params=_params(("parallel", "arbitrary")))(dact, dact, up, up, up, up, up, up, cw, cw, cb, cb)
    dug, duv, dwg, dwv, dbg, dbv = outs
    return dug, duv, jnp.concatenate([dwg, dwv], axis=1), jnp.concatenate([dbg, dbv], axis=1)


def rope_tables(s):
    half = HEAD // 2
    lane = jnp.arange(128)
    j = lane % HEAD
    inv = ROPE_THETA ** (-(j % half).astype(F32) / half)
    ang = jnp.arange(s, dtype=F32)[:, None] * inv[None, :]
    sign = jnp.where(j < half, -1.0, 1.0).astype(F32)
    return jnp.cos(ang), jnp.sin(ang) * sign[None, :]


def _swap_halves(x):
    lane = lax.broadcasted_iota(jnp.int32, x.shape, 1)
    return jnp.where((lane % HEAD) < HEAD // 2, pltpu.roll(x, 128 - HEAD // 2, 1), pltpu.roll(x, HEAD // 2, 1))


def rope_fwd(qkv, cos, sin, *, name):
    s, d3 = qkv.shape
    d = d3 // 3
    scale = HEAD ** -0.5

    def body(xq_ref, xk_ref, c_ref, s_ref, q_ref, k_ref):
        c = c_ref[...]
        sn = s_ref[...]
        for t in range(d // 128):
            cs = slice(t * 128, (t + 1) * 128)
            xq = xq_ref[:, cs]
            xk = xk_ref[:, cs]
            q_ref[:, cs] = (xq * c + _swap_halves(xq) * sn) * scale
            k_ref[:, cs] = xk * c + _swap_halves(xk) * sn

    row = pl.BlockSpec((ROWS, d), lambda i: (i, 0))
    tab = pl.BlockSpec((ROWS, 128), lambda i: (i, 0))
    return pl.pallas_call(
        body, name=name, grid=(s // ROWS,),
        in_specs=[row, pl.BlockSpec((ROWS, d), lambda i: (i, 1)), tab, tab],
        out_specs=[row, row],
        out_shape=[jax.ShapeDtypeStruct((s, d), F32)] * 2,
        compiler_params=_params(("parallel",)))(qkv, qkv, cos, sin)


def rope_bwd(dq, dk, dv, cos, sin, *, name):
    s, d = dq.shape
    scale = HEAD ** -0.5

    def body(dq_ref, dk_ref, dv_ref, c_ref, s_ref, o_ref):
        c = c_ref[...]
        sn = s_ref[...]
        for t in range(d // 128):
            cs = slice(t * 128, (t + 1) * 128)
            gq = dq_ref[:, cs] * scale
            gk = dk_ref[:, cs]
            o_ref[:, t * 128:(t + 1) * 128] = (gq * c + _swap_halves(gq * sn)).astype(BF16)
            o_ref[:, d + t * 128:d + (t + 1) * 128] = (gk * c + _swap_halves(gk * sn)).astype(BF16)
        o_ref[:, 2 * d:3 * d] = dv_ref[...].astype(BF16)

    row = pl.BlockSpec((ROWS, d), lambda i: (i, 0))
    tab = pl.BlockSpec((ROWS, 128), lambda i: (i, 0))
    return pl.pallas_call(
        body, name=name, grid=(s // ROWS,),
        in_specs=[row, row, row, tab, tab],
        out_specs=pl.BlockSpec((ROWS, 3 * d), lambda i: (i, 0)),
        out_shape=jax.ShapeDtypeStruct((s, 3 * d), BF16),
        compiler_params=_params(("parallel",)))(dq, dk, dv, cos, sin)


ATT_T = BLOCK * max(DILATIONS)


FWD_QROWS = 128
BWD_QROWS = 64


def _unit_rows(r, j, dil):
    start = r + dil * BLOCK * j
    return pl.ds(start, BLOCK) if dil == 1 else pl.ds(start, BLOCK, stride=dil)


def _units():
    for bi, dil in enumerate(DILATIONS):
        nsub = ATT_T // (BLOCK * dil)
        for r in range(dil):
            for j in range(nsub):
                yield bi, dil, nsub, r, j


def _band(first_block, part, qrows):
    qi = lax.broadcasted_iota(jnp.int32, (qrows, 2 * BLOCK), 0) + part * qrows
    kj = lax.broadcasted_iota(jnp.int32, (qrows, 2 * BLOCK), 1)
    dist = BLOCK + qi - kj
    band = (dist >= 0) & (dist <= BLOCK)
    return band, band & (jnp.logical_not(first_block) | (kj >= BLOCK))


def _col(tile, h):
    return tile[:, h * HEAD:h * HEAD + 1]


def _keys(cur_ref, prev_ref, r, j, dil, nsub):
    cur = cur_ref[_unit_rows(r, j, dil), :]
    prev = cur_ref[_unit_rows(r, j - 1, dil), :] if j > 0 else prev_ref[_unit_rows(r, nsub - 1, dil), :]
    return jnp.concatenate([prev, cur], axis=0).astype(BF16)


def _att_specs(d, col_off=0):
    nt_cols = d // 128
    cur = pl.BlockSpec((ATT_T, 128), lambda n, p: (n, p + col_off))
    prv = pl.BlockSpec((ATT_T, 128), lambda n, p: (jnp.maximum(n - 1, 0), p + col_off))
    return cur, prv


def attn_fwd(q, k, qkv, *, name):
    s, d = q.shape
    nt = s // ATT_T

    def body(q_ref, kc_ref, kp_ref, vc_ref, vp_ref, o_ref, lse_ref, acc_ref, m_ref, l_ref):
        n = pl.program_id(0)
        QROWS = FWD_QROWS
        nparts = BLOCK // QROWS
        bands = [_band(n == 0, part, QROWS) for part in range(nparts)]
        lo = _pair_masks()
        keep = _head_keep(BLOCK)
        nb = len(DILATIONS)
        for bi, dil, nsub, r, j in _units():
            rows = _unit_rows(r, j, dil)
            kw = _keys(kc_ref, kp_ref, r, j, dil, nsub)
            vw = _keys(vc_ref, vp_ref, r, j, dil, nsub)
            qp = q_ref[rows, :].astype(BF16)
            sc2 = _dot(jnp.concatenate([qp * keep[0], qp * keep[1]], axis=0), kw, NT)
            if bi > 0:
                m_old_t = m_ref[rows, :]
            prs, new_m, new_l, alpha = [], [], [], []
            for c in range(2 * nparts):
                h, part = divmod(c, nparts)
                valid = bands[part][0 if j > 0 else 1]
                sc = jnp.where(valid, sc2[c * QROWS:(c + 1) * QROWS], NEG)
                mx = jnp.max(sc, axis=-1, keepdims=True)
                if bi == 0:
                    m_new = mx
                else:
                    m_old = _col(m_old_t[part * QROWS:(part + 1) * QROWS], h)
                    m_new = jnp.maximum(m_old, mx)
                    alpha.append(jnp.exp(m_old - m_new))
                pr = jnp.exp(sc - m_new)
                new_m.append(m_new)
                new_l.append(jnp.sum(pr, axis=-1, keepdims=True))
                prs.append(pr.astype(BF16))
            pv2 = _dot(jnp.concatenate(prs, axis=0), vw, NN)
            tile = lambda cols: jnp.where(lo, jnp.concatenate(cols[:nparts], axis=0),
                                          jnp.concatenate(cols[nparts:], axis=0))
            m_t = tile(new_m)
            l_t = tile(new_l)
            acc_t = jnp.where(lo, pv2[:BLOCK], pv2[BLOCK:])
            if bi > 0:
                a_t = tile(alpha)
                l_t = a_t * l_ref[rows, :] + l_t
                acc_t = a_t * acc_ref[rows, :] + acc_t
            if bi == nb - 1:
                o_ref[rows, :] = acc_t / l_t
                lse_ref[rows, :] = m_t + jnp.log(l_t)
            else:
                acc_ref[rows, :] = acc_t
                m_ref[rows, :] = m_t
                l_ref[rows, :] = l_t

    cur, prv = _att_specs(d)
    vcur, vprv = _att_specs(d, 2 * (d // 128))
    return pl.pallas_call(
        body, name=name, grid=(nt, d // 128), in_specs=[cur, cur, prv, vcur, vprv], out_specs=[cur, cur],
        out_shape=[jax.ShapeDtypeStruct((s, d), F32)] * 2,
        scratch_shapes=[pltpu.VMEM((ATT_T, 128), F32)] * 3,
        compiler_params=_params(("parallel", "parallel")))(q, k, k, qkv, qkv)


def attn_delta(do, o, *, name):
    s, d = do.shape

    def body(do_ref, o_ref, dl_ref):
        lane = lax.broadcasted_iota(jnp.int32, (ROWS, 128), 1)
        lo = lane < HEAD
        for p in range(d // 128):
            cs = slice(p * 128, (p + 1) * 128)
            pr = do_ref[:, cs] * o_ref[:, cs]
            s0 = jnp.sum(jnp.where(lo, pr, 0.0), axis=-1, keepdims=True)
            s1 = jnp.sum(jnp.where(lo, 0.0, pr), axis=-1, keepdims=True)
            dl_ref[:, cs] = jnp.where(lo, s0, s1)

    row = pl.BlockSpec((ROWS, d), lambda i: (i, 0))
    return pl.pallas_call(
        body, name=name, grid=(s // ROWS,), in_specs=[row, row], out_specs=row,
        out_shape=jax.ShapeDtypeStruct((s, d), F32),
        compiler_params=_params(("parallel",)))(do, o)


def attn_dq(q, k, qkv, do, lse, delta, *, name):
    s, d = q.shape
    nt = s // ATT_T

    def body(q_ref, kc_ref, kp_ref, vc_ref, vp_ref, do_ref, l_ref, dl_ref, dq_ref):
        n = pl.program_id(0)
        QROWS = BWD_QROWS
        nparts = BLOCK // QROWS
        bands = [_band(n == 0, part, QROWS) for part in range(nparts)]
        lo = _pair_masks()
        keep = _head_keep(BLOCK)
        for bi, dil, nsub, r, j in _units():
            rows = _unit_rows(r, j, dil)
            kw = _keys(kc_ref, kp_ref, r, j, dil, nsub)
            vw = _keys(vc_ref, vp_ref, r, j, dil, nsub)
            qp = q_ref[rows, :].astype(BF16)
            dop = do_ref[rows, :].astype(BF16)
            lt = l_ref[rows, :]
            dt = dl_ref[rows, :]
            sc2 = _dot(jnp.concatenate([qp * keep[0], qp * keep[1]], axis=0), kw, NT)
            dp2 = _dot(jnp.concatenate([dop * keep[0], dop * keep[1]], axis=0), vw, NT)
            dss = []
            for c in range(2 * nparts):
                h, part = divmod(c, nparts)
                valid = bands[part][0 if j > 0 else 1]
                cr = slice(c * QROWS, (c + 1) * QROWS)
                pr_rows = slice(part * QROWS, (part + 1) * QROWS)
                pr = jnp.where(valid, jnp.exp(sc2[cr] - _col(lt[pr_rows], h)), 0.0)
                dss.append((pr * (dp2[cr] - _col(dt[pr_rows], h))).astype(BF16))
            dq2 = _dot(jnp.concatenate(dss, axis=0), kw, NN)
            dq_t = jnp.where(lo, dq2[:BLOCK], dq2[BLOCK:])
            if bi > 0:
                dq_t = dq_t + dq_ref[rows, :]
            dq_ref[rows, :] = dq_t

    cur, prv = _att_specs(d)
    vcur, vprv = _att_specs(d, 2 * (d // 128))
    return pl.pallas_call(
        body, name=name, grid=(nt, d // 128), in_specs=[cur, cur, prv, vcur, vprv, cur, cur, cur], out_specs=cur,
        out_shape=jax.ShapeDtypeStruct((s, d), F32),
        compiler_params=_params(("parallel", "parallel")))(q, k, k, qkv, qkv, do, lse, delta)


def attn_dkv(q, k, qkv, do, lse, delta, *, name):
    s, d = q.shape
    nt = s // ATT_T

    def body(k_ref, v_ref, qc_ref, qn_ref, doc_ref, don_ref, lc_ref, ln_ref, dc_ref, dn_ref, dk_ref, dv_ref):
        n = pl.program_id(0)
        qi = lax.broadcasted_iota(jnp.int32, (BLOCK, BLOCK), 0)
        kj = lax.broadcasted_iota(jnp.int32, (BLOCK, BLOCK), 1)
        own = kj <= qi
        nxt = kj >= qi
        nxt_edge = nxt & (n < nt - 1)
        keep = _head_keep(BLOCK)
        lo = _pair_masks()
        for bi, dil, nsub, r, j in _units():
            rows = _unit_rows(r, j, dil)
            inner = j + 1 < nsub
            nrows = _unit_rows(r, j + 1, dil) if inner else _unit_rows(r, 0, dil)
            kp = k_ref[rows, :].astype(BF16)
            vp = v_ref[rows, :].astype(BF16)
            far = not inner
            take = lambda c_ref, n_ref, nx: ((n_ref if far else c_ref)[nrows, :] if nx else c_ref[rows, :])
            qs = [take(qc_ref, qn_ref, nx).astype(BF16) for nx in (False, True)]
            dos = [take(doc_ref, don_ref, nx).astype(BF16) for nx in (False, True)]
            lts = [take(lc_ref, ln_ref, nx) for nx in (False, True)]
            dts = [take(dc_ref, dn_ref, nx) for nx in (False, True)]
            valids = (own, nxt if inner else nxt_edge)
            q4 = jnp.concatenate([qs[nx] * keep[h] for h in range(2) for nx in range(2)], axis=0)
            do4 = jnp.concatenate([dos[nx] * keep[h] for h in range(2) for nx in range(2)], axis=0)
            s4 = _dot(q4, kp, NT)
            dp4 = _dot(do4, vp, NT)
            prs, dss = [], []
            for c in range(4):
                h, nx = divmod(c, 2)
                cr = slice(c * BLOCK, (c + 1) * BLOCK)
                pr = jnp.where(valids[nx], jnp.exp(s4[cr] - _col(lts[nx], h)), 0.0)
                prs.append(pr.astype(BF16))
                dss.append((pr * (dp4[cr] - _col(dts[nx], h))).astype(BF16))
            dv_t = _dot(jnp.concatenate(prs, axis=0), do4, TN)
            dk_t = _dot(jnp.concatenate(dss, axis=0), q4, TN)
            if bi > 0:
                dk_t = dk_t + dk_ref[rows, :]
                dv_t = dv_t + dv_ref[rows, :]
            dk_ref[rows, :] = dk_t
            dv_ref[rows, :] = dv_t

    cur = pl.BlockSpec((ATT_T, 128), lambda n, p: (n, p))
    nxt_spec = pl.BlockSpec((ATT_T, 128), lambda n, p: (jnp.minimum(n + 1, nt - 1), p))
    vcur = pl.BlockSpec((ATT_T, 128), lambda n, p: (n, p + 2 * (d // 128)))
    return pl.pallas_call(
        body, name=name, grid=(nt, d // 128),
        in_specs=[cur, vcur, cur, nxt_spec, cur, nxt_spec, cur, nxt_spec, cur, nxt_spec], out_specs=[cur, cur],
        out_shape=[jax.ShapeDtypeStruct((s, d), F32)] * 2,
        compiler_params=_params(("parallel", "parallel")))(k, qkv, q, q, do, do, lse, lse, delta, delta)


def adamw(parts_list, w, m, v, *, name):
    nk = len(parts_list)
    npart, rk, c = parts_list[0].shape
    r = rk * nk
    assert w.shape == (r, c)
    tr = next(t for t in range(rk, 0, -8) if rk % t == 0 and (t * c * 4 <= 1024 * 1024 or t == 8))
    nbk = rk // tr

    def body(*refs):
        p_refs = refs[:nk]
        w_ref, m_ref, v_ref, g_ref, d_ref, nm_ref, nv_ref = refs[nk:]
        i = pl.program_id(0)
        g = None
        for kk, p_ref in enumerate(p_refs):
            gk = p_ref[0].astype(F32)
            for j in range(1, npart):
                gk = gk + p_ref[j].astype(F32)
            g = gk if g is None else jnp.where(i >= kk * nbk, gk, g)
        m2 = B1 * m_ref[...] + (1.0 - B1) * g
        v2 = B2 * v_ref[...] + (1.0 - B2) * (g * g)
        m_hat = m2 / (1.0 - B1 ** STEP)
        v_hat = v2 / (1.0 - B2 ** STEP)
        g_ref[...] = g
        d_ref[...] = -LR * (m_hat / (jnp.sqrt(v_hat) + ADAM_EPS) + WD * w_ref[...])
        nm_ref[...] = m2
        nv_ref[...] = v2

    blk = pl.BlockSpec((tr, c), lambda i: (i, 0))
    pspec = lambda kk: pl.BlockSpec((npart, tr, c), lambda i: (0, jnp.clip(i - kk * nbk, 0, nbk - 1), 0))
    return pl.pallas_call(
        body, name=name, grid=(r // tr,),
        in_specs=[pspec(kk) for kk in range(nk)] + [blk, blk, blk],
        out_specs=[blk] * 4, out_shape=[jax.ShapeDtypeStruct((r, c), F32)] * 4,
        compiler_params=_params(("parallel",)))(*parts_list, w, m, v)


def _my_index():
    return 4 * lax.axis_index("x") + 2 * lax.axis_index("y") + lax.axis_index("c")


def exchange(arrays, scatter, *, name):
    nt = len(arrays)

    def body(*refs):
        ins = refs[:nt]
        outs = refs[nt:2 * nt]
        send_sems, recv_sems, local_sems = refs[2 * nt:]
        x, y, c = lax.axis_index("x"), lax.axis_index("y"), lax.axis_index("c")
        me = 4 * x + 2 * y + c
        copies = []
        for t in range(nt):
            src = ins[t].at[me] if scatter[t] else ins[t]
            cp = pltpu.make_async_copy(src, outs[t].at[me], local_sems.at[t])
            cp.start()
            copies.append(cp)
        remote = []
        for kk in range(1, N_DEV):
            px, py, pc = x ^ (kk >> 2), y ^ ((kk >> 1) & 1), c ^ (kk & 1)
            peer = 4 * px + 2 * py + pc
            for t in range(nt):
                src = ins[t].at[peer] if scatter[t] else ins[t]
                cp = pltpu.make_async_remote_copy(
                    src_ref=src, dst_ref=outs[t].at[me], send_sem=send_sems.at[t, kk], recv_sem=recv_sems.at[t, kk],
                    device_id=(px, py, pc), device_id_type=pl.DeviceIdType.MESH)
                cp.start()
                remote.append(cp)
        for cp in remote:
            cp.wait()
        for cp in copies:
            cp.wait()

    hbm = pl.BlockSpec(memory_space=pl.ANY)
    out_shape = [jax.ShapeDtypeStruct(a.shape if scatter[t] else (N_DEV,) + a.shape, a.dtype)
                 for t, a in enumerate(arrays)]
    return pl.pallas_call(
        body, name=name, in_specs=[hbm] * nt, out_specs=[hbm] * nt, out_shape=out_shape,
        scratch_shapes=[pltpu.SemaphoreType.DMA((nt, N_DEV)), pltpu.SemaphoreType.DMA((nt, N_DEV)),
                        pltpu.SemaphoreType.DMA((nt,))],
        compiler_params=pltpu.CompilerParams(has_side_effects=True))(*arrays)


def _peer_of(kk):
    x, y, c = lax.axis_index("x"), lax.axis_index("y"), lax.axis_index("c")
    return x ^ (kk >> 2), y ^ ((kk >> 1) & 1), c ^ (kk & 1)


def _peer_copy(t, kk, scatter, ins, lands, send_sems, recv_sems):
    px, py, pc = _peer_of(kk)
    me = _my_index()
    src = ins[t].at[4 * px + 2 * py + pc] if scatter[t] else ins[t]
    return pltpu.make_async_remote_copy(
        src_ref=src, dst_ref=lands[t].at[me], send_sem=send_sems.at[t * N_DEV + kk],
        recv_sem=recv_sems.at[t * N_DEV + kk], device_id=(px, py, pc), device_id_type=pl.DeviceIdType.MESH)


_HBM = pl.BlockSpec(memory_space=pltpu.HBM)
_SEM = pl.BlockSpec(memory_space=pltpu.SEMAPHORE)
_EFFECT = pltpu.SideEffectType.DATAFLOW_SIDE_EFFECTING


def exchange_start(arrays, scatter, *, name):
    nt = len(arrays)
    land_shapes = [a.shape if scatter[t] else (N_DEV,) + a.shape for t, a in enumerate(arrays)]

    def body(*refs):
        ins, lands = refs[:nt], refs[nt:2 * nt]
        send_sems, recv_sems = refs[2 * nt], refs[2 * nt + 1]
        token = refs[-1]
        for kk in range(1, N_DEV):
            for t in range(nt):
                _peer_copy(t, kk, scatter, ins, lands, send_sems, recv_sems).start()
        token[...] = jnp.zeros_like(token)

    sems = pltpu.SemaphoreType.DMA((nt * N_DEV,))
    outs = pl.pallas_call(
        body, name=name,
        out_shape=(sems, sems, *[pltpu.HBM(a.shape, a.dtype) for a in arrays],
                   *[pltpu.HBM(shp, a.dtype) for shp, a in zip(land_shapes, arrays)],
                   jax.ShapeDtypeStruct((8, 128), F32)),
        in_specs=[_HBM] * (2 * nt),
        out_specs=(_SEM, _SEM, *[_HBM] * (2 * nt), pl.BlockSpec(memory_space=pltpu.VMEM)),
        input_output_aliases={i: 2 + i for i in range(2 * nt)},
        compiler_params=pltpu.CompilerParams(has_side_effects=_EFFECT),
    )(*[pltpu.with_memory_space_constraint(a, pltpu.HBM) for a in arrays],
      *[pltpu.with_memory_space_constraint(lax.empty(shp, a.dtype), pltpu.HBM) for shp, a in zip(land_shapes, arrays)])
    return (outs[0], outs[1], outs[2:2 + nt], outs[2 + nt:2 + 2 * nt], scatter), outs[-1]


def exchange_wait(handle, after, *, name):
    send_sems, recv_sems, thru, lands, scatter = handle
    nt = len(thru)

    def body(*refs):
        ins, lnd = refs[:nt], refs[nt:2 * nt]
        s_sems, r_sems = refs[2 * nt], refs[2 * nt + 1]
        for kk in range(1, N_DEV):
            for t in range(nt):
                cp = _peer_copy(t, kk, scatter, ins, lnd, s_sems, r_sems)
                cp.wait_send()
                cp.wait_recv()

    outs = pl.pallas_call(
        body, name=name,
        out_shape=(*[pltpu.HBM(a.shape, a.dtype) for a in thru], *[pltpu.HBM(a.shape, a.dtype) for a in lands]),
        in_specs=[_HBM] * (2 * nt) + [_SEM, _SEM, pl.BlockSpec(memory_space=pl.ANY)],
        out_specs=tuple([_HBM] * (2 * nt)),
        input_output_aliases={i: i for i in range(2 * nt)},
        compiler_params=pltpu.CompilerParams(has_side_effects=_EFFECT),
    )(*thru, *lands, send_sems, recv_sems, after)
    return outs[nt:]


def _with_own(landed, own, me):
    return lax.dynamic_update_index_in_dim(landed, own.astype(landed.dtype), me, 0)


def _cols_from_shards(g):
    g = jnp.moveaxis(g, 0, -2)
    return g.reshape(g.shape[:-2] + (g.shape[-2] * g.shape[-1],))


def _cols_to_shards(w):
    w = w.reshape(w.shape[:-1] + (N_DEV, w.shape[-1] // N_DEV))
    return jnp.moveaxis(w, -2, 0)


def _ffn_fwd(x, g, w_up, cw, cb, w_down, tag):
    h = rms_fwd(x, g, name=f"{tag}_norm")
    up = matmul(h, w_up, name=f"{tag}_up")
    act = ffn_mid_fwd(up, cw, cb, name=f"{tag}_mid")
    out = matmul(act, w_down, res=x, name=f"{tag}_down")
    return out, (h, up, act)


def _ffn_bwd(dx, dxb, x, saved, g, w_up, cw, cb, w_down, tag):
    h, up, act = saved
    dact = matmul(dxb, w_down, tb=True, name=f"{tag}_ddown")
    d_w_down = matmul_ta(act, dxb, name=f"{tag}_gdown")
    dug, duv, dcw, dcb = ffn_mid_bwd(dact, up, cw, cb, name=f"{tag}_dmid")
    dup = jnp.concatenate([dug, duv], axis=1)
    d_w_up = matmul_ta(h, dup, name=f"{tag}_gup")
    dh = matmul(dup, w_up, tb=True, name=f"{tag}_dup")
    dx2, dxb2, dg = rms_bwd(dh, x, g, dx, name=f"{tag}_dnorm")
    return dx2, dxb2, dict(norm_g=dg, w_up=d_w_up, conv_w=dcw, conv_b=dcb, w_down=d_w_down)


def local_step(x0, tgt, a, weights, grads_out):
    s, d = x0.shape
    aw = a['even_v_ln_g'].shape[-1]
    causal = jnp.tril(jnp.ones((CHUNK, CHUNK), dtype=bool))
    wm = jnp.where(causal, a['even_w_s'][0], 0.0).astype(BF16)
    wmt = jnp.swapaxes(wm, 1, 2)
    bm = jnp.repeat(a['even_b_s'][0].T, HEAD, axis=1)
    sel = (jnp.arange(aw)[:, None] // HEAD == jnp.arange(128)[None, :]).astype(BF16)
    cos, sin = rope_tables(s)
    ffn_g, ffn_cb = a['ffn_norm_g'], a['ffn_conv_b']

    w0 = weights(0, None)
    w_in, conv_w, odd_g, ffn_cw = w0['w_in'], w0['conv_w'], w0['odd_g'], w0['ffn_cw']
    h0 = rms_fwd(x0, w0['even_g'], name="even_norm")
    z = matmul(h0, w_in, bias=a['even_b_in'], name="even_in")
    ycat, hc = even_mid_fwd(z, a['even_v_ln_g'], a['even_v_ln_b'], wm, bm, conv_w, a['even_conv_b'],
                            a['even_conv_ln_g'], a['even_conv_ln_b'], name="even_mid")
    w1 = weights(1, ycat)
    x1 = matmul(ycat, w1['w_out'], res=x0, name="even_out")
    x2, ffn0 = _ffn_fwd(x1, ffn_g[0:1], w1['w_up'], ffn_cw[0], ffn_cb[0:1], w1['w_down'], "ffn0")
    h2 = rms_fwd(x2, odd_g, name="odd_norm")
    w2 = weights(2, h2)
    qkv = matmul(h2, w2['w_qkv'], name="odd_qkv")
    q, k = rope_fwd(qkv, cos, sin, name="rope")
    o, lse = attn_fwd(q, k, qkv, name="attn_fwd")
    x3 = matmul(o, w2['w_o'], res=x2, name="odd_out")
    x4, ffn1 = _ffn_fwd(x3, ffn_g[1:2], w2['w_up'], ffn_cw[1], ffn_cb[1:2], w2['w_down'], "ffn1")
    loss_t, dx, dxb, d_final_g = final_loss_bwd(x4, a['final_norm_g'].reshape(1, -1), tgt, name="final_loss")

    dx, dxb, g1 = _ffn_bwd(dx, dxb, x3, ffn1, ffn_g[1:2], w2['w_up'], ffn_cw[1], ffn_cb[1:2], w2['w_down'], "ffn1")
    dep = grads_out(0, dict(w_up=g1['w_up'], w_down=g1['w_down']))
    do = matmul(dxb, w2['w_o'], tb=True, dep=dep, name="odd_dout")
    d_w_o = matmul_ta(o, dxb, name="odd_gout")
    delta = attn_delta(do, o, name="attn_delta")
    dq = attn_dq(q, k, qkv, do, lse, delta, name="attn_dq")
    dk, dv = attn_dkv(q, k, qkv, do, lse, delta, name="attn_dkv")
    dqkv = rope_bwd(dq, dk, dv, cos, sin, name="rope_bwd")
    d_w_qkv = matmul_ta(h2, dqkv, name="odd_gqkv")
    dep = grads_out(1, dict(w_qkv=d_w_qkv, w_o=d_w_o))
    dh2 = matmul(dqkv, w2['w_qkv'], tb=True, dep=dep, name="odd_dqkv")
    dx, dxb, d_odd_g = rms_bwd(dh2, x2, odd_g, dx, name="odd_dnorm")
    dx, dxb, g0 = _ffn_bwd(dx, dxb, x1, ffn0, ffn_g[0:1], w1['w_up'], ffn_cw[0], ffn_cb[0:1], w1['w_down'], "ffn0")
    dep = grads_out(2, dict(w_up=g0['w_up'], w_down=g0['w_down']))
    dycat = matmul(dxb, w1['w_out'], tb=True, dep=dep, name="even_dout")
    d_w_out = matmul_ta(ycat, dxb, name="even_gout")
    (dza, dhc, dba, dvg, dvb, dwm, dbs, dcg, dcbeta, dcb) = even_mid_bwd_rows(
        dycat, z, hc, a['even_v_ln_g'], a['even_v_ln_b'], wm, wmt, bm, sel, a['even_conv_ln_g'],
        a['even_conv_ln_b'], name="even_dmid_rows")
    dzb, dbb, dcw = even_conv_bwd(dhc, z, conv_w, name="even_dmid_conv")
    dz = jnp.concatenate([dza, dzb], axis=1)
    d_w_in = matmul_ta(h0, dz, name="even_gin")
    dh0 = matmul(dz, w_in, tb=True, name="even_din")
    grad_x, _, d_even_g = rms_bwd(dh0, x0, w0['even_g'], dx, name="even_dnorm")

    nh = a['even_w_s'].shape[1]
    small_grads = {
        'even_norm_g': d_even_g, 'even_b_in': jnp.concatenate([dba, dbb], axis=1), 'even_v_ln_g': dvg,
        'even_v_ln_b': dvb, 'even_w_s': jnp.where(causal, dwm, 0.0)[None], 'even_b_s': dbs[:, :nh].T[None],
        'even_conv_w': dcw[None], 'even_conv_b': dcb, 'even_conv_ln_g': dcg, 'even_conv_ln_b': dcbeta,
        'odd_norm_g': d_odd_g, 'ffn_norm_g': jnp.concatenate([g0['norm_g'], g1['norm_g']], axis=0),
        'ffn_conv_w': jnp.stack([g0['conv_w'], g1['conv_w']]),
        'ffn_conv_b': jnp.concatenate([g0['conv_b'], g1['conv_b']], axis=0),
        'final_norm_g': d_final_g.reshape(-1),
    }
    grads_out(3, dict(w_in=d_w_in, w_out=d_w_out, small=small_grads))
    return loss_t, grad_x, small_grads


BIG = ['even_w_in', 'even_w_out', 'odd_w_qkv', 'odd_w_o', 'ffn_w_up', 'ffn_w_down']


def _as_tiles(flat):
    return jnp.pad(flat, (0, (-flat.size) % 1024)).reshape(-1, 128)


def kernel(*args):
    a = dict(zip(NAMES + ['loss_target'] + ['m_' + n for n in WEIGHTS] + ['v_' + n for n in WEIGHTS], args))
    x0 = a['x'][0]
    tgt = a['loss_target'][0]
    s, d = x0.shape
    me = _my_index()
    bf = lambda t: t.astype(BF16)

    small_local = _as_tiles(jnp.concatenate([a['even_conv_w'].reshape(-1), a['odd_norm_g'].reshape(-1),
                                             a['ffn_conv_w'].reshape(-1)]))
    stage_arrays = [
        [bf(a['even_w_in']), small_local],
        [bf(a['even_w_out']), bf(a['ffn_w_up'][0:1]), bf(a['ffn_w_down'][0:1])],
        [bf(a['odd_w_qkv']), bf(a['odd_w_o']), bf(a['ffn_w_up'][1:2]), bf(a['ffn_w_down'][1:2])],
    ]
    started = [exchange_start(arrs, [False] * len(arrs), name=f"gather{i}_start") for i, arrs in enumerate(stage_arrays)]
    order = sum(tok[0, 0] for _, tok in started)

    def weights(stage, after):
        handle, tok = started[stage]
        landed = exchange_wait(handle, tok if after is None else after, name=f"gather{stage}_wait")
        full = [_with_own(l, own, me) for l, own in zip(landed, stage_arrays[stage])]
        rows = lambda g: jnp.moveaxis(g, 0, 1).reshape(-1, d)
        if stage == 0:
            gs = full[1].reshape(N_DEV, -1)
            n_cw, n_og, n_fw = a['even_conv_w'].size, a['odd_norm_g'].size, a['ffn_conv_w'].size
            return dict(
                w_in=_cols_from_shards(full[0])[0], even_g=a['even_norm_g'] + order,
                conv_w=_cols_from_shards(gs[:, :n_cw].reshape((N_DEV,) + a['even_conv_w'].shape))[0],
                odd_g=gs[:, n_cw:n_cw + n_og].reshape(1, -1),
                ffn_cw=_cols_from_shards(gs[:, n_cw + n_og:n_cw + n_og + n_fw].reshape((N_DEV,) + a['ffn_conv_w'].shape)))
        if stage == 1:
            return dict(w_out=rows(full[0]), w_up=_cols_from_shards(full[1])[0], w_down=rows(full[2]))
        return dict(w_qkv=_cols_from_shards(full[0])[0], w_o=rows(full[1]), w_up=_cols_from_shards(full[2])[0],
                    w_down=rows(full[3]))

    sent = {}

    def grads_out(stage, g):
        to_rows = lambda w: w.reshape(N_DEV, 1, -1, d)
        if stage in (0, 2):
            pieces = [_cols_to_shards(g['w_up'][None]), to_rows(g['w_down'])]
        elif stage == 1:
            pieces = [_cols_to_shards(g['w_qkv'][None]), to_rows(g['w_o'])]
        else:
            names = list(g['small'])
            pieces = [_cols_to_shards(g['w_in'][None]), to_rows(g['w_out']),
                      _as_tiles(jnp.concatenate([g['small'][n].reshape(-1) for n in names]))]
        scatter = [True, True] + [False] * (len(pieces) - 2)
        handle, tok = exchange_start(pieces, scatter, name=f"grads{stage}_start")
        sent[stage] = (handle, pieces, scatter)
        return tok

    loss_t, grad_x, small_grads = local_step(x0, tgt, a, weights, grads_out)
    loss = lax.psum(loss_t[0, 0], ("x", "y", "c"))

    received = {}
    for stage, (handle, pieces, scatter) in sent.items():
        landed = exchange_wait(handle, grad_x, name=f"grads{stage}_wait")
        received[stage] = [_with_own(l, p[me] if sc else p, me) for l, p, sc in zip(landed, pieces, scatter)]

    results = {}
    big_parts = {'even_w_in': [received[3][0]], 'even_w_out': [received[3][1]], 'odd_w_qkv': [received[1][0]],
                 'odd_w_o': [received[1][1]], 'ffn_w_up': [received[2][0], received[0][0]],
                 'ffn_w_down': [received[2][1], received[0][1]]}
    for n in BIG:
        shp = a[n].shape
        flat = lambda t: t.reshape(-1, shp[-1])
        outs = adamw([p.reshape(N_DEV, -1, shp[-1]) for p in big_parts[n]], flat(a[n]), flat(a['m_' + n]),
                     flat(a['v_' + n]), name=f"adamw_{n}")
        results[n] = [t.reshape(shp) for t in outs]

    small_names = list(small_grads)
    n_small = sum(small_grads[n].size for n in small_names)
    rs = received[3][2].reshape(N_DEV, -1)[:, :n_small]
    parts, offs = [], 0
    for n in small_names:
        full = small_grads[n].shape
        piece = rs[:, offs:offs + small_grads[n].size].reshape((N_DEV,) + full)
        offs += small_grads[n].size
        shp = a[n].shape
        if shp != full:
            width = shp[-1]
            piece = lax.dynamic_slice_in_dim(piece, me * width, width, axis=piece.ndim - 1)
        parts.append(piece.reshape(N_DEV, -1))
    parts = jnp.concatenate(parts, axis=1)
    pad = (-parts.shape[1]) % 1024
    cat = lambda pre: _as_tiles(jnp.concatenate([a[pre + n].reshape(-1) for n in small_names]))
    outs = adamw([jnp.pad(parts, ((0, 0), (0, pad))).reshape(N_DEV, -1, 128)], cat(''), cat('m_'), cat('v_'),
                 name="adamw_small")
    offs = 0
    for n in small_names:
        size = a[n].size
        results[n] = [t.reshape(-1)[offs:offs + size].reshape(a[n].shape) for t in outs]
        offs += size

    out = [loss, grad_x[None]]
    for i in range(4):
        out += [results[n][i] for n in WEIGHTS]
    return tuple(out)
```

```python
import functools
import math

import jax
import jax.numpy as jnp
from jax import lax
from jax.experimental import pallas as pl
from jax.experimental.pallas import tpu as pltpu

F32 = jnp.float32
BF16 = jnp.bfloat16

N_DEV = 8
EPS = 1e-6
NEG = -1e30
HEAD = 64
CHUNK = 128
BLOCK = 128
CONV_K = 31
FFN_K = 3
DILATIONS = (1, 4, 16)
ROPE_THETA = 10000.0
LR, B1, B2, ADAM_EPS, WD, STEP = 0.001, 0.9, 0.999, 1e-08, 0.01, 10

VMEM_LIMIT = 56 * 1024 * 1024
VMEM_BUDGET = 32 * 1024 * 1024
ROWS = 512
HALO = 32
FHALO = 8

NAMES = ['x', 'even_norm_g', 'even_w_in', 'even_b_in', 'even_v_ln_g', 'even_v_ln_b', 'even_w_s', 'even_b_s',
         'even_conv_w', 'even_conv_b', 'even_conv_ln_g', 'even_conv_ln_b', 'even_w_out', 'odd_norm_g',
         'odd_w_qkv', 'odd_w_o', 'ffn_norm_g', 'ffn_w_up', 'ffn_conv_w', 'ffn_conv_b', 'ffn_w_down',
         'final_norm_g']
WEIGHTS = NAMES[1:]


def _params(sem=None):
    return pltpu.CompilerParams(dimension_semantics=sem, vmem_limit_bytes=VMEM_LIMIT)


def _sigmoid(x):
    return 1.0 / (1.0 + jnp.exp(-x))


def _gelu(x):
    c = math.sqrt(2.0 / math.pi)
    return 0.5 * x * (1.0 + jnp.tanh(c * (x + 0.044715 * x * x * x)))


def _gelu_grad(x):
    c = math.sqrt(2.0 / math.pi)
    t = jnp.tanh(c * (x + 0.044715 * x * x * x))
    return 0.5 * (1.0 + t) + 0.5 * x * (1.0 - t * t) * c * (1.0 + 3.0 * 0.044715 * x * x)


def _ln_stats(x):
    mu = jnp.mean(x, axis=-1, keepdims=True)
    xc = x - mu
    rstd = lax.rsqrt(jnp.mean(xc * xc, axis=-1, keepdims=True) + EPS)
    return xc * rstd, rstd


def _ln_bwd(dy, xhat, rstd, g):
    dxh = dy * g
    return rstd * (dxh - jnp.mean(dxh, axis=-1, keepdims=True) - xhat * jnp.mean(dxh * xhat, axis=-1, keepdims=True))


def _colsum(x):
    return jnp.sum(x, axis=0, keepdims=True)


def _split3(x):
    hi = x.astype(BF16)
    r = x - hi.astype(F32)
    mid = r.astype(BF16)
    lo = (r - mid.astype(F32)).astype(BF16)
    return hi, mid, lo


def _dot(a, b, dims):
    return lax.dot_general(a, b, (dims, ((), ())), preferred_element_type=F32)


NN = ((1,), (0,))
NT = ((1,), (1,))
TN = ((0,), (0,))


def _divisors(n, cands):
    return [c for c in cands if c <= n and n % c == 0]


def _pick_tiles(m, n, k, a_bytes, b_bytes, o_bytes, extra_bytes):
    best = None
    for tm in _divisors(m, (1024, 512, 256, 128)):
        for tn in _divisors(n, (1408, 1024, 768, 704, 512, 384, 256, 128)):
            if tn % 128:
                continue
            need = 2 * (tm * k * a_bytes + k * tn * b_bytes + tm * tn * (o_bytes + extra_bytes)) + tm * tn * 4
            if need <= VMEM_BUDGET and (best is None or tm * tn > best[0] * best[1]):
                best = (tm, tn)
    assert best is not None, (m, n, k)
    return best


def matmul(a, b, *, tb=False, bk=0, bias=None, res=None, dep=None, out_dtype=F32, name):
    m, k = a.shape
    n = b.shape[0] if tb else b.shape[1]
    assert b.shape[1] % k == 0 if tb else (b.shape[0] == k and bk == 0)
    tm, tn = _pick_tiles(m, n, k, a.dtype.itemsize, b.dtype.itemsize, jnp.dtype(out_dtype).itemsize,
                         4 if res is not None else 0)

    def body(*refs):
        a_ref, b_ref = refs[:2]
        o_ref = refs[-1]
        acc = _dot(a_ref[...].astype(BF16), b_ref[...].astype(BF16), NT if tb else NN)
        pos = 2
        if bias is not None:
            acc = acc + refs[pos][...]
            pos += 1
        if res is not None:
            acc = acc + refs[pos][...]
        o_ref[...] = acc.astype(out_dtype)

    in_specs = [pl.BlockSpec((tm, k), lambda i, j: (i, 0)),
                pl.BlockSpec((tn, k), lambda i, j: (j, bk)) if tb else pl.BlockSpec((k, tn), lambda i, j: (0, j))]
    args = [a, b]
    if bias is not None:
        in_specs.append(pl.BlockSpec((1, tn), lambda i, j: (0, j)))
        args.append(bias)
    if res is not None:
        in_specs.append(pl.BlockSpec((tm, tn), lambda i, j: (i, j)))
        args.append(res)
    if dep is not None:
        in_specs.append(pl.BlockSpec(memory_space=pl.ANY))
        args.append(dep)
    return pl.pallas_call(
        body, name=name, grid=(m // tm, n // tn), in_specs=in_specs,
        out_specs=pl.BlockSpec((tm, tn), lambda i, j: (i, j)),
        out_shape=jax.ShapeDtypeStruct((m, n), out_dtype),
        compiler_params=_params(("parallel", "parallel")))(*args)


def matmul_ta(a, b, *, dep=None, out_dtype=BF16, name):
    s, m = a.shape
    n = b.shape[1]
    assert b.shape[0] == s
    best = None
    for tm in _divisors(m, (512, 256, 128)):
        for tn in _divisors(n, (1024, 512, 384, 256, 128)):
            need = 2 * (s * tm * a.dtype.itemsize + s * tn * b.dtype.itemsize + tm * tn * 2) + tm * tn * 4
            if need <= VMEM_BUDGET and (best is None or tm * tn > best[0] * best[1]):
                best = (tm, tn)
    tm, tn = best

    def body(*refs):
        a_ref, b_ref, o_ref = refs[0], refs[1], refs[-1]
        o_ref[...] = _dot(a_ref[...].astype(BF16), b_ref[...].astype(BF16), TN).astype(out_dtype)

    in_specs = [pl.BlockSpec((s, tm), lambda i, j: (0, i)), pl.BlockSpec((s, tn), lambda i, j: (0, j))]
    args = [a, b]
    if dep is not None:
        in_specs.append(pl.BlockSpec(memory_space=pl.ANY))
        args.append(dep)
    return pl.pallas_call(
        body, name=name, grid=(m // tm, n // tn), in_specs=in_specs,
        out_specs=pl.BlockSpec((tm, tn), lambda i, j: (i, j)),
        out_shape=jax.ShapeDtypeStruct((m, n), out_dtype),
        compiler_params=_params(("parallel", "parallel")))(*args)


def rms_fwd(x, g, *, name):
    s, d = x.shape

    def body(x_ref, g_ref, h_ref):
        xv = x_ref[...]
        r = lax.rsqrt(jnp.mean(xv * xv, axis=-1, keepdims=True) + EPS)
        h_ref[...] = (xv * r * g_ref[...]).astype(BF16)

    return pl.pallas_call(
        body, name=name, grid=(s // ROWS,),
        in_specs=[pl.BlockSpec((ROWS, d), lambda i: (i, 0)), pl.BlockSpec((1, d), lambda i: (0, 0))],
        out_specs=pl.BlockSpec((ROWS, d), lambda i: (i, 0)),
        out_shape=jax.ShapeDtypeStruct((s, d), BF16),
        compiler_params=_params(("parallel",)))(x, g)


def rms_bwd(dh, x, g, dres, *, name):
    s, d = x.shape

    def body(dh_ref, x_ref, g_ref, dres_ref, dx_ref, dxb_ref, dg_ref):
        xv = x_ref[...]
        r = lax.rsqrt(jnp.mean(xv * xv, axis=-1, keepdims=True) + EPS)
        xhat = xv * r
        dhv = dh_ref[...]
        dxh = dhv * g_ref[...]
        dx = dres_ref[...] + r * (dxh - xhat * jnp.mean(dxh * xhat, axis=-1, keepdims=True))
        dx_ref[...] = dx
        dxb_ref[...] = dx.astype(BF16)

        @pl.when(pl.program_id(0) == 0)
        def _():
            dg_ref[...] = jnp.zeros_like(dg_ref)
        dg_ref[...] += _colsum(dhv * xhat)

    row = pl.BlockSpec((ROWS, d), lambda i: (i, 0))
    vec = pl.BlockSpec((1, d), lambda i: (0, 0))
    return pl.pallas_call(
        body, name=name, grid=(s // ROWS,),
        in_specs=[row, row, vec, row], out_specs=[row, row, vec],
        out_shape=[jax.ShapeDtypeStruct((s, d), F32), jax.ShapeDtypeStruct((s, d), BF16),
                   jax.ShapeDtypeStruct((1, d), F32)],
        compiler_params=_params(("arbitrary",)))(dh, x, g, dres)


def final_loss_bwd(x, g, tgt, *, name):
    s, d = x.shape

    def body(x_ref, g_ref, t_ref, loss_ref, dx_ref, dxb_ref, dg_ref):
        xv = x_ref[...]
        gv = g_ref[...]
        r = lax.rsqrt(jnp.mean(xv * xv, axis=-1, keepdims=True) + EPS)
        xhat = xv * r
        e = xhat * gv - t_ref[...]
        dy = e * (1.0 / d)
        dxh = dy * gv
        dx = r * (dxh - xhat * jnp.mean(dxh * xhat, axis=-1, keepdims=True))
        dx_ref[...] = dx
        dxb_ref[...] = dx.astype(BF16)

        @pl.when(pl.program_id(0) == 0)
        def _():
            dg_ref[...] = jnp.zeros_like(dg_ref)
            loss_ref[...] = jnp.zeros_like(loss_ref)
        dg_ref[...] += _colsum(dy * xhat)
        loss_ref[...] += 0.5 * jnp.sum(jnp.mean(e * e, axis=-1, keepdims=True))

    row = pl.BlockSpec((ROWS, d), lambda i: (i, 0))
    vec = pl.BlockSpec((1, d), lambda i: (0, 0))
    one = pl.BlockSpec((8, 128), lambda i: (0, 0))
    return pl.pallas_call(
        body, name=name, grid=(s // ROWS,),
        in_specs=[row, vec, row], out_specs=[one, row, row, vec],
        out_shape=[jax.ShapeDtypeStruct((8, 128), F32), jax.ShapeDtypeStruct((s, d), F32),
                   jax.ShapeDtypeStruct((s, d), BF16), jax.ShapeDtypeStruct((1, d), F32)],
        compiler_params=_params(("arbitrary",)))(x, g, tgt)


def _pair_masks():
    lane = lax.broadcasted_iota(jnp.int32, (CHUNK, 128), 1)
    return lane < HEAD


def _head_keep(rows):
    lane = lax.broadcasted_iota(jnp.int32, (rows, 128), 1)
    first = jnp.where(lane < HEAD, 1.0, 0.0)
    return first.astype(BF16), (1.0 - first).astype(BF16)


def _gating_mixed(vn_b, wm_ref, lo):
    rows, aw = vn_b.shape
    out = []
    for c in range(rows // CHUNK):
        tiles = []
        for p in range(aw // 128):
            vp = vn_b[c * CHUNK:(c + 1) * CHUNK, p * 128:(p + 1) * 128]
            r0 = _dot(wm_ref[2 * p], vp, NN)
            r1 = _dot(wm_ref[2 * p + 1], vp, NN)
            tiles.append(jnp.where(lo, r0, r1))
        out.append(jnp.concatenate(tiles, axis=1))
    return jnp.concatenate(out, axis=0)


def even_mid_fwd(z, vg, vb, wm, bm, cw, cb, cg, cbeta, *, name):
    s, zw = z.shape
    aw = zw // 4
    nblk = s // ROWS

    def body(z_ref, zp_ref, vg_ref, vb_ref, wm_ref, bm_ref, cw_ref, cb_ref, cg_ref, cbeta_ref,
             y_ref, hc_ref, ext_ref):
        i = pl.program_id(0)
        lo = _pair_masks()
        u = _gelu(z_ref[:, 0:aw])
        v = _gelu(z_ref[:, aw:2 * aw])
        vhat, _ = _ln_stats(v)
        vn = (vhat * vg_ref[...] + vb_ref[...]).astype(BF16)
        mixed = _gating_mixed(vn, wm_ref, lo)
        bias = jnp.concatenate([bm_ref[...]] * (ROWS // CHUNK), axis=0)
        y_ref[:, 0:aw] = (u * (mixed + bias)).astype(BF16)

        hb = z_ref[:, 2 * aw:3 * aw] * _sigmoid(z_ref[:, 3 * aw:4 * aw])
        hbp = zp_ref[:, 0:aw] * _sigmoid(zp_ref[:, aw:2 * aw])
        ext_ref[0:HALO, :] = jnp.where(i > 0, hbp, 0.0)
        ext_ref[HALO:HALO + ROWS, :] = hb
        acc = jnp.zeros((ROWS, aw), F32) + cb_ref[...]
        for k in range(CONV_K):
            acc = acc + cw_ref[k:k + 1, :] * ext_ref[pl.ds(HALO - (CONV_K - 1) + k, ROWS), :]
        hc_ref[...] = acc
        hhat, _ = _ln_stats(acc)
        hn = hhat * cg_ref[...] + cbeta_ref[...]
        y_ref[:, aw:2 * aw] = (hn * _sigmoid(hn)).astype(BF16)

    hb_per = ROWS // HALO
    vec = pl.BlockSpec((1, aw), lambda i: (0, 0))
    return pl.pallas_call(
        body, name=name, grid=(nblk,),
        in_specs=[pl.BlockSpec((ROWS, zw), lambda i: (i, 0)),
                  pl.BlockSpec((HALO, 2 * aw), lambda i: (jnp.maximum(i * hb_per - 1, 0), 1)),
                  vec, vec,
                  pl.BlockSpec(wm.shape, lambda i: (0, 0, 0)),
                  pl.BlockSpec((CHUNK, aw), lambda i: (0, 0)),
                  pl.BlockSpec((CONV_K, aw), lambda i: (0, 0)), vec, vec, vec],
        out_specs=[pl.BlockSpec((ROWS, 2 * aw), lambda i: (i, 0)), pl.BlockSpec((ROWS, aw), lambda i: (i, 0))],
        out_shape=[jax.ShapeDtypeStruct((s, 2 * aw), BF16), jax.ShapeDtypeStruct((s, aw), F32)],
        scratch_shapes=[pltpu.VMEM((HALO + ROWS, aw), F32)],
        compiler_params=_params(("parallel",)))(z, z, vg, vb, wm, bm, cw, cb, cg, cbeta)


def even_mid_bwd_rows(dy, z, hc, vg, vb, wm, wmt, bm, sel, cg, cbeta, *, name):
    s, zw = z.shape
    aw = zw // 4
    nh = wm.shape[0]

    def body(dy_ref, z_ref, hc_ref, vg_ref, vb_ref, wm_ref, wmt_ref, bm_ref, sel_ref, cg_ref, cbeta_ref,
             dza_ref, dhc_ref, dba_ref, dvg_ref, dvb_ref, dwm_ref, dbs_ref, dcg_ref, dcbeta_ref, dcb_ref):
        @pl.when(pl.program_id(0) == 0)
        def _():
            for r in (dba_ref, dvg_ref, dvb_ref, dwm_ref, dbs_ref, dcg_ref, dcbeta_ref, dcb_ref):
                r[...] = jnp.zeros_like(r)

        lo = _pair_masks()
        keep = _head_keep(CHUNK)
        zu = z_ref[:, 0:aw]
        zv = z_ref[:, aw:2 * aw]
        u = _gelu(zu)
        v = _gelu(zv)
        vhat, vrstd = _ln_stats(v)
        vn = (vhat * vg_ref[...] + vb_ref[...]).astype(BF16)
        mixed = _gating_mixed(vn, wm_ref, lo)
        bias = jnp.concatenate([bm_ref[...]] * (ROWS // CHUNK), axis=0)
        dya = dy_ref[:, 0:aw]
        du = dya * (mixed + bias)
        dmix = dya * u
        dmix_b = dmix.astype(BF16)

        dvn_rows = []
        for c in range(ROWS // CHUNK):
            rs = slice(c * CHUNK, (c + 1) * CHUNK)
            tiles = []
            for p in range(aw // 128):
                cs = slice(p * 128, (p + 1) * 128)
                dm = dmix_b[rs, cs]
                dm0 = dm * keep[0]
                dm1 = dm * keep[1]
                vp = vn[rs, cs]
                tiles.append(_dot(wmt_ref[2 * p], dm0, NN) + _dot(wmt_ref[2 * p + 1], dm1, NN))
                dwm_ref[2 * p] += _dot(dm0, vp, NT)
                dwm_ref[2 * p + 1] += _dot(dm1, vp, NT)
            dvn_rows.append(jnp.concatenate(tiles, axis=1))
            acc = jnp.zeros((CHUNK, 128), F32)
            for part in _split3(dmix[rs, :]):
                acc = acc + _dot(part, sel_ref[...], NN)
            dbs_ref[...] += acc
        dvn = jnp.concatenate(dvn_rows, axis=0)
        dvg_ref[...] += _colsum(dvn * vhat)
        dvb_ref[...] += _colsum(dvn)
        dv = _ln_bwd(dvn, vhat, vrstd, vg_ref[...])
        dzu = du * _gelu_grad(zu)
        dzv = dv * _gelu_grad(zv)
        dza_ref[:, 0:aw] = dzu.astype(BF16)
        dza_ref[:, aw:2 * aw] = dzv.astype(BF16)
        dba_ref[:, 0:aw] += _colsum(dzu)
        dba_ref[:, aw:2 * aw] += _colsum(dzv)

        hcv = hc_ref[...]
        hhat, hrstd = _ln_stats(hcv)
        hn = hhat * cg_ref[...] + cbeta_ref[...]
        sg = _sigmoid(hn)
        dhn = dy_ref[:, aw:2 * aw] * (sg * (1.0 + hn * (1.0 - sg)))
        dcg_ref[...] += _colsum(dhn * hhat)
        dcbeta_ref[...] += _colsum(dhn)
        dhc = _ln_bwd(dhn, hhat, hrstd, cg_ref[...])
        dhc_ref[...] = dhc
        dcb_ref[...] += _colsum(dhc)

    vec = pl.BlockSpec((1, aw), lambda i: (0, 0))
    vec2 = pl.BlockSpec((1, 2 * aw), lambda i: (0, 0))
    w3 = pl.BlockSpec(wm.shape, lambda i: (0, 0, 0))
    sq = pl.BlockSpec((CHUNK, 128), lambda i: (0, 0))
    return pl.pallas_call(
        body, name=name, grid=(s // ROWS,),
        in_specs=[pl.BlockSpec((ROWS, 2 * aw), lambda i: (i, 0)), pl.BlockSpec((ROWS, 2 * aw), lambda i: (i, 0)),
                  pl.BlockSpec((ROWS, aw), lambda i: (i, 0)), vec, vec, w3, w3,
                  pl.BlockSpec((CHUNK, aw), lambda i: (0, 0)), pl.BlockSpec((aw, 128), lambda i: (0, 0)), vec, vec],
        out_specs=[pl.BlockSpec((ROWS, 2 * aw), lambda i: (i, 0)), pl.BlockSpec((ROWS, aw), lambda i: (i, 0)),
                   vec2, vec, vec, w3, sq, vec, vec, vec],
        out_shape=[jax.ShapeDtypeStruct((s, 2 * aw), BF16), jax.ShapeDtypeStruct((s, aw), F32),
                   jax.ShapeDtypeStruct((1, 2 * aw), F32), jax.ShapeDtypeStruct((1, aw), F32),
                   jax.ShapeDtypeStruct((1, aw), F32), jax.ShapeDtypeStruct(wm.shape, F32),
                   jax.ShapeDtypeStruct((CHUNK, 128), F32), jax.ShapeDtypeStruct((1, aw), F32),
                   jax.ShapeDtypeStruct((1, aw), F32), jax.ShapeDtypeStruct((1, aw), F32)],
        compiler_params=_params(("arbitrary",)))(dy, z, hc, vg, vb, wm, wmt, bm, sel, cg, cbeta)


def even_conv_bwd(dhc, z, cw, *, name):
    s, zw = z.shape
    aw = zw // 4
    nblk = s // ROWS
    hb_per = ROWS // HALO

    def body(dc_ref, dn_ref, z_ref, zp_ref, cw_ref, dzb_ref, dbb_ref, dcw_ref, exth_ref, extd_ref):
        i = pl.program_id(0)

        @pl.when(i == 0)
        def _():
            dbb_ref[...] = jnp.zeros_like(dbb_ref)
            dcw_ref[...] = jnp.zeros_like(dcw_ref)

        a = z_ref[:, 0:aw]
        sg = _sigmoid(z_ref[:, aw:2 * aw])
        exth_ref[0:HALO, :] = jnp.where(i > 0, zp_ref[:, 0:aw] * _sigmoid(zp_ref[:, aw:2 * aw]), 0.0)
        exth_ref[HALO:HALO + ROWS, :] = a * sg
        dcur = dc_ref[...]
        extd_ref[0:ROWS, :] = dcur
        extd_ref[ROWS:ROWS + HALO, :] = jnp.where(i < nblk - 1, dn_ref[...], 0.0)
        dhb = jnp.zeros((ROWS, aw), F32)
        for k in range(CONV_K):
            wk = cw_ref[k:k + 1, :]
            dhb = dhb + wk * extd_ref[pl.ds(CONV_K - 1 - k, ROWS), :]
            dcw_ref[k:k + 1, :] += _colsum(dcur * exth_ref[pl.ds(HALO - (CONV_K - 1) + k, ROWS), :])
        da = dhb * sg
        dg = dhb * a * sg * (1.0 - sg)
        dzb_ref[:, 0:aw] = da.astype(BF16)
        dzb_ref[:, aw:2 * aw] = dg.astype(BF16)
        dbb_ref[:, 0:aw] += _colsum(da)
        dbb_ref[:, aw:2 * aw] += _colsum(dg)

    return pl.pallas_call(
        body, name=name, grid=(nblk,),
        in_specs=[pl.BlockSpec((ROWS, aw), lambda i: (i, 0)),
                  pl.BlockSpec((HALO, aw), lambda i: (jnp.minimum((i + 1) * hb_per, nblk * hb_per - 1), 0)),
                  pl.BlockSpec((ROWS, 2 * aw), lambda i: (i, 1)),
                  pl.BlockSpec((HALO, 2 * aw), lambda i: (jnp.maximum(i * hb_per - 1, 0), 1)),
                  pl.BlockSpec((CONV_K, aw), lambda i: (0, 0))],
        out_specs=[pl.BlockSpec((ROWS, 2 * aw), lambda i: (i, 0)), pl.BlockSpec((1, 2 * aw), lambda i: (0, 0)),
                   pl.BlockSpec((CONV_K, aw), lambda i: (0, 0))],
        out_shape=[jax.ShapeDtypeStruct((s, 2 * aw), BF16), jax.ShapeDtypeStruct((1, 2 * aw), F32),
                   jax.ShapeDtypeStruct((CONV_K, aw), F32)],
        scratch_shapes=[pltpu.VMEM((HALO + ROWS, aw), F32), pltpu.VMEM((ROWS + HALO, aw), F32)],
        compiler_params=_params(("arbitrary",)))(dhc, dhc, z, z, cw)


FFN_ROWS = 1024
FFN_CHUNK = 16


def _ffn_tile(f):
    for t in (256, 128):
        if f % t == 0:
            return t
    raise ValueError(f)


def _taps(ext_ref, w, b, r0, rows):
    acc = b
    for k in range(FFN_K):
        acc = acc + w[k] * ext_ref[pl.ds(FHALO - (FFN_K - 1) + k + r0, rows), :]
    return acc


def ffn_mid_fwd(up_g, up_v, cw, cb, *, name):
    s, f = up_g.shape
    tn = _ffn_tile(f)
    nj = f // tn
    ROWS = FFN_ROWS
    per = ROWS // FHALO

    def body(ug_ref, uv_ref, pg_ref, pv_ref, wg_ref, wv_ref, bg_ref, bv_ref, act_ref, eg_ref, ev_ref):
        i = pl.program_id(0)
        for cur_ref, prev_ref, ext_ref in ((ug_ref, pg_ref, eg_ref), (uv_ref, pv_ref, ev_ref)):
            ext_ref[0:FHALO, :] = jnp.where(i > 0, prev_ref[...], 0.0)
            ext_ref[FHALO:FHALO + ROWS, :] = cur_ref[...]
        wg = [wg_ref[k:k + 1, :] for k in range(FFN_K)]
        wv = [wv_ref[k:k + 1, :] for k in range(FFN_K)]
        bg, bv = bg_ref[...], bv_ref[...]
        for r0 in range(0, ROWS, FFN_CHUNK):
            gate = _taps(eg_ref, wg, bg, r0, FFN_CHUNK)
            val = _taps(ev_ref, wv, bv, r0, FFN_CHUNK)
            act_ref[pl.ds(r0, FFN_CHUNK), :] = (gate * _sigmoid(gate) * val).astype(BF16)

    cur = lambda off: pl.BlockSpec((ROWS, tn), lambda i, j: (i, j + off))
    prev = lambda off: pl.BlockSpec((FHALO, tn), lambda i, j: (jnp.maximum(i * per - 1, 0), j + off))
    wsp = lambda off: pl.BlockSpec((FFN_K, tn), lambda i, j: (0, j + off))
    bsp = lambda off: pl.BlockSpec((1, tn), lambda i, j: (0, j + off))
    return pl.pallas_call(
        body, name=name, grid=(s // ROWS, nj),
        in_specs=[cur(0), cur(0), prev(0), prev(0), wsp(0), wsp(nj), bsp(0), bsp(nj)],
        out_specs=pl.BlockSpec((ROWS, tn), lambda i, j: (i, j)),
        out_shape=jax.ShapeDtypeStruct((s, f), BF16),
        scratch_shapes=[pltpu.VMEM((FHALO + ROWS, tn), F32), pltpu.VMEM((FHALO + ROWS, tn), F32)],
        compiler_params=_params(("parallel", "parallel")))(up_g, up_v, up_g, up_v, cw, cw, cb, cb)


def ffn_mid_bwd(dact, up_g, up_v, cw, cb, *, name):
    s, f = up_g.shape
    tn = _ffn_tile(f)
    nj = f // tn
    ROWS = FFN_ROWS
    nblk = s // ROWS
    per = ROWS // FHALO
    ext = ROWS + FHALO

    def body(da_ref, dan_ref, ug_ref, uv_ref, pg_ref, pv_ref, ng_ref, nv_ref, wg_ref, wv_ref, bg_ref, bv_ref,
             dug_ref, duv_ref, dwg_ref, dwv_ref, dbg_ref, dbv_ref, eg_ref, ev_ref, dg_ref, dv_ref):
        i = pl.program_id(1)

        @pl.when(i == 0)
        def _():
            for r in (dwg_ref, dwv_ref, dbg_ref, dbv_ref):
                r[...] = jnp.zeros_like(r)

        for cur_ref, prev_ref, next_ref, ext_ref in ((ug_ref, pg_ref, ng_ref, eg_ref), (uv_ref, pv_ref, nv_ref, ev_ref)):
            ext_ref[0:FHALO, :] = jnp.where(i > 0, prev_ref[...], 0.0)
            ext_ref[FHALO:FHALO + ROWS, :] = cur_ref[...]
            ext_ref[FHALO + ROWS:FHALO + ROWS + FHALO, :] = jnp.where(i < nblk - 1, next_ref[...], 0.0)
        wg = [wg_ref[k:k + 1, :] for k in range(FFN_K)]
        wv = [wv_ref[k:k + 1, :] for k in range(FFN_K)]
        bg, bv = bg_ref[...], bv_ref[...]

        for r0, rows in [(r, FFN_CHUNK) for r in range(0, ROWS, FFN_CHUNK)] + [(ROWS, FHALO)]:
            gate = _taps(eg_ref, wg, bg, r0, rows)
            val = _taps(ev_ref, wv, bv, r0, rows)
            da = da_ref[pl.ds(r0, rows), :] if r0 < ROWS else jnp.where(i < nblk - 1, dan_ref[...], 0.0)
            sg = _sigmoid(gate)
            dg_ref[pl.ds(r0, rows), :] = da * val * (sg * (1.0 + gate * (1.0 - sg)))
            dv_ref[pl.ds(r0, rows), :] = da * (gate * sg)

        def back(d_ref, w, ext_ref, du_ref, dw_ref, db_ref):
            zero = jnp.zeros((FFN_CHUNK, tn), F32)
            acc = [zero] * FFN_K
            accb = zero
            for r0 in range(0, ROWS, FFN_CHUNK):
                d = [d_ref[pl.ds(r0 + FFN_K - 1 - k, FFN_CHUNK), :] for k in range(FFN_K)]
                u = ext_ref[pl.ds(FHALO + r0, FFN_CHUNK), :]
                du = w[0] * d[0]
                for k in range(1, FFN_K):
                    du = du + w[k] * d[k]
                du_ref[pl.ds(r0, FFN_CHUNK), :] = du.astype(BF16)
                acc = [acc[k] + u * d[k] for k in range(FFN_K)]
                accb = accb + d[FFN_K - 1]
            for k in range(FFN_K):
                dw_ref[k:k + 1, :] += _colsum(acc[k])
            db_ref[...] += _colsum(accb)

        back(dg_ref, wg, eg_ref, dug_ref, dwg_ref, dbg_ref)
        back(dv_ref, wv, ev_ref, duv_ref, dwv_ref, dbv_ref)

    cur = lambda off: pl.BlockSpec((ROWS, tn), lambda j, i: (i, j + off))
    prev = lambda off: pl.BlockSpec((FHALO, tn), lambda j, i: (jnp.maximum(i * per - 1, 0), j + off))
    nxt = lambda off: pl.BlockSpec((FHALO, tn), lambda j, i: (jnp.minimum((i + 1) * per, nblk * per - 1), j + off))
    wsp = lambda off: pl.BlockSpec((FFN_K, tn), lambda j, i: (0, j + off))
    bsp = lambda off: pl.BlockSpec((1, tn), lambda j, i: (0, j + off))
    outs = pl.pallas_call(
        body, name=name, grid=(nj, nblk),
        in_specs=[cur(0), nxt(0), cur(0), cur(0), prev(0), prev(0), nxt(0), nxt(0),
                  wsp(0), wsp(nj), bsp(0), bsp(nj)],
        out_specs=[cur(0), cur(0), wsp(0), wsp(0), bsp(0), bsp(0)],
        out_shape=[jax.ShapeDtypeStruct((s, f), BF16), jax.ShapeDtypeStruct((s, f), BF16),
                   jax.ShapeDtypeStruct((FFN_K, f), F32), jax.ShapeDtypeStruct((FFN_K, f), F32),
                   jax.ShapeDtypeStruct((1, f), F32), jax.ShapeDtypeStruct((1, f), F32)],
        scratch_shapes=[pltpu.VMEM((ROWS + 2 * FHALO, tn), F32), pltpu.VMEM((ROWS + 2 * FHALO, tn), F32),
                        pltpu.VMEM((ext, tn), F32), pltpu.VMEM((ext, tn), F32)],
        compiler_params=_params(("parallel", "arbitrary")))(dact, dact, up_g, up_v, up_g, up_v, up_g, up_v,
                                                            cw, cw, cb, cb)
    dug, duv, dwg, dwv, dbg, dbv = outs
    return dug, duv, jnp.concatenate([dwg, dwv], axis=1), jnp.concatenate([dbg, dbv], axis=1)


def rope_tables(s):
    half = HEAD // 2
    lane = jnp.arange(128)
    j = lane % HEAD
    inv = ROPE_THETA ** (-(j % half).astype(F32) / half)
    ang = jnp.arange(s, dtype=F32)[:, None] * inv[None, :]
    sign = jnp.where(j < half, -1.0, 1.0).astype(F32)
    return jnp.cos(ang), jnp.sin(ang) * sign[None, :]


def _swap_halves(x):
    lane = lax.broadcasted_iota(jnp.int32, x.shape, 1)
    return jnp.where((lane % HEAD) < HEAD // 2, pltpu.roll(x, 128 - HEAD // 2, 1), pltpu.roll(x, HEAD // 2, 1))


def rope_fwd(qkv, cos, sin, *, name):
    s, d3 = qkv.shape
    d = d3 // 3
    scale = HEAD ** -0.5

    def body(xq_ref, xk_ref, c_ref, s_ref, q_ref, k_ref):
        c = c_ref[...]
        sn = s_ref[...]
        for t in range(d // 128):
            cs = slice(t * 128, (t + 1) * 128)
            xq = xq_ref[:, cs]
            xk = xk_ref[:, cs]
            q_ref[:, cs] = (xq * c + _swap_halves(xq) * sn) * scale
            k_ref[:, cs] = xk * c + _swap_halves(xk) * sn

    row = pl.BlockSpec((ROWS, d), lambda i: (i, 0))
    tab = pl.BlockSpec((ROWS, 128), lambda i: (i, 0))
    return pl.pallas_call(
        body, name=name, grid=(s // ROWS,),
        in_specs=[row, pl.BlockSpec((ROWS, d), lambda i: (i, 1)), tab, tab],
        out_specs=[row, row],
        out_shape=[jax.ShapeDtypeStruct((s, d), F32)] * 2,
        compiler_params=_params(("parallel",)))(qkv, qkv, cos, sin)


def rope_bwd(dq, dk, dv, cos, sin, *, name):
    s, d = dq.shape
    scale = HEAD ** -0.5

    def body(dq_ref, dk_ref, dv_ref, c_ref, s_ref, o_ref):
        c = c_ref[...]
        sn = s_ref[...]
        for t in range(d // 128):
            cs = slice(t * 128, (t + 1) * 128)
            gq = dq_ref[:, cs] * scale
            gk = dk_ref[:, cs]
            o_ref[:, t * 128:(t + 1) * 128] = (gq * c + _swap_halves(gq * sn)).astype(BF16)
            o_ref[:, d + t * 128:d + (t + 1) * 128] = (gk * c + _swap_halves(gk * sn)).astype(BF16)
        o_ref[:, 2 * d:3 * d] = dv_ref[...].astype(BF16)

    row = pl.BlockSpec((ROWS, d), lambda i: (i, 0))
    tab = pl.BlockSpec((ROWS, 128), lambda i: (i, 0))
    return pl.pallas_call(
        body, name=name, grid=(s // ROWS,),
        in_specs=[row, row, row, tab, tab],
        out_specs=pl.BlockSpec((ROWS, 3 * d), lambda i: (i, 0)),
        out_shape=jax.ShapeDtypeStruct((s, 3 * d), BF16),
        compiler_params=_params(("parallel",)))(dq, dk, dv, cos, sin)


ATT_T = BLOCK * max(DILATIONS)


FWD_QROWS = 128
BWD_QROWS = 64


def _unit_rows(r, j, dil):
    start = r + dil * BLOCK * j
    return pl.ds(start, BLOCK) if dil == 1 else pl.ds(start, BLOCK, stride=dil)


def _units():
    for bi, dil in enumerate(DILATIONS):
        nsub = ATT_T // (BLOCK * dil)
        for r in range(dil):
            for j in range(nsub):
                yield bi, dil, nsub, r, j


def _band(first_block, part, qrows):
    qi = lax.broadcasted_iota(jnp.int32, (qrows, 2 * BLOCK), 0) + part * qrows
    kj = lax.broadcasted_iota(jnp.int32, (qrows, 2 * BLOCK), 1)
    dist = BLOCK + qi - kj
    band = (dist >= 0) & (dist <= BLOCK)
    return band, band & (jnp.logical_not(first_block) | (kj >= BLOCK))


def _col(tile, h):
    return tile[:, h * HEAD:h * HEAD + 1]


def _keys(cur_ref, prev_ref, r, j, dil, nsub):
    cur = cur_ref[_unit_rows(r, j, dil), :]
    prev = cur_ref[_unit_rows(r, j - 1, dil), :] if j > 0 else prev_ref[_unit_rows(r, nsub - 1, dil), :]
    return jnp.concatenate([prev, cur], axis=0).astype(BF16)


def _att_specs(d, col_off=0):
    nt_cols = d // 128
    cur = pl.BlockSpec((ATT_T, 128), lambda n, p: (n, p + col_off))
    prv = pl.BlockSpec((ATT_T, 128), lambda n, p: (jnp.maximum(n - 1, 0), p + col_off))
    return cur, prv


def attn_fwd(q, k, qkv, *, name):
    s, d = q.shape
    nt = s // ATT_T

    def body(q_ref, kc_ref, kp_ref, vc_ref, vp_ref, o_ref, lse_ref, acc_ref, m_ref, l_ref):
        n = pl.program_id(0)
        QROWS = FWD_QROWS
        nparts = BLOCK // QROWS
        bands = [_band(n == 0, part, QROWS) for part in range(nparts)]
        lo = _pair_masks()
        keep = _head_keep(BLOCK)
        nb = len(DILATIONS)
        for bi, dil, nsub, r, j in _units():
            rows = _unit_rows(r, j, dil)
            kw = _keys(kc_ref, kp_ref, r, j, dil, nsub)
            vw = _keys(vc_ref, vp_ref, r, j, dil, nsub)
            qp = q_ref[rows, :].astype(BF16)
            sc2 = _dot(jnp.concatenate([qp * keep[0], qp * keep[1]], axis=0), kw, NT)
            if bi > 0:
                m_old_t = m_ref[rows, :]
            prs, new_m, new_l, alpha = [], [], [], []
            for c in range(2 * nparts):
                h, part = divmod(c, nparts)
                valid = bands[part][0 if j > 0 else 1]
                sc = jnp.where(valid, sc2[c * QROWS:(c + 1) * QROWS], NEG)
                mx = jnp.max(sc, axis=-1, keepdims=True)
                if bi == 0:
                    m_new = mx
                else:
                    m_old = _col(m_old_t[part * QROWS:(part + 1) * QROWS], h)
                    m_new = jnp.maximum(m_old, mx)
                    alpha.append(jnp.exp(m_old - m_new))
                pr = jnp.exp(sc - m_new)
                new_m.append(m_new)
                new_l.append(jnp.sum(pr, axis=-1, keepdims=True))
                prs.append(pr.astype(BF16))
            pv2 = _dot(jnp.concatenate(prs, axis=0), vw, NN)
            tile = lambda cols: jnp.where(lo, jnp.concatenate(cols[:nparts], axis=0),
                                          jnp.concatenate(cols[nparts:], axis=0))
            m_t = tile(new_m)
            l_t = tile(new_l)
            acc_t = jnp.where(lo, pv2[:BLOCK], pv2[BLOCK:])
            if bi > 0:
                a_t = tile(alpha)
                l_t = a_t * l_ref[rows, :] + l_t
                acc_t = a_t * acc_ref[rows, :] + acc_t
            if bi == nb - 1:
                o_ref[rows, :] = acc_t / l_t
                lse_ref[rows, :] = m_t + jnp.log(l_t)
            else:
                acc_ref[rows, :] = acc_t
                m_ref[rows, :] = m_t
                l_ref[rows, :] = l_t

    cur, prv = _att_specs(d)
    vcur, vprv = _att_specs(d, 2 * (d // 128))
    return pl.pallas_call(
        body, name=name, grid=(nt, d // 128), in_specs=[cur, cur, prv, vcur, vprv], out_specs=[cur, cur],
        out_shape=[jax.ShapeDtypeStruct((s, d), F32)] * 2,
        scratch_shapes=[pltpu.VMEM((ATT_T, 128), F32)] * 3,
        compiler_params=_params(("parallel", "parallel")))(q, k, k, qkv, qkv)


def attn_delta(do, o, *, name):
    s, d = do.shape

    def body(do_ref, o_ref, dl_ref):
        lane = lax.broadcasted_iota(jnp.int32, (ROWS, 128), 1)
        lo = lane < HEAD
        for p in range(d // 128):
            cs = slice(p * 128, (p + 1) * 128)
            pr = do_ref[:, cs] * o_ref[:, cs]
            s0 = jnp.sum(jnp.where(lo, pr, 0.0), axis=-1, keepdims=True)
            s1 = jnp.sum(jnp.where(lo, 0.0, pr), axis=-1, keepdims=True)
            dl_ref[:, cs] = jnp.where(lo, s0, s1)

    row = pl.BlockSpec((ROWS, d), lambda i: (i, 0))
    return pl.pallas_call(
        body, name=name, grid=(s // ROWS,), in_specs=[row, row], out_specs=row,
        out_shape=jax.ShapeDtypeStruct((s, d), F32),
        compiler_params=_params(("parallel",)))(do, o)


def attn_dq(q, k, qkv, do, lse, delta, *, name):
    s, d = q.shape
    nt = s // ATT_T

    def body(q_ref, kc_ref, kp_ref, vc_ref, vp_ref, do_ref, l_ref, dl_ref, dq_ref):
        n = pl.program_id(0)
        QROWS = BWD_QROWS
        nparts = BLOCK // QROWS
        bands = [_band(n == 0, part, QROWS) for part in range(nparts)]
        lo = _pair_masks()
        keep = _head_keep(BLOCK)
        for bi, dil, nsub, r, j in _units():
            rows = _unit_rows(r, j, dil)
            kw = _keys(kc_ref, kp_ref, r, j, dil, nsub)
            vw = _keys(vc_ref, vp_ref, r, j, dil, nsub)
            qp = q_ref[rows, :].astype(BF16)
            dop = do_ref[rows, :].astype(BF16)
            lt = l_ref[rows, :]
            dt = dl_ref[rows, :]
            sc2 = _dot(jnp.concatenate([qp * keep[0], qp * keep[1]], axis=0), kw, NT)
            dp2 = _dot(jnp.concatenate([dop * keep[0], dop * keep[1]], axis=0), vw, NT)
            dss = []
            for c in range(2 * nparts):
                h, part = divmod(c, nparts)
                valid = bands[part][0 if j > 0 else 1]
                cr = slice(c * QROWS, (c + 1) * QROWS)
                pr_rows = slice(part * QROWS, (part + 1) * QROWS)
                pr = jnp.where(valid, jnp.exp(sc2[cr] - _col(lt[pr_rows], h)), 0.0)
                dss.append((pr * (dp2[cr] - _col(dt[pr_rows], h))).astype(BF16))
            dq2 = _dot(jnp.concatenate(dss, axis=0), kw, NN)
            dq_t = jnp.where(lo, dq2[:BLOCK], dq2[BLOCK:])
            if bi > 0:
                dq_t = dq_t + dq_ref[rows, :]
            dq_ref[rows, :] = dq_t

    cur, prv = _att_specs(d)
    vcur, vprv = _att_specs(d, 2 * (d // 128))
    return pl.pallas_call(
        body, name=name, grid=(nt, d // 128), in_specs=[cur, cur, prv, vcur, vprv, cur, cur, cur], out_specs=cur,
        out_shape=jax.ShapeDtypeStruct((s, d), F32),
        compiler_params=_params(("parallel", "parallel")))(q, k, k, qkv, qkv, do, lse, delta)


def attn_dkv(q, k, qkv, do, lse, delta, *, name):
    s, d = q.shape
    nt = s // ATT_T

    def body(k_ref, v_ref, qc_ref, qn_ref, doc_ref, don_ref, lc_ref, ln_ref, dc_ref, dn_ref, dk_ref, dv_ref):
        n = pl.program_id(0)
        qi = lax.broadcasted_iota(jnp.int32, (BLOCK, BLOCK), 0)
        kj = lax.broadcasted_iota(jnp.int32, (BLOCK, BLOCK), 1)
        own = kj <= qi
        nxt = kj >= qi
        nxt_edge = nxt & (n < nt - 1)
        keep = _head_keep(BLOCK)
        lo = _pair_masks()
        for bi, dil, nsub, r, j in _units():
            rows = _unit_rows(r, j, dil)
            inner = j + 1 < nsub
            nrows = _unit_rows(r, j + 1, dil) if inner else _unit_rows(r, 0, dil)
            kp = k_ref[rows, :].astype(BF16)
            vp = v_ref[rows, :].astype(BF16)
            far = not inner
            take = lambda c_ref, n_ref, nx: ((n_ref if far else c_ref)[nrows, :] if nx else c_ref[rows, :])
            qs = [take(qc_ref, qn_ref, nx).astype(BF16) for nx in (False, True)]
            dos = [take(doc_ref, don_ref, nx).astype(BF16) for nx in (False, True)]
            lts = [take(lc_ref, ln_ref, nx) for nx in (False, True)]
            dts = [take(dc_ref, dn_ref, nx) for nx in (False, True)]
            valids = (own, nxt if inner else nxt_edge)
            q4 = jnp.concatenate([qs[nx] * keep[h] for h in range(2) for nx in range(2)], axis=0)
            do4 = jnp.concatenate([dos[nx] * keep[h] for h in range(2) for nx in range(2)], axis=0)
            s4 = _dot(q4, kp, NT)
            dp4 = _dot(do4, vp, NT)
            prs, dss = [], []
            for c in range(4):
                h, nx = divmod(c, 2)
                cr = slice(c * BLOCK, (c + 1) * BLOCK)
                pr = jnp.where(valids[nx], jnp.exp(s4[cr] - _col(lts[nx], h)), 0.0)
                prs.append(pr.astype(BF16))
                dss.append((pr * (dp4[cr] - _col(dts[nx], h))).astype(BF16))
            dv_t = _dot(jnp.concatenate(prs, axis=0), do4, TN)
            dk_t = _dot(jnp.concatenate(dss, axis=0), q4, TN)
            if bi > 0:
                dk_t = dk_t + dk_ref[rows, :]
                dv_t = dv_t + dv_ref[rows, :]
            dk_ref[rows, :] = dk_t
            dv_ref[rows, :] = dv_t

    cur = pl.BlockSpec((ATT_T, 128), lambda n, p: (n, p))
    nxt_spec = pl.BlockSpec((ATT_T, 128), lambda n, p: (jnp.minimum(n + 1, nt - 1), p))
    vcur = pl.BlockSpec((ATT_T, 128), lambda n, p: (n, p + 2 * (d // 128)))
    return pl.pallas_call(
        body, name=name, grid=(nt, d // 128),
        in_specs=[cur, vcur, cur, nxt_spec, cur, nxt_spec, cur, nxt_spec, cur, nxt_spec], out_specs=[cur, cur],
        out_shape=[jax.ShapeDtypeStruct((s, d), F32)] * 2,
        compiler_params=_params(("parallel", "parallel")))(k, qkv, q, q, do, do, lse, lse, delta, delta)


def adamw(parts_list, w, m, v, *, name):
    nk = len(parts_list)
    npart, rk, c = parts_list[0].shape
    r = rk * nk
    assert w.shape == (r, c)
    tr = next(t for t in range(rk, 0, -8) if rk % t == 0 and (t * c * 4 <= 1024 * 1024 or t == 8))
    nbk = rk // tr

    def body(*refs):
        p_refs = refs[:nk]
        w_ref, m_ref, v_ref, g_ref, d_ref, nm_ref, nv_ref = refs[nk:]
        i = pl.program_id(0)
        g = None
        for kk, p_ref in enumerate(p_refs):
            gk = p_ref[0].astype(F32)
            for j in range(1, npart):
                gk = gk + p_ref[j].astype(F32)
            g = gk if g is None else jnp.where(i >= kk * nbk, gk, g)
        m2 = B1 * m_ref[...] + (1.0 - B1) * g
        v2 = B2 * v_ref[...] + (1.0 - B2) * (g * g)
        m_hat = m2 / (1.0 - B1 ** STEP)
        v_hat = v2 / (1.0 - B2 ** STEP)
        g_ref[...] = g
        d_ref[...] = -LR * (m_hat / (jnp.sqrt(v_hat) + ADAM_EPS) + WD * w_ref[...])
        nm_ref[...] = m2
        nv_ref[...] = v2

    blk = pl.BlockSpec((tr, c), lambda i: (i, 0))
    pspec = lambda kk: pl.BlockSpec((npart, tr, c), lambda i: (0, jnp.clip(i - kk * nbk, 0, nbk - 1), 0))
    return pl.pallas_call(
        body, name=name, grid=(r // tr,),
        in_specs=[pspec(kk) for kk in range(nk)] + [blk, blk, blk],
        out_specs=[blk] * 4, out_shape=[jax.ShapeDtypeStruct((r, c), F32)] * 4,
        compiler_params=_params(("parallel",)))(*parts_list, w, m, v)


def _my_index():
    return 4 * lax.axis_index("x") + 2 * lax.axis_index("y") + lax.axis_index("c")


def exchange(arrays, scatter, *, name):
    nt = len(arrays)

    def body(*refs):
        ins = refs[:nt]
        outs = refs[nt:2 * nt]
        send_sems, recv_sems, local_sems = refs[2 * nt:]
        x, y, c = lax.axis_index("x"), lax.axis_index("y"), lax.axis_index("c")
        me = 4 * x + 2 * y + c
        copies = []
        for t in range(nt):
            src = ins[t].at[me] if scatter[t] else ins[t]
            cp = pltpu.make_async_copy(src, outs[t].at[me], local_sems.at[t])
            cp.start()
            copies.append(cp)
        remote = []
        for kk in range(1, N_DEV):
            px, py, pc = x ^ (kk >> 2), y ^ ((kk >> 1) & 1), c ^ (kk & 1)
            peer = 4 * px + 2 * py + pc
            for t in range(nt):
                src = ins[t].at[peer] if scatter[t] else ins[t]
                cp = pltpu.make_async_remote_copy(
                    src_ref=src, dst_ref=outs[t].at[me], send_sem=send_sems.at[t, kk], recv_sem=recv_sems.at[t, kk],
                    device_id=(px, py, pc), device_id_type=pl.DeviceIdType.MESH)
                cp.start()
                remote.append(cp)
        for cp in remote:
            cp.wait()
        for cp in copies:
            cp.wait()

    hbm = pl.BlockSpec(memory_space=pl.ANY)
    out_shape = [jax.ShapeDtypeStruct(a.shape if scatter[t] else (N_DEV,) + a.shape, a.dtype)
                 for t, a in enumerate(arrays)]
    return pl.pallas_call(
        body, name=name, in_specs=[hbm] * nt, out_specs=[hbm] * nt, out_shape=out_shape,
        scratch_shapes=[pltpu.SemaphoreType.DMA((nt, N_DEV)), pltpu.SemaphoreType.DMA((nt, N_DEV)),
                        pltpu.SemaphoreType.DMA((nt,))],
        compiler_params=pltpu.CompilerParams(has_side_effects=True))(*arrays)


def _peer_of(kk):
    x, y, c = lax.axis_index("x"), lax.axis_index("y"), lax.axis_index("c")
    return x ^ (kk >> 2), y ^ ((kk >> 1) & 1), c ^ (kk & 1)


def _peer_copy(t, kk, scatter, ins, lands, send_sems, recv_sems):
    px, py, pc = _peer_of(kk)
    me = _my_index()
    src = ins[t].at[4 * px + 2 * py + pc] if scatter[t] else ins[t]
    return pltpu.make_async_remote_copy(
        src_ref=src, dst_ref=lands[t].at[me], send_sem=send_sems.at[t * N_DEV + kk],
        recv_sem=recv_sems.at[t * N_DEV + kk], device_id=(px, py, pc), device_id_type=pl.DeviceIdType.MESH)


_HBM = pl.BlockSpec(memory_space=pltpu.HBM)
_SEM = pl.BlockSpec(memory_space=pltpu.SEMAPHORE)
_EFFECT = pltpu.SideEffectType.DATAFLOW_SIDE_EFFECTING


def exchange_start(arrays, scatter, *, name):
    nt = len(arrays)
    land_shapes = [a.shape if scatter[t] else (N_DEV,) + a.shape for t, a in enumerate(arrays)]

    def body(*refs):
        ins, lands = refs[:nt], refs[nt:2 * nt]
        send_sems, recv_sems = refs[2 * nt], refs[2 * nt + 1]
        token = refs[-1]
        for kk in range(1, N_DEV):
            for t in range(nt):
                _peer_copy(t, kk, scatter, ins, lands, send_sems, recv_sems).start()
        token[...] = jnp.zeros_like(token)

    sems = pltpu.SemaphoreType.DMA((nt * N_DEV,))
    outs = pl.pallas_call(
        body, name=name,
        out_shape=(sems, sems, *[pltpu.HBM(a.shape, a.dtype) for a in arrays],
                   *[pltpu.HBM(shp, a.dtype) for shp, a in zip(land_shapes, arrays)],
                   jax.ShapeDtypeStruct((8, 128), F32)),
        in_specs=[_HBM] * (2 * nt),
        out_specs=(_SEM, _SEM, *[_HBM] * (2 * nt), pl.BlockSpec(memory_space=pltpu.VMEM)),
        input_output_aliases={i: 2 + i for i in range(2 * nt)},
        compiler_params=pltpu.CompilerParams(has_side_effects=_EFFECT),
    )(*[pltpu.with_memory_space_constraint(a, pltpu.HBM) for a in arrays],
      *[pltpu.with_memory_space_constraint(lax.empty(shp, a.dtype), pltpu.HBM) for shp, a in zip(land_shapes, arrays)])
    return (outs[0], outs[1], outs[2:2 + nt], outs[2 + nt:2 + 2 * nt], scatter), outs[-1]


def exchange_wait(handle, after, *, name):
    send_sems, recv_sems, thru, lands, scatter = handle
    nt = len(thru)

    def body(*refs):
        ins, lnd = refs[:nt], refs[nt:2 * nt]
        s_sems, r_sems = refs[2 * nt], refs[2 * nt + 1]
        for kk in range(1, N_DEV):
            for t in range(nt):
                cp = _peer_copy(t, kk, scatter, ins, lnd, s_sems, r_sems)
                cp.wait_send()
                cp.wait_recv()

    outs = pl.pallas_call(
        body, name=name,
        out_shape=(*[pltpu.HBM(a.shape, a.dtype) for a in thru], *[pltpu.HBM(a.shape, a.dtype) for a in lands]),
        in_specs=[_HBM] * (2 * nt) + [_SEM, _SEM, pl.BlockSpec(memory_space=pl.ANY)],
        out_specs=tuple([_HBM] * (2 * nt)),
        input_output_aliases={i: i for i in range(2 * nt)},
        compiler_params=pltpu.CompilerParams(has_side_effects=_EFFECT),
    )(*thru, *lands, send_sems, recv_sems, after)
    return outs[nt:]


def place_own(landed, sent, scatter, *, name):
    nt = len(landed)

    def body(*refs):
        src, dst, sems = refs[:nt], refs[2 * nt:3 * nt], refs[-1]
        me = _my_index()
        copies = [pltpu.make_async_copy(src[t].at[me] if scatter[t] else src[t], dst[t].at[me], sems.at[t])
                  for t in range(nt)]
        for cp in copies:
            cp.start()
        for cp in copies:
            cp.wait()

    hbm = pl.BlockSpec(memory_space=pl.ANY)
    return pl.pallas_call(
        body, name=name, in_specs=[hbm] * (2 * nt), out_specs=[hbm] * nt,
        out_shape=[jax.ShapeDtypeStruct(l.shape, l.dtype) for l in landed],
        input_output_aliases={nt + t: t for t in range(nt)},
        scratch_shapes=[pltpu.SemaphoreType.DMA((nt,))],
        compiler_params=pltpu.CompilerParams(has_side_effects=True))(*sent, *landed)


def _cols_from_shards(g):
    g = jnp.moveaxis(g, 0, -2)
    return g.reshape(g.shape[:-2] + (g.shape[-2] * g.shape[-1],))


def _cols_to_shards(w, nshards=N_DEV):
    w = w.reshape(w.shape[:-1] + (nshards, w.shape[-1] // nshards))
    return jnp.moveaxis(w, -2, 0)


def _half_shards(halves):
    return jnp.concatenate([_cols_to_shards(h[None], N_DEV // 2) for h in halves], axis=0)


def _ffn_fwd(x, g, w_up, cw, cb, w_down, tag):
    h = rms_fwd(x, g, name=f"{tag}_norm")
    up_g = matmul(h, w_up[0], name=f"{tag}_up_g")
    up_v = matmul(h, w_up[1], name=f"{tag}_up_v")
    act = ffn_mid_fwd(up_g, up_v, cw, cb, name=f"{tag}_mid")
    out = matmul(act, w_down, res=x, name=f"{tag}_down")
    return out, (h, up_g, up_v, act)


def _ffn_bwd(dx, dxb, x, saved, g, w_up, cw, cb, w_down, tag):
    h, up_g, up_v, act = saved
    dact = matmul(dxb, w_down, tb=True, name=f"{tag}_ddown")
    d_w_down = matmul_ta(act, dxb, name=f"{tag}_gdown")
    dug, duv, dcw, dcb = ffn_mid_bwd(dact, up_g, up_v, cw, cb, name=f"{tag}_dmid")
    d_w_up = (matmul_ta(h, dug, name=f"{tag}_gup_g"), matmul_ta(h, duv, name=f"{tag}_gup_v"))
    dh = matmul(dug, w_up[0], tb=True, name=f"{tag}_dup_g")
    dh = matmul(duv, w_up[1], tb=True, res=dh, name=f"{tag}_dup_v")
    dx2, dxb2, dg = rms_bwd(dh, x, g, dx, name=f"{tag}_dnorm")
    return dx2, dxb2, dict(norm_g=dg, w_up=d_w_up, conv_w=dcw, conv_b=dcb, w_down=d_w_down)


def local_step(x0, tgt, a, weights, grads_out):
    s, d = x0.shape
    aw = a['even_v_ln_g'].shape[-1]
    causal = jnp.tril(jnp.ones((CHUNK, CHUNK), dtype=bool))
    wm = jnp.where(causal, a['even_w_s'][0], 0.0).astype(BF16)
    wmt = jnp.swapaxes(wm, 1, 2)
    bm = jnp.repeat(a['even_b_s'][0].T, HEAD, axis=1)
    sel = (jnp.arange(aw)[:, None] // HEAD == jnp.arange(128)[None, :]).astype(BF16)
    cos, sin = rope_tables(s)
    ffn_g, ffn_cb = a['ffn_norm_g'], a['ffn_conv_b']

    w0 = weights(0, None)
    w_in, conv_w, odd_g, ffn_cw = w0['w_in'], w0['conv_w'], w0['odd_g'], w0['ffn_cw']
    h0 = rms_fwd(x0, w0['even_g'], name="even_norm")
    z = matmul(h0, w_in, bias=a['even_b_in'], name="even_in")
    ycat, hc = even_mid_fwd(z, a['even_v_ln_g'], a['even_v_ln_b'], wm, bm, conv_w, a['even_conv_b'],
                            a['even_conv_ln_g'], a['even_conv_ln_b'], name="even_mid")
    w1 = weights(1, ycat)
    x1 = matmul(ycat, w1['w_out'], res=x0, name="even_out")
    x2, ffn0 = _ffn_fwd(x1, ffn_g[0:1], w1['w_up'], ffn_cw[0], ffn_cb[0:1], w1['w_down'], "ffn0")
    h2 = rms_fwd(x2, odd_g, name="odd_norm")
    w2 = weights(2, h2)
    qkv = matmul(h2, w2['w_qkv'], name="odd_qkv")
    q, k = rope_fwd(qkv, cos, sin, name="rope")
    o, lse = attn_fwd(q, k, qkv, name="attn_fwd")
    x3 = matmul(o, w2['w_o'], res=x2, name="odd_out")
    x4, ffn1 = _ffn_fwd(x3, ffn_g[1:2], w2['w_up'], ffn_cw[1], ffn_cb[1:2], w2['w_down'], "ffn1")
    loss_t, dx, dxb, d_final_g = final_loss_bwd(x4, a['final_norm_g'].reshape(1, -1), tgt, name="final_loss")

    dx, dxb, g1 = _ffn_bwd(dx, dxb, x3, ffn1, ffn_g[1:2], w2['w_up'], ffn_cw[1], ffn_cb[1:2], w2['w_down'], "ffn1")
    dep = grads_out(0, dict(w_up=g1['w_up'], w_down=g1['w_down']))
    do = matmul(dxb, w2['w_o'], tb=True, dep=dep, name="odd_dout")
    d_w_o = matmul_ta(o, dxb, name="odd_gout")
    delta = attn_delta(do, o, name="attn_delta")
    dq = attn_dq(q, k, qkv, do, lse, delta, name="attn_dq")
    dk, dv = attn_dkv(q, k, qkv, do, lse, delta, name="attn_dkv")
    dqkv = rope_bwd(dq, dk, dv, cos, sin, name="rope_bwd")
    d_w_qkv = matmul_ta(h2, dqkv, name="odd_gqkv")
    dep = grads_out(1, dict(w_qkv=d_w_qkv, w_o=d_w_o))
    dh2 = matmul(dqkv, w2['w_qkv'], tb=True, dep=dep, name="odd_dqkv")
    dx, dxb, d_odd_g = rms_bwd(dh2, x2, odd_g, dx, name="odd_dnorm")
    dx, dxb, g0 = _ffn_bwd(dx, dxb, x1, ffn0, ffn_g[0:1], w1['w_up'], ffn_cw[0], ffn_cb[0:1], w1['w_down'], "ffn0")
    dep = grads_out(2, dict(w_up=g0['w_up'], w_down=g0['w_down']))
    d_w_out = matmul_ta(ycat, dxb, dep=dep, name="even_gout")
    dep = grads_out(3, dict(w_out=d_w_out))
    dycat = matmul(dxb, w1['w_out'], tb=True, dep=dep, name="even_dout")
    (dza, dhc, dba, dvg, dvb, dwm, dbs, dcg, dcbeta, dcb) = even_mid_bwd_rows(
        dycat, z, hc, a['even_v_ln_g'], a['even_v_ln_b'], wm, wmt, bm, sel, a['even_conv_ln_g'],
        a['even_conv_ln_b'], name="even_dmid_rows")
    dzb, dbb, dcw = even_conv_bwd(dhc, z, conv_w, name="even_dmid_conv")
    d_w_in = (matmul_ta(h0, dza, name="even_gin_a"), matmul_ta(h0, dzb, name="even_gin_b"))
    dh0 = matmul(dza, w_in, tb=True, bk=0, name="even_din_a")
    dh0 = matmul(dzb, w_in, tb=True, bk=1, res=dh0, name="even_din_b")
    grad_x, _, d_even_g = rms_bwd(dh0, x0, w0['even_g'], dx, name="even_dnorm")

    nh = a['even_w_s'].shape[1]
    small_grads = {
        'even_norm_g': d_even_g, 'even_b_in': jnp.concatenate([dba, dbb], axis=1), 'even_v_ln_g': dvg,
        'even_v_ln_b': dvb, 'even_w_s': jnp.where(causal, dwm, 0.0)[None], 'even_b_s': dbs[:, :nh].T[None],
        'even_conv_w': dcw[None], 'even_conv_b': dcb, 'even_conv_ln_g': dcg, 'even_conv_ln_b': dcbeta,
        'odd_norm_g': d_odd_g, 'ffn_norm_g': jnp.concatenate([g0['norm_g'], g1['norm_g']], axis=0),
        'ffn_conv_w': jnp.stack([g0['conv_w'], g1['conv_w']]),
        'ffn_conv_b': jnp.concatenate([g0['conv_b'], g1['conv_b']], axis=0),
        'final_norm_g': d_final_g.reshape(-1),
    }
    grads_out(4, dict(w_in=d_w_in, small=small_grads))
    return loss_t, grad_x, small_grads


BIG = ['even_w_in', 'even_w_out', 'odd_w_qkv', 'odd_w_o', 'ffn_w_up', 'ffn_w_down']


def _as_tiles(flat, dtype=F32):
    return jnp.pad(flat, (0, (-flat.size) % 2048)).reshape(-1, 128).astype(dtype)


def kernel(*args):
    a = dict(zip(NAMES + ['loss_target'] + ['m_' + n for n in WEIGHTS] + ['v_' + n for n in WEIGHTS], args))
    x0 = a['x'][0]
    tgt = a['loss_target'][0]
    s, d = x0.shape
    me = _my_index()
    bf = lambda t: t.astype(BF16)

    small_local = _as_tiles(jnp.concatenate([a['even_conv_w'].reshape(-1), a['odd_norm_g'].reshape(-1),
                                             a['ffn_conv_w'].reshape(-1)]))
    stage_arrays = [
        [bf(a['even_w_in']), small_local],
        [bf(a['even_w_out']), bf(a['ffn_w_up'][0:1]), bf(a['ffn_w_down'][0:1])],
        [bf(a['odd_w_qkv']), bf(a['odd_w_o']), bf(a['ffn_w_up'][1:2]), bf(a['ffn_w_down'][1:2])],
    ]
    started = [exchange_start(arrs, [False] * len(arrs), name=f"gather{i}_start") for i, arrs in enumerate(stage_arrays)]
    order = sum(tok[0, 0] for _, tok in started)

    def weights(stage, after):
        handle, tok = started[stage]
        landed = exchange_wait(handle, tok if after is None else after, name=f"gather{stage}_wait")
        full = place_own(landed, stage_arrays[stage], [False] * len(landed), name=f"gather{stage}_own")
        rows = lambda g: jnp.moveaxis(g, 0, 1).reshape(-1, d)
        halves = lambda g: (_cols_from_shards(g[:N_DEV // 2])[0], _cols_from_shards(g[N_DEV // 2:])[0])
        if stage == 0:
            gs = full[1].reshape(N_DEV, -1)
            n_cw, n_og, n_fw = a['even_conv_w'].size, a['odd_norm_g'].size, a['ffn_conv_w'].size
            return dict(
                w_in=_cols_from_shards(full[0])[0], even_g=a['even_norm_g'] + order,
                conv_w=_cols_from_shards(gs[:, :n_cw].reshape((N_DEV,) + a['even_conv_w'].shape))[0],
                odd_g=gs[:, n_cw:n_cw + n_og].reshape(1, -1),
                ffn_cw=_cols_from_shards(gs[:, n_cw + n_og:n_cw + n_og + n_fw].reshape((N_DEV,) + a['ffn_conv_w'].shape)))
        if stage == 1:
            return dict(w_out=rows(full[0]), w_up=halves(full[1]), w_down=rows(full[2]))
        return dict(w_qkv=_cols_from_shards(full[0])[0], w_o=rows(full[1]), w_up=halves(full[2]), w_down=rows(full[3]))

    sent = {}

    def grads_out(stage, g):
        to_rows = lambda w: w.reshape(N_DEV, 1, -1, d)
        if stage in (0, 2):
            pieces, scatter = [_half_shards(g['w_up']), to_rows(g['w_down'])], [True, True]
        elif stage == 1:
            pieces, scatter = [_cols_to_shards(g['w_qkv'][None]), to_rows(g['w_o'])], [True, True]
        elif stage == 3:
            pieces, scatter = [to_rows(g['w_out'])], [True]
        else:
            small = jnp.concatenate([g['small'][n].reshape(-1) for n in g['small']])
            pieces, scatter = [_half_shards(g['w_in']), _as_tiles(small, BF16)], [True, False]
        handle, tok = exchange_start(pieces, scatter, name=f"grads{stage}_start")
        sent[stage] = (handle, pieces, scatter)
        return tok

    loss_t, grad_x, small_grads = local_step(x0, tgt, a, weights, grads_out)
    loss = lax.psum(loss_t[0, 0], ("x", "y", "c"))

    received = {}
    for stage, (handle, pieces, scatter) in sent.items():
        landed = exchange_wait(handle, grad_x, name=f"grads{stage}_wait")
        received[stage] = place_own(landed, pieces, scatter, name=f"grads{stage}_own")

    results = {}
    big_parts = {'even_w_in': [received[4][0]], 'even_w_out': [received[3][0]], 'odd_w_qkv': [received[1][0]],
                 'odd_w_o': [received[1][1]], 'ffn_w_up': [received[2][0], received[0][0]],
                 'ffn_w_down': [received[2][1], received[0][1]]}
    for n in BIG:
        shp = a[n].shape
        flat = lambda t: t.reshape(-1, shp[-1])
        outs = adamw([p.reshape(N_DEV, -1, shp[-1]) for p in big_parts[n]], flat(a[n]), flat(a['m_' + n]),
                     flat(a['v_' + n]), name=f"adamw_{n}")
        results[n] = [t.reshape(shp) for t in outs]

    small_names = list(small_grads)
    n_small = sum(small_grads[n].size for n in small_names)
    rs = received[4][1].reshape(N_DEV, -1)[:, :n_small]
    parts, offs = [], 0
    for n in small_names:
        full = small_grads[n].shape
        piece = rs[:, offs:offs + small_grads[n].size].reshape((N_DEV,) + full)
        offs += small_grads[n].size
        shp = a[n].shape
        if shp != full:
            width = shp[-1]
            piece = lax.dynamic_slice_in_dim(piece, me * width, width, axis=piece.ndim - 1)
        parts.append(piece.reshape(N_DEV, -1))
    parts = jnp.concatenate(parts, axis=1)
    pad = (-parts.shape[1]) % 2048
    cat = lambda pre: _as_tiles(jnp.concatenate([a[pre + n].reshape(-1) for n in small_names]))
    outs = adamw([jnp.pad(parts, ((0, 0), (0, pad))).reshape(N_DEV, -1, 128)], cat(''), cat('m_'), cat('v_'),
                 name="adamw_small")
    offs = 0
    for n in small_names:
        size = a[n].size
        results[n] = [t.reshape(-1)[offs:offs + size].reshape(a[n].shape) for t in outs]
        offs += size

    out = [loss, grad_x[None]]
    for i in range(4):
        out += [results[n][i] for n in WEIGHTS]
    return tuple(out)
```

```python
import functools
import math

import jax
import jax.numpy as jnp
from jax import lax
from jax.experimental import pallas as pl
from jax.experimental.pallas import tpu as pltpu

F32 = jnp.float32
BF16 = jnp.bfloat16

N_DEV = 8
EPS = 1e-6
NEG = -1e30
HEAD = 64
CHUNK = 128
BLOCK = 128
CONV_K = 31
FFN_K = 3
DILATIONS = (1, 4, 16)
ROPE_THETA = 10000.0
LR, B1, B2, ADAM_EPS, WD, STEP = 0.001, 0.9, 0.999, 1e-08, 0.01, 10

VMEM_LIMIT = 56 * 1024 * 1024
VMEM_BUDGET = 32 * 1024 * 1024
ROWS = 512
HALO = 32
FHALO = 8

NAMES = ['x', 'even_norm_g', 'even_w_in', 'even_b_in', 'even_v_ln_g', 'even_v_ln_b', 'even_w_s', 'even_b_s',
         'even_conv_w', 'even_conv_b', 'even_conv_ln_g', 'even_conv_ln_b', 'even_w_out', 'odd_norm_g',
         'odd_w_qkv', 'odd_w_o', 'ffn_norm_g', 'ffn_w_up', 'ffn_conv_w', 'ffn_conv_b', 'ffn_w_down',
         'final_norm_g']
WEIGHTS = NAMES[1:]


def _params(sem=None):
    return pltpu.CompilerParams(dimension_semantics=sem, vmem_limit_bytes=VMEM_LIMIT)


def _sigmoid(x):
    return 1.0 / (1.0 + jnp.exp(-x))


def _gelu(x):
    c = math.sqrt(2.0 / math.pi)
    return 0.5 * x * (1.0 + jnp.tanh(c * (x + 0.044715 * x * x * x)))


def _gelu_grad(x):
    c = math.sqrt(2.0 / math.pi)
    t = jnp.tanh(c * (x + 0.044715 * x * x * x))
    return 0.5 * (1.0 + t) + 0.5 * x * (1.0 - t * t) * c * (1.0 + 3.0 * 0.044715 * x * x)


def _ln_stats(x):
    mu = jnp.mean(x, axis=-1, keepdims=True)
    xc = x - mu
    rstd = lax.rsqrt(jnp.mean(xc * xc, axis=-1, keepdims=True) + EPS)
    return xc * rstd, rstd


def _ln_bwd(dy, xhat, rstd, g):
    dxh = dy * g
    return rstd * (dxh - jnp.mean(dxh, axis=-1, keepdims=True) - xhat * jnp.mean(dxh * xhat, axis=-1, keepdims=True))


def _colsum(x):
    return jnp.sum(x, axis=0, keepdims=True)


def _split3(x):
    hi = x.astype(BF16)
    r = x - hi.astype(F32)
    mid = r.astype(BF16)
    lo = (r - mid.astype(F32)).astype(BF16)
    return hi, mid, lo


def _dot(a, b, dims):
    return lax.dot_general(a, b, (dims, ((), ())), preferred_element_type=F32)


NN = ((1,), (0,))
NT = ((1,), (1,))
TN = ((0,), (0,))


def _divisors(n, cands):
    return [c for c in cands if c <= n and n % c == 0]


def _pick_tiles(m, n, k, a_bytes, b_bytes, o_bytes, extra_bytes):
    best = None
    for tm in _divisors(m, (1024, 512, 256, 128)):
        for tn in _divisors(n, (1408, 1024, 768, 704, 512, 384, 256, 128)):
            if tn % 128:
                continue
            need = 2 * (tm * k * a_bytes + k * tn * b_bytes + tm * tn * (o_bytes + extra_bytes)) + tm * tn * 4
            if need <= VMEM_BUDGET and (best is None or tm * tn > best[0] * best[1]):
                best = (tm, tn)
    assert best is not None, (m, n, k)
    return best


def matmul(a, b, *, tb=False, bk=0, bias=None, res=None, dep=None, out_dtype=F32, name):
    m, k = a.shape
    n = b.shape[0] if tb else b.shape[1]
    assert b.shape[1] % k == 0 if tb else (b.shape[0] == k and bk == 0)
    tm, tn = _pick_tiles(m, n, k, a.dtype.itemsize, b.dtype.itemsize, jnp.dtype(out_dtype).itemsize,
                         4 if res is not None else 0)

    def body(*refs):
        a_ref, b_ref = refs[:2]
        o_ref = refs[-1]
        acc = _dot(a_ref[...].astype(BF16), b_ref[...].astype(BF16), NT if tb else NN)
        pos = 2
        if bias is not None:
            acc = acc + refs[pos][...]
            pos += 1
        if res is not None:
            acc = acc + refs[pos][...]
        o_ref[...] = acc.astype(out_dtype)

    in_specs = [pl.BlockSpec((tm, k), lambda i, j: (i, 0)),
                pl.BlockSpec((tn, k), lambda i, j: (j, bk)) if tb else pl.BlockSpec((k, tn), lambda i, j: (0, j))]
    args = [a, b]
    if bias is not None:
        in_specs.append(pl.BlockSpec((1, tn), lambda i, j: (0, j)))
        args.append(bias)
    if res is not None:
        in_specs.append(pl.BlockSpec((tm, tn), lambda i, j: (i, j)))
        args.append(res)
    if dep is not None:
        in_specs.append(pl.BlockSpec(memory_space=pl.ANY))
        args.append(dep)
    return pl.pallas_call(
        body, name=name, grid=(m // tm, n // tn), in_specs=in_specs,
        out_specs=pl.BlockSpec((tm, tn), lambda i, j: (i, j)),
        out_shape=jax.ShapeDtypeStruct((m, n), out_dtype),
        compiler_params=_params(("parallel", "parallel")))(*args)


def matmul_ta(a, b, *, dep=None, out_dtype=BF16, name):
    s, m = a.shape
    n = b.shape[1]
    assert b.shape[0] == s
    best = None
    for tm in _divisors(m, (512, 256, 128)):
        for tn in _divisors(n, (1024, 512, 384, 256, 128)):
            need = 2 * (s * tm * a.dtype.itemsize + s * tn * b.dtype.itemsize + tm * tn * 2) + tm * tn * 4
            if need <= VMEM_BUDGET and (best is None or tm * tn > best[0] * best[1]):
                best = (tm, tn)
    tm, tn = best

    def body(*refs):
        a_ref, b_ref, o_ref = refs[0], refs[1], refs[-1]
        o_ref[...] = _dot(a_ref[...].astype(BF16), b_ref[...].astype(BF16), TN).astype(out_dtype)

    in_specs = [pl.BlockSpec((s, tm), lambda i, j: (0, i)), pl.BlockSpec((s, tn), lambda i, j: (0, j))]
    args = [a, b]
    if dep is not None:
        in_specs.append(pl.BlockSpec(memory_space=pl.ANY))
        args.append(dep)
    return pl.pallas_call(
        body, name=name, grid=(m // tm, n // tn), in_specs=in_specs,
        out_specs=pl.BlockSpec((tm, tn), lambda i, j: (i, j)),
        out_shape=jax.ShapeDtypeStruct((m, n), out_dtype),
        compiler_params=_params(("parallel", "parallel")))(*args)


def rms_fwd(x, g, *, name):
    s, d = x.shape

    def body(x_ref, g_ref, h_ref):
        xv = x_ref[...]
        r = lax.rsqrt(jnp.mean(xv * xv, axis=-1, keepdims=True) + EPS)
        h_ref[...] = (xv * r * g_ref[...]).astype(BF16)

    return pl.pallas_call(
        body, name=name, grid=(s // ROWS,),
        in_specs=[pl.BlockSpec((ROWS, d), lambda i: (i, 0)), pl.BlockSpec((1, d), lambda i: (0, 0))],
        out_specs=pl.BlockSpec((ROWS, d), lambda i: (i, 0)),
        out_shape=jax.ShapeDtypeStruct((s, d), BF16),
        compiler_params=_params(("parallel",)))(x, g)


def rms_bwd(dh, x, g, dres, *, name):
    s, d = x.shape

    def body(dh_ref, x_ref, g_ref, dres_ref, dx_ref, dxb_ref, dg_ref):
        xv = x_ref[...]
        r = lax.rsqrt(jnp.mean(xv * xv, axis=-1, keepdims=True) + EPS)
        xhat = xv * r
        dhv = dh_ref[...]
        dxh = dhv * g_ref[...]
        dx = dres_ref[...] + r * (dxh - xhat * jnp.mean(dxh * xhat, axis=-1, keepdims=True))
        dx_ref[...] = dx
        dxb_ref[...] = dx.astype(BF16)

        @pl.when(pl.program_id(0) == 0)
        def _():
            dg_ref[...] = jnp.zeros_like(dg_ref)
        dg_ref[...] += _colsum(dhv * xhat)

    row = pl.BlockSpec((ROWS, d), lambda i: (i, 0))
    vec = pl.BlockSpec((1, d), lambda i: (0, 0))
    return pl.pallas_call(
        body, name=name, grid=(s // ROWS,),
        in_specs=[row, row, vec, row], out_specs=[row, row, vec],
        out_shape=[jax.ShapeDtypeStruct((s, d), F32), jax.ShapeDtypeStruct((s, d), BF16),
                   jax.ShapeDtypeStruct((1, d), F32)],
        compiler_params=_params(("arbitrary",)))(dh, x, g, dres)


def final_loss_bwd(x, g, tgt, *, name):
    s, d = x.shape

    def body(x_ref, g_ref, t_ref, loss_ref, dx_ref, dxb_ref, dg_ref):
        xv = x_ref[...]
        gv = g_ref[...]
        r = lax.rsqrt(jnp.mean(xv * xv, axis=-1, keepdims=True) + EPS)
        xhat = xv * r
        e = xhat * gv - t_ref[...]
        dy = e * (1.0 / d)
        dxh = dy * gv
        dx = r * (dxh - xhat * jnp.mean(dxh * xhat, axis=-1, keepdims=True))
        dx_ref[...] = dx
        dxb_ref[...] = dx.astype(BF16)

        @pl.when(pl.program_id(0) == 0)
        def _():
            dg_ref[...] = jnp.zeros_like(dg_ref)
            loss_ref[...] = jnp.zeros_like(loss_ref)
        dg_ref[...] += _colsum(dy * xhat)
        loss_ref[...] += 0.5 * jnp.sum(jnp.mean(e * e, axis=-1, keepdims=True))

    row = pl.BlockSpec((ROWS, d), lambda i: (i, 0))
    vec = pl.BlockSpec((1, d), lambda i: (0, 0))
    one = pl.BlockSpec((8, 128), lambda i: (0, 0))
    return pl.pallas_call(
        body, name=name, grid=(s // ROWS,),
        in_specs=[row, vec, row], out_specs=[one, row, row, vec],
        out_shape=[jax.ShapeDtypeStruct((8, 128), F32), jax.ShapeDtypeStruct((s, d), F32),
                   jax.ShapeDtypeStruct((s, d), BF16), jax.ShapeDtypeStruct((1, d), F32)],
        compiler_params=_params(("arbitrary",)))(x, g, tgt)


def _pair_masks():
    lane = lax.broadcasted_iota(jnp.int32, (CHUNK, 128), 1)
    return lane < HEAD


def _head_keep(rows):
    lane = lax.broadcasted_iota(jnp.int32, (rows, 128), 1)
    first = jnp.where(lane < HEAD, 1.0, 0.0)
    return first.astype(BF16), (1.0 - first).astype(BF16)


def _gating_mixed(vn_b, wm_ref, lo):
    rows, aw = vn_b.shape
    out = []
    for c in range(rows // CHUNK):
        tiles = []
        for p in range(aw // 128):
            vp = vn_b[c * CHUNK:(c + 1) * CHUNK, p * 128:(p + 1) * 128]
            r0 = _dot(wm_ref[2 * p], vp, NN)
            r1 = _dot(wm_ref[2 * p + 1], vp, NN)
            tiles.append(jnp.where(lo, r0, r1))
        out.append(jnp.concatenate(tiles, axis=1))
    return jnp.concatenate(out, axis=0)


def even_mid_fwd(z, vg, vb, wm, bm, cw, cb, cg, cbeta, *, name):
    s, zw = z.shape
    aw = zw // 4
    nblk = s // ROWS

    def body(z_ref, zp_ref, vg_ref, vb_ref, wm_ref, bm_ref, cw_ref, cb_ref, cg_ref, cbeta_ref,
             y_ref, hc_ref, ext_ref):
        i = pl.program_id(0)
        lo = _pair_masks()
        u = _gelu(z_ref[:, 0:aw])
        v = _gelu(z_ref[:, aw:2 * aw])
        vhat, _ = _ln_stats(v)
        vn = (vhat * vg_ref[...] + vb_ref[...]).astype(BF16)
        mixed = _gating_mixed(vn, wm_ref, lo)
        bias = jnp.concatenate([bm_ref[...]] * (ROWS // CHUNK), axis=0)
        y_ref[:, 0:aw] = (u * (mixed + bias)).astype(BF16)

        hb = z_ref[:, 2 * aw:3 * aw] * _sigmoid(z_ref[:, 3 * aw:4 * aw])
        hbp = zp_ref[:, 0:aw] * _sigmoid(zp_ref[:, aw:2 * aw])
        ext_ref[0:HALO, :] = jnp.where(i > 0, hbp, 0.0)
        ext_ref[HALO:HALO + ROWS, :] = hb
        acc = jnp.zeros((ROWS, aw), F32) + cb_ref[...]
        for k in range(CONV_K):
            acc = acc + cw_ref[k:k + 1, :] * ext_ref[pl.ds(HALO - (CONV_K - 1) + k, ROWS), :]
        hc_ref[...] = acc
        hhat, _ = _ln_stats(acc)
        hn = hhat * cg_ref[...] + cbeta_ref[...]
        y_ref[:, aw:2 * aw] = (hn * _sigmoid(hn)).astype(BF16)

    hb_per = ROWS // HALO
    vec = pl.BlockSpec((1, aw), lambda i: (0, 0))
    return pl.pallas_call(
        body, name=name, grid=(nblk,),
        in_specs=[pl.BlockSpec((ROWS, zw), lambda i: (i, 0)),
                  pl.BlockSpec((HALO, 2 * aw), lambda i: (jnp.maximum(i * hb_per - 1, 0), 1)),
                  vec, vec,
                  pl.BlockSpec(wm.shape, lambda i: (0, 0, 0)),
                  pl.BlockSpec((CHUNK, aw), lambda i: (0, 0)),
                  pl.BlockSpec((CONV_K, aw), lambda i: (0, 0)), vec, vec, vec],
        out_specs=[pl.BlockSpec((ROWS, 2 * aw), lambda i: (i, 0)), pl.BlockSpec((ROWS, aw), lambda i: (i, 0))],
        out_shape=[jax.ShapeDtypeStruct((s, 2 * aw), BF16), jax.ShapeDtypeStruct((s, aw), F32)],
        scratch_shapes=[pltpu.VMEM((HALO + ROWS, aw), F32)],
        compiler_params=_params(("parallel",)))(z, z, vg, vb, wm, bm, cw, cb, cg, cbeta)


def even_mid_bwd_rows(dy, z, hc, vg, vb, wm, wmt, bm, sel, cg, cbeta, *, name):
    s, zw = z.shape
    aw = zw // 4
    nh = wm.shape[0]

    def body(dy_ref, z_ref, hc_ref, vg_ref, vb_ref, wm_ref, wmt_ref, bm_ref, sel_ref, cg_ref, cbeta_ref,
             dza_ref, dhc_ref, dba_ref, dvg_ref, dvb_ref, dwm_ref, dbs_ref, dcg_ref, dcbeta_ref, dcb_ref):
        @pl.when(pl.program_id(0) == 0)
        def _():
            for r in (dba_ref, dvg_ref, dvb_ref, dwm_ref, dbs_ref, dcg_ref, dcbeta_ref, dcb_ref):
                r[...] = jnp.zeros_like(r)

        lo = _pair_masks()
        keep = _head_keep(CHUNK)
        zu = z_ref[:, 0:aw]
        zv = z_ref[:, aw:2 * aw]
        u = _gelu(zu)
        v = _gelu(zv)
        vhat, vrstd = _ln_stats(v)
        vn = (vhat * vg_ref[...] + vb_ref[...]).astype(BF16)
        mixed = _gating_mixed(vn, wm_ref, lo)
        bias = jnp.concatenate([bm_ref[...]] * (ROWS // CHUNK), axis=0)
        dya = dy_ref[:, 0:aw]
        du = dya * (mixed + bias)
        dmix = dya * u
        dmix_b = dmix.astype(BF16)

        dvn_rows = []
        for c in range(ROWS // CHUNK):
            rs = slice(c * CHUNK, (c + 1) * CHUNK)
            tiles = []
            for p in range(aw // 128):
                cs = slice(p * 128, (p + 1) * 128)
                dm = dmix_b[rs, cs]
                dm0 = dm * keep[0]
                dm1 = dm * keep[1]
                vp = vn[rs, cs]
                tiles.append(_dot(wmt_ref[2 * p], dm0, NN) + _dot(wmt_ref[2 * p + 1], dm1, NN))
                dwm_ref[2 * p] += _dot(dm0, vp, NT)
                dwm_ref[2 * p + 1] += _dot(dm1, vp, NT)
            dvn_rows.append(jnp.concatenate(tiles, axis=1))
            acc = jnp.zeros((CHUNK, 128), F32)
            for part in _split3(dmix[rs, :]):
                acc = acc + _dot(part, sel_ref[...], NN)
            dbs_ref[...] += acc
        dvn = jnp.concatenate(dvn_rows, axis=0)
        dvg_ref[...] += _colsum(dvn * vhat)
        dvb_ref[...] += _colsum(dvn)
        dv = _ln_bwd(dvn, vhat, vrstd, vg_ref[...])
        dzu = du * _gelu_grad(zu)
        dzv = dv * _gelu_grad(zv)
        dza_ref[:, 0:aw] = dzu.astype(BF16)
        dza_ref[:, aw:2 * aw] = dzv.astype(BF16)
        dba_ref[:, 0:aw] += _colsum(dzu)
        dba_ref[:, aw:2 * aw] += _colsum(dzv)

        hcv = hc_ref[...]
        hhat, hrstd = _ln_stats(hcv)
        hn = hhat * cg_ref[...] + cbeta_ref[...]
        sg = _sigmoid(hn)
        dhn = dy_ref[:, aw:2 * aw] * (sg * (1.0 + hn * (1.0 - sg)))
        dcg_ref[...] += _colsum(dhn * hhat)
        dcbeta_ref[...] += _colsum(dhn)
        dhc = _ln_bwd(dhn, hhat, hrstd, cg_ref[...])
        dhc_ref[...] = dhc
        dcb_ref[...] += _colsum(dhc)

    vec = pl.BlockSpec((1, aw), lambda i: (0, 0))
    vec2 = pl.BlockSpec((1, 2 * aw), lambda i: (0, 0))
    w3 = pl.BlockSpec(wm.shape, lambda i: (0, 0, 0))
    sq = pl.BlockSpec((CHUNK, 128), lambda i: (0, 0))
    return pl.pallas_call(
        body, name=name, grid=(s // ROWS,),
        in_specs=[pl.BlockSpec((ROWS, 2 * aw), lambda i: (i, 0)), pl.BlockSpec((ROWS, 2 * aw), lambda i: (i, 0)),
                  pl.BlockSpec((ROWS, aw), lambda i: (i, 0)), vec, vec, w3, w3,
                  pl.BlockSpec((CHUNK, aw), lambda i: (0, 0)), pl.BlockSpec((aw, 128), lambda i: (0, 0)), vec, vec],
        out_specs=[pl.BlockSpec((ROWS, 2 * aw), lambda i: (i, 0)), pl.BlockSpec((ROWS, aw), lambda i: (i, 0)),
                   vec2, vec, vec, w3, sq, vec, vec, vec],
        out_shape=[jax.ShapeDtypeStruct((s, 2 * aw), BF16), jax.ShapeDtypeStruct((s, aw), F32),
                   jax.ShapeDtypeStruct((1, 2 * aw), F32), jax.ShapeDtypeStruct((1, aw), F32),
                   jax.ShapeDtypeStruct((1, aw), F32), jax.ShapeDtypeStruct(wm.shape, F32),
                   jax.ShapeDtypeStruct((CHUNK, 128), F32), jax.ShapeDtypeStruct((1, aw), F32),
                   jax.ShapeDtypeStruct((1, aw), F32), jax.ShapeDtypeStruct((1, aw), F32)],
        compiler_params=_params(("arbitrary",)))(dy, z, hc, vg, vb, wm, wmt, bm, sel, cg, cbeta)


def even_conv_bwd(dhc, z, cw, *, name):
    s, zw = z.shape
    aw = zw // 4
    nblk = s // ROWS
    hb_per = ROWS // HALO

    def body(dc_ref, dn_ref, z_ref, zp_ref, cw_ref, dzb_ref, dbb_ref, dcw_ref, exth_ref, extd_ref):
        i = pl.program_id(0)

        @pl.when(i == 0)
        def _():
            dbb_ref[...] = jnp.zeros_like(dbb_ref)
            dcw_ref[...] = jnp.zeros_like(dcw_ref)

        a = z_ref[:, 0:aw]
        sg = _sigmoid(z_ref[:, aw:2 * aw])
        exth_ref[0:HALO, :] = jnp.where(i > 0, zp_ref[:, 0:aw] * _sigmoid(zp_ref[:, aw:2 * aw]), 0.0)
        exth_ref[HALO:HALO + ROWS, :] = a * sg
        dcur = dc_ref[...]
        extd_ref[0:ROWS, :] = dcur
        extd_ref[ROWS:ROWS + HALO, :] = jnp.where(i < nblk - 1, dn_ref[...], 0.0)
        dhb = jnp.zeros((ROWS, aw), F32)
        for k in range(CONV_K):
            wk = cw_ref[k:k + 1, :]
            dhb = dhb + wk * extd_ref[pl.ds(CONV_K - 1 - k, ROWS), :]
            dcw_ref[k:k + 1, :] += _colsum(dcur * exth_ref[pl.ds(HALO - (CONV_K - 1) + k, ROWS), :])
        da = dhb * sg
        dg = dhb * a * sg * (1.0 - sg)
        dzb_ref[:, 0:aw] = da.astype(BF16)
        dzb_ref[:, aw:2 * aw] = dg.astype(BF16)
        dbb_ref[:, 0:aw] += _colsum(da)
        dbb_ref[:, aw:2 * aw] += _colsum(dg)

    return pl.pallas_call(
        body, name=name, grid=(nblk,),
        in_specs=[pl.BlockSpec((ROWS, aw), lambda i: (i, 0)),
                  pl.BlockSpec((HALO, aw), lambda i: (jnp.minimum((i + 1) * hb_per, nblk * hb_per - 1), 0)),
                  pl.BlockSpec((ROWS, 2 * aw), lambda i: (i, 1)),
                  pl.BlockSpec((HALO, 2 * aw), lambda i: (jnp.maximum(i * hb_per - 1, 0), 1)),
                  pl.BlockSpec((CONV_K, aw), lambda i: (0, 0))],
        out_specs=[pl.BlockSpec((ROWS, 2 * aw), lambda i: (i, 0)), pl.BlockSpec((1, 2 * aw), lambda i: (0, 0)),
                   pl.BlockSpec((CONV_K, aw), lambda i: (0, 0))],
        out_shape=[jax.ShapeDtypeStruct((s, 2 * aw), BF16), jax.ShapeDtypeStruct((1, 2 * aw), F32),
                   jax.ShapeDtypeStruct((CONV_K, aw), F32)],
        scratch_shapes=[pltpu.VMEM((HALO + ROWS, aw), F32), pltpu.VMEM((ROWS + HALO, aw), F32)],
        compiler_params=_params(("arbitrary",)))(dhc, dhc, z, z, cw)


FFN_ROWS = 1024
FFN_CHUNK = 16


def _ffn_tile(f):
    for t in (256, 128):
        if f % t == 0:
            return t
    raise ValueError(f)


def _taps(ext_ref, w, b, r0, rows):
    acc = b
    for k in range(FFN_K):
        acc = acc + w[k] * ext_ref[pl.ds(FHALO - (FFN_K - 1) + k + r0, rows), :]
    return acc


def ffn_mid_fwd(up_g, up_v, cw, cb, *, name):
    s, f = up_g.shape
    tn = _ffn_tile(f)
    nj = f // tn
    ROWS = FFN_ROWS
    per = ROWS // FHALO

    def body(ug_ref, uv_ref, pg_ref, pv_ref, wg_ref, wv_ref, bg_ref, bv_ref, act_ref, eg_ref, ev_ref):
        i = pl.program_id(0)
        for cur_ref, prev_ref, ext_ref in ((ug_ref, pg_ref, eg_ref), (uv_ref, pv_ref, ev_ref)):
            ext_ref[0:FHALO, :] = jnp.where(i > 0, prev_ref[...], 0.0)
            ext_ref[FHALO:FHALO + ROWS, :] = cur_ref[...]
        wg = [wg_ref[k:k + 1, :] for k in range(FFN_K)]
        wv = [wv_ref[k:k + 1, :] for k in range(FFN_K)]
        bg, bv = bg_ref[...], bv_ref[...]
        for r0 in range(0, ROWS, FFN_CHUNK):
            gate = _taps(eg_ref, wg, bg, r0, FFN_CHUNK)
            val = _taps(ev_ref, wv, bv, r0, FFN_CHUNK)
            act_ref[pl.ds(r0, FFN_CHUNK), :] = (gate * _sigmoid(gate) * val).astype(BF16)

    cur = lambda off: pl.BlockSpec((ROWS, tn), lambda i, j: (i, j + off))
    prev = lambda off: pl.BlockSpec((FHALO, tn), lambda i, j: (jnp.maximum(i * per - 1, 0), j + off))
    wsp = lambda off: pl.BlockSpec((FFN_K, tn), lambda i, j: (0, j + off))
    bsp = lambda off: pl.BlockSpec((1, tn), lambda i, j: (0, j + off))
    return pl.pallas_call(
        body, name=name, grid=(s // ROWS, nj),
        in_specs=[cur(0), cur(0), prev(0), prev(0), wsp(0), wsp(nj), bsp(0), bsp(nj)],
        out_specs=pl.BlockSpec((ROWS, tn), lambda i, j: (i, j)),
        out_shape=jax.ShapeDtypeStruct((s, f), BF16),
        scratch_shapes=[pltpu.VMEM((FHALO + ROWS, tn), F32), pltpu.VMEM((FHALO + ROWS, tn), F32)],
        compiler_params=_params(("parallel", "parallel")))(up_g, up_v, up_g, up_v, cw, cw, cb, cb)


def ffn_mid_bwd(dact, up_g, up_v, cw, cb, *, name):
    s, f = up_g.shape
    tn = _ffn_tile(f)
    nj = f // tn
    ROWS = FFN_ROWS
    nblk = s // ROWS
    per = ROWS // FHALO
    ext = ROWS + FHALO

    def body(da_ref, dan_ref, ug_ref, uv_ref, pg_ref, pv_ref, ng_ref, nv_ref, wg_ref, wv_ref, bg_ref, bv_ref,
             dug_ref, duv_ref, dwg_ref, dwv_ref, dbg_ref, dbv_ref, eg_ref, ev_ref, dg_ref, dv_ref):
        i = pl.program_id(1)

        @pl.when(i == 0)
        def _():
            for r in (dwg_ref, dwv_ref, dbg_ref, dbv_ref):
                r[...] = jnp.zeros_like(r)

        for cur_ref, prev_ref, next_ref, ext_ref in ((ug_ref, pg_ref, ng_ref, eg_ref), (uv_ref, pv_ref, nv_ref, ev_ref)):
            ext_ref[0:FHALO, :] = jnp.where(i > 0, prev_ref[...], 0.0)
            ext_ref[FHALO:FHALO + ROWS, :] = cur_ref[...]
            ext_ref[FHALO + ROWS:FHALO + ROWS + FHALO, :] = jnp.where(i < nblk - 1, next_ref[...], 0.0)
        wg = [wg_ref[k:k + 1, :] for k in range(FFN_K)]
        wv = [wv_ref[k:k + 1, :] for k in range(FFN_K)]
        bg, bv = bg_ref[...], bv_ref[...]

        for r0, rows in [(r, FFN_CHUNK) for r in range(0, ROWS, FFN_CHUNK)] + [(ROWS, FHALO)]:
            gate = _taps(eg_ref, wg, bg, r0, rows)
            val = _taps(ev_ref, wv, bv, r0, rows)
            da = da_ref[pl.ds(r0, rows), :] if r0 < ROWS else jnp.where(i < nblk - 1, dan_ref[...], 0.0)
            sg = _sigmoid(gate)
            dg_ref[pl.ds(r0, rows), :] = da * val * (sg * (1.0 + gate * (1.0 - sg)))
            dv_ref[pl.ds(r0, rows), :] = da * (gate * sg)

        def back(d_ref, w, ext_ref, du_ref, dw_ref, db_ref):
            zero = jnp.zeros((FFN_CHUNK, tn), F32)
            acc = [zero] * FFN_K
            accb = zero
            for r0 in range(0, ROWS, FFN_CHUNK):
                d = [d_ref[pl.ds(r0 + FFN_K - 1 - k, FFN_CHUNK), :] for k in range(FFN_K)]
                u = ext_ref[pl.ds(FHALO + r0, FFN_CHUNK), :]
                du = w[0] * d[0]
                for k in range(1, FFN_K):
                    du = du + w[k] * d[k]
                du_ref[pl.ds(r0, FFN_CHUNK), :] = du.astype(BF16)
                acc = [acc[k] + u * d[k] for k in range(FFN_K)]
                accb = accb + d[FFN_K - 1]
            for k in range(FFN_K):
                dw_ref[k:k + 1, :] += _colsum(acc[k])
            db_ref[...] += _colsum(accb)

        back(dg_ref, wg, eg_ref, dug_ref, dwg_ref, dbg_ref)
        back(dv_ref, wv, ev_ref, duv_ref, dwv_ref, dbv_ref)

    cur = lambda off: pl.BlockSpec((ROWS, tn), lambda j, i: (i, j + off))
    prev = lambda off: pl.BlockSpec((FHALO, tn), lambda j, i: (jnp.maximum(i * per - 1, 0), j + off))
    nxt = lambda off: pl.BlockSpec((FHALO, tn), lambda j, i: (jnp.minimum((i + 1) * per, nblk * per - 1), j + off))
    wsp = lambda off: pl.BlockSpec((FFN_K, tn), lambda j, i: (0, j + off))
    bsp = lambda off: pl.BlockSpec((1, tn), lambda j, i: (0, j + off))
    outs = pl.pallas_call(
        body, name=name, grid=(nj, nblk),
        in_specs=[cur(0), nxt(0), cur(0), cur(0), prev(0), prev(0), nxt(0), nxt(0),
                  wsp(0), wsp(nj), bsp(0), bsp(nj)],
        out_specs=[cur(0), cur(0), wsp(0), wsp(0), bsp(0), bsp(0)],
        out_shape=[jax.ShapeDtypeStruct((s, f), BF16), jax.ShapeDtypeStruct((s, f), BF16),
                   jax.ShapeDtypeStruct((FFN_K, f), F32), jax.ShapeDtypeStruct((FFN_K, f), F32),
                   jax.ShapeDtypeStruct((1, f), F32), jax.ShapeDtypeStruct((1, f), F32)],
        scratch_shapes=[pltpu.VMEM((ROWS + 2 * FHALO, tn), F32), pltpu.VMEM((ROWS + 2 * FHALO, tn), F32),
                        pltpu.VMEM((ext, tn), F32), pltpu.VMEM((ext, tn), F32)],
        compiler_params=_params(("parallel", "arbitrary")))(dact, dact, up_g, up_v, up_g, up_v, up_g, up_v,
                                                            cw, cw, cb, cb)
    dug, duv, dwg, dwv, dbg, dbv = outs
    return dug, duv, jnp.concatenate([dwg, dwv], axis=1), jnp.concatenate([dbg, dbv], axis=1)


def rope_tables(s):
    half = HEAD // 2
    lane = jnp.arange(128)
    j = lane % HEAD
    inv = ROPE_THETA ** (-(j % half).astype(F32) / half)
    ang = jnp.arange(s, dtype=F32)[:, None] * inv[None, :]
    sign = jnp.where(j < half, -1.0, 1.0).astype(F32)
    return jnp.cos(ang), jnp.sin(ang) * sign[None, :]


def _swap_halves(x):
    lane = lax.broadcasted_iota(jnp.int32, x.shape, 1)
    return jnp.where((lane % HEAD) < HEAD // 2, pltpu.roll(x, 128 - HEAD // 2, 1), pltpu.roll(x, HEAD // 2, 1))


def rope_fwd(qkv, cos, sin, *, name):
    s, d3 = qkv.shape
    d = d3 // 3
    scale = HEAD ** -0.5

    def body(xq_ref, xk_ref, c_ref, s_ref, q_ref, k_ref):
        c = c_ref[...]
        sn = s_ref[...]
        for t in range(d // 128):
            cs = slice(t * 128, (t + 1) * 128)
            xq = xq_ref[:, cs]
            xk = xk_ref[:, cs]
            q_ref[:, cs] = (xq * c + _swap_halves(xq) * sn) * scale
            k_ref[:, cs] = xk * c + _swap_halves(xk) * sn

    row = pl.BlockSpec((ROWS, d), lambda i: (i, 0))
    tab = pl.BlockSpec((ROWS, 128), lambda i: (i, 0))
    return pl.pallas_call(
        body, name=name, grid=(s // ROWS,),
        in_specs=[row, pl.BlockSpec((ROWS, d), lambda i: (i, 1)), tab, tab],
        out_specs=[row, row],
        out_shape=[jax.ShapeDtypeStruct((s, d), F32)] * 2,
        compiler_params=_params(("parallel",)))(qkv, qkv, cos, sin)


def rope_bwd(dq, dk, dv, cos, sin, *, name):
    s, d = dq.shape
    scale = HEAD ** -0.5

    def body(dq_ref, dk_ref, dv_ref, c_ref, s_ref, o_ref):
        c = c_ref[...]
        sn = s_ref[...]
        for t in range(d // 128):
            cs = slice(t * 128, (t + 1) * 128)
            gq = dq_ref[:, cs] * scale
            gk = dk_ref[:, cs]
            o_ref[:, t * 128:(t + 1) * 128] = (gq * c + _swap_halves(gq * sn)).astype(BF16)
            o_ref[:, d + t * 128:d + (t + 1) * 128] = (gk * c + _swap_halves(gk * sn)).astype(BF16)
        o_ref[:, 2 * d:3 * d] = dv_ref[...].astype(BF16)

    row = pl.BlockSpec((ROWS, d), lambda i: (i, 0))
    tab = pl.BlockSpec((ROWS, 128), lambda i: (i, 0))
    return pl.pallas_call(
        body, name=name, grid=(s // ROWS,),
        in_specs=[row, row, row, tab, tab],
        out_specs=pl.BlockSpec((ROWS, 3 * d), lambda i: (i, 0)),
        out_shape=jax.ShapeDtypeStruct((s, 3 * d), BF16),
        compiler_params=_params(("parallel",)))(dq, dk, dv, cos, sin)


ATT_T = BLOCK * max(DILATIONS)


FWD_QROWS = 128
BWD_QROWS = 64


def _unit_rows(r, j, dil):
    start = r + dil * BLOCK * j
    return pl.ds(start, BLOCK) if dil == 1 else pl.ds(start, BLOCK, stride=dil)


def _units():
    for bi, dil in enumerate(DILATIONS):
        nsub = ATT_T // (BLOCK * dil)
        for r in range(dil):
            for j in range(nsub):
                yield bi, dil, nsub, r, j


def _band(first_block, part, qrows):
    qi = lax.broadcasted_iota(jnp.int32, (qrows, 2 * BLOCK), 0) + part * qrows
    kj = lax.broadcasted_iota(jnp.int32, (qrows, 2 * BLOCK), 1)
    dist = BLOCK + qi - kj
    band = (dist >= 0) & (dist <= BLOCK)
    return band, band & (jnp.logical_not(first_block) | (kj >= BLOCK))


def _col(tile, h):
    return tile[:, h * HEAD:h * HEAD + 1]


def _keys(cur_ref, prev_ref, r, j, dil, nsub):
    cur = cur_ref[_unit_rows(r, j, dil), :]
    prev = cur_ref[_unit_rows(r, j - 1, dil), :] if j > 0 else prev_ref[_unit_rows(r, nsub - 1, dil), :]
    return jnp.concatenate([prev, cur], axis=0).astype(BF16)


def _att_specs(d, col_off=0):
    nt_cols = d // 128
    cur = pl.BlockSpec((ATT_T, 128), lambda n, p: (n, p + col_off))
    prv = pl.BlockSpec((ATT_T, 128), lambda n, p: (jnp.maximum(n - 1, 0), p + col_off))
    return cur, prv


def attn_fwd(q, k, qkv, *, name):
    s, d = q.shape
    nt = s // ATT_T

    def body(q_ref, kc_ref, kp_ref, vc_ref, vp_ref, o_ref, lse_ref, acc_ref, m_ref, l_ref):
        n = pl.program_id(0)
        QROWS = FWD_QROWS
        nparts = BLOCK // QROWS
        bands = [_band(n == 0, part, QROWS) for part in range(nparts)]
        lo = _pair_masks()
        keep = _head_keep(BLOCK)
        nb = len(DILATIONS)
        for bi, dil, nsub, r, j in _units():
            rows = _unit_rows(r, j, dil)
            kw = _keys(kc_ref, kp_ref, r, j, dil, nsub)
            vw = _keys(vc_ref, vp_ref, r, j, dil, nsub)
            qp = q_ref[rows, :].astype(BF16)
            sc2 = _dot(jnp.concatenate([qp * keep[0], qp * keep[1]], axis=0), kw, NT)
            if bi > 0:
                m_old_t = m_ref[rows, :]
            prs, new_m, new_l, alpha = [], [], [], []
            for c in range(2 * nparts):
                h, part = divmod(c, nparts)
                valid = bands[part][0 if j > 0 else 1]
                sc = jnp.where(valid, sc2[c * QROWS:(c + 1) * QROWS], NEG)
                mx = jnp.max(sc, axis=-1, keepdims=True)
                if bi == 0:
                    m_new = mx
                else:
                    m_old = _col(m_old_t[part * QROWS:(part + 1) * QROWS], h)
                    m_new = jnp.maximum(m_old, mx)
                    alpha.append(jnp.exp(m_old - m_new))
                pr = jnp.exp(sc - m_new)
                new_m.append(m_new)
                new_l.append(jnp.sum(pr, axis=-1, keepdims=True))
                prs.append(pr.astype(BF16))
            pv2 = _dot(jnp.concatenate(prs, axis=0), vw, NN)
            tile = lambda cols: jnp.where(lo, jnp.concatenate(cols[:nparts], axis=0),
                                          jnp.concatenate(cols[nparts:], axis=0))
            m_t = tile(new_m)
            l_t = tile(new_l)
            acc_t = jnp.where(lo, pv2[:BLOCK], pv2[BLOCK:])
            if bi > 0:
                a_t = tile(alpha)
                l_t = a_t * l_ref[rows, :] + l_t
                acc_t = a_t * acc_ref[rows, :] + acc_t
            if bi == nb - 1:
                o_ref[rows, :] = acc_t / l_t
                lse_ref[rows, :] = m_t + jnp.log(l_t)
            else:
                acc_ref[rows, :] = acc_t
                m_ref[rows, :] = m_t
                l_ref[rows, :] = l_t

    cur, prv = _att_specs(d)
    vcur, vprv = _att_specs(d, 2 * (d // 128))
    return pl.pallas_call(
        body, name=name, grid=(nt, d // 128), in_specs=[cur, cur, prv, vcur, vprv], out_specs=[cur, cur],
        out_shape=[jax.ShapeDtypeStruct((s, d), F32)] * 2,
        scratch_shapes=[pltpu.VMEM((ATT_T, 128), F32)] * 3,
        compiler_params=_params(("parallel", "parallel")))(q, k, k, qkv, qkv)


def attn_delta(do, o, *, name):
    s, d = do.shape

    def body(do_ref, o_ref, dl_ref):
        lane = lax.broadcasted_iota(jnp.int32, (ROWS, 128), 1)
        lo = lane < HEAD
        for p in range(d // 128):
            cs = slice(p * 128, (p + 1) * 128)
            pr = do_ref[:, cs] * o_ref[:, cs]
            s0 = jnp.sum(jnp.where(lo, pr, 0.0), axis=-1, keepdims=True)
            s1 = jnp.sum(jnp.where(lo, 0.0, pr), axis=-1, keepdims=True)
            dl_ref[:, cs] = jnp.where(lo, s0, s1)

    row = pl.BlockSpec((ROWS, d), lambda i: (i, 0))
    return pl.pallas_call(
        body, name=name, grid=(s // ROWS,), in_specs=[row, row], out_specs=row,
        out_shape=jax.ShapeDtypeStruct((s, d), F32),
        compiler_params=_params(("parallel",)))(do, o)


def attn_dq(q, k, qkv, do, lse, delta, *, name):
    s, d = q.shape
    nt = s // ATT_T

    def body(q_ref, kc_ref, kp_ref, vc_ref, vp_ref, do_ref, l_ref, dl_ref, dq_ref):
        n = pl.program_id(0)
        QROWS = BWD_QROWS
        nparts = BLOCK // QROWS
        bands = [_band(n == 0, part, QROWS) for part in range(nparts)]
        lo = _pair_masks()
        keep = _head_keep(BLOCK)
        for bi, dil, nsub, r, j in _units():
            rows = _unit_rows(r, j, dil)
            kw = _keys(kc_ref, kp_ref, r, j, dil, nsub)
            vw = _keys(vc_ref, vp_ref, r, j, dil, nsub)
            qp = q_ref[rows, :].astype(BF16)
            dop = do_ref[rows, :].astype(BF16)
            lt = l_ref[rows, :]
            dt = dl_ref[rows, :]
            sc2 = _dot(jnp.concatenate([qp * keep[0], qp * keep[1]], axis=0), kw, NT)
            dp2 = _dot(jnp.concatenate([dop * keep[0], dop * keep[1]], axis=0), vw, NT)
            dss = []
            for c in range(2 * nparts):
                h, part = divmod(c, nparts)
                valid = bands[part][0 if j > 0 else 1]
                cr = slice(c * QROWS, (c + 1) * QROWS)
                pr_rows = slice(part * QROWS, (part + 1) * QROWS)
                pr = jnp.where(valid, jnp.exp(sc2[cr] - _col(lt[pr_rows], h)), 0.0)
                dss.append((pr * (dp2[cr] - _col(dt[pr_rows], h))).astype(BF16))
            dq2 = _dot(jnp.concatenate(dss, axis=0), kw, NN)
            dq_t = jnp.where(lo, dq2[:BLOCK], dq2[BLOCK:])
            if bi > 0:
                dq_t = dq_t + dq_ref[rows, :]
            dq_ref[rows, :] = dq_t

    cur, prv = _att_specs(d)
    vcur, vprv = _att_specs(d, 2 * (d // 128))
    return pl.pallas_call(
        body, name=name, grid=(nt, d // 128), in_specs=[cur, cur, prv, vcur, vprv, cur, cur, cur], out_specs=cur,
        out_shape=jax.ShapeDtypeStruct((s, d), F32),
        compiler_params=_params(("parallel", "parallel")))(q, k, k, qkv, qkv, do, lse, delta)


def attn_dkv(q, k, qkv, do, lse, delta, *, name):
    s, d = q.shape
    nt = s // ATT_T

    def body(k_ref, v_ref, qc_ref, qn_ref, doc_ref, don_ref, lc_ref, ln_ref, dc_ref, dn_ref, dk_ref, dv_ref):
        n = pl.program_id(0)
        qi = lax.broadcasted_iota(jnp.int32, (BLOCK, BLOCK), 0)
        kj = lax.broadcasted_iota(jnp.int32, (BLOCK, BLOCK), 1)
        own = kj <= qi
        nxt = kj >= qi
        nxt_edge = nxt & (n < nt - 1)
        keep = _head_keep(BLOCK)
        lo = _pair_masks()
        for bi, dil, nsub, r, j in _units():
            rows = _unit_rows(r, j, dil)
            inner = j + 1 < nsub
            nrows = _unit_rows(r, j + 1, dil) if inner else _unit_rows(r, 0, dil)
            kp = k_ref[rows, :].astype(BF16)
            vp = v_ref[rows, :].astype(BF16)
            far = not inner
            take = lambda c_ref, n_ref, nx: ((n_ref if far else c_ref)[nrows, :] if nx else c_ref[rows, :])
            qs = [take(qc_ref, qn_ref, nx).astype(BF16) for nx in (False, True)]
            dos = [take(doc_ref, don_ref, nx).astype(BF16) for nx in (False, True)]
            lts = [take(lc_ref, ln_ref, nx) for nx in (False, True)]
            dts = [take(dc_ref, dn_ref, nx) for nx in (False, True)]
            valids = (own, nxt if inner else nxt_edge)
            q4 = jnp.concatenate([qs[nx] * keep[h] for h in range(2) for nx in range(2)], axis=0)
            do4 = jnp.concatenate([dos[nx] * keep[h] for h in range(2) for nx in range(2)], axis=0)
            s4 = _dot(q4, kp, NT)
            dp4 = _dot(do4, vp, NT)
            prs, dss = [], []
            for c in range(4):
                h, nx = divmod(c, 2)
                cr = slice(c * BLOCK, (c + 1) * BLOCK)
                pr = jnp.where(valids[nx], jnp.exp(s4[cr] - _col(lts[nx], h)), 0.0)
                prs.append(pr.astype(BF16))
                dss.append((pr * (dp4[cr] - _col(dts[nx], h))).astype(BF16))
            dv_t = _dot(jnp.concatenate(prs, axis=0), do4, TN)
            dk_t = _dot(jnp.concatenate(dss, axis=0), q4, TN)
            if bi > 0:
                dk_t = dk_t + dk_ref[rows, :]
                dv_t = dv_t + dv_ref[rows, :]
            dk_ref[rows, :] = dk_t
            dv_ref[rows, :] = dv_t

    cur = pl.BlockSpec((ATT_T, 128), lambda n, p: (n, p))
    nxt_spec = pl.BlockSpec((ATT_T, 128), lambda n, p: (jnp.minimum(n + 1, nt - 1), p))
    vcur = pl.BlockSpec((ATT_T, 128), lambda n, p: (n, p + 2 * (d // 128)))
    return pl.pallas_call(
        body, name=name, grid=(nt, d // 128),
        in_specs=[cur, vcur, cur, nxt_spec, cur, nxt_spec, cur, nxt_spec, cur, nxt_spec], out_specs=[cur, cur],
        out_shape=[jax.ShapeDtypeStruct((s, d), F32)] * 2,
        compiler_params=_params(("parallel", "parallel")))(k, qkv, q, q, do, do, lse, lse, delta, delta)


def adamw(parts_list, w, m, v, *, name):
    nk = len(parts_list)
    npart, rk, c = parts_list[0].shape
    r = rk * nk
    assert w.shape == (r, c)
    tr = next(t for t in range(rk, 0, -8) if rk % t == 0 and (t * c * 4 <= 1024 * 1024 or t == 8))
    nbk = rk // tr

    def body(*refs):
        p_refs = refs[:nk]
        w_ref, m_ref, v_ref, g_ref, d_ref, nm_ref, nv_ref = refs[nk:]
        i = pl.program_id(0)
        g = None
        for kk, p_ref in enumerate(p_refs):
            gk = p_ref[0].astype(F32)
            for j in range(1, npart):
                gk = gk + p_ref[j].astype(F32)
            g = gk if g is None else jnp.where(i >= kk * nbk, gk, g)
        m2 = B1 * m_ref[...] + (1.0 - B1) * g
        v2 = B2 * v_ref[...] + (1.0 - B2) * (g * g)
        m_hat = m2 / (1.0 - B1 ** STEP)
        v_hat = v2 / (1.0 - B2 ** STEP)
        g_ref[...] = g
        d_ref[...] = -LR * (m_hat / (jnp.sqrt(v_hat) + ADAM_EPS) + WD * w_ref[...])
        nm_ref[...] = m2
        nv_ref[...] = v2

    blk = pl.BlockSpec((tr, c), lambda i: (i, 0))
    pspec = lambda kk: pl.BlockSpec((npart, tr, c), lambda i: (0, jnp.clip(i - kk * nbk, 0, nbk - 1), 0))
    return pl.pallas_call(
        body, name=name, grid=(r // tr,),
        in_specs=[pspec(kk) for kk in range(nk)] + [blk, blk, blk],
        out_specs=[blk] * 4, out_shape=[jax.ShapeDtypeStruct((r, c), F32)] * 4,
        compiler_params=_params(("parallel",)))(*parts_list, w, m, v)


def _my_index():
    return 4 * lax.axis_index("x") + 2 * lax.axis_index("y") + lax.axis_index("c")


def exchange(arrays, scatter, *, name):
    nt = len(arrays)

    def body(*refs):
        ins = refs[:nt]
        outs = refs[nt:2 * nt]
        send_sems, recv_sems, local_sems = refs[2 * nt:]
        x, y, c = lax.axis_index("x"), lax.axis_index("y"), lax.axis_index("c")
        me = 4 * x + 2 * y + c
        copies = []
        for t in range(nt):
            src = ins[t].at[me] if scatter[t] else ins[t]
            cp = pltpu.make_async_copy(src, outs[t].at[me], local_sems.at[t])
            cp.start()
            copies.append(cp)
        remote = []
        for kk in range(1, N_DEV):
            px, py, pc = x ^ (kk >> 2), y ^ ((kk >> 1) & 1), c ^ (kk & 1)
            peer = 4 * px + 2 * py + pc
            for t in range(nt):
                src = ins[t].at[peer] if scatter[t] else ins[t]
                cp = pltpu.make_async_remote_copy(
                    src_ref=src, dst_ref=outs[t].at[me], send_sem=send_sems.at[t, kk], recv_sem=recv_sems.at[t, kk],
                    device_id=(px, py, pc), device_id_type=pl.DeviceIdType.MESH)
                cp.start()
                remote.append(cp)
        for cp in remote:
            cp.wait()
        for cp in copies:
            cp.wait()

    hbm = pl.BlockSpec(memory_space=pl.ANY)
    out_shape = [jax.ShapeDtypeStruct(a.shape if scatter[t] else (N_DEV,) + a.shape, a.dtype)
                 for t, a in enumerate(arrays)]
    return pl.pallas_call(
        body, name=name, in_specs=[hbm] * nt, out_specs=[hbm] * nt, out_shape=out_shape,
        scratch_shapes=[pltpu.SemaphoreType.DMA((nt, N_DEV)), pltpu.SemaphoreType.DMA((nt, N_DEV)),
                        pltpu.SemaphoreType.DMA((nt,))],
        compiler_params=pltpu.CompilerParams(has_side_effects=True))(*arrays)


def _peer_of(kk):
    x, y, c = lax.axis_index("x"), lax.axis_index("y"), lax.axis_index("c")
    return x ^ (kk >> 2), y ^ ((kk >> 1) & 1), c ^ (kk & 1)


def _peer_copy(t, kk, scatter, ins, lands, send_sems, recv_sems):
    px, py, pc = _peer_of(kk)
    me = _my_index()
    src = ins[t].at[4 * px + 2 * py + pc] if scatter[t] else ins[t]
    return pltpu.make_async_remote_copy(
        src_ref=src, dst_ref=lands[t].at[me], send_sem=send_sems.at[t * N_DEV + kk],
        recv_sem=recv_sems.at[t * N_DEV + kk], device_id=(px, py, pc), device_id_type=pl.DeviceIdType.MESH)


def _own_copy(t, scatter, ins, lands, own_sems):
    me = _my_index()
    return pltpu.make_async_copy(ins[t].at[me] if scatter[t] else ins[t], lands[t].at[me], own_sems.at[t])


_HBM = pl.BlockSpec(memory_space=pltpu.HBM)
_SEM = pl.BlockSpec(memory_space=pltpu.SEMAPHORE)
_EFFECT = pltpu.SideEffectType.DATAFLOW_SIDE_EFFECTING


def exchange_start(arrays, scatter, *, name):
    nt = len(arrays)
    land_shapes = [a.shape if scatter[t] else (N_DEV,) + a.shape for t, a in enumerate(arrays)]

    def body(*refs):
        ins, lands = refs[:nt], refs[nt:2 * nt]
        send_sems, recv_sems, own_sems = refs[2 * nt:2 * nt + 3]
        token = refs[-1]
        for kk in range(1, N_DEV):
            for t in range(nt):
                _peer_copy(t, kk, scatter, ins, lands, send_sems, recv_sems).start()
        for t in range(nt):
            _own_copy(t, scatter, ins, lands, own_sems).start()
        token[...] = jnp.zeros_like(token)

    sems = pltpu.SemaphoreType.DMA((nt * N_DEV,))
    outs = pl.pallas_call(
        body, name=name,
        out_shape=(sems, sems, pltpu.SemaphoreType.DMA((nt,)), *[pltpu.HBM(a.shape, a.dtype) for a in arrays],
                   *[pltpu.HBM(shp, a.dtype) for shp, a in zip(land_shapes, arrays)],
                   jax.ShapeDtypeStruct((8, 128), F32)),
        in_specs=[_HBM] * (2 * nt),
        out_specs=(_SEM, _SEM, _SEM, *[_HBM] * (2 * nt), pl.BlockSpec(memory_space=pltpu.VMEM)),
        input_output_aliases={i: 3 + i for i in range(2 * nt)},
        compiler_params=pltpu.CompilerParams(has_side_effects=_EFFECT),
    )(*[pltpu.with_memory_space_constraint(a, pltpu.HBM) for a in arrays],
      *[pltpu.with_memory_space_constraint(lax.empty(shp, a.dtype), pltpu.HBM) for shp, a in zip(land_shapes, arrays)])
    return (outs[:3], outs[3:3 + nt], outs[3 + nt:3 + 2 * nt], scatter), outs[-1]


def exchange_wait(handle, after, *, name):
    sems, thru, lands, scatter = handle
    nt = len(thru)

    def body(*refs):
        ins, lnd = refs[:nt], refs[nt:2 * nt]
        s_sems, r_sems, o_sems = refs[2 * nt:2 * nt + 3]
        for kk in range(1, N_DEV):
            for t in range(nt):
                cp = _peer_copy(t, kk, scatter, ins, lnd, s_sems, r_sems)
                cp.wait_send()
                cp.wait_recv()
        for t in range(nt):
            _own_copy(t, scatter, ins, lnd, o_sems).wait()

    outs = pl.pallas_call(
        body, name=name,
        out_shape=(*[pltpu.HBM(a.shape, a.dtype) for a in thru], *[pltpu.HBM(a.shape, a.dtype) for a in lands]),
        in_specs=[_HBM] * (2 * nt) + [_SEM, _SEM, _SEM, pl.BlockSpec(memory_space=pl.ANY)],
        out_specs=tuple([_HBM] * (2 * nt)),
        input_output_aliases={i: i for i in range(2 * nt)},
        compiler_params=pltpu.CompilerParams(has_side_effects=_EFFECT),
    )(*thru, *lands, *sems, after)
    return outs[nt:]


def _cols_from_shards(g):
    g = jnp.moveaxis(g, 0, -2)
    return g.reshape(g.shape[:-2] + (g.shape[-2] * g.shape[-1],))


def _cols_to_shards(w, nshards=N_DEV):
    w = w.reshape(w.shape[:-1] + (nshards, w.shape[-1] // nshards))
    return jnp.moveaxis(w, -2, 0)


def _half_shards(halves):
    return jnp.concatenate([_cols_to_shards(h[None], N_DEV // 2) for h in halves], axis=0)


def _ffn_fwd(x, g, w_up, cw, cb, w_down, tag):
    h = rms_fwd(x, g, name=f"{tag}_norm")
    up_g = matmul(h, w_up[0], name=f"{tag}_up_g")
    up_v = matmul(h, w_up[1], name=f"{tag}_up_v")
    act = ffn_mid_fwd(up_g, up_v, cw, cb, name=f"{tag}_mid")
    out = matmul(act, w_down, res=x, name=f"{tag}_down")
    return out, (h, up_g, up_v, act)


def _ffn_bwd(dx, dxb, x, saved, g, w_up, cw, cb, w_down, tag):
    h, up_g, up_v, act = saved
    dact = matmul(dxb, w_down, tb=True, name=f"{tag}_ddown")
    d_w_down = matmul_ta(act, dxb, name=f"{tag}_gdown")
    dug, duv, dcw, dcb = ffn_mid_bwd(dact, up_g, up_v, cw, cb, name=f"{tag}_dmid")
    d_w_up = (matmul_ta(h, dug, name=f"{tag}_gup_g"), matmul_ta(h, duv, name=f"{tag}_gup_v"))
    dh = matmul(dug, w_up[0], tb=True, name=f"{tag}_dup_g")
    dh = matmul(duv, w_up[1], tb=True, res=dh, name=f"{tag}_dup_v")
    dx2, dxb2, dg = rms_bwd(dh, x, g, dx, name=f"{tag}_dnorm")
    return dx2, dxb2, dict(norm_g=dg, w_up=d_w_up, conv_w=dcw, conv_b=dcb, w_down=d_w_down)


def local_step(x0, tgt, a, weights, grads_out):
    s, d = x0.shape
    aw = a['even_v_ln_g'].shape[-1]
    causal = jnp.tril(jnp.ones((CHUNK, CHUNK), dtype=bool))
    wm = jnp.where(causal, a['even_w_s'][0], 0.0).astype(BF16)
    wmt = jnp.swapaxes(wm, 1, 2)
    bm = jnp.repeat(a['even_b_s'][0].T, HEAD, axis=1)
    sel = (jnp.arange(aw)[:, None] // HEAD == jnp.arange(128)[None, :]).astype(BF16)
    cos, sin = rope_tables(s)
    ffn_g, ffn_cb = a['ffn_norm_g'], a['ffn_conv_b']

    w0 = weights(0, None)
    w_in, conv_w, odd_g, ffn_cw = w0['w_in'], w0['conv_w'], w0['odd_g'], w0['ffn_cw']
    h0 = rms_fwd(x0, w0['even_g'], name="even_norm")
    z = matmul(h0, w_in, bias=a['even_b_in'], name="even_in")
    ycat, hc = even_mid_fwd(z, a['even_v_ln_g'], a['even_v_ln_b'], wm, bm, conv_w, a['even_conv_b'],
                            a['even_conv_ln_g'], a['even_conv_ln_b'], name="even_mid")
    w1 = weights(1, ycat)
    x1 = matmul(ycat, w1['w_out'], res=x0, name="even_out")
    x2, ffn0 = _ffn_fwd(x1, ffn_g[0:1], w1['w_up'], ffn_cw[0], ffn_cb[0:1], w1['w_down'], "ffn0")
    h2 = rms_fwd(x2, odd_g, name="odd_norm")
    w2 = weights(2, h2)
    qkv = matmul(h2, w2['w_qkv'], name="odd_qkv")
    q, k = rope_fwd(qkv, cos, sin, name="rope")
    o, lse = attn_fwd(q, k, qkv, name="attn_fwd")
    x3 = matmul(o, w2['w_o'], res=x2, name="odd_out")
    x4, ffn1 = _ffn_fwd(x3, ffn_g[1:2], w2['w_up'], ffn_cw[1], ffn_cb[1:2], w2['w_down'], "ffn1")
    loss_t, dx, dxb, d_final_g = final_loss_bwd(x4, a['final_norm_g'].reshape(1, -1), tgt, name="final_loss")

    dx, dxb, g1 = _ffn_bwd(dx, dxb, x3, ffn1, ffn_g[1:2], w2['w_up'], ffn_cw[1], ffn_cb[1:2], w2['w_down'], "ffn1")
    dep = grads_out(0, dict(w_up=g1['w_up'], w_down=g1['w_down']))
    do = matmul(dxb, w2['w_o'], tb=True, dep=dep, name="odd_dout")
    d_w_o = matmul_ta(o, dxb, name="odd_gout")
    delta = attn_delta(do, o, name="attn_delta")
    dq = attn_dq(q, k, qkv, do, lse, delta, name="attn_dq")
    dk, dv = attn_dkv(q, k, qkv, do, lse, delta, name="attn_dkv")
    dqkv = rope_bwd(dq, dk, dv, cos, sin, name="rope_bwd")
    d_w_qkv = matmul_ta(h2, dqkv, name="odd_gqkv")
    dep = grads_out(1, dict(w_qkv=d_w_qkv, w_o=d_w_o))
    dh2 = matmul(dqkv, w2['w_qkv'], tb=True, dep=dep, name="odd_dqkv")
    dx, dxb, d_odd_g = rms_bwd(dh2, x2, odd_g, dx, name="odd_dnorm")
    dx, dxb, g0 = _ffn_bwd(dx, dxb, x1, ffn0, ffn_g[0:1], w1['w_up'], ffn_cw[0], ffn_cb[0:1], w1['w_down'], "ffn0")
    dep = grads_out(2, dict(w_up=g0['w_up'], w_down=g0['w_down']))
    d_w_out = matmul_ta(ycat, dxb, dep=dep, name="even_gout")
    dep = grads_out(3, dict(w_out=d_w_out))
    dycat = matmul(dxb, w1['w_out'], tb=True, dep=dep, name="even_dout")
    (dza, dhc, dba, dvg, dvb, dwm, dbs, dcg, dcbeta, dcb) = even_mid_bwd_rows(
        dycat, z, hc, a['even_v_ln_g'], a['even_v_ln_b'], wm, wmt, bm, sel, a['even_conv_ln_g'],
        a['even_conv_ln_b'], name="even_dmid_rows")
    dzb, dbb, dcw = even_conv_bwd(dhc, z, conv_w, name="even_dmid_conv")
    d_w_in = (matmul_ta(h0, dza, name="even_gin_a"), matmul_ta(h0, dzb, name="even_gin_b"))
    dh0 = matmul(dza, w_in, tb=True, bk=0, name="even_din_a")
    dh0 = matmul(dzb, w_in, tb=True, bk=1, res=dh0, name="even_din_b")
    grad_x, _, d_even_g = rms_bwd(dh0, x0, w0['even_g'], dx, name="even_dnorm")

    nh = a['even_w_s'].shape[1]
    small_grads = {
        'even_norm_g': d_even_g, 'even_b_in': jnp.concatenate([dba, dbb], axis=1), 'even_v_ln_g': dvg,
        'even_v_ln_b': dvb, 'even_w_s': jnp.where(causal, dwm, 0.0)[None], 'even_b_s': dbs[:, :nh].T[None],
        'even_conv_w': dcw[None], 'even_conv_b': dcb, 'even_conv_ln_g': dcg, 'even_conv_ln_b': dcbeta,
        'odd_norm_g': d_odd_g, 'ffn_norm_g': jnp.concatenate([g0['norm_g'], g1['norm_g']], axis=0),
        'ffn_conv_w': jnp.stack([g0['conv_w'], g1['conv_w']]),
        'ffn_conv_b': jnp.concatenate([g0['conv_b'], g1['conv_b']], axis=0),
        'final_norm_g': d_final_g.reshape(-1),
    }
    grads_out(4, dict(w_in=d_w_in, small=small_grads))
    return loss_t, grad_x, small_grads


BIG = ['even_w_in', 'even_w_out', 'odd_w_qkv', 'odd_w_o', 'ffn_w_up', 'ffn_w_down']


def _as_tiles(flat, dtype=F32):
    return jnp.pad(flat, (0, (-flat.size) % 2048)).reshape(-1, 128).astype(dtype)


def kernel(*args):
    a = dict(zip(NAMES + ['loss_target'] + ['m_' + n for n in WEIGHTS] + ['v_' + n for n in WEIGHTS], args))
    x0 = a['x'][0]
    tgt = a['loss_target'][0]
    s, d = x0.shape
    me = _my_index()
    bf = lambda t: t.astype(BF16)

    small_local = _as_tiles(jnp.concatenate([a['even_conv_w'].reshape(-1), a['odd_norm_g'].reshape(-1),
                                             a['ffn_conv_w'].reshape(-1)]))
    stage_arrays = [
        [bf(a['even_w_in']), small_local],
        [bf(a['even_w_out']), bf(a['ffn_w_up'][0:1]), bf(a['ffn_w_down'][0:1])],
        [bf(a['odd_w_qkv']), bf(a['odd_w_o']), bf(a['ffn_w_up'][1:2]), bf(a['ffn_w_down'][1:2])],
    ]
    started = [exchange_start(arrs, [False] * len(arrs), name=f"gather{i}_start") for i, arrs in enumerate(stage_arrays)]
    order = sum(tok[0, 0] for _, tok in started)

    def weights(stage, after):
        handle, tok = started[stage]
        full = exchange_wait(handle, tok if after is None else after, name=f"gather{stage}_wait")
        rows = lambda g: jnp.moveaxis(g, 0, 1).reshape(-1, d)
        halves = lambda g: (_cols_from_shards(g[:N_DEV // 2])[0], _cols_from_shards(g[N_DEV // 2:])[0])
        if stage == 0:
            gs = full[1].reshape(N_DEV, -1)
            n_cw, n_og, n_fw = a['even_conv_w'].size, a['odd_norm_g'].size, a['ffn_conv_w'].size
            return dict(
                w_in=_cols_from_shards(full[0])[0], even_g=a['even_norm_g'] + order,
                conv_w=_cols_from_shards(gs[:, :n_cw].reshape((N_DEV,) + a['even_conv_w'].shape))[0],
                odd_g=gs[:, n_cw:n_cw + n_og].reshape(1, -1),
                ffn_cw=_cols_from_shards(gs[:, n_cw + n_og:n_cw + n_og + n_fw].reshape((N_DEV,) + a['ffn_conv_w'].shape)))
        if stage == 1:
            return dict(w_out=rows(full[0]), w_up=halves(full[1]), w_down=rows(full[2]))
        return dict(w_qkv=_cols_from_shards(full[0])[0], w_o=rows(full[1]), w_up=halves(full[2]), w_down=rows(full[3]))

    sent = {}

    def grads_out(stage, g):
        to_rows = lambda w: w.reshape(N_DEV, 1, -1, d)
        if stage in (0, 2):
            pieces, scatter = [_half_shards(g['w_up']), to_rows(g['w_down'])], [True, True]
        elif stage == 1:
            pieces, scatter = [_cols_to_shards(g['w_qkv'][None]), to_rows(g['w_o'])], [True, True]
        elif stage == 3:
            pieces, scatter = [to_rows(g['w_out'])], [True]
        else:
            small = jnp.concatenate([g['small'][n].reshape(-1) for n in g['small']])
            pieces, scatter = [_half_shards(g['w_in']), _as_tiles(small, BF16)], [True, False]
        sent[stage], tok = exchange_start(pieces, scatter, name=f"grads{stage}_start")
        return tok

    loss_t, grad_x, small_grads = local_step(x0, tgt, a, weights, grads_out)
    loss = lax.psum(loss_t[0, 0], ("x", "y", "c"))
    received = {stage: exchange_wait(handle, grad_x, name=f"grads{stage}_wait") for stage, handle in sent.items()}

    results = {}
    big_parts = {'even_w_in': [received[4][0]], 'even_w_out': [received[3][0]], 'odd_w_qkv': [received[1][0]],
                 'odd_w_o': [received[1][1]], 'ffn_w_up': [received[2][0], received[0][0]],
                 'ffn_w_down': [received[2][1], received[0][1]]}
    for n in BIG:
        shp = a[n].shape
        flat = lambda t: t.reshape(-1, shp[-1])
        outs = adamw([p.reshape(N_DEV, -1, shp[-1]) for p in big_parts[n]], flat(a[n]), flat(a['m_' + n]),
                     flat(a['v_' + n]), name=f"adamw_{n}")
        results[n] = [t.reshape(shp) for t in outs]

    small_names = list(small_grads)
    n_small = sum(small_grads[n].size for n in small_names)
    rs = received[4][1].reshape(N_DEV, -1)[:, :n_small]
    parts, offs = [], 0
    for n in small_names:
        full = small_grads[n].shape
        piece = rs[:, offs:offs + small_grads[n].size].reshape((N_DEV,) + full)
        offs += small_grads[n].size
        shp = a[n].shape
        if shp != full:
            width = shp[-1]
            piece = lax.dynamic_slice_in_dim(piece, me * width, width, axis=piece.ndim - 1)
        parts.append(piece.reshape(N_DEV, -1))
    parts = jnp.concatenate(parts, axis=1)
    pad = (-parts.shape[1]) % 2048
    cat = lambda pre: _as_tiles(jnp.concatenate([a[pre + n].reshape(-1) for n in small_names]))
    outs = adamw([jnp.pad(parts, ((0, 0), (0, pad))).reshape(N_DEV, -1, 128)], cat(''), cat('m_'), cat('v_'),
                 name="adamw_small")
    offs = 0
    for n in small_names:
        size = a[n].size
        results[n] = [t.reshape(-1)[offs:offs + size].reshape(a[n].shape) for t in outs]
        offs += size

    out = [loss, grad_x[None]]
    for i in range(4):
        out += [results[n][i] for n in WEIGHTS]
    return tuple(out)
```

```python
import functools
import math

import jax
import jax.numpy as jnp
from jax import lax
from jax.experimental import pallas as pl
from jax.experimental.pallas import tpu as pltpu

F32 = jnp.float32
BF16 = jnp.bfloat16

N_DEV = 8
EPS = 1e-6
NEG = -1e30
HEAD = 64
CHUNK = 128
BLOCK = 128
CONV_K = 31
FFN_K = 3
DILATIONS = (1, 4, 16)
ROPE_THETA = 10000.0
LR, B1, B2, ADAM_EPS, WD, STEP = 0.001, 0.9, 0.999, 1e-08, 0.01, 10

VMEM_LIMIT = 56 * 1024 * 1024
VMEM_BUDGET = 32 * 1024 * 1024
ROWS = 512
HALO = 32
FHALO = 8

NAMES = ['x', 'even_norm_g', 'even_w_in', 'even_b_in', 'even_v_ln_g', 'even_v_ln_b', 'even_w_s', 'even_b_s',
         'even_conv_w', 'even_conv_b', 'even_conv_ln_g', 'even_conv_ln_b', 'even_w_out', 'odd_norm_g',
         'odd_w_qkv', 'odd_w_o', 'ffn_norm_g', 'ffn_w_up', 'ffn_conv_w', 'ffn_conv_b', 'ffn_w_down',
         'final_norm_g']
WEIGHTS = NAMES[1:]


def _params(sem=None):
    return pltpu.CompilerParams(dimension_semantics=sem, vmem_limit_bytes=VMEM_LIMIT)


def _sigmoid(x):
    return 1.0 / (1.0 + jnp.exp(-x))


def _gelu(x):
    c = math.sqrt(2.0 / math.pi)
    return 0.5 * x * (1.0 + jnp.tanh(c * (x + 0.044715 * x * x * x)))


def _gelu_grad(x):
    c = math.sqrt(2.0 / math.pi)
    t = jnp.tanh(c * (x + 0.044715 * x * x * x))
    return 0.5 * (1.0 + t) + 0.5 * x * (1.0 - t * t) * c * (1.0 + 3.0 * 0.044715 * x * x)


def _ln_stats(x):
    mu = jnp.mean(x, axis=-1, keepdims=True)
    xc = x - mu
    rstd = lax.rsqrt(jnp.mean(xc * xc, axis=-1, keepdims=True) + EPS)
    return xc * rstd, rstd


def _ln_bwd(dy, xhat, rstd, g):
    dxh = dy * g
    return rstd * (dxh - jnp.mean(dxh, axis=-1, keepdims=True) - xhat * jnp.mean(dxh * xhat, axis=-1, keepdims=True))


def _colsum(x):
    return jnp.sum(x, axis=0, keepdims=True)


def _split3(x):
    hi = x.astype(BF16)
    r = x - hi.astype(F32)
    mid = r.astype(BF16)
    lo = (r - mid.astype(F32)).astype(BF16)
    return hi, mid, lo


def _dot(a, b, dims):
    return lax.dot_general(a, b, (dims, ((), ())), preferred_element_type=F32)


NN = ((1,), (0,))
NT = ((1,), (1,))
TN = ((0,), (0,))


def _divisors(n, cands):
    return [c for c in cands if c <= n and n % c == 0]


def _pick_tiles(m, n, k, a_bytes, b_bytes, o_bytes, extra_bytes):
    best = None
    for tm in _divisors(m, (1024, 512, 256, 128)):
        for tn in _divisors(n, (1408, 1024, 768, 704, 512, 384, 256, 128)):
            if tn % 128:
                continue
            need = 2 * (tm * k * a_bytes + k * tn * b_bytes + tm * tn * (o_bytes + extra_bytes)) + tm * tn * 4
            if need <= VMEM_BUDGET and (best is None or tm * tn > best[0] * best[1]):
                best = (tm, tn)
    assert best is not None, (m, n, k)
    return best


def matmul(a, b, *, tb=False, bk=0, bias=None, res=None, dep=None, out_dtype=F32, name):
    m, k = a.shape
    n = b.shape[0] if tb else b.shape[1]
    assert b.shape[1] % k == 0 if tb else (b.shape[0] == k and bk == 0)
    tm, tn = _pick_tiles(m, n, k, a.dtype.itemsize, b.dtype.itemsize, jnp.dtype(out_dtype).itemsize,
                         4 if res is not None else 0)

    def body(*refs):
        a_ref, b_ref = refs[:2]
        o_ref = refs[-1]
        acc = _dot(a_ref[...].astype(BF16), b_ref[...].astype(BF16), NT if tb else NN)
        pos = 2
        if bias is not None:
            acc = acc + refs[pos][...]
            pos += 1
        if res is not None:
            acc = acc + refs[pos][...]
        o_ref[...] = acc.astype(out_dtype)

    in_specs = [pl.BlockSpec((tm, k), lambda i, j: (i, 0)),
                pl.BlockSpec((tn, k), lambda i, j: (j, bk)) if tb else pl.BlockSpec((k, tn), lambda i, j: (0, j))]
    args = [a, b]
    if bias is not None:
        in_specs.append(pl.BlockSpec((1, tn), lambda i, j: (0, j)))
        args.append(bias)
    if res is not None:
        in_specs.append(pl.BlockSpec((tm, tn), lambda i, j: (i, j)))
        args.append(res)
    if dep is not None:
        in_specs.append(pl.BlockSpec(memory_space=pl.ANY))
        args.append(dep)
    return pl.pallas_call(
        body, name=name, grid=(m // tm, n // tn), in_specs=in_specs,
        out_specs=pl.BlockSpec((tm, tn), lambda i, j: (i, j)),
        out_shape=jax.ShapeDtypeStruct((m, n), out_dtype),
        compiler_params=_params(("parallel", "parallel")))(*args)


def matmul_ta(a, b, *, dep=None, out_dtype=BF16, name):
    s, m = a.shape
    n = b.shape[1]
    assert b.shape[0] == s
    best = None
    for tm in _divisors(m, (512, 256, 128)):
        for tn in _divisors(n, (1024, 512, 384, 256, 128)):
            need = 2 * (s * tm * a.dtype.itemsize + s * tn * b.dtype.itemsize + tm * tn * 2) + tm * tn * 4
            if need <= VMEM_BUDGET and (best is None or tm * tn > best[0] * best[1]):
                best = (tm, tn)
    tm, tn = best

    def body(*refs):
        a_ref, b_ref, o_ref = refs[0], refs[1], refs[-1]
        o_ref[...] = _dot(a_ref[...].astype(BF16), b_ref[...].astype(BF16), TN).astype(out_dtype)

    in_specs = [pl.BlockSpec((s, tm), lambda i, j: (0, i)), pl.BlockSpec((s, tn), lambda i, j: (0, j))]
    args = [a, b]
    if dep is not None:
        in_specs.append(pl.BlockSpec(memory_space=pl.ANY))
        args.append(dep)
    return pl.pallas_call(
        body, name=name, grid=(m // tm, n // tn), in_specs=in_specs,
        out_specs=pl.BlockSpec((tm, tn), lambda i, j: (i, j)),
        out_shape=jax.ShapeDtypeStruct((m, n), out_dtype),
        compiler_params=_params(("parallel", "parallel")))(*args)


def rms_fwd(x, g, *, name):
    s, d = x.shape

    def body(x_ref, g_ref, h_ref):
        xv = x_ref[...]
        r = lax.rsqrt(jnp.mean(xv * xv, axis=-1, keepdims=True) + EPS)
        h_ref[...] = (xv * r * g_ref[...]).astype(BF16)

    return pl.pallas_call(
        body, name=name, grid=(s // ROWS,),
        in_specs=[pl.BlockSpec((ROWS, d), lambda i: (i, 0)), pl.BlockSpec((1, d), lambda i: (0, 0))],
        out_specs=pl.BlockSpec((ROWS, d), lambda i: (i, 0)),
        out_shape=jax.ShapeDtypeStruct((s, d), BF16),
        compiler_params=_params(("parallel",)))(x, g)


def rms_bwd(dh, x, g, dres, *, name):
    s, d = x.shape

    def body(dh_ref, x_ref, g_ref, dres_ref, dx_ref, dxb_ref, dg_ref):
        xv = x_ref[...]
        r = lax.rsqrt(jnp.mean(xv * xv, axis=-1, keepdims=True) + EPS)
        xhat = xv * r
        dhv = dh_ref[...]
        dxh = dhv * g_ref[...]
        dx = dres_ref[...] + r * (dxh - xhat * jnp.mean(dxh * xhat, axis=-1, keepdims=True))
        dx_ref[...] = dx
        dxb_ref[...] = dx.astype(BF16)

        @pl.when(pl.program_id(0) == 0)
        def _():
            dg_ref[...] = jnp.zeros_like(dg_ref)
        dg_ref[...] += _colsum(dhv * xhat)

    row = pl.BlockSpec((ROWS, d), lambda i: (i, 0))
    vec = pl.BlockSpec((1, d), lambda i: (0, 0))
    return pl.pallas_call(
        body, name=name, grid=(s // ROWS,),
        in_specs=[row, row, vec, row], out_specs=[row, row, vec],
        out_shape=[jax.ShapeDtypeStruct((s, d), F32), jax.ShapeDtypeStruct((s, d), BF16),
                   jax.ShapeDtypeStruct((1, d), F32)],
        compiler_params=_params(("arbitrary",)))(dh, x, g, dres)


def final_loss_bwd(x, g, tgt, *, name):
    s, d = x.shape

    def body(x_ref, g_ref, t_ref, loss_ref, dx_ref, dxb_ref, dg_ref):
        xv = x_ref[...]
        gv = g_ref[...]
        r = lax.rsqrt(jnp.mean(xv * xv, axis=-1, keepdims=True) + EPS)
        xhat = xv * r
        e = xhat * gv - t_ref[...]
        dy = e * (1.0 / d)
        dxh = dy * gv
        dx = r * (dxh - xhat * jnp.mean(dxh * xhat, axis=-1, keepdims=True))
        dx_ref[...] = dx
        dxb_ref[...] = dx.astype(BF16)

        @pl.when(pl.program_id(0) == 0)
        def _():
            dg_ref[...] = jnp.zeros_like(dg_ref)
            loss_ref[...] = jnp.zeros_like(loss_ref)
        dg_ref[...] += _colsum(dy * xhat)
        loss_ref[...] += 0.5 * jnp.sum(jnp.mean(e * e, axis=-1, keepdims=True))

    row = pl.BlockSpec((ROWS, d), lambda i: (i, 0))
    vec = pl.BlockSpec((1, d), lambda i: (0, 0))
    one = pl.BlockSpec((8, 128), lambda i: (0, 0))
    return pl.pallas_call(
        body, name=name, grid=(s // ROWS,),
        in_specs=[row, vec, row], out_specs=[one, row, row, vec],
        out_shape=[jax.ShapeDtypeStruct((8, 128), F32), jax.ShapeDtypeStruct((s, d), F32),
                   jax.ShapeDtypeStruct((s, d), BF16), jax.ShapeDtypeStruct((1, d), F32)],
        compiler_params=_params(("arbitrary",)))(x, g, tgt)


def _pair_masks():
    lane = lax.broadcasted_iota(jnp.int32, (CHUNK, 128), 1)
    return lane < HEAD


def _head_keep(rows):
    lane = lax.broadcasted_iota(jnp.int32, (rows, 128), 1)
    first = jnp.where(lane < HEAD, 1.0, 0.0)
    return first.astype(BF16), (1.0 - first).astype(BF16)


def _gating_mixed(vn_b, wm_ref, lo):
    rows, aw = vn_b.shape
    out = []
    for c in range(rows // CHUNK):
        tiles = []
        for p in range(aw // 128):
            vp = vn_b[c * CHUNK:(c + 1) * CHUNK, p * 128:(p + 1) * 128]
            r0 = _dot(wm_ref[2 * p], vp, NN)
            r1 = _dot(wm_ref[2 * p + 1], vp, NN)
            tiles.append(jnp.where(lo, r0, r1))
        out.append(jnp.concatenate(tiles, axis=1))
    return jnp.concatenate(out, axis=0)


def even_mid_fwd(z, vg, vb, wm, bm, cw, cb, cg, cbeta, *, name):
    s, zw = z.shape
    aw = zw // 4
    nblk = s // ROWS

    def body(z_ref, zp_ref, vg_ref, vb_ref, wm_ref, bm_ref, cw_ref, cb_ref, cg_ref, cbeta_ref,
             y_ref, hc_ref, ext_ref):
        i = pl.program_id(0)
        lo = _pair_masks()
        u = _gelu(z_ref[:, 0:aw])
        v = _gelu(z_ref[:, aw:2 * aw])
        vhat, _ = _ln_stats(v)
        vn = (vhat * vg_ref[...] + vb_ref[...]).astype(BF16)
        mixed = _gating_mixed(vn, wm_ref, lo)
        bias = jnp.concatenate([bm_ref[...]] * (ROWS // CHUNK), axis=0)
        y_ref[:, 0:aw] = (u * (mixed + bias)).astype(BF16)

        hb = z_ref[:, 2 * aw:3 * aw] * _sigmoid(z_ref[:, 3 * aw:4 * aw])
        hbp = zp_ref[:, 0:aw] * _sigmoid(zp_ref[:, aw:2 * aw])
        ext_ref[0:HALO, :] = jnp.where(i > 0, hbp, 0.0)
        ext_ref[HALO:HALO + ROWS, :] = hb
        acc = jnp.zeros((ROWS, aw), F32) + cb_ref[...]
        for k in range(CONV_K):
            acc = acc + cw_ref[k:k + 1, :] * ext_ref[pl.ds(HALO - (CONV_K - 1) + k, ROWS), :]
        hc_ref[...] = acc
        hhat, _ = _ln_stats(acc)
        hn = hhat * cg_ref[...] + cbeta_ref[...]
        y_ref[:, aw:2 * aw] = (hn * _sigmoid(hn)).astype(BF16)

    hb_per = ROWS // HALO
    vec = pl.BlockSpec((1, aw), lambda i: (0, 0))
    return pl.pallas_call(
        body, name=name, grid=(nblk,),
        in_specs=[pl.BlockSpec((ROWS, zw), lambda i: (i, 0)),
                  pl.BlockSpec((HALO, 2 * aw), lambda i: (jnp.maximum(i * hb_per - 1, 0), 1)),
                  vec, vec,
                  pl.BlockSpec(wm.shape, lambda i: (0, 0, 0)),
                  pl.BlockSpec((CHUNK, aw), lambda i: (0, 0)),
                  pl.BlockSpec((CONV_K, aw), lambda i: (0, 0)), vec, vec, vec],
        out_specs=[pl.BlockSpec((ROWS, 2 * aw), lambda i: (i, 0)), pl.BlockSpec((ROWS, aw), lambda i: (i, 0))],
        out_shape=[jax.ShapeDtypeStruct((s, 2 * aw), BF16), jax.ShapeDtypeStruct((s, aw), F32)],
        scratch_shapes=[pltpu.VMEM((HALO + ROWS, aw), F32)],
        compiler_params=_params(("parallel",)))(z, z, vg, vb, wm, bm, cw, cb, cg, cbeta)


def even_mid_bwd_rows(dy, z, hc, vg, vb, wm, wmt, bm, sel, cg, cbeta, *, name):
    s, zw = z.shape
    aw = zw // 4
    nh = wm.shape[0]

    def body(dy_ref, z_ref, hc_ref, vg_ref, vb_ref, wm_ref, wmt_ref, bm_ref, sel_ref, cg_ref, cbeta_ref,
             dza_ref, dhc_ref, dba_ref, dvg_ref, dvb_ref, dwm_ref, dbs_ref, dcg_ref, dcbeta_ref, dcb_ref):
        @pl.when(pl.program_id(0) == 0)
        def _():
            for r in (dba_ref, dvg_ref, dvb_ref, dwm_ref, dbs_ref, dcg_ref, dcbeta_ref, dcb_ref):
                r[...] = jnp.zeros_like(r)

        lo = _pair_masks()
        keep = _head_keep(CHUNK)
        zu = z_ref[:, 0:aw]
        zv = z_ref[:, aw:2 * aw]
        u = _gelu(zu)
        v = _gelu(zv)
        vhat, vrstd = _ln_stats(v)
        vn = (vhat * vg_ref[...] + vb_ref[...]).astype(BF16)
        mixed = _gating_mixed(vn, wm_ref, lo)
        bias = jnp.concatenate([bm_ref[...]] * (ROWS // CHUNK), axis=0)
        dya = dy_ref[:, 0:aw]
        du = dya * (mixed + bias)
        dmix = dya * u
        dmix_b = dmix.astype(BF16)

        dvn_rows = []
        for c in range(ROWS // CHUNK):
            rs = slice(c * CHUNK, (c + 1) * CHUNK)
            tiles = []
            for p in range(aw // 128):
                cs = slice(p * 128, (p + 1) * 128)
                dm = dmix_b[rs, cs]
                dm0 = dm * keep[0]
                dm1 = dm * keep[1]
                vp = vn[rs, cs]
                tiles.append(_dot(wmt_ref[2 * p], dm0, NN) + _dot(wmt_ref[2 * p + 1], dm1, NN))
                dwm_ref[2 * p] += _dot(dm0, vp, NT)
                dwm_ref[2 * p + 1] += _dot(dm1, vp, NT)
            dvn_rows.append(jnp.concatenate(tiles, axis=1))
            acc = jnp.zeros((CHUNK, 128), F32)
            for part in _split3(dmix[rs, :]):
                acc = acc + _dot(part, sel_ref[...], NN)
            dbs_ref[...] += acc
        dvn = jnp.concatenate(dvn_rows, axis=0)
        dvg_ref[...] += _colsum(dvn * vhat)
        dvb_ref[...] += _colsum(dvn)
        dv = _ln_bwd(dvn, vhat, vrstd, vg_ref[...])
        dzu = du * _gelu_grad(zu)
        dzv = dv * _gelu_grad(zv)
        dza_ref[:, 0:aw] = dzu.astype(BF16)
        dza_ref[:, aw:2 * aw] = dzv.astype(BF16)
        dba_ref[:, 0:aw] += _colsum(dzu)
        dba_ref[:, aw:2 * aw] += _colsum(dzv)

        hcv = hc_ref[...]
        hhat, hrstd = _ln_stats(hcv)
        hn = hhat * cg_ref[...] + cbeta_ref[...]
        sg = _sigmoid(hn)
        dhn = dy_ref[:, aw:2 * aw] * (sg * (1.0 + hn * (1.0 - sg)))
        dcg_ref[...] += _colsum(dhn * hhat)
        dcbeta_ref[...] += _colsum(dhn)
        dhc = _ln_bwd(dhn, hhat, hrstd, cg_ref[...])
        dhc_ref[...] = dhc
        dcb_ref[...] += _colsum(dhc)

    vec = pl.BlockSpec((1, aw), lambda i: (0, 0))
    vec2 = pl.BlockSpec((1, 2 * aw), lambda i: (0, 0))
    w3 = pl.BlockSpec(wm.shape, lambda i: (0, 0, 0))
    sq = pl.BlockSpec((CHUNK, 128), lambda i: (0, 0))
    return pl.pallas_call(
        body, name=name, grid=(s // ROWS,),
        in_specs=[pl.BlockSpec((ROWS, 2 * aw), lambda i: (i, 0)), pl.BlockSpec((ROWS, 2 * aw), lambda i: (i, 0)),
                  pl.BlockSpec((ROWS, aw), lambda i: (i, 0)), vec, vec, w3, w3,
                  pl.BlockSpec((CHUNK, aw), lambda i: (0, 0)), pl.BlockSpec((aw, 128), lambda i: (0, 0)), vec, vec],
        out_specs=[pl.BlockSpec((ROWS, 2 * aw), lambda i: (i, 0)), pl.BlockSpec((ROWS, aw), lambda i: (i, 0)),
                   vec2, vec, vec, w3, sq, vec, vec, vec],
        out_shape=[jax.ShapeDtypeStruct((s, 2 * aw), BF16), jax.ShapeDtypeStruct((s, aw), F32),
                   jax.ShapeDtypeStruct((1, 2 * aw), F32), jax.ShapeDtypeStruct((1, aw), F32),
                   jax.ShapeDtypeStruct((1, aw), F32), jax.ShapeDtypeStruct(wm.shape, F32),
                   jax.ShapeDtypeStruct((CHUNK, 128), F32), jax.ShapeDtypeStruct((1, aw), F32),
                   jax.ShapeDtypeStruct((1, aw), F32), jax.ShapeDtypeStruct((1, aw), F32)],
        compiler_params=_params(("arbitrary",)))(dy, z, hc, vg, vb, wm, wmt, bm, sel, cg, cbeta)


def even_conv_bwd(dhc, z, cw, *, name):
    s, zw = z.shape
    aw = zw // 4
    nblk = s // ROWS
    hb_per = ROWS // HALO

    def body(dc_ref, dn_ref, z_ref, zp_ref, cw_ref, dzb_ref, dbb_ref, dcw_ref, exth_ref, extd_ref):
        i = pl.program_id(0)

        @pl.when(i == 0)
        def _():
            dbb_ref[...] = jnp.zeros_like(dbb_ref)
            dcw_ref[...] = jnp.zeros_like(dcw_ref)

        a = z_ref[:, 0:aw]
        sg = _sigmoid(z_ref[:, aw:2 * aw])
        exth_ref[0:HALO, :] = jnp.where(i > 0, zp_ref[:, 0:aw] * _sigmoid(zp_ref[:, aw:2 * aw]), 0.0)
        exth_ref[HALO:HALO + ROWS, :] = a * sg
        dcur = dc_ref[...]
        extd_ref[0:ROWS, :] = dcur
        extd_ref[ROWS:ROWS + HALO, :] = jnp.where(i < nblk - 1, dn_ref[...], 0.0)
        dhb = jnp.zeros((ROWS, aw), F32)
        for k in range(CONV_K):
            wk = cw_ref[k:k + 1, :]
            dhb = dhb + wk * extd_ref[pl.ds(CONV_K - 1 - k, ROWS), :]
            dcw_ref[k:k + 1, :] += _colsum(dcur * exth_ref[pl.ds(HALO - (CONV_K - 1) + k, ROWS), :])
        da = dhb * sg
        dg = dhb * a * sg * (1.0 - sg)
        dzb_ref[:, 0:aw] = da.astype(BF16)
        dzb_ref[:, aw:2 * aw] = dg.astype(BF16)
        dbb_ref[:, 0:aw] += _colsum(da)
        dbb_ref[:, aw:2 * aw] += _colsum(dg)

    return pl.pallas_call(
        body, name=name, grid=(nblk,),
        in_specs=[pl.BlockSpec((ROWS, aw), lambda i: (i, 0)),
                  pl.BlockSpec((HALO, aw), lambda i: (jnp.minimum((i + 1) * hb_per, nblk * hb_per - 1), 0)),
                  pl.BlockSpec((ROWS, 2 * aw), lambda i: (i, 1)),
                  pl.BlockSpec((HALO, 2 * aw), lambda i: (jnp.maximum(i * hb_per - 1, 0), 1)),
                  pl.BlockSpec((CONV_K, aw), lambda i: (0, 0))],
        out_specs=[pl.BlockSpec((ROWS, 2 * aw), lambda i: (i, 0)), pl.BlockSpec((1, 2 * aw), lambda i: (0, 0)),
                   pl.BlockSpec((CONV_K, aw), lambda i: (0, 0))],
        out_shape=[jax.ShapeDtypeStruct((s, 2 * aw), BF16), jax.ShapeDtypeStruct((1, 2 * aw), F32),
                   jax.ShapeDtypeStruct((CONV_K, aw), F32)],
        scratch_shapes=[pltpu.VMEM((HALO + ROWS, aw), F32), pltpu.VMEM((ROWS + HALO, aw), F32)],
        compiler_params=_params(("arbitrary",)))(dhc, dhc, z, z, cw)


FFN_ROWS = 1024
FFN_CHUNK = 16


def _ffn_tile(f):
    for t in (256, 128):
        if f % t == 0:
            return t
    raise ValueError(f)


def _taps(ext_ref, w, b, r0, rows):
    acc = b
    for k in range(FFN_K):
        acc = acc + w[k] * ext_ref[pl.ds(FHALO - (FFN_K - 1) + k + r0, rows), :]
    return acc


def ffn_mid_fwd(up_g, up_v, cw, cb, *, name):
    s, f = up_g.shape
    tn = _ffn_tile(f)
    nj = f // tn
    ROWS = FFN_ROWS
    per = ROWS // FHALO

    def body(ug_ref, uv_ref, pg_ref, pv_ref, wg_ref, wv_ref, bg_ref, bv_ref, act_ref, eg_ref, ev_ref):
        i = pl.program_id(0)
        for cur_ref, prev_ref, ext_ref in ((ug_ref, pg_ref, eg_ref), (uv_ref, pv_ref, ev_ref)):
            ext_ref[0:FHALO, :] = jnp.where(i > 0, prev_ref[...], 0.0)
            ext_ref[FHALO:FHALO + ROWS, :] = cur_ref[...]
        wg = [wg_ref[k:k + 1, :] for k in range(FFN_K)]
        wv = [wv_ref[k:k + 1, :] for k in range(FFN_K)]
        bg, bv = bg_ref[...], bv_ref[...]
        for r0 in range(0, ROWS, FFN_CHUNK):
            gate = _taps(eg_ref, wg, bg, r0, FFN_CHUNK)
            val = _taps(ev_ref, wv, bv, r0, FFN_CHUNK)
            act_ref[pl.ds(r0, FFN_CHUNK), :] = (gate * _sigmoid(gate) * val).astype(BF16)

    cur = lambda off: pl.BlockSpec((ROWS, tn), lambda i, j: (i, j + off))
    prev = lambda off: pl.BlockSpec((FHALO, tn), lambda i, j: (jnp.maximum(i * per - 1, 0), j + off))
    wsp = lambda off: pl.BlockSpec((FFN_K, tn), lambda i, j: (0, j + off))
    bsp = lambda off: pl.BlockSpec((1, tn), lambda i, j: (0, j + off))
    return pl.pallas_call(
        body, name=name, grid=(s // ROWS, nj),
        in_specs=[cur(0), cur(0), prev(0), prev(0), wsp(0), wsp(nj), bsp(0), bsp(nj)],
        out_specs=pl.BlockSpec((ROWS, tn), lambda i, j: (i, j)),
        out_shape=jax.ShapeDtypeStruct((s, f), BF16),
        scratch_shapes=[pltpu.VMEM((FHALO + ROWS, tn), F32), pltpu.VMEM((FHALO + ROWS, tn), F32)],
        compiler_params=_params(("parallel", "parallel")))(up_g, up_v, up_g, up_v, cw, cw, cb, cb)


def ffn_mid_bwd(dact, up_g, up_v, cw, cb, *, name):
    s, f = up_g.shape
    tn = _ffn_tile(f)
    nj = f // tn
    ROWS = FFN_ROWS
    nblk = s // ROWS
    per = ROWS // FHALO
    ext = ROWS + FHALO

    def body(da_ref, dan_ref, ug_ref, uv_ref, pg_ref, pv_ref, ng_ref, nv_ref, wg_ref, wv_ref, bg_ref, bv_ref,
             dug_ref, duv_ref, dwg_ref, dwv_ref, dbg_ref, dbv_ref, eg_ref, ev_ref, dg_ref, dv_ref):
        i = pl.program_id(1)

        @pl.when(i == 0)
        def _():
            for r in (dwg_ref, dwv_ref, dbg_ref, dbv_ref):
                r[...] = jnp.zeros_like(r)

        for cur_ref, prev_ref, next_ref, ext_ref in ((ug_ref, pg_ref, ng_ref, eg_ref), (uv_ref, pv_ref, nv_ref, ev_ref)):
            ext_ref[0:FHALO, :] = jnp.where(i > 0, prev_ref[...], 0.0)
            ext_ref[FHALO:FHALO + ROWS, :] = cur_ref[...]
            ext_ref[FHALO + ROWS:FHALO + ROWS + FHALO, :] = jnp.where(i < nblk - 1, next_ref[...], 0.0)
        wg = [wg_ref[k:k + 1, :] for k in range(FFN_K)]
        wv = [wv_ref[k:k + 1, :] for k in range(FFN_K)]
        bg, bv = bg_ref[...], bv_ref[...]

        for r0, rows in [(r, FFN_CHUNK) for r in range(0, ROWS, FFN_CHUNK)] + [(ROWS, FHALO)]:
            gate = _taps(eg_ref, wg, bg, r0, rows)
            val = _taps(ev_ref, wv, bv, r0, rows)
            da = da_ref[pl.ds(r0, rows), :] if r0 < ROWS else jnp.where(i < nblk - 1, dan_ref[...], 0.0)
            sg = _sigmoid(gate)
            dg_ref[pl.ds(r0, rows), :] = da * val * (sg * (1.0 + gate * (1.0 - sg)))
            dv_ref[pl.ds(r0, rows), :] = da * (gate * sg)

        def back(d_ref, w, ext_ref, du_ref, dw_ref, db_ref):
            zero = jnp.zeros((FFN_CHUNK, tn), F32)
            acc = [zero] * FFN_K
            accb = zero
            for r0 in range(0, ROWS, FFN_CHUNK):
                d = [d_ref[pl.ds(r0 + FFN_K - 1 - k, FFN_CHUNK), :] for k in range(FFN_K)]
                u = ext_ref[pl.ds(FHALO + r0, FFN_CHUNK), :]
                du = w[0] * d[0]
                for k in range(1, FFN_K):
                    du = du + w[k] * d[k]
                du_ref[pl.ds(r0, FFN_CHUNK), :] = du.astype(BF16)
                acc = [acc[k] + u * d[k] for k in range(FFN_K)]
                accb = accb + d[FFN_K - 1]
            for k in range(FFN_K):
                dw_ref[k:k + 1, :] += _colsum(acc[k])
            db_ref[...] += _colsum(accb)

        back(dg_ref, wg, eg_ref, dug_ref, dwg_ref, dbg_ref)
        back(dv_ref, wv, ev_ref, duv_ref, dwv_ref, dbv_ref)

    cur = lambda off: pl.BlockSpec((ROWS, tn), lambda j, i: (i, j + off))
    prev = lambda off: pl.BlockSpec((FHALO, tn), lambda j, i: (jnp.maximum(i * per - 1, 0), j + off))
    nxt = lambda off: pl.BlockSpec((FHALO, tn), lambda j, i: (jnp.minimum((i + 1) * per, nblk * per - 1), j + off))
    wsp = lambda off: pl.BlockSpec((FFN_K, tn), lambda j, i: (0, j + off))
    bsp = lambda off: pl.BlockSpec((1, tn), lambda j, i: (0, j + off))
    outs = pl.pallas_call(
        body, name=name, grid=(nj, nblk),
        in_specs=[cur(0), nxt(0), cur(0), cur(0), prev(0), prev(0), nxt(0), nxt(0),
                  wsp(0), wsp(nj), bsp(0), bsp(nj)],
        out_specs=[cur(0), cur(0), wsp(0), wsp(0), bsp(0), bsp(0)],
        out_shape=[jax.ShapeDtypeStruct((s, f), BF16), jax.ShapeDtypeStruct((s, f), BF16),
                   jax.ShapeDtypeStruct((FFN_K, f), F32), jax.ShapeDtypeStruct((FFN_K, f), F32),
                   jax.ShapeDtypeStruct((1, f), F32), jax.ShapeDtypeStruct((1, f), F32)],
        scratch_shapes=[pltpu.VMEM((ROWS + 2 * FHALO, tn), F32), pltpu.VMEM((ROWS + 2 * FHALO, tn), F32),
                        pltpu.VMEM((ext, tn), F32), pltpu.VMEM((ext, tn), F32)],
        compiler_params=_params(("parallel", "arbitrary")))(dact, dact, up_g, up_v, up_g, up_v, up_g, up_v,
                                                            cw, cw, cb, cb)
    dug, duv, dwg, dwv, dbg, dbv = outs
    return dug, duv, jnp.concatenate([dwg, dwv], axis=1), jnp.concatenate([dbg, dbv], axis=1)


def rope_tables(s):
    half = HEAD // 2
    lane = jnp.arange(128)
    j = lane % HEAD
    inv = ROPE_THETA ** (-(j % half).astype(F32) / half)
    ang = jnp.arange(s, dtype=F32)[:, None] * inv[None, :]
    sign = jnp.where(j < half, -1.0, 1.0).astype(F32)
    return jnp.cos(ang), jnp.sin(ang) * sign[None, :]


def _swap_halves(x):
    lane = lax.broadcasted_iota(jnp.int32, x.shape, 1)
    return jnp.where((lane % HEAD) < HEAD // 2, pltpu.roll(x, 128 - HEAD // 2, 1), pltpu.roll(x, HEAD // 2, 1))


def rope_fwd(qkv, cos, sin, *, name):
    s, d3 = qkv.shape
    d = d3 // 3
    scale = HEAD ** -0.5

    def body(xq_ref, xk_ref, c_ref, s_ref, q_ref, k_ref):
        c = c_ref[...]
        sn = s_ref[...]
        for t in range(d // 128):
            cs = slice(t * 128, (t + 1) * 128)
            xq = xq_ref[:, cs]
            xk = xk_ref[:, cs]
            q_ref[:, cs] = (xq * c + _swap_halves(xq) * sn) * scale
            k_ref[:, cs] = xk * c + _swap_halves(xk) * sn

    row = pl.BlockSpec((ROWS, d), lambda i: (i, 0))
    tab = pl.BlockSpec((ROWS, 128), lambda i: (i, 0))
    return pl.pallas_call(
        body, name=name, grid=(s // ROWS,),
        in_specs=[row, pl.BlockSpec((ROWS, d), lambda i: (i, 1)), tab, tab],
        out_specs=[row, row],
        out_shape=[jax.ShapeDtypeStruct((s, d), F32)] * 2,
        compiler_params=_params(("parallel",)))(qkv, qkv, cos, sin)


def rope_bwd(dq, dk, dv, cos, sin, *, name):
    s, d = dq.shape
    scale = HEAD ** -0.5

    def body(dq_ref, dk_ref, dv_ref, c_ref, s_ref, o_ref):
        c = c_ref[...]
        sn = s_ref[...]
        for t in range(d // 128):
            cs = slice(t * 128, (t + 1) * 128)
            gq = dq_ref[:, cs] * scale
            gk = dk_ref[:, cs]
            o_ref[:, t * 128:(t + 1) * 128] = (gq * c + _swap_halves(gq * sn)).astype(BF16)
            o_ref[:, d + t * 128:d + (t + 1) * 128] = (gk * c + _swap_halves(gk * sn)).astype(BF16)
        o_ref[:, 2 * d:3 * d] = dv_ref[...].astype(BF16)

    row = pl.BlockSpec((ROWS, d), lambda i: (i, 0))
    tab = pl.BlockSpec((ROWS, 128), lambda i: (i, 0))
    return pl.pallas_call(
        body, name=name, grid=(s // ROWS,),
        in_specs=[row, row, row, tab, tab],
        out_specs=pl.BlockSpec((ROWS, 3 * d), lambda i: (i, 0)),
        out_shape=jax.ShapeDtypeStruct((s, 3 * d), BF16),
        compiler_params=_params(("parallel",)))(dq, dk, dv, cos, sin)


ATT_T = BLOCK * max(DILATIONS)


FWD_GROUP = 2
ATT_GROUP = 4
FWD_QROWS = 128
BWD_QROWS = 64


def _unit_rows(r, j, dil):
    start = r + dil * BLOCK * j
    return pl.ds(start, BLOCK) if dil == 1 else pl.ds(start, BLOCK, stride=dil)


def _units():
    for bi, dil in enumerate(DILATIONS):
        nsub = ATT_T // (BLOCK * dil)
        for r in range(dil):
            for j in range(nsub):
                yield bi, dil, nsub, r, j


def _band(first_block, part, qrows):
    qi = lax.broadcasted_iota(jnp.int32, (qrows, 2 * BLOCK), 0) + part * qrows
    kj = lax.broadcasted_iota(jnp.int32, (qrows, 2 * BLOCK), 1)
    dist = BLOCK + qi - kj
    band = (dist >= 0) & (dist <= BLOCK)
    return band, band & (jnp.logical_not(first_block) | (kj >= BLOCK))


def _col(tile, h):
    return tile[:, h * HEAD:h * HEAD + 1]


def _keys(cur_ref, prev_ref, r, j, dil, nsub):
    cur = cur_ref[_unit_rows(r, j, dil), :]
    prev = cur_ref[_unit_rows(r, j - 1, dil), :] if j > 0 else prev_ref[_unit_rows(r, nsub - 1, dil), :]
    return jnp.concatenate([prev, cur], axis=0).astype(BF16)


def _att_specs(d, col_off=0):
    nt_cols = d // 128
    cur = pl.BlockSpec((ATT_T, 128), lambda n, p: (n, p + col_off))
    prv = pl.BlockSpec((ATT_T, 128), lambda n, p: (jnp.maximum(n - 1, 0), p + col_off))
    return cur, prv


def attn_fwd(q, k, qkv, *, name):
    s, d = q.shape
    nt = s // ATT_T

    def body(q_ref, kc_ref, kp_ref, vc_ref, vp_ref, o_ref, lse_ref, acc_ref, m_ref, l_ref):
        n = pl.program_id(0)
        QROWS = FWD_QROWS
        nparts = BLOCK // QROWS
        bands = [_band(n == 0, part, QROWS) for part in range(nparts)]
        lo = _pair_masks()
        keep = _head_keep(BLOCK)
        nb = len(DILATIONS)
        tile = lambda cols: jnp.where(lo, jnp.concatenate(cols[:nparts], axis=0),
                                      jnp.concatenate(cols[nparts:], axis=0))

        def scores(unit):
            bi, dil, nsub, r, j = unit
            rows = _unit_rows(r, j, dil)
            kw = _keys(kc_ref, kp_ref, r, j, dil, nsub)
            vw = _keys(vc_ref, vp_ref, r, j, dil, nsub)
            qp = q_ref[rows, :].astype(BF16)
            sc2 = _dot(jnp.concatenate([qp * keep[0], qp * keep[1]], axis=0), kw, NT)
            return dict(bi=bi, j=j, rows=rows, vw=vw, sc2=sc2, m_old=m_ref[rows, :] if bi > 0 else None)

        def softmax(u):
            prs, new_m, new_l, alpha = [], [], [], []
            for c in range(2 * nparts):
                h, part = divmod(c, nparts)
                valid = bands[part][0 if u['j'] > 0 else 1]
                sc = jnp.where(valid, u['sc2'][c * QROWS:(c + 1) * QROWS], NEG)
                mx = jnp.max(sc, axis=-1, keepdims=True)
                if u['bi'] == 0:
                    m_new = mx
                else:
                    m_old = _col(u['m_old'][part * QROWS:(part + 1) * QROWS], h)
                    m_new = jnp.maximum(m_old, mx)
                    alpha.append(jnp.exp(m_old - m_new))
                pr = jnp.exp(sc - m_new)
                new_m.append(m_new)
                new_l.append(jnp.sum(pr, axis=-1, keepdims=True))
                prs.append(pr.astype(BF16))
            u.update(pr2=jnp.concatenate(prs, axis=0), new_m=new_m, new_l=new_l, alpha=alpha)

        def combine(u):
            rows, bi = u['rows'], u['bi']
            pv2 = _dot(u['pr2'], u['vw'], NN)
            m_t = tile(u['new_m'])
            l_t = tile(u['new_l'])
            acc_t = jnp.where(lo, pv2[:BLOCK], pv2[BLOCK:])
            if bi > 0:
                a_t = tile(u['alpha'])
                l_t = a_t * l_ref[rows, :] + l_t
                acc_t = a_t * acc_ref[rows, :] + acc_t
            if bi == nb - 1:
                o_ref[rows, :] = acc_t / l_t
                lse_ref[rows, :] = m_t + jnp.log(l_t)
            else:
                acc_ref[rows, :] = acc_t
                m_ref[rows, :] = m_t
                l_ref[rows, :] = l_t

        units = list(_units())
        for first in range(0, len(units), FWD_GROUP):
            pair = [scores(u) for u in units[first:first + FWD_GROUP]]
            for u in pair:
                softmax(u)
            for u in pair:
                combine(u)

    cur, prv = _att_specs(d)
    vcur, vprv = _att_specs(d, 2 * (d // 128))
    return pl.pallas_call(
        body, name=name, grid=(nt, d // 128), in_specs=[cur, cur, prv, vcur, vprv], out_specs=[cur, cur],
        out_shape=[jax.ShapeDtypeStruct((s, d), F32)] * 2,
        scratch_shapes=[pltpu.VMEM((ATT_T, 128), F32)] * 3,
        compiler_params=_params(("parallel", "parallel")))(q, k, k, qkv, qkv)


def attn_delta(do, o, *, name):
    s, d = do.shape

    def body(do_ref, o_ref, dl_ref):
        lane = lax.broadcasted_iota(jnp.int32, (ROWS, 128), 1)
        lo = lane < HEAD
        for p in range(d // 128):
            cs = slice(p * 128, (p + 1) * 128)
            pr = do_ref[:, cs] * o_ref[:, cs]
            s0 = jnp.sum(jnp.where(lo, pr, 0.0), axis=-1, keepdims=True)
            s1 = jnp.sum(jnp.where(lo, 0.0, pr), axis=-1, keepdims=True)
            dl_ref[:, cs] = jnp.where(lo, s0, s1)

    row = pl.BlockSpec((ROWS, d), lambda i: (i, 0))
    return pl.pallas_call(
        body, name=name, grid=(s // ROWS,), in_specs=[row, row], out_specs=row,
        out_shape=jax.ShapeDtypeStruct((s, d), F32),
        compiler_params=_params(("parallel",)))(do, o)


def attn_dq(q, k, qkv, do, lse, delta, *, name):
    s, d = q.shape
    nt = s // ATT_T

    def body(q_ref, kc_ref, kp_ref, vc_ref, vp_ref, do_ref, l_ref, dl_ref, dq_ref):
        n = pl.program_id(0)
        QROWS = BWD_QROWS
        nparts = BLOCK // QROWS
        bands = [_band(n == 0, part, QROWS) for part in range(nparts)]
        lo = _pair_masks()
        keep = _head_keep(BLOCK)
        def scores(unit):
            bi, dil, nsub, r, j = unit
            rows = _unit_rows(r, j, dil)
            kw = _keys(kc_ref, kp_ref, r, j, dil, nsub)
            vw = _keys(vc_ref, vp_ref, r, j, dil, nsub)
            qp = q_ref[rows, :].astype(BF16)
            dop = do_ref[rows, :].astype(BF16)
            sc2 = _dot(jnp.concatenate([qp * keep[0], qp * keep[1]], axis=0), kw, NT)
            dp2 = _dot(jnp.concatenate([dop * keep[0], dop * keep[1]], axis=0), vw, NT)
            return dict(bi=bi, j=j, rows=rows, kw=kw, sc2=sc2, dp2=dp2, lt=l_ref[rows, :], dt=dl_ref[rows, :])

        def softmax_bwd(u):
            dss = []
            for c in range(2 * nparts):
                h, part = divmod(c, nparts)
                valid = bands[part][0 if u['j'] > 0 else 1]
                cr = slice(c * QROWS, (c + 1) * QROWS)
                pr_rows = slice(part * QROWS, (part + 1) * QROWS)
                pr = jnp.where(valid, jnp.exp(u['sc2'][cr] - _col(u['lt'][pr_rows], h)), 0.0)
                dss.append((pr * (u['dp2'][cr] - _col(u['dt'][pr_rows], h))).astype(BF16))
            u['ds2'] = jnp.concatenate(dss, axis=0)

        def combine(u):
            rows = u['rows']
            dq2 = _dot(u['ds2'], u['kw'], NN)
            dq_t = jnp.where(lo, dq2[:BLOCK], dq2[BLOCK:])
            if u['bi'] > 0:
                dq_t = dq_t + dq_ref[rows, :]
            dq_ref[rows, :] = dq_t

        units = list(_units())
        for first in range(0, len(units), ATT_GROUP):
            pair = [scores(u) for u in units[first:first + ATT_GROUP]]
            for u in pair:
                softmax_bwd(u)
            for u in pair:
                combine(u)

    cur, prv = _att_specs(d)
    vcur, vprv = _att_specs(d, 2 * (d // 128))
    return pl.pallas_call(
        body, name=name, grid=(nt, d // 128), in_specs=[cur, cur, prv, vcur, vprv, cur, cur, cur], out_specs=cur,
        out_shape=jax.ShapeDtypeStruct((s, d), F32),
        compiler_params=_params(("parallel", "parallel")))(q, k, k, qkv, qkv, do, lse, delta)


def attn_dkv(q, k, qkv, do, lse, delta, *, name):
    s, d = q.shape
    nt = s // ATT_T

    def body(k_ref, v_ref, qc_ref, qn_ref, doc_ref, don_ref, lc_ref, ln_ref, dc_ref, dn_ref, dk_ref, dv_ref):
        n = pl.program_id(0)
        qi = lax.broadcasted_iota(jnp.int32, (BLOCK, BLOCK), 0)
        kj = lax.broadcasted_iota(jnp.int32, (BLOCK, BLOCK), 1)
        own = kj <= qi
        nxt = kj >= qi
        nxt_edge = nxt & (n < nt - 1)
        keep = _head_keep(BLOCK)
        lo = _pair_masks()
        def scores(unit):
            bi, dil, nsub, r, j = unit
            rows = _unit_rows(r, j, dil)
            inner = j + 1 < nsub
            nrows = _unit_rows(r, j + 1, dil) if inner else _unit_rows(r, 0, dil)
            kp = k_ref[rows, :].astype(BF16)
            vp = v_ref[rows, :].astype(BF16)
            far = not inner
            take = lambda c_ref, n_ref, nx: ((n_ref if far else c_ref)[nrows, :] if nx else c_ref[rows, :])
            qs = [take(qc_ref, qn_ref, nx).astype(BF16) for nx in (False, True)]
            dos = [take(doc_ref, don_ref, nx).astype(BF16) for nx in (False, True)]
            lts = [take(lc_ref, ln_ref, nx) for nx in (False, True)]
            dts = [take(dc_ref, dn_ref, nx) for nx in (False, True)]
            q4 = jnp.concatenate([qs[nx] * keep[h] for h in range(2) for nx in range(2)], axis=0)
            do4 = jnp.concatenate([dos[nx] * keep[h] for h in range(2) for nx in range(2)], axis=0)
            return dict(bi=bi, rows=rows, q4=q4, do4=do4, s4=_dot(q4, kp, NT), dp4=_dot(do4, vp, NT), lts=lts,
                        dts=dts, valids=(own, nxt if inner else nxt_edge))

        def softmax_bwd(u):
            prs, dss = [], []
            for c in range(4):
                h, nx = divmod(c, 2)
                cr = slice(c * BLOCK, (c + 1) * BLOCK)
                pr = jnp.where(u['valids'][nx], jnp.exp(u['s4'][cr] - _col(u['lts'][nx], h)), 0.0)
                prs.append(pr.astype(BF16))
                dss.append((pr * (u['dp4'][cr] - _col(u['dts'][nx], h))).astype(BF16))
            u.update(pr4=jnp.concatenate(prs, axis=0), ds4=jnp.concatenate(dss, axis=0))

        def combine(u):
            rows = u['rows']
            dv_t = _dot(u['pr4'], u['do4'], TN)
            dk_t = _dot(u['ds4'], u['q4'], TN)
            if u['bi'] > 0:
                dk_t = dk_t + dk_ref[rows, :]
                dv_t = dv_t + dv_ref[rows, :]
            dk_ref[rows, :] = dk_t
            dv_ref[rows, :] = dv_t

        units = list(_units())
        for first in range(0, len(units), ATT_GROUP):
            pair = [scores(u) for u in units[first:first + ATT_GROUP]]
            for u in pair:
                softmax_bwd(u)
            for u in pair:
                combine(u)

    cur = pl.BlockSpec((ATT_T, 128), lambda n, p: (n, p))
    nxt_spec = pl.BlockSpec((ATT_T, 128), lambda n, p: (jnp.minimum(n + 1, nt - 1), p))
    vcur = pl.BlockSpec((ATT_T, 128), lambda n, p: (n, p + 2 * (d // 128)))
    return pl.pallas_call(
        body, name=name, grid=(nt, d // 128),
        in_specs=[cur, vcur, cur, nxt_spec, cur, nxt_spec, cur, nxt_spec, cur, nxt_spec], out_specs=[cur, cur],
        out_shape=[jax.ShapeDtypeStruct((s, d), F32)] * 2,
        compiler_params=_params(("parallel", "parallel")))(k, qkv, q, q, do, do, lse, lse, delta, delta)


def adamw(parts_list, w, m, v, *, name):
    nk = len(parts_list)
    npart, rk, c = parts_list[0].shape
    r = rk * nk
    assert w.shape == (r, c)
    tr = next(t for t in range(rk, 0, -8) if rk % t == 0 and (t * c * 4 <= 1024 * 1024 or t == 8))
    nbk = rk // tr

    def body(*refs):
        p_refs = refs[:nk]
        w_ref, m_ref, v_ref, g_ref, d_ref, nm_ref, nv_ref = refs[nk:]
        i = pl.program_id(0)
        g = None
        for kk, p_ref in enumerate(p_refs):
            gk = p_ref[0].astype(F32)
            for j in range(1, npart):
                gk = gk + p_ref[j].astype(F32)
            g = gk if g is None else jnp.where(i >= kk * nbk, gk, g)
        m2 = B1 * m_ref[...] + (1.0 - B1) * g
        v2 = B2 * v_ref[...] + (1.0 - B2) * (g * g)
        m_hat = m2 / (1.0 - B1 ** STEP)
        v_hat = v2 / (1.0 - B2 ** STEP)
        g_ref[...] = g
        d_ref[...] = -LR * (m_hat / (jnp.sqrt(v_hat) + ADAM_EPS) + WD * w_ref[...])
        nm_ref[...] = m2
        nv_ref[...] = v2

    blk = pl.BlockSpec((tr, c), lambda i: (i, 0))
    pspec = lambda kk: pl.BlockSpec((npart, tr, c), lambda i: (0, jnp.clip(i - kk * nbk, 0, nbk - 1), 0))
    return pl.pallas_call(
        body, name=name, grid=(r // tr,),
        in_specs=[pspec(kk) for kk in range(nk)] + [blk, blk, blk],
        out_specs=[blk] * 4, out_shape=[jax.ShapeDtypeStruct((r, c), F32)] * 4,
        compiler_params=_params(("parallel",)))(*parts_list, w, m, v)


def _my_index():
    return 4 * lax.axis_index("x") + 2 * lax.axis_index("y") + lax.axis_index("c")


def exchange(arrays, scatter, *, name):
    nt = len(arrays)

    def body(*refs):
        ins = refs[:nt]
        outs = refs[nt:2 * nt]
        send_sems, recv_sems, local_sems = refs[2 * nt:]
        x, y, c = lax.axis_index("x"), lax.axis_index("y"), lax.axis_index("c")
        me = 4 * x + 2 * y + c
        copies = []
        for t in range(nt):
            src = ins[t].at[me] if scatter[t] else ins[t]
            cp = pltpu.make_async_copy(src, outs[t].at[me], local_sems.at[t])
            cp.start()
            copies.append(cp)
        remote = []
        for kk in range(1, N_DEV):
            px, py, pc = x ^ (kk >> 2), y ^ ((kk >> 1) & 1), c ^ (kk & 1)
            peer = 4 * px + 2 * py + pc
            for t in range(nt):
                src = ins[t].at[peer] if scatter[t] else ins[t]
                cp = pltpu.make_async_remote_copy(
                    src_ref=src, dst_ref=outs[t].at[me], send_sem=send_sems.at[t, kk], recv_sem=recv_sems.at[t, kk],
                    device_id=(px, py, pc), device_id_type=pl.DeviceIdType.MESH)
                cp.start()
                remote.append(cp)
        for cp in remote:
            cp.wait()
        for cp in copies:
            cp.wait()

    hbm = pl.BlockSpec(memory_space=pl.ANY)
    out_shape = [jax.ShapeDtypeStruct(a.shape if scatter[t] else (N_DEV,) + a.shape, a.dtype)
                 for t, a in enumerate(arrays)]
    return pl.pallas_call(
        body, name=name, in_specs=[hbm] * nt, out_specs=[hbm] * nt, out_shape=out_shape,
        scratch_shapes=[pltpu.SemaphoreType.DMA((nt, N_DEV)), pltpu.SemaphoreType.DMA((nt, N_DEV)),
                        pltpu.SemaphoreType.DMA((nt,))],
        compiler_params=pltpu.CompilerParams(has_side_effects=True))(*arrays)


def _peer_of(kk):
    x, y, c = lax.axis_index("x"), lax.axis_index("y"), lax.axis_index("c")
    return x ^ (kk >> 2), y ^ ((kk >> 1) & 1), c ^ (kk & 1)


def _peer_copy(t, kk, scatter, ins, lands, send_sems, recv_sems):
    px, py, pc = _peer_of(kk)
    me = _my_index()
    src = ins[t].at[4 * px + 2 * py + pc] if scatter[t] else ins[t]
    return pltpu.make_async_remote_copy(
        src_ref=src, dst_ref=lands[t].at[me], send_sem=send_sems.at[t * N_DEV + kk],
        recv_sem=recv_sems.at[t * N_DEV + kk], device_id=(px, py, pc), device_id_type=pl.DeviceIdType.MESH)


def _own_copy(t, scatter, ins, lands, own_sems):
    me = _my_index()
    return pltpu.make_async_copy(ins[t].at[me] if scatter[t] else ins[t], lands[t].at[me], own_sems.at[t])


_HBM = pl.BlockSpec(memory_space=pltpu.HBM)
_SEM = pl.BlockSpec(memory_space=pltpu.SEMAPHORE)
_EFFECT = pltpu.SideEffectType.DATAFLOW_SIDE_EFFECTING


def exchange_start(arrays, scatter, *, name):
    nt = len(arrays)
    land_shapes = [a.shape if scatter[t] else (N_DEV,) + a.shape for t, a in enumerate(arrays)]

    def body(*refs):
        ins, lands = refs[:nt], refs[nt:2 * nt]
        send_sems, recv_sems, own_sems = refs[2 * nt:2 * nt + 3]
        token = refs[-1]
        for kk in range(1, N_DEV):
            for t in range(nt):
                _peer_copy(t, kk, scatter, ins, lands, send_sems, recv_sems).start()
        for t in range(nt):
            _own_copy(t, scatter, ins, lands, own_sems).start()
        token[...] = jnp.zeros_like(token)

    sems = pltpu.SemaphoreType.DMA((nt * N_DEV,))
    outs = pl.pallas_call(
        body, name=name,
        out_shape=(sems, sems, pltpu.SemaphoreType.DMA((nt,)), *[pltpu.HBM(a.shape, a.dtype) for a in arrays],
                   *[pltpu.HBM(shp, a.dtype) for shp, a in zip(land_shapes, arrays)],
                   jax.ShapeDtypeStruct((8, 128), F32)),
        in_specs=[_HBM] * (2 * nt),
        out_specs=(_SEM, _SEM, _SEM, *[_HBM] * (2 * nt), pl.BlockSpec(memory_space=pltpu.VMEM)),
        input_output_aliases={i: 3 + i for i in range(2 * nt)},
        compiler_params=pltpu.CompilerParams(has_side_effects=_EFFECT),
    )(*[pltpu.with_memory_space_constraint(a, pltpu.HBM) for a in arrays],
      *[pltpu.with_memory_space_constraint(lax.empty(shp, a.dtype), pltpu.HBM) for shp, a in zip(land_shapes, arrays)])
    return (outs[:3], outs[3:3 + nt], outs[3 + nt:3 + 2 * nt], scatter), outs[-1]


def exchange_wait(handle, after, *, name):
    sems, thru, lands, scatter = handle
    nt = len(thru)

    def body(*refs):
        ins, lnd = refs[:nt], refs[nt:2 * nt]
        s_sems, r_sems, o_sems = refs[2 * nt:2 * nt + 3]
        for kk in range(1, N_DEV):
            for t in range(nt):
                cp = _peer_copy(t, kk, scatter, ins, lnd, s_sems, r_sems)
                cp.wait_send()
                cp.wait_recv()
        for t in range(nt):
            _own_copy(t, scatter, ins, lnd, o_sems).wait()

    outs = pl.pallas_call(
        body, name=name,
        out_shape=(*[pltpu.HBM(a.shape, a.dtype) for a in thru], *[pltpu.HBM(a.shape, a.dtype) for a in lands]),
        in_specs=[_HBM] * (2 * nt) + [_SEM, _SEM, _SEM, pl.BlockSpec(memory_space=pl.ANY)],
        out_specs=tuple([_HBM] * (2 * nt)),
        input_output_aliases={i: i for i in range(2 * nt)},
        compiler_params=pltpu.CompilerParams(has_side_effects=_EFFECT),
    )(*thru, *lands, *sems, after)
    return outs[nt:]


def _cols_from_shards(g):
    g = jnp.moveaxis(g, 0, -2)
    return g.reshape(g.shape[:-2] + (g.shape[-2] * g.shape[-1],))


def _cols_to_shards(w, nshards=N_DEV):
    w = w.reshape(w.shape[:-1] + (nshards, w.shape[-1] // nshards))
    return jnp.moveaxis(w, -2, 0)


def _half_shards(halves):
    return jnp.concatenate([_cols_to_shards(h[None], N_DEV // 2) for h in halves], axis=0)


def _ffn_fwd(x, g, w_up, cw, cb, get_w_down, tag):
    h = rms_fwd(x, g, name=f"{tag}_norm")
    up_g = matmul(h, w_up[0], name=f"{tag}_up_g")
    up_v = matmul(h, w_up[1], name=f"{tag}_up_v")
    act = ffn_mid_fwd(up_g, up_v, cw, cb, name=f"{tag}_mid")
    w_down = get_w_down(act)
    out = matmul(act, w_down, res=x, name=f"{tag}_down")
    return out, (h, up_g, up_v, act), w_down


def _ffn_bwd(dx, dxb, x, saved, g, w_up, cw, cb, w_down, tag):
    h, up_g, up_v, act = saved
    dact = matmul(dxb, w_down, tb=True, name=f"{tag}_ddown")
    d_w_down = matmul_ta(act, dxb, name=f"{tag}_gdown")
    dug, duv, dcw, dcb = ffn_mid_bwd(dact, up_g, up_v, cw, cb, name=f"{tag}_dmid")
    d_w_up = (matmul_ta(h, dug, name=f"{tag}_gup_g"), matmul_ta(h, duv, name=f"{tag}_gup_v"))
    dh = matmul(dug, w_up[0], tb=True, name=f"{tag}_dup_g")
    dh = matmul(duv, w_up[1], tb=True, res=dh, name=f"{tag}_dup_v")
    dx2, dxb2, dg = rms_bwd(dh, x, g, dx, name=f"{tag}_dnorm")
    return dx2, dxb2, dict(norm_g=dg, w_up=d_w_up, conv_w=dcw, conv_b=dcb, w_down=d_w_down)


def local_step(x0, tgt, a, weights, grads_out):
    s, d = x0.shape
    aw = a['even_v_ln_g'].shape[-1]
    causal = jnp.tril(jnp.ones((CHUNK, CHUNK), dtype=bool))
    wm = jnp.where(causal, a['even_w_s'][0], 0.0).astype(BF16)
    wmt = jnp.swapaxes(wm, 1, 2)
    bm = jnp.repeat(a['even_b_s'][0].T, HEAD, axis=1)
    sel = (jnp.arange(aw)[:, None] // HEAD == jnp.arange(128)[None, :]).astype(BF16)
    cos, sin = rope_tables(s)
    ffn_g, ffn_cb = a['ffn_norm_g'], a['ffn_conv_b']

    w0 = weights(0, None)
    w_in, conv_w, odd_g, ffn_cw = w0['w_in'], w0['conv_w'], w0['odd_g'], w0['ffn_cw']
    h0 = rms_fwd(x0, w0['even_g'], name="even_norm")
    z = matmul(h0, w_in, bias=a['even_b_in'], name="even_in")
    ycat, hc = even_mid_fwd(z, a['even_v_ln_g'], a['even_v_ln_b'], wm, bm, conv_w, a['even_conv_b'],
                            a['even_conv_ln_g'], a['even_conv_ln_b'], name="even_mid")
    w1 = weights(1, ycat)
    x1 = matmul(ycat, w1['w_out'], res=x0, name="even_out")
    x2, ffn0, w_down0 = _ffn_fwd(x1, ffn_g[0:1], w1['w_up'], ffn_cw[0], ffn_cb[0:1],
                                 lambda act: weights(2, act)['w_down'], "ffn0")
    h2 = rms_fwd(x2, odd_g, name="odd_norm")
    w2 = weights(3, h2)
    qkv = matmul(h2, w2['w_qkv'], name="odd_qkv")
    q, k = rope_fwd(qkv, cos, sin, name="rope")
    o, lse = attn_fwd(q, k, qkv, name="attn_fwd")
    x3 = matmul(o, w2['w_o'], res=x2, name="odd_out")
    x4, ffn1, _ = _ffn_fwd(x3, ffn_g[1:2], w2['w_up'], ffn_cw[1], ffn_cb[1:2], lambda act: w2['w_down'], "ffn1")
    loss_t, dx, dxb, d_final_g = final_loss_bwd(x4, a['final_norm_g'].reshape(1, -1), tgt, name="final_loss")

    dx, dxb, g1 = _ffn_bwd(dx, dxb, x3, ffn1, ffn_g[1:2], w2['w_up'], ffn_cw[1], ffn_cb[1:2], w2['w_down'], "ffn1")
    dep = grads_out(0, dict(w_up=g1['w_up'], w_down=g1['w_down']))
    do = matmul(dxb, w2['w_o'], tb=True, dep=dep, name="odd_dout")
    d_w_o = matmul_ta(o, dxb, name="odd_gout")
    delta = attn_delta(do, o, name="attn_delta")
    dq = attn_dq(q, k, qkv, do, lse, delta, name="attn_dq")
    dk, dv = attn_dkv(q, k, qkv, do, lse, delta, name="attn_dkv")
    dqkv = rope_bwd(dq, dk, dv, cos, sin, name="rope_bwd")
    d_w_qkv = matmul_ta(h2, dqkv, name="odd_gqkv")
    dep = grads_out(1, dict(w_qkv=d_w_qkv, w_o=d_w_o))
    dh2 = matmul(dqkv, w2['w_qkv'], tb=True, dep=dep, name="odd_dqkv")
    dx, dxb, d_odd_g = rms_bwd(dh2, x2, odd_g, dx, name="odd_dnorm")
    dx, dxb, g0 = _ffn_bwd(dx, dxb, x1, ffn0, ffn_g[0:1], w1['w_up'], ffn_cw[0], ffn_cb[0:1], w_down0, "ffn0")
    dep = grads_out(2, dict(w_up=g0['w_up'], w_down=g0['w_down']))
    d_w_out = matmul_ta(ycat, dxb, dep=dep, name="even_gout")
    dep = grads_out(3, dict(w_out=d_w_out))
    dycat = matmul(dxb, w1['w_out'], tb=True, dep=dep, name="even_dout")
    (dza, dhc, dba, dvg, dvb, dwm, dbs, dcg, dcbeta, dcb) = even_mid_bwd_rows(
        dycat, z, hc, a['even_v_ln_g'], a['even_v_ln_b'], wm, wmt, bm, sel, a['even_conv_ln_g'],
        a['even_conv_ln_b'], name="even_dmid_rows")
    dzb, dbb, dcw = even_conv_bwd(dhc, z, conv_w, name="even_dmid_conv")
    d_w_in = (matmul_ta(h0, dza, name="even_gin_a"), matmul_ta(h0, dzb, name="even_gin_b"))
    dep = grads_out(4, dict(w_in=d_w_in))
    dh0 = matmul(dza, w_in, tb=True, bk=0, dep=dep, name="even_din_a")
    dh0 = matmul(dzb, w_in, tb=True, bk=1, res=dh0, name="even_din_b")
    grad_x, _, d_even_g = rms_bwd(dh0, x0, w0['even_g'], dx, name="even_dnorm")

    nh = a['even_w_s'].shape[1]
    small_grads = {
        'even_norm_g': d_even_g, 'even_b_in': jnp.concatenate([dba, dbb], axis=1), 'even_v_ln_g': dvg,
        'even_v_ln_b': dvb, 'even_w_s': jnp.where(causal, dwm, 0.0)[None], 'even_b_s': dbs[:, :nh].T[None],
        'even_conv_w': dcw[None], 'even_conv_b': dcb, 'even_conv_ln_g': dcg, 'even_conv_ln_b': dcbeta,
        'odd_norm_g': d_odd_g, 'ffn_norm_g': jnp.concatenate([g0['norm_g'], g1['norm_g']], axis=0),
        'ffn_conv_w': jnp.stack([g0['conv_w'], g1['conv_w']]),
        'ffn_conv_b': jnp.concatenate([g0['conv_b'], g1['conv_b']], axis=0),
        'final_norm_g': d_final_g.reshape(-1),
    }
    grads_out(5, dict(small=small_grads))
    return loss_t, grad_x, small_grads


BIG = ['even_w_in', 'even_w_out', 'odd_w_qkv', 'odd_w_o', 'ffn_w_up', 'ffn_w_down']


def _as_tiles(flat, dtype=F32):
    return jnp.pad(flat, (0, (-flat.size) % 2048)).reshape(-1, 128).astype(dtype)


def kernel(*args):
    a = dict(zip(NAMES + ['loss_target'] + ['m_' + n for n in WEIGHTS] + ['v_' + n for n in WEIGHTS], args))
    x0 = a['x'][0]
    tgt = a['loss_target'][0]
    s, d = x0.shape
    me = _my_index()
    bf = lambda t: t.astype(BF16)

    small_local = _as_tiles(jnp.concatenate([a['even_conv_w'].reshape(-1), a['odd_norm_g'].reshape(-1),
                                             a['ffn_conv_w'].reshape(-1)]))
    stage_arrays = [
        [bf(a['even_w_in']), small_local],
        [bf(a['even_w_out']), bf(a['ffn_w_up'][0:1])],
        [bf(a['ffn_w_down'][0:1])],
        [bf(a['odd_w_qkv']), bf(a['odd_w_o']), bf(a['ffn_w_up'][1:2]), bf(a['ffn_w_down'][1:2])],
    ]
    started = [exchange_start(arrs, [False] * len(arrs), name=f"gather{i}_start") for i, arrs in enumerate(stage_arrays)]
    order = sum(tok[0, 0] for _, tok in started)

    def weights(stage, after):
        handle, tok = started[stage]
        full = exchange_wait(handle, tok if after is None else after, name=f"gather{stage}_wait")
        rows = lambda g: jnp.moveaxis(g, 0, 1).reshape(-1, d)
        halves = lambda g: (_cols_from_shards(g[:N_DEV // 2])[0], _cols_from_shards(g[N_DEV // 2:])[0])
        if stage == 0:
            gs = full[1].reshape(N_DEV, -1)
            n_cw, n_og, n_fw = a['even_conv_w'].size, a['odd_norm_g'].size, a['ffn_conv_w'].size
            return dict(
                w_in=_cols_from_shards(full[0])[0], even_g=a['even_norm_g'] + order,
                conv_w=_cols_from_shards(gs[:, :n_cw].reshape((N_DEV,) + a['even_conv_w'].shape))[0],
                odd_g=gs[:, n_cw:n_cw + n_og].reshape(1, -1),
                ffn_cw=_cols_from_shards(gs[:, n_cw + n_og:n_cw + n_og + n_fw].reshape((N_DEV,) + a['ffn_conv_w'].shape)))
        if stage == 1:
            return dict(w_out=rows(full[0]), w_up=halves(full[1]))
        if stage == 2:
            return dict(w_down=rows(full[0]))
        return dict(w_qkv=_cols_from_shards(full[0])[0], w_o=rows(full[1]), w_up=halves(full[2]), w_down=rows(full[3]))

    sent = {}

    def grads_out(stage, g):
        to_rows = lambda w: w.reshape(N_DEV, 1, -1, d)
        if stage in (0, 2):
            pieces, scatter = [_half_shards(g['w_up']), to_rows(g['w_down'])], [True, True]
        elif stage == 1:
            pieces, scatter = [_cols_to_shards(g['w_qkv'][None]), to_rows(g['w_o'])], [True, True]
        elif stage == 3:
            pieces, scatter = [to_rows(g['w_out'])], [True]
        elif stage == 4:
            pieces, scatter = [_half_shards(g['w_in'])], [True]
        else:
            small = jnp.concatenate([g['small'][n].reshape(-1) for n in g['small']])
            pieces, scatter = [_as_tiles(small, BF16)], [False]
        sent[stage], tok = exchange_start(pieces, scatter, name=f"grads{stage}_start")
        return tok

    loss_t, grad_x, small_grads = local_step(x0, tgt, a, weights, grads_out)
    loss = lax.psum(loss_t[0, 0], ("x", "y", "c"))
    received = {stage: exchange_wait(handle, grad_x, name=f"grads{stage}_wait") for stage, handle in sent.items()}

    results = {}
    big_parts = {'even_w_in': [received[4][0]], 'even_w_out': [received[3][0]], 'odd_w_qkv': [received[1][0]],
                 'odd_w_o': [received[1][1]], 'ffn_w_up': [received[2][0], received[0][0]],
                 'ffn_w_down': [received[2][1], received[0][1]]}
    for n in BIG:
        shp = a[n].shape
        flat = lambda t: t.reshape(-1, shp[-1])
        outs = adamw([p.reshape(N_DEV, -1, shp[-1]) for p in big_parts[n]], flat(a[n]), flat(a['m_' + n]),
                     flat(a['v_' + n]), name=f"adamw_{n}")
        results[n] = [t.reshape(shp) for t in outs]

    small_names = list(small_grads)
    n_small = sum(small_grads[n].size for n in small_names)
    rs = received[5][0].reshape(N_DEV, -1)[:, :n_small]
    parts, offs = [], 0
    for n in small_names:
        full = small_grads[n].shape
        piece = rs[:, offs:offs + small_grads[n].size].reshape((N_DEV,) + full)
        offs += small_grads[n].size
        shp = a[n].shape
        if shp != full:
            width = shp[-1]
            piece = lax.dynamic_slice_in_dim(piece, me * width, width, axis=piece.ndim - 1)
        parts.append(piece.reshape(N_DEV, -1))
    parts = jnp.concatenate(parts, axis=1)
    pad = (-parts.shape[1]) % 2048
    cat = lambda pre: _as_tiles(jnp.concatenate([a[pre + n].reshape(-1) for n in small_names]))
    outs = adamw([jnp.pad(parts, ((0, 0), (0, pad))).reshape(N_DEV, -1, 128)], cat(''), cat('m_'), cat('v_'),
                 name="adamw_small")
    offs = 0
    for n in small_names:
        size = a[n].size
        results[n] = [t.reshape(-1)[offs:offs + size].reshape(a[n].shape) for t in outs]
        offs += size

    out = [loss, grad_x[None]]
    for i in range(4):
        out += [results[n][i] for n in WEIGHTS]
    return tuple(out)
```

```python
import functools
import math

import jax
import jax.numpy as jnp
from jax import lax
from jax.experimental import pallas as pl
from jax.experimental.pallas import tpu as pltpu

F32 = jnp.float32
BF16 = jnp.bfloat16

N_DEV = 8
EPS = 1e-6
NEG = -1e30
HEAD = 64
CHUNK = 128
BLOCK = 128
CONV_K = 31
FFN_K = 3
DILATIONS = (1, 4, 16)
ROPE_THETA = 10000.0
LR, B1, B2, ADAM_EPS, WD, STEP = 0.001, 0.9, 0.999, 1e-08, 0.01, 10

VMEM_LIMIT = 56 * 1024 * 1024
VMEM_BUDGET = 32 * 1024 * 1024
ROWS = 512
HALO = 32
FHALO = 8

NAMES = ['x', 'even_norm_g', 'even_w_in', 'even_b_in', 'even_v_ln_g', 'even_v_ln_b', 'even_w_s', 'even_b_s',
         'even_conv_w', 'even_conv_b', 'even_conv_ln_g', 'even_conv_ln_b', 'even_w_out', 'odd_norm_g',
         'odd_w_qkv', 'odd_w_o', 'ffn_norm_g', 'ffn_w_up', 'ffn_conv_w', 'ffn_conv_b', 'ffn_w_down',
         'final_norm_g']
WEIGHTS = NAMES[1:]


def _params(sem=None):
    return pltpu.CompilerParams(dimension_semantics=sem, vmem_limit_bytes=VMEM_LIMIT)


def _sigmoid(x):
    return 1.0 / (1.0 + jnp.exp(-x))


def _gelu(x):
    c = math.sqrt(2.0 / math.pi)
    return 0.5 * x * (1.0 + jnp.tanh(c * (x + 0.044715 * x * x * x)))


def _gelu_grad(x):
    c = math.sqrt(2.0 / math.pi)
    t = jnp.tanh(c * (x + 0.044715 * x * x * x))
    return 0.5 * (1.0 + t) + 0.5 * x * (1.0 - t * t) * c * (1.0 + 3.0 * 0.044715 * x * x)


def _ln_stats(x):
    mu = jnp.mean(x, axis=-1, keepdims=True)
    xc = x - mu
    rstd = lax.rsqrt(jnp.mean(xc * xc, axis=-1, keepdims=True) + EPS)
    return xc * rstd, rstd


def _ln_bwd(dy, xhat, rstd, g):
    dxh = dy * g
    return rstd * (dxh - jnp.mean(dxh, axis=-1, keepdims=True) - xhat * jnp.mean(dxh * xhat, axis=-1, keepdims=True))


def _colsum(x):
    return jnp.sum(x, axis=0, keepdims=True)


def _split3(x):
    hi = x.astype(BF16)
    r = x - hi.astype(F32)
    mid = r.astype(BF16)
    lo = (r - mid.astype(F32)).astype(BF16)
    return hi, mid, lo


def _dot(a, b, dims):
    return lax.dot_general(a, b, (dims, ((), ())), preferred_element_type=F32)


NN = ((1,), (0,))
NT = ((1,), (1,))
TN = ((0,), (0,))


def _divisors(n, cands):
    return [c for c in cands if c <= n and n % c == 0]


def _pick_tiles(m, n, k, a_bytes, b_bytes, o_bytes, extra_bytes):
    best = None
    for tm in _divisors(m, (1024, 512, 256, 128)):
        for tn in _divisors(n, (1408, 1024, 768, 704, 512, 384, 256, 128)):
            if tn % 128:
                continue
            need = 2 * (tm * k * a_bytes + k * tn * b_bytes + tm * tn * (o_bytes + extra_bytes)) + tm * tn * 4
            if need <= VMEM_BUDGET and (best is None or tm * tn > best[0] * best[1]):
                best = (tm, tn)
    assert best is not None, (m, n, k)
    return best


def matmul(a, b, *, tb=False, bk=0, bias=None, res=None, dep=None, out_dtype=F32, name):
    m, k = a.shape
    n = b.shape[0] if tb else b.shape[1]
    assert b.shape[1] % k == 0 if tb else (b.shape[0] == k and bk == 0)
    tm, tn = _pick_tiles(m, n, k, a.dtype.itemsize, b.dtype.itemsize, jnp.dtype(out_dtype).itemsize,
                         4 if res is not None else 0)

    def body(*refs):
        a_ref, b_ref = refs[:2]
        o_ref = refs[-1]
        acc = _dot(a_ref[...].astype(BF16), b_ref[...].astype(BF16), NT if tb else NN)
        pos = 2
        if bias is not None:
            acc = acc + refs[pos][...]
            pos += 1
        if res is not None:
            acc = acc + refs[pos][...]
        o_ref[...] = acc.astype(out_dtype)

    in_specs = [pl.BlockSpec((tm, k), lambda i, j: (i, 0)),
                pl.BlockSpec((tn, k), lambda i, j: (j, bk)) if tb else pl.BlockSpec((k, tn), lambda i, j: (0, j))]
    args = [a, b]
    if bias is not None:
        in_specs.append(pl.BlockSpec((1, tn), lambda i, j: (0, j)))
        args.append(bias)
    if res is not None:
        in_specs.append(pl.BlockSpec((tm, tn), lambda i, j: (i, j)))
        args.append(res)
    if dep is not None:
        in_specs.append(pl.BlockSpec(memory_space=pl.ANY))
        args.append(dep)
    return pl.pallas_call(
        body, name=name, grid=(m // tm, n // tn), in_specs=in_specs,
        out_specs=pl.BlockSpec((tm, tn), lambda i, j: (i, j)),
        out_shape=jax.ShapeDtypeStruct((m, n), out_dtype),
        compiler_params=_params(("parallel", "parallel")))(*args)


def matmul_ta(a, b, *, dep=None, out_dtype=BF16, name):
    s, m = a.shape
    n = b.shape[1]
    assert b.shape[0] == s
    best = None
    for tm in _divisors(m, (512, 256, 128)):
        for tn in _divisors(n, (1024, 512, 384, 256, 128)):
            need = 2 * (s * tm * a.dtype.itemsize + s * tn * b.dtype.itemsize + tm * tn * 2) + tm * tn * 4
            if need <= VMEM_BUDGET and (best is None or tm * tn > best[0] * best[1]):
                best = (tm, tn)
    tm, tn = best

    def body(*refs):
        a_ref, b_ref, o_ref = refs[0], refs[1], refs[-1]
        o_ref[...] = _dot(a_ref[...].astype(BF16), b_ref[...].astype(BF16), TN).astype(out_dtype)

    in_specs = [pl.BlockSpec((s, tm), lambda i, j: (0, i)), pl.BlockSpec((s, tn), lambda i, j: (0, j))]
    args = [a, b]
    if dep is not None:
        in_specs.append(pl.BlockSpec(memory_space=pl.ANY))
        args.append(dep)
    return pl.pallas_call(
        body, name=name, grid=(m // tm, n // tn), in_specs=in_specs,
        out_specs=pl.BlockSpec((tm, tn), lambda i, j: (i, j)),
        out_shape=jax.ShapeDtypeStruct((m, n), out_dtype),
        compiler_params=_params(("parallel", "parallel")))(*args)


def rms_fwd(x, g, *, name):
    s, d = x.shape

    def body(x_ref, g_ref, h_ref):
        xv = x_ref[...]
        r = lax.rsqrt(jnp.mean(xv * xv, axis=-1, keepdims=True) + EPS)
        h_ref[...] = (xv * r * g_ref[...]).astype(BF16)

    return pl.pallas_call(
        body, name=name, grid=(s // ROWS,),
        in_specs=[pl.BlockSpec((ROWS, d), lambda i: (i, 0)), pl.BlockSpec((1, d), lambda i: (0, 0))],
        out_specs=pl.BlockSpec((ROWS, d), lambda i: (i, 0)),
        out_shape=jax.ShapeDtypeStruct((s, d), BF16),
        compiler_params=_params(("parallel",)))(x, g)


def rms_bwd(dh, x, g, dres, *, name):
    s, d = x.shape

    def body(dh_ref, x_ref, g_ref, dres_ref, dx_ref, dxb_ref, dg_ref):
        xv = x_ref[...]
        r = lax.rsqrt(jnp.mean(xv * xv, axis=-1, keepdims=True) + EPS)
        xhat = xv * r
        dhv = dh_ref[...]
        dxh = dhv * g_ref[...]
        dx = dres_ref[...] + r * (dxh - xhat * jnp.mean(dxh * xhat, axis=-1, keepdims=True))
        dx_ref[...] = dx
        dxb_ref[...] = dx.astype(BF16)

        @pl.when(pl.program_id(0) == 0)
        def _():
            dg_ref[...] = jnp.zeros_like(dg_ref)
        dg_ref[...] += _colsum(dhv * xhat)

    row = pl.BlockSpec((ROWS, d), lambda i: (i, 0))
    vec = pl.BlockSpec((1, d), lambda i: (0, 0))
    return pl.pallas_call(
        body, name=name, grid=(s // ROWS,),
        in_specs=[row, row, vec, row], out_specs=[row, row, vec],
        out_shape=[jax.ShapeDtypeStruct((s, d), F32), jax.ShapeDtypeStruct((s, d), BF16),
                   jax.ShapeDtypeStruct((1, d), F32)],
        compiler_params=_params(("arbitrary",)))(dh, x, g, dres)


def final_loss_bwd(x, g, tgt, *, name):
    s, d = x.shape

    def body(x_ref, g_ref, t_ref, loss_ref, dx_ref, dxb_ref, dg_ref):
        xv = x_ref[...]
        gv = g_ref[...]
        r = lax.rsqrt(jnp.mean(xv * xv, axis=-1, keepdims=True) + EPS)
        xhat = xv * r
        e = xhat * gv - t_ref[...]
        dy = e * (1.0 / d)
        dxh = dy * gv
        dx = r * (dxh - xhat * jnp.mean(dxh * xhat, axis=-1, keepdims=True))
        dx_ref[...] = dx
        dxb_ref[...] = dx.astype(BF16)

        @pl.when(pl.program_id(0) == 0)
        def _():
            dg_ref[...] = jnp.zeros_like(dg_ref)
            loss_ref[...] = jnp.zeros_like(loss_ref)
        dg_ref[...] += _colsum(dy * xhat)
        loss_ref[...] += 0.5 * jnp.sum(jnp.mean(e * e, axis=-1, keepdims=True))

    row = pl.BlockSpec((ROWS, d), lambda i: (i, 0))
    vec = pl.BlockSpec((1, d), lambda i: (0, 0))
    one = pl.BlockSpec((8, 128), lambda i: (0, 0))
    return pl.pallas_call(
        body, name=name, grid=(s // ROWS,),
        in_specs=[row, vec, row], out_specs=[one, row, row, vec],
        out_shape=[jax.ShapeDtypeStruct((8, 128), F32), jax.ShapeDtypeStruct((s, d), F32),
                   jax.ShapeDtypeStruct((s, d), BF16), jax.ShapeDtypeStruct((1, d), F32)],
        compiler_params=_params(("arbitrary",)))(x, g, tgt)


def _pair_masks():
    lane = lax.broadcasted_iota(jnp.int32, (CHUNK, 128), 1)
    return lane < HEAD


def _head_keep(rows):
    lane = lax.broadcasted_iota(jnp.int32, (rows, 128), 1)
    first = jnp.where(lane < HEAD, 1.0, 0.0)
    return first.astype(BF16), (1.0 - first).astype(BF16)


def _gating_mixed(vn_b, wm_ref, lo):
    rows, aw = vn_b.shape
    out = []
    for c in range(rows // CHUNK):
        tiles = []
        for p in range(aw // 128):
            vp = vn_b[c * CHUNK:(c + 1) * CHUNK, p * 128:(p + 1) * 128]
            r0 = _dot(wm_ref[2 * p], vp, NN)
            r1 = _dot(wm_ref[2 * p + 1], vp, NN)
            tiles.append(jnp.where(lo, r0, r1))
        out.append(jnp.concatenate(tiles, axis=1))
    return jnp.concatenate(out, axis=0)


def even_mid_fwd(z, vg, vb, wm, bm, cw, cb, cg, cbeta, *, name):
    s, zw = z.shape
    aw = zw // 4
    nblk = s // ROWS

    def body(z_ref, zp_ref, vg_ref, vb_ref, wm_ref, bm_ref, cw_ref, cb_ref, cg_ref, cbeta_ref,
             y_ref, hc_ref, ext_ref):
        i = pl.program_id(0)
        lo = _pair_masks()
        u = _gelu(z_ref[:, 0:aw])
        v = _gelu(z_ref[:, aw:2 * aw])
        vhat, _ = _ln_stats(v)
        vn = (vhat * vg_ref[...] + vb_ref[...]).astype(BF16)
        mixed = _gating_mixed(vn, wm_ref, lo)
        bias = jnp.concatenate([bm_ref[...]] * (ROWS // CHUNK), axis=0)
        y_ref[:, 0:aw] = (u * (mixed + bias)).astype(BF16)

        hb = z_ref[:, 2 * aw:3 * aw] * _sigmoid(z_ref[:, 3 * aw:4 * aw])
        hbp = zp_ref[:, 0:aw] * _sigmoid(zp_ref[:, aw:2 * aw])
        ext_ref[0:HALO, :] = jnp.where(i > 0, hbp, 0.0)
        ext_ref[HALO:HALO + ROWS, :] = hb
        acc = jnp.zeros((ROWS, aw), F32) + cb_ref[...]
        for k in range(CONV_K):
            acc = acc + cw_ref[k:k + 1, :] * ext_ref[pl.ds(HALO - (CONV_K - 1) + k, ROWS), :]
        hc_ref[...] = acc
        hhat, _ = _ln_stats(acc)
        hn = hhat * cg_ref[...] + cbeta_ref[...]
        y_ref[:, aw:2 * aw] = (hn * _sigmoid(hn)).astype(BF16)

    hb_per = ROWS // HALO
    vec = pl.BlockSpec((1, aw), lambda i: (0, 0))
    return pl.pallas_call(
        body, name=name, grid=(nblk,),
        in_specs=[pl.BlockSpec((ROWS, zw), lambda i: (i, 0)),
                  pl.BlockSpec((HALO, 2 * aw), lambda i: (jnp.maximum(i * hb_per - 1, 0), 1)),
                  vec, vec,
                  pl.BlockSpec(wm.shape, lambda i: (0, 0, 0)),
                  pl.BlockSpec((CHUNK, aw), lambda i: (0, 0)),
                  pl.BlockSpec((CONV_K, aw), lambda i: (0, 0)), vec, vec, vec],
        out_specs=[pl.BlockSpec((ROWS, 2 * aw), lambda i: (i, 0)), pl.BlockSpec((ROWS, aw), lambda i: (i, 0))],
        out_shape=[jax.ShapeDtypeStruct((s, 2 * aw), BF16), jax.ShapeDtypeStruct((s, aw), F32)],
        scratch_shapes=[pltpu.VMEM((HALO + ROWS, aw), F32)],
        compiler_params=_params(("parallel",)))(z, z, vg, vb, wm, bm, cw, cb, cg, cbeta)


def even_mid_bwd_rows(dy, z, hc, vg, vb, wm, wmt, bm, sel, cg, cbeta, *, name):
    s, zw = z.shape
    aw = zw // 4
    nh = wm.shape[0]

    def body(dy_ref, z_ref, hc_ref, vg_ref, vb_ref, wm_ref, wmt_ref, bm_ref, sel_ref, cg_ref, cbeta_ref,
             dza_ref, dhc_ref, dba_ref, dvg_ref, dvb_ref, dwm_ref, dbs_ref, dcg_ref, dcbeta_ref, dcb_ref):
        @pl.when(pl.program_id(0) == 0)
        def _():
            for r in (dba_ref, dvg_ref, dvb_ref, dwm_ref, dbs_ref, dcg_ref, dcbeta_ref, dcb_ref):
                r[...] = jnp.zeros_like(r)

        lo = _pair_masks()
        keep = _head_keep(CHUNK)
        zu = z_ref[:, 0:aw]
        zv = z_ref[:, aw:2 * aw]
        u = _gelu(zu)
        v = _gelu(zv)
        vhat, vrstd = _ln_stats(v)
        vn = (vhat * vg_ref[...] + vb_ref[...]).astype(BF16)
        mixed = _gating_mixed(vn, wm_ref, lo)
        bias = jnp.concatenate([bm_ref[...]] * (ROWS // CHUNK), axis=0)
        dya = dy_ref[:, 0:aw]
        du = dya * (mixed + bias)
        dmix = dya * u
        dmix_b = dmix.astype(BF16)

        dvn_rows = []
        for c in range(ROWS // CHUNK):
            rs = slice(c * CHUNK, (c + 1) * CHUNK)
            tiles = []
            for p in range(aw // 128):
                cs = slice(p * 128, (p + 1) * 128)
                dm = dmix_b[rs, cs]
                dm0 = dm * keep[0]
                dm1 = dm * keep[1]
                vp = vn[rs, cs]
                tiles.append(_dot(wmt_ref[2 * p], dm0, NN) + _dot(wmt_ref[2 * p + 1], dm1, NN))
                dwm_ref[2 * p] += _dot(dm0, vp, NT)
                dwm_ref[2 * p + 1] += _dot(dm1, vp, NT)
            dvn_rows.append(jnp.concatenate(tiles, axis=1))
            acc = jnp.zeros((CHUNK, 128), F32)
            for part in _split3(dmix[rs, :]):
                acc = acc + _dot(part, sel_ref[...], NN)
            dbs_ref[...] += acc
        dvn = jnp.concatenate(dvn_rows, axis=0)
        dvg_ref[...] += _colsum(dvn * vhat)
        dvb_ref[...] += _colsum(dvn)
        dv = _ln_bwd(dvn, vhat, vrstd, vg_ref[...])
        dzu = du * _gelu_grad(zu)
        dzv = dv * _gelu_grad(zv)
        dza_ref[:, 0:aw] = dzu.astype(BF16)
        dza_ref[:, aw:2 * aw] = dzv.astype(BF16)
        dba_ref[:, 0:aw] += _colsum(dzu)
        dba_ref[:, aw:2 * aw] += _colsum(dzv)

        hcv = hc_ref[...]
        hhat, hrstd = _ln_stats(hcv)
        hn = hhat * cg_ref[...] + cbeta_ref[...]
        sg = _sigmoid(hn)
        dhn = dy_ref[:, aw:2 * aw] * (sg * (1.0 + hn * (1.0 - sg)))
        dcg_ref[...] += _colsum(dhn * hhat)
        dcbeta_ref[...] += _colsum(dhn)
        dhc = _ln_bwd(dhn, hhat, hrstd, cg_ref[...])
        dhc_ref[...] = dhc
        dcb_ref[...] += _colsum(dhc)

    vec = pl.BlockSpec((1, aw), lambda i: (0, 0))
    vec2 = pl.BlockSpec((1, 2 * aw), lambda i: (0, 0))
    w3 = pl.BlockSpec(wm.shape, lambda i: (0, 0, 0))
    sq = pl.BlockSpec((CHUNK, 128), lambda i: (0, 0))
    return pl.pallas_call(
        body, name=name, grid=(s // ROWS,),
        in_specs=[pl.BlockSpec((ROWS, 2 * aw), lambda i: (i, 0)), pl.BlockSpec((ROWS, 2 * aw), lambda i: (i, 0)),
                  pl.BlockSpec((ROWS, aw), lambda i: (i, 0)), vec, vec, w3, w3,
                  pl.BlockSpec((CHUNK, aw), lambda i: (0, 0)), pl.BlockSpec((aw, 128), lambda i: (0, 0)), vec, vec],
        out_specs=[pl.BlockSpec((ROWS, 2 * aw), lambda i: (i, 0)), pl.BlockSpec((ROWS, aw), lambda i: (i, 0)),
                   vec2, vec, vec, w3, sq, vec, vec, vec],
        out_shape=[jax.ShapeDtypeStruct((s, 2 * aw), BF16), jax.ShapeDtypeStruct((s, aw), F32),
                   jax.ShapeDtypeStruct((1, 2 * aw), F32), jax.ShapeDtypeStruct((1, aw), F32),
                   jax.ShapeDtypeStruct((1, aw), F32), jax.ShapeDtypeStruct(wm.shape, F32),
                   jax.ShapeDtypeStruct((CHUNK, 128), F32), jax.ShapeDtypeStruct((1, aw), F32),
                   jax.ShapeDtypeStruct((1, aw), F32), jax.ShapeDtypeStruct((1, aw), F32)],
        compiler_params=_params(("arbitrary",)))(dy, z, hc, vg, vb, wm, wmt, bm, sel, cg, cbeta)


def even_conv_bwd(dhc, z, cw, *, name):
    s, zw = z.shape
    aw = zw // 4
    nblk = s // ROWS
    hb_per = ROWS // HALO

    def body(dc_ref, dn_ref, z_ref, zp_ref, cw_ref, dzb_ref, dbb_ref, dcw_ref, exth_ref, extd_ref):
        i = pl.program_id(0)

        @pl.when(i == 0)
        def _():
            dbb_ref[...] = jnp.zeros_like(dbb_ref)
            dcw_ref[...] = jnp.zeros_like(dcw_ref)

        a = z_ref[:, 0:aw]
        sg = _sigmoid(z_ref[:, aw:2 * aw])
        exth_ref[0:HALO, :] = jnp.where(i > 0, zp_ref[:, 0:aw] * _sigmoid(zp_ref[:, aw:2 * aw]), 0.0)
        exth_ref[HALO:HALO + ROWS, :] = a * sg
        dcur = dc_ref[...]
        extd_ref[0:ROWS, :] = dcur
        extd_ref[ROWS:ROWS + HALO, :] = jnp.where(i < nblk - 1, dn_ref[...], 0.0)
        dhb = jnp.zeros((ROWS, aw), F32)
        for k in range(CONV_K):
            wk = cw_ref[k:k + 1, :]
            dhb = dhb + wk * extd_ref[pl.ds(CONV_K - 1 - k, ROWS), :]
            dcw_ref[k:k + 1, :] += _colsum(dcur * exth_ref[pl.ds(HALO - (CONV_K - 1) + k, ROWS), :])
        da = dhb * sg
        dg = dhb * a * sg * (1.0 - sg)
        dzb_ref[:, 0:aw] = da.astype(BF16)
        dzb_ref[:, aw:2 * aw] = dg.astype(BF16)
        dbb_ref[:, 0:aw] += _colsum(da)
        dbb_ref[:, aw:2 * aw] += _colsum(dg)

    return pl.pallas_call(
        body, name=name, grid=(nblk,),
        in_specs=[pl.BlockSpec((ROWS, aw), lambda i: (i, 0)),
                  pl.BlockSpec((HALO, aw), lambda i: (jnp.minimum((i + 1) * hb_per, nblk * hb_per - 1), 0)),
                  pl.BlockSpec((ROWS, 2 * aw), lambda i: (i, 1)),
                  pl.BlockSpec((HALO, 2 * aw), lambda i: (jnp.maximum(i * hb_per - 1, 0), 1)),
                  pl.BlockSpec((CONV_K, aw), lambda i: (0, 0))],
        out_specs=[pl.BlockSpec((ROWS, 2 * aw), lambda i: (i, 0)), pl.BlockSpec((1, 2 * aw), lambda i: (0, 0)),
                   pl.BlockSpec((CONV_K, aw), lambda i: (0, 0))],
        out_shape=[jax.ShapeDtypeStruct((s, 2 * aw), BF16), jax.ShapeDtypeStruct((1, 2 * aw), F32),
                   jax.ShapeDtypeStruct((CONV_K, aw), F32)],
        scratch_shapes=[pltpu.VMEM((HALO + ROWS, aw), F32), pltpu.VMEM((ROWS + HALO, aw), F32)],
        compiler_params=_params(("arbitrary",)))(dhc, dhc, z, z, cw)


FFN_ROWS = 1024
FFN_CHUNK = 16


def _ffn_tile(f):
    for t in (256, 128):
        if f % t == 0:
            return t
    raise ValueError(f)


def _taps(ext_ref, w, b, r0, rows):
    acc = b
    for k in range(FFN_K):
        acc = acc + w[k] * ext_ref[pl.ds(FHALO - (FFN_K - 1) + k + r0, rows), :]
    return acc


def ffn_mid_fwd(up_g, up_v, cw, cb, *, name):
    s, f = up_g.shape
    tn = _ffn_tile(f)
    nj = f // tn
    ROWS = FFN_ROWS
    per = ROWS // FHALO

    def body(ug_ref, uv_ref, pg_ref, pv_ref, wg_ref, wv_ref, bg_ref, bv_ref, act_ref, cg_ref, cv_ref, eg_ref, ev_ref):
        i = pl.program_id(0)
        for cur_ref, prev_ref, ext_ref in ((ug_ref, pg_ref, eg_ref), (uv_ref, pv_ref, ev_ref)):
            ext_ref[0:FHALO, :] = jnp.where(i > 0, prev_ref[...], 0.0)
            ext_ref[FHALO:FHALO + ROWS, :] = cur_ref[...]
        wg = [wg_ref[k:k + 1, :] for k in range(FFN_K)]
        wv = [wv_ref[k:k + 1, :] for k in range(FFN_K)]
        bg, bv = bg_ref[...], bv_ref[...]
        for r0 in range(0, ROWS, FFN_CHUNK):
            rows = pl.ds(r0, FFN_CHUNK)
            gate = _taps(eg_ref, wg, bg, r0, FFN_CHUNK)
            val = _taps(ev_ref, wv, bv, r0, FFN_CHUNK)
            cg_ref[rows, :] = gate
            cv_ref[rows, :] = val
            act_ref[rows, :] = (gate * _sigmoid(gate) * val).astype(BF16)

    cur = lambda off: pl.BlockSpec((ROWS, tn), lambda i, j: (i, j + off))
    prev = lambda off: pl.BlockSpec((FHALO, tn), lambda i, j: (jnp.maximum(i * per - 1, 0), j + off))
    wsp = lambda off: pl.BlockSpec((FFN_K, tn), lambda i, j: (0, j + off))
    bsp = lambda off: pl.BlockSpec((1, tn), lambda i, j: (0, j + off))
    return pl.pallas_call(
        body, name=name, grid=(s // ROWS, nj),
        in_specs=[cur(0), cur(0), prev(0), prev(0), wsp(0), wsp(nj), bsp(0), bsp(nj)],
        out_specs=[cur(0), cur(0), cur(0)],
        out_shape=[jax.ShapeDtypeStruct((s, f), BF16), jax.ShapeDtypeStruct((s, f), F32),
                   jax.ShapeDtypeStruct((s, f), F32)],
        scratch_shapes=[pltpu.VMEM((FHALO + ROWS, tn), F32), pltpu.VMEM((FHALO + ROWS, tn), F32)],
        compiler_params=_params(("parallel", "parallel")))(up_g, up_v, up_g, up_v, cw, cw, cb, cb)


def ffn_mid_bwd(dact, up_g, up_v, conv_g, conv_v, cw, *, name):
    s, f = up_g.shape
    tn = _ffn_tile(f)
    nj = f // tn
    ROWS = FFN_ROWS
    nblk = s // ROWS
    per = ROWS // FHALO
    ext = ROWS + FHALO

    def body(da_ref, dan_ref, ug_ref, uv_ref, cg_ref, cv_ref, cgn_ref, cvn_ref, wg_ref, wv_ref,
             dug_ref, duv_ref, dwg_ref, dwv_ref, dbg_ref, dbv_ref, dg_ref, dv_ref):
        i = pl.program_id(1)

        @pl.when(i == 0)
        def _():
            for r in (dwg_ref, dwv_ref, dbg_ref, dbv_ref):
                r[...] = jnp.zeros_like(r)

        wg = [wg_ref[k:k + 1, :] for k in range(FFN_K)]
        wv = [wv_ref[k:k + 1, :] for k in range(FFN_K)]

        for r0, rows in [(r, FFN_CHUNK) for r in range(0, ROWS, FFN_CHUNK)] + [(ROWS, FHALO)]:
            if r0 < ROWS:
                gate, val, da = cg_ref[pl.ds(r0, rows), :], cv_ref[pl.ds(r0, rows), :], da_ref[pl.ds(r0, rows), :]
            else:
                gate, val, da = cgn_ref[...], cvn_ref[...], jnp.where(i < nblk - 1, dan_ref[...], 0.0)
            sg = _sigmoid(gate)
            dg_ref[pl.ds(r0, rows), :] = da * val * (sg * (1.0 + gate * (1.0 - sg)))
            dv_ref[pl.ds(r0, rows), :] = da * (gate * sg)

        def back(d_ref, w, u_ref, du_ref, dw_ref, db_ref):
            zero = jnp.zeros((FFN_CHUNK, tn), F32)
            acc = [zero] * FFN_K
            accb = zero
            for r0 in range(0, ROWS, FFN_CHUNK):
                d = [d_ref[pl.ds(r0 + FFN_K - 1 - k, FFN_CHUNK), :] for k in range(FFN_K)]
                u = u_ref[pl.ds(r0, FFN_CHUNK), :]
                du = w[0] * d[0]
                for k in range(1, FFN_K):
                    du = du + w[k] * d[k]
                du_ref[pl.ds(r0, FFN_CHUNK), :] = du.astype(BF16)
                acc = [acc[k] + u * d[k] for k in range(FFN_K)]
                accb = accb + d[FFN_K - 1]
            for k in range(FFN_K):
                dw_ref[k:k + 1, :] += _colsum(acc[k])
            db_ref[...] += _colsum(accb)

        back(dg_ref, wg, ug_ref, dug_ref, dwg_ref, dbg_ref)
        back(dv_ref, wv, uv_ref, duv_ref, dwv_ref, dbv_ref)

    cur = pl.BlockSpec((ROWS, tn), lambda j, i: (i, j))
    nxt = pl.BlockSpec((FHALO, tn), lambda j, i: (jnp.minimum((i + 1) * per, nblk * per - 1), j))
    wsp = lambda off: pl.BlockSpec((FFN_K, tn), lambda j, i: (0, j + off))
    bsp = pl.BlockSpec((1, tn), lambda j, i: (0, j))
    outs = pl.pallas_call(
        body, name=name, grid=(nj, nblk),
        in_specs=[cur, nxt, cur, cur, cur, cur, nxt, nxt, wsp(0), wsp(nj)],
        out_specs=[cur, cur, wsp(0), wsp(0), bsp, bsp],
        out_shape=[jax.ShapeDtypeStruct((s, f), BF16), jax.ShapeDtypeStruct((s, f), BF16),
                   jax.ShapeDtypeStruct((FFN_K, f), F32), jax.ShapeDtypeStruct((FFN_K, f), F32),
                   jax.ShapeDtypeStruct((1, f), F32), jax.ShapeDtypeStruct((1, f), F32)],
        scratch_shapes=[pltpu.VMEM((ext, tn), F32), pltpu.VMEM((ext, tn), F32)],
        compiler_params=_params(("parallel", "arbitrary")))(dact, dact, up_g, up_v, conv_g, conv_v, conv_g, conv_v,
                                                            cw, cw)
    dug, duv, dwg, dwv, dbg, dbv = outs
    return dug, duv, jnp.concatenate([dwg, dwv], axis=1), jnp.concatenate([dbg, dbv], axis=1)


def rope_tables(s):
    half = HEAD // 2
    lane = jnp.arange(128)
    j = lane % HEAD
    inv = ROPE_THETA ** (-(j % half).astype(F32) / half)
    ang = jnp.arange(s, dtype=F32)[:, None] * inv[None, :]
    sign = jnp.where(j < half, -1.0, 1.0).astype(F32)
    return jnp.cos(ang), jnp.sin(ang) * sign[None, :]


def _swap_halves(x):
    lane = lax.broadcasted_iota(jnp.int32, x.shape, 1)
    return jnp.where((lane % HEAD) < HEAD // 2, pltpu.roll(x, 128 - HEAD // 2, 1), pltpu.roll(x, HEAD // 2, 1))


def rope_fwd(qkv, cos, sin, *, name):
    s, d3 = qkv.shape
    d = d3 // 3
    scale = HEAD ** -0.5

    def body(xq_ref, xk_ref, c_ref, s_ref, q_ref, k_ref):
        c = c_ref[...]
        sn = s_ref[...]
        for t in range(d // 128):
            cs = slice(t * 128, (t + 1) * 128)
            xq = xq_ref[:, cs]
            xk = xk_ref[:, cs]
            q_ref[:, cs] = (xq * c + _swap_halves(xq) * sn) * scale
            k_ref[:, cs] = xk * c + _swap_halves(xk) * sn

    row = pl.BlockSpec((ROWS, d), lambda i: (i, 0))
    tab = pl.BlockSpec((ROWS, 128), lambda i: (i, 0))
    return pl.pallas_call(
        body, name=name, grid=(s // ROWS,),
        in_specs=[row, pl.BlockSpec((ROWS, d), lambda i: (i, 1)), tab, tab],
        out_specs=[row, row],
        out_shape=[jax.ShapeDtypeStruct((s, d), F32)] * 2,
        compiler_params=_params(("parallel",)))(qkv, qkv, cos, sin)


def rope_bwd(dq, dk, dv, cos, sin, *, name):
    s, d = dq.shape
    scale = HEAD ** -0.5

    def body(dq_ref, dk_ref, dv_ref, c_ref, s_ref, o_ref):
        c = c_ref[...]
        sn = s_ref[...]
        for t in range(d // 128):
            cs = slice(t * 128, (t + 1) * 128)
            gq = dq_ref[:, cs] * scale
            gk = dk_ref[:, cs]
            o_ref[:, t * 128:(t + 1) * 128] = (gq * c + _swap_halves(gq * sn)).astype(BF16)
            o_ref[:, d + t * 128:d + (t + 1) * 128] = (gk * c + _swap_halves(gk * sn)).astype(BF16)
        o_ref[:, 2 * d:3 * d] = dv_ref[...].astype(BF16)

    row = pl.BlockSpec((ROWS, d), lambda i: (i, 0))
    tab = pl.BlockSpec((ROWS, 128), lambda i: (i, 0))
    return pl.pallas_call(
        body, name=name, grid=(s // ROWS,),
        in_specs=[row, row, row, tab, tab],
        out_specs=pl.BlockSpec((ROWS, 3 * d), lambda i: (i, 0)),
        out_shape=jax.ShapeDtypeStruct((s, 3 * d), BF16),
        compiler_params=_params(("parallel",)))(dq, dk, dv, cos, sin)


ATT_T = BLOCK * max(DILATIONS)


FWD_GROUP = 2
ATT_GROUP = 4
FWD_QROWS = 128
BWD_QROWS = 64


def _unit_rows(r, j, dil):
    start = r + dil * BLOCK * j
    return pl.ds(start, BLOCK) if dil == 1 else pl.ds(start, BLOCK, stride=dil)


def _units():
    for bi, dil in enumerate(DILATIONS):
        nsub = ATT_T // (BLOCK * dil)
        for r in range(dil):
            for j in range(nsub):
                yield bi, dil, nsub, r, j


def _band(first_block, part, qrows):
    qi = lax.broadcasted_iota(jnp.int32, (qrows, 2 * BLOCK), 0) + part * qrows
    kj = lax.broadcasted_iota(jnp.int32, (qrows, 2 * BLOCK), 1)
    dist = BLOCK + qi - kj
    band = (dist >= 0) & (dist <= BLOCK)
    return band, band & (jnp.logical_not(first_block) | (kj >= BLOCK))


def _col(tile, h):
    return tile[:, h * HEAD:h * HEAD + 1]


def _keys(cur_ref, prev_ref, r, j, dil, nsub):
    cur = cur_ref[_unit_rows(r, j, dil), :]
    prev = cur_ref[_unit_rows(r, j - 1, dil), :] if j > 0 else prev_ref[_unit_rows(r, nsub - 1, dil), :]
    return jnp.concatenate([prev, cur], axis=0).astype(BF16)


def _att_specs(d, col_off=0):
    nt_cols = d // 128
    cur = pl.BlockSpec((ATT_T, 128), lambda n, p: (n, p + col_off))
    prv = pl.BlockSpec((ATT_T, 128), lambda n, p: (jnp.maximum(n - 1, 0), p + col_off))
    return cur, prv


def attn_fwd(q, k, qkv, *, name):
    s, d = q.shape
    nt = s // ATT_T

    def body(q_ref, kc_ref, kp_ref, vc_ref, vp_ref, o_ref, lse_ref, acc_ref, m_ref, l_ref):
        n = pl.program_id(0)
        QROWS = FWD_QROWS
        nparts = BLOCK // QROWS
        bands = [_band(n == 0, part, QROWS) for part in range(nparts)]
        lo = _pair_masks()
        keep = _head_keep(BLOCK)
        nb = len(DILATIONS)
        tile = lambda cols: jnp.where(lo, jnp.concatenate(cols[:nparts], axis=0),
                                      jnp.concatenate(cols[nparts:], axis=0))

        def scores(unit):
            bi, dil, nsub, r, j = unit
            rows = _unit_rows(r, j, dil)
            kw = _keys(kc_ref, kp_ref, r, j, dil, nsub)
            vw = _keys(vc_ref, vp_ref, r, j, dil, nsub)
            qp = q_ref[rows, :].astype(BF16)
            sc2 = _dot(jnp.concatenate([qp * keep[0], qp * keep[1]], axis=0), kw, NT)
            return dict(bi=bi, j=j, rows=rows, vw=vw, sc2=sc2, m_old=m_ref[rows, :] if bi > 0 else None)

        def softmax(u):
            prs, new_m, new_l, alpha = [], [], [], []
            for c in range(2 * nparts):
                h, part = divmod(c, nparts)
                valid = bands[part][0 if u['j'] > 0 else 1]
                sc = jnp.where(valid, u['sc2'][c * QROWS:(c + 1) * QROWS], NEG)
                mx = jnp.max(sc, axis=-1, keepdims=True)
                if u['bi'] == 0:
                    m_new = mx
                else:
                    m_old = _col(u['m_old'][part * QROWS:(part + 1) * QROWS], h)
                    m_new = jnp.maximum(m_old, mx)
                    alpha.append(jnp.exp(m_old - m_new))
                pr = jnp.exp(sc - m_new)
                new_m.append(m_new)
                new_l.append(jnp.sum(pr, axis=-1, keepdims=True))
                prs.append(pr.astype(BF16))
            u.update(pr2=jnp.concatenate(prs, axis=0), new_m=new_m, new_l=new_l, alpha=alpha)

        def combine(u):
            rows, bi = u['rows'], u['bi']
            pv2 = _dot(u['pr2'], u['vw'], NN)
            m_t = tile(u['new_m'])
            l_t = tile(u['new_l'])
            acc_t = jnp.where(lo, pv2[:BLOCK], pv2[BLOCK:])
            if bi > 0:
                a_t = tile(u['alpha'])
                l_t = a_t * l_ref[rows, :] + l_t
                acc_t = a_t * acc_ref[rows, :] + acc_t
            if bi == nb - 1:
                o_ref[rows, :] = acc_t / l_t
                lse_ref[rows, :] = m_t + jnp.log(l_t)
            else:
                acc_ref[rows, :] = acc_t
                m_ref[rows, :] = m_t
                l_ref[rows, :] = l_t

        units = list(_units())
        for first in range(0, len(units), FWD_GROUP):
            pair = [scores(u) for u in units[first:first + FWD_GROUP]]
            for u in pair:
                softmax(u)
            for u in pair:
                combine(u)

    cur, prv = _att_specs(d)
    vcur, vprv = _att_specs(d, 2 * (d // 128))
    return pl.pallas_call(
        body, name=name, grid=(nt, d // 128), in_specs=[cur, cur, prv, vcur, vprv], out_specs=[cur, cur],
        out_shape=[jax.ShapeDtypeStruct((s, d), F32)] * 2,
        scratch_shapes=[pltpu.VMEM((ATT_T, 128), F32)] * 3,
        compiler_params=_params(("parallel", "parallel")))(q, k, k, qkv, qkv)


def attn_delta(do, o, *, name):
    s, d = do.shape

    def body(do_ref, o_ref, dl_ref):
        lane = lax.broadcasted_iota(jnp.int32, (ROWS, 128), 1)
        lo = lane < HEAD
        for p in range(d // 128):
            cs = slice(p * 128, (p + 1) * 128)
            pr = do_ref[:, cs] * o_ref[:, cs]
            s0 = jnp.sum(jnp.where(lo, pr, 0.0), axis=-1, keepdims=True)
            s1 = jnp.sum(jnp.where(lo, 0.0, pr), axis=-1, keepdims=True)
            dl_ref[:, cs] = jnp.where(lo, s0, s1)

    row = pl.BlockSpec((ROWS, d), lambda i: (i, 0))
    return pl.pallas_call(
        body, name=name, grid=(s // ROWS,), in_specs=[row, row], out_specs=row,
        out_shape=jax.ShapeDtypeStruct((s, d), F32),
        compiler_params=_params(("parallel",)))(do, o)


def attn_dq(q, k, qkv, do, lse, delta, *, name):
    s, d = q.shape
    nt = s // ATT_T

    def body(q_ref, kc_ref, kp_ref, vc_ref, vp_ref, do_ref, l_ref, dl_ref, dq_ref):
        n = pl.program_id(0)
        QROWS = BWD_QROWS
        nparts = BLOCK // QROWS
        bands = [_band(n == 0, part, QROWS) for part in range(nparts)]
        lo = _pair_masks()
        keep = _head_keep(BLOCK)
        def scores(unit):
            bi, dil, nsub, r, j = unit
            rows = _unit_rows(r, j, dil)
            kw = _keys(kc_ref, kp_ref, r, j, dil, nsub)
            vw = _keys(vc_ref, vp_ref, r, j, dil, nsub)
            qp = q_ref[rows, :].astype(BF16)
            dop = do_ref[rows, :].astype(BF16)
            sc2 = _dot(jnp.concatenate([qp * keep[0], qp * keep[1]], axis=0), kw, NT)
            dp2 = _dot(jnp.concatenate([dop * keep[0], dop * keep[1]], axis=0), vw, NT)
            return dict(bi=bi, j=j, rows=rows, kw=kw, sc2=sc2, dp2=dp2, lt=l_ref[rows, :], dt=dl_ref[rows, :])

        def softmax_bwd(u):
            dss = []
            for c in range(2 * nparts):
                h, part = divmod(c, nparts)
                valid = bands[part][0 if u['j'] > 0 else 1]
                cr = slice(c * QROWS, (c + 1) * QROWS)
                pr_rows = slice(part * QROWS, (part + 1) * QROWS)
                pr = jnp.where(valid, jnp.exp(u['sc2'][cr] - _col(u['lt'][pr_rows], h)), 0.0)
                dss.append((pr * (u['dp2'][cr] - _col(u['dt'][pr_rows], h))).astype(BF16))
            u['ds2'] = jnp.concatenate(dss, axis=0)

        def combine(u):
            rows = u['rows']
            dq2 = _dot(u['ds2'], u['kw'], NN)
            dq_t = jnp.where(lo, dq2[:BLOCK], dq2[BLOCK:])
            if u['bi'] > 0:
                dq_t = dq_t + dq_ref[rows, :]
            dq_ref[rows, :] = dq_t

        units = list(_units())
        for first in range(0, len(units), ATT_GROUP):
            pair = [scores(u) for u in units[first:first + ATT_GROUP]]
            for u in pair:
                softmax_bwd(u)
            for u in pair:
                combine(u)

    cur, prv = _att_specs(d)
    vcur, vprv = _att_specs(d, 2 * (d // 128))
    return pl.pallas_call(
        body, name=name, grid=(nt, d // 128), in_specs=[cur, cur, prv, vcur, vprv, cur, cur, cur], out_specs=cur,
        out_shape=jax.ShapeDtypeStruct((s, d), F32),
        compiler_params=_params(("parallel", "parallel")))(q, k, k, qkv, qkv, do, lse, delta)


def attn_dkv(q, k, qkv, do, lse, delta, *, name):
    s, d = q.shape
    nt = s // ATT_T

    def body(k_ref, v_ref, qc_ref, qn_ref, doc_ref, don_ref, lc_ref, ln_ref, dc_ref, dn_ref, dk_ref, dv_ref):
        n = pl.program_id(0)
        qi = lax.broadcasted_iota(jnp.int32, (BLOCK, BLOCK), 0)
        kj = lax.broadcasted_iota(jnp.int32, (BLOCK, BLOCK), 1)
        own = kj <= qi
        nxt = kj >= qi
        nxt_edge = nxt & (n < nt - 1)
        keep = _head_keep(BLOCK)
        lo = _pair_masks()
        def scores(unit):
            bi, dil, nsub, r, j = unit
            rows = _unit_rows(r, j, dil)
            inner = j + 1 < nsub
            nrows = _unit_rows(r, j + 1, dil) if inner else _unit_rows(r, 0, dil)
            kp = k_ref[rows, :].astype(BF16)
            vp = v_ref[rows, :].astype(BF16)
            far = not inner
            take = lambda c_ref, n_ref, nx: ((n_ref if far else c_ref)[nrows, :] if nx else c_ref[rows, :])
            qs = [take(qc_ref, qn_ref, nx).astype(BF16) for nx in (False, True)]
            dos = [take(doc_ref, don_ref, nx).astype(BF16) for nx in (False, True)]
            lts = [take(lc_ref, ln_ref, nx) for nx in (False, True)]
            dts = [take(dc_ref, dn_ref, nx) for nx in (False, True)]
            q4 = jnp.concatenate([qs[nx] * keep[h] for h in range(2) for nx in range(2)], axis=0)
            do4 = jnp.concatenate([dos[nx] * keep[h] for h in range(2) for nx in range(2)], axis=0)
            return dict(bi=bi, rows=rows, q4=q4, do4=do4, s4=_dot(q4, kp, NT), dp4=_dot(do4, vp, NT), lts=lts,
                        dts=dts, valids=(own, nxt if inner else nxt_edge))

        def softmax_bwd(u):
            prs, dss = [], []
            for c in range(4):
                h, nx = divmod(c, 2)
                cr = slice(c * BLOCK, (c + 1) * BLOCK)
                pr = jnp.where(u['valids'][nx], jnp.exp(u['s4'][cr] - _col(u['lts'][nx], h)), 0.0)
                prs.append(pr.astype(BF16))
                dss.append((pr * (u['dp4'][cr] - _col(u['dts'][nx], h))).astype(BF16))
            u.update(pr4=jnp.concatenate(prs, axis=0), ds4=jnp.concatenate(dss, axis=0))

        def combine(u):
            rows = u['rows']
            dv_t = _dot(u['pr4'], u['do4'], TN)
            dk_t = _dot(u['ds4'], u['q4'], TN)
            if u['bi'] > 0:
                dk_t = dk_t + dk_ref[rows, :]
                dv_t = dv_t + dv_ref[rows, :]
            dk_ref[rows, :] = dk_t
            dv_ref[rows, :] = dv_t

        units = list(_units())
        for first in range(0, len(units), ATT_GROUP):
            pair = [scores(u) for u in units[first:first + ATT_GROUP]]
            for u in pair:
                softmax_bwd(u)
            for u in pair:
                combine(u)

    cur = pl.BlockSpec((ATT_T, 128), lambda n, p: (n, p))
    nxt_spec = pl.BlockSpec((ATT_T, 128), lambda n, p: (jnp.minimum(n + 1, nt - 1), p))
    vcur = pl.BlockSpec((ATT_T, 128), lambda n, p: (n, p + 2 * (d // 128)))
    return pl.pallas_call(
        body, name=name, grid=(nt, d // 128),
        in_specs=[cur, vcur, cur, nxt_spec, cur, nxt_spec, cur, nxt_spec, cur, nxt_spec], out_specs=[cur, cur],
        out_shape=[jax.ShapeDtypeStruct((s, d), F32)] * 2,
        compiler_params=_params(("parallel", "parallel")))(k, qkv, q, q, do, do, lse, lse, delta, delta)


def adamw(parts_list, w, m, v, *, name):
    nk = len(parts_list)
    npart, rk, c = parts_list[0].shape
    r = rk * nk
    assert w.shape == (r, c)
    tr = next(t for t in range(rk, 0, -8) if rk % t == 0 and (t * c * 4 <= 1024 * 1024 or t == 8))
    nbk = rk // tr

    def body(*refs):
        p_refs = refs[:nk]
        w_ref, m_ref, v_ref, g_ref, d_ref, nm_ref, nv_ref = refs[nk:]
        i = pl.program_id(0)
        g = None
        for kk, p_ref in enumerate(p_refs):
            gk = p_ref[0].astype(F32)
            for j in range(1, npart):
                gk = gk + p_ref[j].astype(F32)
            g = gk if g is None else jnp.where(i >= kk * nbk, gk, g)
        m2 = B1 * m_ref[...] + (1.0 - B1) * g
        v2 = B2 * v_ref[...] + (1.0 - B2) * (g * g)
        m_hat = m2 / (1.0 - B1 ** STEP)
        v_hat = v2 / (1.0 - B2 ** STEP)
        g_ref[...] = g
        d_ref[...] = -LR * (m_hat / (jnp.sqrt(v_hat) + ADAM_EPS) + WD * w_ref[...])
        nm_ref[...] = m2
        nv_ref[...] = v2

    blk = pl.BlockSpec((tr, c), lambda i: (i, 0))
    pspec = lambda kk: pl.BlockSpec((npart, tr, c), lambda i: (0, jnp.clip(i - kk * nbk, 0, nbk - 1), 0))
    return pl.pallas_call(
        body, name=name, grid=(r // tr,),
        in_specs=[pspec(kk) for kk in range(nk)] + [blk, blk, blk],
        out_specs=[blk] * 4, out_shape=[jax.ShapeDtypeStruct((r, c), F32)] * 4,
        compiler_params=_params(("parallel",)))(*parts_list, w, m, v)


def _my_index():
    return 4 * lax.axis_index("x") + 2 * lax.axis_index("y") + lax.axis_index("c")


def exchange(arrays, scatter, *, name):
    nt = len(arrays)

    def body(*refs):
        ins = refs[:nt]
        outs = refs[nt:2 * nt]
        send_sems, recv_sems, local_sems = refs[2 * nt:]
        x, y, c = lax.axis_index("x"), lax.axis_index("y"), lax.axis_index("c")
        me = 4 * x + 2 * y + c
        copies = []
        for t in range(nt):
            src = ins[t].at[me] if scatter[t] else ins[t]
            cp = pltpu.make_async_copy(src, outs[t].at[me], local_sems.at[t])
            cp.start()
            copies.append(cp)
        remote = []
        for kk in range(1, N_DEV):
            px, py, pc = x ^ (kk >> 2), y ^ ((kk >> 1) & 1), c ^ (kk & 1)
            peer = 4 * px + 2 * py + pc
            for t in range(nt):
                src = ins[t].at[peer] if scatter[t] else ins[t]
                cp = pltpu.make_async_remote_copy(
                    src_ref=src, dst_ref=outs[t].at[me], send_sem=send_sems.at[t, kk], recv_sem=recv_sems.at[t, kk],
                    device_id=(px, py, pc), device_id_type=pl.DeviceIdType.MESH)
                cp.start()
                remote.append(cp)
        for cp in remote:
            cp.wait()
        for cp in copies:
            cp.wait()

    hbm = pl.BlockSpec(memory_space=pl.ANY)
    out_shape = [jax.ShapeDtypeStruct(a.shape if scatter[t] else (N_DEV,) + a.shape, a.dtype)
                 for t, a in enumerate(arrays)]
    return pl.pallas_call(
        body, name=name, in_specs=[hbm] * nt, out_specs=[hbm] * nt, out_shape=out_shape,
        scratch_shapes=[pltpu.SemaphoreType.DMA((nt, N_DEV)), pltpu.SemaphoreType.DMA((nt, N_DEV)),
                        pltpu.SemaphoreType.DMA((nt,))],
        compiler_params=pltpu.CompilerParams(has_side_effects=True))(*arrays)


def _peer_of(kk):
    x, y, c = lax.axis_index("x"), lax.axis_index("y"), lax.axis_index("c")
    return x ^ (kk >> 2), y ^ ((kk >> 1) & 1), c ^ (kk & 1)


def _peer_copy(t, kk, scatter, ins, lands, send_sems, recv_sems):
    px, py, pc = _peer_of(kk)
    me = _my_index()
    src = ins[t].at[4 * px + 2 * py + pc] if scatter[t] else ins[t]
    return pltpu.make_async_remote_copy(
        src_ref=src, dst_ref=lands[t].at[me], send_sem=send_sems.at[t * N_DEV + kk],
        recv_sem=recv_sems.at[t * N_DEV + kk], device_id=(px, py, pc), device_id_type=pl.DeviceIdType.MESH)


def _own_copy(t, scatter, ins, lands, own_sems):
    me = _my_index()
    return pltpu.make_async_copy(ins[t].at[me] if scatter[t] else ins[t], lands[t].at[me], own_sems.at[t])


_HBM = pl.BlockSpec(memory_space=pltpu.HBM)
_SEM = pl.BlockSpec(memory_space=pltpu.SEMAPHORE)
_EFFECT = pltpu.SideEffectType.DATAFLOW_SIDE_EFFECTING


def exchange_start(arrays, scatter, *, name):
    nt = len(arrays)
    land_shapes = [a.shape if scatter[t] else (N_DEV,) + a.shape for t, a in enumerate(arrays)]

    def body(*refs):
        ins, lands = refs[:nt], refs[nt:2 * nt]
        send_sems, recv_sems, own_sems = refs[2 * nt:2 * nt + 3]
        token = refs[-1]
        for kk in range(1, N_DEV):
            for t in range(nt):
                _peer_copy(t, kk, scatter, ins, lands, send_sems, recv_sems).start()
        for t in range(nt):
            _own_copy(t, scatter, ins, lands, own_sems).start()
        token[...] = jnp.zeros_like(token)

    sems = pltpu.SemaphoreType.DMA((nt * N_DEV,))
    outs = pl.pallas_call(
        body, name=name,
        out_shape=(sems, sems, pltpu.SemaphoreType.DMA((nt,)), *[pltpu.HBM(a.shape, a.dtype) for a in arrays],
                   *[pltpu.HBM(shp, a.dtype) for shp, a in zip(land_shapes, arrays)],
                   jax.ShapeDtypeStruct((8, 128), F32)),
        in_specs=[_HBM] * (2 * nt),
        out_specs=(_SEM, _SEM, _SEM, *[_HBM] * (2 * nt), pl.BlockSpec(memory_space=pltpu.VMEM)),
        input_output_aliases={i: 3 + i for i in range(2 * nt)},
        compiler_params=pltpu.CompilerParams(has_side_effects=_EFFECT),
    )(*[pltpu.with_memory_space_constraint(a, pltpu.HBM) for a in arrays],
      *[pltpu.with_memory_space_constraint(lax.empty(shp, a.dtype), pltpu.HBM) for shp, a in zip(land_shapes, arrays)])
    return (outs[:3], outs[3:3 + nt], outs[3 + nt:3 + 2 * nt], scatter), outs[-1]


def exchange_wait(handle, after, *, name):
    sems, thru, lands, scatter = handle
    nt = len(thru)

    def body(*refs):
        ins, lnd = refs[:nt], refs[nt:2 * nt]
        s_sems, r_sems, o_sems = refs[2 * nt:2 * nt + 3]
        for kk in range(1, N_DEV):
            for t in range(nt):
                cp = _peer_copy(t, kk, scatter, ins, lnd, s_sems, r_sems)
                cp.wait_send()
                cp.wait_recv()
        for t in range(nt):
            _own_copy(t, scatter, ins, lnd, o_sems).wait()

    outs = pl.pallas_call(
        body, name=name,
        out_shape=(*[pltpu.HBM(a.shape, a.dtype) for a in thru], *[pltpu.HBM(a.shape, a.dtype) for a in lands]),
        in_specs=[_HBM] * (2 * nt) + [_SEM, _SEM, _SEM, pl.BlockSpec(memory_space=pl.ANY)],
        out_specs=tuple([_HBM] * (2 * nt)),
        input_output_aliases={i: i for i in range(2 * nt)},
        compiler_params=pltpu.CompilerParams(has_side_effects=_EFFECT),
    )(*thru, *lands, *sems, after)
    return outs[nt:]


def _cols_from_shards(g):
    g = jnp.moveaxis(g, 0, -2)
    return g.reshape(g.shape[:-2] + (g.shape[-2] * g.shape[-1],))


def _cols_to_shards(w, nshards=N_DEV):
    w = w.reshape(w.shape[:-1] + (nshards, w.shape[-1] // nshards))
    return jnp.moveaxis(w, -2, 0)


def _half_shards(halves):
    return jnp.concatenate([_cols_to_shards(h[None], N_DEV // 2) for h in halves], axis=0)


def _ffn_fwd(x, g, w_up, cw, cb, get_w_down, tag):
    h = rms_fwd(x, g, name=f"{tag}_norm")
    up_g = matmul(h, w_up[0], name=f"{tag}_up_g")
    up_v = matmul(h, w_up[1], name=f"{tag}_up_v")
    act, conv_g, conv_v = ffn_mid_fwd(up_g, up_v, cw, cb, name=f"{tag}_mid")
    w_down = get_w_down(act)
    out = matmul(act, w_down, res=x, name=f"{tag}_down")
    return out, (h, up_g, up_v, conv_g, conv_v, act), w_down


def _ffn_bwd(dx, dxb, x, saved, g, w_up, cw, w_down, tag):
    h, up_g, up_v, conv_g, conv_v, act = saved
    dact = matmul(dxb, w_down, tb=True, name=f"{tag}_ddown")
    d_w_down = matmul_ta(act, dxb, name=f"{tag}_gdown")
    dug, duv, dcw, dcb = ffn_mid_bwd(dact, up_g, up_v, conv_g, conv_v, cw, name=f"{tag}_dmid")
    d_w_up = (matmul_ta(h, dug, name=f"{tag}_gup_g"), matmul_ta(h, duv, name=f"{tag}_gup_v"))
    dh = matmul(dug, w_up[0], tb=True, name=f"{tag}_dup_g")
    dh = matmul(duv, w_up[1], tb=True, res=dh, name=f"{tag}_dup_v")
    dx2, dxb2, dg = rms_bwd(dh, x, g, dx, name=f"{tag}_dnorm")
    return dx2, dxb2, dict(norm_g=dg, w_up=d_w_up, conv_w=dcw, conv_b=dcb, w_down=d_w_down)


def local_step(x0, tgt, a, weights, grads_out):
    s, d = x0.shape
    aw = a['even_v_ln_g'].shape[-1]
    causal = jnp.tril(jnp.ones((CHUNK, CHUNK), dtype=bool))
    wm = jnp.where(causal, a['even_w_s'][0], 0.0).astype(BF16)
    wmt = jnp.swapaxes(wm, 1, 2)
    bm = jnp.repeat(a['even_b_s'][0].T, HEAD, axis=1)
    sel = (jnp.arange(aw)[:, None] // HEAD == jnp.arange(128)[None, :]).astype(BF16)
    cos, sin = rope_tables(s)
    ffn_g, ffn_cb = a['ffn_norm_g'], a['ffn_conv_b']

    w0 = weights(0, None)
    w_in, conv_w, odd_g, ffn_cw = w0['w_in'], w0['conv_w'], w0['odd_g'], w0['ffn_cw']
    h0 = rms_fwd(x0, w0['even_g'], name="even_norm")
    z = matmul(h0, w_in, bias=a['even_b_in'], name="even_in")
    ycat, hc = even_mid_fwd(z, a['even_v_ln_g'], a['even_v_ln_b'], wm, bm, conv_w, a['even_conv_b'],
                            a['even_conv_ln_g'], a['even_conv_ln_b'], name="even_mid")
    w1 = weights(1, ycat)
    x1 = matmul(ycat, w1['w_out'], res=x0, name="even_out")
    x2, ffn0, w_down0 = _ffn_fwd(x1, ffn_g[0:1], w1['w_up'], ffn_cw[0], ffn_cb[0:1],
                                 lambda act: weights(2, act)['w_down'], "ffn0")
    h2 = rms_fwd(x2, odd_g, name="odd_norm")
    w2 = weights(3, h2)
    qkv = matmul(h2, w2['w_qkv'], name="odd_qkv")
    q, k = rope_fwd(qkv, cos, sin, name="rope")
    o, lse = attn_fwd(q, k, qkv, name="attn_fwd")
    x3 = matmul(o, w2['w_o'], res=x2, name="odd_out")
    w3 = weights(4, x3)
    x4, ffn1, _ = _ffn_fwd(x3, ffn_g[1:2], w3['w_up'], ffn_cw[1], ffn_cb[1:2], lambda act: w3['w_down'], "ffn1")
    loss_t, dx, dxb, d_final_g = final_loss_bwd(x4, a['final_norm_g'].reshape(1, -1), tgt, name="final_loss")

    dx, dxb, g1 = _ffn_bwd(dx, dxb, x3, ffn1, ffn_g[1:2], w3['w_up'], ffn_cw[1], w3['w_down'], "ffn1")
    dep = grads_out(0, dict(w_up=g1['w_up'], w_down=g1['w_down']))
    do = matmul(dxb, w2['w_o'], tb=True, dep=dep, name="odd_dout")
    d_w_o = matmul_ta(o, dxb, name="odd_gout")
    delta = attn_delta(do, o, name="attn_delta")
    dq = attn_dq(q, k, qkv, do, lse, delta, name="attn_dq")
    dk, dv = attn_dkv(q, k, qkv, do, lse, delta, name="attn_dkv")
    dqkv = rope_bwd(dq, dk, dv, cos, sin, name="rope_bwd")
    d_w_qkv = matmul_ta(h2, dqkv, name="odd_gqkv")
    dep = grads_out(1, dict(w_qkv=d_w_qkv, w_o=d_w_o))
    dh2 = matmul(dqkv, w2['w_qkv'], tb=True, dep=dep, name="odd_dqkv")
    dx, dxb, d_odd_g = rms_bwd(dh2, x2, odd_g, dx, name="odd_dnorm")
    dx, dxb, g0 = _ffn_bwd(dx, dxb, x1, ffn0, ffn_g[0:1], w1['w_up'], ffn_cw[0], w_down0, "ffn0")
    dep = grads_out(2, dict(w_up=g0['w_up'], w_down=g0['w_down']))
    d_w_out = matmul_ta(ycat, dxb, dep=dep, name="even_gout")
    dep = grads_out(3, dict(w_out=d_w_out))
    dycat = matmul(dxb, w1['w_out'], tb=True, dep=dep, name="even_dout")
    (dza, dhc, dba, dvg, dvb, dwm, dbs, dcg, dcbeta, dcb) = even_mid_bwd_rows(
        dycat, z, hc, a['even_v_ln_g'], a['even_v_ln_b'], wm, wmt, bm, sel, a['even_conv_ln_g'],
        a['even_conv_ln_b'], name="even_dmid_rows")
    dzb, dbb, dcw = even_conv_bwd(dhc, z, conv_w, name="even_dmid_conv")
    d_w_in = (matmul_ta(h0, dza, name="even_gin_a"), matmul_ta(h0, dzb, name="even_gin_b"))
    dep = grads_out(4, dict(w_in=d_w_in))
    dh0 = matmul(dza, w_in, tb=True, bk=0, dep=dep, name="even_din_a")
    dh0 = matmul(dzb, w_in, tb=True, bk=1, res=dh0, name="even_din_b")
    grad_x, _, d_even_g = rms_bwd(dh0, x0, w0['even_g'], dx, name="even_dnorm")

    nh = a['even_w_s'].shape[1]
    small_grads = {
        'even_norm_g': d_even_g, 'even_b_in': jnp.concatenate([dba, dbb], axis=1), 'even_v_ln_g': dvg,
        'even_v_ln_b': dvb, 'even_w_s': jnp.where(causal, dwm, 0.0)[None], 'even_b_s': dbs[:, :nh].T[None],
        'even_conv_w': dcw[None], 'even_conv_b': dcb, 'even_conv_ln_g': dcg, 'even_conv_ln_b': dcbeta,
        'odd_norm_g': d_odd_g, 'ffn_norm_g': jnp.concatenate([g0['norm_g'], g1['norm_g']], axis=0),
        'ffn_conv_w': jnp.stack([g0['conv_w'], g1['conv_w']]),
        'ffn_conv_b': jnp.concatenate([g0['conv_b'], g1['conv_b']], axis=0),
        'final_norm_g': d_final_g.reshape(-1),
    }
    grads_out(5, dict(small=small_grads))
    return loss_t, grad_x, small_grads


BIG = ['even_w_in', 'even_w_out', 'odd_w_qkv', 'odd_w_o', 'ffn_w_up', 'ffn_w_down']


def _as_tiles(flat, dtype=F32):
    return jnp.pad(flat, (0, (-flat.size) % 2048)).reshape(-1, 128).astype(dtype)


def kernel(*args):
    a = dict(zip(NAMES + ['loss_target'] + ['m_' + n for n in WEIGHTS] + ['v_' + n for n in WEIGHTS], args))
    x0 = a['x'][0]
    tgt = a['loss_target'][0]
    s, d = x0.shape
    me = _my_index()
    bf = lambda t: t.astype(BF16)

    small_local = _as_tiles(jnp.concatenate([a['even_conv_w'].reshape(-1), a['odd_norm_g'].reshape(-1),
                                             a['ffn_conv_w'].reshape(-1)]))
    stage_arrays = [
        [bf(a['even_w_in']), small_local],
        [bf(a['even_w_out']), bf(a['ffn_w_up'][0:1])],
        [bf(a['ffn_w_down'][0:1])],
        [bf(a['odd_w_qkv']), bf(a['odd_w_o'])],
        [bf(a['ffn_w_up'][1:2]), bf(a['ffn_w_down'][1:2])],
    ]
    started = [exchange_start(arrs, [False] * len(arrs), name=f"gather{i}_start") for i, arrs in enumerate(stage_arrays)]
    order = sum(tok[0, 0] for _, tok in started)

    def weights(stage, after):
        handle, tok = started[stage]
        full = exchange_wait(handle, tok if after is None else after, name=f"gather{stage}_wait")
        rows = lambda g: jnp.moveaxis(g, 0, 1).reshape(-1, d)
        halves = lambda g: (_cols_from_shards(g[:N_DEV // 2])[0], _cols_from_shards(g[N_DEV // 2:])[0])
        if stage == 0:
            gs = full[1].reshape(N_DEV, -1)
            n_cw, n_og, n_fw = a['even_conv_w'].size, a['odd_norm_g'].size, a['ffn_conv_w'].size
            return dict(
                w_in=_cols_from_shards(full[0])[0], even_g=a['even_norm_g'] + order,
                conv_w=_cols_from_shards(gs[:, :n_cw].reshape((N_DEV,) + a['even_conv_w'].shape))[0],
                odd_g=gs[:, n_cw:n_cw + n_og].reshape(1, -1),
                ffn_cw=_cols_from_shards(gs[:, n_cw + n_og:n_cw + n_og + n_fw].reshape((N_DEV,) + a['ffn_conv_w'].shape)))
        if stage == 1:
            return dict(w_out=rows(full[0]), w_up=halves(full[1]))
        if stage == 2:
            return dict(w_down=rows(full[0]))
        if stage == 3:
            return dict(w_qkv=_cols_from_shards(full[0])[0], w_o=rows(full[1]))
        return dict(w_up=halves(full[0]), w_down=rows(full[1]))

    sent = {}

    def grads_out(stage, g):
        to_rows = lambda w: w.reshape(N_DEV, 1, -1, d)
        if stage in (0, 2):
            pieces, scatter = [_half_shards(g['w_up']), to_rows(g['w_down'])], [True, True]
        elif stage == 1:
            pieces, scatter = [_cols_to_shards(g['w_qkv'][None]), to_rows(g['w_o'])], [True, True]
        elif stage == 3:
            pieces, scatter = [to_rows(g['w_out'])], [True]
        elif stage == 4:
            pieces, scatter = [_half_shards(g['w_in'])], [True]
        else:
            small = jnp.concatenate([g['small'][n].reshape(-1) for n in g['small']])
            pieces, scatter = [_as_tiles(small, BF16)], [False]
        sent[stage], tok = exchange_start(pieces, scatter, name=f"grads{stage}_start")
        return tok

    loss_t, grad_x, small_grads = local_step(x0, tgt, a, weights, grads_out)
    loss = lax.psum(loss_t[0, 0], ("x", "y", "c"))
    received = {stage: exchange_wait(handle, grad_x, name=f"grads{stage}_wait") for stage, handle in sent.items()}

    results = {}
    big_parts = {'even_w_in': [received[4][0]], 'even_w_out': [received[3][0]], 'odd_w_qkv': [received[1][0]],
                 'odd_w_o': [received[1][1]], 'ffn_w_up': [received[2][0], received[0][0]],
                 'ffn_w_down': [received[2][1], received[0][1]]}
    for n in BIG:
        shp = a[n].shape
        flat = lambda t: t.reshape(-1, shp[-1])
        outs = adamw([p.reshape(N_DEV, -1, shp[-1]) for p in big_parts[n]], flat(a[n]), flat(a['m_' + n]),
                     flat(a['v_' + n]), name=f"adamw_{n}")
        results[n] = [t.reshape(shp) for t in outs]

    small_names = list(small_grads)
    n_small = sum(small_grads[n].size for n in small_names)
    rs = received[5][0].reshape(N_DEV, -1)[:, :n_small]
    parts, offs = [], 0
    for n in small_names:
        full = small_grads[n].shape
        piece = rs[:, offs:offs + small_grads[n].size].reshape((N_DEV,) + full)
        offs += small_grads[n].size
        shp = a[n].shape
        if shp != full:
            width = shp[-1]
            piece = lax.dynamic_slice_in_dim(piece, me * width, width, axis=piece.ndim - 1)
        parts.append(piece.reshape(N_DEV, -1))
    parts = jnp.concatenate(parts, axis=1)
    pad = (-parts.shape[1]) % 2048
    cat = lambda pre: _as_tiles(jnp.concatenate([a[pre + n].reshape(-1) for n in small_names]))
    outs = adamw([jnp.pad(parts, ((0, 0), (0, pad))).reshape(N_DEV, -1, 128)], cat(''), cat('m_'), cat('v_'),
                 name="adamw_small")
    offs = 0
    for n in small_names:
        size = a[n].size
        results[n] = [t.reshape(-1)[offs:offs + size].reshape(a[n].shape) for t in outs]
        offs += size

    out = [loss, grad_x[None]]
    for i in range(4):
        out += [results[n][i] for n in WEIGHTS]
    return tuple(out)
```

```python
import functools
import math

import jax
import jax.numpy as jnp
from jax import lax
from jax.experimental import pallas as pl
from jax.experimental.pallas import tpu as pltpu

F32 = jnp.float32
BF16 = jnp.bfloat16

N_DEV = 8
EPS = 1e-6
NEG = -1e30
HEAD = 64
CHUNK = 128
BLOCK = 128
CONV_K = 31
FFN_K = 3
DILATIONS = (1, 4, 16)
ROPE_THETA = 10000.0
LR, B1, B2, ADAM_EPS, WD, STEP = 0.001, 0.9, 0.999, 1e-08, 0.01, 10

VMEM_LIMIT = 56 * 1024 * 1024
VMEM_BUDGET = 32 * 1024 * 1024
ROWS = 512
HALO = 32
FHALO = 8

NAMES = ['x', 'even_norm_g', 'even_w_in', 'even_b_in', 'even_v_ln_g', 'even_v_ln_b', 'even_w_s', 'even_b_s',
         'even_conv_w', 'even_conv_b', 'even_conv_ln_g', 'even_conv_ln_b', 'even_w_out', 'odd_norm_g',
         'odd_w_qkv', 'odd_w_o', 'ffn_norm_g', 'ffn_w_up', 'ffn_conv_w', 'ffn_conv_b', 'ffn_w_down',
         'final_norm_g']
WEIGHTS = NAMES[1:]


def _params(sem=None):
    return pltpu.CompilerParams(dimension_semantics=sem, vmem_limit_bytes=VMEM_LIMIT)


def _sigmoid(x):
    return 1.0 / (1.0 + jnp.exp(-x))


def _gelu(x):
    c = math.sqrt(2.0 / math.pi)
    return 0.5 * x * (1.0 + jnp.tanh(c * (x + 0.044715 * x * x * x)))


def _gelu_grad(x):
    c = math.sqrt(2.0 / math.pi)
    t = jnp.tanh(c * (x + 0.044715 * x * x * x))
    return 0.5 * (1.0 + t) + 0.5 * x * (1.0 - t * t) * c * (1.0 + 3.0 * 0.044715 * x * x)


def _ln_stats(x):
    mu = jnp.mean(x, axis=-1, keepdims=True)
    xc = x - mu
    rstd = lax.rsqrt(jnp.mean(xc * xc, axis=-1, keepdims=True) + EPS)
    return xc * rstd, rstd


def _ln_bwd(dy, xhat, rstd, g):
    dxh = dy * g
    return rstd * (dxh - jnp.mean(dxh, axis=-1, keepdims=True) - xhat * jnp.mean(dxh * xhat, axis=-1, keepdims=True))


def _colsum(x):
    return jnp.sum(x, axis=0, keepdims=True)


def _split3(x):
    hi = x.astype(BF16)
    r = x - hi.astype(F32)
    mid = r.astype(BF16)
    lo = (r - mid.astype(F32)).astype(BF16)
    return hi, mid, lo


def _dot(a, b, dims):
    return lax.dot_general(a, b, (dims, ((), ())), preferred_element_type=F32)


NN = ((1,), (0,))
NT = ((1,), (1,))
TN = ((0,), (0,))


def _divisors(n, cands):
    return [c for c in cands if c <= n and n % c == 0]


def _pick_tiles(m, n, k, a_bytes, b_bytes, o_bytes, extra_bytes):
    best = None
    for tm in _divisors(m, (1024, 512, 256, 128)):
        for tn in _divisors(n, (1408, 1024, 768, 704, 512, 384, 256, 128)):
            if tn % 128:
                continue
            need = 2 * (tm * k * a_bytes + k * tn * b_bytes + tm * tn * (o_bytes + extra_bytes)) + tm * tn * 4
            if need <= VMEM_BUDGET and (best is None or tm * tn > best[0] * best[1]):
                best = (tm, tn)
    assert best is not None, (m, n, k)
    return best


def matmul(a, b, *, tb=False, bk=0, bias=None, res=None, norm_g=None, dep=None, out_dtype=F32, name):
    m, k = a.shape
    n = b.shape[0] if tb else b.shape[1]
    assert b.shape[1] % k == 0 if tb else (b.shape[0] == k and bk == 0)
    if norm_g is None:
        tm, tn = _pick_tiles(m, n, k, a.dtype.itemsize, b.dtype.itemsize, jnp.dtype(out_dtype).itemsize,
                             4 if res is not None else 0)
    else:
        tm, tn = _divisors(m, (512,))[0], n

    def body(*refs):
        a_ref, b_ref = refs[:2]
        o_ref = refs[-2] if norm_g is not None else refs[-1]
        acc = _dot(a_ref[...].astype(BF16), b_ref[...].astype(BF16), NT if tb else NN)
        pos = 2
        if bias is not None:
            acc = acc + refs[pos][...]
            pos += 1
        if res is not None:
            acc = acc + refs[pos][...]
            pos += 1
        o_ref[...] = acc.astype(out_dtype)
        if norm_g is not None:
            r = lax.rsqrt(jnp.mean(acc * acc, axis=-1, keepdims=True) + EPS)
            refs[-1][...] = (acc * r * refs[pos][...]).astype(BF16)

    in_specs = [pl.BlockSpec((tm, k), lambda i, j: (i, 0)),
                pl.BlockSpec((tn, k), lambda i, j: (j, bk)) if tb else pl.BlockSpec((k, tn), lambda i, j: (0, j))]
    args = [a, b]
    if bias is not None:
        in_specs.append(pl.BlockSpec((1, tn), lambda i, j: (0, j)))
        args.append(bias)
    if res is not None:
        in_specs.append(pl.BlockSpec((tm, tn), lambda i, j: (i, j)))
        args.append(res)
    if norm_g is not None:
        in_specs.append(pl.BlockSpec((1, tn), lambda i, j: (0, j)))
        args.append(norm_g)
    if dep is not None:
        in_specs.append(pl.BlockSpec(memory_space=pl.ANY))
        args.append(dep)
    blk = pl.BlockSpec((tm, tn), lambda i, j: (i, j))
    out_shape = jax.ShapeDtypeStruct((m, n), out_dtype)
    return pl.pallas_call(
        body, name=name, grid=(m // tm, n // tn), in_specs=in_specs,
        out_specs=blk if norm_g is None else [blk, blk],
        out_shape=out_shape if norm_g is None else [out_shape, jax.ShapeDtypeStruct((m, n), BF16)],
        compiler_params=_params(("parallel", "parallel")))(*args)


def matmul_ta(a, b, *, dep=None, out_dtype=BF16, name):
    s, m = a.shape
    n = b.shape[1]
    assert b.shape[0] == s
    best = None
    for tm in _divisors(m, (512, 256, 128)):
        for tn in _divisors(n, (1024, 512, 384, 256, 128)):
            need = 2 * (s * tm * a.dtype.itemsize + s * tn * b.dtype.itemsize + tm * tn * 2) + tm * tn * 4
            if need <= VMEM_BUDGET and (best is None or tm * tn > best[0] * best[1]):
                best = (tm, tn)
    tm, tn = best

    def body(*refs):
        a_ref, b_ref, o_ref = refs[0], refs[1], refs[-1]
        o_ref[...] = _dot(a_ref[...].astype(BF16), b_ref[...].astype(BF16), TN).astype(out_dtype)

    in_specs = [pl.BlockSpec((s, tm), lambda i, j: (0, i)), pl.BlockSpec((s, tn), lambda i, j: (0, j))]
    args = [a, b]
    if dep is not None:
        in_specs.append(pl.BlockSpec(memory_space=pl.ANY))
        args.append(dep)
    return pl.pallas_call(
        body, name=name, grid=(m // tm, n // tn), in_specs=in_specs,
        out_specs=pl.BlockSpec((tm, tn), lambda i, j: (i, j)),
        out_shape=jax.ShapeDtypeStruct((m, n), out_dtype),
        compiler_params=_params(("parallel", "parallel")))(*args)


def rms_fwd(x, g, *, name):
    s, d = x.shape

    def body(x_ref, g_ref, h_ref):
        xv = x_ref[...]
        r = lax.rsqrt(jnp.mean(xv * xv, axis=-1, keepdims=True) + EPS)
        h_ref[...] = (xv * r * g_ref[...]).astype(BF16)

    return pl.pallas_call(
        body, name=name, grid=(s // ROWS,),
        in_specs=[pl.BlockSpec((ROWS, d), lambda i: (i, 0)), pl.BlockSpec((1, d), lambda i: (0, 0))],
        out_specs=pl.BlockSpec((ROWS, d), lambda i: (i, 0)),
        out_shape=jax.ShapeDtypeStruct((s, d), BF16),
        compiler_params=_params(("parallel",)))(x, g)


def rms_bwd(dh, x, g, dres, *, name):
    s, d = x.shape

    def body(dh_ref, x_ref, g_ref, dres_ref, dx_ref, dxb_ref, dg_ref):
        xv = x_ref[...]
        r = lax.rsqrt(jnp.mean(xv * xv, axis=-1, keepdims=True) + EPS)
        xhat = xv * r
        dhv = dh_ref[...]
        dxh = dhv * g_ref[...]
        dx = dres_ref[...] + r * (dxh - xhat * jnp.mean(dxh * xhat, axis=-1, keepdims=True))
        dx_ref[...] = dx
        dxb_ref[...] = dx.astype(BF16)

        @pl.when(pl.program_id(0) == 0)
        def _():
            dg_ref[...] = jnp.zeros_like(dg_ref)
        dg_ref[...] += _colsum(dhv * xhat)

    row = pl.BlockSpec((ROWS, d), lambda i: (i, 0))
    vec = pl.BlockSpec((1, d), lambda i: (0, 0))
    return pl.pallas_call(
        body, name=name, grid=(s // ROWS,),
        in_specs=[row, row, vec, row], out_specs=[row, row, vec],
        out_shape=[jax.ShapeDtypeStruct((s, d), F32), jax.ShapeDtypeStruct((s, d), BF16),
                   jax.ShapeDtypeStruct((1, d), F32)],
        compiler_params=_params(("arbitrary",)))(dh, x, g, dres)


def final_loss_bwd(x, g, tgt, *, name):
    s, d = x.shape

    def body(x_ref, g_ref, t_ref, loss_ref, dx_ref, dxb_ref, dg_ref):
        xv = x_ref[...]
        gv = g_ref[...]
        r = lax.rsqrt(jnp.mean(xv * xv, axis=-1, keepdims=True) + EPS)
        xhat = xv * r
        e = xhat * gv - t_ref[...]
        dy = e * (1.0 / d)
        dxh = dy * gv
        dx = r * (dxh - xhat * jnp.mean(dxh * xhat, axis=-1, keepdims=True))
        dx_ref[...] = dx
        dxb_ref[...] = dx.astype(BF16)

        @pl.when(pl.program_id(0) == 0)
        def _():
            dg_ref[...] = jnp.zeros_like(dg_ref)
            loss_ref[...] = jnp.zeros_like(loss_ref)
        dg_ref[...] += _colsum(dy * xhat)
        loss_ref[...] += 0.5 * jnp.sum(jnp.mean(e * e, axis=-1, keepdims=True))

    row = pl.BlockSpec((ROWS, d), lambda i: (i, 0))
    vec = pl.BlockSpec((1, d), lambda i: (0, 0))
    one = pl.BlockSpec((8, 128), lambda i: (0, 0))
    return pl.pallas_call(
        body, name=name, grid=(s // ROWS,),
        in_specs=[row, vec, row], out_specs=[one, row, row, vec],
        out_shape=[jax.ShapeDtypeStruct((8, 128), F32), jax.ShapeDtypeStruct((s, d), F32),
                   jax.ShapeDtypeStruct((s, d), BF16), jax.ShapeDtypeStruct((1, d), F32)],
        compiler_params=_params(("arbitrary",)))(x, g, tgt)


def _pair_masks():
    lane = lax.broadcasted_iota(jnp.int32, (CHUNK, 128), 1)
    return lane < HEAD


def _head_keep(rows):
    lane = lax.broadcasted_iota(jnp.int32, (rows, 128), 1)
    first = jnp.where(lane < HEAD, 1.0, 0.0)
    return first.astype(BF16), (1.0 - first).astype(BF16)


def _gating_mixed(vn_b, wm_ref, lo):
    rows, aw = vn_b.shape
    out = []
    for c in range(rows // CHUNK):
        tiles = []
        for p in range(aw // 128):
            vp = vn_b[c * CHUNK:(c + 1) * CHUNK, p * 128:(p + 1) * 128]
            r0 = _dot(wm_ref[2 * p], vp, NN)
            r1 = _dot(wm_ref[2 * p + 1], vp, NN)
            tiles.append(jnp.where(lo, r0, r1))
        out.append(jnp.concatenate(tiles, axis=1))
    return jnp.concatenate(out, axis=0)


def even_mid_fwd(z, vg, vb, wm, bm, cw, cb, cg, cbeta, *, name):
    s, zw = z.shape
    aw = zw // 4
    nblk = s // ROWS

    def body(z_ref, zp_ref, vg_ref, vb_ref, wm_ref, bm_ref, cw_ref, cb_ref, cg_ref, cbeta_ref,
             y_ref, hc_ref, ext_ref):
        i = pl.program_id(0)
        lo = _pair_masks()
        u = _gelu(z_ref[:, 0:aw])
        v = _gelu(z_ref[:, aw:2 * aw])
        vhat, _ = _ln_stats(v)
        vn = (vhat * vg_ref[...] + vb_ref[...]).astype(BF16)
        mixed = _gating_mixed(vn, wm_ref, lo)
        bias = jnp.concatenate([bm_ref[...]] * (ROWS // CHUNK), axis=0)
        y_ref[:, 0:aw] = (u * (mixed + bias)).astype(BF16)

        hb = z_ref[:, 2 * aw:3 * aw] * _sigmoid(z_ref[:, 3 * aw:4 * aw])
        hbp = zp_ref[:, 0:aw] * _sigmoid(zp_ref[:, aw:2 * aw])
        ext_ref[0:HALO, :] = jnp.where(i > 0, hbp, 0.0)
        ext_ref[HALO:HALO + ROWS, :] = hb
        acc = jnp.zeros((ROWS, aw), F32) + cb_ref[...]
        for k in range(CONV_K):
            acc = acc + cw_ref[k:k + 1, :] * ext_ref[pl.ds(HALO - (CONV_K - 1) + k, ROWS), :]
        hc_ref[...] = acc
        hhat, _ = _ln_stats(acc)
        hn = hhat * cg_ref[...] + cbeta_ref[...]
        y_ref[:, aw:2 * aw] = (hn * _sigmoid(hn)).astype(BF16)

    hb_per = ROWS // HALO
    vec = pl.BlockSpec((1, aw), lambda i: (0, 0))
    return pl.pallas_call(
        body, name=name, grid=(nblk,),
        in_specs=[pl.BlockSpec((ROWS, zw), lambda i: (i, 0)),
                  pl.BlockSpec((HALO, 2 * aw), lambda i: (jnp.maximum(i * hb_per - 1, 0), 1)),
                  vec, vec,
                  pl.BlockSpec(wm.shape, lambda i: (0, 0, 0)),
                  pl.BlockSpec((CHUNK, aw), lambda i: (0, 0)),
                  pl.BlockSpec((CONV_K, aw), lambda i: (0, 0)), vec, vec, vec],
        out_specs=[pl.BlockSpec((ROWS, 2 * aw), lambda i: (i, 0)), pl.BlockSpec((ROWS, aw), lambda i: (i, 0))],
        out_shape=[jax.ShapeDtypeStruct((s, 2 * aw), BF16), jax.ShapeDtypeStruct((s, aw), F32)],
        scratch_shapes=[pltpu.VMEM((HALO + ROWS, aw), F32)],
        compiler_params=_params(("parallel",)))(z, z, vg, vb, wm, bm, cw, cb, cg, cbeta)


def even_mid_bwd_rows(dy, z, hc, vg, vb, wm, wmt, bm, sel, cg, cbeta, *, name):
    s, zw = z.shape
    aw = zw // 4
    nh = wm.shape[0]

    def body(dy_ref, z_ref, hc_ref, vg_ref, vb_ref, wm_ref, wmt_ref, bm_ref, sel_ref, cg_ref, cbeta_ref,
             dza_ref, dhc_ref, dba_ref, dvg_ref, dvb_ref, dwm_ref, dbs_ref, dcg_ref, dcbeta_ref, dcb_ref):
        @pl.when(pl.program_id(0) == 0)
        def _():
            for r in (dba_ref, dvg_ref, dvb_ref, dwm_ref, dbs_ref, dcg_ref, dcbeta_ref, dcb_ref):
                r[...] = jnp.zeros_like(r)

        lo = _pair_masks()
        keep = _head_keep(CHUNK)
        zu = z_ref[:, 0:aw]
        zv = z_ref[:, aw:2 * aw]
        u = _gelu(zu)
        v = _gelu(zv)
        vhat, vrstd = _ln_stats(v)
        vn = (vhat * vg_ref[...] + vb_ref[...]).astype(BF16)
        mixed = _gating_mixed(vn, wm_ref, lo)
        bias = jnp.concatenate([bm_ref[...]] * (ROWS // CHUNK), axis=0)
        dya = dy_ref[:, 0:aw]
        du = dya * (mixed + bias)
        dmix = dya * u
        dmix_b = dmix.astype(BF16)

        dvn_rows = []
        for c in range(ROWS // CHUNK):
            rs = slice(c * CHUNK, (c + 1) * CHUNK)
            tiles = []
            for p in range(aw // 128):
                cs = slice(p * 128, (p + 1) * 128)
                dm = dmix_b[rs, cs]
                dm0 = dm * keep[0]
                dm1 = dm * keep[1]
                vp = vn[rs, cs]
                tiles.append(_dot(wmt_ref[2 * p], dm0, NN) + _dot(wmt_ref[2 * p + 1], dm1, NN))
                dwm_ref[2 * p] += _dot(dm0, vp, NT)
                dwm_ref[2 * p + 1] += _dot(dm1, vp, NT)
            dvn_rows.append(jnp.concatenate(tiles, axis=1))
            acc = jnp.zeros((CHUNK, 128), F32)
            for part in _split3(dmix[rs, :]):
                acc = acc + _dot(part, sel_ref[...], NN)
            dbs_ref[...] += acc
        dvn = jnp.concatenate(dvn_rows, axis=0)
        dvg_ref[...] += _colsum(dvn * vhat)
        dvb_ref[...] += _colsum(dvn)
        dv = _ln_bwd(dvn, vhat, vrstd, vg_ref[...])
        dzu = du * _gelu_grad(zu)
        dzv = dv * _gelu_grad(zv)
        dza_ref[:, 0:aw] = dzu.astype(BF16)
        dza_ref[:, aw:2 * aw] = dzv.astype(BF16)
        dba_ref[:, 0:aw] += _colsum(dzu)
        dba_ref[:, aw:2 * aw] += _colsum(dzv)

        hcv = hc_ref[...]
        hhat, hrstd = _ln_stats(hcv)
        hn = hhat * cg_ref[...] + cbeta_ref[...]
        sg = _sigmoid(hn)
        dhn = dy_ref[:, aw:2 * aw] * (sg * (1.0 + hn * (1.0 - sg)))
        dcg_ref[...] += _colsum(dhn * hhat)
        dcbeta_ref[...] += _colsum(dhn)
        dhc = _ln_bwd(dhn, hhat, hrstd, cg_ref[...])
        dhc_ref[...] = dhc
        dcb_ref[...] += _colsum(dhc)

    vec = pl.BlockSpec((1, aw), lambda i: (0, 0))
    vec2 = pl.BlockSpec((1, 2 * aw), lambda i: (0, 0))
    w3 = pl.BlockSpec(wm.shape, lambda i: (0, 0, 0))
    sq = pl.BlockSpec((CHUNK, 128), lambda i: (0, 0))
    return pl.pallas_call(
        body, name=name, grid=(s // ROWS,),
        in_specs=[pl.BlockSpec((ROWS, 2 * aw), lambda i: (i, 0)), pl.BlockSpec((ROWS, 2 * aw), lambda i: (i, 0)),
                  pl.BlockSpec((ROWS, aw), lambda i: (i, 0)), vec, vec, w3, w3,
                  pl.BlockSpec((CHUNK, aw), lambda i: (0, 0)), pl.BlockSpec((aw, 128), lambda i: (0, 0)), vec, vec],
        out_specs=[pl.BlockSpec((ROWS, 2 * aw), lambda i: (i, 0)), pl.BlockSpec((ROWS, aw), lambda i: (i, 0)),
                   vec2, vec, vec, w3, sq, vec, vec, vec],
        out_shape=[jax.ShapeDtypeStruct((s, 2 * aw), BF16), jax.ShapeDtypeStruct((s, aw), F32),
                   jax.ShapeDtypeStruct((1, 2 * aw), F32), jax.ShapeDtypeStruct((1, aw), F32),
                   jax.ShapeDtypeStruct((1, aw), F32), jax.ShapeDtypeStruct(wm.shape, F32),
                   jax.ShapeDtypeStruct((CHUNK, 128), F32), jax.ShapeDtypeStruct((1, aw), F32),
                   jax.ShapeDtypeStruct((1, aw), F32), jax.ShapeDtypeStruct((1, aw), F32)],
        compiler_params=_params(("arbitrary",)))(dy, z, hc, vg, vb, wm, wmt, bm, sel, cg, cbeta)


def even_conv_bwd(dhc, z, cw, *, name):
    s, zw = z.shape
    aw = zw // 4
    nblk = s // ROWS
    hb_per = ROWS // HALO

    def body(dc_ref, dn_ref, z_ref, zp_ref, cw_ref, dzb_ref, dbb_ref, dcw_ref, exth_ref, extd_ref):
        i = pl.program_id(0)

        @pl.when(i == 0)
        def _():
            dbb_ref[...] = jnp.zeros_like(dbb_ref)
            dcw_ref[...] = jnp.zeros_like(dcw_ref)

        a = z_ref[:, 0:aw]
        sg = _sigmoid(z_ref[:, aw:2 * aw])
        exth_ref[0:HALO, :] = jnp.where(i > 0, zp_ref[:, 0:aw] * _sigmoid(zp_ref[:, aw:2 * aw]), 0.0)
        exth_ref[HALO:HALO + ROWS, :] = a * sg
        dcur = dc_ref[...]
        extd_ref[0:ROWS, :] = dcur
        extd_ref[ROWS:ROWS + HALO, :] = jnp.where(i < nblk - 1, dn_ref[...], 0.0)
        dhb = jnp.zeros((ROWS, aw), F32)
        for k in range(CONV_K):
            wk = cw_ref[k:k + 1, :]
            dhb = dhb + wk * extd_ref[pl.ds(CONV_K - 1 - k, ROWS), :]
            dcw_ref[k:k + 1, :] += _colsum(dcur * exth_ref[pl.ds(HALO - (CONV_K - 1) + k, ROWS), :])
        da = dhb * sg
        dg = dhb * a * sg * (1.0 - sg)
        dzb_ref[:, 0:aw] = da.astype(BF16)
        dzb_ref[:, aw:2 * aw] = dg.astype(BF16)
        dbb_ref[:, 0:aw] += _colsum(da)
        dbb_ref[:, aw:2 * aw] += _colsum(dg)

    return pl.pallas_call(
        body, name=name, grid=(nblk,),
        in_specs=[pl.BlockSpec((ROWS, aw), lambda i: (i, 0)),
                  pl.BlockSpec((HALO, aw), lambda i: (jnp.minimum((i + 1) * hb_per, nblk * hb_per - 1), 0)),
                  pl.BlockSpec((ROWS, 2 * aw), lambda i: (i, 1)),
                  pl.BlockSpec((HALO, 2 * aw), lambda i: (jnp.maximum(i * hb_per - 1, 0), 1)),
                  pl.BlockSpec((CONV_K, aw), lambda i: (0, 0))],
        out_specs=[pl.BlockSpec((ROWS, 2 * aw), lambda i: (i, 0)), pl.BlockSpec((1, 2 * aw), lambda i: (0, 0)),
                   pl.BlockSpec((CONV_K, aw), lambda i: (0, 0))],
        out_shape=[jax.ShapeDtypeStruct((s, 2 * aw), BF16), jax.ShapeDtypeStruct((1, 2 * aw), F32),
                   jax.ShapeDtypeStruct((CONV_K, aw), F32)],
        scratch_shapes=[pltpu.VMEM((HALO + ROWS, aw), F32), pltpu.VMEM((ROWS + HALO, aw), F32)],
        compiler_params=_params(("arbitrary",)))(dhc, dhc, z, z, cw)


FFN_ROWS = 1024
FFN_CHUNK = 16


def _ffn_tile(f):
    for t in (256, 128):
        if f % t == 0:
            return t
    raise ValueError(f)


def _taps(ext_ref, w, b, r0, rows):
    acc = b
    for k in range(FFN_K):
        acc = acc + w[k] * ext_ref[pl.ds(FHALO - (FFN_K - 1) + k + r0, rows), :]
    return acc


def ffn_mid_fwd(up_g, up_v, cw, cb, *, name):
    s, f = up_g.shape
    tn = _ffn_tile(f)
    nj = f // tn
    ROWS = FFN_ROWS
    per = ROWS // FHALO

    def body(ug_ref, uv_ref, pg_ref, pv_ref, wg_ref, wv_ref, bg_ref, bv_ref, act_ref, cg_ref, cv_ref, eg_ref, ev_ref):
        i = pl.program_id(0)
        for cur_ref, prev_ref, ext_ref in ((ug_ref, pg_ref, eg_ref), (uv_ref, pv_ref, ev_ref)):
            ext_ref[0:FHALO, :] = jnp.where(i > 0, prev_ref[...], 0.0)
            ext_ref[FHALO:FHALO + ROWS, :] = cur_ref[...]
        wg = [wg_ref[k:k + 1, :] for k in range(FFN_K)]
        wv = [wv_ref[k:k + 1, :] for k in range(FFN_K)]
        bg, bv = bg_ref[...], bv_ref[...]
        for r0 in range(0, ROWS, FFN_CHUNK):
            rows = pl.ds(r0, FFN_CHUNK)
            gate = _taps(eg_ref, wg, bg, r0, FFN_CHUNK)
            val = _taps(ev_ref, wv, bv, r0, FFN_CHUNK)
            cg_ref[rows, :] = gate
            cv_ref[rows, :] = val
            act_ref[rows, :] = (gate * _sigmoid(gate) * val).astype(BF16)

    cur = lambda off: pl.BlockSpec((ROWS, tn), lambda i, j: (i, j + off))
    prev = lambda off: pl.BlockSpec((FHALO, tn), lambda i, j: (jnp.maximum(i * per - 1, 0), j + off))
    wsp = lambda off: pl.BlockSpec((FFN_K, tn), lambda i, j: (0, j + off))
    bsp = lambda off: pl.BlockSpec((1, tn), lambda i, j: (0, j + off))
    return pl.pallas_call(
        body, name=name, grid=(s // ROWS, nj),
        in_specs=[cur(0), cur(0), prev(0), prev(0), wsp(0), wsp(nj), bsp(0), bsp(nj)],
        out_specs=[cur(0), cur(0), cur(0)],
        out_shape=[jax.ShapeDtypeStruct((s, f), BF16), jax.ShapeDtypeStruct((s, f), F32),
                   jax.ShapeDtypeStruct((s, f), F32)],
        scratch_shapes=[pltpu.VMEM((FHALO + ROWS, tn), F32), pltpu.VMEM((FHALO + ROWS, tn), F32)],
        compiler_params=_params(("parallel", "parallel")))(up_g, up_v, up_g, up_v, cw, cw, cb, cb)


def ffn_mid_bwd(dact, up_g, up_v, conv_g, conv_v, cw, *, name):
    s, f = up_g.shape
    tn = _ffn_tile(f)
    nj = f // tn
    ROWS = FFN_ROWS
    nblk = s // ROWS
    per = ROWS // FHALO
    ext = ROWS + FHALO

    def body(da_ref, dan_ref, ug_ref, uv_ref, cg_ref, cv_ref, cgn_ref, cvn_ref, wg_ref, wv_ref,
             dug_ref, duv_ref, dwg_ref, dwv_ref, dbg_ref, dbv_ref, dg_ref, dv_ref):
        i = pl.program_id(1)

        @pl.when(i == 0)
        def _():
            for r in (dwg_ref, dwv_ref, dbg_ref, dbv_ref):
                r[...] = jnp.zeros_like(r)

        wg = [wg_ref[k:k + 1, :] for k in range(FFN_K)]
        wv = [wv_ref[k:k + 1, :] for k in range(FFN_K)]

        for r0, rows in [(r, FFN_CHUNK) for r in range(0, ROWS, FFN_CHUNK)] + [(ROWS, FHALO)]:
            if r0 < ROWS:
                gate, val, da = cg_ref[pl.ds(r0, rows), :], cv_ref[pl.ds(r0, rows), :], da_ref[pl.ds(r0, rows), :]
            else:
                gate, val, da = cgn_ref[...], cvn_ref[...], jnp.where(i < nblk - 1, dan_ref[...], 0.0)
            sg = _sigmoid(gate)
            dg_ref[pl.ds(r0, rows), :] = da * val * (sg * (1.0 + gate * (1.0 - sg)))
            dv_ref[pl.ds(r0, rows), :] = da * (gate * sg)

        def back(d_ref, w, u_ref, du_ref, dw_ref, db_ref):
            zero = jnp.zeros((FFN_CHUNK, tn), F32)
            acc = [zero] * FFN_K
            accb = zero
            for r0 in range(0, ROWS, FFN_CHUNK):
                d = [d_ref[pl.ds(r0 + FFN_K - 1 - k, FFN_CHUNK), :] for k in range(FFN_K)]
                u = u_ref[pl.ds(r0, FFN_CHUNK), :]
                du = w[0] * d[0]
                for k in range(1, FFN_K):
                    du = du + w[k] * d[k]
                du_ref[pl.ds(r0, FFN_CHUNK), :] = du.astype(BF16)
                acc = [acc[k] + u * d[k] for k in range(FFN_K)]
                accb = accb + d[FFN_K - 1]
            for k in range(FFN_K):
                dw_ref[k:k + 1, :] += _colsum(acc[k])
            db_ref[...] += _colsum(accb)

        back(dg_ref, wg, ug_ref, dug_ref, dwg_ref, dbg_ref)
        back(dv_ref, wv, uv_ref, duv_ref, dwv_ref, dbv_ref)

    cur = pl.BlockSpec((ROWS, tn), lambda j, i: (i, j))
    nxt = pl.BlockSpec((FHALO, tn), lambda j, i: (jnp.minimum((i + 1) * per, nblk * per - 1), j))
    wsp = lambda off: pl.BlockSpec((FFN_K, tn), lambda j, i: (0, j + off))
    bsp = pl.BlockSpec((1, tn), lambda j, i: (0, j))
    outs = pl.pallas_call(
        body, name=name, grid=(nj, nblk),
        in_specs=[cur, nxt, cur, cur, cur, cur, nxt, nxt, wsp(0), wsp(nj)],
        out_specs=[cur, cur, wsp(0), wsp(0), bsp, bsp],
        out_shape=[jax.ShapeDtypeStruct((s, f), BF16), jax.ShapeDtypeStruct((s, f), BF16),
                   jax.ShapeDtypeStruct((FFN_K, f), F32), jax.ShapeDtypeStruct((FFN_K, f), F32),
                   jax.ShapeDtypeStruct((1, f), F32), jax.ShapeDtypeStruct((1, f), F32)],
        scratch_shapes=[pltpu.VMEM((ext, tn), F32), pltpu.VMEM((ext, tn), F32)],
        compiler_params=_params(("parallel", "arbitrary")))(dact, dact, up_g, up_v, conv_g, conv_v, conv_g, conv_v,
                                                            cw, cw)
    dug, duv, dwg, dwv, dbg, dbv = outs
    return dug, duv, jnp.concatenate([dwg, dwv], axis=1), jnp.concatenate([dbg, dbv], axis=1)


def rope_tables(s):
    half = HEAD // 2
    lane = jnp.arange(128)
    j = lane % HEAD
    inv = ROPE_THETA ** (-(j % half).astype(F32) / half)
    ang = jnp.arange(s, dtype=F32)[:, None] * inv[None, :]
    sign = jnp.where(j < half, -1.0, 1.0).astype(F32)
    return jnp.cos(ang), jnp.sin(ang) * sign[None, :]


def _swap_halves(x):
    lane = lax.broadcasted_iota(jnp.int32, x.shape, 1)
    return jnp.where((lane % HEAD) < HEAD // 2, pltpu.roll(x, 128 - HEAD // 2, 1), pltpu.roll(x, HEAD // 2, 1))


def rope_fwd(qkv, cos, sin, *, name):
    s, d3 = qkv.shape
    d = d3 // 3
    scale = HEAD ** -0.5

    def body(xq_ref, xk_ref, c_ref, s_ref, q_ref, k_ref):
        c = c_ref[...]
        sn = s_ref[...]
        for t in range(d // 128):
            cs = slice(t * 128, (t + 1) * 128)
            xq = xq_ref[:, cs]
            xk = xk_ref[:, cs]
            q_ref[:, cs] = (xq * c + _swap_halves(xq) * sn) * scale
            k_ref[:, cs] = xk * c + _swap_halves(xk) * sn

    row = pl.BlockSpec((ROWS, d), lambda i: (i, 0))
    tab = pl.BlockSpec((ROWS, 128), lambda i: (i, 0))
    return pl.pallas_call(
        body, name=name, grid=(s // ROWS,),
        in_specs=[row, pl.BlockSpec((ROWS, d), lambda i: (i, 1)), tab, tab],
        out_specs=[row, row],
        out_shape=[jax.ShapeDtypeStruct((s, d), F32)] * 2,
        compiler_params=_params(("parallel",)))(qkv, qkv, cos, sin)


def rope_bwd(dq, dk, dv, cos, sin, *, name):
    s, d = dq.shape
    scale = HEAD ** -0.5

    def body(dq_ref, dk_ref, dv_ref, c_ref, s_ref, o_ref):
        c = c_ref[...]
        sn = s_ref[...]
        for t in range(d // 128):
            cs = slice(t * 128, (t + 1) * 128)
            gq = dq_ref[:, cs] * scale
            gk = dk_ref[:, cs]
            o_ref[:, t * 128:(t + 1) * 128] = (gq * c + _swap_halves(gq * sn)).astype(BF16)
            o_ref[:, d + t * 128:d + (t + 1) * 128] = (gk * c + _swap_halves(gk * sn)).astype(BF16)
        o_ref[:, 2 * d:3 * d] = dv_ref[...].astype(BF16)

    row = pl.BlockSpec((ROWS, d), lambda i: (i, 0))
    tab = pl.BlockSpec((ROWS, 128), lambda i: (i, 0))
    return pl.pallas_call(
        body, name=name, grid=(s // ROWS,),
        in_specs=[row, row, row, tab, tab],
        out_specs=pl.BlockSpec((ROWS, 3 * d), lambda i: (i, 0)),
        out_shape=jax.ShapeDtypeStruct((s, 3 * d), BF16),
        compiler_params=_params(("parallel",)))(dq, dk, dv, cos, sin)


ATT_T = BLOCK * max(DILATIONS)


FWD_GROUP = 2
ATT_GROUP = 4
FWD_QROWS = 128
BWD_QROWS = 64


def _unit_rows(r, j, dil):
    start = r + dil * BLOCK * j
    return pl.ds(start, BLOCK) if dil == 1 else pl.ds(start, BLOCK, stride=dil)


def _units():
    for bi, dil in enumerate(DILATIONS):
        nsub = ATT_T // (BLOCK * dil)
        for r in range(dil):
            for j in range(nsub):
                yield bi, dil, nsub, r, j


def _band(first_block, part, qrows):
    qi = lax.broadcasted_iota(jnp.int32, (qrows, 2 * BLOCK), 0) + part * qrows
    kj = lax.broadcasted_iota(jnp.int32, (qrows, 2 * BLOCK), 1)
    dist = BLOCK + qi - kj
    band = (dist >= 0) & (dist <= BLOCK)
    return band, band & (jnp.logical_not(first_block) | (kj >= BLOCK))


def _col(tile, h):
    return tile[:, h * HEAD:h * HEAD + 1]


def _keys(cur_ref, prev_ref, r, j, dil, nsub):
    cur = cur_ref[_unit_rows(r, j, dil), :]
    prev = cur_ref[_unit_rows(r, j - 1, dil), :] if j > 0 else prev_ref[_unit_rows(r, nsub - 1, dil), :]
    return jnp.concatenate([prev, cur], axis=0).astype(BF16)


def _att_specs(d, col_off=0):
    nt_cols = d // 128
    cur = pl.BlockSpec((ATT_T, 128), lambda n, p: (n, p + col_off))
    prv = pl.BlockSpec((ATT_T, 128), lambda n, p: (jnp.maximum(n - 1, 0), p + col_off))
    return cur, prv


def attn_fwd(q, k, qkv, *, name):
    s, d = q.shape
    nt = s // ATT_T

    def body(q_ref, kc_ref, kp_ref, vc_ref, vp_ref, o_ref, lse_ref, acc_ref, m_ref, l_ref):
        n = pl.program_id(0)
        QROWS = FWD_QROWS
        nparts = BLOCK // QROWS
        bands = [_band(n == 0, part, QROWS) for part in range(nparts)]
        lo = _pair_masks()
        keep = _head_keep(BLOCK)
        nb = len(DILATIONS)
        tile = lambda cols: jnp.where(lo, jnp.concatenate(cols[:nparts], axis=0),
                                      jnp.concatenate(cols[nparts:], axis=0))

        def scores(unit):
            bi, dil, nsub, r, j = unit
            rows = _unit_rows(r, j, dil)
            kw = _keys(kc_ref, kp_ref, r, j, dil, nsub)
            vw = _keys(vc_ref, vp_ref, r, j, dil, nsub)
            qp = q_ref[rows, :].astype(BF16)
            sc2 = _dot(jnp.concatenate([qp * keep[0], qp * keep[1]], axis=0), kw, NT)
            return dict(bi=bi, j=j, rows=rows, vw=vw, sc2=sc2, m_old=m_ref[rows, :] if bi > 0 else None)

        def softmax(u):
            prs, new_m, new_l, alpha = [], [], [], []
            for c in range(2 * nparts):
                h, part = divmod(c, nparts)
                valid = bands[part][0 if u['j'] > 0 else 1]
                sc = jnp.where(valid, u['sc2'][c * QROWS:(c + 1) * QROWS], NEG)
                mx = jnp.max(sc, axis=-1, keepdims=True)
                if u['bi'] == 0:
                    m_new = mx
                else:
                    m_old = _col(u['m_old'][part * QROWS:(part + 1) * QROWS], h)
                    m_new = jnp.maximum(m_old, mx)
                    alpha.append(jnp.exp(m_old - m_new))
                pr = jnp.exp(sc - m_new)
                new_m.append(m_new)
                new_l.append(jnp.sum(pr, axis=-1, keepdims=True))
                prs.append(pr.astype(BF16))
            u.update(pr2=jnp.concatenate(prs, axis=0), new_m=new_m, new_l=new_l, alpha=alpha)

        def combine(u):
            rows, bi = u['rows'], u['bi']
            pv2 = _dot(u['pr2'], u['vw'], NN)
            m_t = tile(u['new_m'])
            l_t = tile(u['new_l'])
            acc_t = jnp.where(lo, pv2[:BLOCK], pv2[BLOCK:])
            if bi > 0:
                a_t = tile(u['alpha'])
                l_t = a_t * l_ref[rows, :] + l_t
                acc_t = a_t * acc_ref[rows, :] + acc_t
            if bi == nb - 1:
                o_ref[rows, :] = acc_t / l_t
                lse_ref[rows, :] = m_t + jnp.log(l_t)
            else:
                acc_ref[rows, :] = acc_t
                m_ref[rows, :] = m_t
                l_ref[rows, :] = l_t

        units = list(_units())
        for first in range(0, len(units), FWD_GROUP):
            pair = [scores(u) for u in units[first:first + FWD_GROUP]]
            for u in pair:
                softmax(u)
            for u in pair:
                combine(u)

    cur, prv = _att_specs(d)
    vcur, vprv = _att_specs(d, 2 * (d // 128))
    return pl.pallas_call(
        body, name=name, grid=(nt, d // 128), in_specs=[cur, cur, prv, vcur, vprv], out_specs=[cur, cur],
        out_shape=[jax.ShapeDtypeStruct((s, d), F32)] * 2,
        scratch_shapes=[pltpu.VMEM((ATT_T, 128), F32)] * 3,
        compiler_params=_params(("parallel", "parallel")))(q, k, k, qkv, qkv)


def attn_delta(do, o, *, name):
    s, d = do.shape

    def body(do_ref, o_ref, dl_ref):
        lane = lax.broadcasted_iota(jnp.int32, (ROWS, 128), 1)
        lo = lane < HEAD
        for p in range(d // 128):
            cs = slice(p * 128, (p + 1) * 128)
            pr = do_ref[:, cs] * o_ref[:, cs]
            s0 = jnp.sum(jnp.where(lo, pr, 0.0), axis=-1, keepdims=True)
            s1 = jnp.sum(jnp.where(lo, 0.0, pr), axis=-1, keepdims=True)
            dl_ref[:, cs] = jnp.where(lo, s0, s1)

    row = pl.BlockSpec((ROWS, d), lambda i: (i, 0))
    return pl.pallas_call(
        body, name=name, grid=(s // ROWS,), in_specs=[row, row], out_specs=row,
        out_shape=jax.ShapeDtypeStruct((s, d), F32),
        compiler_params=_params(("parallel",)))(do, o)


def attn_dq(q, k, qkv, do, lse, delta, *, name):
    s, d = q.shape
    nt = s // ATT_T

    def body(q_ref, kc_ref, kp_ref, vc_ref, vp_ref, do_ref, l_ref, dl_ref, dq_ref):
        n = pl.program_id(0)
        QROWS = BWD_QROWS
        nparts = BLOCK // QROWS
        bands = [_band(n == 0, part, QROWS) for part in range(nparts)]
        lo = _pair_masks()
        keep = _head_keep(BLOCK)
        def scores(unit):
            bi, dil, nsub, r, j = unit
            rows = _unit_rows(r, j, dil)
            kw = _keys(kc_ref, kp_ref, r, j, dil, nsub)
            vw = _keys(vc_ref, vp_ref, r, j, dil, nsub)
            qp = q_ref[rows, :].astype(BF16)
            dop = do_ref[rows, :].astype(BF16)
            sc2 = _dot(jnp.concatenate([qp * keep[0], qp * keep[1]], axis=0), kw, NT)
            dp2 = _dot(jnp.concatenate([dop * keep[0], dop * keep[1]], axis=0), vw, NT)
            return dict(bi=bi, j=j, rows=rows, kw=kw, sc2=sc2, dp2=dp2, lt=l_ref[rows, :], dt=dl_ref[rows, :])

        def softmax_bwd(u):
            dss = []
            for c in range(2 * nparts):
                h, part = divmod(c, nparts)
                valid = bands[part][0 if u['j'] > 0 else 1]
                cr = slice(c * QROWS, (c + 1) * QROWS)
                pr_rows = slice(part * QROWS, (part + 1) * QROWS)
                pr = jnp.where(valid, jnp.exp(u['sc2'][cr] - _col(u['lt'][pr_rows], h)), 0.0)
                dss.append((pr * (u['dp2'][cr] - _col(u['dt'][pr_rows], h))).astype(BF16))
            u['ds2'] = jnp.concatenate(dss, axis=0)

        def combine(u):
            rows = u['rows']
            dq2 = _dot(u['ds2'], u['kw'], NN)
            dq_t = jnp.where(lo, dq2[:BLOCK], dq2[BLOCK:])
            if u['bi'] > 0:
                dq_t = dq_t + dq_ref[rows, :]
            dq_ref[rows, :] = dq_t

        units = list(_units())
        for first in range(0, len(units), ATT_GROUP):
            pair = [scores(u) for u in units[first:first + ATT_GROUP]]
            for u in pair:
                softmax_bwd(u)
            for u in pair:
                combine(u)

    cur, prv = _att_specs(d)
    vcur, vprv = _att_specs(d, 2 * (d // 128))
    return pl.pallas_call(
        body, name=name, grid=(nt, d // 128), in_specs=[cur, cur, prv, vcur, vprv, cur, cur, cur], out_specs=cur,
        out_shape=jax.ShapeDtypeStruct((s, d), F32),
        compiler_params=_params(("parallel", "parallel")))(q, k, k, qkv, qkv, do, lse, delta)


def attn_dkv(q, k, qkv, do, lse, delta, *, name):
    s, d = q.shape
    nt = s // ATT_T

    def body(k_ref, v_ref, qc_ref, qn_ref, doc_ref, don_ref, lc_ref, ln_ref, dc_ref, dn_ref, dk_ref, dv_ref):
        n = pl.program_id(0)
        qi = lax.broadcasted_iota(jnp.int32, (BLOCK, BLOCK), 0)
        kj = lax.broadcasted_iota(jnp.int32, (BLOCK, BLOCK), 1)
        own = kj <= qi
        nxt = kj >= qi
        nxt_edge = nxt & (n < nt - 1)
        keep = _head_keep(BLOCK)
        lo = _pair_masks()
        def scores(unit):
            bi, dil, nsub, r, j = unit
            rows = _unit_rows(r, j, dil)
            inner = j + 1 < nsub
            nrows = _unit_rows(r, j + 1, dil) if inner else _unit_rows(r, 0, dil)
            kp = k_ref[rows, :].astype(BF16)
            vp = v_ref[rows, :].astype(BF16)
            far = not inner
            take = lambda c_ref, n_ref, nx: ((n_ref if far else c_ref)[nrows, :] if nx else c_ref[rows, :])
            qs = [take(qc_ref, qn_ref, nx).astype(BF16) for nx in (False, True)]
            dos = [take(doc_ref, don_ref, nx).astype(BF16) for nx in (False, True)]
            lts = [take(lc_ref, ln_ref, nx) for nx in (False, True)]
            dts = [take(dc_ref, dn_ref, nx) for nx in (False, True)]
            q4 = jnp.concatenate([qs[nx] * keep[h] for h in range(2) for nx in range(2)], axis=0)
            do4 = jnp.concatenate([dos[nx] * keep[h] for h in range(2) for nx in range(2)], axis=0)
            return dict(bi=bi, rows=rows, q4=q4, do4=do4, s4=_dot(q4, kp, NT), dp4=_dot(do4, vp, NT), lts=lts,
                        dts=dts, valids=(own, nxt if inner else nxt_edge))

        def softmax_bwd(u):
            prs, dss = [], []
            for c in range(4):
                h, nx = divmod(c, 2)
                cr = slice(c * BLOCK, (c + 1) * BLOCK)
                pr = jnp.where(u['valids'][nx], jnp.exp(u['s4'][cr] - _col(u['lts'][nx], h)), 0.0)
                prs.append(pr.astype(BF16))
                dss.append((pr * (u['dp4'][cr] - _col(u['dts'][nx], h))).astype(BF16))
            u.update(pr4=jnp.concatenate(prs, axis=0), ds4=jnp.concatenate(dss, axis=0))

        def combine(u):
            rows = u['rows']
            dv_t = _dot(u['pr4'], u['do4'], TN)
            dk_t = _dot(u['ds4'], u['q4'], TN)
            if u['bi'] > 0:
                dk_t = dk_t + dk_ref[rows, :]
                dv_t = dv_t + dv_ref[rows, :]
            dk_ref[rows, :] = dk_t
            dv_ref[rows, :] = dv_t

        units = list(_units())
        for first in range(0, len(units), ATT_GROUP):
            pair = [scores(u) for u in units[first:first + ATT_GROUP]]
            for u in pair:
                softmax_bwd(u)
            for u in pair:
                combine(u)

    cur = pl.BlockSpec((ATT_T, 128), lambda n, p: (n, p))
    nxt_spec = pl.BlockSpec((ATT_T, 128), lambda n, p: (jnp.minimum(n + 1, nt - 1), p))
    vcur = pl.BlockSpec((ATT_T, 128), lambda n, p: (n, p + 2 * (d // 128)))
    return pl.pallas_call(
        body, name=name, grid=(nt, d // 128),
        in_specs=[cur, vcur, cur, nxt_spec, cur, nxt_spec, cur, nxt_spec, cur, nxt_spec], out_specs=[cur, cur],
        out_shape=[jax.ShapeDtypeStruct((s, d), F32)] * 2,
        compiler_params=_params(("parallel", "parallel")))(k, qkv, q, q, do, do, lse, lse, delta, delta)


def adamw(parts_list, w, m, v, *, name):
    nk = len(parts_list)
    npart, rk, c = parts_list[0].shape
    r = rk * nk
    assert w.shape == (r, c)
    tr = next(t for t in range(rk, 0, -8) if rk % t == 0 and (t * c * 4 <= 1024 * 1024 or t == 8))
    nbk = rk // tr

    def body(*refs):
        p_refs = refs[:nk]
        w_ref, m_ref, v_ref, g_ref, d_ref, nm_ref, nv_ref = refs[nk:]
        i = pl.program_id(0)
        g = None
        for kk, p_ref in enumerate(p_refs):
            gk = p_ref[0].astype(F32)
            for j in range(1, npart):
                gk = gk + p_ref[j].astype(F32)
            g = gk if g is None else jnp.where(i >= kk * nbk, gk, g)
        m2 = B1 * m_ref[...] + (1.0 - B1) * g
        v2 = B2 * v_ref[...] + (1.0 - B2) * (g * g)
        m_hat = m2 / (1.0 - B1 ** STEP)
        v_hat = v2 / (1.0 - B2 ** STEP)
        g_ref[...] = g
        d_ref[...] = -LR * (m_hat / (jnp.sqrt(v_hat) + ADAM_EPS) + WD * w_ref[...])
        nm_ref[...] = m2
        nv_ref[...] = v2

    blk = pl.BlockSpec((tr, c), lambda i: (i, 0))
    pspec = lambda kk: pl.BlockSpec((npart, tr, c), lambda i: (0, jnp.clip(i - kk * nbk, 0, nbk - 1), 0))
    return pl.pallas_call(
        body, name=name, grid=(r // tr,),
        in_specs=[pspec(kk) for kk in range(nk)] + [blk, blk, blk],
        out_specs=[blk] * 4, out_shape=[jax.ShapeDtypeStruct((r, c), F32)] * 4,
        compiler_params=_params(("parallel",)))(*parts_list, w, m, v)


def _my_index():
    return 4 * lax.axis_index("x") + 2 * lax.axis_index("y") + lax.axis_index("c")


def exchange(arrays, scatter, *, name):
    nt = len(arrays)

    def body(*refs):
        ins = refs[:nt]
        outs = refs[nt:2 * nt]
        send_sems, recv_sems, local_sems = refs[2 * nt:]
        x, y, c = lax.axis_index("x"), lax.axis_index("y"), lax.axis_index("c")
        me = 4 * x + 2 * y + c
        copies = []
        for t in range(nt):
            src = ins[t].at[me] if scatter[t] else ins[t]
            cp = pltpu.make_async_copy(src, outs[t].at[me], local_sems.at[t])
            cp.start()
            copies.append(cp)
        remote = []
        for kk in range(1, N_DEV):
            px, py, pc = x ^ (kk >> 2), y ^ ((kk >> 1) & 1), c ^ (kk & 1)
            peer = 4 * px + 2 * py + pc
            for t in range(nt):
                src = ins[t].at[peer] if scatter[t] else ins[t]
                cp = pltpu.make_async_remote_copy(
                    src_ref=src, dst_ref=outs[t].at[me], send_sem=send_sems.at[t, kk], recv_sem=recv_sems.at[t, kk],
                    device_id=(px, py, pc), device_id_type=pl.DeviceIdType.MESH)
                cp.start()
                remote.append(cp)
        for cp in remote:
            cp.wait()
        for cp in copies:
            cp.wait()

    hbm = pl.BlockSpec(memory_space=pl.ANY)
    out_shape = [jax.ShapeDtypeStruct(a.shape if scatter[t] else (N_DEV,) + a.shape, a.dtype)
                 for t, a in enumerate(arrays)]
    return pl.pallas_call(
        body, name=name, in_specs=[hbm] * nt, out_specs=[hbm] * nt, out_shape=out_shape,
        scratch_shapes=[pltpu.SemaphoreType.DMA((nt, N_DEV)), pltpu.SemaphoreType.DMA((nt, N_DEV)),
                        pltpu.SemaphoreType.DMA((nt,))],
        compiler_params=pltpu.CompilerParams(has_side_effects=True))(*arrays)


def _peer_of(kk):
    x, y, c = lax.axis_index("x"), lax.axis_index("y"), lax.axis_index("c")
    return x ^ (kk >> 2), y ^ ((kk >> 1) & 1), c ^ (kk & 1)


def _peer_copy(t, kk, scatter, ins, lands, send_sems, recv_sems):
    px, py, pc = _peer_of(kk)
    me = _my_index()
    src = ins[t].at[4 * px + 2 * py + pc] if scatter[t] else ins[t]
    return pltpu.make_async_remote_copy(
        src_ref=src, dst_ref=lands[t].at[me], send_sem=send_sems.at[t * N_DEV + kk],
        recv_sem=recv_sems.at[t * N_DEV + kk], device_id=(px, py, pc), device_id_type=pl.DeviceIdType.MESH)


def _own_copy(t, scatter, ins, lands, own_sems):
    me = _my_index()
    return pltpu.make_async_copy(ins[t].at[me] if scatter[t] else ins[t], lands[t].at[me], own_sems.at[t])


_HBM = pl.BlockSpec(memory_space=pltpu.HBM)
_SEM = pl.BlockSpec(memory_space=pltpu.SEMAPHORE)
_EFFECT = pltpu.SideEffectType.DATAFLOW_SIDE_EFFECTING


def exchange_start(arrays, scatter, *, name):
    nt = len(arrays)
    land_shapes = [a.shape if scatter[t] else (N_DEV,) + a.shape for t, a in enumerate(arrays)]

    def body(*refs):
        ins, lands = refs[:nt], refs[nt:2 * nt]
        send_sems, recv_sems, own_sems = refs[2 * nt:2 * nt + 3]
        token = refs[-1]
        for kk in range(1, N_DEV):
            for t in range(nt):
                _peer_copy(t, kk, scatter, ins, lands, send_sems, recv_sems).start()
        for t in range(nt):
            _own_copy(t, scatter, ins, lands, own_sems).start()
        token[...] = jnp.zeros_like(token)

    sems = pltpu.SemaphoreType.DMA((nt * N_DEV,))
    outs = pl.pallas_call(
        body, name=name,
        out_shape=(sems, sems, pltpu.SemaphoreType.DMA((nt,)), *[pltpu.HBM(a.shape, a.dtype) for a in arrays],
                   *[pltpu.HBM(shp, a.dtype) for shp, a in zip(land_shapes, arrays)],
                   jax.ShapeDtypeStruct((8, 128), F32)),
        in_specs=[_HBM] * (2 * nt),
        out_specs=(_SEM, _SEM, _SEM, *[_HBM] * (2 * nt), pl.BlockSpec(memory_space=pltpu.VMEM)),
        input_output_aliases={i: 3 + i for i in range(2 * nt)},
        compiler_params=pltpu.CompilerParams(has_side_effects=_EFFECT),
    )(*[pltpu.with_memory_space_constraint(a, pltpu.HBM) for a in arrays],
      *[pltpu.with_memory_space_constraint(lax.empty(shp, a.dtype), pltpu.HBM) for shp, a in zip(land_shapes, arrays)])
    return (outs[:3], outs[3:3 + nt], outs[3 + nt:3 + 2 * nt], scatter), outs[-1]


def exchange_wait(handle, after, *, name):
    sems, thru, lands, scatter = handle
    nt = len(thru)

    def body(*refs):
        ins, lnd = refs[:nt], refs[nt:2 * nt]
        s_sems, r_sems, o_sems = refs[2 * nt:2 * nt + 3]
        for kk in range(1, N_DEV):
            for t in range(nt):
                cp = _peer_copy(t, kk, scatter, ins, lnd, s_sems, r_sems)
                cp.wait_send()
                cp.wait_recv()
        for t in range(nt):
            _own_copy(t, scatter, ins, lnd, o_sems).wait()

    outs = pl.pallas_call(
        body, name=name,
        out_shape=(*[pltpu.HBM(a.shape, a.dtype) for a in thru], *[pltpu.HBM(a.shape, a.dtype) for a in lands]),
        in_specs=[_HBM] * (2 * nt) + [_SEM, _SEM, _SEM, pl.BlockSpec(memory_space=pl.ANY)],
        out_specs=tuple([_HBM] * (2 * nt)),
        input_output_aliases={i: i for i in range(2 * nt)},
        compiler_params=pltpu.CompilerParams(has_side_effects=_EFFECT),
    )(*thru, *lands, *sems, after)
    return outs[nt:]


def _cols_from_shards(g):
    g = jnp.moveaxis(g, 0, -2)
    return g.reshape(g.shape[:-2] + (g.shape[-2] * g.shape[-1],))


def _cols_to_shards(w, nshards=N_DEV):
    w = w.reshape(w.shape[:-1] + (nshards, w.shape[-1] // nshards))
    return jnp.moveaxis(w, -2, 0)


def _half_shards(halves):
    return jnp.concatenate([_cols_to_shards(h[None], N_DEV // 2) for h in halves], axis=0)


def _ffn_fwd(x, h, w_up, cw, cb, get_w_down, next_g, tag):
    up_g = matmul(h, w_up[0], name=f"{tag}_up_g")
    up_v = matmul(h, w_up[1], name=f"{tag}_up_v")
    act, conv_g, conv_v = ffn_mid_fwd(up_g, up_v, cw, cb, name=f"{tag}_mid")
    w_down = get_w_down(act)
    out = matmul(act, w_down, res=x, norm_g=next_g, name=f"{tag}_down")
    return out, (h, up_g, up_v, conv_g, conv_v, act), w_down


def _ffn_bwd(dx, dxb, x, saved, g, w_up, cw, w_down, tag):
    h, up_g, up_v, conv_g, conv_v, act = saved
    dact = matmul(dxb, w_down, tb=True, name=f"{tag}_ddown")
    d_w_down = matmul_ta(act, dxb, name=f"{tag}_gdown")
    dug, duv, dcw, dcb = ffn_mid_bwd(dact, up_g, up_v, conv_g, conv_v, cw, name=f"{tag}_dmid")
    d_w_up = (matmul_ta(h, dug, name=f"{tag}_gup_g"), matmul_ta(h, duv, name=f"{tag}_gup_v"))
    dh = matmul(dug, w_up[0], tb=True, name=f"{tag}_dup_g")
    dh = matmul(duv, w_up[1], tb=True, res=dh, name=f"{tag}_dup_v")
    dx2, dxb2, dg = rms_bwd(dh, x, g, dx, name=f"{tag}_dnorm")
    return dx2, dxb2, dict(norm_g=dg, w_up=d_w_up, conv_w=dcw, conv_b=dcb, w_down=d_w_down)


def local_step(x0, tgt, a, weights, grads_out):
    s, d = x0.shape
    aw = a['even_v_ln_g'].shape[-1]
    causal = jnp.tril(jnp.ones((CHUNK, CHUNK), dtype=bool))
    wm = jnp.where(causal, a['even_w_s'][0], 0.0).astype(BF16)
    wmt = jnp.swapaxes(wm, 1, 2)
    bm = jnp.repeat(a['even_b_s'][0].T, HEAD, axis=1)
    sel = (jnp.arange(aw)[:, None] // HEAD == jnp.arange(128)[None, :]).astype(BF16)
    cos, sin = rope_tables(s)
    ffn_g, ffn_cb = a['ffn_norm_g'], a['ffn_conv_b']

    w0 = weights(0, None)
    w_in, conv_w, odd_g, ffn_cw = w0['w_in'], w0['conv_w'], w0['odd_g'], w0['ffn_cw']
    h0 = rms_fwd(x0, w0['even_g'], name="even_norm")
    z = matmul(h0, w_in, bias=a['even_b_in'], name="even_in")
    ycat, hc = even_mid_fwd(z, a['even_v_ln_g'], a['even_v_ln_b'], wm, bm, conv_w, a['even_conv_b'],
                            a['even_conv_ln_g'], a['even_conv_ln_b'], name="even_mid")
    w1 = weights(1, ycat)
    x1, h1 = matmul(ycat, w1['w_out'], res=x0, norm_g=ffn_g[0:1], name="even_out")
    (x2, h2), ffn0, w_down0 = _ffn_fwd(x1, h1, w1['w_up'], ffn_cw[0], ffn_cb[0:1],
                                       lambda act: weights(2, act)['w_down'], odd_g, "ffn0")
    w2 = weights(3, h2)
    qkv = matmul(h2, w2['w_qkv'], name="odd_qkv")
    q, k = rope_fwd(qkv, cos, sin, name="rope")
    o, lse = attn_fwd(q, k, qkv, name="attn_fwd")
    x3, h3 = matmul(o, w2['w_o'], res=x2, norm_g=ffn_g[1:2], name="odd_out")
    w3 = weights(4, x3)
    x4, ffn1, _ = _ffn_fwd(x3, h3, w3['w_up'], ffn_cw[1], ffn_cb[1:2], lambda act: w3['w_down'], None, "ffn1")
    loss_t, dx, dxb, d_final_g = final_loss_bwd(x4, a['final_norm_g'].reshape(1, -1), tgt, name="final_loss")

    dx, dxb, g1 = _ffn_bwd(dx, dxb, x3, ffn1, ffn_g[1:2], w3['w_up'], ffn_cw[1], w3['w_down'], "ffn1")
    dep = grads_out(0, dict(w_up=g1['w_up'], w_down=g1['w_down']))
    do = matmul(dxb, w2['w_o'], tb=True, dep=dep, name="odd_dout")
    d_w_o = matmul_ta(o, dxb, name="odd_gout")
    delta = attn_delta(do, o, name="attn_delta")
    dq = attn_dq(q, k, qkv, do, lse, delta, name="attn_dq")
    dk, dv = attn_dkv(q, k, qkv, do, lse, delta, name="attn_dkv")
    dqkv = rope_bwd(dq, dk, dv, cos, sin, name="rope_bwd")
    d_w_qkv = matmul_ta(h2, dqkv, name="odd_gqkv")
    dep = grads_out(1, dict(w_qkv=d_w_qkv, w_o=d_w_o))
    dh2 = matmul(dqkv, w2['w_qkv'], tb=True, dep=dep, name="odd_dqkv")
    dx, dxb, d_odd_g = rms_bwd(dh2, x2, odd_g, dx, name="odd_dnorm")
    dx, dxb, g0 = _ffn_bwd(dx, dxb, x1, ffn0, ffn_g[0:1], w1['w_up'], ffn_cw[0], w_down0, "ffn0")
    dep = grads_out(2, dict(w_up=g0['w_up'], w_down=g0['w_down']))
    d_w_out = matmul_ta(ycat, dxb, dep=dep, name="even_gout")
    dep = grads_out(3, dict(w_out=d_w_out))
    dycat = matmul(dxb, w1['w_out'], tb=True, dep=dep, name="even_dout")
    (dza, dhc, dba, dvg, dvb, dwm, dbs, dcg, dcbeta, dcb) = even_mid_bwd_rows(
        dycat, z, hc, a['even_v_ln_g'], a['even_v_ln_b'], wm, wmt, bm, sel, a['even_conv_ln_g'],
        a['even_conv_ln_b'], name="even_dmid_rows")
    dzb, dbb, dcw = even_conv_bwd(dhc, z, conv_w, name="even_dmid_conv")
    nh = a['even_w_s'].shape[1]
    small_grads = {
        'even_b_in': jnp.concatenate([dba, dbb], axis=1), 'even_v_ln_g': dvg,
        'even_v_ln_b': dvb, 'even_w_s': jnp.where(causal, dwm, 0.0)[None], 'even_b_s': dbs[:, :nh].T[None],
        'even_conv_w': dcw[None], 'even_conv_b': dcb, 'even_conv_ln_g': dcg, 'even_conv_ln_b': dcbeta,
        'odd_norm_g': d_odd_g, 'ffn_norm_g': jnp.concatenate([g0['norm_g'], g1['norm_g']], axis=0),
        'ffn_conv_w': jnp.stack([g0['conv_w'], g1['conv_w']]),
        'ffn_conv_b': jnp.concatenate([g0['conv_b'], g1['conv_b']], axis=0),
        'final_norm_g': d_final_g.reshape(-1),
    }
    dep = grads_out(5, dict(small=small_grads))
    d_w_in = (matmul_ta(h0, dza, dep=dep, name="even_gin_a"), matmul_ta(h0, dzb, name="even_gin_b"))
    dep = grads_out(4, dict(w_in=d_w_in))
    dh0 = matmul(dza, w_in, tb=True, bk=0, dep=dep, name="even_din_a")
    dh0 = matmul(dzb, w_in, tb=True, bk=1, res=dh0, name="even_din_b")
    grad_x, _, d_even_g = rms_bwd(dh0, x0, w0['even_g'], dx, name="even_dnorm")
    last = {'even_norm_g': d_even_g}
    grads_out(6, dict(small=last))
    return loss_t, grad_x, {**last, **small_grads}


BIG = ['even_w_in', 'even_w_out', 'odd_w_qkv', 'odd_w_o', 'ffn_w_up', 'ffn_w_down']


def _as_tiles(flat, dtype=F32):
    return jnp.pad(flat, (0, (-flat.size) % 2048)).reshape(-1, 128).astype(dtype)


def kernel(*args):
    a = dict(zip(NAMES + ['loss_target'] + ['m_' + n for n in WEIGHTS] + ['v_' + n for n in WEIGHTS], args))
    x0 = a['x'][0]
    tgt = a['loss_target'][0]
    s, d = x0.shape
    me = _my_index()
    bf = lambda t: t.astype(BF16)

    small_local = _as_tiles(jnp.concatenate([a['even_conv_w'].reshape(-1), a['odd_norm_g'].reshape(-1),
                                             a['ffn_conv_w'].reshape(-1)]))
    stage_arrays = [
        [bf(a['even_w_in']), small_local],
        [bf(a['even_w_out']), bf(a['ffn_w_up'][0:1])],
        [bf(a['ffn_w_down'][0:1])],
        [bf(a['odd_w_qkv']), bf(a['odd_w_o'])],
        [bf(a['ffn_w_up'][1:2]), bf(a['ffn_w_down'][1:2])],
    ]
    started = [exchange_start(arrs, [False] * len(arrs), name=f"gather{i}_start") for i, arrs in enumerate(stage_arrays)]
    order = sum(tok[0, 0] for _, tok in started)

    def weights(stage, after):
        handle, tok = started[stage]
        full = exchange_wait(handle, tok if after is None else after, name=f"gather{stage}_wait")
        rows = lambda g: jnp.moveaxis(g, 0, 1).reshape(-1, d)
        halves = lambda g: (_cols_from_shards(g[:N_DEV // 2])[0], _cols_from_shards(g[N_DEV // 2:])[0])
        if stage == 0:
            gs = full[1].reshape(N_DEV, -1)
            n_cw, n_og, n_fw = a['even_conv_w'].size, a['odd_norm_g'].size, a['ffn_conv_w'].size
            return dict(
                w_in=_cols_from_shards(full[0])[0], even_g=a['even_norm_g'] + order,
                conv_w=_cols_from_shards(gs[:, :n_cw].reshape((N_DEV,) + a['even_conv_w'].shape))[0],
                odd_g=gs[:, n_cw:n_cw + n_og].reshape(1, -1),
                ffn_cw=_cols_from_shards(gs[:, n_cw + n_og:n_cw + n_og + n_fw].reshape((N_DEV,) + a['ffn_conv_w'].shape)))
        if stage == 1:
            return dict(w_out=rows(full[0]), w_up=halves(full[1]))
        if stage == 2:
            return dict(w_down=rows(full[0]))
        if stage == 3:
            return dict(w_qkv=_cols_from_shards(full[0])[0], w_o=rows(full[1]))
        return dict(w_up=halves(full[0]), w_down=rows(full[1]))

    sent = {}

    def grads_out(stage, g):
        to_rows = lambda w: w.reshape(N_DEV, 1, -1, d)
        if stage in (0, 2):
            pieces, scatter = [_half_shards(g['w_up']), to_rows(g['w_down'])], [True, True]
        elif stage == 1:
            pieces, scatter = [_cols_to_shards(g['w_qkv'][None]), to_rows(g['w_o'])], [True, True]
        elif stage == 3:
            pieces, scatter = [to_rows(g['w_out'])], [True]
        elif stage == 4:
            pieces, scatter = [_half_shards(g['w_in'])], [True]
        else:
            small = jnp.concatenate([g['small'][n].reshape(-1) for n in g['small']])
            pieces, scatter = [_as_tiles(small, BF16)], [False]
        sent[stage], tok = exchange_start(pieces, scatter, name=f"grads{stage}_start")
        return tok

    loss_t, grad_x, small_grads = local_step(x0, tgt, a, weights, grads_out)
    loss = lax.psum(loss_t[0, 0], ("x", "y", "c"))
    received = {stage: exchange_wait(handle, grad_x, name=f"grads{stage}_wait") for stage, handle in sent.items()}

    results = {}
    big_parts = {'even_w_in': [received[4][0]], 'even_w_out': [received[3][0]], 'odd_w_qkv': [received[1][0]],
                 'odd_w_o': [received[1][1]], 'ffn_w_up': [received[2][0], received[0][0]],
                 'ffn_w_down': [received[2][1], received[0][1]]}
    for n in BIG:
        shp = a[n].shape
        flat = lambda t: t.reshape(-1, shp[-1])
        outs = adamw([p.reshape(N_DEV, -1, shp[-1]) for p in big_parts[n]], flat(a[n]), flat(a['m_' + n]),
                     flat(a['v_' + n]), name=f"adamw_{n}")
        results[n] = [t.reshape(shp) for t in outs]

    small_names = list(small_grads)
    n_small = sum(small_grads[n].size for n in small_names)
    n_last = small_grads[small_names[0]].size
    rs = jnp.concatenate([received[6][0].reshape(N_DEV, -1)[:, :n_last],
                          received[5][0].reshape(N_DEV, -1)[:, :n_small - n_last]], axis=1)
    parts, offs = [], 0
    for n in small_names:
        full = small_grads[n].shape
        piece = rs[:, offs:offs + small_grads[n].size].reshape((N_DEV,) + full)
        offs += small_grads[n].size
        shp = a[n].shape
        if shp != full:
            width = shp[-1]
            piece = lax.dynamic_slice_in_dim(piece, me * width, width, axis=piece.ndim - 1)
        parts.append(piece.reshape(N_DEV, -1))
    parts = jnp.concatenate(parts, axis=1)
    pad = (-parts.shape[1]) % 2048
    cat = lambda pre: _as_tiles(jnp.concatenate([a[pre + n].reshape(-1) for n in small_names]))
    outs = adamw([jnp.pad(parts, ((0, 0), (0, pad))).reshape(N_DEV, -1, 128)], cat(''), cat('m_'), cat('v_'),
                 name="adamw_small")
    offs = 0
    for n in small_names:
        size = a[n].size
        results[n] = [t.reshape(-1)[offs:offs + size].reshape(a[n].shape) for t in outs]
        offs += size

    out = [loss, grad_x[None]]
    for i in range(4):
        out += [results[n][i] for n in WEIGHTS]
    return tuple(out)
```

```python
import functools
import math

import jax
import jax.numpy as jnp
from jax import lax
from jax.experimental import pallas as pl
from jax.experimental.pallas import tpu as pltpu

F32 = jnp.float32
BF16 = jnp.bfloat16

N_DEV = 8
EPS = 1e-6
NEG = -1e30
HEAD = 64
CHUNK = 128
BLOCK = 128
CONV_K = 31
FFN_K = 3
DILATIONS = (1, 4, 16)
ROPE_THETA = 10000.0
LR, B1, B2, ADAM_EPS, WD, STEP = 0.001, 0.9, 0.999, 1e-08, 0.01, 10

VMEM_LIMIT = 56 * 1024 * 1024
VMEM_BUDGET = 32 * 1024 * 1024
ROWS = 512
HALO = 32
FHALO = 8

NAMES = ['x', 'even_norm_g', 'even_w_in', 'even_b_in', 'even_v_ln_g', 'even_v_ln_b', 'even_w_s', 'even_b_s',
         'even_conv_w', 'even_conv_b', 'even_conv_ln_g', 'even_conv_ln_b', 'even_w_out', 'odd_norm_g',
         'odd_w_qkv', 'odd_w_o', 'ffn_norm_g', 'ffn_w_up', 'ffn_conv_w', 'ffn_conv_b', 'ffn_w_down',
         'final_norm_g']
WEIGHTS = NAMES[1:]


def _params(sem=None):
    return pltpu.CompilerParams(dimension_semantics=sem, vmem_limit_bytes=VMEM_LIMIT)


def _sigmoid(x):
    return 1.0 / (1.0 + jnp.exp(-x))


def _gelu(x):
    c = math.sqrt(2.0 / math.pi)
    return 0.5 * x * (1.0 + jnp.tanh(c * (x + 0.044715 * x * x * x)))


def _gelu_grad(x):
    c = math.sqrt(2.0 / math.pi)
    t = jnp.tanh(c * (x + 0.044715 * x * x * x))
    return 0.5 * (1.0 + t) + 0.5 * x * (1.0 - t * t) * c * (1.0 + 3.0 * 0.044715 * x * x)


def _ln_stats(x):
    mu = jnp.mean(x, axis=-1, keepdims=True)
    xc = x - mu
    rstd = lax.rsqrt(jnp.mean(xc * xc, axis=-1, keepdims=True) + EPS)
    return xc * rstd, rstd


def _ln_bwd(dy, xhat, rstd, g):
    dxh = dy * g
    return rstd * (dxh - jnp.mean(dxh, axis=-1, keepdims=True) - xhat * jnp.mean(dxh * xhat, axis=-1, keepdims=True))


def _colsum(x):
    return jnp.sum(x, axis=0, keepdims=True)


def _split3(x):
    hi = x.astype(BF16)
    r = x - hi.astype(F32)
    mid = r.astype(BF16)
    lo = (r - mid.astype(F32)).astype(BF16)
    return hi, mid, lo


def _dot(a, b, dims):
    return lax.dot_general(a, b, (dims, ((), ())), preferred_element_type=F32)


NN = ((1,), (0,))
NT = ((1,), (1,))
TN = ((0,), (0,))


def _divisors(n, cands):
    return [c for c in cands if c <= n and n % c == 0]


def _pick_tiles(m, n, k, a_bytes, b_bytes, o_bytes, extra_bytes):
    best = None
    for tm in _divisors(m, (1024, 512, 256, 128)):
        for tn in _divisors(n, (1408, 1024, 768, 704, 512, 384, 256, 128)):
            if tn % 128:
                continue
            need = 2 * (tm * k * a_bytes + k * tn * b_bytes + tm * tn * (o_bytes + extra_bytes)) + tm * tn * 4
            if need <= VMEM_BUDGET and (best is None or tm * tn > best[0] * best[1]):
                best = (tm, tn)
    assert best is not None, (m, n, k)
    return best


def matmul(a, b, *, tb=False, bk=0, bias=None, res=None, norm_g=None, dep=None, out_dtype=F32, name):
    m, k = a.shape
    n = b.shape[0] if tb else b.shape[1]
    assert b.shape[1] % k == 0 if tb else (b.shape[0] == k and bk == 0)
    if norm_g is None:
        tm, tn = _pick_tiles(m, n, k, a.dtype.itemsize, b.dtype.itemsize, jnp.dtype(out_dtype).itemsize,
                             4 if res is not None else 0)
    else:
        tm, tn = _divisors(m, (512,))[0], n

    def body(*refs):
        a_ref, b_ref = refs[:2]
        o_ref = refs[-2] if norm_g is not None else refs[-1]
        acc = _dot(a_ref[...].astype(BF16), b_ref[...].astype(BF16), NT if tb else NN)
        pos = 2
        if bias is not None:
            acc = acc + refs[pos][...]
            pos += 1
        if res is not None:
            acc = acc + refs[pos][...]
            pos += 1
        o_ref[...] = acc.astype(out_dtype)
        if norm_g is not None:
            r = lax.rsqrt(jnp.mean(acc * acc, axis=-1, keepdims=True) + EPS)
            refs[-1][...] = (acc * r * refs[pos][...]).astype(BF16)

    in_specs = [pl.BlockSpec((tm, k), lambda i, j: (i, 0)),
                pl.BlockSpec((tn, k), lambda i, j: (j, bk)) if tb else pl.BlockSpec((k, tn), lambda i, j: (0, j))]
    args = [a, b]
    if bias is not None:
        in_specs.append(pl.BlockSpec((1, tn), lambda i, j: (0, j)))
        args.append(bias)
    if res is not None:
        in_specs.append(pl.BlockSpec((tm, tn), lambda i, j: (i, j)))
        args.append(res)
    if norm_g is not None:
        in_specs.append(pl.BlockSpec((1, tn), lambda i, j: (0, j)))
        args.append(norm_g)
    if dep is not None:
        in_specs.append(pl.BlockSpec(memory_space=pl.ANY))
        args.append(dep)
    blk = pl.BlockSpec((tm, tn), lambda i, j: (i, j))
    out_shape = jax.ShapeDtypeStruct((m, n), out_dtype)
    return pl.pallas_call(
        body, name=name, grid=(m // tm, n // tn), in_specs=in_specs,
        out_specs=blk if norm_g is None else [blk, blk],
        out_shape=out_shape if norm_g is None else [out_shape, jax.ShapeDtypeStruct((m, n), BF16)],
        compiler_params=_params(("parallel", "parallel")))(*args)


def matmul_ta(a, b, *, dep=None, out_dtype=BF16, name):
    s, m = a.shape
    n = b.shape[1]
    assert b.shape[0] == s
    best = None
    for tm in _divisors(m, (512, 256, 128)):
        for tn in _divisors(n, (1024, 512, 384, 256, 128)):
            need = 2 * (s * tm * a.dtype.itemsize + s * tn * b.dtype.itemsize + tm * tn * 2) + tm * tn * 4
            if need <= VMEM_BUDGET and (best is None or tm * tn > best[0] * best[1]):
                best = (tm, tn)
    tm, tn = best

    def body(*refs):
        a_ref, b_ref, o_ref = refs[0], refs[1], refs[-1]
        o_ref[...] = _dot(a_ref[...].astype(BF16), b_ref[...].astype(BF16), TN).astype(out_dtype)

    in_specs = [pl.BlockSpec((s, tm), lambda i, j: (0, i)), pl.BlockSpec((s, tn), lambda i, j: (0, j))]
    args = [a, b]
    if dep is not None:
        in_specs.append(pl.BlockSpec(memory_space=pl.ANY))
        args.append(dep)
    return pl.pallas_call(
        body, name=name, grid=(m // tm, n // tn), in_specs=in_specs,
        out_specs=pl.BlockSpec((tm, tn), lambda i, j: (i, j)),
        out_shape=jax.ShapeDtypeStruct((m, n), out_dtype),
        compiler_params=_params(("parallel", "parallel")))(*args)


def rms_fwd(x, g, *, name):
    s, d = x.shape

    def body(x_ref, g_ref, h_ref):
        xv = x_ref[...]
        r = lax.rsqrt(jnp.mean(xv * xv, axis=-1, keepdims=True) + EPS)
        h_ref[...] = (xv * r * g_ref[...]).astype(BF16)

    return pl.pallas_call(
        body, name=name, grid=(s // ROWS,),
        in_specs=[pl.BlockSpec((ROWS, d), lambda i: (i, 0)), pl.BlockSpec((1, d), lambda i: (0, 0))],
        out_specs=pl.BlockSpec((ROWS, d), lambda i: (i, 0)),
        out_shape=jax.ShapeDtypeStruct((s, d), BF16),
        compiler_params=_params(("parallel",)))(x, g)


def rms_bwd(dh, x, g, dres, *, name):
    s, d = x.shape

    def body(dh_ref, x_ref, g_ref, dres_ref, dx_ref, dxb_ref, dg_ref):
        xv = x_ref[...]
        r = lax.rsqrt(jnp.mean(xv * xv, axis=-1, keepdims=True) + EPS)
        xhat = xv * r
        dhv = dh_ref[...]
        dxh = dhv * g_ref[...]
        dx = dres_ref[...] + r * (dxh - xhat * jnp.mean(dxh * xhat, axis=-1, keepdims=True))
        dx_ref[...] = dx
        dxb_ref[...] = dx.astype(BF16)

        @pl.when(pl.program_id(0) == 0)
        def _():
            dg_ref[...] = jnp.zeros_like(dg_ref)
        dg_ref[...] += _colsum(dhv * xhat)

    row = pl.BlockSpec((ROWS, d), lambda i: (i, 0))
    vec = pl.BlockSpec((1, d), lambda i: (0, 0))
    return pl.pallas_call(
        body, name=name, grid=(s // ROWS,),
        in_specs=[row, row, vec, row], out_specs=[row, row, vec],
        out_shape=[jax.ShapeDtypeStruct((s, d), F32), jax.ShapeDtypeStruct((s, d), BF16),
                   jax.ShapeDtypeStruct((1, d), F32)],
        compiler_params=_params(("arbitrary",)))(dh, x, g, dres)


def final_loss_bwd(x, g, tgt, *, name):
    s, d = x.shape

    def body(x_ref, g_ref, t_ref, loss_ref, dx_ref, dxb_ref, dg_ref):
        xv = x_ref[...]
        gv = g_ref[...]
        r = lax.rsqrt(jnp.mean(xv * xv, axis=-1, keepdims=True) + EPS)
        xhat = xv * r
        e = xhat * gv - t_ref[...]
        dy = e * (1.0 / d)
        dxh = dy * gv
        dx = r * (dxh - xhat * jnp.mean(dxh * xhat, axis=-1, keepdims=True))
        dx_ref[...] = dx
        dxb_ref[...] = dx.astype(BF16)

        @pl.when(pl.program_id(0) == 0)
        def _():
            dg_ref[...] = jnp.zeros_like(dg_ref)
            loss_ref[...] = jnp.zeros_like(loss_ref)
        dg_ref[...] += _colsum(dy * xhat)
        loss_ref[...] += 0.5 * jnp.sum(jnp.mean(e * e, axis=-1, keepdims=True))

    row = pl.BlockSpec((ROWS, d), lambda i: (i, 0))
    vec = pl.BlockSpec((1, d), lambda i: (0, 0))
    one = pl.BlockSpec((8, 128), lambda i: (0, 0))
    return pl.pallas_call(
        body, name=name, grid=(s // ROWS,),
        in_specs=[row, vec, row], out_specs=[one, row, row, vec],
        out_shape=[jax.ShapeDtypeStruct((8, 128), F32), jax.ShapeDtypeStruct((s, d), F32),
                   jax.ShapeDtypeStruct((s, d), BF16), jax.ShapeDtypeStruct((1, d), F32)],
        compiler_params=_params(("arbitrary",)))(x, g, tgt)


def _pair_masks():
    lane = lax.broadcasted_iota(jnp.int32, (CHUNK, 128), 1)
    return lane < HEAD


def _head_keep(rows):
    lane = lax.broadcasted_iota(jnp.int32, (rows, 128), 1)
    first = jnp.where(lane < HEAD, 1.0, 0.0)
    return first.astype(BF16), (1.0 - first).astype(BF16)


def _gating_mixed(vn_b, wm_ref, lo):
    rows, aw = vn_b.shape
    out = []
    for c in range(rows // CHUNK):
        tiles = []
        for p in range(aw // 128):
            vp = vn_b[c * CHUNK:(c + 1) * CHUNK, p * 128:(p + 1) * 128]
            r0 = _dot(wm_ref[2 * p], vp, NN)
            r1 = _dot(wm_ref[2 * p + 1], vp, NN)
            tiles.append(jnp.where(lo, r0, r1))
        out.append(jnp.concatenate(tiles, axis=1))
    return jnp.concatenate(out, axis=0)


def even_mid_fwd(z, vg, vb, wm, bm, cw, cb, cg, cbeta, *, name):
    s, zw = z.shape
    aw = zw // 4
    nblk = s // ROWS

    def body(z_ref, zp_ref, vg_ref, vb_ref, wm_ref, bm_ref, cw_ref, cb_ref, cg_ref, cbeta_ref,
             y_ref, hc_ref, ext_ref):
        i = pl.program_id(0)
        lo = _pair_masks()
        u = _gelu(z_ref[:, 0:aw])
        v = _gelu(z_ref[:, aw:2 * aw])
        vhat, _ = _ln_stats(v)
        vn = (vhat * vg_ref[...] + vb_ref[...]).astype(BF16)
        mixed = _gating_mixed(vn, wm_ref, lo)
        bias = jnp.concatenate([bm_ref[...]] * (ROWS // CHUNK), axis=0)
        y_ref[:, 0:aw] = (u * (mixed + bias)).astype(BF16)

        hb = z_ref[:, 2 * aw:3 * aw] * _sigmoid(z_ref[:, 3 * aw:4 * aw])
        hbp = zp_ref[:, 0:aw] * _sigmoid(zp_ref[:, aw:2 * aw])
        ext_ref[0:HALO, :] = jnp.where(i > 0, hbp, 0.0)
        ext_ref[HALO:HALO + ROWS, :] = hb
        acc = jnp.zeros((ROWS, aw), F32) + cb_ref[...]
        for k in range(CONV_K):
            acc = acc + cw_ref[k:k + 1, :] * ext_ref[pl.ds(HALO - (CONV_K - 1) + k, ROWS), :]
        hc_ref[...] = acc
        hhat, _ = _ln_stats(acc)
        hn = hhat * cg_ref[...] + cbeta_ref[...]
        y_ref[:, aw:2 * aw] = (hn * _sigmoid(hn)).astype(BF16)

    hb_per = ROWS // HALO
    vec = pl.BlockSpec((1, aw), lambda i: (0, 0))
    return pl.pallas_call(
        body, name=name, grid=(nblk,),
        in_specs=[pl.BlockSpec((ROWS, zw), lambda i: (i, 0)),
                  pl.BlockSpec((HALO, 2 * aw), lambda i: (jnp.maximum(i * hb_per - 1, 0), 1)),
                  vec, vec,
                  pl.BlockSpec(wm.shape, lambda i: (0, 0, 0)),
                  pl.BlockSpec((CHUNK, aw), lambda i: (0, 0)),
                  pl.BlockSpec((CONV_K, aw), lambda i: (0, 0)), vec, vec, vec],
        out_specs=[pl.BlockSpec((ROWS, 2 * aw), lambda i: (i, 0)), pl.BlockSpec((ROWS, aw), lambda i: (i, 0))],
        out_shape=[jax.ShapeDtypeStruct((s, 2 * aw), BF16), jax.ShapeDtypeStruct((s, aw), F32)],
        scratch_shapes=[pltpu.VMEM((HALO + ROWS, aw), F32)],
        compiler_params=_params(("parallel",)))(z, z, vg, vb, wm, bm, cw, cb, cg, cbeta)


def even_mid_bwd_rows(dy, z, hc, vg, vb, wm, wmt, bm, sel, cg, cbeta, *, name):
    s, zw = z.shape
    aw = zw // 4
    nh = wm.shape[0]

    def body(dy_ref, z_ref, hc_ref, vg_ref, vb_ref, wm_ref, wmt_ref, bm_ref, sel_ref, cg_ref, cbeta_ref,
             dza_ref, dhc_ref, dba_ref, dvg_ref, dvb_ref, dwm_ref, dbs_ref, dcg_ref, dcbeta_ref, dcb_ref):
        @pl.when(pl.program_id(0) == 0)
        def _():
            for r in (dba_ref, dvg_ref, dvb_ref, dwm_ref, dbs_ref, dcg_ref, dcbeta_ref, dcb_ref):
                r[...] = jnp.zeros_like(r)

        lo = _pair_masks()
        keep = _head_keep(CHUNK)
        zu = z_ref[:, 0:aw]
        zv = z_ref[:, aw:2 * aw]
        u = _gelu(zu)
        v = _gelu(zv)
        vhat, vrstd = _ln_stats(v)
        vn = (vhat * vg_ref[...] + vb_ref[...]).astype(BF16)
        mixed = _gating_mixed(vn, wm_ref, lo)
        bias = jnp.concatenate([bm_ref[...]] * (ROWS // CHUNK), axis=0)
        dya = dy_ref[:, 0:aw]
        du = dya * (mixed + bias)
        dmix = dya * u
        dmix_b = dmix.astype(BF16)

        dvn_rows = []
        for c in range(ROWS // CHUNK):
            rs = slice(c * CHUNK, (c + 1) * CHUNK)
            tiles = []
            for p in range(aw // 128):
                cs = slice(p * 128, (p + 1) * 128)
                dm = dmix_b[rs, cs]
                dm0 = dm * keep[0]
                dm1 = dm * keep[1]
                vp = vn[rs, cs]
                tiles.append(_dot(wmt_ref[2 * p], dm0, NN) + _dot(wmt_ref[2 * p + 1], dm1, NN))
                dwm_ref[2 * p] += _dot(dm0, vp, NT)
                dwm_ref[2 * p + 1] += _dot(dm1, vp, NT)
            dvn_rows.append(jnp.concatenate(tiles, axis=1))
            acc = jnp.zeros((CHUNK, 128), F32)
            for part in _split3(dmix[rs, :]):
                acc = acc + _dot(part, sel_ref[...], NN)
            dbs_ref[...] += acc
        dvn = jnp.concatenate(dvn_rows, axis=0)
        dvg_ref[...] += _colsum(dvn * vhat)
        dvb_ref[...] += _colsum(dvn)
        dv = _ln_bwd(dvn, vhat, vrstd, vg_ref[...])
        dzu = du * _gelu_grad(zu)
        dzv = dv * _gelu_grad(zv)
        dza_ref[:, 0:aw] = dzu.astype(BF16)
        dza_ref[:, aw:2 * aw] = dzv.astype(BF16)
        dba_ref[:, 0:aw] += _colsum(dzu)
        dba_ref[:, aw:2 * aw] += _colsum(dzv)

        hcv = hc_ref[...]
        hhat, hrstd = _ln_stats(hcv)
        hn = hhat * cg_ref[...] + cbeta_ref[...]
        sg = _sigmoid(hn)
        dhn = dy_ref[:, aw:2 * aw] * (sg * (1.0 + hn * (1.0 - sg)))
        dcg_ref[...] += _colsum(dhn * hhat)
        dcbeta_ref[...] += _colsum(dhn)
        dhc = _ln_bwd(dhn, hhat, hrstd, cg_ref[...])
        dhc_ref[...] = dhc
        dcb_ref[...] += _colsum(dhc)

    vec = pl.BlockSpec((1, aw), lambda i: (0, 0))
    vec2 = pl.BlockSpec((1, 2 * aw), lambda i: (0, 0))
    w3 = pl.BlockSpec(wm.shape, lambda i: (0, 0, 0))
    sq = pl.BlockSpec((CHUNK, 128), lambda i: (0, 0))
    return pl.pallas_call(
        body, name=name, grid=(s // ROWS,),
        in_specs=[pl.BlockSpec((ROWS, 2 * aw), lambda i: (i, 0)), pl.BlockSpec((ROWS, 2 * aw), lambda i: (i, 0)),
                  pl.BlockSpec((ROWS, aw), lambda i: (i, 0)), vec, vec, w3, w3,
                  pl.BlockSpec((CHUNK, aw), lambda i: (0, 0)), pl.BlockSpec((aw, 128), lambda i: (0, 0)), vec, vec],
        out_specs=[pl.BlockSpec((ROWS, 2 * aw), lambda i: (i, 0)), pl.BlockSpec((ROWS, aw), lambda i: (i, 0)),
                   vec2, vec, vec, w3, sq, vec, vec, vec],
        out_shape=[jax.ShapeDtypeStruct((s, 2 * aw), BF16), jax.ShapeDtypeStruct((s, aw), F32),
                   jax.ShapeDtypeStruct((1, 2 * aw), F32), jax.ShapeDtypeStruct((1, aw), F32),
                   jax.ShapeDtypeStruct((1, aw), F32), jax.ShapeDtypeStruct(wm.shape, F32),
                   jax.ShapeDtypeStruct((CHUNK, 128), F32), jax.ShapeDtypeStruct((1, aw), F32),
                   jax.ShapeDtypeStruct((1, aw), F32), jax.ShapeDtypeStruct((1, aw), F32)],
        compiler_params=_params(("arbitrary",)))(dy, z, hc, vg, vb, wm, wmt, bm, sel, cg, cbeta)


def even_conv_bwd(dhc, z, cw, *, name):
    s, zw = z.shape
    aw = zw // 4
    nblk = s // ROWS
    hb_per = ROWS // HALO

    def body(dc_ref, dn_ref, z_ref, zp_ref, cw_ref, dzb_ref, dbb_ref, dcw_ref, exth_ref, extd_ref):
        i = pl.program_id(0)

        @pl.when(i == 0)
        def _():
            dbb_ref[...] = jnp.zeros_like(dbb_ref)
            dcw_ref[...] = jnp.zeros_like(dcw_ref)

        a = z_ref[:, 0:aw]
        sg = _sigmoid(z_ref[:, aw:2 * aw])
        exth_ref[0:HALO, :] = jnp.where(i > 0, zp_ref[:, 0:aw] * _sigmoid(zp_ref[:, aw:2 * aw]), 0.0)
        exth_ref[HALO:HALO + ROWS, :] = a * sg
        dcur = dc_ref[...]
        extd_ref[0:ROWS, :] = dcur
        extd_ref[ROWS:ROWS + HALO, :] = jnp.where(i < nblk - 1, dn_ref[...], 0.0)
        dhb = jnp.zeros((ROWS, aw), F32)
        for k in range(CONV_K):
            wk = cw_ref[k:k + 1, :]
            dhb = dhb + wk * extd_ref[pl.ds(CONV_K - 1 - k, ROWS), :]
            dcw_ref[k:k + 1, :] += _colsum(dcur * exth_ref[pl.ds(HALO - (CONV_K - 1) + k, ROWS), :])
        da = dhb * sg
        dg = dhb * a * sg * (1.0 - sg)
        dzb_ref[:, 0:aw] = da.astype(BF16)
        dzb_ref[:, aw:2 * aw] = dg.astype(BF16)
        dbb_ref[:, 0:aw] += _colsum(da)
        dbb_ref[:, aw:2 * aw] += _colsum(dg)

    return pl.pallas_call(
        body, name=name, grid=(nblk,),
        in_specs=[pl.BlockSpec((ROWS, aw), lambda i: (i, 0)),
                  pl.BlockSpec((HALO, aw), lambda i: (jnp.minimum((i + 1) * hb_per, nblk * hb_per - 1), 0)),
                  pl.BlockSpec((ROWS, 2 * aw), lambda i: (i, 1)),
                  pl.BlockSpec((HALO, 2 * aw), lambda i: (jnp.maximum(i * hb_per - 1, 0), 1)),
                  pl.BlockSpec((CONV_K, aw), lambda i: (0, 0))],
        out_specs=[pl.BlockSpec((ROWS, 2 * aw), lambda i: (i, 0)), pl.BlockSpec((1, 2 * aw), lambda i: (0, 0)),
                   pl.BlockSpec((CONV_K, aw), lambda i: (0, 0))],
        out_shape=[jax.ShapeDtypeStruct((s, 2 * aw), BF16), jax.ShapeDtypeStruct((1, 2 * aw), F32),
                   jax.ShapeDtypeStruct((CONV_K, aw), F32)],
        scratch_shapes=[pltpu.VMEM((HALO + ROWS, aw), F32), pltpu.VMEM((ROWS + HALO, aw), F32)],
        compiler_params=_params(("arbitrary",)))(dhc, dhc, z, z, cw)


FFN_ROWS = 1024
FFN_CHUNK = 16


def _ffn_tile(f):
    for t in (256, 128):
        if f % t == 0:
            return t
    raise ValueError(f)


def _taps(ext_ref, w, b, r0, rows, halo):
    acc = b
    for k in range(FFN_K):
        acc = acc + w[k] * ext_ref[pl.ds(halo - (FFN_K - 1) + k + r0, rows), :]
    return acc


UP_HALO = 16
UP_SUB = 128


def ffn_up_mid(h, w_g, w_v, cw, cb, *, name):
    s, d = h.shape
    f = w_g.shape[1]
    tn = _ffn_tile(f)
    nj = f // tn
    ROWS = FFN_ROWS
    per = ROWS // UP_HALO

    def body(h_ref, hp_ref, wgm_ref, wvm_ref, wg_ref, wv_ref, bg_ref, bv_ref,
             act_ref, ug_ref, uv_ref, cg_ref, cv_ref, eg_ref, ev_ref):
        i = pl.program_id(0)
        wg = [wg_ref[k:k + 1, :] for k in range(FFN_K)]
        wv = [wv_ref[k:k + 1, :] for k in range(FFN_K)]
        bg, bv = bg_ref[...], bv_ref[...]
        sides = ((wgm_ref, eg_ref, ug_ref), (wvm_ref, ev_ref, uv_ref))
        tails = [jnp.where(i > 0, _dot(hp_ref[...], wm_ref[...], NN), 0.0) for wm_ref, _, _ in sides]

        def project(sub, tails):
            r0 = sub * UP_SUB
            src = h_ref[pl.ds(r0, UP_SUB), :]
            new_tails = []
            for (wm_ref, ext_ref, up_ref), tail in zip(sides, tails):
                u = _dot(src, wm_ref[...], NN)
                ext_ref[sub % 2, 0:UP_HALO, :] = tail
                ext_ref[sub % 2, UP_HALO:UP_HALO + UP_SUB, :] = u
                up_ref[pl.ds(r0, UP_SUB), :] = u
                new_tails.append(u[UP_SUB - UP_HALO:, :])
            return new_tails

        nsub = ROWS // UP_SUB
        tails = project(0, tails)
        for sub in range(nsub):
            if sub + 1 < nsub:
                tails = project(sub + 1, tails)
            for r0 in range(0, UP_SUB, FFN_CHUNK):
                rows = pl.ds(sub * UP_SUB + r0, FFN_CHUNK)
                gate = _taps(eg_ref.at[sub % 2], wg, bg, r0, FFN_CHUNK, UP_HALO)
                val = _taps(ev_ref.at[sub % 2], wv, bv, r0, FFN_CHUNK, UP_HALO)
                cg_ref[rows, :] = gate
                cv_ref[rows, :] = val
                act_ref[rows, :] = (gate * _sigmoid(gate) * val).astype(BF16)

    blk = pl.BlockSpec((ROWS, tn), lambda i, j: (i, j))
    wm = pl.BlockSpec((d, tn), lambda i, j: (0, j))
    wsp = lambda off: pl.BlockSpec((FFN_K, tn), lambda i, j: (0, j + off))
    bsp = lambda off: pl.BlockSpec((1, tn), lambda i, j: (0, j + off))
    return pl.pallas_call(
        body, name=name, grid=(s // ROWS, nj),
        in_specs=[pl.BlockSpec((ROWS, d), lambda i, j: (i, 0)),
                  pl.BlockSpec((UP_HALO, d), lambda i, j: (jnp.maximum(i * per - 1, 0), 0)),
                  wm, wm, wsp(0), wsp(nj), bsp(0), bsp(nj)],
        out_specs=[blk] * 5,
        out_shape=[jax.ShapeDtypeStruct((s, f), BF16)] + [jax.ShapeDtypeStruct((s, f), F32)] * 4,
        scratch_shapes=[pltpu.VMEM((2, UP_HALO + UP_SUB, tn), F32), pltpu.VMEM((2, UP_HALO + UP_SUB, tn), F32)],
        compiler_params=_params(("parallel", "parallel")))(h, h, w_g, w_v, cw, cw, cb, cb)


def ffn_mid_bwd(dact, up_g, up_v, conv_g, conv_v, cw, *, name):
    s, f = up_g.shape
    tn = _ffn_tile(f)
    nj = f // tn
    ROWS = FFN_ROWS
    nblk = s // ROWS
    per = ROWS // FHALO
    ext = ROWS + FHALO

    def body(da_ref, dan_ref, ug_ref, uv_ref, cg_ref, cv_ref, cgn_ref, cvn_ref, wg_ref, wv_ref,
             dug_ref, duv_ref, dwg_ref, dwv_ref, dbg_ref, dbv_ref, dg_ref, dv_ref):
        i = pl.program_id(1)

        @pl.when(i == 0)
        def _():
            for r in (dwg_ref, dwv_ref, dbg_ref, dbv_ref):
                r[...] = jnp.zeros_like(r)

        wg = [wg_ref[k:k + 1, :] for k in range(FFN_K)]
        wv = [wv_ref[k:k + 1, :] for k in range(FFN_K)]

        for r0, rows in [(r, FFN_CHUNK) for r in range(0, ROWS, FFN_CHUNK)] + [(ROWS, FHALO)]:
            if r0 < ROWS:
                gate, val, da = cg_ref[pl.ds(r0, rows), :], cv_ref[pl.ds(r0, rows), :], da_ref[pl.ds(r0, rows), :]
            else:
                gate, val, da = cgn_ref[...], cvn_ref[...], jnp.where(i < nblk - 1, dan_ref[...], 0.0)
            sg = _sigmoid(gate)
            dg_ref[pl.ds(r0, rows), :] = da * val * (sg * (1.0 + gate * (1.0 - sg)))
            dv_ref[pl.ds(r0, rows), :] = da * (gate * sg)

        def back(d_ref, w, u_ref, du_ref, dw_ref, db_ref):
            zero = jnp.zeros((FFN_CHUNK, tn), F32)
            acc = [zero] * FFN_K
            accb = zero
            for r0 in range(0, ROWS, FFN_CHUNK):
                d = [d_ref[pl.ds(r0 + FFN_K - 1 - k, FFN_CHUNK), :] for k in range(FFN_K)]
                u = u_ref[pl.ds(r0, FFN_CHUNK), :]
                du = w[0] * d[0]
                for k in range(1, FFN_K):
                    du = du + w[k] * d[k]
                du_ref[pl.ds(r0, FFN_CHUNK), :] = du.astype(BF16)
                acc = [acc[k] + u * d[k] for k in range(FFN_K)]
                accb = accb + d[FFN_K - 1]
            for k in range(FFN_K):
                dw_ref[k:k + 1, :] += _colsum(acc[k])
            db_ref[...] += _colsum(accb)

        back(dg_ref, wg, ug_ref, dug_ref, dwg_ref, dbg_ref)
        back(dv_ref, wv, uv_ref, duv_ref, dwv_ref, dbv_ref)

    cur = pl.BlockSpec((ROWS, tn), lambda j, i: (i, j))
    nxt = pl.BlockSpec((FHALO, tn), lambda j, i: (jnp.minimum((i + 1) * per, nblk * per - 1), j))
    wsp = lambda off: pl.BlockSpec((FFN_K, tn), lambda j, i: (0, j + off))
    bsp = pl.BlockSpec((1, tn), lambda j, i: (0, j))
    outs = pl.pallas_call(
        body, name=name, grid=(nj, nblk),
        in_specs=[cur, nxt, cur, cur, cur, cur, nxt, nxt, wsp(0), wsp(nj)],
        out_specs=[cur, cur, wsp(0), wsp(0), bsp, bsp],
        out_shape=[jax.ShapeDtypeStruct((s, f), BF16), jax.ShapeDtypeStruct((s, f), BF16),
                   jax.ShapeDtypeStruct((FFN_K, f), F32), jax.ShapeDtypeStruct((FFN_K, f), F32),
                   jax.ShapeDtypeStruct((1, f), F32), jax.ShapeDtypeStruct((1, f), F32)],
        scratch_shapes=[pltpu.VMEM((ext, tn), F32), pltpu.VMEM((ext, tn), F32)],
        compiler_params=_params(("parallel", "arbitrary")))(dact, dact, up_g, up_v, conv_g, conv_v, conv_g, conv_v,
                                                            cw, cw)
    dug, duv, dwg, dwv, dbg, dbv = outs
    return dug, duv, jnp.concatenate([dwg, dwv], axis=1), jnp.concatenate([dbg, dbv], axis=1)


def rope_tables(s):
    half = HEAD // 2
    lane = jnp.arange(128)
    j = lane % HEAD
    inv = ROPE_THETA ** (-(j % half).astype(F32) / half)
    ang = jnp.arange(s, dtype=F32)[:, None] * inv[None, :]
    sign = jnp.where(j < half, -1.0, 1.0).astype(F32)
    return jnp.cos(ang), jnp.sin(ang) * sign[None, :]


def _swap_halves(x):
    lane = lax.broadcasted_iota(jnp.int32, x.shape, 1)
    return jnp.where((lane % HEAD) < HEAD // 2, pltpu.roll(x, 128 - HEAD // 2, 1), pltpu.roll(x, HEAD // 2, 1))


def rope_fwd(qkv, cos, sin, *, name):
    s, d3 = qkv.shape
    d = d3 // 3
    scale = HEAD ** -0.5

    def body(xq_ref, xk_ref, c_ref, s_ref, q_ref, k_ref):
        c = c_ref[...]
        sn = s_ref[...]
        for t in range(d // 128):
            cs = slice(t * 128, (t + 1) * 128)
            xq = xq_ref[:, cs]
            xk = xk_ref[:, cs]
            q_ref[:, cs] = (xq * c + _swap_halves(xq) * sn) * scale
            k_ref[:, cs] = xk * c + _swap_halves(xk) * sn

    row = pl.BlockSpec((ROWS, d), lambda i: (i, 0))
    tab = pl.BlockSpec((ROWS, 128), lambda i: (i, 0))
    return pl.pallas_call(
        body, name=name, grid=(s // ROWS,),
        in_specs=[row, pl.BlockSpec((ROWS, d), lambda i: (i, 1)), tab, tab],
        out_specs=[row, row],
        out_shape=[jax.ShapeDtypeStruct((s, d), F32)] * 2,
        compiler_params=_params(("parallel",)))(qkv, qkv, cos, sin)


def rope_bwd(dq, dk, dv, cos, sin, *, name):
    s, d = dq.shape
    scale = HEAD ** -0.5

    def body(dq_ref, dk_ref, dv_ref, c_ref, s_ref, o_ref):
        c = c_ref[...]
        sn = s_ref[...]
        for t in range(d // 128):
            cs = slice(t * 128, (t + 1) * 128)
            gq = dq_ref[:, cs] * scale
            gk = dk_ref[:, cs]
            o_ref[:, t * 128:(t + 1) * 128] = (gq * c + _swap_halves(gq * sn)).astype(BF16)
            o_ref[:, d + t * 128:d + (t + 1) * 128] = (gk * c + _swap_halves(gk * sn)).astype(BF16)
        o_ref[:, 2 * d:3 * d] = dv_ref[...].astype(BF16)

    row = pl.BlockSpec((ROWS, d), lambda i: (i, 0))
    tab = pl.BlockSpec((ROWS, 128), lambda i: (i, 0))
    return pl.pallas_call(
        body, name=name, grid=(s // ROWS,),
        in_specs=[row, row, row, tab, tab],
        out_specs=pl.BlockSpec((ROWS, 3 * d), lambda i: (i, 0)),
        out_shape=jax.ShapeDtypeStruct((s, 3 * d), BF16),
        compiler_params=_params(("parallel",)))(dq, dk, dv, cos, sin)


ATT_T = BLOCK * max(DILATIONS)


FWD_GROUP = 2
ATT_GROUP = 4
FWD_QROWS = 128
BWD_QROWS = 64


def _unit_rows(r, j, dil):
    start = r + dil * BLOCK * j
    return pl.ds(start, BLOCK) if dil == 1 else pl.ds(start, BLOCK, stride=dil)


def _units():
    for bi, dil in enumerate(DILATIONS):
        nsub = ATT_T // (BLOCK * dil)
        for r in range(dil):
            for j in range(nsub):
                yield bi, dil, nsub, r, j


def _band(first_block, part, qrows):
    qi = lax.broadcasted_iota(jnp.int32, (qrows, 2 * BLOCK), 0) + part * qrows
    kj = lax.broadcasted_iota(jnp.int32, (qrows, 2 * BLOCK), 1)
    dist = BLOCK + qi - kj
    band = (dist >= 0) & (dist <= BLOCK)
    return band, band & (jnp.logical_not(first_block) | (kj >= BLOCK))


def _col(tile, h):
    return tile[:, h * HEAD:h * HEAD + 1]


def _keys(cur_ref, prev_ref, r, j, dil, nsub):
    cur = cur_ref[_unit_rows(r, j, dil), :]
    prev = cur_ref[_unit_rows(r, j - 1, dil), :] if j > 0 else prev_ref[_unit_rows(r, nsub - 1, dil), :]
    return jnp.concatenate([prev, cur], axis=0).astype(BF16)


def _att_specs(d, col_off=0):
    nt_cols = d // 128
    cur = pl.BlockSpec((ATT_T, 128), lambda n, p: (n, p + col_off))
    prv = pl.BlockSpec((ATT_T, 128), lambda n, p: (jnp.maximum(n - 1, 0), p + col_off))
    return cur, prv


def attn_fwd(q, k, qkv, *, name):
    s, d = q.shape
    nt = s // ATT_T

    def body(q_ref, kc_ref, kp_ref, vc_ref, vp_ref, o_ref, lse_ref, acc_ref, m_ref, l_ref):
        n = pl.program_id(0)
        QROWS = FWD_QROWS
        nparts = BLOCK // QROWS
        bands = [_band(n == 0, part, QROWS) for part in range(nparts)]
        lo = _pair_masks()
        keep = _head_keep(BLOCK)
        nb = len(DILATIONS)
        tile = lambda cols: jnp.where(lo, jnp.concatenate(cols[:nparts], axis=0),
                                      jnp.concatenate(cols[nparts:], axis=0))

        def scores(unit):
            bi, dil, nsub, r, j = unit
            rows = _unit_rows(r, j, dil)
            kw = _keys(kc_ref, kp_ref, r, j, dil, nsub)
            vw = _keys(vc_ref, vp_ref, r, j, dil, nsub)
            qp = q_ref[rows, :].astype(BF16)
            sc2 = _dot(jnp.concatenate([qp * keep[0], qp * keep[1]], axis=0), kw, NT)
            return dict(bi=bi, j=j, rows=rows, vw=vw, sc2=sc2, m_old=m_ref[rows, :] if bi > 0 else None)

        def softmax(u):
            prs, new_m, new_l, alpha = [], [], [], []
            for c in range(2 * nparts):
                h, part = divmod(c, nparts)
                valid = bands[part][0 if u['j'] > 0 else 1]
                sc = jnp.where(valid, u['sc2'][c * QROWS:(c + 1) * QROWS], NEG)
                mx = jnp.max(sc, axis=-1, keepdims=True)
                if u['bi'] == 0:
                    m_new = mx
                else:
                    m_old = _col(u['m_old'][part * QROWS:(part + 1) * QROWS], h)
                    m_new = jnp.maximum(m_old, mx)
                    alpha.append(jnp.exp(m_old - m_new))
                pr = jnp.exp(sc - m_new)
                new_m.append(m_new)
                new_l.append(jnp.sum(pr, axis=-1, keepdims=True))
                prs.append(pr.astype(BF16))
            u.update(pr2=jnp.concatenate(prs, axis=0), new_m=new_m, new_l=new_l, alpha=alpha)

        def combine(u):
            rows, bi = u['rows'], u['bi']
            pv2 = _dot(u['pr2'], u['vw'], NN)
            m_t = tile(u['new_m'])
            l_t = tile(u['new_l'])
            acc_t = jnp.where(lo, pv2[:BLOCK], pv2[BLOCK:])
            if bi > 0:
                a_t = tile(u['alpha'])
                l_t = a_t * l_ref[rows, :] + l_t
                acc_t = a_t * acc_ref[rows, :] + acc_t
            if bi == nb - 1:
                o_ref[rows, :] = acc_t / l_t
                lse_ref[rows, :] = m_t + jnp.log(l_t)
            else:
                acc_ref[rows, :] = acc_t
                m_ref[rows, :] = m_t
                l_ref[rows, :] = l_t

        units = list(_units())
        for first in range(0, len(units), FWD_GROUP):
            pair = [scores(u) for u in units[first:first + FWD_GROUP]]
            for u in pair:
                softmax(u)
            for u in pair:
                combine(u)

    cur, prv = _att_specs(d)
    vcur, vprv = _att_specs(d, 2 * (d // 128))
    return pl.pallas_call(
        body, name=name, grid=(nt, d // 128), in_specs=[cur, cur, prv, vcur, vprv], out_specs=[cur, cur],
        out_shape=[jax.ShapeDtypeStruct((s, d), F32)] * 2,
        scratch_shapes=[pltpu.VMEM((ATT_T, 128), F32)] * 3,
        compiler_params=_params(("parallel", "parallel")))(q, k, k, qkv, qkv)


def attn_delta(do, o, *, name):
    s, d = do.shape

    def body(do_ref, o_ref, dl_ref):
        lane = lax.broadcasted_iota(jnp.int32, (ROWS, 128), 1)
        lo = lane < HEAD
        for p in range(d // 128):
            cs = slice(p * 128, (p + 1) * 128)
            pr = do_ref[:, cs] * o_ref[:, cs]
            s0 = jnp.sum(jnp.where(lo, pr, 0.0), axis=-1, keepdims=True)
            s1 = jnp.sum(jnp.where(lo, 0.0, pr), axis=-1, keepdims=True)
            dl_ref[:, cs] = jnp.where(lo, s0, s1)

    row = pl.BlockSpec((ROWS, d), lambda i: (i, 0))
    return pl.pallas_call(
        body, name=name, grid=(s // ROWS,), in_specs=[row, row], out_specs=row,
        out_shape=jax.ShapeDtypeStruct((s, d), F32),
        compiler_params=_params(("parallel",)))(do, o)


def attn_dq(q, k, qkv, do, lse, delta, *, name):
    s, d = q.shape
    nt = s // ATT_T

    def body(q_ref, kc_ref, kp_ref, vc_ref, vp_ref, do_ref, l_ref, dl_ref, dq_ref):
        n = pl.program_id(0)
        QROWS = BWD_QROWS
        nparts = BLOCK // QROWS
        bands = [_band(n == 0, part, QROWS) for part in range(nparts)]
        lo = _pair_masks()
        keep = _head_keep(BLOCK)
        def scores(unit):
            bi, dil, nsub, r, j = unit
            rows = _unit_rows(r, j, dil)
            kw = _keys(kc_ref, kp_ref, r, j, dil, nsub)
            vw = _keys(vc_ref, vp_ref, r, j, dil, nsub)
            qp = q_ref[rows, :].astype(BF16)
            dop = do_ref[rows, :].astype(BF16)
            sc2 = _dot(jnp.concatenate([qp * keep[0], qp * keep[1]], axis=0), kw, NT)
            dp2 = _dot(jnp.concatenate([dop * keep[0], dop * keep[1]], axis=0), vw, NT)
            return dict(bi=bi, j=j, rows=rows, kw=kw, sc2=sc2, dp2=dp2, lt=l_ref[rows, :], dt=dl_ref[rows, :])

        def softmax_bwd(u):
            dss = []
            for c in range(2 * nparts):
                h, part = divmod(c, nparts)
                valid = bands[part][0 if u['j'] > 0 else 1]
                cr = slice(c * QROWS, (c + 1) * QROWS)
                pr_rows = slice(part * QROWS, (part + 1) * QROWS)
                pr = jnp.where(valid, jnp.exp(u['sc2'][cr] - _col(u['lt'][pr_rows], h)), 0.0)
                dss.append((pr * (u['dp2'][cr] - _col(u['dt'][pr_rows], h))).astype(BF16))
            u['ds2'] = jnp.concatenate(dss, axis=0)

        def combine(u):
            rows = u['rows']
            dq2 = _dot(u['ds2'], u['kw'], NN)
            dq_t = jnp.where(lo, dq2[:BLOCK], dq2[BLOCK:])
            if u['bi'] > 0:
                dq_t = dq_t + dq_ref[rows, :]
            dq_ref[rows, :] = dq_t

        units = list(_units())
        for first in range(0, len(units), ATT_GROUP):
            pair = [scores(u) for u in units[first:first + ATT_GROUP]]
            for u in pair:
                softmax_bwd(u)
            for u in pair:
                combine(u)

    cur, prv = _att_specs(d)
    vcur, vprv = _att_specs(d, 2 * (d // 128))
    return pl.pallas_call(
        body, name=name, grid=(nt, d // 128), in_specs=[cur, cur, prv, vcur, vprv, cur, cur, cur], out_specs=cur,
        out_shape=jax.ShapeDtypeStruct((s, d), F32),
        compiler_params=_params(("parallel", "parallel")))(q, k, k, qkv, qkv, do, lse, delta)


def attn_dkv(q, k, qkv, do, lse, delta, *, name):
    s, d = q.shape
    nt = s // ATT_T

    def body(k_ref, v_ref, qc_ref, qn_ref, doc_ref, don_ref, lc_ref, ln_ref, dc_ref, dn_ref, dk_ref, dv_ref):
        n = pl.program_id(0)
        qi = lax.broadcasted_iota(jnp.int32, (BLOCK, BLOCK), 0)
        kj = lax.broadcasted_iota(jnp.int32, (BLOCK, BLOCK), 1)
        own = kj <= qi
        nxt = kj >= qi
        nxt_edge = nxt & (n < nt - 1)
        keep = _head_keep(BLOCK)
        lo = _pair_masks()
        def scores(unit):
            bi, dil, nsub, r, j = unit
            rows = _unit_rows(r, j, dil)
            inner = j + 1 < nsub
            nrows = _unit_rows(r, j + 1, dil) if inner else _unit_rows(r, 0, dil)
            kp = k_ref[rows, :].astype(BF16)
            vp = v_ref[rows, :].astype(BF16)
            far = not inner
            take = lambda c_ref, n_ref, nx: ((n_ref if far else c_ref)[nrows, :] if nx else c_ref[rows, :])
            qs = [take(qc_ref, qn_ref, nx).astype(BF16) for nx in (False, True)]
            dos = [take(doc_ref, don_ref, nx).astype(BF16) for nx in (False, True)]
            lts = [take(lc_ref, ln_ref, nx) for nx in (False, True)]
            dts = [take(dc_ref, dn_ref, nx) for nx in (False, True)]
            q4 = jnp.concatenate([qs[nx] * keep[h] for h in range(2) for nx in range(2)], axis=0)
            do4 = jnp.concatenate([dos[nx] * keep[h] for h in range(2) for nx in range(2)], axis=0)
            return dict(bi=bi, rows=rows, q4=q4, do4=do4, s4=_dot(q4, kp, NT), dp4=_dot(do4, vp, NT), lts=lts,
                        dts=dts, valids=(own, nxt if inner else nxt_edge))

        def softmax_bwd(u):
            prs, dss = [], []
            for c in range(4):
                h, nx = divmod(c, 2)
                cr = slice(c * BLOCK, (c + 1) * BLOCK)
                pr = jnp.where(u['valids'][nx], jnp.exp(u['s4'][cr] - _col(u['lts'][nx], h)), 0.0)
                prs.append(pr.astype(BF16))
                dss.append((pr * (u['dp4'][cr] - _col(u['dts'][nx], h))).astype(BF16))
            u.update(pr4=jnp.concatenate(prs, axis=0), ds4=jnp.concatenate(dss, axis=0))

        def combine(u):
            rows = u['rows']
            dv_t = _dot(u['pr4'], u['do4'], TN)
            dk_t = _dot(u['ds4'], u['q4'], TN)
            if u['bi'] > 0:
                dk_t = dk_t + dk_ref[rows, :]
                dv_t = dv_t + dv_ref[rows, :]
            dk_ref[rows, :] = dk_t
            dv_ref[rows, :] = dv_t

        units = list(_units())
        for first in range(0, len(units), ATT_GROUP):
            pair = [scores(u) for u in units[first:first + ATT_GROUP]]
            for u in pair:
                softmax_bwd(u)
            for u in pair:
                combine(u)

    cur = pl.BlockSpec((ATT_T, 128), lambda n, p: (n, p))
    nxt_spec = pl.BlockSpec((ATT_T, 128), lambda n, p: (jnp.minimum(n + 1, nt - 1), p))
    vcur = pl.BlockSpec((ATT_T, 128), lambda n, p: (n, p + 2 * (d // 128)))
    return pl.pallas_call(
        body, name=name, grid=(nt, d // 128),
        in_specs=[cur, vcur, cur, nxt_spec, cur, nxt_spec, cur, nxt_spec, cur, nxt_spec], out_specs=[cur, cur],
        out_shape=[jax.ShapeDtypeStruct((s, d), F32)] * 2,
        compiler_params=_params(("parallel", "parallel")))(k, qkv, q, q, do, do, lse, lse, delta, delta)


def adamw(parts_list, w, m, v, *, name):
    nk = len(parts_list)
    npart, rk, c = parts_list[0].shape
    r = rk * nk
    assert w.shape == (r, c)
    tr = next(t for t in range(rk, 0, -8) if rk % t == 0 and (t * c * 4 <= 1024 * 1024 or t == 8))
    nbk = rk // tr

    def body(*refs):
        p_refs = refs[:nk]
        w_ref, m_ref, v_ref, g_ref, d_ref, nm_ref, nv_ref = refs[nk:]
        i = pl.program_id(0)
        g = None
        for kk, p_ref in enumerate(p_refs):
            gk = p_ref[0].astype(F32)
            for j in range(1, npart):
                gk = gk + p_ref[j].astype(F32)
            g = gk if g is None else jnp.where(i >= kk * nbk, gk, g)
        m2 = B1 * m_ref[...] + (1.0 - B1) * g
        v2 = B2 * v_ref[...] + (1.0 - B2) * (g * g)
        m_hat = m2 / (1.0 - B1 ** STEP)
        v_hat = v2 / (1.0 - B2 ** STEP)
        g_ref[...] = g
        d_ref[...] = -LR * (m_hat / (jnp.sqrt(v_hat) + ADAM_EPS) + WD * w_ref[...])
        nm_ref[...] = m2
        nv_ref[...] = v2

    blk = pl.BlockSpec((tr, c), lambda i: (i, 0))
    pspec = lambda kk: pl.BlockSpec((npart, tr, c), lambda i: (0, jnp.clip(i - kk * nbk, 0, nbk - 1), 0))
    return pl.pallas_call(
        body, name=name, grid=(r // tr,),
        in_specs=[pspec(kk) for kk in range(nk)] + [blk, blk, blk],
        out_specs=[blk] * 4, out_shape=[jax.ShapeDtypeStruct((r, c), F32)] * 4,
        compiler_params=_params(("parallel",)))(*parts_list, w, m, v)


def _my_index():
    return 4 * lax.axis_index("x") + 2 * lax.axis_index("y") + lax.axis_index("c")


def exchange(arrays, scatter, *, name):
    nt = len(arrays)

    def body(*refs):
        ins = refs[:nt]
        outs = refs[nt:2 * nt]
        send_sems, recv_sems, local_sems = refs[2 * nt:]
        x, y, c = lax.axis_index("x"), lax.axis_index("y"), lax.axis_index("c")
        me = 4 * x + 2 * y + c
        copies = []
        for t in range(nt):
            src = ins[t].at[me] if scatter[t] else ins[t]
            cp = pltpu.make_async_copy(src, outs[t].at[me], local_sems.at[t])
            cp.start()
            copies.append(cp)
        remote = []
        for kk in range(1, N_DEV):
            px, py, pc = x ^ (kk >> 2), y ^ ((kk >> 1) & 1), c ^ (kk & 1)
            peer = 4 * px + 2 * py + pc
            for t in range(nt):
                src = ins[t].at[peer] if scatter[t] else ins[t]
                cp = pltpu.make_async_remote_copy(
                    src_ref=src, dst_ref=outs[t].at[me], send_sem=send_sems.at[t, kk], recv_sem=recv_sems.at[t, kk],
                    device_id=(px, py, pc), device_id_type=pl.DeviceIdType.MESH)
                cp.start()
                remote.append(cp)
        for cp in remote:
            cp.wait()
        for cp in copies:
            cp.wait()

    hbm = pl.BlockSpec(memory_space=pl.ANY)
    out_shape = [jax.ShapeDtypeStruct(a.shape if scatter[t] else (N_DEV,) + a.shape, a.dtype)
                 for t, a in enumerate(arrays)]
    return pl.pallas_call(
        body, name=name, in_specs=[hbm] * nt, out_specs=[hbm] * nt, out_shape=out_shape,
        scratch_shapes=[pltpu.SemaphoreType.DMA((nt, N_DEV)), pltpu.SemaphoreType.DMA((nt, N_DEV)),
                        pltpu.SemaphoreType.DMA((nt,))],
        compiler_params=pltpu.CompilerParams(has_side_effects=True))(*arrays)


def _peer_of(kk):
    x, y, c = lax.axis_index("x"), lax.axis_index("y"), lax.axis_index("c")
    return x ^ (kk >> 2), y ^ ((kk >> 1) & 1), c ^ (kk & 1)


def _peer_copy(t, kk, scatter, ins, lands, send_sems, recv_sems):
    px, py, pc = _peer_of(kk)
    me = _my_index()
    src = ins[t].at[4 * px + 2 * py + pc] if scatter[t] else ins[t]
    return pltpu.make_async_remote_copy(
        src_ref=src, dst_ref=lands[t].at[me], send_sem=send_sems.at[t * N_DEV + kk],
        recv_sem=recv_sems.at[t * N_DEV + kk], device_id=(px, py, pc), device_id_type=pl.DeviceIdType.MESH)


def _own_copy(t, scatter, ins, lands, own_sems):
    me = _my_index()
    return pltpu.make_async_copy(ins[t].at[me] if scatter[t] else ins[t], lands[t].at[me], own_sems.at[t])


_HBM = pl.BlockSpec(memory_space=pltpu.HBM)
_SEM = pl.BlockSpec(memory_space=pltpu.SEMAPHORE)
_EFFECT = pltpu.SideEffectType.DATAFLOW_SIDE_EFFECTING


def exchange_start(arrays, scatter, *, name):
    nt = len(arrays)
    land_shapes = [a.shape if scatter[t] else (N_DEV,) + a.shape for t, a in enumerate(arrays)]

    def body(*refs):
        ins, lands = refs[:nt], refs[nt:2 * nt]
        send_sems, recv_sems, own_sems = refs[2 * nt:2 * nt + 3]
        token = refs[-1]
        for kk in range(1, N_DEV):
            for t in range(nt):
                _peer_copy(t, kk, scatter, ins, lands, send_sems, recv_sems).start()
        for t in range(nt):
            _own_copy(t, scatter, ins, lands, own_sems).start()
        token[...] = jnp.zeros_like(token)

    sems = pltpu.SemaphoreType.DMA((nt * N_DEV,))
    outs = pl.pallas_call(
        body, name=name,
        out_shape=(sems, sems, pltpu.SemaphoreType.DMA((nt,)), *[pltpu.HBM(a.shape, a.dtype) for a in arrays],
                   *[pltpu.HBM(shp, a.dtype) for shp, a in zip(land_shapes, arrays)],
                   jax.ShapeDtypeStruct((8, 128), F32)),
        in_specs=[_HBM] * (2 * nt),
        out_specs=(_SEM, _SEM, _SEM, *[_HBM] * (2 * nt), pl.BlockSpec(memory_space=pltpu.VMEM)),
        input_output_aliases={i: 3 + i for i in range(2 * nt)},
        compiler_params=pltpu.CompilerParams(has_side_effects=_EFFECT),
    )(*[pltpu.with_memory_space_constraint(a, pltpu.HBM) for a in arrays],
      *[pltpu.with_memory_space_constraint(lax.empty(shp, a.dtype), pltpu.HBM) for shp, a in zip(land_shapes, arrays)])
    return (outs[:3], outs[3:3 + nt], outs[3 + nt:3 + 2 * nt], scatter), outs[-1]


def exchange_wait(handle, after, *, name):
    sems, thru, lands, scatter = handle
    nt = len(thru)

    def body(*refs):
        ins, lnd = refs[:nt], refs[nt:2 * nt]
        s_sems, r_sems, o_sems = refs[2 * nt:2 * nt + 3]
        for kk in range(1, N_DEV):
            for t in range(nt):
                cp = _peer_copy(t, kk, scatter, ins, lnd, s_sems, r_sems)
                cp.wait_send()
                cp.wait_recv()
        for t in range(nt):
            _own_copy(t, scatter, ins, lnd, o_sems).wait()

    outs = pl.pallas_call(
        body, name=name,
        out_shape=(*[pltpu.HBM(a.shape, a.dtype) for a in thru], *[pltpu.HBM(a.shape, a.dtype) for a in lands]),
        in_specs=[_HBM] * (2 * nt) + [_SEM, _SEM, _SEM, pl.BlockSpec(memory_space=pl.ANY)],
        out_specs=tuple([_HBM] * (2 * nt)),
        input_output_aliases={i: i for i in range(2 * nt)},
        compiler_params=pltpu.CompilerParams(has_side_effects=_EFFECT),
    )(*thru, *lands, *sems, after)
    return outs[nt:]


def _cols_from_shards(g):
    g = jnp.moveaxis(g, 0, -2)
    return g.reshape(g.shape[:-2] + (g.shape[-2] * g.shape[-1],))


def _cols_to_shards(w, nshards=N_DEV):
    w = w.reshape(w.shape[:-1] + (nshards, w.shape[-1] // nshards))
    return jnp.moveaxis(w, -2, 0)


def _half_shards(halves):
    return jnp.concatenate([_cols_to_shards(h[None], N_DEV // 2) for h in halves], axis=0)


def _ffn_fwd(x, h, w_up, cw, cb, get_w_down, next_g, tag):
    act, up_g, up_v, conv_g, conv_v = ffn_up_mid(h, w_up[0], w_up[1], cw, cb, name=f"{tag}_up_mid")
    w_down = get_w_down(act)
    out = matmul(act, w_down, res=x, norm_g=next_g, name=f"{tag}_down")
    return out, (h, up_g, up_v, conv_g, conv_v, act), w_down


def _ffn_bwd(dx, dxb, x, saved, g, w_up, cw, w_down, tag):
    h, up_g, up_v, conv_g, conv_v, act = saved
    dact = matmul(dxb, w_down, tb=True, name=f"{tag}_ddown")
    d_w_down = matmul_ta(act, dxb, name=f"{tag}_gdown")
    dug, duv, dcw, dcb = ffn_mid_bwd(dact, up_g, up_v, conv_g, conv_v, cw, name=f"{tag}_dmid")
    d_w_up = (matmul_ta(h, dug, name=f"{tag}_gup_g"), matmul_ta(h, duv, name=f"{tag}_gup_v"))
    dh = matmul(dug, w_up[0], tb=True, name=f"{tag}_dup_g")
    dh = matmul(duv, w_up[1], tb=True, res=dh, name=f"{tag}_dup_v")
    dx2, dxb2, dg = rms_bwd(dh, x, g, dx, name=f"{tag}_dnorm")
    return dx2, dxb2, dict(norm_g=dg, w_up=d_w_up, conv_w=dcw, conv_b=dcb, w_down=d_w_down)


def local_step(x0, tgt, a, weights, grads_out):
    s, d = x0.shape
    aw = a['even_v_ln_g'].shape[-1]
    causal = jnp.tril(jnp.ones((CHUNK, CHUNK), dtype=bool))
    wm = jnp.where(causal, a['even_w_s'][0], 0.0).astype(BF16)
    wmt = jnp.swapaxes(wm, 1, 2)
    bm = jnp.repeat(a['even_b_s'][0].T, HEAD, axis=1)
    sel = (jnp.arange(aw)[:, None] // HEAD == jnp.arange(128)[None, :]).astype(BF16)
    cos, sin = rope_tables(s)
    ffn_g, ffn_cb = a['ffn_norm_g'], a['ffn_conv_b']

    w0 = weights(0, None)
    w_in, conv_w, odd_g, ffn_cw = w0['w_in'], w0['conv_w'], w0['odd_g'], w0['ffn_cw']
    h0 = rms_fwd(x0, w0['even_g'], name="even_norm")
    z = matmul(h0, w_in, bias=a['even_b_in'], name="even_in")
    ycat, hc = even_mid_fwd(z, a['even_v_ln_g'], a['even_v_ln_b'], wm, bm, conv_w, a['even_conv_b'],
                            a['even_conv_ln_g'], a['even_conv_ln_b'], name="even_mid")
    w1 = weights(1, ycat)
    x1, h1 = matmul(ycat, w1['w_out'], res=x0, norm_g=ffn_g[0:1], name="even_out")
    (x2, h2), ffn0, w_down0 = _ffn_fwd(x1, h1, w1['w_up'], ffn_cw[0], ffn_cb[0:1],
                                       lambda act: weights(2, act)['w_down'], odd_g, "ffn0")
    w2 = weights(3, h2)
    qkv = matmul(h2, w2['w_qkv'], name="odd_qkv")
    q, k = rope_fwd(qkv, cos, sin, name="rope")
    o, lse = attn_fwd(q, k, qkv, name="attn_fwd")
    x3, h3 = matmul(o, w2['w_o'], res=x2, norm_g=ffn_g[1:2], name="odd_out")
    w3 = weights(4, x3)
    x4, ffn1, _ = _ffn_fwd(x3, h3, w3['w_up'], ffn_cw[1], ffn_cb[1:2], lambda act: w3['w_down'], None, "ffn1")
    loss_t, dx, dxb, d_final_g = final_loss_bwd(x4, a['final_norm_g'].reshape(1, -1), tgt, name="final_loss")

    dx, dxb, g1 = _ffn_bwd(dx, dxb, x3, ffn1, ffn_g[1:2], w3['w_up'], ffn_cw[1], w3['w_down'], "ffn1")
    dep = grads_out(0, dict(w_up=g1['w_up'], w_down=g1['w_down']))
    do = matmul(dxb, w2['w_o'], tb=True, dep=dep, name="odd_dout")
    d_w_o = matmul_ta(o, dxb, name="odd_gout")
    delta = attn_delta(do, o, name="attn_delta")
    dq = attn_dq(q, k, qkv, do, lse, delta, name="attn_dq")
    dk, dv = attn_dkv(q, k, qkv, do, lse, delta, name="attn_dkv")
    dqkv = rope_bwd(dq, dk, dv, cos, sin, name="rope_bwd")
    d_w_qkv = matmul_ta(h2, dqkv, name="odd_gqkv")
    dep = grads_out(1, dict(w_qkv=d_w_qkv, w_o=d_w_o))
    dh2 = matmul(dqkv, w2['w_qkv'], tb=True, dep=dep, name="odd_dqkv")
    dx, dxb, d_odd_g = rms_bwd(dh2, x2, odd_g, dx, name="odd_dnorm")
    dx, dxb, g0 = _ffn_bwd(dx, dxb, x1, ffn0, ffn_g[0:1], w1['w_up'], ffn_cw[0], w_down0, "ffn0")
    dep = grads_out(2, dict(w_up=g0['w_up'], w_down=g0['w_down']))
    d_w_out = matmul_ta(ycat, dxb, dep=dep, name="even_gout")
    dep = grads_out(3, dict(w_out=d_w_out))
    dycat = matmul(dxb, w1['w_out'], tb=True, dep=dep, name="even_dout")
    (dza, dhc, dba, dvg, dvb, dwm, dbs, dcg, dcbeta, dcb) = even_mid_bwd_rows(
        dycat, z, hc, a['even_v_ln_g'], a['even_v_ln_b'], wm, wmt, bm, sel, a['even_conv_ln_g'],
        a['even_conv_ln_b'], name="even_dmid_rows")
    dzb, dbb, dcw = even_conv_bwd(dhc, z, conv_w, name="even_dmid_conv")
    nh = a['even_w_s'].shape[1]
    small_grads = {
        'even_b_in': jnp.concatenate([dba, dbb], axis=1), 'even_v_ln_g': dvg,
        'even_v_ln_b': dvb, 'even_w_s': jnp.where(causal, dwm, 0.0)[None], 'even_b_s': dbs[:, :nh].T[None],
        'even_conv_w': dcw[None], 'even_conv_b': dcb, 'even_conv_ln_g': dcg, 'even_conv_ln_b': dcbeta,
        'odd_norm_g': d_odd_g, 'ffn_norm_g': jnp.concatenate([g0['norm_g'], g1['norm_g']], axis=0),
        'ffn_conv_w': jnp.stack([g0['conv_w'], g1['conv_w']]),
        'ffn_conv_b': jnp.concatenate([g0['conv_b'], g1['conv_b']], axis=0),
        'final_norm_g': d_final_g.reshape(-1),
    }
    dep = grads_out(5, dict(small=small_grads))
    d_w_in = (matmul_ta(h0, dza, dep=dep, name="even_gin_a"), matmul_ta(h0, dzb, name="even_gin_b"))
    dep = grads_out(4, dict(w_in=d_w_in))
    dh0 = matmul(dza, w_in, tb=True, bk=0, dep=dep, name="even_din_a")
    dh0 = matmul(dzb, w_in, tb=True, bk=1, res=dh0, name="even_din_b")
    grad_x, _, d_even_g = rms_bwd(dh0, x0, w0['even_g'], dx, name="even_dnorm")
    last = {'even_norm_g': d_even_g}
    grads_out(6, dict(small=last))
    return loss_t, grad_x, {**last, **small_grads}


BIG = ['even_w_in', 'even_w_out', 'odd_w_qkv', 'odd_w_o', 'ffn_w_up', 'ffn_w_down']


def _as_tiles(flat, dtype=F32):
    return jnp.pad(flat, (0, (-flat.size) % 2048)).reshape(-1, 128).astype(dtype)


def kernel(*args):
    a = dict(zip(NAMES + ['loss_target'] + ['m_' + n for n in WEIGHTS] + ['v_' + n for n in WEIGHTS], args))
    x0 = a['x'][0]
    tgt = a['loss_target'][0]
    s, d = x0.shape
    me = _my_index()
    bf = lambda t: t.astype(BF16)

    small_local = _as_tiles(jnp.concatenate([a['even_conv_w'].reshape(-1), a['odd_norm_g'].reshape(-1),
                                             a['ffn_conv_w'].reshape(-1)]))
    stage_arrays = [
        [bf(a['even_w_in']), small_local],
        [bf(a['even_w_out']), bf(a['ffn_w_up'][0:1])],
        [bf(a['ffn_w_down'][0:1])],
        [bf(a['odd_w_qkv']), bf(a['odd_w_o'])],
        [bf(a['ffn_w_up'][1:2]), bf(a['ffn_w_down'][1:2])],
    ]
    started = [exchange_start(arrs, [False] * len(arrs), name=f"gather{i}_start") for i, arrs in enumerate(stage_arrays)]
    order = sum(tok[0, 0] for _, tok in started)

    def weights(stage, after):
        handle, tok = started[stage]
        full = exchange_wait(handle, tok if after is None else after, name=f"gather{stage}_wait")
        rows = lambda g: jnp.moveaxis(g, 0, 1).reshape(-1, d)
        halves = lambda g: (_cols_from_shards(g[:N_DEV // 2])[0], _cols_from_shards(g[N_DEV // 2:])[0])
        if stage == 0:
            gs = full[1].reshape(N_DEV, -1)
            n_cw, n_og, n_fw = a['even_conv_w'].size, a['odd_norm_g'].size, a['ffn_conv_w'].size
            return dict(
                w_in=_cols_from_shards(full[0])[0], even_g=a['even_norm_g'] + order,
                conv_w=_cols_from_shards(gs[:, :n_cw].reshape((N_DEV,) + a['even_conv_w'].shape))[0],
                odd_g=gs[:, n_cw:n_cw + n_og].reshape(1, -1),
                ffn_cw=_cols_from_shards(gs[:, n_cw + n_og:n_cw + n_og + n_fw].reshape((N_DEV,) + a['ffn_conv_w'].shape)))
        if stage == 1:
            return dict(w_out=rows(full[0]), w_up=halves(full[1]))
        if stage == 2:
            return dict(w_down=rows(full[0]))
        if stage == 3:
            return dict(w_qkv=_cols_from_shards(full[0])[0], w_o=rows(full[1]))
        return dict(w_up=halves(full[0]), w_down=rows(full[1]))

    sent = {}

    def grads_out(stage, g):
        to_rows = lambda w: w.reshape(N_DEV, 1, -1, d)
        if stage in (0, 2):
            pieces, scatter = [_half_shards(g['w_up']), to_rows(g['w_down'])], [True, True]
        elif stage == 1:
            pieces, scatter = [_cols_to_shards(g['w_qkv'][None]), to_rows(g['w_o'])], [True, True]
        elif stage == 3:
            pieces, scatter = [to_rows(g['w_out'])], [True]
        elif stage == 4:
            pieces, scatter = [_half_shards(g['w_in'])], [True]
        else:
            small = jnp.concatenate([g['small'][n].reshape(-1) for n in g['small']])
            pieces, scatter = [_as_tiles(small, BF16)], [False]
        sent[stage], tok = exchange_start(pieces, scatter, name=f"grads{stage}_start")
        return tok

    loss_t, grad_x, small_grads = local_step(x0, tgt, a, weights, grads_out)
    loss = lax.psum(loss_t[0, 0], ("x", "y", "c"))
    received = {stage: exchange_wait(handle, grad_x, name=f"grads{stage}_wait") for stage, handle in sent.items()}

    results = {}
    big_parts = {'even_w_in': [received[4][0]], 'even_w_out': [received[3][0]], 'odd_w_qkv': [received[1][0]],
                 'odd_w_o': [received[1][1]], 'ffn_w_up': [received[2][0], received[0][0]],
                 'ffn_w_down': [received[2][1], received[0][1]]}
    for n in BIG:
        shp = a[n].shape
        flat = lambda t: t.reshape(-1, shp[-1])
        outs = adamw([p.reshape(N_DEV, -1, shp[-1]) for p in big_parts[n]], flat(a[n]), flat(a['m_' + n]),
                     flat(a['v_' + n]), name=f"adamw_{n}")
        results[n] = [t.reshape(shp) for t in outs]

    small_names = list(small_grads)
    n_small = sum(small_grads[n].size for n in small_names)
    n_last = small_grads[small_names[0]].size
    rs = jnp.concatenate([received[6][0].reshape(N_DEV, -1)[:, :n_last],
                          received[5][0].reshape(N_DEV, -1)[:, :n_small - n_last]], axis=1)
    parts, offs = [], 0
    for n in small_names:
        full = small_grads[n].shape
        piece = rs[:, offs:offs + small_grads[n].size].reshape((N_DEV,) + full)
        offs += small_grads[n].size
        shp = a[n].shape
        if shp != full:
            width = shp[-1]
            piece = lax.dynamic_slice_in_dim(piece, me * width, width, axis=piece.ndim - 1)
        parts.append(piece.reshape(N_DEV, -1))
    parts = jnp.concatenate(parts, axis=1)
    pad = (-parts.shape[1]) % 2048
    cat = lambda pre: _as_tiles(jnp.concatenate([a[pre + n].reshape(-1) for n in small_names]))
    outs = adamw([jnp.pad(parts, ((0, 0), (0, pad))).reshape(N_DEV, -1, 128)], cat(''), cat('m_'), cat('v_'),
                 name="adamw_small")
    offs = 0
    for n in small_names:
        size = a[n].size
        results[n] = [t.reshape(-1)[offs:offs + size].reshape(a[n].shape) for t in outs]
        offs += size

    out = [loss, grad_x[None]]
    for i in range(4):
        out += [results[n][i] for n in WEIGHTS]
    return tuple(out)
```

```python
import functools
import math

import jax
import jax.numpy as jnp
from jax import lax
from jax.experimental import pallas as pl
from jax.experimental.pallas import tpu as pltpu

F32 = jnp.float32
BF16 = jnp.bfloat16

N_DEV = 8
EPS = 1e-6
NEG = -1e30
HEAD = 64
CHUNK = 128
BLOCK = 128
CONV_K = 31
FFN_K = 3
DILATIONS = (1, 4, 16)
ROPE_THETA = 10000.0
LR, B1, B2, ADAM_EPS, WD, STEP = 0.001, 0.9, 0.999, 1e-08, 0.01, 10

VMEM_LIMIT = 56 * 1024 * 1024
VMEM_BUDGET = 32 * 1024 * 1024
ROWS = 512
HALO = 32
FHALO = 8

NAMES = ['x', 'even_norm_g', 'even_w_in', 'even_b_in', 'even_v_ln_g', 'even_v_ln_b', 'even_w_s', 'even_b_s',
         'even_conv_w', 'even_conv_b', 'even_conv_ln_g', 'even_conv_ln_b', 'even_w_out', 'odd_norm_g',
         'odd_w_qkv', 'odd_w_o', 'ffn_norm_g', 'ffn_w_up', 'ffn_conv_w', 'ffn_conv_b', 'ffn_w_down',
         'final_norm_g']
WEIGHTS = NAMES[1:]


def _params(sem=None):
    return pltpu.CompilerParams(dimension_semantics=sem, vmem_limit_bytes=VMEM_LIMIT)


def _sigmoid(x):
    return 1.0 / (1.0 + jnp.exp(-x))


def _gelu(x):
    c = math.sqrt(2.0 / math.pi)
    return 0.5 * x * (1.0 + jnp.tanh(c * (x + 0.044715 * x * x * x)))


def _gelu_grad(x):
    c = math.sqrt(2.0 / math.pi)
    t = jnp.tanh(c * (x + 0.044715 * x * x * x))
    return 0.5 * (1.0 + t) + 0.5 * x * (1.0 - t * t) * c * (1.0 + 3.0 * 0.044715 * x * x)


def _ln_stats(x):
    mu = jnp.mean(x, axis=-1, keepdims=True)
    xc = x - mu
    rstd = lax.rsqrt(jnp.mean(xc * xc, axis=-1, keepdims=True) + EPS)
    return xc * rstd, rstd


def _ln_bwd(dy, xhat, rstd, g):
    dxh = dy * g
    return rstd * (dxh - jnp.mean(dxh, axis=-1, keepdims=True) - xhat * jnp.mean(dxh * xhat, axis=-1, keepdims=True))


def _colsum(x):
    return jnp.sum(x, axis=0, keepdims=True)


def _split3(x):
    hi = x.astype(BF16)
    r = x - hi.astype(F32)
    mid = r.astype(BF16)
    lo = (r - mid.astype(F32)).astype(BF16)
    return hi, mid, lo


def _dot(a, b, dims):
    return lax.dot_general(a, b, (dims, ((), ())), preferred_element_type=F32)


NN = ((1,), (0,))
NT = ((1,), (1,))
TN = ((0,), (0,))


def _divisors(n, cands):
    return [c for c in cands if c <= n and n % c == 0]


def _pick_tiles(m, n, k, a_bytes, b_bytes, o_bytes, extra_bytes):
    best = None
    for tm in _divisors(m, (1024, 512, 256, 128)):
        for tn in _divisors(n, (1408, 1024, 768, 704, 512, 384, 256, 128)):
            if tn % 128:
                continue
            need = 2 * (tm * k * a_bytes + k * tn * b_bytes + tm * tn * (o_bytes + extra_bytes)) + tm * tn * 4
            if need <= VMEM_BUDGET and (best is None or tm * tn > best[0] * best[1]):
                best = (tm, tn)
    assert best is not None, (m, n, k)
    return best


def matmul(a, b, *, tb=False, bk=0, bias=None, res=None, norm_g=None, dep=None, out_dtype=F32, name):
    m, k = a.shape
    n = b.shape[0] if tb else b.shape[1]
    assert b.shape[1] % k == 0 if tb else (b.shape[0] == k and bk == 0)
    if norm_g is None:
        tm, tn = _pick_tiles(m, n, k, a.dtype.itemsize, b.dtype.itemsize, jnp.dtype(out_dtype).itemsize,
                             4 if res is not None else 0)
    else:
        tm, tn = _divisors(m, (512,))[0], n

    def body(*refs):
        a_ref, b_ref = refs[:2]
        o_ref = refs[-2] if norm_g is not None else refs[-1]
        acc = _dot(a_ref[...].astype(BF16), b_ref[...].astype(BF16), NT if tb else NN)
        pos = 2
        if bias is not None:
            acc = acc + refs[pos][...]
            pos += 1
        if res is not None:
            acc = acc + refs[pos][...]
            pos += 1
        o_ref[...] = acc.astype(out_dtype)
        if norm_g is not None:
            r = lax.rsqrt(jnp.mean(acc * acc, axis=-1, keepdims=True) + EPS)
            refs[-1][...] = (acc * r * refs[pos][...]).astype(BF16)

    in_specs = [pl.BlockSpec((tm, k), lambda i, j: (i, 0)),
                pl.BlockSpec((tn, k), lambda i, j: (j, bk)) if tb else pl.BlockSpec((k, tn), lambda i, j: (0, j))]
    args = [a, b]
    if bias is not None:
        in_specs.append(pl.BlockSpec((1, tn), lambda i, j: (0, j)))
        args.append(bias)
    if res is not None:
        in_specs.append(pl.BlockSpec((tm, tn), lambda i, j: (i, j)))
        args.append(res)
    if norm_g is not None:
        in_specs.append(pl.BlockSpec((1, tn), lambda i, j: (0, j)))
        args.append(norm_g)
    if dep is not None:
        in_specs.append(pl.BlockSpec(memory_space=pl.ANY))
        args.append(dep)
    blk = pl.BlockSpec((tm, tn), lambda i, j: (i, j))
    out_shape = jax.ShapeDtypeStruct((m, n), out_dtype)
    return pl.pallas_call(
        body, name=name, grid=(m // tm, n // tn), in_specs=in_specs,
        out_specs=blk if norm_g is None else [blk, blk],
        out_shape=out_shape if norm_g is None else [out_shape, jax.ShapeDtypeStruct((m, n), BF16)],
        compiler_params=_params(("parallel", "parallel")))(*args)


def matmul_ta(a, b, *, dep=None, out_dtype=BF16, name):
    s, m = a.shape
    n = b.shape[1]
    assert b.shape[0] == s
    best = None
    for tm in _divisors(m, (512, 256, 128)):
        for tn in _divisors(n, (1024, 512, 384, 256, 128)):
            need = 2 * (s * tm * a.dtype.itemsize + s * tn * b.dtype.itemsize + tm * tn * 2) + tm * tn * 4
            if need <= VMEM_BUDGET and (best is None or tm * tn > best[0] * best[1]):
                best = (tm, tn)
    tm, tn = best

    def body(*refs):
        a_ref, b_ref, o_ref = refs[0], refs[1], refs[-1]
        o_ref[...] = _dot(a_ref[...].astype(BF16), b_ref[...].astype(BF16), TN).astype(out_dtype)

    in_specs = [pl.BlockSpec((s, tm), lambda i, j: (0, i)), pl.BlockSpec((s, tn), lambda i, j: (0, j))]
    args = [a, b]
    if dep is not None:
        in_specs.append(pl.BlockSpec(memory_space=pl.ANY))
        args.append(dep)
    return pl.pallas_call(
        body, name=name, grid=(m // tm, n // tn), in_specs=in_specs,
        out_specs=pl.BlockSpec((tm, tn), lambda i, j: (i, j)),
        out_shape=jax.ShapeDtypeStruct((m, n), out_dtype),
        compiler_params=_params(("parallel", "parallel")))(*args)


def rms_fwd(x, g, *, name):
    s, d = x.shape

    def body(x_ref, g_ref, h_ref):
        xv = x_ref[...]
        r = lax.rsqrt(jnp.mean(xv * xv, axis=-1, keepdims=True) + EPS)
        h_ref[...] = (xv * r * g_ref[...]).astype(BF16)

    return pl.pallas_call(
        body, name=name, grid=(s // ROWS,),
        in_specs=[pl.BlockSpec((ROWS, d), lambda i: (i, 0)), pl.BlockSpec((1, d), lambda i: (0, 0))],
        out_specs=pl.BlockSpec((ROWS, d), lambda i: (i, 0)),
        out_shape=jax.ShapeDtypeStruct((s, d), BF16),
        compiler_params=_params(("parallel",)))(x, g)


def rms_bwd(dh, x, g, dres, *, name):
    s, d = x.shape

    def body(dh_ref, x_ref, g_ref, dres_ref, dx_ref, dxb_ref, dg_ref):
        xv = x_ref[...]
        r = lax.rsqrt(jnp.mean(xv * xv, axis=-1, keepdims=True) + EPS)
        xhat = xv * r
        dhv = dh_ref[...]
        dxh = dhv * g_ref[...]
        dx = dres_ref[...] + r * (dxh - xhat * jnp.mean(dxh * xhat, axis=-1, keepdims=True))
        dx_ref[...] = dx
        dxb_ref[...] = dx.astype(BF16)

        @pl.when(pl.program_id(0) == 0)
        def _():
            dg_ref[...] = jnp.zeros_like(dg_ref)
        dg_ref[...] += _colsum(dhv * xhat)

    row = pl.BlockSpec((ROWS, d), lambda i: (i, 0))
    vec = pl.BlockSpec((1, d), lambda i: (0, 0))
    return pl.pallas_call(
        body, name=name, grid=(s // ROWS,),
        in_specs=[row, row, vec, row], out_specs=[row, row, vec],
        out_shape=[jax.ShapeDtypeStruct((s, d), F32), jax.ShapeDtypeStruct((s, d), BF16),
                   jax.ShapeDtypeStruct((1, d), F32)],
        compiler_params=_params(("arbitrary",)))(dh, x, g, dres)


def final_loss_bwd(x, g, tgt, *, name):
    s, d = x.shape

    def body(x_ref, g_ref, t_ref, loss_ref, dx_ref, dxb_ref, dg_ref):
        xv = x_ref[...]
        gv = g_ref[...]
        r = lax.rsqrt(jnp.mean(xv * xv, axis=-1, keepdims=True) + EPS)
        xhat = xv * r
        e = xhat * gv - t_ref[...]
        dy = e * (1.0 / d)
        dxh = dy * gv
        dx = r * (dxh - xhat * jnp.mean(dxh * xhat, axis=-1, keepdims=True))
        dx_ref[...] = dx
        dxb_ref[...] = dx.astype(BF16)

        @pl.when(pl.program_id(0) == 0)
        def _():
            dg_ref[...] = jnp.zeros_like(dg_ref)
            loss_ref[...] = jnp.zeros_like(loss_ref)
        dg_ref[...] += _colsum(dy * xhat)
        loss_ref[...] += 0.5 * jnp.sum(jnp.mean(e * e, axis=-1, keepdims=True))

    row = pl.BlockSpec((ROWS, d), lambda i: (i, 0))
    vec = pl.BlockSpec((1, d), lambda i: (0, 0))
    one = pl.BlockSpec((8, 128), lambda i: (0, 0))
    return pl.pallas_call(
        body, name=name, grid=(s // ROWS,),
        in_specs=[row, vec, row], out_specs=[one, row, row, vec],
        out_shape=[jax.ShapeDtypeStruct((8, 128), F32), jax.ShapeDtypeStruct((s, d), F32),
                   jax.ShapeDtypeStruct((s, d), BF16), jax.ShapeDtypeStruct((1, d), F32)],
        compiler_params=_params(("arbitrary",)))(x, g, tgt)


def _pair_masks():
    lane = lax.broadcasted_iota(jnp.int32, (CHUNK, 128), 1)
    return lane < HEAD


def _head_keep(rows):
    lane = lax.broadcasted_iota(jnp.int32, (rows, 128), 1)
    first = jnp.where(lane < HEAD, 1.0, 0.0)
    return first.astype(BF16), (1.0 - first).astype(BF16)


def _gating_mixed(vn_b, wm_ref, lo):
    rows, aw = vn_b.shape
    out = []
    for c in range(rows // CHUNK):
        tiles = []
        for p in range(aw // 128):
            vp = vn_b[c * CHUNK:(c + 1) * CHUNK, p * 128:(p + 1) * 128]
            r0 = _dot(wm_ref[2 * p], vp, NN)
            r1 = _dot(wm_ref[2 * p + 1], vp, NN)
            tiles.append(jnp.where(lo, r0, r1))
        out.append(jnp.concatenate(tiles, axis=1))
    return jnp.concatenate(out, axis=0)


def even_mid_fwd(z, vg, vb, wm, bm, cw, cb, cg, cbeta, *, name):
    s, zw = z.shape
    aw = zw // 4
    nblk = s // ROWS

    def body(z_ref, zp_ref, vg_ref, vb_ref, wm_ref, bm_ref, cw_ref, cb_ref, cg_ref, cbeta_ref,
             y_ref, hc_ref, ext_ref):
        i = pl.program_id(0)
        lo = _pair_masks()
        u = _gelu(z_ref[:, 0:aw])
        v = _gelu(z_ref[:, aw:2 * aw])
        vhat, _ = _ln_stats(v)
        vn = (vhat * vg_ref[...] + vb_ref[...]).astype(BF16)
        mixed = _gating_mixed(vn, wm_ref, lo)
        bias = jnp.concatenate([bm_ref[...]] * (ROWS // CHUNK), axis=0)
        y_ref[:, 0:aw] = (u * (mixed + bias)).astype(BF16)

        hb = z_ref[:, 2 * aw:3 * aw] * _sigmoid(z_ref[:, 3 * aw:4 * aw])
        hbp = zp_ref[:, 0:aw] * _sigmoid(zp_ref[:, aw:2 * aw])
        ext_ref[0:HALO, :] = jnp.where(i > 0, hbp, 0.0)
        ext_ref[HALO:HALO + ROWS, :] = hb
        acc = jnp.zeros((ROWS, aw), F32) + cb_ref[...]
        for k in range(CONV_K):
            acc = acc + cw_ref[k:k + 1, :] * ext_ref[pl.ds(HALO - (CONV_K - 1) + k, ROWS), :]
        hc_ref[...] = acc
        hhat, _ = _ln_stats(acc)
        hn = hhat * cg_ref[...] + cbeta_ref[...]
        y_ref[:, aw:2 * aw] = (hn * _sigmoid(hn)).astype(BF16)

    hb_per = ROWS // HALO
    vec = pl.BlockSpec((1, aw), lambda i: (0, 0))
    return pl.pallas_call(
        body, name=name, grid=(nblk,),
        in_specs=[pl.BlockSpec((ROWS, zw), lambda i: (i, 0)),
                  pl.BlockSpec((HALO, 2 * aw), lambda i: (jnp.maximum(i * hb_per - 1, 0), 1)),
                  vec, vec,
                  pl.BlockSpec(wm.shape, lambda i: (0, 0, 0)),
                  pl.BlockSpec((CHUNK, aw), lambda i: (0, 0)),
                  pl.BlockSpec((CONV_K, aw), lambda i: (0, 0)), vec, vec, vec],
        out_specs=[pl.BlockSpec((ROWS, 2 * aw), lambda i: (i, 0)), pl.BlockSpec((ROWS, aw), lambda i: (i, 0))],
        out_shape=[jax.ShapeDtypeStruct((s, 2 * aw), BF16), jax.ShapeDtypeStruct((s, aw), F32)],
        scratch_shapes=[pltpu.VMEM((HALO + ROWS, aw), F32)],
        compiler_params=_params(("parallel",)))(z, z, vg, vb, wm, bm, cw, cb, cg, cbeta)


def even_mid_bwd_rows(dy, z, hc, vg, vb, wm, wmt, bm, sel, cg, cbeta, *, name):
    s, zw = z.shape
    aw = zw // 4
    nh = wm.shape[0]

    def body(dy_ref, z_ref, hc_ref, vg_ref, vb_ref, wm_ref, wmt_ref, bm_ref, sel_ref, cg_ref, cbeta_ref,
             dza_ref, dhc_ref, dba_ref, dvg_ref, dvb_ref, dwm_ref, dbs_ref, dcg_ref, dcbeta_ref, dcb_ref):
        @pl.when(pl.program_id(0) == 0)
        def _():
            for r in (dba_ref, dvg_ref, dvb_ref, dwm_ref, dbs_ref, dcg_ref, dcbeta_ref, dcb_ref):
                r[...] = jnp.zeros_like(r)

        lo = _pair_masks()
        keep = _head_keep(CHUNK)
        zu = z_ref[:, 0:aw]
        zv = z_ref[:, aw:2 * aw]
        u = _gelu(zu)
        v = _gelu(zv)
        vhat, vrstd = _ln_stats(v)
        vn = (vhat * vg_ref[...] + vb_ref[...]).astype(BF16)
        mixed = _gating_mixed(vn, wm_ref, lo)
        bias = jnp.concatenate([bm_ref[...]] * (ROWS // CHUNK), axis=0)
        dya = dy_ref[:, 0:aw]
        du = dya * (mixed + bias)
        dmix = dya * u
        dmix_b = dmix.astype(BF16)

        dvn_rows = []
        for c in range(ROWS // CHUNK):
            rs = slice(c * CHUNK, (c + 1) * CHUNK)
            tiles = []
            for p in range(aw // 128):
                cs = slice(p * 128, (p + 1) * 128)
                dm = dmix_b[rs, cs]
                dm0 = dm * keep[0]
                dm1 = dm * keep[1]
                vp = vn[rs, cs]
                tiles.append(_dot(wmt_ref[2 * p], dm0, NN) + _dot(wmt_ref[2 * p + 1], dm1, NN))
                dwm_ref[2 * p] += _dot(dm0, vp, NT)
                dwm_ref[2 * p + 1] += _dot(dm1, vp, NT)
            dvn_rows.append(jnp.concatenate(tiles, axis=1))
            acc = jnp.zeros((CHUNK, 128), F32)
            for part in _split3(dmix[rs, :]):
                acc = acc + _dot(part, sel_ref[...], NN)
            dbs_ref[...] += acc
        dvn = jnp.concatenate(dvn_rows, axis=0)
        dvg_ref[...] += _colsum(dvn * vhat)
        dvb_ref[...] += _colsum(dvn)
        dv = _ln_bwd(dvn, vhat, vrstd, vg_ref[...])
        dzu = du * _gelu_grad(zu)
        dzv = dv * _gelu_grad(zv)
        dza_ref[:, 0:aw] = dzu.astype(BF16)
        dza_ref[:, aw:2 * aw] = dzv.astype(BF16)
        dba_ref[:, 0:aw] += _colsum(dzu)
        dba_ref[:, aw:2 * aw] += _colsum(dzv)

        hcv = hc_ref[...]
        hhat, hrstd = _ln_stats(hcv)
        hn = hhat * cg_ref[...] + cbeta_ref[...]
        sg = _sigmoid(hn)
        dhn = dy_ref[:, aw:2 * aw] * (sg * (1.0 + hn * (1.0 - sg)))
        dcg_ref[...] += _colsum(dhn * hhat)
        dcbeta_ref[...] += _colsum(dhn)
        dhc = _ln_bwd(dhn, hhat, hrstd, cg_ref[...])
        dhc_ref[...] = dhc
        dcb_ref[...] += _colsum(dhc)

    vec = pl.BlockSpec((1, aw), lambda i: (0, 0))
    vec2 = pl.BlockSpec((1, 2 * aw), lambda i: (0, 0))
    w3 = pl.BlockSpec(wm.shape, lambda i: (0, 0, 0))
    sq = pl.BlockSpec((CHUNK, 128), lambda i: (0, 0))
    return pl.pallas_call(
        body, name=name, grid=(s // ROWS,),
        in_specs=[pl.BlockSpec((ROWS, 2 * aw), lambda i: (i, 0)), pl.BlockSpec((ROWS, 2 * aw), lambda i: (i, 0)),
                  pl.BlockSpec((ROWS, aw), lambda i: (i, 0)), vec, vec, w3, w3,
                  pl.BlockSpec((CHUNK, aw), lambda i: (0, 0)), pl.BlockSpec((aw, 128), lambda i: (0, 0)), vec, vec],
        out_specs=[pl.BlockSpec((ROWS, 2 * aw), lambda i: (i, 0)), pl.BlockSpec((ROWS, aw), lambda i: (i, 0)),
                   vec2, vec, vec, w3, sq, vec, vec, vec],
        out_shape=[jax.ShapeDtypeStruct((s, 2 * aw), BF16), jax.ShapeDtypeStruct((s, aw), F32),
                   jax.ShapeDtypeStruct((1, 2 * aw), F32), jax.ShapeDtypeStruct((1, aw), F32),
                   jax.ShapeDtypeStruct((1, aw), F32), jax.ShapeDtypeStruct(wm.shape, F32),
                   jax.ShapeDtypeStruct((CHUNK, 128), F32), jax.ShapeDtypeStruct((1, aw), F32),
                   jax.ShapeDtypeStruct((1, aw), F32), jax.ShapeDtypeStruct((1, aw), F32)],
        compiler_params=_params(("arbitrary",)))(dy, z, hc, vg, vb, wm, wmt, bm, sel, cg, cbeta)


def even_conv_bwd(dhc, z, cw, *, name):
    s, zw = z.shape
    aw = zw // 4
    nblk = s // ROWS
    hb_per = ROWS // HALO
    rc, lc = 8, 128

    def body(dc_ref, dn_ref, z_ref, cw_ref, dzb_ref, dbb_ref, dcw_ref, extd_ref):
        i = pl.program_id(0)

        @pl.when(i == 0)
        def _():
            dbb_ref[...] = jnp.zeros_like(dbb_ref)
            dcw_ref[...] = jnp.zeros_like(dcw_ref)

        extd_ref[0:ROWS, :] = dc_ref[...]
        extd_ref[ROWS:ROWS + HALO, :] = jnp.where(i < nblk - 1, dn_ref[...], 0.0)
        for c0 in range(0, aw, lc):
            lanes = slice(c0, c0 + lc)
            acc = [jnp.zeros((rc, lc), F32)] * CONV_K
            acc_a = jnp.zeros((rc, lc), F32)
            acc_g = jnp.zeros((rc, lc), F32)
            pending = None
            for r0 in range(0, ROWS, rc):
                a = z_ref[pl.ds(r0, rc), c0:c0 + lc]
                sg = _sigmoid(z_ref[pl.ds(r0, rc), aw + c0:aw + c0 + lc])
                hb = a * sg
                dhb = jnp.zeros((rc, lc), F32)
                for k in range(CONV_K):
                    d = extd_ref[pl.ds(r0 + CONV_K - 1 - k, rc), lanes]
                    dhb = dhb + cw_ref[k:k + 1, lanes] * d
                    acc[k] = acc[k] + hb * d
                da = dhb * sg
                dg = da * a * (1.0 - sg)
                acc_a = acc_a + da
                acc_g = acc_g + dg
                if pending is None:
                    pending = (da, dg)
                else:
                    rows = pl.ds(r0 - rc, 2 * rc)
                    dzb_ref[rows, c0:c0 + lc] = jnp.concatenate([pending[0], da], axis=0).astype(BF16)
                    dzb_ref[rows, aw + c0:aw + c0 + lc] = jnp.concatenate([pending[1], dg], axis=0).astype(BF16)
                    pending = None
            for k in range(CONV_K):
                dcw_ref[k:k + 1, lanes] += _colsum(acc[k])
            dbb_ref[:, c0:c0 + lc] += _colsum(acc_a)
            dbb_ref[:, aw + c0:aw + c0 + lc] += _colsum(acc_g)

    return pl.pallas_call(
        body, name=name, grid=(nblk,),
        in_specs=[pl.BlockSpec((ROWS, aw), lambda i: (i, 0)),
                  pl.BlockSpec((HALO, aw), lambda i: (jnp.minimum((i + 1) * hb_per, nblk * hb_per - 1), 0)),
                  pl.BlockSpec((ROWS, 2 * aw), lambda i: (i, 1)),
                  pl.BlockSpec((CONV_K, aw), lambda i: (0, 0))],
        out_specs=[pl.BlockSpec((ROWS, 2 * aw), lambda i: (i, 0)), pl.BlockSpec((1, 2 * aw), lambda i: (0, 0)),
                   pl.BlockSpec((CONV_K, aw), lambda i: (0, 0))],
        out_shape=[jax.ShapeDtypeStruct((s, 2 * aw), BF16), jax.ShapeDtypeStruct((1, 2 * aw), F32),
                   jax.ShapeDtypeStruct((CONV_K, aw), F32)],
        scratch_shapes=[pltpu.VMEM((ROWS + HALO, aw), F32)],
        compiler_params=_params(("arbitrary",)))(dhc, dhc, z, cw)


FFN_ROWS = 1024
FFN_CHUNK = 16


def _ffn_tile(f):
    for t in (256, 128):
        if f % t == 0:
            return t
    raise ValueError(f)


def _taps(ext_ref, w, b, r0, rows, halo):
    acc = b
    for k in range(FFN_K):
        acc = acc + w[k] * ext_ref[pl.ds(halo - (FFN_K - 1) + k + r0, rows), :]
    return acc


UP_HALO = 16
UP_SUB = 128


def ffn_up_mid(h, w_g, w_v, cw, cb, *, name):
    s, d = h.shape
    f = w_g.shape[1]
    tn = _ffn_tile(f)
    nj = f // tn
    ROWS = FFN_ROWS
    per = ROWS // UP_HALO

    def body(h_ref, hp_ref, wgm_ref, wvm_ref, wg_ref, wv_ref, bg_ref, bv_ref,
             act_ref, ug_ref, uv_ref, cg_ref, cv_ref, eg_ref, ev_ref):
        i = pl.program_id(0)
        wg = [wg_ref[k:k + 1, :] for k in range(FFN_K)]
        wv = [wv_ref[k:k + 1, :] for k in range(FFN_K)]
        bg, bv = bg_ref[...], bv_ref[...]
        sides = ((wgm_ref, eg_ref, ug_ref), (wvm_ref, ev_ref, uv_ref))
        tails = [jnp.where(i > 0, _dot(hp_ref[...], wm_ref[...], NN), 0.0) for wm_ref, _, _ in sides]

        def project(sub, tails):
            r0 = sub * UP_SUB
            src = h_ref[pl.ds(r0, UP_SUB), :]
            new_tails = []
            for (wm_ref, ext_ref, up_ref), tail in zip(sides, tails):
                u = _dot(src, wm_ref[...], NN)
                ext_ref[sub % 2, 0:UP_HALO, :] = tail
                ext_ref[sub % 2, UP_HALO:UP_HALO + UP_SUB, :] = u
                up_ref[pl.ds(r0, UP_SUB), :] = u
                new_tails.append(u[UP_SUB - UP_HALO:, :])
            return new_tails

        nsub = ROWS // UP_SUB
        tails = project(0, tails)
        for sub in range(nsub):
            if sub + 1 < nsub:
                tails = project(sub + 1, tails)
            for r0 in range(0, UP_SUB, FFN_CHUNK):
                rows = pl.ds(sub * UP_SUB + r0, FFN_CHUNK)
                gate = _taps(eg_ref.at[sub % 2], wg, bg, r0, FFN_CHUNK, UP_HALO)
                val = _taps(ev_ref.at[sub % 2], wv, bv, r0, FFN_CHUNK, UP_HALO)
                cg_ref[rows, :] = gate
                cv_ref[rows, :] = val
                act_ref[rows, :] = (gate * _sigmoid(gate) * val).astype(BF16)

    blk = pl.BlockSpec((ROWS, tn), lambda i, j: (i, j))
    wm = pl.BlockSpec((d, tn), lambda i, j: (0, j))
    wsp = lambda off: pl.BlockSpec((FFN_K, tn), lambda i, j: (0, j + off))
    bsp = lambda off: pl.BlockSpec((1, tn), lambda i, j: (0, j + off))
    return pl.pallas_call(
        body, name=name, grid=(s // ROWS, nj),
        in_specs=[pl.BlockSpec((ROWS, d), lambda i, j: (i, 0)),
                  pl.BlockSpec((UP_HALO, d), lambda i, j: (jnp.maximum(i * per - 1, 0), 0)),
                  wm, wm, wsp(0), wsp(nj), bsp(0), bsp(nj)],
        out_specs=[blk] * 5,
        out_shape=[jax.ShapeDtypeStruct((s, f), BF16)] + [jax.ShapeDtypeStruct((s, f), F32)] * 4,
        scratch_shapes=[pltpu.VMEM((2, UP_HALO + UP_SUB, tn), F32), pltpu.VMEM((2, UP_HALO + UP_SUB, tn), F32)],
        compiler_params=_params(("parallel", "parallel")))(h, h, w_g, w_v, cw, cw, cb, cb)


def ffn_mid_bwd(dact, up_g, up_v, conv_g, conv_v, cw, *, name):
    s, f = up_g.shape
    tn = _ffn_tile(f)
    nj = f // tn
    ROWS = FFN_ROWS
    nblk = s // ROWS
    per = ROWS // FHALO
    ext = ROWS + FHALO

    def body(da_ref, dan_ref, ug_ref, uv_ref, cg_ref, cv_ref, cgn_ref, cvn_ref, wg_ref, wv_ref,
             dug_ref, duv_ref, dwg_ref, dwv_ref, dbg_ref, dbv_ref, dg_ref, dv_ref):
        i = pl.program_id(1)

        @pl.when(i == 0)
        def _():
            for r in (dwg_ref, dwv_ref, dbg_ref, dbv_ref):
                r[...] = jnp.zeros_like(r)

        wg = [wg_ref[k:k + 1, :] for k in range(FFN_K)]
        wv = [wv_ref[k:k + 1, :] for k in range(FFN_K)]

        for r0, rows in [(r, FFN_CHUNK) for r in range(0, ROWS, FFN_CHUNK)] + [(ROWS, FHALO)]:
            if r0 < ROWS:
                gate, val, da = cg_ref[pl.ds(r0, rows), :], cv_ref[pl.ds(r0, rows), :], da_ref[pl.ds(r0, rows), :]
            else:
                gate, val, da = cgn_ref[...], cvn_ref[...], jnp.where(i < nblk - 1, dan_ref[...], 0.0)
            sg = _sigmoid(gate)
            dg_ref[pl.ds(r0, rows), :] = da * val * (sg * (1.0 + gate * (1.0 - sg)))
            dv_ref[pl.ds(r0, rows), :] = da * (gate * sg)

        def back(d_ref, w, u_ref, du_ref, dw_ref, db_ref):
            zero = jnp.zeros((FFN_CHUNK, tn), F32)
            acc = [zero] * FFN_K
            accb = zero
            for r0 in range(0, ROWS, FFN_CHUNK):
                d = [d_ref[pl.ds(r0 + FFN_K - 1 - k, FFN_CHUNK), :] for k in range(FFN_K)]
                u = u_ref[pl.ds(r0, FFN_CHUNK), :]
                du = w[0] * d[0]
                for k in range(1, FFN_K):
                    du = du + w[k] * d[k]
                du_ref[pl.ds(r0, FFN_CHUNK), :] = du.astype(BF16)
                acc = [acc[k] + u * d[k] for k in range(FFN_K)]
                accb = accb + d[FFN_K - 1]
            for k in range(FFN_K):
                dw_ref[k:k + 1, :] += _colsum(acc[k])
            db_ref[...] += _colsum(accb)

        back(dg_ref, wg, ug_ref, dug_ref, dwg_ref, dbg_ref)
        back(dv_ref, wv, uv_ref, duv_ref, dwv_ref, dbv_ref)

    cur = pl.BlockSpec((ROWS, tn), lambda j, i: (i, j))
    nxt = pl.BlockSpec((FHALO, tn), lambda j, i: (jnp.minimum((i + 1) * per, nblk * per - 1), j))
    wsp = lambda off: pl.BlockSpec((FFN_K, tn), lambda j, i: (0, j + off))
    bsp = pl.BlockSpec((1, tn), lambda j, i: (0, j))
    outs = pl.pallas_call(
        body, name=name, grid=(nj, nblk),
        in_specs=[cur, nxt, cur, cur, cur, cur, nxt, nxt, wsp(0), wsp(nj)],
        out_specs=[cur, cur, wsp(0), wsp(0), bsp, bsp],
        out_shape=[jax.ShapeDtypeStruct((s, f), BF16), jax.ShapeDtypeStruct((s, f), BF16),
                   jax.ShapeDtypeStruct((FFN_K, f), F32), jax.ShapeDtypeStruct((FFN_K, f), F32),
                   jax.ShapeDtypeStruct((1, f), F32), jax.ShapeDtypeStruct((1, f), F32)],
        scratch_shapes=[pltpu.VMEM((ext, tn), F32), pltpu.VMEM((ext, tn), F32)],
        compiler_params=_params(("parallel", "arbitrary")))(dact, dact, up_g, up_v, conv_g, conv_v, conv_g, conv_v,
                                                            cw, cw)
    dug, duv, dwg, dwv, dbg, dbv = outs
    return dug, duv, jnp.concatenate([dwg, dwv], axis=1), jnp.concatenate([dbg, dbv], axis=1)


def rope_tables(s):
    half = HEAD // 2
    lane = jnp.arange(128)
    j = lane % HEAD
    inv = ROPE_THETA ** (-(j % half).astype(F32) / half)
    ang = jnp.arange(s, dtype=F32)[:, None] * inv[None, :]
    sign = jnp.where(j < half, -1.0, 1.0).astype(F32)
    return jnp.cos(ang), jnp.sin(ang) * sign[None, :]


def _swap_halves(x):
    lane = lax.broadcasted_iota(jnp.int32, x.shape, 1)
    return jnp.where((lane % HEAD) < HEAD // 2, pltpu.roll(x, 128 - HEAD // 2, 1), pltpu.roll(x, HEAD // 2, 1))


def rope_fwd(qkv, cos, sin, *, name):
    s, d3 = qkv.shape
    d = d3 // 3
    scale = HEAD ** -0.5

    def body(xq_ref, xk_ref, c_ref, s_ref, q_ref, k_ref):
        c = c_ref[...]
        sn = s_ref[...]
        for t in range(d // 128):
            cs = slice(t * 128, (t + 1) * 128)
            xq = xq_ref[:, cs]
            xk = xk_ref[:, cs]
            q_ref[:, cs] = (xq * c + _swap_halves(xq) * sn) * scale
            k_ref[:, cs] = xk * c + _swap_halves(xk) * sn

    row = pl.BlockSpec((ROWS, d), lambda i: (i, 0))
    tab = pl.BlockSpec((ROWS, 128), lambda i: (i, 0))
    return pl.pallas_call(
        body, name=name, grid=(s // ROWS,),
        in_specs=[row, pl.BlockSpec((ROWS, d), lambda i: (i, 1)), tab, tab],
        out_specs=[row, row],
        out_shape=[jax.ShapeDtypeStruct((s, d), F32)] * 2,
        compiler_params=_params(("parallel",)))(qkv, qkv, cos, sin)


def rope_bwd(dq, dk, dv, cos, sin, *, name):
    s, d = dq.shape
    scale = HEAD ** -0.5

    def body(dq_ref, dk_ref, dv_ref, c_ref, s_ref, o_ref):
        c = c_ref[...]
        sn = s_ref[...]
        for t in range(d // 128):
            cs = slice(t * 128, (t + 1) * 128)
            gq = dq_ref[:, cs] * scale
            gk = dk_ref[:, cs]
            o_ref[:, t * 128:(t + 1) * 128] = (gq * c + _swap_halves(gq * sn)).astype(BF16)
            o_ref[:, d + t * 128:d + (t + 1) * 128] = (gk * c + _swap_halves(gk * sn)).astype(BF16)
        o_ref[:, 2 * d:3 * d] = dv_ref[...].astype(BF16)

    row = pl.BlockSpec((ROWS, d), lambda i: (i, 0))
    tab = pl.BlockSpec((ROWS, 128), lambda i: (i, 0))
    return pl.pallas_call(
        body, name=name, grid=(s // ROWS,),
        in_specs=[row, row, row, tab, tab],
        out_specs=pl.BlockSpec((ROWS, 3 * d), lambda i: (i, 0)),
        out_shape=jax.ShapeDtypeStruct((s, 3 * d), BF16),
        compiler_params=_params(("parallel",)))(dq, dk, dv, cos, sin)


ATT_T = BLOCK * max(DILATIONS)


FWD_GROUP = 2
ATT_GROUP = 4
FWD_QROWS = 128
BWD_QROWS = 64


def _unit_rows(r, j, dil):
    start = r + dil * BLOCK * j
    return pl.ds(start, BLOCK) if dil == 1 else pl.ds(start, BLOCK, stride=dil)


def _units():
    for bi, dil in enumerate(DILATIONS):
        nsub = ATT_T // (BLOCK * dil)
        for r in range(dil):
            for j in range(nsub):
                yield bi, dil, nsub, r, j


def _band(first_block, part, qrows):
    qi = lax.broadcasted_iota(jnp.int32, (qrows, 2 * BLOCK), 0) + part * qrows
    kj = lax.broadcasted_iota(jnp.int32, (qrows, 2 * BLOCK), 1)
    dist = BLOCK + qi - kj
    band = (dist >= 0) & (dist <= BLOCK)
    return band, band & (jnp.logical_not(first_block) | (kj >= BLOCK))


def _col(tile, h):
    return tile[:, h * HEAD:h * HEAD + 1]


def _keys(cur_ref, prev_ref, r, j, dil, nsub):
    cur = cur_ref[_unit_rows(r, j, dil), :]
    prev = cur_ref[_unit_rows(r, j - 1, dil), :] if j > 0 else prev_ref[_unit_rows(r, nsub - 1, dil), :]
    return jnp.concatenate([prev, cur], axis=0).astype(BF16)


def _att_specs(d, col_off=0):
    nt_cols = d // 128
    cur = pl.BlockSpec((ATT_T, 128), lambda n, p: (n, p + col_off))
    prv = pl.BlockSpec((ATT_T, 128), lambda n, p: (jnp.maximum(n - 1, 0), p + col_off))
    return cur, prv


def attn_fwd(q, k, qkv, *, name):
    s, d = q.shape
    nt = s // ATT_T

    def body(q_ref, kc_ref, kp_ref, vc_ref, vp_ref, o_ref, lse_ref, acc_ref, m_ref, l_ref):
        n = pl.program_id(0)
        QROWS = FWD_QROWS
        nparts = BLOCK // QROWS
        bands = [_band(n == 0, part, QROWS) for part in range(nparts)]
        lo = _pair_masks()
        keep = _head_keep(BLOCK)
        nb = len(DILATIONS)
        tile = lambda cols: jnp.where(lo, jnp.concatenate(cols[:nparts], axis=0),
                                      jnp.concatenate(cols[nparts:], axis=0))

        def scores(unit):
            bi, dil, nsub, r, j = unit
            rows = _unit_rows(r, j, dil)
            kw = _keys(kc_ref, kp_ref, r, j, dil, nsub)
            vw = _keys(vc_ref, vp_ref, r, j, dil, nsub)
            qp = q_ref[rows, :].astype(BF16)
            sc2 = _dot(jnp.concatenate([qp * keep[0], qp * keep[1]], axis=0), kw, NT)
            return dict(bi=bi, j=j, rows=rows, vw=vw, sc2=sc2, m_old=m_ref[rows, :] if bi > 0 else None)

        def softmax(u):
            prs, new_m, new_l, alpha = [], [], [], []
            for c in range(2 * nparts):
                h, part = divmod(c, nparts)
                valid = bands[part][0 if u['j'] > 0 else 1]
                sc = jnp.where(valid, u['sc2'][c * QROWS:(c + 1) * QROWS], NEG)
                mx = jnp.max(sc, axis=-1, keepdims=True)
                if u['bi'] == 0:
                    m_new = mx
                else:
                    m_old = _col(u['m_old'][part * QROWS:(part + 1) * QROWS], h)
                    m_new = jnp.maximum(m_old, mx)
                    alpha.append(jnp.exp(m_old - m_new))
                pr = jnp.exp(sc - m_new)
                new_m.append(m_new)
                new_l.append(jnp.sum(pr, axis=-1, keepdims=True))
                prs.append(pr.astype(BF16))
            u.update(pr2=jnp.concatenate(prs, axis=0), new_m=new_m, new_l=new_l, alpha=alpha)

        def combine(u):
            rows, bi = u['rows'], u['bi']
            pv2 = _dot(u['pr2'], u['vw'], NN)
            m_t = tile(u['new_m'])
            l_t = tile(u['new_l'])
            acc_t = jnp.where(lo, pv2[:BLOCK], pv2[BLOCK:])
            if bi > 0:
                a_t = tile(u['alpha'])
                l_t = a_t * l_ref[rows, :] + l_t
                acc_t = a_t * acc_ref[rows, :] + acc_t
            if bi == nb - 1:
                o_ref[rows, :] = acc_t / l_t
                lse_ref[rows, :] = m_t + jnp.log(l_t)
            else:
                acc_ref[rows, :] = acc_t
                m_ref[rows, :] = m_t
                l_ref[rows, :] = l_t

        units = list(_units())
        for first in range(0, len(units), FWD_GROUP):
            pair = [scores(u) for u in units[first:first + FWD_GROUP]]
            for u in pair:
                softmax(u)
            for u in pair:
                combine(u)

    cur, prv = _att_specs(d)
    vcur, vprv = _att_specs(d, 2 * (d // 128))
    return pl.pallas_call(
        body, name=name, grid=(nt, d // 128), in_specs=[cur, cur, prv, vcur, vprv], out_specs=[cur, cur],
        out_shape=[jax.ShapeDtypeStruct((s, d), F32)] * 2,
        scratch_shapes=[pltpu.VMEM((ATT_T, 128), F32)] * 3,
        compiler_params=_params(("parallel", "parallel")))(q, k, k, qkv, qkv)


def attn_delta(do, o, *, name):
    s, d = do.shape

    def body(do_ref, o_ref, dl_ref):
        lane = lax.broadcasted_iota(jnp.int32, (ROWS, 128), 1)
        lo = lane < HEAD
        for p in range(d // 128):
            cs = slice(p * 128, (p + 1) * 128)
            pr = do_ref[:, cs] * o_ref[:, cs]
            s0 = jnp.sum(jnp.where(lo, pr, 0.0), axis=-1, keepdims=True)
            s1 = jnp.sum(jnp.where(lo, 0.0, pr), axis=-1, keepdims=True)
            dl_ref[:, cs] = jnp.where(lo, s0, s1)

    row = pl.BlockSpec((ROWS, d), lambda i: (i, 0))
    return pl.pallas_call(
        body, name=name, grid=(s // ROWS,), in_specs=[row, row], out_specs=row,
        out_shape=jax.ShapeDtypeStruct((s, d), F32),
        compiler_params=_params(("parallel",)))(do, o)


def attn_dq(q, k, qkv, do, lse, delta, *, name):
    s, d = q.shape
    nt = s // ATT_T

    def body(q_ref, kc_ref, kp_ref, vc_ref, vp_ref, do_ref, l_ref, dl_ref, dq_ref):
        n = pl.program_id(0)
        QROWS = BWD_QROWS
        nparts = BLOCK // QROWS
        bands = [_band(n == 0, part, QROWS) for part in range(nparts)]
        lo = _pair_masks()
        keep = _head_keep(BLOCK)
        def scores(unit):
            bi, dil, nsub, r, j = unit
            rows = _unit_rows(r, j, dil)
            kw = _keys(kc_ref, kp_ref, r, j, dil, nsub)
            vw = _keys(vc_ref, vp_ref, r, j, dil, nsub)
            qp = q_ref[rows, :].astype(BF16)
            dop = do_ref[rows, :].astype(BF16)
            sc2 = _dot(jnp.concatenate([qp * keep[0], qp * keep[1]], axis=0), kw, NT)
            dp2 = _dot(jnp.concatenate([dop * keep[0], dop * keep[1]], axis=0), vw, NT)
            return dict(bi=bi, j=j, rows=rows, kw=kw, sc2=sc2, dp2=dp2, lt=l_ref[rows, :], dt=dl_ref[rows, :])

        def softmax_bwd(u):
            dss = []
            for c in range(2 * nparts):
                h, part = divmod(c, nparts)
                valid = bands[part][0 if u['j'] > 0 else 1]
                cr = slice(c * QROWS, (c + 1) * QROWS)
                pr_rows = slice(part * QROWS, (part + 1) * QROWS)
                pr = jnp.where(valid, jnp.exp(u['sc2'][cr] - _col(u['lt'][pr_rows], h)), 0.0)
                dss.append((pr * (u['dp2'][cr] - _col(u['dt'][pr_rows], h))).astype(BF16))
            u['ds2'] = jnp.concatenate(dss, axis=0)

        def combine(u):
            rows = u['rows']
            dq2 = _dot(u['ds2'], u['kw'], NN)
            dq_t = jnp.where(lo, dq2[:BLOCK], dq2[BLOCK:])
            if u['bi'] > 0:
                dq_t = dq_t + dq_ref[rows, :]
            dq_ref[rows, :] = dq_t

        units = list(_units())
        for first in range(0, len(units), ATT_GROUP):
            pair = [scores(u) for u in units[first:first + ATT_GROUP]]
            for u in pair:
                softmax_bwd(u)
            for u in pair:
                combine(u)

    cur, prv = _att_specs(d)
    vcur, vprv = _att_specs(d, 2 * (d // 128))
    return pl.pallas_call(
        body, name=name, grid=(nt, d // 128), in_specs=[cur, cur, prv, vcur, vprv, cur, cur, cur], out_specs=cur,
        out_shape=jax.ShapeDtypeStruct((s, d), F32),
        compiler_params=_params(("parallel", "parallel")))(q, k, k, qkv, qkv, do, lse, delta)


def attn_dkv(q, k, qkv, do, lse, delta, *, name):
    s, d = q.shape
    nt = s // ATT_T

    def body(k_ref, v_ref, qc_ref, qn_ref, doc_ref, don_ref, lc_ref, ln_ref, dc_ref, dn_ref, dk_ref, dv_ref):
        n = pl.program_id(0)
        qi = lax.broadcasted_iota(jnp.int32, (BLOCK, BLOCK), 0)
        kj = lax.broadcasted_iota(jnp.int32, (BLOCK, BLOCK), 1)
        own = kj <= qi
        nxt = kj >= qi
        nxt_edge = nxt & (n < nt - 1)
        keep = _head_keep(BLOCK)
        lo = _pair_masks()
        def scores(unit):
            bi, dil, nsub, r, j = unit
            rows = _unit_rows(r, j, dil)
            inner = j + 1 < nsub
            nrows = _unit_rows(r, j + 1, dil) if inner else _unit_rows(r, 0, dil)
            kp = k_ref[rows, :].astype(BF16)
            vp = v_ref[rows, :].astype(BF16)
            far = not inner
            take = lambda c_ref, n_ref, nx: ((n_ref if far else c_ref)[nrows, :] if nx else c_ref[rows, :])
            qs = [take(qc_ref, qn_ref, nx).astype(BF16) for nx in (False, True)]
            dos = [take(doc_ref, don_ref, nx).astype(BF16) for nx in (False, True)]
            lts = [take(lc_ref, ln_ref, nx) for nx in (False, True)]
            dts = [take(dc_ref, dn_ref, nx) for nx in (False, True)]
            q4 = jnp.concatenate([qs[nx] * keep[h] for h in range(2) for nx in range(2)], axis=0)
            do4 = jnp.concatenate([dos[nx] * keep[h] for h in range(2) for nx in range(2)], axis=0)
            return dict(bi=bi, rows=rows, q4=q4, do4=do4, s4=_dot(q4, kp, NT), dp4=_dot(do4, vp, NT), lts=lts,
                        dts=dts, valids=(own, nxt if inner else nxt_edge))

        def softmax_bwd(u):
            prs, dss = [], []
            for c in range(4):
                h, nx = divmod(c, 2)
                cr = slice(c * BLOCK, (c + 1) * BLOCK)
                pr = jnp.where(u['valids'][nx], jnp.exp(u['s4'][cr] - _col(u['lts'][nx], h)), 0.0)
                prs.append(pr.astype(BF16))
                dss.append((pr * (u['dp4'][cr] - _col(u['dts'][nx], h))).astype(BF16))
            u.update(pr4=jnp.concatenate(prs, axis=0), ds4=jnp.concatenate(dss, axis=0))

        def combine(u):
            rows = u['rows']
            dv_t = _dot(u['pr4'], u['do4'], TN)
            dk_t = _dot(u['ds4'], u['q4'], TN)
            if u['bi'] > 0:
                dk_t = dk_t + dk_ref[rows, :]
                dv_t = dv_t + dv_ref[rows, :]
            dk_ref[rows, :] = dk_t
            dv_ref[rows, :] = dv_t

        units = list(_units())
        for first in range(0, len(units), ATT_GROUP):
            pair = [scores(u) for u in units[first:first + ATT_GROUP]]
            for u in pair:
                softmax_bwd(u)
            for u in pair:
                combine(u)

    cur = pl.BlockSpec((ATT_T, 128), lambda n, p: (n, p))
    nxt_spec = pl.BlockSpec((ATT_T, 128), lambda n, p: (jnp.minimum(n + 1, nt - 1), p))
    vcur = pl.BlockSpec((ATT_T, 128), lambda n, p: (n, p + 2 * (d // 128)))
    return pl.pallas_call(
        body, name=name, grid=(nt, d // 128),
        in_specs=[cur, vcur, cur, nxt_spec, cur, nxt_spec, cur, nxt_spec, cur, nxt_spec], out_specs=[cur, cur],
        out_shape=[jax.ShapeDtypeStruct((s, d), F32)] * 2,
        compiler_params=_params(("parallel", "parallel")))(k, qkv, q, q, do, do, lse, lse, delta, delta)


def adamw(parts_list, w, m, v, *, name):
    nk = len(parts_list)
    npart, rk, c = parts_list[0].shape
    r = rk * nk
    assert w.shape == (r, c)
    tr = next(t for t in range(rk, 0, -8) if rk % t == 0 and (t * c * 4 <= 1024 * 1024 or t == 8))
    nbk = rk // tr

    def body(*refs):
        p_refs = refs[:nk]
        w_ref, m_ref, v_ref, g_ref, d_ref, nm_ref, nv_ref = refs[nk:]
        i = pl.program_id(0)
        g = None
        for kk, p_ref in enumerate(p_refs):
            gk = p_ref[0].astype(F32)
            for j in range(1, npart):
                gk = gk + p_ref[j].astype(F32)
            g = gk if g is None else jnp.where(i >= kk * nbk, gk, g)
        m2 = B1 * m_ref[...] + (1.0 - B1) * g
        v2 = B2 * v_ref[...] + (1.0 - B2) * (g * g)
        m_hat = m2 / (1.0 - B1 ** STEP)
        v_hat = v2 / (1.0 - B2 ** STEP)
        g_ref[...] = g
        d_ref[...] = -LR * (m_hat / (jnp.sqrt(v_hat) + ADAM_EPS) + WD * w_ref[...])
        nm_ref[...] = m2
        nv_ref[...] = v2

    blk = pl.BlockSpec((tr, c), lambda i: (i, 0))
    pspec = lambda kk: pl.BlockSpec((npart, tr, c), lambda i: (0, jnp.clip(i - kk * nbk, 0, nbk - 1), 0))
    return pl.pallas_call(
        body, name=name, grid=(r // tr,),
        in_specs=[pspec(kk) for kk in range(nk)] + [blk, blk, blk],
        out_specs=[blk] * 4, out_shape=[jax.ShapeDtypeStruct((r, c), F32)] * 4,
        compiler_params=_params(("parallel",)))(*parts_list, w, m, v)


def _my_index():
    return 4 * lax.axis_index("x") + 2 * lax.axis_index("y") + lax.axis_index("c")


def exchange(arrays, scatter, *, name):
    nt = len(arrays)

    def body(*refs):
        ins = refs[:nt]
        outs = refs[nt:2 * nt]
        send_sems, recv_sems, local_sems = refs[2 * nt:]
        x, y, c = lax.axis_index("x"), lax.axis_index("y"), lax.axis_index("c")
        me = 4 * x + 2 * y + c
        copies = []
        for t in range(nt):
            src = ins[t].at[me] if scatter[t] else ins[t]
            cp = pltpu.make_async_copy(src, outs[t].at[me], local_sems.at[t])
            cp.start()
            copies.append(cp)
        remote = []
        for kk in range(1, N_DEV):
            px, py, pc = x ^ (kk >> 2), y ^ ((kk >> 1) & 1), c ^ (kk & 1)
            peer = 4 * px + 2 * py + pc
            for t in range(nt):
                src = ins[t].at[peer] if scatter[t] else ins[t]
                cp = pltpu.make_async_remote_copy(
                    src_ref=src, dst_ref=outs[t].at[me], send_sem=send_sems.at[t, kk], recv_sem=recv_sems.at[t, kk],
                    device_id=(px, py, pc), device_id_type=pl.DeviceIdType.MESH)
                cp.start()
                remote.append(cp)
        for cp in remote:
            cp.wait()
        for cp in copies:
            cp.wait()

    hbm = pl.BlockSpec(memory_space=pl.ANY)
    out_shape = [jax.ShapeDtypeStruct(a.shape if scatter[t] else (N_DEV,) + a.shape, a.dtype)
                 for t, a in enumerate(arrays)]
    return pl.pallas_call(
        body, name=name, in_specs=[hbm] * nt, out_specs=[hbm] * nt, out_shape=out_shape,
        scratch_shapes=[pltpu.SemaphoreType.DMA((nt, N_DEV)), pltpu.SemaphoreType.DMA((nt, N_DEV)),
                        pltpu.SemaphoreType.DMA((nt,))],
        compiler_params=pltpu.CompilerParams(has_side_effects=True))(*arrays)


def _peer_of(kk):
    x, y, c = lax.axis_index("x"), lax.axis_index("y"), lax.axis_index("c")
    return x ^ (kk >> 2), y ^ ((kk >> 1) & 1), c ^ (kk & 1)


def _peer_copy(t, kk, scatter, ins, lands, send_sems, recv_sems):
    px, py, pc = _peer_of(kk)
    me = _my_index()
    src = ins[t].at[4 * px + 2 * py + pc] if scatter[t] else ins[t]
    return pltpu.make_async_remote_copy(
        src_ref=src, dst_ref=lands[t].at[me], send_sem=send_sems.at[t * N_DEV + kk],
        recv_sem=recv_sems.at[t * N_DEV + kk], device_id=(px, py, pc), device_id_type=pl.DeviceIdType.MESH)


def _own_copy(t, scatter, ins, lands, own_sems):
    me = _my_index()
    return pltpu.make_async_copy(ins[t].at[me] if scatter[t] else ins[t], lands[t].at[me], own_sems.at[t])


_HBM = pl.BlockSpec(memory_space=pltpu.HBM)
_SEM = pl.BlockSpec(memory_space=pltpu.SEMAPHORE)
_EFFECT = pltpu.SideEffectType.DATAFLOW_SIDE_EFFECTING


def exchange_start(arrays, scatter, *, name):
    nt = len(arrays)
    land_shapes = [a.shape if scatter[t] else (N_DEV,) + a.shape for t, a in enumerate(arrays)]

    def body(*refs):
        ins, lands = refs[:nt], refs[nt:2 * nt]
        send_sems, recv_sems, own_sems = refs[2 * nt:2 * nt + 3]
        token = refs[-1]
        for kk in range(1, N_DEV):
            for t in range(nt):
                _peer_copy(t, kk, scatter, ins, lands, send_sems, recv_sems).start()
        for t in range(nt):
            _own_copy(t, scatter, ins, lands, own_sems).start()
        token[...] = jnp.zeros_like(token)

    sems = pltpu.SemaphoreType.DMA((nt * N_DEV,))
    outs = pl.pallas_call(
        body, name=name,
        out_shape=(sems, sems, pltpu.SemaphoreType.DMA((nt,)), *[pltpu.HBM(a.shape, a.dtype) for a in arrays],
                   *[pltpu.HBM(shp, a.dtype) for shp, a in zip(land_shapes, arrays)],
                   jax.ShapeDtypeStruct((8, 128), F32)),
        in_specs=[_HBM] * (2 * nt),
        out_specs=(_SEM, _SEM, _SEM, *[_HBM] * (2 * nt), pl.BlockSpec(memory_space=pltpu.VMEM)),
        input_output_aliases={i: 3 + i for i in range(2 * nt)},
        compiler_params=pltpu.CompilerParams(has_side_effects=_EFFECT),
    )(*[pltpu.with_memory_space_constraint(a, pltpu.HBM) for a in arrays],
      *[pltpu.with_memory_space_constraint(lax.empty(shp, a.dtype), pltpu.HBM) for shp, a in zip(land_shapes, arrays)])
    return (outs[:3], outs[3:3 + nt], outs[3 + nt:3 + 2 * nt], scatter), outs[-1]


def exchange_wait(handle, after, *, name):
    sems, thru, lands, scatter = handle
    nt = len(thru)

    def body(*refs):
        ins, lnd = refs[:nt], refs[nt:2 * nt]
        s_sems, r_sems, o_sems = refs[2 * nt:2 * nt + 3]
        for kk in range(1, N_DEV):
            for t in range(nt):
                cp = _peer_copy(t, kk, scatter, ins, lnd, s_sems, r_sems)
                cp.wait_send()
                cp.wait_recv()
        for t in range(nt):
            _own_copy(t, scatter, ins, lnd, o_sems).wait()

    outs = pl.pallas_call(
        body, name=name,
        out_shape=(*[pltpu.HBM(a.shape, a.dtype) for a in thru], *[pltpu.HBM(a.shape, a.dtype) for a in lands]),
        in_specs=[_HBM] * (2 * nt) + [_SEM, _SEM, _SEM, pl.BlockSpec(memory_space=pl.ANY)],
        out_specs=tuple([_HBM] * (2 * nt)),
        input_output_aliases={i: i for i in range(2 * nt)},
        compiler_params=pltpu.CompilerParams(has_side_effects=_EFFECT),
    )(*thru, *lands, *sems, after)
    return outs[nt:]


def _cols_from_shards(g):
    g = jnp.moveaxis(g, 0, -2)
    return g.reshape(g.shape[:-2] + (g.shape[-2] * g.shape[-1],))


def _cols_to_shards(w, nshards=N_DEV):
    w = w.reshape(w.shape[:-1] + (nshards, w.shape[-1] // nshards))
    return jnp.moveaxis(w, -2, 0)


def _half_shards(halves):
    return jnp.concatenate([_cols_to_shards(h[None], N_DEV // 2) for h in halves], axis=0)


def _ffn_fwd(x, h, w_up, cw, cb, get_w_down, next_g, tag):
    act, up_g, up_v, conv_g, conv_v = ffn_up_mid(h, w_up[0], w_up[1], cw, cb, name=f"{tag}_up_mid")
    w_down = get_w_down(act)
    out = matmul(act, w_down, res=x, norm_g=next_g, name=f"{tag}_down")
    return out, (h, up_g, up_v, conv_g, conv_v, act), w_down


def _ffn_bwd(dx, dxb, x, saved, g, w_up, cw, w_down, tag):
    h, up_g, up_v, conv_g, conv_v, act = saved
    dact = matmul(dxb, w_down, tb=True, name=f"{tag}_ddown")
    d_w_down = matmul_ta(act, dxb, name=f"{tag}_gdown")
    dug, duv, dcw, dcb = ffn_mid_bwd(dact, up_g, up_v, conv_g, conv_v, cw, name=f"{tag}_dmid")
    d_w_up = (matmul_ta(h, dug, name=f"{tag}_gup_g"), matmul_ta(h, duv, name=f"{tag}_gup_v"))
    dh = matmul(dug, w_up[0], tb=True, name=f"{tag}_dup_g")
    dh = matmul(duv, w_up[1], tb=True, res=dh, name=f"{tag}_dup_v")
    dx2, dxb2, dg = rms_bwd(dh, x, g, dx, name=f"{tag}_dnorm")
    return dx2, dxb2, dict(norm_g=dg, w_up=d_w_up, conv_w=dcw, conv_b=dcb, w_down=d_w_down)


def local_step(x0, tgt, a, weights, grads_out):
    s, d = x0.shape
    aw = a['even_v_ln_g'].shape[-1]
    causal = jnp.tril(jnp.ones((CHUNK, CHUNK), dtype=bool))
    wm = jnp.where(causal, a['even_w_s'][0], 0.0).astype(BF16)
    wmt = jnp.swapaxes(wm, 1, 2)
    bm = jnp.repeat(a['even_b_s'][0].T, HEAD, axis=1)
    sel = (jnp.arange(aw)[:, None] // HEAD == jnp.arange(128)[None, :]).astype(BF16)
    cos, sin = rope_tables(s)
    ffn_g, ffn_cb = a['ffn_norm_g'], a['ffn_conv_b']

    w0 = weights(0, None)
    w_in, conv_w, odd_g, ffn_cw = w0['w_in'], w0['conv_w'], w0['odd_g'], w0['ffn_cw']
    h0 = rms_fwd(x0, w0['even_g'], name="even_norm")
    z = matmul(h0, w_in, bias=a['even_b_in'], name="even_in")
    ycat, hc = even_mid_fwd(z, a['even_v_ln_g'], a['even_v_ln_b'], wm, bm, conv_w, a['even_conv_b'],
                            a['even_conv_ln_g'], a['even_conv_ln_b'], name="even_mid")
    w1 = weights(1, ycat)
    x1, h1 = matmul(ycat, w1['w_out'], res=x0, norm_g=ffn_g[0:1], name="even_out")
    (x2, h2), ffn0, w_down0 = _ffn_fwd(x1, h1, w1['w_up'], ffn_cw[0], ffn_cb[0:1],
                                       lambda act: weights(2, act)['w_down'], odd_g, "ffn0")
    w2 = weights(3, h2)
    qkv = matmul(h2, w2['w_qkv'], name="odd_qkv")
    q, k = rope_fwd(qkv, cos, sin, name="rope")
    o, lse = attn_fwd(q, k, qkv, name="attn_fwd")
    x3, h3 = matmul(o, w2['w_o'], res=x2, norm_g=ffn_g[1:2], name="odd_out")
    w3 = weights(4, x3)
    x4, ffn1, _ = _ffn_fwd(x3, h3, w3['w_up'], ffn_cw[1], ffn_cb[1:2], lambda act: w3['w_down'], None, "ffn1")
    loss_t, dx, dxb, d_final_g = final_loss_bwd(x4, a['final_norm_g'].reshape(1, -1), tgt, name="final_loss")

    dx, dxb, g1 = _ffn_bwd(dx, dxb, x3, ffn1, ffn_g[1:2], w3['w_up'], ffn_cw[1], w3['w_down'], "ffn1")
    dep = grads_out(0, dict(w_up=g1['w_up'], w_down=g1['w_down']))
    do = matmul(dxb, w2['w_o'], tb=True, dep=dep, name="odd_dout")
    d_w_o = matmul_ta(o, dxb, name="odd_gout")
    delta = attn_delta(do, o, name="attn_delta")
    dq = attn_dq(q, k, qkv, do, lse, delta, name="attn_dq")
    dk, dv = attn_dkv(q, k, qkv, do, lse, delta, name="attn_dkv")
    dqkv = rope_bwd(dq, dk, dv, cos, sin, name="rope_bwd")
    d_w_qkv = matmul_ta(h2, dqkv, name="odd_gqkv")
    dep = grads_out(1, dict(w_qkv=d_w_qkv, w_o=d_w_o))
    dh2 = matmul(dqkv, w2['w_qkv'], tb=True, dep=dep, name="odd_dqkv")
    dx, dxb, d_odd_g = rms_bwd(dh2, x2, odd_g, dx, name="odd_dnorm")
    dx, dxb, g0 = _ffn_bwd(dx, dxb, x1, ffn0, ffn_g[0:1], w1['w_up'], ffn_cw[0], w_down0, "ffn0")
    dep = grads_out(2, dict(w_up=g0['w_up'], w_down=g0['w_down']))
    d_w_out = matmul_ta(ycat, dxb, dep=dep, name="even_gout")
    dep = grads_out(3, dict(w_out=d_w_out))
    dycat = matmul(dxb, w1['w_out'], tb=True, dep=dep, name="even_dout")
    (dza, dhc, dba, dvg, dvb, dwm, dbs, dcg, dcbeta, dcb) = even_mid_bwd_rows(
        dycat, z, hc, a['even_v_ln_g'], a['even_v_ln_b'], wm, wmt, bm, sel, a['even_conv_ln_g'],
        a['even_conv_ln_b'], name="even_dmid_rows")
    dzb, dbb, dcw = even_conv_bwd(dhc, z, conv_w, name="even_dmid_conv")
    nh = a['even_w_s'].shape[1]
    small_grads = {
        'even_b_in': jnp.concatenate([dba, dbb], axis=1), 'even_v_ln_g': dvg,
        'even_v_ln_b': dvb, 'even_w_s': jnp.where(causal, dwm, 0.0)[None], 'even_b_s': dbs[:, :nh].T[None],
        'even_conv_w': dcw[None], 'even_conv_b': dcb, 'even_conv_ln_g': dcg, 'even_conv_ln_b': dcbeta,
        'odd_norm_g': d_odd_g, 'ffn_norm_g': jnp.concatenate([g0['norm_g'], g1['norm_g']], axis=0),
        'ffn_conv_w': jnp.stack([g0['conv_w'], g1['conv_w']]),
        'ffn_conv_b': jnp.concatenate([g0['conv_b'], g1['conv_b']], axis=0),
        'final_norm_g': d_final_g.reshape(-1),
    }
    dep = grads_out(5, dict(small=small_grads))
    d_w_in = (matmul_ta(h0, dza, dep=dep, name="even_gin_a"), matmul_ta(h0, dzb, name="even_gin_b"))
    dep = grads_out(4, dict(w_in=d_w_in))
    dh0 = matmul(dza, w_in, tb=True, bk=0, dep=dep, name="even_din_a")
    dh0 = matmul(dzb, w_in, tb=True, bk=1, res=dh0, name="even_din_b")
    grad_x, _, d_even_g = rms_bwd(dh0, x0, w0['even_g'], dx, name="even_dnorm")
    last = {'even_norm_g': d_even_g}
    grads_out(6, dict(small=last))
    return loss_t, grad_x, {**last, **small_grads}


BIG = ['even_w_in', 'even_w_out', 'odd_w_qkv', 'odd_w_o', 'ffn_w_up', 'ffn_w_down']


def _as_tiles(flat, dtype=F32):
    return jnp.pad(flat, (0, (-flat.size) % 2048)).reshape(-1, 128).astype(dtype)


def kernel(*args):
    a = dict(zip(NAMES + ['loss_target'] + ['m_' + n for n in WEIGHTS] + ['v_' + n for n in WEIGHTS], args))
    x0 = a['x'][0]
    tgt = a['loss_target'][0]
    s, d = x0.shape
    me = _my_index()
    bf = lambda t: t.astype(BF16)

    small_local = _as_tiles(jnp.concatenate([a['even_conv_w'].reshape(-1), a['odd_norm_g'].reshape(-1),
                                             a['ffn_conv_w'].reshape(-1)]))
    stage_arrays = [
        [bf(a['even_w_in']), small_local],
        [bf(a['even_w_out']), bf(a['ffn_w_up'][0:1])],
        [bf(a['ffn_w_down'][0:1])],
        [bf(a['odd_w_qkv']), bf(a['odd_w_o'])],
        [bf(a['ffn_w_up'][1:2]), bf(a['ffn_w_down'][1:2])],
    ]
    started = [exchange_start(arrs, [False] * len(arrs), name=f"gather{i}_start") for i, arrs in enumerate(stage_arrays)]
    order = sum(tok[0, 0] for _, tok in started)

    def weights(stage, after):
        handle, tok = started[stage]
        full = exchange_wait(handle, tok if after is None else after, name=f"gather{stage}_wait")
        rows = lambda g: jnp.moveaxis(g, 0, 1).reshape(-1, d)
        halves = lambda g: (_cols_from_shards(g[:N_DEV // 2])[0], _cols_from_shards(g[N_DEV // 2:])[0])
        if stage == 0:
            gs = full[1].reshape(N_DEV, -1)
            n_cw, n_og, n_fw = a['even_conv_w'].size, a['odd_norm_g'].size, a['ffn_conv_w'].size
            return dict(
                w_in=_cols_from_shards(full[0])[0], even_g=a['even_norm_g'] + order,
                conv_w=_cols_from_shards(gs[:, :n_cw].reshape((N_DEV,) + a['even_conv_w'].shape))[0],
                odd_g=gs[:, n_cw:n_cw + n_og].reshape(1, -1),
                ffn_cw=_cols_from_shards(gs[:, n_cw + n_og:n_cw + n_og + n_fw].reshape((N_DEV,) + a['ffn_conv_w'].shape)))
        if stage == 1:
            return dict(w_out=rows(full[0]), w_up=halves(full[1]))
        if stage == 2:
            return dict(w_down=rows(full[0]))
        if stage == 3:
            return dict(w_qkv=_cols_from_shards(full[0])[0], w_o=rows(full[1]))
        return dict(w_up=halves(full[0]), w_down=rows(full[1]))

    sent = {}

    def grads_out(stage, g):
        to_rows = lambda w: w.reshape(N_DEV, 1, -1, d)
        if stage in (0, 2):
            pieces, scatter = [_half_shards(g['w_up']), to_rows(g['w_down'])], [True, True]
        elif stage == 1:
            pieces, scatter = [_cols_to_shards(g['w_qkv'][None]), to_rows(g['w_o'])], [True, True]
        elif stage == 3:
            pieces, scatter = [to_rows(g['w_out'])], [True]
        elif stage == 4:
            pieces, scatter = [_half_shards(g['w_in'])], [True]
        else:
            small = jnp.concatenate([g['small'][n].reshape(-1) for n in g['small']])
            pieces, scatter = [_as_tiles(small, BF16)], [False]
        sent[stage], tok = exchange_start(pieces, scatter, name=f"grads{stage}_start")
        return tok

    loss_t, grad_x, small_grads = local_step(x0, tgt, a, weights, grads_out)
    loss = lax.psum(loss_t[0, 0], ("x", "y", "c"))
    received = {stage: exchange_wait(handle, grad_x, name=f"grads{stage}_wait") for stage, handle in sent.items()}

    results = {}
    big_parts = {'even_w_in': [received[4][0]], 'even_w_out': [received[3][0]], 'odd_w_qkv': [received[1][0]],
                 'odd_w_o': [received[1][1]], 'ffn_w_up': [received[2][0], received[0][0]],
                 'ffn_w_down': [received[2][1], received[0][1]]}
    for n in BIG:
        shp = a[n].shape
        flat = lambda t: t.reshape(-1, shp[-1])
        outs = adamw([p.reshape(N_DEV, -1, shp[-1]) for p in big_parts[n]], flat(a[n]), flat(a['m_' + n]),
                     flat(a['v_' + n]), name=f"adamw_{n}")
        results[n] = [t.reshape(shp) for t in outs]

    small_names = list(small_grads)
    n_small = sum(small_grads[n].size for n in small_names)
    n_last = small_grads[small_names[0]].size
    rs = jnp.concatenate([received[6][0].reshape(N_DEV, -1)[:, :n_last],
                          received[5][0].reshape(N_DEV, -1)[:, :n_small - n_last]], axis=1)
    parts, offs = [], 0
    for n in small_names:
        full = small_grads[n].shape
        piece = rs[:, offs:offs + small_grads[n].size].reshape((N_DEV,) + full)
        offs += small_grads[n].size
        shp = a[n].shape
        if shp != full:
            width = shp[-1]
            piece = lax.dynamic_slice_in_dim(piece, me * width, width, axis=piece.ndim - 1)
        parts.append(piece.reshape(N_DEV, -1))
    parts = jnp.concatenate(parts, axis=1)
    pad = (-parts.shape[1]) % 2048
    cat = lambda pre: _as_tiles(jnp.concatenate([a[pre + n].reshape(-1) for n in small_names]))
    outs = adamw([jnp.pad(parts, ((0, 0), (0, pad))).reshape(N_DEV, -1, 128)], cat(''), cat('m_'), cat('v_'),
                 name="adamw_small")
    offs = 0
    for n in small_names:
        size = a[n].size
        results[n] = [t.reshape(-1)[offs:offs + size].reshape(a[n].shape) for t in outs]
        offs += size

    out = [loss, grad_x[None]]
    for i in range(4):
        out += [results[n][i] for n in WEIGHTS]
    return tuple(out)
```

```python
import math

import jax
import jax.numpy as jnp
from jax import lax
from jax.experimental import pallas as pl
from jax.experimental.pallas import tpu as pltpu

F32 = jnp.float32
BF16 = jnp.bfloat16

N_DEV = 8
EPS = 1e-6
NEG = -1e30
HEAD = 64
CHUNK = 128
BLOCK = 128
CONV_K = 31
FFN_K = 3
DILATIONS = (1, 4, 16)
ROPE_THETA = 10000.0
LR, B1, B2, ADAM_EPS, WD, STEP = 0.001, 0.9, 0.999, 1e-08, 0.01, 10

VMEM_LIMIT = 56 * 1024 * 1024
VMEM_BUDGET = 32 * 1024 * 1024
ROWS = 512
HALO = 32
FHALO = 8

NAMES = ['x', 'even_norm_g', 'even_w_in', 'even_b_in', 'even_v_ln_g', 'even_v_ln_b', 'even_w_s', 'even_b_s',
         'even_conv_w', 'even_conv_b', 'even_conv_ln_g', 'even_conv_ln_b', 'even_w_out', 'odd_norm_g',
         'odd_w_qkv', 'odd_w_o', 'ffn_norm_g', 'ffn_w_up', 'ffn_conv_w', 'ffn_conv_b', 'ffn_w_down',
         'final_norm_g']
WEIGHTS = NAMES[1:]


def _params(sem=None):
    return pltpu.CompilerParams(dimension_semantics=sem, vmem_limit_bytes=VMEM_LIMIT)


def _sigmoid(x):
    return 1.0 / (1.0 + jnp.exp(-x))


def _gelu(x):
    c = math.sqrt(2.0 / math.pi)
    return 0.5 * x * (1.0 + jnp.tanh(c * (x + 0.044715 * x * x * x)))


def _gelu_grad(x):
    c = math.sqrt(2.0 / math.pi)
    t = jnp.tanh(c * (x + 0.044715 * x * x * x))
    return 0.5 * (1.0 + t) + 0.5 * x * (1.0 - t * t) * c * (1.0 + 3.0 * 0.044715 * x * x)


def _ln_stats(x):
    mu = jnp.mean(x, axis=-1, keepdims=True)
    xc = x - mu
    rstd = lax.rsqrt(jnp.mean(xc * xc, axis=-1, keepdims=True) + EPS)
    return xc * rstd, rstd


def _ln_bwd(dy, xhat, rstd, g):
    dxh = dy * g
    return rstd * (dxh - jnp.mean(dxh, axis=-1, keepdims=True) - xhat * jnp.mean(dxh * xhat, axis=-1, keepdims=True))


def _colsum(x):
    return jnp.sum(x, axis=0, keepdims=True)


def _split3(x):
    hi = x.astype(BF16)
    r = x - hi.astype(F32)
    mid = r.astype(BF16)
    lo = (r - mid.astype(F32)).astype(BF16)
    return hi, mid, lo


def _dot(a, b, dims):
    return lax.dot_general(a, b, (dims, ((), ())), preferred_element_type=F32)


NN = ((1,), (0,))
NT = ((1,), (1,))
TN = ((0,), (0,))


def _divisors(n, cands):
    return [c for c in cands if c <= n and n % c == 0]


def _pick_tiles(m, n, k, a_bytes, b_bytes, o_bytes, extra_bytes):
    best = None
    for tm in _divisors(m, (1024, 512, 256, 128)):
        for tn in _divisors(n, (1408, 1024, 768, 704, 512, 384, 256, 128)):
            if tn % 128:
                continue
            need = 2 * (tm * k * a_bytes + k * tn * b_bytes + tm * tn * (o_bytes + extra_bytes)) + tm * tn * 4
            if need <= VMEM_BUDGET and (best is None or tm * tn > best[0] * best[1]):
                best = (tm, tn)
    assert best is not None, (m, n, k)
    return best


def matmul(a, b, *, tb=False, bk=0, bias=None, res=None, norm_g=None, dep=None, out_dtype=F32, name):
    m, k = a.shape
    n = b.shape[0] if tb else b.shape[1]
    assert b.shape[1] % k == 0 if tb else (b.shape[0] == k and bk == 0)
    if norm_g is None:
        tm, tn = _pick_tiles(m, n, k, a.dtype.itemsize, b.dtype.itemsize, jnp.dtype(out_dtype).itemsize,
                             4 if res is not None else 0)
    else:
        tm, tn = _divisors(m, (512,))[0], n

    def body(*refs):
        a_ref, b_ref = refs[:2]
        o_ref = refs[-2] if norm_g is not None else refs[-1]
        acc = _dot(a_ref[...].astype(BF16), b_ref[...].astype(BF16), NT if tb else NN)
        pos = 2
        if bias is not None:
            acc = acc + refs[pos][...]
            pos += 1
        if res is not None:
            acc = acc + refs[pos][...]
            pos += 1
        o_ref[...] = acc.astype(out_dtype)
        if norm_g is not None:
            r = lax.rsqrt(jnp.mean(acc * acc, axis=-1, keepdims=True) + EPS)
            refs[-1][...] = (acc * r * refs[pos][...]).astype(BF16)

    in_specs = [pl.BlockSpec((tm, k), lambda i, j: (i, 0)),
                pl.BlockSpec((tn, k), lambda i, j: (j, bk)) if tb else pl.BlockSpec((k, tn), lambda i, j: (0, j))]
    args = [a, b]
    if bias is not None:
        in_specs.append(pl.BlockSpec((1, tn), lambda i, j: (0, j)))
        args.append(bias)
    if res is not None:
        in_specs.append(pl.BlockSpec((tm, tn), lambda i, j: (i, j)))
        args.append(res)
    if norm_g is not None:
        in_specs.append(pl.BlockSpec((1, tn), lambda i, j: (0, j)))
        args.append(norm_g)
    if dep is not None:
        in_specs.append(pl.BlockSpec(memory_space=pl.ANY))
        args.append(dep)
    blk = pl.BlockSpec((tm, tn), lambda i, j: (i, j))
    out_shape = jax.ShapeDtypeStruct((m, n), out_dtype)
    return pl.pallas_call(
        body, name=name, grid=(m // tm, n // tn), in_specs=in_specs,
        out_specs=blk if norm_g is None else [blk, blk],
        out_shape=out_shape if norm_g is None else [out_shape, jax.ShapeDtypeStruct((m, n), BF16)],
        compiler_params=_params(("parallel", "parallel")))(*args)


def matmul_ta(a, b, *, dep=None, out_dtype=BF16, name):
    s, m = a.shape
    n = b.shape[1]
    assert b.shape[0] == s
    best = None
    for tm in _divisors(m, (512, 256, 128)):
        for tn in _divisors(n, (1024, 512, 384, 256, 128)):
            need = 2 * (s * tm * a.dtype.itemsize + s * tn * b.dtype.itemsize + tm * tn * 2) + tm * tn * 4
            if need <= VMEM_BUDGET and (best is None or tm * tn > best[0] * best[1]):
                best = (tm, tn)
    tm, tn = best

    def body(*refs):
        a_ref, b_ref, o_ref = refs[0], refs[1], refs[-1]
        o_ref[...] = _dot(a_ref[...].astype(BF16), b_ref[...].astype(BF16), TN).astype(out_dtype)

    in_specs = [pl.BlockSpec((s, tm), lambda i, j: (0, i)), pl.BlockSpec((s, tn), lambda i, j: (0, j))]
    args = [a, b]
    if dep is not None:
        in_specs.append(pl.BlockSpec(memory_space=pl.ANY))
        args.append(dep)
    return pl.pallas_call(
        body, name=name, grid=(m // tm, n // tn), in_specs=in_specs,
        out_specs=pl.BlockSpec((tm, tn), lambda i, j: (i, j)),
        out_shape=jax.ShapeDtypeStruct((m, n), out_dtype),
        compiler_params=_params(("parallel", "parallel")))(*args)


def rms_fwd(x, g, *, name):
    s, d = x.shape

    def body(x_ref, g_ref, h_ref):
        xv = x_ref[...]
        r = lax.rsqrt(jnp.mean(xv * xv, axis=-1, keepdims=True) + EPS)
        h_ref[...] = (xv * r * g_ref[...]).astype(BF16)

    return pl.pallas_call(
        body, name=name, grid=(s // ROWS,),
        in_specs=[pl.BlockSpec((ROWS, d), lambda i: (i, 0)), pl.BlockSpec((1, d), lambda i: (0, 0))],
        out_specs=pl.BlockSpec((ROWS, d), lambda i: (i, 0)),
        out_shape=jax.ShapeDtypeStruct((s, d), BF16),
        compiler_params=_params(("parallel",)))(x, g)


def rms_bwd(dh, x, g, dres, *, name):
    s, d = x.shape

    def body(dh_ref, x_ref, g_ref, dres_ref, dx_ref, dxb_ref, dg_ref):
        xv = x_ref[...]
        r = lax.rsqrt(jnp.mean(xv * xv, axis=-1, keepdims=True) + EPS)
        xhat = xv * r
        dhv = dh_ref[...]
        dxh = dhv * g_ref[...]
        dx = dres_ref[...] + r * (dxh - xhat * jnp.mean(dxh * xhat, axis=-1, keepdims=True))
        dx_ref[...] = dx
        dxb_ref[...] = dx.astype(BF16)

        @pl.when(pl.program_id(0) == 0)
        def _():
            dg_ref[...] = jnp.zeros_like(dg_ref)
        dg_ref[...] += _colsum(dhv * xhat)

    row = pl.BlockSpec((ROWS, d), lambda i: (i, 0))
    vec = pl.BlockSpec((1, d), lambda i: (0, 0))
    return pl.pallas_call(
        body, name=name, grid=(s // ROWS,),
        in_specs=[row, row, vec, row], out_specs=[row, row, vec],
        out_shape=[jax.ShapeDtypeStruct((s, d), F32), jax.ShapeDtypeStruct((s, d), BF16),
                   jax.ShapeDtypeStruct((1, d), F32)],
        compiler_params=_params(("arbitrary",)))(dh, x, g, dres)


def final_loss_bwd(x, g, tgt, *, name):
    s, d = x.shape

    def body(x_ref, g_ref, t_ref, loss_ref, dx_ref, dxb_ref, dg_ref):
        xv = x_ref[...]
        gv = g_ref[...]
        r = lax.rsqrt(jnp.mean(xv * xv, axis=-1, keepdims=True) + EPS)
        xhat = xv * r
        e = xhat * gv - t_ref[...]
        dy = e * (1.0 / d)
        dxh = dy * gv
        dx = r * (dxh - xhat * jnp.mean(dxh * xhat, axis=-1, keepdims=True))
        dx_ref[...] = dx
        dxb_ref[...] = dx.astype(BF16)

        @pl.when(pl.program_id(0) == 0)
        def _():
            dg_ref[...] = jnp.zeros_like(dg_ref)
            loss_ref[...] = jnp.zeros_like(loss_ref)
        dg_ref[...] += _colsum(dy * xhat)
        loss_ref[...] += 0.5 * jnp.sum(jnp.mean(e * e, axis=-1, keepdims=True))

    row = pl.BlockSpec((ROWS, d), lambda i: (i, 0))
    vec = pl.BlockSpec((1, d), lambda i: (0, 0))
    one = pl.BlockSpec((8, 128), lambda i: (0, 0))
    return pl.pallas_call(
        body, name=name, grid=(s // ROWS,),
        in_specs=[row, vec, row], out_specs=[one, row, row, vec],
        out_shape=[jax.ShapeDtypeStruct((8, 128), F32), jax.ShapeDtypeStruct((s, d), F32),
                   jax.ShapeDtypeStruct((s, d), BF16), jax.ShapeDtypeStruct((1, d), F32)],
        compiler_params=_params(("arbitrary",)))(x, g, tgt)


def _pair_masks():
    lane = lax.broadcasted_iota(jnp.int32, (CHUNK, 128), 1)
    return lane < HEAD


def _head_keep(rows):
    lane = lax.broadcasted_iota(jnp.int32, (rows, 128), 1)
    first = jnp.where(lane < HEAD, 1.0, 0.0)
    return first.astype(BF16), (1.0 - first).astype(BF16)


def _gating_mixed(vn_b, wm_ref, lo):
    rows, aw = vn_b.shape
    out = []
    for c in range(rows // CHUNK):
        tiles = []
        for p in range(aw // 128):
            vp = vn_b[c * CHUNK:(c + 1) * CHUNK, p * 128:(p + 1) * 128]
            r0 = _dot(wm_ref[2 * p], vp, NN)
            r1 = _dot(wm_ref[2 * p + 1], vp, NN)
            tiles.append(jnp.where(lo, r0, r1))
        out.append(jnp.concatenate(tiles, axis=1))
    return jnp.concatenate(out, axis=0)


def even_mid_fwd(z, vg, vb, wm, bm, cw, cb, cg, cbeta, *, name):
    s, zw = z.shape
    aw = zw // 4
    nblk = s // ROWS

    def body(z_ref, zp_ref, vg_ref, vb_ref, wm_ref, bm_ref, cw_ref, cb_ref, cg_ref, cbeta_ref,
             y_ref, hc_ref, ext_ref):
        i = pl.program_id(0)
        lo = _pair_masks()
        u = _gelu(z_ref[:, 0:aw])
        v = _gelu(z_ref[:, aw:2 * aw])
        vhat, _ = _ln_stats(v)
        vn = (vhat * vg_ref[...] + vb_ref[...]).astype(BF16)
        mixed = _gating_mixed(vn, wm_ref, lo)
        bias = jnp.concatenate([bm_ref[...]] * (ROWS // CHUNK), axis=0)
        y_ref[:, 0:aw] = (u * (mixed + bias)).astype(BF16)

        hb = z_ref[:, 2 * aw:3 * aw] * _sigmoid(z_ref[:, 3 * aw:4 * aw])
        hbp = zp_ref[:, 0:aw] * _sigmoid(zp_ref[:, aw:2 * aw])
        ext_ref[0:HALO, :] = jnp.where(i > 0, hbp, 0.0)
        ext_ref[HALO:HALO + ROWS, :] = hb
        acc = jnp.zeros((ROWS, aw), F32) + cb_ref[...]
        for k in range(CONV_K):
            acc = acc + cw_ref[k:k + 1, :] * ext_ref[pl.ds(HALO - (CONV_K - 1) + k, ROWS), :]
        hc_ref[...] = acc
        hhat, _ = _ln_stats(acc)
        hn = hhat * cg_ref[...] + cbeta_ref[...]
        y_ref[:, aw:2 * aw] = (hn * _sigmoid(hn)).astype(BF16)

    hb_per = ROWS // HALO
    vec = pl.BlockSpec((1, aw), lambda i: (0, 0))
    return pl.pallas_call(
        body, name=name, grid=(nblk,),
        in_specs=[pl.BlockSpec((ROWS, zw), lambda i: (i, 0)),
                  pl.BlockSpec((HALO, 2 * aw), lambda i: (jnp.maximum(i * hb_per - 1, 0), 1)),
                  vec, vec,
                  pl.BlockSpec(wm.shape, lambda i: (0, 0, 0)),
                  pl.BlockSpec((CHUNK, aw), lambda i: (0, 0)),
                  pl.BlockSpec((CONV_K, aw), lambda i: (0, 0)), vec, vec, vec],
        out_specs=[pl.BlockSpec((ROWS, 2 * aw), lambda i: (i, 0)), pl.BlockSpec((ROWS, aw), lambda i: (i, 0))],
        out_shape=[jax.ShapeDtypeStruct((s, 2 * aw), BF16), jax.ShapeDtypeStruct((s, aw), F32)],
        scratch_shapes=[pltpu.VMEM((HALO + ROWS, aw), F32)],
        compiler_params=_params(("parallel",)))(z, z, vg, vb, wm, bm, cw, cb, cg, cbeta)


def even_mid_bwd_rows(dy, z, hc, vg, vb, wm, wmt, bm, sel, cg, cbeta, *, name):
    s, zw = z.shape
    aw = zw // 4
    nh = wm.shape[0]

    def body(dy_ref, z_ref, hc_ref, vg_ref, vb_ref, wm_ref, wmt_ref, bm_ref, sel_ref, cg_ref, cbeta_ref,
             dza_ref, dhc_ref, dba_ref, dvg_ref, dvb_ref, dwm_ref, dbs_ref, dcg_ref, dcbeta_ref, dcb_ref):
        @pl.when(pl.program_id(0) == 0)
        def _():
            for r in (dba_ref, dvg_ref, dvb_ref, dwm_ref, dbs_ref, dcg_ref, dcbeta_ref, dcb_ref):
                r[...] = jnp.zeros_like(r)

        lo = _pair_masks()
        keep = _head_keep(CHUNK)
        zu = z_ref[:, 0:aw]
        zv = z_ref[:, aw:2 * aw]
        u = _gelu(zu)
        v = _gelu(zv)
        vhat, vrstd = _ln_stats(v)
        vn = (vhat * vg_ref[...] + vb_ref[...]).astype(BF16)
        mixed = _gating_mixed(vn, wm_ref, lo)
        bias = jnp.concatenate([bm_ref[...]] * (ROWS // CHUNK), axis=0)
        dya = dy_ref[:, 0:aw]
        du = dya * (mixed + bias)
        dmix = dya * u
        dmix_b = dmix.astype(BF16)

        dvn_rows = []
        for c in range(ROWS // CHUNK):
            rs = slice(c * CHUNK, (c + 1) * CHUNK)
            tiles = []
            for p in range(aw // 128):
                cs = slice(p * 128, (p + 1) * 128)
                dm = dmix_b[rs, cs]
                dm0 = dm * keep[0]
                dm1 = dm * keep[1]
                vp = vn[rs, cs]
                tiles.append(_dot(wmt_ref[2 * p], dm0, NN) + _dot(wmt_ref[2 * p + 1], dm1, NN))
                dwm_ref[2 * p] += _dot(dm0, vp, NT)
                dwm_ref[2 * p + 1] += _dot(dm1, vp, NT)
            dvn_rows.append(jnp.concatenate(tiles, axis=1))
            acc = jnp.zeros((CHUNK, 128), F32)
            for part in _split3(dmix[rs, :]):
                acc = acc + _dot(part, sel_ref[...], NN)
            dbs_ref[...] += acc
        dvn = jnp.concatenate(dvn_rows, axis=0)
        dvg_ref[...] += _colsum(dvn * vhat)
        dvb_ref[...] += _colsum(dvn)
        dv = _ln_bwd(dvn, vhat, vrstd, vg_ref[...])
        dzu = du * _gelu_grad(zu)
        dzv = dv * _gelu_grad(zv)
        dza_ref[:, 0:aw] = dzu.astype(BF16)
        dza_ref[:, aw:2 * aw] = dzv.astype(BF16)
        dba_ref[:, 0:aw] += _colsum(dzu)
        dba_ref[:, aw:2 * aw] += _colsum(dzv)

        hcv = hc_ref[...]
        hhat, hrstd = _ln_stats(hcv)
        hn = hhat * cg_ref[...] + cbeta_ref[...]
        sg = _sigmoid(hn)
        dhn = dy_ref[:, aw:2 * aw] * (sg * (1.0 + hn * (1.0 - sg)))
        dcg_ref[...] += _colsum(dhn * hhat)
        dcbeta_ref[...] += _colsum(dhn)
        dhc = _ln_bwd(dhn, hhat, hrstd, cg_ref[...])
        dhc_ref[...] = dhc
        dcb_ref[...] += _colsum(dhc)

    vec = pl.BlockSpec((1, aw), lambda i: (0, 0))
    vec2 = pl.BlockSpec((1, 2 * aw), lambda i: (0, 0))
    w3 = pl.BlockSpec(wm.shape, lambda i: (0, 0, 0))
    sq = pl.BlockSpec((CHUNK, 128), lambda i: (0, 0))
    return pl.pallas_call(
        body, name=name, grid=(s // ROWS,),
        in_specs=[pl.BlockSpec((ROWS, 2 * aw), lambda i: (i, 0)), pl.BlockSpec((ROWS, 2 * aw), lambda i: (i, 0)),
                  pl.BlockSpec((ROWS, aw), lambda i: (i, 0)), vec, vec, w3, w3,
                  pl.BlockSpec((CHUNK, aw), lambda i: (0, 0)), pl.BlockSpec((aw, 128), lambda i: (0, 0)), vec, vec],
        out_specs=[pl.BlockSpec((ROWS, 2 * aw), lambda i: (i, 0)), pl.BlockSpec((ROWS, aw), lambda i: (i, 0)),
                   vec2, vec, vec, w3, sq, vec, vec, vec],
        out_shape=[jax.ShapeDtypeStruct((s, 2 * aw), BF16), jax.ShapeDtypeStruct((s, aw), F32),
                   jax.ShapeDtypeStruct((1, 2 * aw), F32), jax.ShapeDtypeStruct((1, aw), F32),
                   jax.ShapeDtypeStruct((1, aw), F32), jax.ShapeDtypeStruct(wm.shape, F32),
                   jax.ShapeDtypeStruct((CHUNK, 128), F32), jax.ShapeDtypeStruct((1, aw), F32),
                   jax.ShapeDtypeStruct((1, aw), F32), jax.ShapeDtypeStruct((1, aw), F32)],
        compiler_params=_params(("arbitrary",)))(dy, z, hc, vg, vb, wm, wmt, bm, sel, cg, cbeta)


def even_conv_bwd(dhc, z, cw, *, name):
    s, zw = z.shape
    aw = zw // 4
    nblk = s // ROWS
    hb_per = ROWS // HALO
    rc, lc = 8, 128

    def body(dc_ref, dn_ref, z_ref, cw_ref, dzb_ref, dbb_ref, dcw_ref, extd_ref):
        i = pl.program_id(0)

        @pl.when(i == 0)
        def _():
            dbb_ref[...] = jnp.zeros_like(dbb_ref)
            dcw_ref[...] = jnp.zeros_like(dcw_ref)

        extd_ref[0:ROWS, :] = dc_ref[...]
        extd_ref[ROWS:ROWS + HALO, :] = jnp.where(i < nblk - 1, dn_ref[...], 0.0)
        for c0 in range(0, aw, lc):
            lanes = slice(c0, c0 + lc)
            acc = [jnp.zeros((rc, lc), F32)] * CONV_K
            acc_a = jnp.zeros((rc, lc), F32)
            acc_g = jnp.zeros((rc, lc), F32)
            pending = None
            for r0 in range(0, ROWS, rc):
                a = z_ref[pl.ds(r0, rc), c0:c0 + lc]
                sg = _sigmoid(z_ref[pl.ds(r0, rc), aw + c0:aw + c0 + lc])
                hb = a * sg
                dhb = jnp.zeros((rc, lc), F32)
                for k in range(CONV_K):
                    d = extd_ref[pl.ds(r0 + CONV_K - 1 - k, rc), lanes]
                    dhb = dhb + cw_ref[k:k + 1, lanes] * d
                    acc[k] = acc[k] + hb * d
                da = dhb * sg
                dg = da * a * (1.0 - sg)
                acc_a = acc_a + da
                acc_g = acc_g + dg
                if pending is None:
                    pending = (da, dg)
                else:
                    rows = pl.ds(r0 - rc, 2 * rc)
                    dzb_ref[rows, c0:c0 + lc] = jnp.concatenate([pending[0], da], axis=0).astype(BF16)
                    dzb_ref[rows, aw + c0:aw + c0 + lc] = jnp.concatenate([pending[1], dg], axis=0).astype(BF16)
                    pending = None
            for k in range(CONV_K):
                dcw_ref[k:k + 1, lanes] += _colsum(acc[k])
            dbb_ref[:, c0:c0 + lc] += _colsum(acc_a)
            dbb_ref[:, aw + c0:aw + c0 + lc] += _colsum(acc_g)

    return pl.pallas_call(
        body, name=name, grid=(nblk,),
        in_specs=[pl.BlockSpec((ROWS, aw), lambda i: (i, 0)),
                  pl.BlockSpec((HALO, aw), lambda i: (jnp.minimum((i + 1) * hb_per, nblk * hb_per - 1), 0)),
                  pl.BlockSpec((ROWS, 2 * aw), lambda i: (i, 1)),
                  pl.BlockSpec((CONV_K, aw), lambda i: (0, 0))],
        out_specs=[pl.BlockSpec((ROWS, 2 * aw), lambda i: (i, 0)), pl.BlockSpec((1, 2 * aw), lambda i: (0, 0)),
                   pl.BlockSpec((CONV_K, aw), lambda i: (0, 0))],
        out_shape=[jax.ShapeDtypeStruct((s, 2 * aw), BF16), jax.ShapeDtypeStruct((1, 2 * aw), F32),
                   jax.ShapeDtypeStruct((CONV_K, aw), F32)],
        scratch_shapes=[pltpu.VMEM((ROWS + HALO, aw), F32)],
        compiler_params=_params(("arbitrary",)))(dhc, dhc, z, cw)


FFN_ROWS = 1024
FFN_CHUNK = 16


def _ffn_tile(f):
    for t in (256, 128):
        if f % t == 0:
            return t
    raise ValueError(f)


def _taps(ext_ref, w, b, r0, rows, halo):
    acc = b
    for k in range(FFN_K):
        acc = acc + w[k] * ext_ref[pl.ds(halo - (FFN_K - 1) + k + r0, rows), :]
    return acc


UP_HALO = 16
UP_SUB = 128


def ffn_up_mid(h, w_g, w_v, cw, cb, *, name):
    s, d = h.shape
    f = w_g.shape[1]
    tn = _ffn_tile(f)
    nj = f // tn
    ROWS = FFN_ROWS
    per = ROWS // UP_HALO

    def body(h_ref, hp_ref, wgm_ref, wvm_ref, wg_ref, wv_ref, bg_ref, bv_ref,
             act_ref, ug_ref, uv_ref, cg_ref, cv_ref, eg_ref, ev_ref):
        i = pl.program_id(0)
        wg = [wg_ref[k:k + 1, :] for k in range(FFN_K)]
        wv = [wv_ref[k:k + 1, :] for k in range(FFN_K)]
        bg, bv = bg_ref[...], bv_ref[...]
        sides = ((wgm_ref, eg_ref, ug_ref), (wvm_ref, ev_ref, uv_ref))
        tails = [jnp.where(i > 0, _dot(hp_ref[...], wm_ref[...], NN), 0.0) for wm_ref, _, _ in sides]

        def project(sub, tails):
            r0 = sub * UP_SUB
            src = h_ref[pl.ds(r0, UP_SUB), :]
            new_tails = []
            for (wm_ref, ext_ref, up_ref), tail in zip(sides, tails):
                u = _dot(src, wm_ref[...], NN)
                ext_ref[sub % 2, 0:UP_HALO, :] = tail
                ext_ref[sub % 2, UP_HALO:UP_HALO + UP_SUB, :] = u
                up_ref[pl.ds(r0, UP_SUB), :] = u
                new_tails.append(u[UP_SUB - UP_HALO:, :])
            return new_tails

        nsub = ROWS // UP_SUB
        tails = project(0, tails)
        for sub in range(nsub):
            if sub + 1 < nsub:
                tails = project(sub + 1, tails)
            for r0 in range(0, UP_SUB, FFN_CHUNK):
                rows = pl.ds(sub * UP_SUB + r0, FFN_CHUNK)
                gate = _taps(eg_ref.at[sub % 2], wg, bg, r0, FFN_CHUNK, UP_HALO)
                val = _taps(ev_ref.at[sub % 2], wv, bv, r0, FFN_CHUNK, UP_HALO)
                cg_ref[rows, :] = gate
                cv_ref[rows, :] = val
                act_ref[rows, :] = (gate * _sigmoid(gate) * val).astype(BF16)

    blk = pl.BlockSpec((ROWS, tn), lambda i, j: (i, j))
    wm = pl.BlockSpec((d, tn), lambda i, j: (0, j))
    wsp = lambda off: pl.BlockSpec((FFN_K, tn), lambda i, j: (0, j + off))
    bsp = lambda off: pl.BlockSpec((1, tn), lambda i, j: (0, j + off))
    return pl.pallas_call(
        body, name=name, grid=(s // ROWS, nj),
        in_specs=[pl.BlockSpec((ROWS, d), lambda i, j: (i, 0)),
                  pl.BlockSpec((UP_HALO, d), lambda i, j: (jnp.maximum(i * per - 1, 0), 0)),
                  wm, wm, wsp(0), wsp(nj), bsp(0), bsp(nj)],
        out_specs=[blk] * 5,
        out_shape=[jax.ShapeDtypeStruct((s, f), BF16)] + [jax.ShapeDtypeStruct((s, f), F32)] * 4,
        scratch_shapes=[pltpu.VMEM((2, UP_HALO + UP_SUB, tn), F32), pltpu.VMEM((2, UP_HALO + UP_SUB, tn), F32)],
        compiler_params=_params(("parallel", "parallel")))(h, h, w_g, w_v, cw, cw, cb, cb)


def ffn_mid_bwd(dact, up_g, up_v, conv_g, conv_v, cw, *, name):
    s, f = up_g.shape
    tn = _ffn_tile(f)
    nj = f // tn
    ROWS = FFN_ROWS
    nblk = s // ROWS
    per = ROWS // FHALO
    ext = ROWS + FHALO

    def body(da_ref, dan_ref, ug_ref, uv_ref, cg_ref, cv_ref, cgn_ref, cvn_ref, wg_ref, wv_ref,
             dug_ref, duv_ref, dwg_ref, dwv_ref, dbg_ref, dbv_ref, dg_ref, dv_ref):
        i = pl.program_id(1)

        @pl.when(i == 0)
        def _():
            for r in (dwg_ref, dwv_ref, dbg_ref, dbv_ref):
                r[...] = jnp.zeros_like(r)

        wg = [wg_ref[k:k + 1, :] for k in range(FFN_K)]
        wv = [wv_ref[k:k + 1, :] for k in range(FFN_K)]

        for r0, rows in [(r, FFN_CHUNK) for r in range(0, ROWS, FFN_CHUNK)] + [(ROWS, FHALO)]:
            if r0 < ROWS:
                gate, val, da = cg_ref[pl.ds(r0, rows), :], cv_ref[pl.ds(r0, rows), :], da_ref[pl.ds(r0, rows), :]
            else:
                gate, val, da = cgn_ref[...], cvn_ref[...], jnp.where(i < nblk - 1, dan_ref[...], 0.0)
            sg = _sigmoid(gate)
            dg_ref[pl.ds(r0, rows), :] = da * val * (sg * (1.0 + gate * (1.0 - sg)))
            dv_ref[pl.ds(r0, rows), :] = da * (gate * sg)

        def back(d_ref, w, u_ref, du_ref, dw_ref, db_ref):
            zero = jnp.zeros((FFN_CHUNK, tn), F32)
            acc = [zero] * FFN_K
            accb = zero
            for r0 in range(0, ROWS, FFN_CHUNK):
                d = [d_ref[pl.ds(r0 + FFN_K - 1 - k, FFN_CHUNK), :] for k in range(FFN_K)]
                u = u_ref[pl.ds(r0, FFN_CHUNK), :]
                du = w[0] * d[0]
                for k in range(1, FFN_K):
                    du = du + w[k] * d[k]
                du_ref[pl.ds(r0, FFN_CHUNK), :] = du.astype(BF16)
                acc = [acc[k] + u * d[k] for k in range(FFN_K)]
                accb = accb + d[FFN_K - 1]
            for k in range(FFN_K):
                dw_ref[k:k + 1, :] += _colsum(acc[k])
            db_ref[...] += _colsum(accb)

        back(dg_ref, wg, ug_ref, dug_ref, dwg_ref, dbg_ref)
        back(dv_ref, wv, uv_ref, duv_ref, dwv_ref, dbv_ref)

    cur = pl.BlockSpec((ROWS, tn), lambda j, i: (i, j))
    nxt = pl.BlockSpec((FHALO, tn), lambda j, i: (jnp.minimum((i + 1) * per, nblk * per - 1), j))
    wsp = lambda off: pl.BlockSpec((FFN_K, tn), lambda j, i: (0, j + off))
    bsp = pl.BlockSpec((1, tn), lambda j, i: (0, j))
    outs = pl.pallas_call(
        body, name=name, grid=(nj, nblk),
        in_specs=[cur, nxt, cur, cur, cur, cur, nxt, nxt, wsp(0), wsp(nj)],
        out_specs=[cur, cur, wsp(0), wsp(0), bsp, bsp],
        out_shape=[jax.ShapeDtypeStruct((s, f), BF16), jax.ShapeDtypeStruct((s, f), BF16),
                   jax.ShapeDtypeStruct((FFN_K, f), F32), jax.ShapeDtypeStruct((FFN_K, f), F32),
                   jax.ShapeDtypeStruct((1, f), F32), jax.ShapeDtypeStruct((1, f), F32)],
        scratch_shapes=[pltpu.VMEM((ext, tn), F32), pltpu.VMEM((ext, tn), F32)],
        compiler_params=_params(("parallel", "arbitrary")))(dact, dact, up_g, up_v, conv_g, conv_v, conv_g, conv_v,
                                                            cw, cw)
    dug, duv, dwg, dwv, dbg, dbv = outs
    return dug, duv, jnp.concatenate([dwg, dwv], axis=1), jnp.concatenate([dbg, dbv], axis=1)


def rope_tables(s):
    half = HEAD // 2
    lane = jnp.arange(128)
    j = lane % HEAD
    inv = ROPE_THETA ** (-(j % half).astype(F32) / half)
    ang = jnp.arange(s, dtype=F32)[:, None] * inv[None, :]
    sign = jnp.where(j < half, -1.0, 1.0).astype(F32)
    return jnp.cos(ang), jnp.sin(ang) * sign[None, :]


def _swap_halves(x):
    lane = lax.broadcasted_iota(jnp.int32, x.shape, 1)
    return jnp.where((lane % HEAD) < HEAD // 2, pltpu.roll(x, 128 - HEAD // 2, 1), pltpu.roll(x, HEAD // 2, 1))


def qkv_rope(h, w, cos, sin, *, name):
    s, k = h.shape
    d = w.shape[1] // 3
    tm = _divisors(s, (1024, 512))[0]
    scale = HEAD ** -0.5

    def body(a_ref, b_ref, c_ref, s_ref, o_ref):
        third = pl.program_id(1)
        o_ref[...] = _dot(a_ref[...], b_ref[...], NN)

        @pl.when(third < 2)
        def _():
            c = c_ref[...]
            sn = s_ref[...]
            factor = jnp.where(third == 0, scale, 1.0)
            for t in range(d // 128):
                cs = slice(t * 128, (t + 1) * 128)
                x = o_ref[:, cs]
                o_ref[:, cs] = (x * c + _swap_halves(x) * sn) * factor

    tab = pl.BlockSpec((tm, 128), lambda i, j: (i, 0))
    return pl.pallas_call(
        body, name=name, grid=(s // tm, 3),
        in_specs=[pl.BlockSpec((tm, k), lambda i, j: (i, 0)), pl.BlockSpec((k, d), lambda i, j: (0, j)), tab, tab],
        out_specs=pl.BlockSpec((tm, d), lambda i, j: (i, j)),
        out_shape=jax.ShapeDtypeStruct((s, 3 * d), F32),
        compiler_params=_params(("parallel", "parallel")))(h, w, cos, sin)


def rope_bwd(dq, dk, dv, cos, sin, *, name):
    s, d = dq.shape
    scale = HEAD ** -0.5

    def body(dq_ref, dk_ref, dv_ref, c_ref, s_ref, o_ref):
        c = c_ref[...]
        sn = s_ref[...]
        for t in range(d // 128):
            cs = slice(t * 128, (t + 1) * 128)
            gq = dq_ref[:, cs] * scale
            gk = dk_ref[:, cs]
            o_ref[:, t * 128:(t + 1) * 128] = (gq * c + _swap_halves(gq * sn)).astype(BF16)
            o_ref[:, d + t * 128:d + (t + 1) * 128] = (gk * c + _swap_halves(gk * sn)).astype(BF16)
        o_ref[:, 2 * d:3 * d] = dv_ref[...].astype(BF16)

    row = pl.BlockSpec((ROWS, d), lambda i: (i, 0))
    tab = pl.BlockSpec((ROWS, 128), lambda i: (i, 0))
    return pl.pallas_call(
        body, name=name, grid=(s // ROWS,),
        in_specs=[row, row, row, tab, tab],
        out_specs=pl.BlockSpec((ROWS, 3 * d), lambda i: (i, 0)),
        out_shape=jax.ShapeDtypeStruct((s, 3 * d), BF16),
        compiler_params=_params(("parallel",)))(dq, dk, dv, cos, sin)


ATT_T = BLOCK * max(DILATIONS)


FWD_GROUP = 2
ATT_GROUP = 4
FWD_QROWS = 128
BWD_QROWS = 64


def _unit_rows(r, j, dil):
    start = r + dil * BLOCK * j
    return pl.ds(start, BLOCK) if dil == 1 else pl.ds(start, BLOCK, stride=dil)


def _units():
    for bi, dil in enumerate(DILATIONS):
        nsub = ATT_T // (BLOCK * dil)
        for r in range(dil):
            for j in range(nsub):
                yield bi, dil, nsub, r, j


def _band(first_block, part, qrows):
    qi = lax.broadcasted_iota(jnp.int32, (qrows, 2 * BLOCK), 0) + part * qrows
    kj = lax.broadcasted_iota(jnp.int32, (qrows, 2 * BLOCK), 1)
    dist = BLOCK + qi - kj
    band = (dist >= 0) & (dist <= BLOCK)
    return band, band & (jnp.logical_not(first_block) | (kj >= BLOCK))


def _col(tile, h):
    return tile[:, h * HEAD:h * HEAD + 1]


def _keys(cur_ref, prev_ref, r, j, dil, nsub):
    cur = cur_ref[_unit_rows(r, j, dil), :]
    prev = cur_ref[_unit_rows(r, j - 1, dil), :] if j > 0 else prev_ref[_unit_rows(r, nsub - 1, dil), :]
    return jnp.concatenate([prev, cur], axis=0).astype(BF16)


def _att_specs(col_off=0):
    cur = pl.BlockSpec((ATT_T, 128), lambda n, p: (n, p + col_off))
    prv = pl.BlockSpec((ATT_T, 128), lambda n, p: (jnp.maximum(n - 1, 0), p + col_off))
    return cur, prv


def attn_fwd(qkv, *, name):
    s, d = qkv.shape[0], qkv.shape[1] // 3
    nt = s // ATT_T

    def body(q_ref, kc_ref, kp_ref, vc_ref, vp_ref, o_ref, lse_ref, acc_ref, m_ref, l_ref):
        n = pl.program_id(0)
        QROWS = FWD_QROWS
        nparts = BLOCK // QROWS
        bands = [_band(n == 0, part, QROWS) for part in range(nparts)]
        lo = _pair_masks()
        keep = _head_keep(BLOCK)
        nb = len(DILATIONS)
        tile = lambda cols: jnp.where(lo, jnp.concatenate(cols[:nparts], axis=0),
                                      jnp.concatenate(cols[nparts:], axis=0))

        def scores(unit):
            bi, dil, nsub, r, j = unit
            rows = _unit_rows(r, j, dil)
            kw = _keys(kc_ref, kp_ref, r, j, dil, nsub)
            vw = _keys(vc_ref, vp_ref, r, j, dil, nsub)
            qp = q_ref[rows, :].astype(BF16)
            sc2 = _dot(jnp.concatenate([qp * keep[0], qp * keep[1]], axis=0), kw, NT)
            return dict(bi=bi, j=j, rows=rows, vw=vw, sc2=sc2, m_old=m_ref[rows, :] if bi > 0 else None)

        def softmax(u):
            prs, new_m, new_l, alpha = [], [], [], []
            for c in range(2 * nparts):
                h, part = divmod(c, nparts)
                valid = bands[part][0 if u['j'] > 0 else 1]
                sc = jnp.where(valid, u['sc2'][c * QROWS:(c + 1) * QROWS], NEG)
                mx = jnp.max(sc, axis=-1, keepdims=True)
                if u['bi'] == 0:
                    m_new = mx
                else:
                    m_old = _col(u['m_old'][part * QROWS:(part + 1) * QROWS], h)
                    m_new = jnp.maximum(m_old, mx)
                    alpha.append(jnp.exp(m_old - m_new))
                pr = jnp.exp(sc - m_new)
                new_m.append(m_new)
                new_l.append(jnp.sum(pr, axis=-1, keepdims=True))
                prs.append(pr.astype(BF16))
            u.update(pr2=jnp.concatenate(prs, axis=0), new_m=new_m, new_l=new_l, alpha=alpha)

        def combine(u):
            rows, bi = u['rows'], u['bi']
            pv2 = _dot(u['pr2'], u['vw'], NN)
            m_t = tile(u['new_m'])
            l_t = tile(u['new_l'])
            acc_t = jnp.where(lo, pv2[:BLOCK], pv2[BLOCK:])
            if bi > 0:
                a_t = tile(u['alpha'])
                l_t = a_t * l_ref[rows, :] + l_t
                acc_t = a_t * acc_ref[rows, :] + acc_t
            if bi == nb - 1:
                o_ref[rows, :] = acc_t / l_t
                lse_ref[rows, :] = m_t + jnp.log(l_t)
            else:
                acc_ref[rows, :] = acc_t
                m_ref[rows, :] = m_t
                l_ref[rows, :] = l_t

        units = list(_units())
        for first in range(0, len(units), FWD_GROUP):
            pair = [scores(u) for u in units[first:first + FWD_GROUP]]
            for u in pair:
                softmax(u)
            for u in pair:
                combine(u)

    cur, _ = _att_specs()
    kcur, kprv = _att_specs(d // 128)
    vcur, vprv = _att_specs(2 * (d // 128))
    return pl.pallas_call(
        body, name=name, grid=(nt, d // 128), in_specs=[cur, kcur, kprv, vcur, vprv], out_specs=[cur, cur],
        out_shape=[jax.ShapeDtypeStruct((s, d), F32)] * 2,
        scratch_shapes=[pltpu.VMEM((ATT_T, 128), F32)] * 3,
        compiler_params=_params(("parallel", "parallel")))(qkv, qkv, qkv, qkv, qkv)


def attn_delta(do, o, *, name):
    s, d = do.shape

    def body(do_ref, o_ref, dl_ref):
        lane = lax.broadcasted_iota(jnp.int32, (ROWS, 128), 1)
        lo = lane < HEAD
        for p in range(d // 128):
            cs = slice(p * 128, (p + 1) * 128)
            pr = do_ref[:, cs] * o_ref[:, cs]
            s0 = jnp.sum(jnp.where(lo, pr, 0.0), axis=-1, keepdims=True)
            s1 = jnp.sum(jnp.where(lo, 0.0, pr), axis=-1, keepdims=True)
            dl_ref[:, cs] = jnp.where(lo, s0, s1)

    row = pl.BlockSpec((ROWS, d), lambda i: (i, 0))
    return pl.pallas_call(
        body, name=name, grid=(s // ROWS,), in_specs=[row, row], out_specs=row,
        out_shape=jax.ShapeDtypeStruct((s, d), F32),
        compiler_params=_params(("parallel",)))(do, o)


def attn_dq(qkv, do, lse, delta, *, name):
    s, d = do.shape
    nt = s // ATT_T

    def body(q_ref, kc_ref, kp_ref, vc_ref, vp_ref, do_ref, l_ref, dl_ref, dq_ref):
        n = pl.program_id(0)
        QROWS = BWD_QROWS
        nparts = BLOCK // QROWS
        bands = [_band(n == 0, part, QROWS) for part in range(nparts)]
        lo = _pair_masks()
        keep = _head_keep(BLOCK)
        def scores(unit):
            bi, dil, nsub, r, j = unit
            rows = _unit_rows(r, j, dil)
            kw = _keys(kc_ref, kp_ref, r, j, dil, nsub)
            vw = _keys(vc_ref, vp_ref, r, j, dil, nsub)
            qp = q_ref[rows, :].astype(BF16)
            dop = do_ref[rows, :].astype(BF16)
            sc2 = _dot(jnp.concatenate([qp * keep[0], qp * keep[1]], axis=0), kw, NT)
            dp2 = _dot(jnp.concatenate([dop * keep[0], dop * keep[1]], axis=0), vw, NT)
            return dict(bi=bi, j=j, rows=rows, kw=kw, sc2=sc2, dp2=dp2, lt=l_ref[rows, :], dt=dl_ref[rows, :])

        def softmax_bwd(u):
            dss = []
            for c in range(2 * nparts):
                h, part = divmod(c, nparts)
                valid = bands[part][0 if u['j'] > 0 else 1]
                cr = slice(c * QROWS, (c + 1) * QROWS)
                pr_rows = slice(part * QROWS, (part + 1) * QROWS)
                pr = jnp.where(valid, jnp.exp(u['sc2'][cr] - _col(u['lt'][pr_rows], h)), 0.0)
                dss.append((pr * (u['dp2'][cr] - _col(u['dt'][pr_rows], h))).astype(BF16))
            u['ds2'] = jnp.concatenate(dss, axis=0)

        def combine(u):
            rows = u['rows']
            dq2 = _dot(u['ds2'], u['kw'], NN)
            dq_t = jnp.where(lo, dq2[:BLOCK], dq2[BLOCK:])
            if u['bi'] > 0:
                dq_t = dq_t + dq_ref[rows, :]
            dq_ref[rows, :] = dq_t

        units = list(_units())
        for first in range(0, len(units), ATT_GROUP):
            pair = [scores(u) for u in units[first:first + ATT_GROUP]]
            for u in pair:
                softmax_bwd(u)
            for u in pair:
                combine(u)

    cur, _ = _att_specs()
    kcur, kprv = _att_specs(d // 128)
    vcur, vprv = _att_specs(2 * (d // 128))
    return pl.pallas_call(
        body, name=name, grid=(nt, d // 128), in_specs=[cur, kcur, kprv, vcur, vprv, cur, cur, cur], out_specs=cur,
        out_shape=jax.ShapeDtypeStruct((s, d), F32),
        compiler_params=_params(("parallel", "parallel")))(qkv, qkv, qkv, qkv, qkv, do, lse, delta)


def attn_dkv(qkv, do, lse, delta, *, name):
    s, d = do.shape
    nt = s // ATT_T

    def body(k_ref, v_ref, qc_ref, qn_ref, doc_ref, don_ref, lc_ref, ln_ref, dc_ref, dn_ref, dk_ref, dv_ref):
        n = pl.program_id(0)
        qi = lax.broadcasted_iota(jnp.int32, (BLOCK, BLOCK), 0)
        kj = lax.broadcasted_iota(jnp.int32, (BLOCK, BLOCK), 1)
        own = kj <= qi
        nxt = kj >= qi
        nxt_edge = nxt & (n < nt - 1)
        keep = _head_keep(BLOCK)

        def scores(unit):
            bi, dil, nsub, r, j = unit
            rows = _unit_rows(r, j, dil)
            inner = j + 1 < nsub
            nrows = _unit_rows(r, j + 1, dil) if inner else _unit_rows(r, 0, dil)
            kp = k_ref[rows, :].astype(BF16)
            vp = v_ref[rows, :].astype(BF16)
            far = not inner
            take = lambda c_ref, n_ref, nx: ((n_ref if far else c_ref)[nrows, :] if nx else c_ref[rows, :])
            qs = [take(qc_ref, qn_ref, nx).astype(BF16) for nx in (False, True)]
            dos = [take(doc_ref, don_ref, nx).astype(BF16) for nx in (False, True)]
            lts = [take(lc_ref, ln_ref, nx) for nx in (False, True)]
            dts = [take(dc_ref, dn_ref, nx) for nx in (False, True)]
            q4 = jnp.concatenate([qs[nx] * keep[h] for h in range(2) for nx in range(2)], axis=0)
            do4 = jnp.concatenate([dos[nx] * keep[h] for h in range(2) for nx in range(2)], axis=0)
            return dict(bi=bi, rows=rows, q4=q4, do4=do4, s4=_dot(q4, kp, NT), dp4=_dot(do4, vp, NT), lts=lts,
                        dts=dts, valids=(own, nxt if inner else nxt_edge))

        def softmax_bwd(u):
            prs, dss = [], []
            for c in range(4):
                h, nx = divmod(c, 2)
                cr = slice(c * BLOCK, (c + 1) * BLOCK)
                pr = jnp.where(u['valids'][nx], jnp.exp(u['s4'][cr] - _col(u['lts'][nx], h)), 0.0)
                prs.append(pr.astype(BF16))
                dss.append((pr * (u['dp4'][cr] - _col(u['dts'][nx], h))).astype(BF16))
            u.update(pr4=jnp.concatenate(prs, axis=0), ds4=jnp.concatenate(dss, axis=0))

        def combine(u):
            rows = u['rows']
            dv_t = _dot(u['pr4'], u['do4'], TN)
            dk_t = _dot(u['ds4'], u['q4'], TN)
            if u['bi'] > 0:
                dk_t = dk_t + dk_ref[rows, :]
                dv_t = dv_t + dv_ref[rows, :]
            dk_ref[rows, :] = dk_t
            dv_ref[rows, :] = dv_t

        units = list(_units())
        for first in range(0, len(units), ATT_GROUP):
            pair = [scores(u) for u in units[first:first + ATT_GROUP]]
            for u in pair:
                softmax_bwd(u)
            for u in pair:
                combine(u)

    cur = pl.BlockSpec((ATT_T, 128), lambda n, p: (n, p))
    nxt_spec = pl.BlockSpec((ATT_T, 128), lambda n, p: (jnp.minimum(n + 1, nt - 1), p))
    kcur, _ = _att_specs(d // 128)
    vcur, _ = _att_specs(2 * (d // 128))
    return pl.pallas_call(
        body, name=name, grid=(nt, d // 128),
        in_specs=[kcur, vcur, cur, nxt_spec, cur, nxt_spec, cur, nxt_spec, cur, nxt_spec], out_specs=[cur, cur],
        out_shape=[jax.ShapeDtypeStruct((s, d), F32)] * 2,
        compiler_params=_params(("parallel", "parallel")))(qkv, qkv, qkv, qkv, do, do, lse, lse, delta, delta)


def adamw(parts_list, w, m, v, *, name):
    nk = len(parts_list)
    npart, rk, c = parts_list[0].shape
    r = rk * nk
    assert w.shape == (r, c)
    tr = next(t for t in range(rk, 0, -8) if rk % t == 0 and (t * c * 4 <= 1024 * 1024 or t == 8))
    nbk = rk // tr

    def body(*refs):
        p_refs = refs[:nk]
        w_ref, m_ref, v_ref, g_ref, d_ref, nm_ref, nv_ref = refs[nk:]
        i = pl.program_id(0)
        g = None
        for kk, p_ref in enumerate(p_refs):
            gk = p_ref[0].astype(F32)
            for j in range(1, npart):
                gk = gk + p_ref[j].astype(F32)
            g = gk if g is None else jnp.where(i >= kk * nbk, gk, g)
        m2 = B1 * m_ref[...] + (1.0 - B1) * g
        v2 = B2 * v_ref[...] + (1.0 - B2) * (g * g)
        m_hat = m2 / (1.0 - B1 ** STEP)
        v_hat = v2 / (1.0 - B2 ** STEP)
        g_ref[...] = g
        d_ref[...] = -LR * (m_hat / (jnp.sqrt(v_hat) + ADAM_EPS) + WD * w_ref[...])
        nm_ref[...] = m2
        nv_ref[...] = v2

    blk = pl.BlockSpec((tr, c), lambda i: (i, 0))
    pspec = lambda kk: pl.BlockSpec((npart, tr, c), lambda i: (0, jnp.clip(i - kk * nbk, 0, nbk - 1), 0))
    return pl.pallas_call(
        body, name=name, grid=(r // tr,),
        in_specs=[pspec(kk) for kk in range(nk)] + [blk, blk, blk],
        out_specs=[blk] * 4, out_shape=[jax.ShapeDtypeStruct((r, c), F32)] * 4,
        compiler_params=_params(("parallel",)))(*parts_list, w, m, v)


def _my_index():
    return 4 * lax.axis_index("x") + 2 * lax.axis_index("y") + lax.axis_index("c")


def _peer_of(kk):
    x, y, c = lax.axis_index("x"), lax.axis_index("y"), lax.axis_index("c")
    return x ^ (kk >> 2), y ^ ((kk >> 1) & 1), c ^ (kk & 1)


def _peer_copy(t, kk, scatter, ins, lands, send_sems, recv_sems):
    px, py, pc = _peer_of(kk)
    me = _my_index()
    src = ins[t].at[4 * px + 2 * py + pc] if scatter[t] else ins[t]
    return pltpu.make_async_remote_copy(
        src_ref=src, dst_ref=lands[t].at[me], send_sem=send_sems.at[t * N_DEV + kk],
        recv_sem=recv_sems.at[t * N_DEV + kk], device_id=(px, py, pc), device_id_type=pl.DeviceIdType.MESH)


def _own_copy(t, scatter, ins, lands, own_sems):
    me = _my_index()
    return pltpu.make_async_copy(ins[t].at[me] if scatter[t] else ins[t], lands[t].at[me], own_sems.at[t])


_HBM = pl.BlockSpec(memory_space=pltpu.HBM)
_SEM = pl.BlockSpec(memory_space=pltpu.SEMAPHORE)
_EFFECT = pltpu.SideEffectType.DATAFLOW_SIDE_EFFECTING


def exchange_start(arrays, scatter, *, name):
    nt = len(arrays)
    land_shapes = [a.shape if scatter[t] else (N_DEV,) + a.shape for t, a in enumerate(arrays)]

    def body(*refs):
        ins, lands = refs[:nt], refs[nt:2 * nt]
        send_sems, recv_sems, own_sems = refs[2 * nt:2 * nt + 3]
        token = refs[-1]
        for kk in range(1, N_DEV):
            for t in range(nt):
                _peer_copy(t, kk, scatter, ins, lands, send_sems, recv_sems).start()
        for t in range(nt):
            _own_copy(t, scatter, ins, lands, own_sems).start()
        token[...] = jnp.zeros_like(token)

    sems = pltpu.SemaphoreType.DMA((nt * N_DEV,))
    outs = pl.pallas_call(
        body, name=name,
        out_shape=(sems, sems, pltpu.SemaphoreType.DMA((nt,)), *[pltpu.HBM(a.shape, a.dtype) for a in arrays],
                   *[pltpu.HBM(shp, a.dtype) for shp, a in zip(land_shapes, arrays)],
                   jax.ShapeDtypeStruct((8, 128), F32)),
        in_specs=[_HBM] * (2 * nt),
        out_specs=(_SEM, _SEM, _SEM, *[_HBM] * (2 * nt), pl.BlockSpec(memory_space=pltpu.VMEM)),
        input_output_aliases={i: 3 + i for i in range(2 * nt)},
        compiler_params=pltpu.CompilerParams(has_side_effects=_EFFECT),
    )(*[pltpu.with_memory_space_constraint(a, pltpu.HBM) for a in arrays],
      *[pltpu.with_memory_space_constraint(lax.empty(shp, a.dtype), pltpu.HBM) for shp, a in zip(land_shapes, arrays)])
    return (outs[:3], outs[3:3 + nt], outs[3 + nt:3 + 2 * nt], scatter), outs[-1]


def exchange_wait(handle, after, *, name):
    sems, thru, lands, scatter = handle
    nt = len(thru)

    def body(*refs):
        ins, lnd = refs[:nt], refs[nt:2 * nt]
        s_sems, r_sems, o_sems = refs[2 * nt:2 * nt + 3]
        for kk in range(1, N_DEV):
            for t in range(nt):
                cp = _peer_copy(t, kk, scatter, ins, lnd, s_sems, r_sems)
                cp.wait_send()
                cp.wait_recv()
        for t in range(nt):
            _own_copy(t, scatter, ins, lnd, o_sems).wait()

    outs = pl.pallas_call(
        body, name=name,
        out_shape=(*[pltpu.HBM(a.shape, a.dtype) for a in thru], *[pltpu.HBM(a.shape, a.dtype) for a in lands]),
        in_specs=[_HBM] * (2 * nt) + [_SEM, _SEM, _SEM, pl.BlockSpec(memory_space=pl.ANY)],
        out_specs=tuple([_HBM] * (2 * nt)),
        input_output_aliases={i: i for i in range(2 * nt)},
        compiler_params=pltpu.CompilerParams(has_side_effects=_EFFECT),
    )(*thru, *lands, *sems, after)
    return outs[nt:]


def _cols_from_shards(g):
    g = jnp.moveaxis(g, 0, -2)
    return g.reshape(g.shape[:-2] + (g.shape[-2] * g.shape[-1],))


def _cols_to_shards(w, nshards=N_DEV):
    w = w.reshape(w.shape[:-1] + (nshards, w.shape[-1] // nshards))
    return jnp.moveaxis(w, -2, 0)


def _half_shards(halves):
    return jnp.concatenate([_cols_to_shards(h[None], N_DEV // 2) for h in halves], axis=0)


def _ffn_fwd(x, h, w_up, cw, cb, get_w_down, next_g, tag):
    act, up_g, up_v, conv_g, conv_v = ffn_up_mid(h, w_up[0], w_up[1], cw, cb, name=f"{tag}_up_mid")
    w_down = get_w_down(act)
    out = matmul(act, w_down, res=x, norm_g=next_g, name=f"{tag}_down")
    return out, (h, up_g, up_v, conv_g, conv_v, act), w_down


def _ffn_bwd(dx, dxb, x, saved, g, w_up, cw, w_down, tag):
    h, up_g, up_v, conv_g, conv_v, act = saved
    dact = matmul(dxb, w_down, tb=True, name=f"{tag}_ddown")
    d_w_down = matmul_ta(act, dxb, name=f"{tag}_gdown")
    dug, duv, dcw, dcb = ffn_mid_bwd(dact, up_g, up_v, conv_g, conv_v, cw, name=f"{tag}_dmid")
    d_w_up = (matmul_ta(h, dug, name=f"{tag}_gup_g"), matmul_ta(h, duv, name=f"{tag}_gup_v"))
    dh = matmul(dug, w_up[0], tb=True, name=f"{tag}_dup_g")
    dh = matmul(duv, w_up[1], tb=True, res=dh, name=f"{tag}_dup_v")
    dx2, dxb2, dg = rms_bwd(dh, x, g, dx, name=f"{tag}_dnorm")
    return dx2, dxb2, dict(norm_g=dg, w_up=d_w_up, conv_w=dcw, conv_b=dcb, w_down=d_w_down)


def local_step(x0, tgt, a, weights, grads_out):
    s, d = x0.shape
    aw = a['even_v_ln_g'].shape[-1]
    causal = jnp.tril(jnp.ones((CHUNK, CHUNK), dtype=bool))
    wm = jnp.where(causal, a['even_w_s'][0], 0.0).astype(BF16)
    wmt = jnp.swapaxes(wm, 1, 2)
    bm = jnp.repeat(a['even_b_s'][0].T, HEAD, axis=1)
    sel = (jnp.arange(aw)[:, None] // HEAD == jnp.arange(128)[None, :]).astype(BF16)
    cos, sin = rope_tables(s)
    ffn_g, ffn_cb = a['ffn_norm_g'], a['ffn_conv_b']

    w0 = weights(0, None)
    w_in, conv_w, odd_g, ffn_cw = w0['w_in'], w0['conv_w'], w0['odd_g'], w0['ffn_cw']
    h0 = rms_fwd(x0, w0['even_g'], name="even_norm")
    z = matmul(h0, w_in, bias=a['even_b_in'], name="even_in")
    ycat, hc = even_mid_fwd(z, a['even_v_ln_g'], a['even_v_ln_b'], wm, bm, conv_w, a['even_conv_b'],
                            a['even_conv_ln_g'], a['even_conv_ln_b'], name="even_mid")
    w1 = weights(1, ycat)
    x1, h1 = matmul(ycat, w1['w_out'], res=x0, norm_g=ffn_g[0:1], name="even_out")
    (x2, h2), ffn0, w_down0 = _ffn_fwd(x1, h1, w1['w_up'], ffn_cw[0], ffn_cb[0:1],
                                       lambda act: weights(2, act)['w_down'], odd_g, "ffn0")
    w2 = weights(3, h2)
    qkv = qkv_rope(h2, w2['w_qkv'], cos, sin, name="odd_qkv_rope")
    o, lse = attn_fwd(qkv, name="attn_fwd")
    x3, h3 = matmul(o, w2['w_o'], res=x2, norm_g=ffn_g[1:2], name="odd_out")
    w3 = weights(4, x3)
    x4, ffn1, _ = _ffn_fwd(x3, h3, w3['w_up'], ffn_cw[1], ffn_cb[1:2], lambda act: w3['w_down'], None, "ffn1")
    loss_t, dx, dxb, d_final_g = final_loss_bwd(x4, a['final_norm_g'].reshape(1, -1), tgt, name="final_loss")

    dx, dxb, g1 = _ffn_bwd(dx, dxb, x3, ffn1, ffn_g[1:2], w3['w_up'], ffn_cw[1], w3['w_down'], "ffn1")
    dep = grads_out(0, dict(w_up=g1['w_up'], w_down=g1['w_down']))
    do = matmul(dxb, w2['w_o'], tb=True, dep=dep, name="odd_dout")
    d_w_o = matmul_ta(o, dxb, name="odd_gout")
    delta = attn_delta(do, o, name="attn_delta")
    dq = attn_dq(qkv, do, lse, delta, name="attn_dq")
    dk, dv = attn_dkv(qkv, do, lse, delta, name="attn_dkv")
    dqkv = rope_bwd(dq, dk, dv, cos, sin, name="rope_bwd")
    d_w_qkv = matmul_ta(h2, dqkv, name="odd_gqkv")
    dep = grads_out(1, dict(w_qkv=d_w_qkv, w_o=d_w_o))
    dh2 = matmul(dqkv, w2['w_qkv'], tb=True, dep=dep, name="odd_dqkv")
    dx, dxb, d_odd_g = rms_bwd(dh2, x2, odd_g, dx, name="odd_dnorm")
    dx, dxb, g0 = _ffn_bwd(dx, dxb, x1, ffn0, ffn_g[0:1], w1['w_up'], ffn_cw[0], w_down0, "ffn0")
    dep = grads_out(2, dict(w_up=g0['w_up'], w_down=g0['w_down']))
    d_w_out = matmul_ta(ycat, dxb, dep=dep, name="even_gout")
    dep = grads_out(3, dict(w_out=d_w_out))
    dycat = matmul(dxb, w1['w_out'], tb=True, dep=dep, name="even_dout")
    (dza, dhc, dba, dvg, dvb, dwm, dbs, dcg, dcbeta, dcb) = even_mid_bwd_rows(
        dycat, z, hc, a['even_v_ln_g'], a['even_v_ln_b'], wm, wmt, bm, sel, a['even_conv_ln_g'],
        a['even_conv_ln_b'], name="even_dmid_rows")
    dzb, dbb, dcw = even_conv_bwd(dhc, z, conv_w, name="even_dmid_conv")
    nh = a['even_w_s'].shape[1]
    small_grads = {
        'even_b_in': jnp.concatenate([dba, dbb], axis=1), 'even_v_ln_g': dvg,
        'even_v_ln_b': dvb, 'even_w_s': jnp.where(causal, dwm, 0.0)[None], 'even_b_s': dbs[:, :nh].T[None],
        'even_conv_w': dcw[None], 'even_conv_b': dcb, 'even_conv_ln_g': dcg, 'even_conv_ln_b': dcbeta,
        'odd_norm_g': d_odd_g, 'ffn_norm_g': jnp.concatenate([g0['norm_g'], g1['norm_g']], axis=0),
        'ffn_conv_w': jnp.stack([g0['conv_w'], g1['conv_w']]),
        'ffn_conv_b': jnp.concatenate([g0['conv_b'], g1['conv_b']], axis=0),
        'final_norm_g': d_final_g.reshape(-1),
    }
    dep = grads_out(5, dict(small=small_grads))
    d_w_in = (matmul_ta(h0, dza, dep=dep, name="even_gin_a"), matmul_ta(h0, dzb, name="even_gin_b"))
    dep = grads_out(4, dict(w_in=d_w_in))
    dh0 = matmul(dza, w_in, tb=True, bk=0, dep=dep, name="even_din_a")
    dh0 = matmul(dzb, w_in, tb=True, bk=1, res=dh0, name="even_din_b")
    grad_x, _, d_even_g = rms_bwd(dh0, x0, w0['even_g'], dx, name="even_dnorm")
    last = {'even_norm_g': d_even_g}
    grads_out(6, dict(small=last))
    return loss_t, grad_x, {**last, **small_grads}


BIG = ['even_w_in', 'even_w_out', 'odd_w_qkv', 'odd_w_o', 'ffn_w_up', 'ffn_w_down']


def _as_tiles(flat, dtype=F32):
    return jnp.pad(flat, (0, (-flat.size) % 2048)).reshape(-1, 128).astype(dtype)


def kernel(*args):
    a = dict(zip(NAMES + ['loss_target'] + ['m_' + n for n in WEIGHTS] + ['v_' + n for n in WEIGHTS], args))
    x0 = a['x'][0]
    tgt = a['loss_target'][0]
    s, d = x0.shape
    me = _my_index()
    bf = lambda t: t.astype(BF16)

    small_local = _as_tiles(jnp.concatenate([a['even_conv_w'].reshape(-1), a['odd_norm_g'].reshape(-1),
                                             a['ffn_conv_w'].reshape(-1)]))
    stage_arrays = [
        [bf(a['even_w_in']), small_local],
        [bf(a['even_w_out']), bf(a['ffn_w_up'][0:1])],
        [bf(a['ffn_w_down'][0:1])],
        [bf(a['odd_w_qkv']), bf(a['odd_w_o'])],
        [bf(a['ffn_w_up'][1:2]), bf(a['ffn_w_down'][1:2])],
    ]
    started = [exchange_start(arrs, [False] * len(arrs), name=f"gather{i}_start") for i, arrs in enumerate(stage_arrays)]
    order = sum(tok[0, 0] for _, tok in started)

    def weights(stage, after):
        handle, tok = started[stage]
        full = exchange_wait(handle, tok if after is None else after, name=f"gather{stage}_wait")
        rows = lambda g: jnp.moveaxis(g, 0, 1).reshape(-1, d)
        halves = lambda g: (_cols_from_shards(g[:N_DEV // 2])[0], _cols_from_shards(g[N_DEV // 2:])[0])
        if stage == 0:
            gs = full[1].reshape(N_DEV, -1)
            n_cw, n_og, n_fw = a['even_conv_w'].size, a['odd_norm_g'].size, a['ffn_conv_w'].size
            return dict(
                w_in=_cols_from_shards(full[0])[0], even_g=a['even_norm_g'] + order,
                conv_w=_cols_from_shards(gs[:, :n_cw].reshape((N_DEV,) + a['even_conv_w'].shape))[0],
                odd_g=gs[:, n_cw:n_cw + n_og].reshape(1, -1),
                ffn_cw=_cols_from_shards(gs[:, n_cw + n_og:n_cw + n_og + n_fw].reshape((N_DEV,) + a['ffn_conv_w'].shape)))
        if stage == 1:
            return dict(w_out=rows(full[0]), w_up=halves(full[1]))
        if stage == 2:
            return dict(w_down=rows(full[0]))
        if stage == 3:
            return dict(w_qkv=_cols_from_shards(full[0])[0], w_o=rows(full[1]))
        return dict(w_up=halves(full[0]), w_down=rows(full[1]))

    sent = {}

    def grads_out(stage, g):
        to_rows = lambda w: w.reshape(N_DEV, 1, -1, d)
        if stage in (0, 2):
            pieces, scatter = [_half_shards(g['w_up']), to_rows(g['w_down'])], [True, True]
        elif stage == 1:
            pieces, scatter = [_cols_to_shards(g['w_qkv'][None]), to_rows(g['w_o'])], [True, True]
        elif stage == 3:
            pieces, scatter = [to_rows(g['w_out'])], [True]
        elif stage == 4:
            pieces, scatter = [_half_shards(g['w_in'])], [True]
        else:
            small = jnp.concatenate([g['small'][n].reshape(-1) for n in g['small']])
            pieces, scatter = [_as_tiles(small, BF16)], [False]
        sent[stage], tok = exchange_start(pieces, scatter, name=f"grads{stage}_start")
        return tok

    loss_t, grad_x, small_grads = local_step(x0, tgt, a, weights, grads_out)
    loss = lax.psum(loss_t[0, 0], ("x", "y", "c"))
    received = {stage: exchange_wait(handle, grad_x, name=f"grads{stage}_wait") for stage, handle in sent.items()}

    results = {}
    big_parts = {'even_w_in': [received[4][0]], 'even_w_out': [received[3][0]], 'odd_w_qkv': [received[1][0]],
                 'odd_w_o': [received[1][1]], 'ffn_w_up': [received[2][0], received[0][0]],
                 'ffn_w_down': [received[2][1], received[0][1]]}
    for n in BIG:
        shp = a[n].shape
        flat = lambda t: t.reshape(-1, shp[-1])
        outs = adamw([p.reshape(N_DEV, -1, shp[-1]) for p in big_parts[n]], flat(a[n]), flat(a['m_' + n]),
                     flat(a['v_' + n]), name=f"adamw_{n}")
        results[n] = [t.reshape(shp) for t in outs]

    small_names = list(small_grads)
    n_small = sum(small_grads[n].size for n in small_names)
    n_last = small_grads[small_names[0]].size
    rs = jnp.concatenate([received[6][0].reshape(N_DEV, -1)[:, :n_last],
                          received[5][0].reshape(N_DEV, -1)[:, :n_small - n_last]], axis=1)
    parts, offs = [], 0
    for n in small_names:
        full = small_grads[n].shape
        piece = rs[:, offs:offs + small_grads[n].size].reshape((N_DEV,) + full)
        offs += small_grads[n].size
        shp = a[n].shape
        if shp != full:
            width = shp[-1]
            piece = lax.dynamic_slice_in_dim(piece, me * width, width, axis=piece.ndim - 1)
        parts.append(piece.reshape(N_DEV, -1))
    parts = jnp.concatenate(parts, axis=1)
    pad = (-parts.shape[1]) % 2048
    cat = lambda pre: _as_tiles(jnp.concatenate([a[pre + n].reshape(-1) for n in small_names]))
    outs = adamw([jnp.pad(parts, ((0, 0), (0, pad))).reshape(N_DEV, -1, 128)], cat(''), cat('m_'), cat('v_'),
                 name="adamw_small")
    offs = 0
    for n in small_names:
        size = a[n].size
        results[n] = [t.reshape(-1)[offs:offs + size].reshape(a[n].shape) for t in outs]
        offs += size

    out = [loss, grad_x[None]]
    for i in range(4):
        out += [results[n][i] for n in WEIGHTS]
    return tuple(out)
```

```python
import math

import jax
import jax.numpy as jnp
from jax import lax
from jax.experimental import pallas as pl
from jax.experimental.pallas import tpu as pltpu

F32 = jnp.float32
BF16 = jnp.bfloat16

N_DEV = 8
EPS = 1e-6
NEG = -1e30
HEAD = 64
CHUNK = 128
BLOCK = 128
CONV_K = 31
FFN_K = 3
DILATIONS = (1, 4, 16)
ROPE_THETA = 10000.0
LR, B1, B2, ADAM_EPS, WD, STEP = 0.001, 0.9, 0.999, 1e-08, 0.01, 10

VMEM_LIMIT = 56 * 1024 * 1024
VMEM_BUDGET = 32 * 1024 * 1024
ROWS = 512
HALO = 32
FHALO = 8

NAMES = ['x', 'even_norm_g', 'even_w_in', 'even_b_in', 'even_v_ln_g', 'even_v_ln_b', 'even_w_s', 'even_b_s',
         'even_conv_w', 'even_conv_b', 'even_conv_ln_g', 'even_conv_ln_b', 'even_w_out', 'odd_norm_g',
         'odd_w_qkv', 'odd_w_o', 'ffn_norm_g', 'ffn_w_up', 'ffn_conv_w', 'ffn_conv_b', 'ffn_w_down',
         'final_norm_g']
WEIGHTS = NAMES[1:]


def _params(sem=None):
    return pltpu.CompilerParams(dimension_semantics=sem, vmem_limit_bytes=VMEM_LIMIT)


def _sigmoid(x):
    return 1.0 / (1.0 + jnp.exp(-x))


def _gelu(x):
    c = math.sqrt(2.0 / math.pi)
    return 0.5 * x * (1.0 + jnp.tanh(c * (x + 0.044715 * x * x * x)))


def _gelu_grad(x):
    c = math.sqrt(2.0 / math.pi)
    t = jnp.tanh(c * (x + 0.044715 * x * x * x))
    return 0.5 * (1.0 + t) + 0.5 * x * (1.0 - t * t) * c * (1.0 + 3.0 * 0.044715 * x * x)


def _ln_stats(x):
    mu = jnp.mean(x, axis=-1, keepdims=True)
    xc = x - mu
    rstd = lax.rsqrt(jnp.mean(xc * xc, axis=-1, keepdims=True) + EPS)
    return xc * rstd, rstd


def _ln_bwd(dy, xhat, rstd, g):
    dxh = dy * g
    return rstd * (dxh - jnp.mean(dxh, axis=-1, keepdims=True) - xhat * jnp.mean(dxh * xhat, axis=-1, keepdims=True))


def _colsum(x):
    return jnp.sum(x, axis=0, keepdims=True)


def _split3(x):
    hi = x.astype(BF16)
    r = x - hi.astype(F32)
    mid = r.astype(BF16)
    lo = (r - mid.astype(F32)).astype(BF16)
    return hi, mid, lo


def _dot(a, b, dims):
    return lax.dot_general(a, b, (dims, ((), ())), preferred_element_type=F32)


NN = ((1,), (0,))
NT = ((1,), (1,))
TN = ((0,), (0,))


def _divisors(n, cands):
    return [c for c in cands if c <= n and n % c == 0]


def _pick_tiles(m, n, k, a_bytes, b_bytes, o_bytes, extra_bytes):
    best = None
    for tm in _divisors(m, (1024, 512, 256, 128)):
        for tn in _divisors(n, (1408, 1024, 768, 704, 512, 384, 256, 128)):
            if tn % 128:
                continue
            need = 2 * (tm * k * a_bytes + k * tn * b_bytes + tm * tn * (o_bytes + extra_bytes)) + tm * tn * 4
            if need <= VMEM_BUDGET and (best is None or tm * tn > best[0] * best[1]):
                best = (tm, tn)
    assert best is not None, (m, n, k)
    return best


def matmul(a, b, *, tb=False, bk=0, bias=None, res=None, norm_g=None, dep=None, out_dtype=F32, name):
    m, k = a.shape
    n = b.shape[0] if tb else b.shape[1]
    assert b.shape[1] % k == 0 if tb else (b.shape[0] == k and bk == 0)
    if norm_g is None:
        tm, tn = _pick_tiles(m, n, k, a.dtype.itemsize, b.dtype.itemsize, jnp.dtype(out_dtype).itemsize,
                             4 if res is not None else 0)
    else:
        tm, tn = _divisors(m, (512,))[0], n

    def body(*refs):
        a_ref, b_ref = refs[:2]
        o_ref = refs[-2] if norm_g is not None else refs[-1]
        acc = _dot(a_ref[...].astype(BF16), b_ref[...].astype(BF16), NT if tb else NN)
        pos = 2
        if bias is not None:
            acc = acc + refs[pos][...]
            pos += 1
        if res is not None:
            acc = acc + refs[pos][...]
            pos += 1
        o_ref[...] = acc.astype(out_dtype)
        if norm_g is not None:
            r = lax.rsqrt(jnp.mean(acc * acc, axis=-1, keepdims=True) + EPS)
            refs[-1][...] = (acc * r * refs[pos][...]).astype(BF16)

    in_specs = [pl.BlockSpec((tm, k), lambda i, j: (i, 0)),
                pl.BlockSpec((tn, k), lambda i, j: (j, bk)) if tb else pl.BlockSpec((k, tn), lambda i, j: (0, j))]
    args = [a, b]
    if bias is not None:
        in_specs.append(pl.BlockSpec((1, tn), lambda i, j: (0, j)))
        args.append(bias)
    if res is not None:
        in_specs.append(pl.BlockSpec((tm, tn), lambda i, j: (i, j)))
        args.append(res)
    if norm_g is not None:
        in_specs.append(pl.BlockSpec((1, tn), lambda i, j: (0, j)))
        args.append(norm_g)
    if dep is not None:
        in_specs.append(pl.BlockSpec(memory_space=pl.ANY))
        args.append(dep)
    blk = pl.BlockSpec((tm, tn), lambda i, j: (i, j))
    out_shape = jax.ShapeDtypeStruct((m, n), out_dtype)
    return pl.pallas_call(
        body, name=name, grid=(m // tm, n // tn), in_specs=in_specs,
        out_specs=blk if norm_g is None else [blk, blk],
        out_shape=out_shape if norm_g is None else [out_shape, jax.ShapeDtypeStruct((m, n), BF16)],
        compiler_params=_params(("parallel", "parallel")))(*args)


def matmul_norm_bwd(a, b, x, g, dres, *, bk=0, res=None, dep=None, name):
    m, k = a.shape
    n = b.shape[0]
    tm = _divisors(m, (512,))[0]

    def body(*refs):
        a_ref, b_ref, x_ref, g_ref, dres_ref = refs[:5]
        dx_ref, dxb_ref, dg_ref = refs[-3:]
        dhv = _dot(a_ref[...].astype(BF16), b_ref[...].astype(BF16), NT)
        if res is not None:
            dhv = dhv + refs[5][...]
        xv = x_ref[...]
        r = lax.rsqrt(jnp.mean(xv * xv, axis=-1, keepdims=True) + EPS)
        xhat = xv * r
        dxh = dhv * g_ref[...]
        dx = dres_ref[...] + r * (dxh - xhat * jnp.mean(dxh * xhat, axis=-1, keepdims=True))
        dx_ref[...] = dx
        dxb_ref[...] = dx.astype(BF16)

        @pl.when(pl.program_id(0) == 0)
        def _():
            dg_ref[...] = jnp.zeros_like(dg_ref)
        dg_ref[...] += _colsum(dhv * xhat)

    row = pl.BlockSpec((tm, n), lambda i: (i, 0))
    vec = pl.BlockSpec((1, n), lambda i: (0, 0))
    in_specs = [pl.BlockSpec((tm, k), lambda i: (i, 0)), pl.BlockSpec((n, k), lambda i: (0, bk)), row, vec, row]
    args = [a, b, x, g, dres]
    if res is not None:
        in_specs.append(row)
        args.append(res)
    if dep is not None:
        in_specs.append(pl.BlockSpec(memory_space=pl.ANY))
        args.append(dep)
    return pl.pallas_call(
        body, name=name, grid=(m // tm,), in_specs=in_specs, out_specs=[row, row, vec],
        out_shape=[jax.ShapeDtypeStruct((m, n), F32), jax.ShapeDtypeStruct((m, n), BF16),
                   jax.ShapeDtypeStruct((1, n), F32)],
        compiler_params=_params(("arbitrary",)))(*args)


def matmul_ta(a, b, *, dep=None, out_dtype=BF16, name):
    s, m = a.shape
    n = b.shape[1]
    assert b.shape[0] == s
    best = None
    for tm in _divisors(m, (512, 256, 128)):
        for tn in _divisors(n, (1024, 512, 384, 256, 128)):
            need = 2 * (s * tm * a.dtype.itemsize + s * tn * b.dtype.itemsize + tm * tn * 2) + tm * tn * 4
            if need <= VMEM_BUDGET and (best is None or tm * tn > best[0] * best[1]):
                best = (tm, tn)
    tm, tn = best

    def body(*refs):
        a_ref, b_ref, o_ref = refs[0], refs[1], refs[-1]
        o_ref[...] = _dot(a_ref[...].astype(BF16), b_ref[...].astype(BF16), TN).astype(out_dtype)

    in_specs = [pl.BlockSpec((s, tm), lambda i, j: (0, i)), pl.BlockSpec((s, tn), lambda i, j: (0, j))]
    args = [a, b]
    if dep is not None:
        in_specs.append(pl.BlockSpec(memory_space=pl.ANY))
        args.append(dep)
    return pl.pallas_call(
        body, name=name, grid=(m // tm, n // tn), in_specs=in_specs,
        out_specs=pl.BlockSpec((tm, tn), lambda i, j: (i, j)),
        out_shape=jax.ShapeDtypeStruct((m, n), out_dtype),
        compiler_params=_params(("parallel", "parallel")))(*args)


def rms_fwd(x, g, *, name):
    s, d = x.shape

    def body(x_ref, g_ref, h_ref):
        xv = x_ref[...]
        r = lax.rsqrt(jnp.mean(xv * xv, axis=-1, keepdims=True) + EPS)
        h_ref[...] = (xv * r * g_ref[...]).astype(BF16)

    return pl.pallas_call(
        body, name=name, grid=(s // ROWS,),
        in_specs=[pl.BlockSpec((ROWS, d), lambda i: (i, 0)), pl.BlockSpec((1, d), lambda i: (0, 0))],
        out_specs=pl.BlockSpec((ROWS, d), lambda i: (i, 0)),
        out_shape=jax.ShapeDtypeStruct((s, d), BF16),
        compiler_params=_params(("parallel",)))(x, g)


def final_loss_bwd(x, g, tgt, *, name):
    s, d = x.shape

    def body(x_ref, g_ref, t_ref, loss_ref, dx_ref, dxb_ref, dg_ref):
        xv = x_ref[...]
        gv = g_ref[...]
        r = lax.rsqrt(jnp.mean(xv * xv, axis=-1, keepdims=True) + EPS)
        xhat = xv * r
        e = xhat * gv - t_ref[...]
        dy = e * (1.0 / d)
        dxh = dy * gv
        dx = r * (dxh - xhat * jnp.mean(dxh * xhat, axis=-1, keepdims=True))
        dx_ref[...] = dx
        dxb_ref[...] = dx.astype(BF16)

        @pl.when(pl.program_id(0) == 0)
        def _():
            dg_ref[...] = jnp.zeros_like(dg_ref)
            loss_ref[...] = jnp.zeros_like(loss_ref)
        dg_ref[...] += _colsum(dy * xhat)
        loss_ref[...] += 0.5 * jnp.sum(jnp.mean(e * e, axis=-1, keepdims=True))

    row = pl.BlockSpec((ROWS, d), lambda i: (i, 0))
    vec = pl.BlockSpec((1, d), lambda i: (0, 0))
    one = pl.BlockSpec((8, 128), lambda i: (0, 0))
    return pl.pallas_call(
        body, name=name, grid=(s // ROWS,),
        in_specs=[row, vec, row], out_specs=[one, row, row, vec],
        out_shape=[jax.ShapeDtypeStruct((8, 128), F32), jax.ShapeDtypeStruct((s, d), F32),
                   jax.ShapeDtypeStruct((s, d), BF16), jax.ShapeDtypeStruct((1, d), F32)],
        compiler_params=_params(("arbitrary",)))(x, g, tgt)


def _pair_masks():
    lane = lax.broadcasted_iota(jnp.int32, (CHUNK, 128), 1)
    return lane < HEAD


def _head_keep(rows):
    lane = lax.broadcasted_iota(jnp.int32, (rows, 128), 1)
    first = jnp.where(lane < HEAD, 1.0, 0.0)
    return first.astype(BF16), (1.0 - first).astype(BF16)


def _gating_mixed(vn_b, wm_ref, lo):
    rows, aw = vn_b.shape
    out = []
    for c in range(rows // CHUNK):
        tiles = []
        for p in range(aw // 128):
            vp = vn_b[c * CHUNK:(c + 1) * CHUNK, p * 128:(p + 1) * 128]
            r0 = _dot(wm_ref[2 * p], vp, NN)
            r1 = _dot(wm_ref[2 * p + 1], vp, NN)
            tiles.append(jnp.where(lo, r0, r1))
        out.append(jnp.concatenate(tiles, axis=1))
    return jnp.concatenate(out, axis=0)


def even_mid_fwd(z, vg, vb, wm, bm, cw, cb, cg, cbeta, *, name):
    s, zw = z.shape
    aw = zw // 4
    nblk = s // ROWS

    def body(z_ref, zp_ref, vg_ref, vb_ref, wm_ref, bm_ref, cw_ref, cb_ref, cg_ref, cbeta_ref,
             y_ref, hc_ref, ext_ref):
        i = pl.program_id(0)
        lo = _pair_masks()
        u = _gelu(z_ref[:, 0:aw])
        v = _gelu(z_ref[:, aw:2 * aw])
        vhat, _ = _ln_stats(v)
        vn = (vhat * vg_ref[...] + vb_ref[...]).astype(BF16)
        mixed = _gating_mixed(vn, wm_ref, lo)
        bias = jnp.concatenate([bm_ref[...]] * (ROWS // CHUNK), axis=0)
        y_ref[:, 0:aw] = (u * (mixed + bias)).astype(BF16)

        hb = z_ref[:, 2 * aw:3 * aw] * _sigmoid(z_ref[:, 3 * aw:4 * aw])
        hbp = zp_ref[:, 0:aw] * _sigmoid(zp_ref[:, aw:2 * aw])
        ext_ref[0:HALO, :] = jnp.where(i > 0, hbp, 0.0)
        ext_ref[HALO:HALO + ROWS, :] = hb
        acc = jnp.zeros((ROWS, aw), F32) + cb_ref[...]
        for k in range(CONV_K):
            acc = acc + cw_ref[k:k + 1, :] * ext_ref[pl.ds(HALO - (CONV_K - 1) + k, ROWS), :]
        hc_ref[...] = acc
        hhat, _ = _ln_stats(acc)
        hn = hhat * cg_ref[...] + cbeta_ref[...]
        y_ref[:, aw:2 * aw] = (hn * _sigmoid(hn)).astype(BF16)

    hb_per = ROWS // HALO
    vec = pl.BlockSpec((1, aw), lambda i: (0, 0))
    return pl.pallas_call(
        body, name=name, grid=(nblk,),
        in_specs=[pl.BlockSpec((ROWS, zw), lambda i: (i, 0)),
                  pl.BlockSpec((HALO, 2 * aw), lambda i: (jnp.maximum(i * hb_per - 1, 0), 1)),
                  vec, vec,
                  pl.BlockSpec(wm.shape, lambda i: (0, 0, 0)),
                  pl.BlockSpec((CHUNK, aw), lambda i: (0, 0)),
                  pl.BlockSpec((CONV_K, aw), lambda i: (0, 0)), vec, vec, vec],
        out_specs=[pl.BlockSpec((ROWS, 2 * aw), lambda i: (i, 0)), pl.BlockSpec((ROWS, aw), lambda i: (i, 0))],
        out_shape=[jax.ShapeDtypeStruct((s, 2 * aw), BF16), jax.ShapeDtypeStruct((s, aw), F32)],
        scratch_shapes=[pltpu.VMEM((HALO + ROWS, aw), F32)],
        compiler_params=_params(("parallel",)))(z, z, vg, vb, wm, bm, cw, cb, cg, cbeta)


def even_mid_bwd_rows(dy, z, hc, vg, vb, wm, wmt, bm, sel, cg, cbeta, *, name):
    s, zw = z.shape
    aw = zw // 4
    nh = wm.shape[0]

    def body(dy_ref, z_ref, hc_ref, vg_ref, vb_ref, wm_ref, wmt_ref, bm_ref, sel_ref, cg_ref, cbeta_ref,
             dza_ref, dhc_ref, dba_ref, dvg_ref, dvb_ref, dwm_ref, dbs_ref, dcg_ref, dcbeta_ref, dcb_ref):
        @pl.when(pl.program_id(0) == 0)
        def _():
            for r in (dba_ref, dvg_ref, dvb_ref, dwm_ref, dbs_ref, dcg_ref, dcbeta_ref, dcb_ref):
                r[...] = jnp.zeros_like(r)

        lo = _pair_masks()
        keep = _head_keep(CHUNK)
        zu = z_ref[:, 0:aw]
        zv = z_ref[:, aw:2 * aw]
        u = _gelu(zu)
        v = _gelu(zv)
        vhat, vrstd = _ln_stats(v)
        vn = (vhat * vg_ref[...] + vb_ref[...]).astype(BF16)
        mixed = _gating_mixed(vn, wm_ref, lo)
        bias = jnp.concatenate([bm_ref[...]] * (ROWS // CHUNK), axis=0)
        dya = dy_ref[:, 0:aw]
        du = dya * (mixed + bias)
        dmix = dya * u
        dmix_b = dmix.astype(BF16)

        dvn_rows = []
        for c in range(ROWS // CHUNK):
            rs = slice(c * CHUNK, (c + 1) * CHUNK)
            tiles = []
            for p in range(aw // 128):
                cs = slice(p * 128, (p + 1) * 128)
                dm = dmix_b[rs, cs]
                dm0 = dm * keep[0]
                dm1 = dm * keep[1]
                vp = vn[rs, cs]
                tiles.append(_dot(wmt_ref[2 * p], dm0, NN) + _dot(wmt_ref[2 * p + 1], dm1, NN))
                dwm_ref[2 * p] += _dot(dm0, vp, NT)
                dwm_ref[2 * p + 1] += _dot(dm1, vp, NT)
            dvn_rows.append(jnp.concatenate(tiles, axis=1))
            acc = jnp.zeros((CHUNK, 128), F32)
            for part in _split3(dmix[rs, :]):
                acc = acc + _dot(part, sel_ref[...], NN)
            dbs_ref[...] += acc
        dvn = jnp.concatenate(dvn_rows, axis=0)
        dvg_ref[...] += _colsum(dvn * vhat)
        dvb_ref[...] += _colsum(dvn)
        dv = _ln_bwd(dvn, vhat, vrstd, vg_ref[...])
        dzu = du * _gelu_grad(zu)
        dzv = dv * _gelu_grad(zv)
        dza_ref[:, 0:aw] = dzu.astype(BF16)
        dza_ref[:, aw:2 * aw] = dzv.astype(BF16)
        dba_ref[:, 0:aw] += _colsum(dzu)
        dba_ref[:, aw:2 * aw] += _colsum(dzv)

        hcv = hc_ref[...]
        hhat, hrstd = _ln_stats(hcv)
        hn = hhat * cg_ref[...] + cbeta_ref[...]
        sg = _sigmoid(hn)
        dhn = dy_ref[:, aw:2 * aw] * (sg * (1.0 + hn * (1.0 - sg)))
        dcg_ref[...] += _colsum(dhn * hhat)
        dcbeta_ref[...] += _colsum(dhn)
        dhc = _ln_bwd(dhn, hhat, hrstd, cg_ref[...])
        dhc_ref[...] = dhc
        dcb_ref[...] += _colsum(dhc)

    vec = pl.BlockSpec((1, aw), lambda i: (0, 0))
    vec2 = pl.BlockSpec((1, 2 * aw), lambda i: (0, 0))
    w3 = pl.BlockSpec(wm.shape, lambda i: (0, 0, 0))
    sq = pl.BlockSpec((CHUNK, 128), lambda i: (0, 0))
    return pl.pallas_call(
        body, name=name, grid=(s // ROWS,),
        in_specs=[pl.BlockSpec((ROWS, 2 * aw), lambda i: (i, 0)), pl.BlockSpec((ROWS, 2 * aw), lambda i: (i, 0)),
                  pl.BlockSpec((ROWS, aw), lambda i: (i, 0)), vec, vec, w3, w3,
                  pl.BlockSpec((CHUNK, aw), lambda i: (0, 0)), pl.BlockSpec((aw, 128), lambda i: (0, 0)), vec, vec],
        out_specs=[pl.BlockSpec((ROWS, 2 * aw), lambda i: (i, 0)), pl.BlockSpec((ROWS, aw), lambda i: (i, 0)),
                   vec2, vec, vec, w3, sq, vec, vec, vec],
        out_shape=[jax.ShapeDtypeStruct((s, 2 * aw), BF16), jax.ShapeDtypeStruct((s, aw), F32),
                   jax.ShapeDtypeStruct((1, 2 * aw), F32), jax.ShapeDtypeStruct((1, aw), F32),
                   jax.ShapeDtypeStruct((1, aw), F32), jax.ShapeDtypeStruct(wm.shape, F32),
                   jax.ShapeDtypeStruct((CHUNK, 128), F32), jax.ShapeDtypeStruct((1, aw), F32),
                   jax.ShapeDtypeStruct((1, aw), F32), jax.ShapeDtypeStruct((1, aw), F32)],
        compiler_params=_params(("arbitrary",)))(dy, z, hc, vg, vb, wm, wmt, bm, sel, cg, cbeta)


def even_conv_bwd(dhc, z, cw, *, name):
    s, zw = z.shape
    aw = zw // 4
    nblk = s // ROWS
    hb_per = ROWS // HALO
    rc, lc = 8, 128

    def body(dc_ref, dn_ref, z_ref, cw_ref, dzb_ref, dbb_ref, dcw_ref, extd_ref):
        i = pl.program_id(0)

        @pl.when(i == 0)
        def _():
            dbb_ref[...] = jnp.zeros_like(dbb_ref)
            dcw_ref[...] = jnp.zeros_like(dcw_ref)

        extd_ref[0:ROWS, :] = dc_ref[...]
        extd_ref[ROWS:ROWS + HALO, :] = jnp.where(i < nblk - 1, dn_ref[...], 0.0)
        for c0 in range(0, aw, lc):
            lanes = slice(c0, c0 + lc)
            acc = [jnp.zeros((rc, lc), F32)] * CONV_K
            acc_a = jnp.zeros((rc, lc), F32)
            acc_g = jnp.zeros((rc, lc), F32)
            pending = None
            for r0 in range(0, ROWS, rc):
                a = z_ref[pl.ds(r0, rc), c0:c0 + lc]
                sg = _sigmoid(z_ref[pl.ds(r0, rc), aw + c0:aw + c0 + lc])
                hb = a * sg
                dhb = jnp.zeros((rc, lc), F32)
                for k in range(CONV_K):
                    d = extd_ref[pl.ds(r0 + CONV_K - 1 - k, rc), lanes]
                    dhb = dhb + cw_ref[k:k + 1, lanes] * d
                    acc[k] = acc[k] + hb * d
                da = dhb * sg
                dg = da * a * (1.0 - sg)
                acc_a = acc_a + da
                acc_g = acc_g + dg
                if pending is None:
                    pending = (da, dg)
                else:
                    rows = pl.ds(r0 - rc, 2 * rc)
                    dzb_ref[rows, c0:c0 + lc] = jnp.concatenate([pending[0], da], axis=0).astype(BF16)
                    dzb_ref[rows, aw + c0:aw + c0 + lc] = jnp.concatenate([pending[1], dg], axis=0).astype(BF16)
                    pending = None
            for k in range(CONV_K):
                dcw_ref[k:k + 1, lanes] += _colsum(acc[k])
            dbb_ref[:, c0:c0 + lc] += _colsum(acc_a)
            dbb_ref[:, aw + c0:aw + c0 + lc] += _colsum(acc_g)

    return pl.pallas_call(
        body, name=name, grid=(nblk,),
        in_specs=[pl.BlockSpec((ROWS, aw), lambda i: (i, 0)),
                  pl.BlockSpec((HALO, aw), lambda i: (jnp.minimum((i + 1) * hb_per, nblk * hb_per - 1), 0)),
                  pl.BlockSpec((ROWS, 2 * aw), lambda i: (i, 1)),
                  pl.BlockSpec((CONV_K, aw), lambda i: (0, 0))],
        out_specs=[pl.BlockSpec((ROWS, 2 * aw), lambda i: (i, 0)), pl.BlockSpec((1, 2 * aw), lambda i: (0, 0)),
                   pl.BlockSpec((CONV_K, aw), lambda i: (0, 0))],
        out_shape=[jax.ShapeDtypeStruct((s, 2 * aw), BF16), jax.ShapeDtypeStruct((1, 2 * aw), F32),
                   jax.ShapeDtypeStruct((CONV_K, aw), F32)],
        scratch_shapes=[pltpu.VMEM((ROWS + HALO, aw), F32)],
        compiler_params=_params(("arbitrary",)))(dhc, dhc, z, cw)


FFN_ROWS = 1024
FFN_CHUNK = 16


def _ffn_tile(f):
    for t in (256, 128):
        if f % t == 0:
            return t
    raise ValueError(f)


def _taps(ext_ref, w, b, r0, rows, halo):
    acc = b
    for k in range(FFN_K):
        acc = acc + w[k] * ext_ref[pl.ds(halo - (FFN_K - 1) + k + r0, rows), :]
    return acc


UP_HALO = 16
UP_SUB = 128


def ffn_up_mid(h, w_g, w_v, cw, cb, *, name):
    s, d = h.shape
    f = w_g.shape[1]
    tn = _ffn_tile(f)
    nj = f // tn
    ROWS = FFN_ROWS
    per = ROWS // UP_HALO

    def body(h_ref, hp_ref, wgm_ref, wvm_ref, wg_ref, wv_ref, bg_ref, bv_ref,
             act_ref, ug_ref, uv_ref, cg_ref, cv_ref, eg_ref, ev_ref):
        i = pl.program_id(0)
        wg = [wg_ref[k:k + 1, :] for k in range(FFN_K)]
        wv = [wv_ref[k:k + 1, :] for k in range(FFN_K)]
        bg, bv = bg_ref[...], bv_ref[...]
        sides = ((wgm_ref, eg_ref, ug_ref), (wvm_ref, ev_ref, uv_ref))
        tails = [jnp.where(i > 0, _dot(hp_ref[...], wm_ref[...], NN), 0.0) for wm_ref, _, _ in sides]

        def project(sub, tails):
            r0 = sub * UP_SUB
            src = h_ref[pl.ds(r0, UP_SUB), :]
            new_tails = []
            for (wm_ref, ext_ref, up_ref), tail in zip(sides, tails):
                u = _dot(src, wm_ref[...], NN)
                ext_ref[sub % 2, 0:UP_HALO, :] = tail
                ext_ref[sub % 2, UP_HALO:UP_HALO + UP_SUB, :] = u
                up_ref[pl.ds(r0, UP_SUB), :] = u
                new_tails.append(u[UP_SUB - UP_HALO:, :])
            return new_tails

        nsub = ROWS // UP_SUB
        tails = project(0, tails)
        for sub in range(nsub):
            if sub + 1 < nsub:
                tails = project(sub + 1, tails)
            for r0 in range(0, UP_SUB, FFN_CHUNK):
                rows = pl.ds(sub * UP_SUB + r0, FFN_CHUNK)
                gate = _taps(eg_ref.at[sub % 2], wg, bg, r0, FFN_CHUNK, UP_HALO)
                val = _taps(ev_ref.at[sub % 2], wv, bv, r0, FFN_CHUNK, UP_HALO)
                cg_ref[rows, :] = gate
                cv_ref[rows, :] = val
                act_ref[rows, :] = (gate * _sigmoid(gate) * val).astype(BF16)

    blk = pl.BlockSpec((ROWS, tn), lambda i, j: (i, j))
    wm = pl.BlockSpec((d, tn), lambda i, j: (0, j))
    wsp = lambda off: pl.BlockSpec((FFN_K, tn), lambda i, j: (0, j + off))
    bsp = lambda off: pl.BlockSpec((1, tn), lambda i, j: (0, j + off))
    return pl.pallas_call(
        body, name=name, grid=(s // ROWS, nj),
        in_specs=[pl.BlockSpec((ROWS, d), lambda i, j: (i, 0)),
                  pl.BlockSpec((UP_HALO, d), lambda i, j: (jnp.maximum(i * per - 1, 0), 0)),
                  wm, wm, wsp(0), wsp(nj), bsp(0), bsp(nj)],
        out_specs=[blk] * 5,
        out_shape=[jax.ShapeDtypeStruct((s, f), BF16)] + [jax.ShapeDtypeStruct((s, f), F32)] * 4,
        scratch_shapes=[pltpu.VMEM((2, UP_HALO + UP_SUB, tn), F32), pltpu.VMEM((2, UP_HALO + UP_SUB, tn), F32)],
        compiler_params=_params(("parallel", "parallel")))(h, h, w_g, w_v, cw, cw, cb, cb)


def ffn_mid_bwd(dact, up_g, up_v, conv_g, conv_v, cw, *, name):
    s, f = up_g.shape
    tn = _ffn_tile(f)
    nj = f // tn
    ROWS = FFN_ROWS
    nblk = s // ROWS
    per = ROWS // FHALO
    ext = ROWS + FHALO

    def body(da_ref, dan_ref, ug_ref, uv_ref, cg_ref, cv_ref, cgn_ref, cvn_ref, wg_ref, wv_ref,
             dug_ref, duv_ref, dwg_ref, dwv_ref, dbg_ref, dbv_ref, dg_ref, dv_ref):
        i = pl.program_id(1)

        @pl.when(i == 0)
        def _():
            for r in (dwg_ref, dwv_ref, dbg_ref, dbv_ref):
                r[...] = jnp.zeros_like(r)

        wg = [wg_ref[k:k + 1, :] for k in range(FFN_K)]
        wv = [wv_ref[k:k + 1, :] for k in range(FFN_K)]

        for r0, rows in [(r, FFN_CHUNK) for r in range(0, ROWS, FFN_CHUNK)] + [(ROWS, FHALO)]:
            if r0 < ROWS:
                gate, val, da = cg_ref[pl.ds(r0, rows), :], cv_ref[pl.ds(r0, rows), :], da_ref[pl.ds(r0, rows), :]
            else:
                gate, val, da = cgn_ref[...], cvn_ref[...], jnp.where(i < nblk - 1, dan_ref[...], 0.0)
            sg = _sigmoid(gate)
            dg_ref[pl.ds(r0, rows), :] = da * val * (sg * (1.0 + gate * (1.0 - sg)))
            dv_ref[pl.ds(r0, rows), :] = da * (gate * sg)

        def back(d_ref, w, u_ref, du_ref, dw_ref, db_ref):
            zero = jnp.zeros((FFN_CHUNK, tn), F32)
            acc = [zero] * FFN_K
            accb = zero
            for r0 in range(0, ROWS, FFN_CHUNK):
                d = [d_ref[pl.ds(r0 + FFN_K - 1 - k, FFN_CHUNK), :] for k in range(FFN_K)]
                u = u_ref[pl.ds(r0, FFN_CHUNK), :]
                du = w[0] * d[0]
                for k in range(1, FFN_K):
                    du = du + w[k] * d[k]
                du_ref[pl.ds(r0, FFN_CHUNK), :] = du.astype(BF16)
                acc = [acc[k] + u * d[k] for k in range(FFN_K)]
                accb = accb + d[FFN_K - 1]
            for k in range(FFN_K):
                dw_ref[k:k + 1, :] += _colsum(acc[k])
            db_ref[...] += _colsum(accb)

        back(dg_ref, wg, ug_ref, dug_ref, dwg_ref, dbg_ref)
        back(dv_ref, wv, uv_ref, duv_ref, dwv_ref, dbv_ref)

    cur = pl.BlockSpec((ROWS, tn), lambda j, i: (i, j))
    nxt = pl.BlockSpec((FHALO, tn), lambda j, i: (jnp.minimum((i + 1) * per, nblk * per - 1), j))
    wsp = lambda off: pl.BlockSpec((FFN_K, tn), lambda j, i: (0, j + off))
    bsp = pl.BlockSpec((1, tn), lambda j, i: (0, j))
    outs = pl.pallas_call(
        body, name=name, grid=(nj, nblk),
        in_specs=[cur, nxt, cur, cur, cur, cur, nxt, nxt, wsp(0), wsp(nj)],
        out_specs=[cur, cur, wsp(0), wsp(0), bsp, bsp],
        out_shape=[jax.ShapeDtypeStruct((s, f), BF16), jax.ShapeDtypeStruct((s, f), BF16),
                   jax.ShapeDtypeStruct((FFN_K, f), F32), jax.ShapeDtypeStruct((FFN_K, f), F32),
                   jax.ShapeDtypeStruct((1, f), F32), jax.ShapeDtypeStruct((1, f), F32)],
        scratch_shapes=[pltpu.VMEM((ext, tn), F32), pltpu.VMEM((ext, tn), F32)],
        compiler_params=_params(("parallel", "arbitrary")))(dact, dact, up_g, up_v, conv_g, conv_v, conv_g, conv_v,
                                                            cw, cw)
    dug, duv, dwg, dwv, dbg, dbv = outs
    return dug, duv, jnp.concatenate([dwg, dwv], axis=1), jnp.concatenate([dbg, dbv], axis=1)


def rope_tables(s):
    half = HEAD // 2
    lane = jnp.arange(128)
    j = lane % HEAD
    inv = ROPE_THETA ** (-(j % half).astype(F32) / half)
    ang = jnp.arange(s, dtype=F32)[:, None] * inv[None, :]
    sign = jnp.where(j < half, -1.0, 1.0).astype(F32)
    return jnp.cos(ang), jnp.sin(ang) * sign[None, :]


def _swap_halves(x):
    lane = lax.broadcasted_iota(jnp.int32, x.shape, 1)
    return jnp.where((lane % HEAD) < HEAD // 2, pltpu.roll(x, 128 - HEAD // 2, 1), pltpu.roll(x, HEAD // 2, 1))


def qkv_rope(h, w, cos, sin, *, name):
    s, k = h.shape
    d = w.shape[1] // 3
    tm = _divisors(s, (1024, 512))[0]
    scale = HEAD ** -0.5

    def body(a_ref, b_ref, c_ref, s_ref, o_ref):
        third = pl.program_id(1)
        o_ref[...] = _dot(a_ref[...], b_ref[...], NN)

        @pl.when(third < 2)
        def _():
            c = c_ref[...]
            sn = s_ref[...]
            factor = jnp.where(third == 0, scale, 1.0)
            for t in range(d // 128):
                cs = slice(t * 128, (t + 1) * 128)
                x = o_ref[:, cs]
                o_ref[:, cs] = (x * c + _swap_halves(x) * sn) * factor

    tab = pl.BlockSpec((tm, 128), lambda i, j: (i, 0))
    return pl.pallas_call(
        body, name=name, grid=(s // tm, 3),
        in_specs=[pl.BlockSpec((tm, k), lambda i, j: (i, 0)), pl.BlockSpec((k, d), lambda i, j: (0, j)), tab, tab],
        out_specs=pl.BlockSpec((tm, d), lambda i, j: (i, j)),
        out_shape=jax.ShapeDtypeStruct((s, 3 * d), F32),
        compiler_params=_params(("parallel", "parallel")))(h, w, cos, sin)


def rope_bwd(dq, dk, dv, cos, sin, *, name):
    s, d = dq.shape
    scale = HEAD ** -0.5

    def body(dq_ref, dk_ref, dv_ref, c_ref, s_ref, o_ref):
        c = c_ref[...]
        sn = s_ref[...]
        for t in range(d // 128):
            cs = slice(t * 128, (t + 1) * 128)
            gq = dq_ref[:, cs] * scale
            gk = dk_ref[:, cs]
            o_ref[:, t * 128:(t + 1) * 128] = (gq * c + _swap_halves(gq * sn)).astype(BF16)
            o_ref[:, d + t * 128:d + (t + 1) * 128] = (gk * c + _swap_halves(gk * sn)).astype(BF16)
        o_ref[:, 2 * d:3 * d] = dv_ref[...].astype(BF16)

    row = pl.BlockSpec((ROWS, d), lambda i: (i, 0))
    tab = pl.BlockSpec((ROWS, 128), lambda i: (i, 0))
    return pl.pallas_call(
        body, name=name, grid=(s // ROWS,),
        in_specs=[row, row, row, tab, tab],
        out_specs=pl.BlockSpec((ROWS, 3 * d), lambda i: (i, 0)),
        out_shape=jax.ShapeDtypeStruct((s, 3 * d), BF16),
        compiler_params=_params(("parallel",)))(dq, dk, dv, cos, sin)


ATT_T = BLOCK * max(DILATIONS)


FWD_GROUP = 2
ATT_GROUP = 4
FWD_QROWS = 128
BWD_QROWS = 64


def _unit_rows(r, j, dil):
    start = r + dil * BLOCK * j
    return pl.ds(start, BLOCK) if dil == 1 else pl.ds(start, BLOCK, stride=dil)


def _units():
    for bi, dil in enumerate(DILATIONS):
        nsub = ATT_T // (BLOCK * dil)
        for r in range(dil):
            for j in range(nsub):
                yield bi, dil, nsub, r, j


def _band(first_block, part, qrows):
    qi = lax.broadcasted_iota(jnp.int32, (qrows, 2 * BLOCK), 0) + part * qrows
    kj = lax.broadcasted_iota(jnp.int32, (qrows, 2 * BLOCK), 1)
    dist = BLOCK + qi - kj
    band = (dist >= 0) & (dist <= BLOCK)
    return band, band & (jnp.logical_not(first_block) | (kj >= BLOCK))


def _col(tile, h):
    return tile[:, h * HEAD:h * HEAD + 1]


def _keys(cur_ref, prev_ref, r, j, dil, nsub):
    cur = cur_ref[_unit_rows(r, j, dil), :]
    prev = cur_ref[_unit_rows(r, j - 1, dil), :] if j > 0 else prev_ref[_unit_rows(r, nsub - 1, dil), :]
    return jnp.concatenate([prev, cur], axis=0).astype(BF16)


def _att_specs(col_off=0):
    cur = pl.BlockSpec((ATT_T, 128), lambda n, p: (n, p + col_off))
    prv = pl.BlockSpec((ATT_T, 128), lambda n, p: (jnp.maximum(n - 1, 0), p + col_off))
    return cur, prv


def attn_fwd(qkv, *, name):
    s, d = qkv.shape[0], qkv.shape[1] // 3
    nt = s // ATT_T

    def body(q_ref, kc_ref, kp_ref, vc_ref, vp_ref, o_ref, lse_ref, acc_ref, m_ref, l_ref):
        n = pl.program_id(0)
        QROWS = FWD_QROWS
        nparts = BLOCK // QROWS
        bands = [_band(n == 0, part, QROWS) for part in range(nparts)]
        lo = _pair_masks()
        keep = _head_keep(BLOCK)
        nb = len(DILATIONS)
        tile = lambda cols: jnp.where(lo, jnp.concatenate(cols[:nparts], axis=0),
                                      jnp.concatenate(cols[nparts:], axis=0))

        def scores(unit):
            bi, dil, nsub, r, j = unit
            rows = _unit_rows(r, j, dil)
            kw = _keys(kc_ref, kp_ref, r, j, dil, nsub)
            vw = _keys(vc_ref, vp_ref, r, j, dil, nsub)
            qp = q_ref[rows, :].astype(BF16)
            sc2 = _dot(jnp.concatenate([qp * keep[0], qp * keep[1]], axis=0), kw, NT)
            return dict(bi=bi, j=j, rows=rows, vw=vw, sc2=sc2, m_old=m_ref[rows, :] if bi > 0 else None)

        def softmax(u):
            prs, new_m, new_l, alpha = [], [], [], []
            for c in range(2 * nparts):
                h, part = divmod(c, nparts)
                valid = bands[part][0 if u['j'] > 0 else 1]
                sc = jnp.where(valid, u['sc2'][c * QROWS:(c + 1) * QROWS], NEG)
                mx = jnp.max(sc, axis=-1, keepdims=True)
                if u['bi'] == 0:
                    m_new = mx
                else:
                    m_old = _col(u['m_old'][part * QROWS:(part + 1) * QROWS], h)
                    m_new = jnp.maximum(m_old, mx)
                    alpha.append(jnp.exp(m_old - m_new))
                pr = jnp.exp(sc - m_new)
                new_m.append(m_new)
                new_l.append(jnp.sum(pr, axis=-1, keepdims=True))
                prs.append(pr.astype(BF16))
            u.update(pr2=jnp.concatenate(prs, axis=0), new_m=new_m, new_l=new_l, alpha=alpha)

        def combine(u):
            rows, bi = u['rows'], u['bi']
            pv2 = _dot(u['pr2'], u['vw'], NN)
            m_t = tile(u['new_m'])
            l_t = tile(u['new_l'])
            acc_t = jnp.where(lo, pv2[:BLOCK], pv2[BLOCK:])
            if bi > 0:
                a_t = tile(u['alpha'])
                l_t = a_t * l_ref[rows, :] + l_t
                acc_t = a_t * acc_ref[rows, :] + acc_t
            if bi == nb - 1:
                o_ref[rows, :] = acc_t / l_t
                lse_ref[rows, :] = m_t + jnp.log(l_t)
            else:
                acc_ref[rows, :] = acc_t
                m_ref[rows, :] = m_t
                l_ref[rows, :] = l_t

        units = list(_units())
        for first in range(0, len(units), FWD_GROUP):
            pair = [scores(u) for u in units[first:first + FWD_GROUP]]
            for u in pair:
                softmax(u)
            for u in pair:
                combine(u)

    cur, _ = _att_specs()
    kcur, kprv = _att_specs(d // 128)
    vcur, vprv = _att_specs(2 * (d // 128))
    return pl.pallas_call(
        body, name=name, grid=(nt, d // 128), in_specs=[cur, kcur, kprv, vcur, vprv], out_specs=[cur, cur],
        out_shape=[jax.ShapeDtypeStruct((s, d), F32)] * 2,
        scratch_shapes=[pltpu.VMEM((ATT_T, 128), F32)] * 3,
        compiler_params=_params(("parallel", "parallel")))(qkv, qkv, qkv, qkv, qkv)


def attn_delta(do, o, *, name):
    s, d = do.shape

    def body(do_ref, o_ref, dl_ref):
        lane = lax.broadcasted_iota(jnp.int32, (ROWS, 128), 1)
        lo = lane < HEAD
        for p in range(d // 128):
            cs = slice(p * 128, (p + 1) * 128)
            pr = do_ref[:, cs] * o_ref[:, cs]
            s0 = jnp.sum(jnp.where(lo, pr, 0.0), axis=-1, keepdims=True)
            s1 = jnp.sum(jnp.where(lo, 0.0, pr), axis=-1, keepdims=True)
            dl_ref[:, cs] = jnp.where(lo, s0, s1)

    row = pl.BlockSpec((ROWS, d), lambda i: (i, 0))
    return pl.pallas_call(
        body, name=name, grid=(s // ROWS,), in_specs=[row, row], out_specs=row,
        out_shape=jax.ShapeDtypeStruct((s, d), F32),
        compiler_params=_params(("parallel",)))(do, o)


def attn_dq(qkv, do, lse, delta, *, name):
    s, d = do.shape
    nt = s // ATT_T

    def body(q_ref, kc_ref, kp_ref, vc_ref, vp_ref, do_ref, l_ref, dl_ref, dq_ref):
        n = pl.program_id(0)
        QROWS = BWD_QROWS
        nparts = BLOCK // QROWS
        bands = [_band(n == 0, part, QROWS) for part in range(nparts)]
        lo = _pair_masks()
        keep = _head_keep(BLOCK)
        def scores(unit):
            bi, dil, nsub, r, j = unit
            rows = _unit_rows(r, j, dil)
            kw = _keys(kc_ref, kp_ref, r, j, dil, nsub)
            vw = _keys(vc_ref, vp_ref, r, j, dil, nsub)
            qp = q_ref[rows, :].astype(BF16)
            dop = do_ref[rows, :].astype(BF16)
            sc2 = _dot(jnp.concatenate([qp * keep[0], qp * keep[1]], axis=0), kw, NT)
            dp2 = _dot(jnp.concatenate([dop * keep[0], dop * keep[1]], axis=0), vw, NT)
            return dict(bi=bi, j=j, rows=rows, kw=kw, sc2=sc2, dp2=dp2, lt=l_ref[rows, :], dt=dl_ref[rows, :])

        def softmax_bwd(u):
            dss = []
            for c in range(2 * nparts):
                h, part = divmod(c, nparts)
                valid = bands[part][0 if u['j'] > 0 else 1]
                cr = slice(c * QROWS, (c + 1) * QROWS)
                pr_rows = slice(part * QROWS, (part + 1) * QROWS)
                pr = jnp.where(valid, jnp.exp(u['sc2'][cr] - _col(u['lt'][pr_rows], h)), 0.0)
                dss.append((pr * (u['dp2'][cr] - _col(u['dt'][pr_rows], h))).astype(BF16))
            u['ds2'] = jnp.concatenate(dss, axis=0)

        def combine(u):
            rows = u['rows']
            dq2 = _dot(u['ds2'], u['kw'], NN)
            dq_t = jnp.where(lo, dq2[:BLOCK], dq2[BLOCK:])
            if u['bi'] > 0:
                dq_t = dq_t + dq_ref[rows, :]
            dq_ref[rows, :] = dq_t

        units = list(_units())
        for first in range(0, len(units), ATT_GROUP):
            pair = [scores(u) for u in units[first:first + ATT_GROUP]]
            for u in pair:
                softmax_bwd(u)
            for u in pair:
                combine(u)

    cur, _ = _att_specs()
    kcur, kprv = _att_specs(d // 128)
    vcur, vprv = _att_specs(2 * (d // 128))
    return pl.pallas_call(
        body, name=name, grid=(nt, d // 128), in_specs=[cur, kcur, kprv, vcur, vprv, cur, cur, cur], out_specs=cur,
        out_shape=jax.ShapeDtypeStruct((s, d), F32),
        compiler_params=_params(("parallel", "parallel")))(qkv, qkv, qkv, qkv, qkv, do, lse, delta)


def attn_dkv(qkv, do, lse, delta, *, name):
    s, d = do.shape
    nt = s // ATT_T

    def body(k_ref, v_ref, qc_ref, qn_ref, doc_ref, don_ref, lc_ref, ln_ref, dc_ref, dn_ref, dk_ref, dv_ref):
        n = pl.program_id(0)
        qi = lax.broadcasted_iota(jnp.int32, (BLOCK, BLOCK), 0)
        kj = lax.broadcasted_iota(jnp.int32, (BLOCK, BLOCK), 1)
        own = kj <= qi
        nxt = kj >= qi
        nxt_edge = nxt & (n < nt - 1)
        keep = _head_keep(BLOCK)

        def scores(unit):
            bi, dil, nsub, r, j = unit
            rows = _unit_rows(r, j, dil)
            inner = j + 1 < nsub
            nrows = _unit_rows(r, j + 1, dil) if inner else _unit_rows(r, 0, dil)
            kp = k_ref[rows, :].astype(BF16)
            vp = v_ref[rows, :].astype(BF16)
            far = not inner
            take = lambda c_ref, n_ref, nx: ((n_ref if far else c_ref)[nrows, :] if nx else c_ref[rows, :])
            qs = [take(qc_ref, qn_ref, nx).astype(BF16) for nx in (False, True)]
            dos = [take(doc_ref, don_ref, nx).astype(BF16) for nx in (False, True)]
            lts = [take(lc_ref, ln_ref, nx) for nx in (False, True)]
            dts = [take(dc_ref, dn_ref, nx) for nx in (False, True)]
            q4 = jnp.concatenate([qs[nx] * keep[h] for h in range(2) for nx in range(2)], axis=0)
            do4 = jnp.concatenate([dos[nx] * keep[h] for h in range(2) for nx in range(2)], axis=0)
            return dict(bi=bi, rows=rows, q4=q4, do4=do4, s4=_dot(q4, kp, NT), dp4=_dot(do4, vp, NT), lts=lts,
                        dts=dts, valids=(own, nxt if inner else nxt_edge))

        def softmax_bwd(u):
            prs, dss = [], []
            for c in range(4):
                h, nx = divmod(c, 2)
                cr = slice(c * BLOCK, (c + 1) * BLOCK)
                pr = jnp.where(u['valids'][nx], jnp.exp(u['s4'][cr] - _col(u['lts'][nx], h)), 0.0)
                prs.append(pr.astype(BF16))
                dss.append((pr * (u['dp4'][cr] - _col(u['dts'][nx], h))).astype(BF16))
            u.update(pr4=jnp.concatenate(prs, axis=0), ds4=jnp.concatenate(dss, axis=0))

        def combine(u):
            rows = u['rows']
            dv_t = _dot(u['pr4'], u['do4'], TN)
            dk_t = _dot(u['ds4'], u['q4'], TN)
            if u['bi'] > 0:
                dk_t = dk_t + dk_ref[rows, :]
                dv_t = dv_t + dv_ref[rows, :]
            dk_ref[rows, :] = dk_t
            dv_ref[rows, :] = dv_t

        units = list(_units())
        for first in range(0, len(units), ATT_GROUP):
            pair = [scores(u) for u in units[first:first + ATT_GROUP]]
            for u in pair:
                softmax_bwd(u)
            for u in pair:
                combine(u)

    cur = pl.BlockSpec((ATT_T, 128), lambda n, p: (n, p))
    nxt_spec = pl.BlockSpec((ATT_T, 128), lambda n, p: (jnp.minimum(n + 1, nt - 1), p))
    kcur, _ = _att_specs(d // 128)
    vcur, _ = _att_specs(2 * (d // 128))
    return pl.pallas_call(
        body, name=name, grid=(nt, d // 128),
        in_specs=[kcur, vcur, cur, nxt_spec, cur, nxt_spec, cur, nxt_spec, cur, nxt_spec], out_specs=[cur, cur],
        out_shape=[jax.ShapeDtypeStruct((s, d), F32)] * 2,
        compiler_params=_params(("parallel", "parallel")))(qkv, qkv, qkv, qkv, do, do, lse, lse, delta, delta)


def adamw(parts_list, w, m, v, *, name):
    nk = len(parts_list)
    npart, rk, c = parts_list[0].shape
    r = rk * nk
    assert w.shape == (r, c)
    tr = next(t for t in range(rk, 0, -8) if rk % t == 0 and (t * c * 4 <= 1024 * 1024 or t == 8))
    nbk = rk // tr

    def body(*refs):
        p_refs = refs[:nk]
        w_ref, m_ref, v_ref, g_ref, d_ref, nm_ref, nv_ref = refs[nk:]
        i = pl.program_id(0)
        g = None
        for kk, p_ref in enumerate(p_refs):
            gk = p_ref[0].astype(F32)
            for j in range(1, npart):
                gk = gk + p_ref[j].astype(F32)
            g = gk if g is None else jnp.where(i >= kk * nbk, gk, g)
        m2 = B1 * m_ref[...] + (1.0 - B1) * g
        v2 = B2 * v_ref[...] + (1.0 - B2) * (g * g)
        m_hat = m2 / (1.0 - B1 ** STEP)
        v_hat = v2 / (1.0 - B2 ** STEP)
        g_ref[...] = g
        d_ref[...] = -LR * (m_hat / (jnp.sqrt(v_hat) + ADAM_EPS) + WD * w_ref[...])
        nm_ref[...] = m2
        nv_ref[...] = v2

    blk = pl.BlockSpec((tr, c), lambda i: (i, 0))
    pspec = lambda kk: pl.BlockSpec((npart, tr, c), lambda i: (0, jnp.clip(i - kk * nbk, 0, nbk - 1), 0))
    return pl.pallas_call(
        body, name=name, grid=(r // tr,),
        in_specs=[pspec(kk) for kk in range(nk)] + [blk, blk, blk],
        out_specs=[blk] * 4, out_shape=[jax.ShapeDtypeStruct((r, c), F32)] * 4,
        compiler_params=_params(("parallel",)))(*parts_list, w, m, v)


def _my_index():
    return 4 * lax.axis_index("x") + 2 * lax.axis_index("y") + lax.axis_index("c")


def _peer_of(kk):
    x, y, c = lax.axis_index("x"), lax.axis_index("y"), lax.axis_index("c")
    return x ^ (kk >> 2), y ^ ((kk >> 1) & 1), c ^ (kk & 1)


def _peer_copy(t, kk, scatter, ins, lands, send_sems, recv_sems):
    px, py, pc = _peer_of(kk)
    me = _my_index()
    src = ins[t].at[4 * px + 2 * py + pc] if scatter[t] else ins[t]
    return pltpu.make_async_remote_copy(
        src_ref=src, dst_ref=lands[t].at[me], send_sem=send_sems.at[t * N_DEV + kk],
        recv_sem=recv_sems.at[t * N_DEV + kk], device_id=(px, py, pc), device_id_type=pl.DeviceIdType.MESH)


def _own_copy(t, scatter, ins, lands, own_sems):
    me = _my_index()
    return pltpu.make_async_copy(ins[t].at[me] if scatter[t] else ins[t], lands[t].at[me], own_sems.at[t])


_HBM = pl.BlockSpec(memory_space=pltpu.HBM)
_SEM = pl.BlockSpec(memory_space=pltpu.SEMAPHORE)
_EFFECT = pltpu.SideEffectType.DATAFLOW_SIDE_EFFECTING


def exchange_start(arrays, scatter, *, name):
    nt = len(arrays)
    land_shapes = [a.shape if scatter[t] else (N_DEV,) + a.shape for t, a in enumerate(arrays)]

    def body(*refs):
        ins, lands = refs[:nt], refs[nt:2 * nt]
        send_sems, recv_sems, own_sems = refs[2 * nt:2 * nt + 3]
        token = refs[-1]
        for kk in range(1, N_DEV):
            for t in range(nt):
                _peer_copy(t, kk, scatter, ins, lands, send_sems, recv_sems).start()
        for t in range(nt):
            _own_copy(t, scatter, ins, lands, own_sems).start()
        token[...] = jnp.zeros_like(token)

    sems = pltpu.SemaphoreType.DMA((nt * N_DEV,))
    outs = pl.pallas_call(
        body, name=name,
        out_shape=(sems, sems, pltpu.SemaphoreType.DMA((nt,)), *[pltpu.HBM(a.shape, a.dtype) for a in arrays],
                   *[pltpu.HBM(shp, a.dtype) for shp, a in zip(land_shapes, arrays)],
                   jax.ShapeDtypeStruct((8, 128), F32)),
        in_specs=[_HBM] * (2 * nt),
        out_specs=(_SEM, _SEM, _SEM, *[_HBM] * (2 * nt), pl.BlockSpec(memory_space=pltpu.VMEM)),
        input_output_aliases={i: 3 + i for i in range(2 * nt)},
        compiler_params=pltpu.CompilerParams(has_side_effects=_EFFECT),
    )(*[pltpu.with_memory_space_constraint(a, pltpu.HBM) for a in arrays],
      *[pltpu.with_memory_space_constraint(lax.empty(shp, a.dtype), pltpu.HBM) for shp, a in zip(land_shapes, arrays)])
    return (outs[:3], outs[3:3 + nt], outs[3 + nt:3 + 2 * nt], scatter), outs[-1]


def exchange_wait(handle, after, *, name):
    sems, thru, lands, scatter = handle
    nt = len(thru)

    def body(*refs):
        ins, lnd = refs[:nt], refs[nt:2 * nt]
        s_sems, r_sems, o_sems = refs[2 * nt:2 * nt + 3]
        for kk in range(1, N_DEV):
            for t in range(nt):
                cp = _peer_copy(t, kk, scatter, ins, lnd, s_sems, r_sems)
                cp.wait_send()
                cp.wait_recv()
        for t in range(nt):
            _own_copy(t, scatter, ins, lnd, o_sems).wait()

    outs = pl.pallas_call(
        body, name=name,
        out_shape=(*[pltpu.HBM(a.shape, a.dtype) for a in thru], *[pltpu.HBM(a.shape, a.dtype) for a in lands]),
        in_specs=[_HBM] * (2 * nt) + [_SEM, _SEM, _SEM, pl.BlockSpec(memory_space=pl.ANY)],
        out_specs=tuple([_HBM] * (2 * nt)),
        input_output_aliases={i: i for i in range(2 * nt)},
        compiler_params=pltpu.CompilerParams(has_side_effects=_EFFECT),
    )(*thru, *lands, *sems, after)
    return outs[nt:]


def _cols_from_shards(g):
    g = jnp.moveaxis(g, 0, -2)
    return g.reshape(g.shape[:-2] + (g.shape[-2] * g.shape[-1],))


def _cols_to_shards(w, nshards=N_DEV):
    w = w.reshape(w.shape[:-1] + (nshards, w.shape[-1] // nshards))
    return jnp.moveaxis(w, -2, 0)


def _half_shards(halves):
    return jnp.concatenate([_cols_to_shards(h[None], N_DEV // 2) for h in halves], axis=0)


def _ffn_fwd(x, h, w_up, cw, cb, get_w_down, next_g, tag):
    act, up_g, up_v, conv_g, conv_v = ffn_up_mid(h, w_up[0], w_up[1], cw, cb, name=f"{tag}_up_mid")
    w_down = get_w_down(act)
    out = matmul(act, w_down, res=x, norm_g=next_g, name=f"{tag}_down")
    return out, (h, up_g, up_v, conv_g, conv_v, act), w_down


def _ffn_bwd(dx, dxb, x, saved, g, w_up, cw, w_down, tag):
    h, up_g, up_v, conv_g, conv_v, act = saved
    dact = matmul(dxb, w_down, tb=True, name=f"{tag}_ddown")
    d_w_down = matmul_ta(act, dxb, name=f"{tag}_gdown")
    dug, duv, dcw, dcb = ffn_mid_bwd(dact, up_g, up_v, conv_g, conv_v, cw, name=f"{tag}_dmid")
    d_w_up = (matmul_ta(h, dug, name=f"{tag}_gup_g"), matmul_ta(h, duv, name=f"{tag}_gup_v"))
    dh = matmul(dug, w_up[0], tb=True, name=f"{tag}_dup_g")
    dx2, dxb2, dg = matmul_norm_bwd(duv, w_up[1], x, g, dx, res=dh, name=f"{tag}_dup_v_dnorm")
    return dx2, dxb2, dict(norm_g=dg, w_up=d_w_up, conv_w=dcw, conv_b=dcb, w_down=d_w_down)


def local_step(x0, tgt, a, weights, grads_out):
    s, d = x0.shape
    aw = a['even_v_ln_g'].shape[-1]
    causal = jnp.tril(jnp.ones((CHUNK, CHUNK), dtype=bool))
    wm = jnp.where(causal, a['even_w_s'][0], 0.0).astype(BF16)
    wmt = jnp.swapaxes(wm, 1, 2)
    bm = jnp.repeat(a['even_b_s'][0].T, HEAD, axis=1)
    sel = (jnp.arange(aw)[:, None] // HEAD == jnp.arange(128)[None, :]).astype(BF16)
    cos, sin = rope_tables(s)
    ffn_g, ffn_cb = a['ffn_norm_g'], a['ffn_conv_b']

    w0 = weights(0, None)
    w_in, conv_w, odd_g, ffn_cw = w0['w_in'], w0['conv_w'], w0['odd_g'], w0['ffn_cw']
    h0 = rms_fwd(x0, w0['even_g'], name="even_norm")
    z = matmul(h0, w_in, bias=a['even_b_in'], name="even_in")
    ycat, hc = even_mid_fwd(z, a['even_v_ln_g'], a['even_v_ln_b'], wm, bm, conv_w, a['even_conv_b'],
                            a['even_conv_ln_g'], a['even_conv_ln_b'], name="even_mid")
    w1 = weights(1, ycat)
    x1, h1 = matmul(ycat, w1['w_out'], res=x0, norm_g=ffn_g[0:1], name="even_out")
    (x2, h2), ffn0, w_down0 = _ffn_fwd(x1, h1, w1['w_up'], ffn_cw[0], ffn_cb[0:1],
                                       lambda act: weights(2, act)['w_down'], odd_g, "ffn0")
    w2 = weights(3, h2)
    qkv = qkv_rope(h2, w2['w_qkv'], cos, sin, name="odd_qkv_rope")
    o, lse = attn_fwd(qkv, name="attn_fwd")
    x3, h3 = matmul(o, w2['w_o'], res=x2, norm_g=ffn_g[1:2], name="odd_out")
    w3 = weights(4, x3)
    x4, ffn1, _ = _ffn_fwd(x3, h3, w3['w_up'], ffn_cw[1], ffn_cb[1:2], lambda act: w3['w_down'], None, "ffn1")
    loss_t, dx, dxb, d_final_g = final_loss_bwd(x4, a['final_norm_g'].reshape(1, -1), tgt, name="final_loss")

    dx, dxb, g1 = _ffn_bwd(dx, dxb, x3, ffn1, ffn_g[1:2], w3['w_up'], ffn_cw[1], w3['w_down'], "ffn1")
    dep = grads_out(0, dict(w_up=g1['w_up'], w_down=g1['w_down']))
    do = matmul(dxb, w2['w_o'], tb=True, dep=dep, name="odd_dout")
    d_w_o = matmul_ta(o, dxb, name="odd_gout")
    delta = attn_delta(do, o, name="attn_delta")
    dq = attn_dq(qkv, do, lse, delta, name="attn_dq")
    dk, dv = attn_dkv(qkv, do, lse, delta, name="attn_dkv")
    dqkv = rope_bwd(dq, dk, dv, cos, sin, name="rope_bwd")
    d_w_qkv = matmul_ta(h2, dqkv, name="odd_gqkv")
    dep = grads_out(1, dict(w_qkv=d_w_qkv, w_o=d_w_o))
    dx, dxb, d_odd_g = matmul_norm_bwd(dqkv, w2['w_qkv'], x2, odd_g, dx, dep=dep, name="odd_dqkv_dnorm")
    dx, dxb, g0 = _ffn_bwd(dx, dxb, x1, ffn0, ffn_g[0:1], w1['w_up'], ffn_cw[0], w_down0, "ffn0")
    dep = grads_out(2, dict(w_up=g0['w_up'], w_down=g0['w_down']))
    d_w_out = matmul_ta(ycat, dxb, dep=dep, name="even_gout")
    dep = grads_out(3, dict(w_out=d_w_out))
    dycat = matmul(dxb, w1['w_out'], tb=True, dep=dep, name="even_dout")
    (dza, dhc, dba, dvg, dvb, dwm, dbs, dcg, dcbeta, dcb) = even_mid_bwd_rows(
        dycat, z, hc, a['even_v_ln_g'], a['even_v_ln_b'], wm, wmt, bm, sel, a['even_conv_ln_g'],
        a['even_conv_ln_b'], name="even_dmid_rows")
    dzb, dbb, dcw = even_conv_bwd(dhc, z, conv_w, name="even_dmid_conv")
    nh = a['even_w_s'].shape[1]
    small_grads = {
        'even_b_in': jnp.concatenate([dba, dbb], axis=1), 'even_v_ln_g': dvg,
        'even_v_ln_b': dvb, 'even_w_s': jnp.where(causal, dwm, 0.0)[None], 'even_b_s': dbs[:, :nh].T[None],
        'even_conv_w': dcw[None], 'even_conv_b': dcb, 'even_conv_ln_g': dcg, 'even_conv_ln_b': dcbeta,
        'odd_norm_g': d_odd_g, 'ffn_norm_g': jnp.concatenate([g0['norm_g'], g1['norm_g']], axis=0),
        'ffn_conv_w': jnp.stack([g0['conv_w'], g1['conv_w']]),
        'ffn_conv_b': jnp.concatenate([g0['conv_b'], g1['conv_b']], axis=0),
        'final_norm_g': d_final_g.reshape(-1),
    }
    dep = grads_out(5, dict(small=small_grads))
    d_w_in = (matmul_ta(h0, dza, dep=dep, name="even_gin_a"), matmul_ta(h0, dzb, name="even_gin_b"))
    dep = grads_out(4, dict(w_in=d_w_in))
    dh0 = matmul(dza, w_in, tb=True, bk=0, dep=dep, name="even_din_a")
    grad_x, _, d_even_g = matmul_norm_bwd(dzb, w_in, x0, w0['even_g'], dx, bk=1, res=dh0, name="even_din_b_dnorm")
    last = {'even_norm_g': d_even_g}
    grads_out(6, dict(small=last))
    return loss_t, grad_x, {**last, **small_grads}


BIG = ['even_w_in', 'even_w_out', 'odd_w_qkv', 'odd_w_o', 'ffn_w_up', 'ffn_w_down']


def _as_tiles(flat, dtype=F32):
    return jnp.pad(flat, (0, (-flat.size) % 2048)).reshape(-1, 128).astype(dtype)


def kernel(*args):
    a = dict(zip(NAMES + ['loss_target'] + ['m_' + n for n in WEIGHTS] + ['v_' + n for n in WEIGHTS], args))
    x0 = a['x'][0]
    tgt = a['loss_target'][0]
    s, d = x0.shape
    me = _my_index()
    bf = lambda t: t.astype(BF16)

    small_local = _as_tiles(jnp.concatenate([a['even_conv_w'].reshape(-1), a['odd_norm_g'].reshape(-1),
                                             a['ffn_conv_w'].reshape(-1)]))
    stage_arrays = [
        [bf(a['even_w_in']), small_local],
        [bf(a['even_w_out']), bf(a['ffn_w_up'][0:1])],
        [bf(a['ffn_w_down'][0:1])],
        [bf(a['odd_w_qkv']), bf(a['odd_w_o'])],
        [bf(a['ffn_w_up'][1:2]), bf(a['ffn_w_down'][1:2])],
    ]
    started = [exchange_start(arrs, [False] * len(arrs), name=f"gather{i}_start") for i, arrs in enumerate(stage_arrays)]
    order = sum(tok[0, 0] for _, tok in started)

    def weights(stage, after):
        handle, tok = started[stage]
        full = exchange_wait(handle, tok if after is None else after, name=f"gather{stage}_wait")
        rows = lambda g: jnp.moveaxis(g, 0, 1).reshape(-1, d)
        halves = lambda g: (_cols_from_shards(g[:N_DEV // 2])[0], _cols_from_shards(g[N_DEV // 2:])[0])
        if stage == 0:
            gs = full[1].reshape(N_DEV, -1)
            n_cw, n_og, n_fw = a['even_conv_w'].size, a['odd_norm_g'].size, a['ffn_conv_w'].size
            return dict(
                w_in=_cols_from_shards(full[0])[0], even_g=a['even_norm_g'] + order,
                conv_w=_cols_from_shards(gs[:, :n_cw].reshape((N_DEV,) + a['even_conv_w'].shape))[0],
                odd_g=gs[:, n_cw:n_cw + n_og].reshape(1, -1),
                ffn_cw=_cols_from_shards(gs[:, n_cw + n_og:n_cw + n_og + n_fw].reshape((N_DEV,) + a['ffn_conv_w'].shape)))
        if stage == 1:
            return dict(w_out=rows(full[0]), w_up=halves(full[1]))
        if stage == 2:
            return dict(w_down=rows(full[0]))
        if stage == 3:
            return dict(w_qkv=_cols_from_shards(full[0])[0], w_o=rows(full[1]))
        return dict(w_up=halves(full[0]), w_down=rows(full[1]))

    sent = {}

    def grads_out(stage, g):
        to_rows = lambda w: w.reshape(N_DEV, 1, -1, d)
        if stage in (0, 2):
            pieces, scatter = [_half_shards(g['w_up']), to_rows(g['w_down'])], [True, True]
        elif stage == 1:
            pieces, scatter = [_cols_to_shards(g['w_qkv'][None]), to_rows(g['w_o'])], [True, True]
        elif stage == 3:
            pieces, scatter = [to_rows(g['w_out'])], [True]
        elif stage == 4:
            pieces, scatter = [_half_shards(g['w_in'])], [True]
        else:
            small = jnp.concatenate([g['small'][n].reshape(-1) for n in g['small']])
            pieces, scatter = [_as_tiles(small, BF16)], [False]
        sent[stage], tok = exchange_start(pieces, scatter, name=f"grads{stage}_start")
        return tok

    loss_t, grad_x, small_grads = local_step(x0, tgt, a, weights, grads_out)
    loss = lax.psum(loss_t[0, 0], ("x", "y", "c"))
    received = {stage: exchange_wait(handle, grad_x, name=f"grads{stage}_wait") for stage, handle in sent.items()}

    results = {}
    big_parts = {'even_w_in': [received[4][0]], 'even_w_out': [received[3][0]], 'odd_w_qkv': [received[1][0]],
                 'odd_w_o': [received[1][1]], 'ffn_w_up': [received[2][0], received[0][0]],
                 'ffn_w_down': [received[2][1], received[0][1]]}
    for n in BIG:
        shp = a[n].shape
        flat = lambda t: t.reshape(-1, shp[-1])
        outs = adamw([p.reshape(N_DEV, -1, shp[-1]) for p in big_parts[n]], flat(a[n]), flat(a['m_' + n]),
                     flat(a['v_' + n]), name=f"adamw_{n}")
        results[n] = [t.reshape(shp) for t in outs]

    small_names = list(small_grads)
    n_small = sum(small_grads[n].size for n in small_names)
    n_last = small_grads[small_names[0]].size
    rs = jnp.concatenate([received[6][0].reshape(N_DEV, -1)[:, :n_last],
                          received[5][0].reshape(N_DEV, -1)[:, :n_small - n_last]], axis=1)
    parts, offs = [], 0
    for n in small_names:
        full = small_grads[n].shape
        piece = rs[:, offs:offs + small_grads[n].size].reshape((N_DEV,) + full)
        offs += small_grads[n].size
        shp = a[n].shape
        if shp != full:
            width = shp[-1]
            piece = lax.dynamic_slice_in_dim(piece, me * width, width, axis=piece.ndim - 1)
        parts.append(piece.reshape(N_DEV, -1))
    parts = jnp.concatenate(parts, axis=1)
    pad = (-parts.shape[1]) % 2048
    cat = lambda pre: _as_tiles(jnp.concatenate([a[pre + n].reshape(-1) for n in small_names]))
    outs = adamw([jnp.pad(parts, ((0, 0), (0, pad))).reshape(N_DEV, -1, 128)], cat(''), cat('m_'), cat('v_'),
                 name="adamw_small")
    offs = 0
    for n in small_names:
        size = a[n].size
        results[n] = [t.reshape(-1)[offs:offs + size].reshape(a[n].shape) for t in outs]
        offs += size

    out = [loss, grad_x[None]]
    for i in range(4):
        out += [results[n][i] for n in WEIGHTS]
    return tuple(out)
```

```python
import math

import jax
import jax.numpy as jnp
from jax import lax
from jax.experimental import pallas as pl
from jax.experimental.pallas import tpu as pltpu

F32 = jnp.float32
BF16 = jnp.bfloat16

N_DEV = 8
EPS = 1e-6
NEG = -1e30
HEAD = 64
CHUNK = 128
BLOCK = 128
CONV_K = 31
FFN_K = 3
DILATIONS = (1, 4, 16)
ROPE_THETA = 10000.0
LR, B1, B2, ADAM_EPS, WD, STEP = 0.001, 0.9, 0.999, 1e-08, 0.01, 10

VMEM_LIMIT = 56 * 1024 * 1024
VMEM_BUDGET = 32 * 1024 * 1024
ROWS = 512
HALO = 32
FHALO = 8

NAMES = ['x', 'even_norm_g', 'even_w_in', 'even_b_in', 'even_v_ln_g', 'even_v_ln_b', 'even_w_s', 'even_b_s',
         'even_conv_w', 'even_conv_b', 'even_conv_ln_g', 'even_conv_ln_b', 'even_w_out', 'odd_norm_g',
         'odd_w_qkv', 'odd_w_o', 'ffn_norm_g', 'ffn_w_up', 'ffn_conv_w', 'ffn_conv_b', 'ffn_w_down',
         'final_norm_g']
WEIGHTS = NAMES[1:]


def _params(sem=None):
    return pltpu.CompilerParams(dimension_semantics=sem, vmem_limit_bytes=VMEM_LIMIT)


def _sigmoid(x):
    return 1.0 / (1.0 + jnp.exp(-x))


def _gelu(x):
    c = math.sqrt(2.0 / math.pi)
    return 0.5 * x * (1.0 + jnp.tanh(c * (x + 0.044715 * x * x * x)))


def _gelu_grad(x):
    c = math.sqrt(2.0 / math.pi)
    t = jnp.tanh(c * (x + 0.044715 * x * x * x))
    return 0.5 * (1.0 + t) + 0.5 * x * (1.0 - t * t) * c * (1.0 + 3.0 * 0.044715 * x * x)


def _ln_stats(x):
    mu = jnp.mean(x, axis=-1, keepdims=True)
    xc = x - mu
    rstd = lax.rsqrt(jnp.mean(xc * xc, axis=-1, keepdims=True) + EPS)
    return xc * rstd, rstd


def _ln_bwd(dy, xhat, rstd, g):
    dxh = dy * g
    return rstd * (dxh - jnp.mean(dxh, axis=-1, keepdims=True) - xhat * jnp.mean(dxh * xhat, axis=-1, keepdims=True))


def _colsum(x):
    return jnp.sum(x, axis=0, keepdims=True)


def _split3(x):
    hi = x.astype(BF16)
    r = x - hi.astype(F32)
    mid = r.astype(BF16)
    lo = (r - mid.astype(F32)).astype(BF16)
    return hi, mid, lo


def _dot(a, b, dims):
    return lax.dot_general(a, b, (dims, ((), ())), preferred_element_type=F32)


NN = ((1,), (0,))
NT = ((1,), (1,))
TN = ((0,), (0,))


def _divisors(n, cands):
    return [c for c in cands if c <= n and n % c == 0]


def _pick_tiles(m, n, k, a_bytes, b_bytes, o_bytes, extra_bytes):
    best = None
    for tm in _divisors(m, (1024, 512, 256, 128)):
        for tn in _divisors(n, (1408, 1024, 768, 704, 512, 384, 256, 128)):
            if tn % 128:
                continue
            need = 2 * (tm * k * a_bytes + k * tn * b_bytes + tm * tn * (o_bytes + extra_bytes)) + tm * tn * 4
            if need <= VMEM_BUDGET and (best is None or tm * tn > best[0] * best[1]):
                best = (tm, tn)
    assert best is not None, (m, n, k)
    return best


def matmul(a, b, *, tb=False, bk=0, bias=None, res=None, norm_g=None, dep=None, out_dtype=F32, name):
    m, k = a.shape
    n = b.shape[0] if tb else b.shape[1]
    assert b.shape[1] % k == 0 if tb else (b.shape[0] == k and bk == 0)
    if norm_g is None:
        tm, tn = _pick_tiles(m, n, k, a.dtype.itemsize, b.dtype.itemsize, jnp.dtype(out_dtype).itemsize,
                             4 if res is not None else 0)
    else:
        tm, tn = _divisors(m, (512,))[0], n

    def body(*refs):
        a_ref, b_ref = refs[:2]
        o_ref = refs[-2] if norm_g is not None else refs[-1]
        acc = _dot(a_ref[...].astype(BF16), b_ref[...].astype(BF16), NT if tb else NN)
        pos = 2
        if bias is not None:
            acc = acc + refs[pos][...]
            pos += 1
        if res is not None:
            acc = acc + refs[pos][...]
            pos += 1
        o_ref[...] = acc.astype(out_dtype)
        if norm_g is not None:
            r = lax.rsqrt(jnp.mean(acc * acc, axis=-1, keepdims=True) + EPS)
            refs[-1][...] = (acc * r * refs[pos][...]).astype(BF16)

    in_specs = [pl.BlockSpec((tm, k), lambda i, j: (i, 0)),
                pl.BlockSpec((tn, k), lambda i, j: (j, bk)) if tb else pl.BlockSpec((k, tn), lambda i, j: (0, j))]
    args = [a, b]
    if bias is not None:
        in_specs.append(pl.BlockSpec((1, tn), lambda i, j: (0, j)))
        args.append(bias)
    if res is not None:
        in_specs.append(pl.BlockSpec((tm, tn), lambda i, j: (i, j)))
        args.append(res)
    if norm_g is not None:
        in_specs.append(pl.BlockSpec((1, tn), lambda i, j: (0, j)))
        args.append(norm_g)
    if dep is not None:
        in_specs.append(pl.BlockSpec(memory_space=pl.ANY))
        args.append(dep)
    blk = pl.BlockSpec((tm, tn), lambda i, j: (i, j))
    out_shape = jax.ShapeDtypeStruct((m, n), out_dtype)
    return pl.pallas_call(
        body, name=name, grid=(m // tm, n // tn), in_specs=in_specs,
        out_specs=blk if norm_g is None else [blk, blk],
        out_shape=out_shape if norm_g is None else [out_shape, jax.ShapeDtypeStruct((m, n), BF16)],
        compiler_params=_params(("parallel", "parallel")))(*args)


def matmul_norm_bwd(a, b, x, g, dres, *, bk=0, res=None, dep=None, name):
    m, k = a.shape
    n = b.shape[0]
    tm = _divisors(m, (512,))[0]

    def body(*refs):
        a_ref, b_ref, x_ref, g_ref, dres_ref = refs[:5]
        dx_ref, dxb_ref, dg_ref = refs[-3:]
        dhv = _dot(a_ref[...].astype(BF16), b_ref[...].astype(BF16), NT)
        if res is not None:
            dhv = dhv + refs[5][...]
        xv = x_ref[...]
        r = lax.rsqrt(jnp.mean(xv * xv, axis=-1, keepdims=True) + EPS)
        xhat = xv * r
        dxh = dhv * g_ref[...]
        dx = dres_ref[...] + r * (dxh - xhat * jnp.mean(dxh * xhat, axis=-1, keepdims=True))
        dx_ref[...] = dx
        dxb_ref[...] = dx.astype(BF16)

        @pl.when(pl.program_id(0) == 0)
        def _():
            dg_ref[...] = jnp.zeros_like(dg_ref)
        dg_ref[...] += _colsum(dhv * xhat)

    row = pl.BlockSpec((tm, n), lambda i: (i, 0))
    vec = pl.BlockSpec((1, n), lambda i: (0, 0))
    in_specs = [pl.BlockSpec((tm, k), lambda i: (i, 0)), pl.BlockSpec((n, k), lambda i: (0, bk)), row, vec, row]
    args = [a, b, x, g, dres]
    if res is not None:
        in_specs.append(row)
        args.append(res)
    if dep is not None:
        in_specs.append(pl.BlockSpec(memory_space=pl.ANY))
        args.append(dep)
    return pl.pallas_call(
        body, name=name, grid=(m // tm,), in_specs=in_specs, out_specs=[row, row, vec],
        out_shape=[jax.ShapeDtypeStruct((m, n), F32), jax.ShapeDtypeStruct((m, n), BF16),
                   jax.ShapeDtypeStruct((1, n), F32)],
        compiler_params=_params(("arbitrary",)))(*args)


def matmul_ta(a, b, *, dep=None, out_dtype=BF16, name):
    s, m = a.shape
    n = b.shape[1]
    assert b.shape[0] == s
    best = None
    for tm in _divisors(m, (512, 256, 128)):
        for tn in _divisors(n, (1024, 512, 384, 256, 128)):
            need = 2 * (s * tm * a.dtype.itemsize + s * tn * b.dtype.itemsize + tm * tn * 2) + tm * tn * 4
            if need <= VMEM_BUDGET and (best is None or tm * tn > best[0] * best[1]):
                best = (tm, tn)
    tm, tn = best

    def body(*refs):
        a_ref, b_ref, o_ref = refs[0], refs[1], refs[-1]
        o_ref[...] = _dot(a_ref[...].astype(BF16), b_ref[...].astype(BF16), TN).astype(out_dtype)

    in_specs = [pl.BlockSpec((s, tm), lambda i, j: (0, i)), pl.BlockSpec((s, tn), lambda i, j: (0, j))]
    args = [a, b]
    if dep is not None:
        in_specs.append(pl.BlockSpec(memory_space=pl.ANY))
        args.append(dep)
    return pl.pallas_call(
        body, name=name, grid=(m // tm, n // tn), in_specs=in_specs,
        out_specs=pl.BlockSpec((tm, tn), lambda i, j: (i, j)),
        out_shape=jax.ShapeDtypeStruct((m, n), out_dtype),
        compiler_params=_params(("parallel", "parallel")))(*args)


def rms_fwd(x, g, *, name):
    s, d = x.shape

    def body(x_ref, g_ref, h_ref):
        xv = x_ref[...]
        r = lax.rsqrt(jnp.mean(xv * xv, axis=-1, keepdims=True) + EPS)
        h_ref[...] = (xv * r * g_ref[...]).astype(BF16)

    return pl.pallas_call(
        body, name=name, grid=(s // ROWS,),
        in_specs=[pl.BlockSpec((ROWS, d), lambda i: (i, 0)), pl.BlockSpec((1, d), lambda i: (0, 0))],
        out_specs=pl.BlockSpec((ROWS, d), lambda i: (i, 0)),
        out_shape=jax.ShapeDtypeStruct((s, d), BF16),
        compiler_params=_params(("parallel",)))(x, g)


def final_loss_bwd(a, b, res, g, tgt, *, name):
    s, k = a.shape
    d = b.shape[1]

    def body(a_ref, b_ref, r_ref, g_ref, t_ref, loss_ref, dx_ref, dxb_ref, dg_ref):
        xv = _dot(a_ref[...], b_ref[...], NN) + r_ref[...]
        gv = g_ref[...]
        r = lax.rsqrt(jnp.mean(xv * xv, axis=-1, keepdims=True) + EPS)
        xhat = xv * r
        e = xhat * gv - t_ref[...]
        dy = e * (1.0 / d)
        dxh = dy * gv
        dx = r * (dxh - xhat * jnp.mean(dxh * xhat, axis=-1, keepdims=True))
        dx_ref[...] = dx
        dxb_ref[...] = dx.astype(BF16)

        @pl.when(pl.program_id(0) == 0)
        def _():
            dg_ref[...] = jnp.zeros_like(dg_ref)
            loss_ref[...] = jnp.zeros_like(loss_ref)
        dg_ref[...] += _colsum(dy * xhat)
        loss_ref[...] += 0.5 * jnp.sum(jnp.mean(e * e, axis=-1, keepdims=True))

    row = pl.BlockSpec((ROWS, d), lambda i: (i, 0))
    vec = pl.BlockSpec((1, d), lambda i: (0, 0))
    one = pl.BlockSpec((8, 128), lambda i: (0, 0))
    return pl.pallas_call(
        body, name=name, grid=(s // ROWS,),
        in_specs=[pl.BlockSpec((ROWS, k), lambda i: (i, 0)), pl.BlockSpec((k, d), lambda i: (0, 0)), row, vec, row],
        out_specs=[one, row, row, vec],
        out_shape=[jax.ShapeDtypeStruct((8, 128), F32), jax.ShapeDtypeStruct((s, d), F32),
                   jax.ShapeDtypeStruct((s, d), BF16), jax.ShapeDtypeStruct((1, d), F32)],
        compiler_params=_params(("arbitrary",)))(a, b, res, g, tgt)


def _pair_masks():
    lane = lax.broadcasted_iota(jnp.int32, (CHUNK, 128), 1)
    return lane < HEAD


def _head_keep(rows):
    lane = lax.broadcasted_iota(jnp.int32, (rows, 128), 1)
    first = jnp.where(lane < HEAD, 1.0, 0.0)
    return first.astype(BF16), (1.0 - first).astype(BF16)


def _gating_mixed(vn_b, wm_ref, lo):
    rows, aw = vn_b.shape
    out = []
    for c in range(rows // CHUNK):
        tiles = []
        for p in range(aw // 128):
            vp = vn_b[c * CHUNK:(c + 1) * CHUNK, p * 128:(p + 1) * 128]
            r0 = _dot(wm_ref[2 * p], vp, NN)
            r1 = _dot(wm_ref[2 * p + 1], vp, NN)
            tiles.append(jnp.where(lo, r0, r1))
        out.append(jnp.concatenate(tiles, axis=1))
    return jnp.concatenate(out, axis=0)


def even_mid_fwd(z, vg, vb, wm, bm, cw, cb, cg, cbeta, *, name):
    s, zw = z.shape
    aw = zw // 4
    nblk = s // ROWS

    def body(z_ref, zp_ref, vg_ref, vb_ref, wm_ref, bm_ref, cw_ref, cb_ref, cg_ref, cbeta_ref,
             y_ref, hc_ref, ext_ref):
        i = pl.program_id(0)
        lo = _pair_masks()
        u = _gelu(z_ref[:, 0:aw])
        v = _gelu(z_ref[:, aw:2 * aw])
        vhat, _ = _ln_stats(v)
        vn = (vhat * vg_ref[...] + vb_ref[...]).astype(BF16)
        mixed = _gating_mixed(vn, wm_ref, lo)
        bias = jnp.concatenate([bm_ref[...]] * (ROWS // CHUNK), axis=0)
        y_ref[:, 0:aw] = (u * (mixed + bias)).astype(BF16)

        hb = z_ref[:, 2 * aw:3 * aw] * _sigmoid(z_ref[:, 3 * aw:4 * aw])
        hbp = zp_ref[:, 0:aw] * _sigmoid(zp_ref[:, aw:2 * aw])
        ext_ref[0:HALO, :] = jnp.where(i > 0, hbp, 0.0)
        ext_ref[HALO:HALO + ROWS, :] = hb
        acc = jnp.zeros((ROWS, aw), F32) + cb_ref[...]
        for k in range(CONV_K):
            acc = acc + cw_ref[k:k + 1, :] * ext_ref[pl.ds(HALO - (CONV_K - 1) + k, ROWS), :]
        hc_ref[...] = acc
        hhat, _ = _ln_stats(acc)
        hn = hhat * cg_ref[...] + cbeta_ref[...]
        y_ref[:, aw:2 * aw] = (hn * _sigmoid(hn)).astype(BF16)

    hb_per = ROWS // HALO
    vec = pl.BlockSpec((1, aw), lambda i: (0, 0))
    return pl.pallas_call(
        body, name=name, grid=(nblk,),
        in_specs=[pl.BlockSpec((ROWS, zw), lambda i: (i, 0)),
                  pl.BlockSpec((HALO, 2 * aw), lambda i: (jnp.maximum(i * hb_per - 1, 0), 1)),
                  vec, vec,
                  pl.BlockSpec(wm.shape, lambda i: (0, 0, 0)),
                  pl.BlockSpec((CHUNK, aw), lambda i: (0, 0)),
                  pl.BlockSpec((CONV_K, aw), lambda i: (0, 0)), vec, vec, vec],
        out_specs=[pl.BlockSpec((ROWS, 2 * aw), lambda i: (i, 0)), pl.BlockSpec((ROWS, aw), lambda i: (i, 0))],
        out_shape=[jax.ShapeDtypeStruct((s, 2 * aw), BF16), jax.ShapeDtypeStruct((s, aw), F32)],
        scratch_shapes=[pltpu.VMEM((HALO + ROWS, aw), F32)],
        compiler_params=_params(("parallel",)))(z, z, vg, vb, wm, bm, cw, cb, cg, cbeta)


def even_mid_bwd_rows(dy, z, hc, vg, vb, wm, wmt, bm, sel, cg, cbeta, *, name):
    s, zw = z.shape
    aw = zw // 4
    nh = wm.shape[0]

    def body(dy_ref, z_ref, hc_ref, vg_ref, vb_ref, wm_ref, wmt_ref, bm_ref, sel_ref, cg_ref, cbeta_ref,
             dza_ref, dhc_ref, dba_ref, dvg_ref, dvb_ref, dwm_ref, dbs_ref, dcg_ref, dcbeta_ref, dcb_ref):
        @pl.when(pl.program_id(0) == 0)
        def _():
            for r in (dba_ref, dvg_ref, dvb_ref, dwm_ref, dbs_ref, dcg_ref, dcbeta_ref, dcb_ref):
                r[...] = jnp.zeros_like(r)

        lo = _pair_masks()
        keep = _head_keep(CHUNK)
        zu = z_ref[:, 0:aw]
        zv = z_ref[:, aw:2 * aw]
        u = _gelu(zu)
        v = _gelu(zv)
        vhat, vrstd = _ln_stats(v)
        vn = (vhat * vg_ref[...] + vb_ref[...]).astype(BF16)
        mixed = _gating_mixed(vn, wm_ref, lo)
        bias = jnp.concatenate([bm_ref[...]] * (ROWS // CHUNK), axis=0)
        dya = dy_ref[:, 0:aw]
        du = dya * (mixed + bias)
        dmix = dya * u
        dmix_b = dmix.astype(BF16)

        dvn_rows = []
        for c in range(ROWS // CHUNK):
            rs = slice(c * CHUNK, (c + 1) * CHUNK)
            tiles = []
            for p in range(aw // 128):
                cs = slice(p * 128, (p + 1) * 128)
                dm = dmix_b[rs, cs]
                dm0 = dm * keep[0]
                dm1 = dm * keep[1]
                vp = vn[rs, cs]
                tiles.append(_dot(wmt_ref[2 * p], dm0, NN) + _dot(wmt_ref[2 * p + 1], dm1, NN))
                dwm_ref[2 * p] += _dot(dm0, vp, NT)
                dwm_ref[2 * p + 1] += _dot(dm1, vp, NT)
            dvn_rows.append(jnp.concatenate(tiles, axis=1))
            acc = jnp.zeros((CHUNK, 128), F32)
            for part in _split3(dmix[rs, :]):
                acc = acc + _dot(part, sel_ref[...], NN)
            dbs_ref[...] += acc
        dvn = jnp.concatenate(dvn_rows, axis=0)
        dvg_ref[...] += _colsum(dvn * vhat)
        dvb_ref[...] += _colsum(dvn)
        dv = _ln_bwd(dvn, vhat, vrstd, vg_ref[...])
        dzu = du * _gelu_grad(zu)
        dzv = dv * _gelu_grad(zv)
        dza_ref[:, 0:aw] = dzu.astype(BF16)
        dza_ref[:, aw:2 * aw] = dzv.astype(BF16)
        dba_ref[:, 0:aw] += _colsum(dzu)
        dba_ref[:, aw:2 * aw] += _colsum(dzv)

        hcv = hc_ref[...]
        hhat, hrstd = _ln_stats(hcv)
        hn = hhat * cg_ref[...] + cbeta_ref[...]
        sg = _sigmoid(hn)
        dhn = dy_ref[:, aw:2 * aw] * (sg * (1.0 + hn * (1.0 - sg)))
        dcg_ref[...] += _colsum(dhn * hhat)
        dcbeta_ref[...] += _colsum(dhn)
        dhc = _ln_bwd(dhn, hhat, hrstd, cg_ref[...])
        dhc_ref[...] = dhc
        dcb_ref[...] += _colsum(dhc)

    vec = pl.BlockSpec((1, aw), lambda i: (0, 0))
    vec2 = pl.BlockSpec((1, 2 * aw), lambda i: (0, 0))
    w3 = pl.BlockSpec(wm.shape, lambda i: (0, 0, 0))
    sq = pl.BlockSpec((CHUNK, 128), lambda i: (0, 0))
    return pl.pallas_call(
        body, name=name, grid=(s // ROWS,),
        in_specs=[pl.BlockSpec((ROWS, 2 * aw), lambda i: (i, 0)), pl.BlockSpec((ROWS, 2 * aw), lambda i: (i, 0)),
                  pl.BlockSpec((ROWS, aw), lambda i: (i, 0)), vec, vec, w3, w3,
                  pl.BlockSpec((CHUNK, aw), lambda i: (0, 0)), pl.BlockSpec((aw, 128), lambda i: (0, 0)), vec, vec],
        out_specs=[pl.BlockSpec((ROWS, 2 * aw), lambda i: (i, 0)), pl.BlockSpec((ROWS, aw), lambda i: (i, 0)),
                   vec2, vec, vec, w3, sq, vec, vec, vec],
        out_shape=[jax.ShapeDtypeStruct((s, 2 * aw), BF16), jax.ShapeDtypeStruct((s, aw), F32),
                   jax.ShapeDtypeStruct((1, 2 * aw), F32), jax.ShapeDtypeStruct((1, aw), F32),
                   jax.ShapeDtypeStruct((1, aw), F32), jax.ShapeDtypeStruct(wm.shape, F32),
                   jax.ShapeDtypeStruct((CHUNK, 128), F32), jax.ShapeDtypeStruct((1, aw), F32),
                   jax.ShapeDtypeStruct((1, aw), F32), jax.ShapeDtypeStruct((1, aw), F32)],
        compiler_params=_params(("arbitrary",)))(dy, z, hc, vg, vb, wm, wmt, bm, sel, cg, cbeta)


def even_conv_bwd(dhc, z, cw, *, name):
    s, zw = z.shape
    aw = zw // 4
    nblk = s // ROWS
    hb_per = ROWS // HALO
    rc, lc = 8, 128

    def body(dc_ref, dn_ref, z_ref, cw_ref, dzb_ref, dbb_ref, dcw_ref, extd_ref):
        i = pl.program_id(0)

        @pl.when(i == 0)
        def _():
            dbb_ref[...] = jnp.zeros_like(dbb_ref)
            dcw_ref[...] = jnp.zeros_like(dcw_ref)

        extd_ref[0:ROWS, :] = dc_ref[...]
        extd_ref[ROWS:ROWS + HALO, :] = jnp.where(i < nblk - 1, dn_ref[...], 0.0)
        for c0 in range(0, aw, lc):
            lanes = slice(c0, c0 + lc)
            acc = [jnp.zeros((rc, lc), F32)] * CONV_K
            acc_a = jnp.zeros((rc, lc), F32)
            acc_g = jnp.zeros((rc, lc), F32)
            pending = None
            for r0 in range(0, ROWS, rc):
                a = z_ref[pl.ds(r0, rc), c0:c0 + lc]
                sg = _sigmoid(z_ref[pl.ds(r0, rc), aw + c0:aw + c0 + lc])
                hb = a * sg
                dhb = jnp.zeros((rc, lc), F32)
                for k in range(CONV_K):
                    d = extd_ref[pl.ds(r0 + CONV_K - 1 - k, rc), lanes]
                    dhb = dhb + cw_ref[k:k + 1, lanes] * d
                    acc[k] = acc[k] + hb * d
                da = dhb * sg
                dg = da * a * (1.0 - sg)
                acc_a = acc_a + da
                acc_g = acc_g + dg
                if pending is None:
                    pending = (da, dg)
                else:
                    rows = pl.ds(r0 - rc, 2 * rc)
                    dzb_ref[rows, c0:c0 + lc] = jnp.concatenate([pending[0], da], axis=0).astype(BF16)
                    dzb_ref[rows, aw + c0:aw + c0 + lc] = jnp.concatenate([pending[1], dg], axis=0).astype(BF16)
                    pending = None
            for k in range(CONV_K):
                dcw_ref[k:k + 1, lanes] += _colsum(acc[k])
            dbb_ref[:, c0:c0 + lc] += _colsum(acc_a)
            dbb_ref[:, aw + c0:aw + c0 + lc] += _colsum(acc_g)

    return pl.pallas_call(
        body, name=name, grid=(nblk,),
        in_specs=[pl.BlockSpec((ROWS, aw), lambda i: (i, 0)),
                  pl.BlockSpec((HALO, aw), lambda i: (jnp.minimum((i + 1) * hb_per, nblk * hb_per - 1), 0)),
                  pl.BlockSpec((ROWS, 2 * aw), lambda i: (i, 1)),
                  pl.BlockSpec((CONV_K, aw), lambda i: (0, 0))],
        out_specs=[pl.BlockSpec((ROWS, 2 * aw), lambda i: (i, 0)), pl.BlockSpec((1, 2 * aw), lambda i: (0, 0)),
                   pl.BlockSpec((CONV_K, aw), lambda i: (0, 0))],
        out_shape=[jax.ShapeDtypeStruct((s, 2 * aw), BF16), jax.ShapeDtypeStruct((1, 2 * aw), F32),
                   jax.ShapeDtypeStruct((CONV_K, aw), F32)],
        scratch_shapes=[pltpu.VMEM((ROWS + HALO, aw), F32)],
        compiler_params=_params(("arbitrary",)))(dhc, dhc, z, cw)


FFN_ROWS = 1024
FFN_CHUNK = 16


def _ffn_tile(f):
    for t in (256, 128):
        if f % t == 0:
            return t
    raise ValueError(f)


def _taps(ext_ref, w, b, r0, rows, halo):
    acc = b
    for k in range(FFN_K):
        acc = acc + w[k] * ext_ref[pl.ds(halo - (FFN_K - 1) + k + r0, rows), :]
    return acc


UP_HALO = 16
UP_SUB = 128


def ffn_up_mid(h, w_g, w_v, cw, cb, *, name):
    s, d = h.shape
    f = w_g.shape[1]
    tn = _ffn_tile(f)
    nj = f // tn
    ROWS = FFN_ROWS
    per = ROWS // UP_HALO

    def body(h_ref, hp_ref, wgm_ref, wvm_ref, wg_ref, wv_ref, bg_ref, bv_ref,
             act_ref, ug_ref, uv_ref, cg_ref, cv_ref, eg_ref, ev_ref):
        i = pl.program_id(0)
        wg = [wg_ref[k:k + 1, :] for k in range(FFN_K)]
        wv = [wv_ref[k:k + 1, :] for k in range(FFN_K)]
        bg, bv = bg_ref[...], bv_ref[...]
        sides = ((wgm_ref, eg_ref, ug_ref), (wvm_ref, ev_ref, uv_ref))
        tails = [jnp.where(i > 0, _dot(hp_ref[...], wm_ref[...], NN), 0.0) for wm_ref, _, _ in sides]

        def project(sub, tails):
            r0 = sub * UP_SUB
            src = h_ref[pl.ds(r0, UP_SUB), :]
            new_tails = []
            for (wm_ref, ext_ref, up_ref), tail in zip(sides, tails):
                u = _dot(src, wm_ref[...], NN)
                ext_ref[sub % 2, 0:UP_HALO, :] = tail
                ext_ref[sub % 2, UP_HALO:UP_HALO + UP_SUB, :] = u
                up_ref[pl.ds(r0, UP_SUB), :] = u
                new_tails.append(u[UP_SUB - UP_HALO:, :])
            return new_tails

        nsub = ROWS // UP_SUB
        tails = project(0, tails)
        for sub in range(nsub):
            if sub + 1 < nsub:
                tails = project(sub + 1, tails)
            for r0 in range(0, UP_SUB, FFN_CHUNK):
                rows = pl.ds(sub * UP_SUB + r0, FFN_CHUNK)
                gate = _taps(eg_ref.at[sub % 2], wg, bg, r0, FFN_CHUNK, UP_HALO)
                val = _taps(ev_ref.at[sub % 2], wv, bv, r0, FFN_CHUNK, UP_HALO)
                cg_ref[rows, :] = gate
                cv_ref[rows, :] = val
                act_ref[rows, :] = (gate * _sigmoid(gate) * val).astype(BF16)

    blk = pl.BlockSpec((ROWS, tn), lambda i, j: (i, j))
    wm = pl.BlockSpec((d, tn), lambda i, j: (0, j))
    wsp = lambda off: pl.BlockSpec((FFN_K, tn), lambda i, j: (0, j + off))
    bsp = lambda off: pl.BlockSpec((1, tn), lambda i, j: (0, j + off))
    return pl.pallas_call(
        body, name=name, grid=(s // ROWS, nj),
        in_specs=[pl.BlockSpec((ROWS, d), lambda i, j: (i, 0)),
                  pl.BlockSpec((UP_HALO, d), lambda i, j: (jnp.maximum(i * per - 1, 0), 0)),
                  wm, wm, wsp(0), wsp(nj), bsp(0), bsp(nj)],
        out_specs=[blk] * 5,
        out_shape=[jax.ShapeDtypeStruct((s, f), BF16)] + [jax.ShapeDtypeStruct((s, f), F32)] * 4,
        scratch_shapes=[pltpu.VMEM((2, UP_HALO + UP_SUB, tn), F32), pltpu.VMEM((2, UP_HALO + UP_SUB, tn), F32)],
        compiler_params=_params(("parallel", "parallel")))(h, h, w_g, w_v, cw, cw, cb, cb)


def ffn_mid_bwd(dact, up_g, up_v, conv_g, conv_v, cw, *, name):
    s, f = up_g.shape
    tn = _ffn_tile(f)
    nj = f // tn
    ROWS = FFN_ROWS
    nblk = s // ROWS
    per = ROWS // FHALO
    ext = ROWS + FHALO

    def body(da_ref, dan_ref, ug_ref, uv_ref, cg_ref, cv_ref, cgn_ref, cvn_ref, wg_ref, wv_ref,
             dug_ref, duv_ref, dwg_ref, dwv_ref, dbg_ref, dbv_ref, dg_ref, dv_ref):
        i = pl.program_id(1)

        @pl.when(i == 0)
        def _():
            for r in (dwg_ref, dwv_ref, dbg_ref, dbv_ref):
                r[...] = jnp.zeros_like(r)

        wg = [wg_ref[k:k + 1, :] for k in range(FFN_K)]
        wv = [wv_ref[k:k + 1, :] for k in range(FFN_K)]

        for r0, rows in [(r, FFN_CHUNK) for r in range(0, ROWS, FFN_CHUNK)] + [(ROWS, FHALO)]:
            if r0 < ROWS:
                gate, val, da = cg_ref[pl.ds(r0, rows), :], cv_ref[pl.ds(r0, rows), :], da_ref[pl.ds(r0, rows), :]
            else:
                gate, val, da = cgn_ref[...], cvn_ref[...], jnp.where(i < nblk - 1, dan_ref[...], 0.0)
            sg = _sigmoid(gate)
            dg_ref[pl.ds(r0, rows), :] = da * val * (sg * (1.0 + gate * (1.0 - sg)))
            dv_ref[pl.ds(r0, rows), :] = da * (gate * sg)

        def back(d_ref, w, u_ref, du_ref, dw_ref, db_ref):
            zero = jnp.zeros((FFN_CHUNK, tn), F32)
            acc = [zero] * FFN_K
            accb = zero
            for r0 in range(0, ROWS, FFN_CHUNK):
                d = [d_ref[pl.ds(r0 + FFN_K - 1 - k, FFN_CHUNK), :] for k in range(FFN_K)]
                u = u_ref[pl.ds(r0, FFN_CHUNK), :]
                du = w[0] * d[0]
                for k in range(1, FFN_K):
                    du = du + w[k] * d[k]
                du_ref[pl.ds(r0, FFN_CHUNK), :] = du.astype(BF16)
                acc = [acc[k] + u * d[k] for k in range(FFN_K)]
                accb = accb + d[FFN_K - 1]
            for k in range(FFN_K):
                dw_ref[k:k + 1, :] += _colsum(acc[k])
            db_ref[...] += _colsum(accb)

        back(dg_ref, wg, ug_ref, dug_ref, dwg_ref, dbg_ref)
        back(dv_ref, wv, uv_ref, duv_ref, dwv_ref, dbv_ref)

    cur = pl.BlockSpec((ROWS, tn), lambda j, i: (i, j))
    nxt = pl.BlockSpec((FHALO, tn), lambda j, i: (jnp.minimum((i + 1) * per, nblk * per - 1), j))
    wsp = lambda off: pl.BlockSpec((FFN_K, tn), lambda j, i: (0, j + off))
    bsp = pl.BlockSpec((1, tn), lambda j, i: (0, j))
    outs = pl.pallas_call(
        body, name=name, grid=(nj, nblk),
        in_specs=[cur, nxt, cur, cur, cur, cur, nxt, nxt, wsp(0), wsp(nj)],
        out_specs=[cur, cur, wsp(0), wsp(0), bsp, bsp],
        out_shape=[jax.ShapeDtypeStruct((s, f), BF16), jax.ShapeDtypeStruct((s, f), BF16),
                   jax.ShapeDtypeStruct((FFN_K, f), F32), jax.ShapeDtypeStruct((FFN_K, f), F32),
                   jax.ShapeDtypeStruct((1, f), F32), jax.ShapeDtypeStruct((1, f), F32)],
        scratch_shapes=[pltpu.VMEM((ext, tn), F32), pltpu.VMEM((ext, tn), F32)],
        compiler_params=_params(("parallel", "arbitrary")))(dact, dact, up_g, up_v, conv_g, conv_v, conv_g, conv_v,
                                                            cw, cw)
    dug, duv, dwg, dwv, dbg, dbv = outs
    return dug, duv, jnp.concatenate([dwg, dwv], axis=1), jnp.concatenate([dbg, dbv], axis=1)


def rope_tables(s):
    half = HEAD // 2
    lane = jnp.arange(128)
    j = lane % HEAD
    inv = ROPE_THETA ** (-(j % half).astype(F32) / half)
    ang = jnp.arange(s, dtype=F32)[:, None] * inv[None, :]
    sign = jnp.where(j < half, -1.0, 1.0).astype(F32)
    return jnp.cos(ang), jnp.sin(ang) * sign[None, :]


def _swap_halves(x):
    lane = lax.broadcasted_iota(jnp.int32, x.shape, 1)
    return jnp.where((lane % HEAD) < HEAD // 2, pltpu.roll(x, 128 - HEAD // 2, 1), pltpu.roll(x, HEAD // 2, 1))


def qkv_rope(h, w, cos, sin, *, name):
    s, k = h.shape
    d = w.shape[1] // 3
    tm = _divisors(s, (1024, 512))[0]
    scale = HEAD ** -0.5

    def body(a_ref, b_ref, c_ref, s_ref, o_ref):
        third = pl.program_id(1)
        o_ref[...] = _dot(a_ref[...], b_ref[...], NN)

        @pl.when(third < 2)
        def _():
            c = c_ref[...]
            sn = s_ref[...]
            factor = jnp.where(third == 0, scale, 1.0)
            for t in range(d // 128):
                cs = slice(t * 128, (t + 1) * 128)
                x = o_ref[:, cs]
                o_ref[:, cs] = (x * c + _swap_halves(x) * sn) * factor

    tab = pl.BlockSpec((tm, 128), lambda i, j: (i, 0))
    return pl.pallas_call(
        body, name=name, grid=(s // tm, 3),
        in_specs=[pl.BlockSpec((tm, k), lambda i, j: (i, 0)), pl.BlockSpec((k, d), lambda i, j: (0, j)), tab, tab],
        out_specs=pl.BlockSpec((tm, d), lambda i, j: (i, j)),
        out_shape=jax.ShapeDtypeStruct((s, 3 * d), F32),
        compiler_params=_params(("parallel", "parallel")))(h, w, cos, sin)


def rope_bwd(dq, dk, dv, cos, sin, *, name):
    s, d = dq.shape
    scale = HEAD ** -0.5

    def body(dq_ref, dk_ref, dv_ref, c_ref, s_ref, o_ref):
        c = c_ref[...]
        sn = s_ref[...]
        for t in range(d // 128):
            cs = slice(t * 128, (t + 1) * 128)
            gq = dq_ref[:, cs] * scale
            gk = dk_ref[:, cs]
            o_ref[:, t * 128:(t + 1) * 128] = (gq * c + _swap_halves(gq * sn)).astype(BF16)
            o_ref[:, d + t * 128:d + (t + 1) * 128] = (gk * c + _swap_halves(gk * sn)).astype(BF16)
        o_ref[:, 2 * d:3 * d] = dv_ref[...].astype(BF16)

    row = pl.BlockSpec((ROWS, d), lambda i: (i, 0))
    tab = pl.BlockSpec((ROWS, 128), lambda i: (i, 0))
    return pl.pallas_call(
        body, name=name, grid=(s // ROWS,),
        in_specs=[row, row, row, tab, tab],
        out_specs=pl.BlockSpec((ROWS, 3 * d), lambda i: (i, 0)),
        out_shape=jax.ShapeDtypeStruct((s, 3 * d), BF16),
        compiler_params=_params(("parallel",)))(dq, dk, dv, cos, sin)


ATT_T = BLOCK * max(DILATIONS)


FWD_GROUP = 2
ATT_GROUP = 4
FWD_QROWS = 128
BWD_QROWS = 64


def _unit_rows(r, j, dil):
    start = r + dil * BLOCK * j
    return pl.ds(start, BLOCK) if dil == 1 else pl.ds(start, BLOCK, stride=dil)


def _units():
    for bi, dil in enumerate(DILATIONS):
        nsub = ATT_T // (BLOCK * dil)
        for r in range(dil):
            for j in range(nsub):
                yield bi, dil, nsub, r, j


def _band(first_block, part, qrows):
    qi = lax.broadcasted_iota(jnp.int32, (qrows, 2 * BLOCK), 0) + part * qrows
    kj = lax.broadcasted_iota(jnp.int32, (qrows, 2 * BLOCK), 1)
    dist = BLOCK + qi - kj
    band = (dist >= 0) & (dist <= BLOCK)
    return band, band & (jnp.logical_not(first_block) | (kj >= BLOCK))


def _col(tile, h):
    return tile[:, h * HEAD:h * HEAD + 1]


def _keys(cur_ref, prev_ref, r, j, dil, nsub):
    cur = cur_ref[_unit_rows(r, j, dil), :]
    prev = cur_ref[_unit_rows(r, j - 1, dil), :] if j > 0 else prev_ref[_unit_rows(r, nsub - 1, dil), :]
    return jnp.concatenate([prev, cur], axis=0).astype(BF16)


def _att_specs(col_off=0):
    cur = pl.BlockSpec((ATT_T, 128), lambda n, p: (n, p + col_off))
    prv = pl.BlockSpec((ATT_T, 128), lambda n, p: (jnp.maximum(n - 1, 0), p + col_off))
    return cur, prv


def attn_fwd(qkv, *, name):
    s, d = qkv.shape[0], qkv.shape[1] // 3
    nt = s // ATT_T

    def body(q_ref, kc_ref, kp_ref, vc_ref, vp_ref, o_ref, lse_ref, acc_ref, m_ref, l_ref):
        n = pl.program_id(0)
        QROWS = FWD_QROWS
        nparts = BLOCK // QROWS
        bands = [_band(n == 0, part, QROWS) for part in range(nparts)]
        lo = _pair_masks()
        keep = _head_keep(BLOCK)
        nb = len(DILATIONS)
        tile = lambda cols: jnp.where(lo, jnp.concatenate(cols[:nparts], axis=0),
                                      jnp.concatenate(cols[nparts:], axis=0))

        def scores(unit):
            bi, dil, nsub, r, j = unit
            rows = _unit_rows(r, j, dil)
            kw = _keys(kc_ref, kp_ref, r, j, dil, nsub)
            vw = _keys(vc_ref, vp_ref, r, j, dil, nsub)
            qp = q_ref[rows, :].astype(BF16)
            sc2 = _dot(jnp.concatenate([qp * keep[0], qp * keep[1]], axis=0), kw, NT)
            return dict(bi=bi, j=j, rows=rows, vw=vw, sc2=sc2, m_old=m_ref[rows, :] if bi > 0 else None)

        def softmax(u):
            prs, new_m, new_l, alpha = [], [], [], []
            for c in range(2 * nparts):
                h, part = divmod(c, nparts)
                valid = bands[part][0 if u['j'] > 0 else 1]
                sc = jnp.where(valid, u['sc2'][c * QROWS:(c + 1) * QROWS], NEG)
                mx = jnp.max(sc, axis=-1, keepdims=True)
                if u['bi'] == 0:
                    m_new = mx
                else:
                    m_old = _col(u['m_old'][part * QROWS:(part + 1) * QROWS], h)
                    m_new = jnp.maximum(m_old, mx)
                    alpha.append(jnp.exp(m_old - m_new))
                pr = jnp.exp(sc - m_new)
                new_m.append(m_new)
                new_l.append(jnp.sum(pr, axis=-1, keepdims=True))
                prs.append(pr.astype(BF16))
            u.update(pr2=jnp.concatenate(prs, axis=0), new_m=new_m, new_l=new_l, alpha=alpha)

        def combine(u):
            rows, bi = u['rows'], u['bi']
            pv2 = _dot(u['pr2'], u['vw'], NN)
            m_t = tile(u['new_m'])
            l_t = tile(u['new_l'])
            acc_t = jnp.where(lo, pv2[:BLOCK], pv2[BLOCK:])
            if bi > 0:
                a_t = tile(u['alpha'])
                l_t = a_t * l_ref[rows, :] + l_t
                acc_t = a_t * acc_ref[rows, :] + acc_t
            if bi == nb - 1:
                o_ref[rows, :] = acc_t / l_t
                lse_ref[rows, :] = m_t + jnp.log(l_t)
            else:
                acc_ref[rows, :] = acc_t
                m_ref[rows, :] = m_t
                l_ref[rows, :] = l_t

        units = list(_units())
        for first in range(0, len(units), FWD_GROUP):
            pair = [scores(u) for u in units[first:first + FWD_GROUP]]
            for u in pair:
                softmax(u)
            for u in pair:
                combine(u)

    cur, _ = _att_specs()
    kcur, kprv = _att_specs(d // 128)
    vcur, vprv = _att_specs(2 * (d // 128))
    return pl.pallas_call(
        body, name=name, grid=(nt, d // 128), in_specs=[cur, kcur, kprv, vcur, vprv], out_specs=[cur, cur],
        out_shape=[jax.ShapeDtypeStruct((s, d), F32)] * 2,
        scratch_shapes=[pltpu.VMEM((ATT_T, 128), F32)] * 3,
        compiler_params=_params(("parallel", "parallel")))(qkv, qkv, qkv, qkv, qkv)


def attn_delta(do, o, *, name):
    s, d = do.shape

    def body(do_ref, o_ref, dl_ref):
        lane = lax.broadcasted_iota(jnp.int32, (ROWS, 128), 1)
        lo = lane < HEAD
        for p in range(d // 128):
            cs = slice(p * 128, (p + 1) * 128)
            pr = do_ref[:, cs] * o_ref[:, cs]
            s0 = jnp.sum(jnp.where(lo, pr, 0.0), axis=-1, keepdims=True)
            s1 = jnp.sum(jnp.where(lo, 0.0, pr), axis=-1, keepdims=True)
            dl_ref[:, cs] = jnp.where(lo, s0, s1)

    row = pl.BlockSpec((ROWS, d), lambda i: (i, 0))
    return pl.pallas_call(
        body, name=name, grid=(s // ROWS,), in_specs=[row, row], out_specs=row,
        out_shape=jax.ShapeDtypeStruct((s, d), F32),
        compiler_params=_params(("parallel",)))(do, o)


def attn_dq(qkv, do, lse, delta, *, name):
    s, d = do.shape
    nt = s // ATT_T

    def body(q_ref, kc_ref, kp_ref, vc_ref, vp_ref, do_ref, l_ref, dl_ref, dq_ref):
        n = pl.program_id(0)
        QROWS = BWD_QROWS
        nparts = BLOCK // QROWS
        bands = [_band(n == 0, part, QROWS) for part in range(nparts)]
        lo = _pair_masks()
        keep = _head_keep(BLOCK)
        def scores(unit):
            bi, dil, nsub, r, j = unit
            rows = _unit_rows(r, j, dil)
            kw = _keys(kc_ref, kp_ref, r, j, dil, nsub)
            vw = _keys(vc_ref, vp_ref, r, j, dil, nsub)
            qp = q_ref[rows, :].astype(BF16)
            dop = do_ref[rows, :].astype(BF16)
            sc2 = _dot(jnp.concatenate([qp * keep[0], qp * keep[1]], axis=0), kw, NT)
            dp2 = _dot(jnp.concatenate([dop * keep[0], dop * keep[1]], axis=0), vw, NT)
            return dict(bi=bi, j=j, rows=rows, kw=kw, sc2=sc2, dp2=dp2, lt=l_ref[rows, :], dt=dl_ref[rows, :])

        def softmax_bwd(u):
            dss = []
            for c in range(2 * nparts):
                h, part = divmod(c, nparts)
                valid = bands[part][0 if u['j'] > 0 else 1]
                cr = slice(c * QROWS, (c + 1) * QROWS)
                pr_rows = slice(part * QROWS, (part + 1) * QROWS)
                pr = jnp.where(valid, jnp.exp(u['sc2'][cr] - _col(u['lt'][pr_rows], h)), 0.0)
                dss.append((pr * (u['dp2'][cr] - _col(u['dt'][pr_rows], h))).astype(BF16))
            u['ds2'] = jnp.concatenate(dss, axis=0)

        def combine(u):
            rows = u['rows']
            dq2 = _dot(u['ds2'], u['kw'], NN)
            dq_t = jnp.where(lo, dq2[:BLOCK], dq2[BLOCK:])
            if u['bi'] > 0:
                dq_t = dq_t + dq_ref[rows, :]
            dq_ref[rows, :] = dq_t

        units = list(_units())
        for first in range(0, len(units), ATT_GROUP):
            pair = [scores(u) for u in units[first:first + ATT_GROUP]]
            for u in pair:
                softmax_bwd(u)
            for u in pair:
                combine(u)

    cur, _ = _att_specs()
    kcur, kprv = _att_specs(d // 128)
    vcur, vprv = _att_specs(2 * (d // 128))
    return pl.pallas_call(
        body, name=name, grid=(nt, d // 128), in_specs=[cur, kcur, kprv, vcur, vprv, cur, cur, cur], out_specs=cur,
        out_shape=jax.ShapeDtypeStruct((s, d), F32),
        compiler_params=_params(("parallel", "parallel")))(qkv, qkv, qkv, qkv, qkv, do, lse, delta)


def attn_dkv(qkv, do, lse, delta, *, name):
    s, d = do.shape
    nt = s // ATT_T

    def body(k_ref, v_ref, qc_ref, qn_ref, doc_ref, don_ref, lc_ref, ln_ref, dc_ref, dn_ref, dk_ref, dv_ref):
        n = pl.program_id(0)
        qi = lax.broadcasted_iota(jnp.int32, (BLOCK, BLOCK), 0)
        kj = lax.broadcasted_iota(jnp.int32, (BLOCK, BLOCK), 1)
        own = kj <= qi
        nxt = kj >= qi
        nxt_edge = nxt & (n < nt - 1)
        keep = _head_keep(BLOCK)

        def scores(unit):
            bi, dil, nsub, r, j = unit
            rows = _unit_rows(r, j, dil)
            inner = j + 1 < nsub
            nrows = _unit_rows(r, j + 1, dil) if inner else _unit_rows(r, 0, dil)
            kp = k_ref[rows, :].astype(BF16)
            vp = v_ref[rows, :].astype(BF16)
            far = not inner
            take = lambda c_ref, n_ref, nx: ((n_ref if far else c_ref)[nrows, :] if nx else c_ref[rows, :])
            qs = [take(qc_ref, qn_ref, nx).astype(BF16) for nx in (False, True)]
            dos = [take(doc_ref, don_ref, nx).astype(BF16) for nx in (False, True)]
            lts = [take(lc_ref, ln_ref, nx) for nx in (False, True)]
            dts = [take(dc_ref, dn_ref, nx) for nx in (False, True)]
            q4 = jnp.concatenate([qs[nx] * keep[h] for h in range(2) for nx in range(2)], axis=0)
            do4 = jnp.concatenate([dos[nx] * keep[h] for h in range(2) for nx in range(2)], axis=0)
            return dict(bi=bi, rows=rows, q4=q4, do4=do4, s4=_dot(q4, kp, NT), dp4=_dot(do4, vp, NT), lts=lts,
                        dts=dts, valids=(own, nxt if inner else nxt_edge))

        def softmax_bwd(u):
            prs, dss = [], []
            for c in range(4):
                h, nx = divmod(c, 2)
                cr = slice(c * BLOCK, (c + 1) * BLOCK)
                pr = jnp.where(u['valids'][nx], jnp.exp(u['s4'][cr] - _col(u['lts'][nx], h)), 0.0)
                prs.append(pr.astype(BF16))
                dss.append((pr * (u['dp4'][cr] - _col(u['dts'][nx], h))).astype(BF16))
            u.update(pr4=jnp.concatenate(prs, axis=0), ds4=jnp.concatenate(dss, axis=0))

        def combine(u):
            rows = u['rows']
            dv_t = _dot(u['pr4'], u['do4'], TN)
            dk_t = _dot(u['ds4'], u['q4'], TN)
            if u['bi'] > 0:
                dk_t = dk_t + dk_ref[rows, :]
                dv_t = dv_t + dv_ref[rows, :]
            dk_ref[rows, :] = dk_t
            dv_ref[rows, :] = dv_t

        units = list(_units())
        for first in range(0, len(units), ATT_GROUP):
            pair = [scores(u) for u in units[first:first + ATT_GROUP]]
            for u in pair:
                softmax_bwd(u)
            for u in pair:
                combine(u)

    cur = pl.BlockSpec((ATT_T, 128), lambda n, p: (n, p))
    nxt_spec = pl.BlockSpec((ATT_T, 128), lambda n, p: (jnp.minimum(n + 1, nt - 1), p))
    kcur, _ = _att_specs(d // 128)
    vcur, _ = _att_specs(2 * (d // 128))
    return pl.pallas_call(
        body, name=name, grid=(nt, d // 128),
        in_specs=[kcur, vcur, cur, nxt_spec, cur, nxt_spec, cur, nxt_spec, cur, nxt_spec], out_specs=[cur, cur],
        out_shape=[jax.ShapeDtypeStruct((s, d), F32)] * 2,
        compiler_params=_params(("parallel", "parallel")))(qkv, qkv, qkv, qkv, do, do, lse, lse, delta, delta)


def adamw(parts_list, w, m, v, *, name):
    nk = len(parts_list)
    npart, rk, c = parts_list[0].shape
    r = rk * nk
    assert w.shape == (r, c)
    tr = next(t for t in range(rk, 0, -8) if rk % t == 0 and (t * c * 4 <= 1024 * 1024 or t == 8))
    nbk = rk // tr

    def body(*refs):
        p_refs = refs[:nk]
        w_ref, m_ref, v_ref, g_ref, d_ref, nm_ref, nv_ref = refs[nk:]
        i = pl.program_id(0)
        g = None
        for kk, p_ref in enumerate(p_refs):
            gk = p_ref[0].astype(F32)
            for j in range(1, npart):
                gk = gk + p_ref[j].astype(F32)
            g = gk if g is None else jnp.where(i >= kk * nbk, gk, g)
        m2 = B1 * m_ref[...] + (1.0 - B1) * g
        v2 = B2 * v_ref[...] + (1.0 - B2) * (g * g)
        m_hat = m2 / (1.0 - B1 ** STEP)
        v_hat = v2 / (1.0 - B2 ** STEP)
        g_ref[...] = g
        d_ref[...] = -LR * (m_hat / (jnp.sqrt(v_hat) + ADAM_EPS) + WD * w_ref[...])
        nm_ref[...] = m2
        nv_ref[...] = v2

    blk = pl.BlockSpec((tr, c), lambda i: (i, 0))
    pspec = lambda kk: pl.BlockSpec((npart, tr, c), lambda i: (0, jnp.clip(i - kk * nbk, 0, nbk - 1), 0))
    return pl.pallas_call(
        body, name=name, grid=(r // tr,),
        in_specs=[pspec(kk) for kk in range(nk)] + [blk, blk, blk],
        out_specs=[blk] * 4, out_shape=[jax.ShapeDtypeStruct((r, c), F32)] * 4,
        compiler_params=_params(("parallel",)))(*parts_list, w, m, v)


def _my_index():
    return 4 * lax.axis_index("x") + 2 * lax.axis_index("y") + lax.axis_index("c")


def _peer_of(kk):
    x, y, c = lax.axis_index("x"), lax.axis_index("y"), lax.axis_index("c")
    return x ^ (kk >> 2), y ^ ((kk >> 1) & 1), c ^ (kk & 1)


def _peer_copy(t, kk, scatter, ins, lands, send_sems, recv_sems):
    px, py, pc = _peer_of(kk)
    me = _my_index()
    src = ins[t].at[4 * px + 2 * py + pc] if scatter[t] else ins[t]
    return pltpu.make_async_remote_copy(
        src_ref=src, dst_ref=lands[t].at[me], send_sem=send_sems.at[t * N_DEV + kk],
        recv_sem=recv_sems.at[t * N_DEV + kk], device_id=(px, py, pc), device_id_type=pl.DeviceIdType.MESH)


def _own_copy(t, scatter, ins, lands, own_sems):
    me = _my_index()
    return pltpu.make_async_copy(ins[t].at[me] if scatter[t] else ins[t], lands[t].at[me], own_sems.at[t])


_HBM = pl.BlockSpec(memory_space=pltpu.HBM)
_SEM = pl.BlockSpec(memory_space=pltpu.SEMAPHORE)
_EFFECT = pltpu.SideEffectType.DATAFLOW_SIDE_EFFECTING


def exchange_start(arrays, scatter, *, name):
    nt = len(arrays)
    land_shapes = [a.shape if scatter[t] else (N_DEV,) + a.shape for t, a in enumerate(arrays)]

    def body(*refs):
        ins, lands = refs[:nt], refs[nt:2 * nt]
        send_sems, recv_sems, own_sems = refs[2 * nt:2 * nt + 3]
        token = refs[-1]
        for kk in range(1, N_DEV):
            for t in range(nt):
                _peer_copy(t, kk, scatter, ins, lands, send_sems, recv_sems).start()
        for t in range(nt):
            _own_copy(t, scatter, ins, lands, own_sems).start()
        token[...] = jnp.zeros_like(token)

    sems = pltpu.SemaphoreType.DMA((nt * N_DEV,))
    outs = pl.pallas_call(
        body, name=name,
        out_shape=(sems, sems, pltpu.SemaphoreType.DMA((nt,)), *[pltpu.HBM(a.shape, a.dtype) for a in arrays],
                   *[pltpu.HBM(shp, a.dtype) for shp, a in zip(land_shapes, arrays)],
                   jax.ShapeDtypeStruct((8, 128), F32)),
        in_specs=[_HBM] * (2 * nt),
        out_specs=(_SEM, _SEM, _SEM, *[_HBM] * (2 * nt), pl.BlockSpec(memory_space=pltpu.VMEM)),
        input_output_aliases={i: 3 + i for i in range(2 * nt)},
        compiler_params=pltpu.CompilerParams(has_side_effects=_EFFECT),
    )(*[pltpu.with_memory_space_constraint(a, pltpu.HBM) for a in arrays],
      *[pltpu.with_memory_space_constraint(lax.empty(shp, a.dtype), pltpu.HBM) for shp, a in zip(land_shapes, arrays)])
    return (outs[:3], outs[3:3 + nt], outs[3 + nt:3 + 2 * nt], scatter), outs[-1]


def exchange_wait(handle, after, *, name):
    sems, thru, lands, scatter = handle
    nt = len(thru)

    def body(*refs):
        ins, lnd = refs[:nt], refs[nt:2 * nt]
        s_sems, r_sems, o_sems = refs[2 * nt:2 * nt + 3]
        for kk in range(1, N_DEV):
            for t in range(nt):
                cp = _peer_copy(t, kk, scatter, ins, lnd, s_sems, r_sems)
                cp.wait_send()
                cp.wait_recv()
        for t in range(nt):
            _own_copy(t, scatter, ins, lnd, o_sems).wait()

    outs = pl.pallas_call(
        body, name=name,
        out_shape=(*[pltpu.HBM(a.shape, a.dtype) for a in thru], *[pltpu.HBM(a.shape, a.dtype) for a in lands]),
        in_specs=[_HBM] * (2 * nt) + [_SEM, _SEM, _SEM, pl.BlockSpec(memory_space=pl.ANY)],
        out_specs=tuple([_HBM] * (2 * nt)),
        input_output_aliases={i: i for i in range(2 * nt)},
        compiler_params=pltpu.CompilerParams(has_side_effects=_EFFECT),
    )(*thru, *lands, *sems, after)
    return outs[nt:]


def _cols_from_shards(g):
    g = jnp.moveaxis(g, 0, -2)
    return g.reshape(g.shape[:-2] + (g.shape[-2] * g.shape[-1],))


def _cols_to_shards(w, nshards=N_DEV):
    w = w.reshape(w.shape[:-1] + (nshards, w.shape[-1] // nshards))
    return jnp.moveaxis(w, -2, 0)


def _half_shards(halves):
    return jnp.concatenate([_cols_to_shards(h[None], N_DEV // 2) for h in halves], axis=0)


def _ffn_fwd(x, h, w_up, cw, cb, get_w_down, down, tag):
    act, up_g, up_v, conv_g, conv_v = ffn_up_mid(h, w_up[0], w_up[1], cw, cb, name=f"{tag}_up_mid")
    w_down = get_w_down(act)
    return down(act, w_down, x), (h, up_g, up_v, conv_g, conv_v, act), w_down


def _ffn_bwd(dx, dxb, x, saved, g, w_up, cw, w_down, tag):
    h, up_g, up_v, conv_g, conv_v, act = saved
    dact = matmul(dxb, w_down, tb=True, name=f"{tag}_ddown")
    d_w_down = matmul_ta(act, dxb, name=f"{tag}_gdown")
    dug, duv, dcw, dcb = ffn_mid_bwd(dact, up_g, up_v, conv_g, conv_v, cw, name=f"{tag}_dmid")
    d_w_up = (matmul_ta(h, dug, name=f"{tag}_gup_g"), matmul_ta(h, duv, name=f"{tag}_gup_v"))
    dh = matmul(dug, w_up[0], tb=True, name=f"{tag}_dup_g")
    dx2, dxb2, dg = matmul_norm_bwd(duv, w_up[1], x, g, dx, res=dh, name=f"{tag}_dup_v_dnorm")
    return dx2, dxb2, dict(norm_g=dg, w_up=d_w_up, conv_w=dcw, conv_b=dcb, w_down=d_w_down)


def local_step(x0, tgt, a, weights, grads_out):
    s, d = x0.shape
    aw = a['even_v_ln_g'].shape[-1]
    causal = jnp.tril(jnp.ones((CHUNK, CHUNK), dtype=bool))
    wm = jnp.where(causal, a['even_w_s'][0], 0.0).astype(BF16)
    wmt = jnp.swapaxes(wm, 1, 2)
    bm = jnp.repeat(a['even_b_s'][0].T, HEAD, axis=1)
    sel = (jnp.arange(aw)[:, None] // HEAD == jnp.arange(128)[None, :]).astype(BF16)
    cos, sin = rope_tables(s)
    ffn_g, ffn_cb = a['ffn_norm_g'], a['ffn_conv_b']

    w0 = weights(0, None)
    w_in, conv_w, odd_g, ffn_cw = w0['w_in'], w0['conv_w'], w0['odd_g'], w0['ffn_cw']
    h0 = rms_fwd(x0, w0['even_g'], name="even_norm")
    z = matmul(h0, w_in, bias=a['even_b_in'], name="even_in")
    ycat, hc = even_mid_fwd(z, a['even_v_ln_g'], a['even_v_ln_b'], wm, bm, conv_w, a['even_conv_b'],
                            a['even_conv_ln_g'], a['even_conv_ln_b'], name="even_mid")
    w1 = weights(1, ycat)
    x1, h1 = matmul(ycat, w1['w_out'], res=x0, norm_g=ffn_g[0:1], name="even_out")
    (x2, h2), ffn0, w_down0 = _ffn_fwd(
        x1, h1, w1['w_up'], ffn_cw[0], ffn_cb[0:1], lambda act: weights(2, act)['w_down'],
        lambda act, w_down, x: matmul(act, w_down, res=x, norm_g=odd_g, name="ffn0_down"), "ffn0")
    w2 = weights(3, h2)
    qkv = qkv_rope(h2, w2['w_qkv'], cos, sin, name="odd_qkv_rope")
    o, lse = attn_fwd(qkv, name="attn_fwd")
    x3, h3 = matmul(o, w2['w_o'], res=x2, norm_g=ffn_g[1:2], name="odd_out")
    w3 = weights(4, x3)
    final_g = a['final_norm_g'].reshape(1, -1)
    (loss_t, dx, dxb, d_final_g), ffn1, _ = _ffn_fwd(
        x3, h3, w3['w_up'], ffn_cw[1], ffn_cb[1:2], lambda act: w3['w_down'],
        lambda act, w_down, x: final_loss_bwd(act, w_down, x, final_g, tgt, name="ffn1_down_loss"), "ffn1")

    dx, dxb, g1 = _ffn_bwd(dx, dxb, x3, ffn1, ffn_g[1:2], w3['w_up'], ffn_cw[1], w3['w_down'], "ffn1")
    dep = grads_out(0, dict(w_up=g1['w_up'], w_down=g1['w_down']))
    do = matmul(dxb, w2['w_o'], tb=True, dep=dep, name="odd_dout")
    d_w_o = matmul_ta(o, dxb, name="odd_gout")
    delta = attn_delta(do, o, name="attn_delta")
    dq = attn_dq(qkv, do, lse, delta, name="attn_dq")
    dk, dv = attn_dkv(qkv, do, lse, delta, name="attn_dkv")
    dqkv = rope_bwd(dq, dk, dv, cos, sin, name="rope_bwd")
    d_w_qkv = matmul_ta(h2, dqkv, name="odd_gqkv")
    dep = grads_out(1, dict(w_qkv=d_w_qkv, w_o=d_w_o))
    dx, dxb, d_odd_g = matmul_norm_bwd(dqkv, w2['w_qkv'], x2, odd_g, dx, dep=dep, name="odd_dqkv_dnorm")
    dx, dxb, g0 = _ffn_bwd(dx, dxb, x1, ffn0, ffn_g[0:1], w1['w_up'], ffn_cw[0], w_down0, "ffn0")
    dep = grads_out(2, dict(w_up=g0['w_up'], w_down=g0['w_down']))
    d_w_out = matmul_ta(ycat, dxb, dep=dep, name="even_gout")
    dep = grads_out(3, dict(w_out=d_w_out))
    dycat = matmul(dxb, w1['w_out'], tb=True, dep=dep, name="even_dout")
    (dza, dhc, dba, dvg, dvb, dwm, dbs, dcg, dcbeta, dcb) = even_mid_bwd_rows(
        dycat, z, hc, a['even_v_ln_g'], a['even_v_ln_b'], wm, wmt, bm, sel, a['even_conv_ln_g'],
        a['even_conv_ln_b'], name="even_dmid_rows")
    dzb, dbb, dcw = even_conv_bwd(dhc, z, conv_w, name="even_dmid_conv")
    nh = a['even_w_s'].shape[1]
    small_grads = {
        'even_b_in': jnp.concatenate([dba, dbb], axis=1), 'even_v_ln_g': dvg,
        'even_v_ln_b': dvb, 'even_w_s': jnp.where(causal, dwm, 0.0)[None], 'even_b_s': dbs[:, :nh].T[None],
        'even_conv_w': dcw[None], 'even_conv_b': dcb, 'even_conv_ln_g': dcg, 'even_conv_ln_b': dcbeta,
        'odd_norm_g': d_odd_g, 'ffn_norm_g': jnp.concatenate([g0['norm_g'], g1['norm_g']], axis=0),
        'ffn_conv_w': jnp.stack([g0['conv_w'], g1['conv_w']]),
        'ffn_conv_b': jnp.concatenate([g0['conv_b'], g1['conv_b']], axis=0),
        'final_norm_g': d_final_g.reshape(-1),
    }
    dep = grads_out(5, dict(small=small_grads))
    d_w_in = (matmul_ta(h0, dza, dep=dep, name="even_gin_a"), matmul_ta(h0, dzb, name="even_gin_b"))
    dep = grads_out(4, dict(w_in=d_w_in))
    dh0 = matmul(dza, w_in, tb=True, bk=0, dep=dep, name="even_din_a")
    grad_x, _, d_even_g = matmul_norm_bwd(dzb, w_in, x0, w0['even_g'], dx, bk=1, res=dh0, name="even_din_b_dnorm")
    last = {'even_norm_g': d_even_g}
    grads_out(6, dict(small=last))
    return loss_t, grad_x, {**last, **small_grads}


BIG = ['even_w_in', 'even_w_out', 'odd_w_qkv', 'odd_w_o', 'ffn_w_up', 'ffn_w_down']


def _as_tiles(flat, dtype=F32):
    return jnp.pad(flat, (0, (-flat.size) % 2048)).reshape(-1, 128).astype(dtype)


def kernel(*args):
    a = dict(zip(NAMES + ['loss_target'] + ['m_' + n for n in WEIGHTS] + ['v_' + n for n in WEIGHTS], args))
    x0 = a['x'][0]
    tgt = a['loss_target'][0]
    s, d = x0.shape
    me = _my_index()
    bf = lambda t: t.astype(BF16)

    small_local = _as_tiles(jnp.concatenate([a['even_conv_w'].reshape(-1), a['odd_norm_g'].reshape(-1),
                                             a['ffn_conv_w'].reshape(-1)]))
    stage_arrays = [
        [bf(a['even_w_in']), small_local],
        [bf(a['even_w_out']), bf(a['ffn_w_up'][0:1])],
        [bf(a['ffn_w_down'][0:1])],
        [bf(a['odd_w_qkv']), bf(a['odd_w_o'])],
        [bf(a['ffn_w_up'][1:2]), bf(a['ffn_w_down'][1:2])],
    ]
    started = [exchange_start(arrs, [False] * len(arrs), name=f"gather{i}_start") for i, arrs in enumerate(stage_arrays)]
    order = sum(tok[0, 0] for _, tok in started)

    def weights(stage, after):
        handle, tok = started[stage]
        full = exchange_wait(handle, tok if after is None else after, name=f"gather{stage}_wait")
        rows = lambda g: jnp.moveaxis(g, 0, 1).reshape(-1, d)
        halves = lambda g: (_cols_from_shards(g[:N_DEV // 2])[0], _cols_from_shards(g[N_DEV // 2:])[0])
        if stage == 0:
            gs = full[1].reshape(N_DEV, -1)
            n_cw, n_og, n_fw = a['even_conv_w'].size, a['odd_norm_g'].size, a['ffn_conv_w'].size
            return dict(
                w_in=_cols_from_shards(full[0])[0], even_g=a['even_norm_g'] + order,
                conv_w=_cols_from_shards(gs[:, :n_cw].reshape((N_DEV,) + a['even_conv_w'].shape))[0],
                odd_g=gs[:, n_cw:n_cw + n_og].reshape(1, -1),
                ffn_cw=_cols_from_shards(gs[:, n_cw + n_og:n_cw + n_og + n_fw].reshape((N_DEV,) + a['ffn_conv_w'].shape)))
        if stage == 1:
            return dict(w_out=rows(full[0]), w_up=halves(full[1]))
        if stage == 2:
            return dict(w_down=rows(full[0]))
        if stage == 3:
            return dict(w_qkv=_cols_from_shards(full[0])[0], w_o=rows(full[1]))
        return dict(w_up=halves(full[0]), w_down=rows(full[1]))

    sent = {}

    def grads_out(stage, g):
        to_rows = lambda w: w.reshape(N_DEV, 1, -1, d)
        if stage in (0, 2):
            pieces, scatter = [_half_shards(g['w_up']), to_rows(g['w_down'])], [True, True]
        elif stage == 1:
            pieces, scatter = [_cols_to_shards(g['w_qkv'][None]), to_rows(g['w_o'])], [True, True]
        elif stage == 3:
            pieces, scatter = [to_rows(g['w_out'])], [True]
        elif stage == 4:
            pieces, scatter = [_half_shards(g['w_in'])], [True]
        else:
            small = jnp.concatenate([g['small'][n].reshape(-1) for n in g['small']])
            pieces, scatter = [_as_tiles(small, BF16)], [False]
        sent[stage], tok = exchange_start(pieces, scatter, name=f"grads{stage}_start")
        return tok

    loss_t, grad_x, small_grads = local_step(x0, tgt, a, weights, grads_out)
    loss = lax.psum(loss_t[0, 0], ("x", "y", "c"))
    received = {stage: exchange_wait(handle, grad_x, name=f"grads{stage}_wait") for stage, handle in sent.items()}

    results = {}
    big_parts = {'even_w_in': [received[4][0]], 'even_w_out': [received[3][0]], 'odd_w_qkv': [received[1][0]],
                 'odd_w_o': [received[1][1]], 'ffn_w_up': [received[2][0], received[0][0]],
                 'ffn_w_down': [received[2][1], received[0][1]]}
    for n in BIG:
        shp = a[n].shape
        flat = lambda t: t.reshape(-1, shp[-1])
        outs = adamw([p.reshape(N_DEV, -1, shp[-1]) for p in big_parts[n]], flat(a[n]), flat(a['m_' + n]),
                     flat(a['v_' + n]), name=f"adamw_{n}")
        results[n] = [t.reshape(shp) for t in outs]

    small_names = list(small_grads)
    n_small = sum(small_grads[n].size for n in small_names)
    n_last = small_grads[small_names[0]].size
    rs = jnp.concatenate([received[6][0].reshape(N_DEV, -1)[:, :n_last],
                          received[5][0].reshape(N_DEV, -1)[:, :n_small - n_last]], axis=1)
    parts, offs = [], 0
    for n in small_names:
        full = small_grads[n].shape
        piece = rs[:, offs:offs + small_grads[n].size].reshape((N_DEV,) + full)
        offs += small_grads[n].size
        shp = a[n].shape
        if shp != full:
            width = shp[-1]
            piece = lax.dynamic_slice_in_dim(piece, me * width, width, axis=piece.ndim - 1)
        parts.append(piece.reshape(N_DEV, -1))
    parts = jnp.concatenate(parts, axis=1)
    pad = (-parts.shape[1]) % 2048
    cat = lambda pre: _as_tiles(jnp.concatenate([a[pre + n].reshape(-1) for n in small_names]))
    outs = adamw([jnp.pad(parts, ((0, 0), (0, pad))).reshape(N_DEV, -1, 128)], cat(''), cat('m_'), cat('v_'),
                 name="adamw_small")
    offs = 0
    for n in small_names:
        size = a[n].size
        results[n] = [t.reshape(-1)[offs:offs + size].reshape(a[n].shape) for t in outs]
        offs += size

    out = [loss, grad_x[None]]
    for i in range(4):
        out += [results[n][i] for n in WEIGHTS]
    return tuple(out)
```

```python
import math

import jax
import jax.numpy as jnp
from jax import lax
from jax.experimental import pallas as pl
from jax.experimental.pallas import tpu as pltpu

F32 = jnp.float32
BF16 = jnp.bfloat16

N_DEV = 8
EPS = 1e-6
NEG = -1e30
HEAD = 64
CHUNK = 128
BLOCK = 128
CONV_K = 31
FFN_K = 3
DILATIONS = (1, 4, 16)
ROPE_THETA = 10000.0
LR, B1, B2, ADAM_EPS, WD, STEP = 0.001, 0.9, 0.999, 1e-08, 0.01, 10

VMEM_LIMIT = 56 * 1024 * 1024
VMEM_BUDGET = 32 * 1024 * 1024
ROWS = 256
HALO = 32
FHALO = 8

NAMES = ['x', 'even_norm_g', 'even_w_in', 'even_b_in', 'even_v_ln_g', 'even_v_ln_b', 'even_w_s', 'even_b_s',
         'even_conv_w', 'even_conv_b', 'even_conv_ln_g', 'even_conv_ln_b', 'even_w_out', 'odd_norm_g',
         'odd_w_qkv', 'odd_w_o', 'ffn_norm_g', 'ffn_w_up', 'ffn_conv_w', 'ffn_conv_b', 'ffn_w_down',
         'final_norm_g']
WEIGHTS = NAMES[1:]


def _params(sem=None):
    return pltpu.CompilerParams(dimension_semantics=sem, vmem_limit_bytes=VMEM_LIMIT)


def _sigmoid(x):
    return 1.0 / (1.0 + jnp.exp(-x))


def _gelu(x):
    c = math.sqrt(2.0 / math.pi)
    return 0.5 * x * (1.0 + jnp.tanh(c * (x + 0.044715 * x * x * x)))


def _gelu_grad(x):
    c = math.sqrt(2.0 / math.pi)
    t = jnp.tanh(c * (x + 0.044715 * x * x * x))
    return 0.5 * (1.0 + t) + 0.5 * x * (1.0 - t * t) * c * (1.0 + 3.0 * 0.044715 * x * x)


def _ln_stats(x):
    mu = jnp.mean(x, axis=-1, keepdims=True)
    xc = x - mu
    rstd = lax.rsqrt(jnp.mean(xc * xc, axis=-1, keepdims=True) + EPS)
    return xc * rstd, rstd


def _ln_bwd(dy, xhat, rstd, g):
    dxh = dy * g
    return rstd * (dxh - jnp.mean(dxh, axis=-1, keepdims=True) - xhat * jnp.mean(dxh * xhat, axis=-1, keepdims=True))


def _colsum(x):
    return jnp.sum(x, axis=0, keepdims=True)


def _split3(x):
    hi = x.astype(BF16)
    r = x - hi.astype(F32)
    mid = r.astype(BF16)
    lo = (r - mid.astype(F32)).astype(BF16)
    return hi, mid, lo


def _dot(a, b, dims):
    return lax.dot_general(a, b, (dims, ((), ())), preferred_element_type=F32)


NN = ((1,), (0,))
NT = ((1,), (1,))
TN = ((0,), (0,))


def _divisors(n, cands):
    return [c for c in cands if c <= n and n % c == 0]


def _pick_tiles(m, n, k, a_bytes, b_bytes, o_bytes, extra_bytes):
    best = None
    for tm in _divisors(m, (1024, 512, 256, 128)):
        for tn in _divisors(n, (1408, 1024, 768, 704, 512, 384, 256, 128)):
            if tn % 128:
                continue
            need = 2 * (tm * k * a_bytes + k * tn * b_bytes + tm * tn * (o_bytes + extra_bytes)) + tm * tn * 4
            if need <= VMEM_BUDGET and (best is None or tm * tn > best[0] * best[1]):
                best = (tm, tn)
    assert best is not None, (m, n, k)
    return best


def matmul(a, b, *, tb=False, bk=0, bias=None, res=None, norm_g=None, dep=None, out_dtype=F32, name):
    m, k = a.shape
    n = b.shape[0] if tb else b.shape[1]
    assert b.shape[1] % k == 0 if tb else (b.shape[0] == k and bk == 0)
    if norm_g is None:
        tm, tn = _pick_tiles(m, n, k, a.dtype.itemsize, b.dtype.itemsize, jnp.dtype(out_dtype).itemsize,
                             4 if res is not None else 0)
    else:
        tm, tn = _divisors(m, (512,))[0], n

    def body(*refs):
        a_ref, b_ref = refs[:2]
        o_ref = refs[-2] if norm_g is not None else refs[-1]
        acc = _dot(a_ref[...].astype(BF16), b_ref[...].astype(BF16), NT if tb else NN)
        pos = 2
        if bias is not None:
            acc = acc + refs[pos][...]
            pos += 1
        if res is not None:
            acc = acc + refs[pos][...]
            pos += 1
        o_ref[...] = acc.astype(out_dtype)
        if norm_g is not None:
            r = lax.rsqrt(jnp.mean(acc * acc, axis=-1, keepdims=True) + EPS)
            refs[-1][...] = (acc * r * refs[pos][...]).astype(BF16)

    in_specs = [pl.BlockSpec((tm, k), lambda i, j: (i, 0)),
                pl.BlockSpec((tn, k), lambda i, j: (j, bk)) if tb else pl.BlockSpec((k, tn), lambda i, j: (0, j))]
    args = [a, b]
    if bias is not None:
        in_specs.append(pl.BlockSpec((1, tn), lambda i, j: (0, j)))
        args.append(bias)
    if res is not None:
        in_specs.append(pl.BlockSpec((tm, tn), lambda i, j: (i, j)))
        args.append(res)
    if norm_g is not None:
        in_specs.append(pl.BlockSpec((1, tn), lambda i, j: (0, j)))
        args.append(norm_g)
    if dep is not None:
        in_specs.append(pl.BlockSpec(memory_space=pl.ANY))
        args.append(dep)
    blk = pl.BlockSpec((tm, tn), lambda i, j: (i, j))
    out_shape = jax.ShapeDtypeStruct((m, n), out_dtype)
    return pl.pallas_call(
        body, name=name, grid=(m // tm, n // tn), in_specs=in_specs,
        out_specs=blk if norm_g is None else [blk, blk],
        out_shape=out_shape if norm_g is None else [out_shape, jax.ShapeDtypeStruct((m, n), BF16)],
        compiler_params=_params(("parallel", "parallel")))(*args)


def matmul_norm_bwd(a, b, x, g, dres, *, bk=0, res=None, dep=None, name):
    m, k = a.shape
    n = b.shape[0]
    tm = _divisors(m, (512,))[0]

    def body(*refs):
        a_ref, b_ref, x_ref, g_ref, dres_ref = refs[:5]
        dx_ref, dxb_ref, dg_ref = refs[-3:]
        dhv = _dot(a_ref[...].astype(BF16), b_ref[...].astype(BF16), NT)
        if res is not None:
            dhv = dhv + refs[5][...]
        xv = x_ref[...]
        r = lax.rsqrt(jnp.mean(xv * xv, axis=-1, keepdims=True) + EPS)
        xhat = xv * r
        dxh = dhv * g_ref[...]
        dx = dres_ref[...] + r * (dxh - xhat * jnp.mean(dxh * xhat, axis=-1, keepdims=True))
        dx_ref[...] = dx
        dxb_ref[...] = dx.astype(BF16)

        @pl.when(pl.program_id(0) == 0)
        def _():
            dg_ref[...] = jnp.zeros_like(dg_ref)
        dg_ref[...] += _colsum(dhv * xhat)

    row = pl.BlockSpec((tm, n), lambda i: (i, 0))
    vec = pl.BlockSpec((1, n), lambda i: (0, 0))
    in_specs = [pl.BlockSpec((tm, k), lambda i: (i, 0)), pl.BlockSpec((n, k), lambda i: (0, bk)), row, vec, row]
    args = [a, b, x, g, dres]
    if res is not None:
        in_specs.append(row)
        args.append(res)
    if dep is not None:
        in_specs.append(pl.BlockSpec(memory_space=pl.ANY))
        args.append(dep)
    return pl.pallas_call(
        body, name=name, grid=(m // tm,), in_specs=in_specs, out_specs=[row, row, vec],
        out_shape=[jax.ShapeDtypeStruct((m, n), F32), jax.ShapeDtypeStruct((m, n), BF16),
                   jax.ShapeDtypeStruct((1, n), F32)],
        compiler_params=_params(("arbitrary",)))(*args)


def matmul_ta(a, b, *, dep=None, out_dtype=BF16, name):
    s, m = a.shape
    n = b.shape[1]
    assert b.shape[0] == s
    best = None
    for tm in _divisors(m, (512, 256, 128)):
        for tn in _divisors(n, (1024, 512, 384, 256, 128)):
            need = 2 * (s * tm * a.dtype.itemsize + s * tn * b.dtype.itemsize + tm * tn * 2) + tm * tn * 4
            if need <= VMEM_BUDGET and (best is None or tm * tn > best[0] * best[1]):
                best = (tm, tn)
    tm, tn = best

    def body(*refs):
        a_ref, b_ref, o_ref = refs[0], refs[1], refs[-1]
        o_ref[...] = _dot(a_ref[...].astype(BF16), b_ref[...].astype(BF16), TN).astype(out_dtype)

    in_specs = [pl.BlockSpec((s, tm), lambda i, j: (0, i)), pl.BlockSpec((s, tn), lambda i, j: (0, j))]
    args = [a, b]
    if dep is not None:
        in_specs.append(pl.BlockSpec(memory_space=pl.ANY))
        args.append(dep)
    return pl.pallas_call(
        body, name=name, grid=(m // tm, n // tn), in_specs=in_specs,
        out_specs=pl.BlockSpec((tm, tn), lambda i, j: (i, j)),
        out_shape=jax.ShapeDtypeStruct((m, n), out_dtype),
        compiler_params=_params(("parallel", "parallel")))(*args)


def rms_fwd(x, g, *, name):
    s, d = x.shape

    def body(x_ref, g_ref, h_ref):
        xv = x_ref[...]
        r = lax.rsqrt(jnp.mean(xv * xv, axis=-1, keepdims=True) + EPS)
        h_ref[...] = (xv * r * g_ref[...]).astype(BF16)

    return pl.pallas_call(
        body, name=name, grid=(s // ROWS,),
        in_specs=[pl.BlockSpec((ROWS, d), lambda i: (i, 0)), pl.BlockSpec((1, d), lambda i: (0, 0))],
        out_specs=pl.BlockSpec((ROWS, d), lambda i: (i, 0)),
        out_shape=jax.ShapeDtypeStruct((s, d), BF16),
        compiler_params=_params(("parallel",)))(x, g)


def final_loss_bwd(a, b, res, g, tgt, *, name):
    s, k = a.shape
    d = b.shape[1]

    def body(a_ref, b_ref, r_ref, g_ref, t_ref, loss_ref, dx_ref, dxb_ref, dg_ref):
        xv = _dot(a_ref[...], b_ref[...], NN) + r_ref[...]
        gv = g_ref[...]
        r = lax.rsqrt(jnp.mean(xv * xv, axis=-1, keepdims=True) + EPS)
        xhat = xv * r
        e = xhat * gv - t_ref[...]
        dy = e * (1.0 / d)
        dxh = dy * gv
        dx = r * (dxh - xhat * jnp.mean(dxh * xhat, axis=-1, keepdims=True))
        dx_ref[...] = dx
        dxb_ref[...] = dx.astype(BF16)

        @pl.when(pl.program_id(0) == 0)
        def _():
            dg_ref[...] = jnp.zeros_like(dg_ref)
            loss_ref[...] = jnp.zeros_like(loss_ref)
        dg_ref[...] += _colsum(dy * xhat)
        loss_ref[...] += 0.5 * jnp.sum(jnp.mean(e * e, axis=-1, keepdims=True))

    row = pl.BlockSpec((ROWS, d), lambda i: (i, 0))
    vec = pl.BlockSpec((1, d), lambda i: (0, 0))
    one = pl.BlockSpec((8, 128), lambda i: (0, 0))
    return pl.pallas_call(
        body, name=name, grid=(s // ROWS,),
        in_specs=[pl.BlockSpec((ROWS, k), lambda i: (i, 0)), pl.BlockSpec((k, d), lambda i: (0, 0)), row, vec, row],
        out_specs=[one, row, row, vec],
        out_shape=[jax.ShapeDtypeStruct((8, 128), F32), jax.ShapeDtypeStruct((s, d), F32),
                   jax.ShapeDtypeStruct((s, d), BF16), jax.ShapeDtypeStruct((1, d), F32)],
        compiler_params=_params(("arbitrary",)))(a, b, res, g, tgt)


def _pair_masks():
    lane = lax.broadcasted_iota(jnp.int32, (CHUNK, 128), 1)
    return lane < HEAD


def _head_keep(rows):
    lane = lax.broadcasted_iota(jnp.int32, (rows, 128), 1)
    first = jnp.where(lane < HEAD, 1.0, 0.0)
    return first.astype(BF16), (1.0 - first).astype(BF16)


def _gating_mixed(vn_b, wm_ref, lo):
    rows, aw = vn_b.shape
    out = []
    for c in range(rows // CHUNK):
        tiles = []
        for p in range(aw // 128):
            vp = vn_b[c * CHUNK:(c + 1) * CHUNK, p * 128:(p + 1) * 128]
            r0 = _dot(wm_ref[2 * p], vp, NN)
            r1 = _dot(wm_ref[2 * p + 1], vp, NN)
            tiles.append(jnp.where(lo, r0, r1))
        out.append(jnp.concatenate(tiles, axis=1))
    return jnp.concatenate(out, axis=0)


def even_mid_fwd(z, vg, vb, wm, bm, cw, cb, cg, cbeta, *, name):
    s, zw = z.shape
    aw = zw // 4
    nblk = s // ROWS

    def body(z_ref, zp_ref, vg_ref, vb_ref, wm_ref, bm_ref, cw_ref, cb_ref, cg_ref, cbeta_ref,
             y_ref, hc_ref, ext_ref):
        i = pl.program_id(0)
        lo = _pair_masks()
        u = _gelu(z_ref[:, 0:aw])
        v = _gelu(z_ref[:, aw:2 * aw])
        vhat, _ = _ln_stats(v)
        vn = (vhat * vg_ref[...] + vb_ref[...]).astype(BF16)
        mixed = _gating_mixed(vn, wm_ref, lo)
        bias = jnp.concatenate([bm_ref[...]] * (ROWS // CHUNK), axis=0)
        y_ref[:, 0:aw] = (u * (mixed + bias)).astype(BF16)

        hb = z_ref[:, 2 * aw:3 * aw] * _sigmoid(z_ref[:, 3 * aw:4 * aw])
        hbp = zp_ref[:, 0:aw] * _sigmoid(zp_ref[:, aw:2 * aw])
        ext_ref[0:HALO, :] = jnp.where(i > 0, hbp, 0.0)
        ext_ref[HALO:HALO + ROWS, :] = hb
        acc = jnp.zeros((ROWS, aw), F32) + cb_ref[...]
        for k in range(CONV_K):
            acc = acc + cw_ref[k:k + 1, :] * ext_ref[pl.ds(HALO - (CONV_K - 1) + k, ROWS), :]
        hc_ref[...] = acc
        hhat, _ = _ln_stats(acc)
        hn = hhat * cg_ref[...] + cbeta_ref[...]
        y_ref[:, aw:2 * aw] = (hn * _sigmoid(hn)).astype(BF16)

    hb_per = ROWS // HALO
    vec = pl.BlockSpec((1, aw), lambda i: (0, 0))
    return pl.pallas_call(
        body, name=name, grid=(nblk,),
        in_specs=[pl.BlockSpec((ROWS, zw), lambda i: (i, 0)),
                  pl.BlockSpec((HALO, 2 * aw), lambda i: (jnp.maximum(i * hb_per - 1, 0), 1)),
                  vec, vec,
                  pl.BlockSpec(wm.shape, lambda i: (0, 0, 0)),
                  pl.BlockSpec((CHUNK, aw), lambda i: (0, 0)),
                  pl.BlockSpec((CONV_K, aw), lambda i: (0, 0)), vec, vec, vec],
        out_specs=[pl.BlockSpec((ROWS, 2 * aw), lambda i: (i, 0)), pl.BlockSpec((ROWS, aw), lambda i: (i, 0))],
        out_shape=[jax.ShapeDtypeStruct((s, 2 * aw), BF16), jax.ShapeDtypeStruct((s, aw), F32)],
        scratch_shapes=[pltpu.VMEM((HALO + ROWS, aw), F32)],
        compiler_params=_params(("parallel",)))(z, z, vg, vb, wm, bm, cw, cb, cg, cbeta)


def even_mid_bwd_rows(dy, z, hc, vg, vb, wm, wmt, bm, sel, cg, cbeta, *, name):
    s, zw = z.shape
    aw = zw // 4
    nh = wm.shape[0]

    def body(dy_ref, z_ref, hc_ref, vg_ref, vb_ref, wm_ref, wmt_ref, bm_ref, sel_ref, cg_ref, cbeta_ref,
             dza_ref, dhc_ref, dba_ref, dvg_ref, dvb_ref, dwm_ref, dbs_ref, dcg_ref, dcbeta_ref, dcb_ref):
        @pl.when(pl.program_id(0) == 0)
        def _():
            for r in (dba_ref, dvg_ref, dvb_ref, dwm_ref, dbs_ref, dcg_ref, dcbeta_ref, dcb_ref):
                r[...] = jnp.zeros_like(r)

        lo = _pair_masks()
        keep = _head_keep(CHUNK)
        zu = z_ref[:, 0:aw]
        zv = z_ref[:, aw:2 * aw]
        u = _gelu(zu)
        v = _gelu(zv)
        vhat, vrstd = _ln_stats(v)
        vn = (vhat * vg_ref[...] + vb_ref[...]).astype(BF16)
        mixed = _gating_mixed(vn, wm_ref, lo)
        bias = jnp.concatenate([bm_ref[...]] * (ROWS // CHUNK), axis=0)
        dya = dy_ref[:, 0:aw]
        du = dya * (mixed + bias)
        dmix = dya * u
        dmix_b = dmix.astype(BF16)

        dvn_rows = []
        for c in range(ROWS // CHUNK):
            rs = slice(c * CHUNK, (c + 1) * CHUNK)
            tiles = []
            for p in range(aw // 128):
                cs = slice(p * 128, (p + 1) * 128)
                dm = dmix_b[rs, cs]
                dm0 = dm * keep[0]
                dm1 = dm * keep[1]
                vp = vn[rs, cs]
                tiles.append(_dot(wmt_ref[2 * p], dm0, NN) + _dot(wmt_ref[2 * p + 1], dm1, NN))
                dwm_ref[2 * p] += _dot(dm0, vp, NT)
                dwm_ref[2 * p + 1] += _dot(dm1, vp, NT)
            dvn_rows.append(jnp.concatenate(tiles, axis=1))
            acc = jnp.zeros((CHUNK, 128), F32)
            for part in _split3(dmix[rs, :]):
                acc = acc + _dot(part, sel_ref[...], NN)
            dbs_ref[...] += acc
        dvn = jnp.concatenate(dvn_rows, axis=0)
        dvg_ref[...] += _colsum(dvn * vhat)
        dvb_ref[...] += _colsum(dvn)
        dv = _ln_bwd(dvn, vhat, vrstd, vg_ref[...])
        dzu = du * _gelu_grad(zu)
        dzv = dv * _gelu_grad(zv)
        dza_ref[:, 0:aw] = dzu.astype(BF16)
        dza_ref[:, aw:2 * aw] = dzv.astype(BF16)
        dba_ref[:, 0:aw] += _colsum(dzu)
        dba_ref[:, aw:2 * aw] += _colsum(dzv)

        hcv = hc_ref[...]
        hhat, hrstd = _ln_stats(hcv)
        hn = hhat * cg_ref[...] + cbeta_ref[...]
        sg = _sigmoid(hn)
        dhn = dy_ref[:, aw:2 * aw] * (sg * (1.0 + hn * (1.0 - sg)))
        dcg_ref[...] += _colsum(dhn * hhat)
        dcbeta_ref[...] += _colsum(dhn)
        dhc = _ln_bwd(dhn, hhat, hrstd, cg_ref[...])
        dhc_ref[...] = dhc
        dcb_ref[...] += _colsum(dhc)

    vec = pl.BlockSpec((1, aw), lambda i: (0, 0))
    vec2 = pl.BlockSpec((1, 2 * aw), lambda i: (0, 0))
    w3 = pl.BlockSpec(wm.shape, lambda i: (0, 0, 0))
    sq = pl.BlockSpec((CHUNK, 128), lambda i: (0, 0))
    return pl.pallas_call(
        body, name=name, grid=(s // ROWS,),
        in_specs=[pl.BlockSpec((ROWS, 2 * aw), lambda i: (i, 0)), pl.BlockSpec((ROWS, 2 * aw), lambda i: (i, 0)),
                  pl.BlockSpec((ROWS, aw), lambda i: (i, 0)), vec, vec, w3, w3,
                  pl.BlockSpec((CHUNK, aw), lambda i: (0, 0)), pl.BlockSpec((aw, 128), lambda i: (0, 0)), vec, vec],
        out_specs=[pl.BlockSpec((ROWS, 2 * aw), lambda i: (i, 0)), pl.BlockSpec((ROWS, aw), lambda i: (i, 0)),
                   vec2, vec, vec, w3, sq, vec, vec, vec],
        out_shape=[jax.ShapeDtypeStruct((s, 2 * aw), BF16), jax.ShapeDtypeStruct((s, aw), F32),
                   jax.ShapeDtypeStruct((1, 2 * aw), F32), jax.ShapeDtypeStruct((1, aw), F32),
                   jax.ShapeDtypeStruct((1, aw), F32), jax.ShapeDtypeStruct(wm.shape, F32),
                   jax.ShapeDtypeStruct((CHUNK, 128), F32), jax.ShapeDtypeStruct((1, aw), F32),
                   jax.ShapeDtypeStruct((1, aw), F32), jax.ShapeDtypeStruct((1, aw), F32)],
        compiler_params=_params(("arbitrary",)))(dy, z, hc, vg, vb, wm, wmt, bm, sel, cg, cbeta)


def even_conv_bwd(dhc, z, cw, *, name):
    s, zw = z.shape
    aw = zw // 4
    nblk = s // ROWS
    hb_per = ROWS // HALO
    rc, lc = 8, 128

    def body(dc_ref, dn_ref, z_ref, cw_ref, dzb_ref, dbb_ref, dcw_ref, extd_ref):
        i = pl.program_id(0)

        @pl.when(i == 0)
        def _():
            dbb_ref[...] = jnp.zeros_like(dbb_ref)
            dcw_ref[...] = jnp.zeros_like(dcw_ref)

        extd_ref[0:ROWS, :] = dc_ref[...]
        extd_ref[ROWS:ROWS + HALO, :] = jnp.where(i < nblk - 1, dn_ref[...], 0.0)
        for c0 in range(0, aw, lc):
            lanes = slice(c0, c0 + lc)
            acc = [jnp.zeros((rc, lc), F32)] * CONV_K
            acc_a = jnp.zeros((rc, lc), F32)
            acc_g = jnp.zeros((rc, lc), F32)
            pending = None
            for r0 in range(0, ROWS, rc):
                a = z_ref[pl.ds(r0, rc), c0:c0 + lc]
                sg = _sigmoid(z_ref[pl.ds(r0, rc), aw + c0:aw + c0 + lc])
                hb = a * sg
                dhb = jnp.zeros((rc, lc), F32)
                for k in range(CONV_K):
                    d = extd_ref[pl.ds(r0 + CONV_K - 1 - k, rc), lanes]
                    dhb = dhb + cw_ref[k:k + 1, lanes] * d
                    acc[k] = acc[k] + hb * d
                da = dhb * sg
                dg = da * a * (1.0 - sg)
                acc_a = acc_a + da
                acc_g = acc_g + dg
                if pending is None:
                    pending = (da, dg)
                else:
                    rows = pl.ds(r0 - rc, 2 * rc)
                    dzb_ref[rows, c0:c0 + lc] = jnp.concatenate([pending[0], da], axis=0).astype(BF16)
                    dzb_ref[rows, aw + c0:aw + c0 + lc] = jnp.concatenate([pending[1], dg], axis=0).astype(BF16)
                    pending = None
            for k in range(CONV_K):
                dcw_ref[k:k + 1, lanes] += _colsum(acc[k])
            dbb_ref[:, c0:c0 + lc] += _colsum(acc_a)
            dbb_ref[:, aw + c0:aw + c0 + lc] += _colsum(acc_g)

    return pl.pallas_call(
        body, name=name, grid=(nblk,),
        in_specs=[pl.BlockSpec((ROWS, aw), lambda i: (i, 0)),
                  pl.BlockSpec((HALO, aw), lambda i: (jnp.minimum((i + 1) * hb_per, nblk * hb_per - 1), 0)),
                  pl.BlockSpec((ROWS, 2 * aw), lambda i: (i, 1)),
                  pl.BlockSpec((CONV_K, aw), lambda i: (0, 0))],
        out_specs=[pl.BlockSpec((ROWS, 2 * aw), lambda i: (i, 0)), pl.BlockSpec((1, 2 * aw), lambda i: (0, 0)),
                   pl.BlockSpec((CONV_K, aw), lambda i: (0, 0))],
        out_shape=[jax.ShapeDtypeStruct((s, 2 * aw), BF16), jax.ShapeDtypeStruct((1, 2 * aw), F32),
                   jax.ShapeDtypeStruct((CONV_K, aw), F32)],
        scratch_shapes=[pltpu.VMEM((ROWS + HALO, aw), F32)],
        compiler_params=_params(("arbitrary",)))(dhc, dhc, z, cw)


FFN_ROWS = 1024
FFN_CHUNK = 16


def _ffn_tile(f):
    for t in (256, 128):
        if f % t == 0:
            return t
    raise ValueError(f)


def _taps(ext_ref, w, b, r0, rows, halo):
    acc = b
    for k in range(FFN_K):
        acc = acc + w[k] * ext_ref[pl.ds(halo - (FFN_K - 1) + k + r0, rows), :]
    return acc


UP_HALO = 16
UP_SUB = 128


def ffn_up_mid(h, w_g, w_v, cw, cb, *, name):
    s, d = h.shape
    f = w_g.shape[1]
    tn = _ffn_tile(f)
    nj = f // tn
    ROWS = FFN_ROWS
    per = ROWS // UP_HALO

    def body(h_ref, hp_ref, wgm_ref, wvm_ref, wg_ref, wv_ref, bg_ref, bv_ref,
             act_ref, ug_ref, uv_ref, cg_ref, cv_ref, eg_ref, ev_ref):
        i = pl.program_id(0)
        wg = [wg_ref[k:k + 1, :] for k in range(FFN_K)]
        wv = [wv_ref[k:k + 1, :] for k in range(FFN_K)]
        bg, bv = bg_ref[...], bv_ref[...]
        sides = ((wgm_ref, eg_ref, ug_ref), (wvm_ref, ev_ref, uv_ref))
        tails = [jnp.where(i > 0, _dot(hp_ref[...], wm_ref[...], NN), 0.0) for wm_ref, _, _ in sides]

        def project(sub, tails):
            r0 = sub * UP_SUB
            src = h_ref[pl.ds(r0, UP_SUB), :]
            new_tails = []
            for (wm_ref, ext_ref, up_ref), tail in zip(sides, tails):
                u = _dot(src, wm_ref[...], NN)
                ext_ref[sub % 2, 0:UP_HALO, :] = tail
                ext_ref[sub % 2, UP_HALO:UP_HALO + UP_SUB, :] = u
                up_ref[pl.ds(r0, UP_SUB), :] = u
                new_tails.append(u[UP_SUB - UP_HALO:, :])
            return new_tails

        nsub = ROWS // UP_SUB
        tails = project(0, tails)
        for sub in range(nsub):
            if sub + 1 < nsub:
                tails = project(sub + 1, tails)
            for r0 in range(0, UP_SUB, FFN_CHUNK):
                rows = pl.ds(sub * UP_SUB + r0, FFN_CHUNK)
                gate = _taps(eg_ref.at[sub % 2], wg, bg, r0, FFN_CHUNK, UP_HALO)
                val = _taps(ev_ref.at[sub % 2], wv, bv, r0, FFN_CHUNK, UP_HALO)
                cg_ref[rows, :] = gate
                cv_ref[rows, :] = val
                act_ref[rows, :] = (gate * _sigmoid(gate) * val).astype(BF16)

    blk = pl.BlockSpec((ROWS, tn), lambda i, j: (i, j))
    wm = pl.BlockSpec((d, tn), lambda i, j: (0, j))
    wsp = lambda off: pl.BlockSpec((FFN_K, tn), lambda i, j: (0, j + off))
    bsp = lambda off: pl.BlockSpec((1, tn), lambda i, j: (0, j + off))
    return pl.pallas_call(
        body, name=name, grid=(s // ROWS, nj),
        in_specs=[pl.BlockSpec((ROWS, d), lambda i, j: (i, 0)),
                  pl.BlockSpec((UP_HALO, d), lambda i, j: (jnp.maximum(i * per - 1, 0), 0)),
                  wm, wm, wsp(0), wsp(nj), bsp(0), bsp(nj)],
        out_specs=[blk] * 5,
        out_shape=[jax.ShapeDtypeStruct((s, f), BF16)] + [jax.ShapeDtypeStruct((s, f), F32)] * 4,
        scratch_shapes=[pltpu.VMEM((2, UP_HALO + UP_SUB, tn), F32), pltpu.VMEM((2, UP_HALO + UP_SUB, tn), F32)],
        compiler_params=_params(("parallel", "parallel")))(h, h, w_g, w_v, cw, cw, cb, cb)


def ffn_mid_bwd(dact, up_g, up_v, conv_g, conv_v, cw, *, name):
    s, f = up_g.shape
    tn = _ffn_tile(f)
    nj = f // tn
    ROWS = FFN_ROWS
    nblk = s // ROWS
    per = ROWS // FHALO
    ext = ROWS + FHALO

    def body(da_ref, dan_ref, ug_ref, uv_ref, cg_ref, cv_ref, cgn_ref, cvn_ref, wg_ref, wv_ref,
             dug_ref, duv_ref, dwg_ref, dwv_ref, dbg_ref, dbv_ref, dg_ref, dv_ref):
        i = pl.program_id(1)

        @pl.when(i == 0)
        def _():
            for r in (dwg_ref, dwv_ref, dbg_ref, dbv_ref):
                r[...] = jnp.zeros_like(r)

        wg = [wg_ref[k:k + 1, :] for k in range(FFN_K)]
        wv = [wv_ref[k:k + 1, :] for k in range(FFN_K)]

        for r0, rows in [(r, FFN_CHUNK) for r in range(0, ROWS, FFN_CHUNK)] + [(ROWS, FHALO)]:
            if r0 < ROWS:
                gate, val, da = cg_ref[pl.ds(r0, rows), :], cv_ref[pl.ds(r0, rows), :], da_ref[pl.ds(r0, rows), :]
            else:
                gate, val, da = cgn_ref[...], cvn_ref[...], jnp.where(i < nblk - 1, dan_ref[...], 0.0)
            sg = _sigmoid(gate)
            dg_ref[pl.ds(r0, rows), :] = da * val * (sg * (1.0 + gate * (1.0 - sg)))
            dv_ref[pl.ds(r0, rows), :] = da * (gate * sg)

        def back(d_ref, w, u_ref, du_ref, dw_ref, db_ref):
            zero = jnp.zeros((FFN_CHUNK, tn), F32)
            acc = [zero] * FFN_K
            accb = zero
            for r0 in range(0, ROWS, FFN_CHUNK):
                d = [d_ref[pl.ds(r0 + FFN_K - 1 - k, FFN_CHUNK), :] for k in range(FFN_K)]
                u = u_ref[pl.ds(r0, FFN_CHUNK), :]
                du = w[0] * d[0]
                for k in range(1, FFN_K):
                    du = du + w[k] * d[k]
                du_ref[pl.ds(r0, FFN_CHUNK), :] = du.astype(BF16)
                acc = [acc[k] + u * d[k] for k in range(FFN_K)]
                accb = accb + d[FFN_K - 1]
            for k in range(FFN_K):
                dw_ref[k:k + 1, :] += _colsum(acc[k])
            db_ref[...] += _colsum(accb)

        back(dg_ref, wg, ug_ref, dug_ref, dwg_ref, dbg_ref)
        back(dv_ref, wv, uv_ref, duv_ref, dwv_ref, dbv_ref)

    cur = pl.BlockSpec((ROWS, tn), lambda j, i: (i, j))
    nxt = pl.BlockSpec((FHALO, tn), lambda j, i: (jnp.minimum((i + 1) * per, nblk * per - 1), j))
    wsp = lambda off: pl.BlockSpec((FFN_K, tn), lambda j, i: (0, j + off))
    bsp = pl.BlockSpec((1, tn), lambda j, i: (0, j))
    outs = pl.pallas_call(
        body, name=name, grid=(nj, nblk),
        in_specs=[cur, nxt, cur, cur, cur, cur, nxt, nxt, wsp(0), wsp(nj)],
        out_specs=[cur, cur, wsp(0), wsp(0), bsp, bsp],
        out_shape=[jax.ShapeDtypeStruct((s, f), BF16), jax.ShapeDtypeStruct((s, f), BF16),
                   jax.ShapeDtypeStruct((FFN_K, f), F32), jax.ShapeDtypeStruct((FFN_K, f), F32),
                   jax.ShapeDtypeStruct((1, f), F32), jax.ShapeDtypeStruct((1, f), F32)],
        scratch_shapes=[pltpu.VMEM((ext, tn), F32), pltpu.VMEM((ext, tn), F32)],
        compiler_params=_params(("parallel", "arbitrary")))(dact, dact, up_g, up_v, conv_g, conv_v, conv_g, conv_v,
                                                            cw, cw)
    dug, duv, dwg, dwv, dbg, dbv = outs
    return dug, duv, jnp.concatenate([dwg, dwv], axis=1), jnp.concatenate([dbg, dbv], axis=1)


def rope_tables(s):
    half = HEAD // 2
    lane = jnp.arange(128)
    j = lane % HEAD
    inv = ROPE_THETA ** (-(j % half).astype(F32) / half)
    ang = jnp.arange(s, dtype=F32)[:, None] * inv[None, :]
    sign = jnp.where(j < half, -1.0, 1.0).astype(F32)
    return jnp.cos(ang), jnp.sin(ang) * sign[None, :]


def _swap_halves(x):
    lane = lax.broadcasted_iota(jnp.int32, x.shape, 1)
    return jnp.where((lane % HEAD) < HEAD // 2, pltpu.roll(x, 128 - HEAD // 2, 1), pltpu.roll(x, HEAD // 2, 1))


def qkv_rope(h, w, cos, sin, *, name):
    s, k = h.shape
    d = w.shape[1] // 3
    tm = _divisors(s, (1024, 512))[0]
    scale = HEAD ** -0.5

    def body(a_ref, b_ref, c_ref, s_ref, o_ref):
        third = pl.program_id(1)
        o_ref[...] = _dot(a_ref[...], b_ref[...], NN)

        @pl.when(third < 2)
        def _():
            c = c_ref[...]
            sn = s_ref[...]
            factor = jnp.where(third == 0, scale, 1.0)
            for t in range(d // 128):
                cs = slice(t * 128, (t + 1) * 128)
                x = o_ref[:, cs]
                o_ref[:, cs] = (x * c + _swap_halves(x) * sn) * factor

    tab = pl.BlockSpec((tm, 128), lambda i, j: (i, 0))
    return pl.pallas_call(
        body, name=name, grid=(s // tm, 3),
        in_specs=[pl.BlockSpec((tm, k), lambda i, j: (i, 0)), pl.BlockSpec((k, d), lambda i, j: (0, j)), tab, tab],
        out_specs=pl.BlockSpec((tm, d), lambda i, j: (i, j)),
        out_shape=jax.ShapeDtypeStruct((s, 3 * d), F32),
        compiler_params=_params(("parallel", "parallel")))(h, w, cos, sin)


def rope_bwd(dq, dk, dv, cos, sin, *, name):
    s, d = dq.shape
    scale = HEAD ** -0.5

    def body(dq_ref, dk_ref, dv_ref, c_ref, s_ref, o_ref):
        c = c_ref[...]
        sn = s_ref[...]
        for t in range(d // 128):
            cs = slice(t * 128, (t + 1) * 128)
            gq = dq_ref[:, cs] * scale
            gk = dk_ref[:, cs]
            o_ref[:, t * 128:(t + 1) * 128] = (gq * c + _swap_halves(gq * sn)).astype(BF16)
            o_ref[:, d + t * 128:d + (t + 1) * 128] = (gk * c + _swap_halves(gk * sn)).astype(BF16)
        o_ref[:, 2 * d:3 * d] = dv_ref[...].astype(BF16)

    row = pl.BlockSpec((ROWS, d), lambda i: (i, 0))
    tab = pl.BlockSpec((ROWS, 128), lambda i: (i, 0))
    return pl.pallas_call(
        body, name=name, grid=(s // ROWS,),
        in_specs=[row, row, row, tab, tab],
        out_specs=pl.BlockSpec((ROWS, 3 * d), lambda i: (i, 0)),
        out_shape=jax.ShapeDtypeStruct((s, 3 * d), BF16),
        compiler_params=_params(("parallel",)))(dq, dk, dv, cos, sin)


ATT_T = BLOCK * max(DILATIONS)


FWD_GROUP = 2
ATT_GROUP = 4
FWD_QROWS = 128
BWD_QROWS = 64


def _unit_rows(r, j, dil):
    start = r + dil * BLOCK * j
    return pl.ds(start, BLOCK) if dil == 1 else pl.ds(start, BLOCK, stride=dil)


def _units():
    for bi, dil in enumerate(DILATIONS):
        nsub = ATT_T // (BLOCK * dil)
        for r in range(dil):
            for j in range(nsub):
                yield bi, dil, nsub, r, j


def _band(first_block, part, qrows):
    qi = lax.broadcasted_iota(jnp.int32, (qrows, 2 * BLOCK), 0) + part * qrows
    kj = lax.broadcasted_iota(jnp.int32, (qrows, 2 * BLOCK), 1)
    dist = BLOCK + qi - kj
    band = (dist >= 0) & (dist <= BLOCK)
    return band, band & (jnp.logical_not(first_block) | (kj >= BLOCK))


def _col(tile, h):
    return tile[:, h * HEAD:h * HEAD + 1]


def _keys(cur_ref, prev_ref, r, j, dil, nsub):
    cur = cur_ref[_unit_rows(r, j, dil), :]
    prev = cur_ref[_unit_rows(r, j - 1, dil), :] if j > 0 else prev_ref[_unit_rows(r, nsub - 1, dil), :]
    return jnp.concatenate([prev, cur], axis=0).astype(BF16)


def _att_specs(col_off=0):
    cur = pl.BlockSpec((ATT_T, 128), lambda n, p: (n, p + col_off))
    prv = pl.BlockSpec((ATT_T, 128), lambda n, p: (jnp.maximum(n - 1, 0), p + col_off))
    return cur, prv


def attn_fwd(qkv, *, name):
    s, d = qkv.shape[0], qkv.shape[1] // 3
    nt = s // ATT_T

    def body(q_ref, kc_ref, kp_ref, vc_ref, vp_ref, o_ref, lse_ref, acc_ref, m_ref, l_ref):
        n = pl.program_id(0)
        QROWS = FWD_QROWS
        nparts = BLOCK // QROWS
        bands = [_band(n == 0, part, QROWS) for part in range(nparts)]
        lo = _pair_masks()
        keep = _head_keep(BLOCK)
        nb = len(DILATIONS)
        tile = lambda cols: jnp.where(lo, jnp.concatenate(cols[:nparts], axis=0),
                                      jnp.concatenate(cols[nparts:], axis=0))

        def scores(unit):
            bi, dil, nsub, r, j = unit
            rows = _unit_rows(r, j, dil)
            kw = _keys(kc_ref, kp_ref, r, j, dil, nsub)
            vw = _keys(vc_ref, vp_ref, r, j, dil, nsub)
            qp = q_ref[rows, :].astype(BF16)
            sc2 = _dot(jnp.concatenate([qp * keep[0], qp * keep[1]], axis=0), kw, NT)
            return dict(bi=bi, j=j, rows=rows, vw=vw, sc2=sc2, m_old=m_ref[rows, :] if bi > 0 else None)

        def softmax(u):
            prs, new_m, new_l, alpha = [], [], [], []
            for c in range(2 * nparts):
                h, part = divmod(c, nparts)
                valid = bands[part][0 if u['j'] > 0 else 1]
                sc = jnp.where(valid, u['sc2'][c * QROWS:(c + 1) * QROWS], NEG)
                mx = jnp.max(sc, axis=-1, keepdims=True)
                if u['bi'] == 0:
                    m_new = mx
                else:
                    m_old = _col(u['m_old'][part * QROWS:(part + 1) * QROWS], h)
                    m_new = jnp.maximum(m_old, mx)
                    alpha.append(jnp.exp(m_old - m_new))
                pr = jnp.exp(sc - m_new)
                new_m.append(m_new)
                new_l.append(jnp.sum(pr, axis=-1, keepdims=True))
                prs.append(pr.astype(BF16))
            u.update(pr2=jnp.concatenate(prs, axis=0), new_m=new_m, new_l=new_l, alpha=alpha)

        def combine(u):
            rows, bi = u['rows'], u['bi']
            pv2 = _dot(u['pr2'], u['vw'], NN)
            m_t = tile(u['new_m'])
            l_t = tile(u['new_l'])
            acc_t = jnp.where(lo, pv2[:BLOCK], pv2[BLOCK:])
            if bi > 0:
                a_t = tile(u['alpha'])
                l_t = a_t * l_ref[rows, :] + l_t
                acc_t = a_t * acc_ref[rows, :] + acc_t
            if bi == nb - 1:
                o_ref[rows, :] = acc_t / l_t
                lse_ref[rows, :] = m_t + jnp.log(l_t)
            else:
                acc_ref[rows, :] = acc_t
                m_ref[rows, :] = m_t
                l_ref[rows, :] = l_t

        units = list(_units())
        for first in range(0, len(units), FWD_GROUP):
            pair = [scores(u) for u in units[first:first + FWD_GROUP]]
            for u in pair:
                softmax(u)
            for u in pair:
                combine(u)

    cur, _ = _att_specs()
    kcur, kprv = _att_specs(d // 128)
    vcur, vprv = _att_specs(2 * (d // 128))
    return pl.pallas_call(
        body, name=name, grid=(nt, d // 128), in_specs=[cur, kcur, kprv, vcur, vprv], out_specs=[cur, cur],
        out_shape=[jax.ShapeDtypeStruct((s, d), F32)] * 2,
        scratch_shapes=[pltpu.VMEM((ATT_T, 128), F32)] * 3,
        compiler_params=_params(("parallel", "parallel")))(qkv, qkv, qkv, qkv, qkv)


def attn_delta(do, o, *, name):
    s, d = do.shape

    def body(do_ref, o_ref, dl_ref):
        lane = lax.broadcasted_iota(jnp.int32, (ROWS, 128), 1)
        lo = lane < HEAD
        for p in range(d // 128):
            cs = slice(p * 128, (p + 1) * 128)
            pr = do_ref[:, cs] * o_ref[:, cs]
            s0 = jnp.sum(jnp.where(lo, pr, 0.0), axis=-1, keepdims=True)
            s1 = jnp.sum(jnp.where(lo, 0.0, pr), axis=-1, keepdims=True)
            dl_ref[:, cs] = jnp.where(lo, s0, s1)

    row = pl.BlockSpec((ROWS, d), lambda i: (i, 0))
    return pl.pallas_call(
        body, name=name, grid=(s // ROWS,), in_specs=[row, row], out_specs=row,
        out_shape=jax.ShapeDtypeStruct((s, d), F32),
        compiler_params=_params(("parallel",)))(do, o)


def attn_dq(qkv, do, lse, delta, *, name):
    s, d = do.shape
    nt = s // ATT_T

    def body(q_ref, kc_ref, kp_ref, vc_ref, vp_ref, do_ref, l_ref, dl_ref, dq_ref):
        n = pl.program_id(0)
        QROWS = BWD_QROWS
        nparts = BLOCK // QROWS
        bands = [_band(n == 0, part, QROWS) for part in range(nparts)]
        lo = _pair_masks()
        keep = _head_keep(BLOCK)
        def scores(unit):
            bi, dil, nsub, r, j = unit
            rows = _unit_rows(r, j, dil)
            kw = _keys(kc_ref, kp_ref, r, j, dil, nsub)
            vw = _keys(vc_ref, vp_ref, r, j, dil, nsub)
            qp = q_ref[rows, :].astype(BF16)
            dop = do_ref[rows, :].astype(BF16)
            sc2 = _dot(jnp.concatenate([qp * keep[0], qp * keep[1]], axis=0), kw, NT)
            dp2 = _dot(jnp.concatenate([dop * keep[0], dop * keep[1]], axis=0), vw, NT)
            return dict(bi=bi, j=j, rows=rows, kw=kw, sc2=sc2, dp2=dp2, lt=l_ref[rows, :], dt=dl_ref[rows, :])

        def softmax_bwd(u):
            dss = []
            for c in range(2 * nparts):
                h, part = divmod(c, nparts)
                valid = bands[part][0 if u['j'] > 0 else 1]
                cr = slice(c * QROWS, (c + 1) * QROWS)
                pr_rows = slice(part * QROWS, (part + 1) * QROWS)
                pr = jnp.where(valid, jnp.exp(u['sc2'][cr] - _col(u['lt'][pr_rows], h)), 0.0)
                dss.append((pr * (u['dp2'][cr] - _col(u['dt'][pr_rows], h))).astype(BF16))
            u['ds2'] = jnp.concatenate(dss, axis=0)

        def combine(u):
            rows = u['rows']
            dq2 = _dot(u['ds2'], u['kw'], NN)
            dq_t = jnp.where(lo, dq2[:BLOCK], dq2[BLOCK:])
            if u['bi'] > 0:
                dq_t = dq_t + dq_ref[rows, :]
            dq_ref[rows, :] = dq_t

        units = list(_units())
        for first in range(0, len(units), ATT_GROUP):
            pair = [scores(u) for u in units[first:first + ATT_GROUP]]
            for u in pair:
                softmax_bwd(u)
            for u in pair:
                combine(u)

    cur, _ = _att_specs()
    kcur, kprv = _att_specs(d // 128)
    vcur, vprv = _att_specs(2 * (d // 128))
    return pl.pallas_call(
        body, name=name, grid=(nt, d // 128), in_specs=[cur, kcur, kprv, vcur, vprv, cur, cur, cur], out_specs=cur,
        out_shape=jax.ShapeDtypeStruct((s, d), F32),
        compiler_params=_params(("parallel", "parallel")))(qkv, qkv, qkv, qkv, qkv, do, lse, delta)


def attn_dkv(qkv, do, lse, delta, *, name):
    s, d = do.shape
    nt = s // ATT_T

    def body(k_ref, v_ref, qc_ref, qn_ref, doc_ref, don_ref, lc_ref, ln_ref, dc_ref, dn_ref, dk_ref, dv_ref):
        n = pl.program_id(0)
        qi = lax.broadcasted_iota(jnp.int32, (BLOCK, BLOCK), 0)
        kj = lax.broadcasted_iota(jnp.int32, (BLOCK, BLOCK), 1)
        own = kj <= qi
        nxt = kj >= qi
        nxt_edge = nxt & (n < nt - 1)
        keep = _head_keep(BLOCK)

        def scores(unit):
            bi, dil, nsub, r, j = unit
            rows = _unit_rows(r, j, dil)
            inner = j + 1 < nsub
            nrows = _unit_rows(r, j + 1, dil) if inner else _unit_rows(r, 0, dil)
            kp = k_ref[rows, :].astype(BF16)
            vp = v_ref[rows, :].astype(BF16)
            far = not inner
            take = lambda c_ref, n_ref, nx: ((n_ref if far else c_ref)[nrows, :] if nx else c_ref[rows, :])
            qs = [take(qc_ref, qn_ref, nx).astype(BF16) for nx in (False, True)]
            dos = [take(doc_ref, don_ref, nx).astype(BF16) for nx in (False, True)]
            lts = [take(lc_ref, ln_ref, nx) for nx in (False, True)]
            dts = [take(dc_ref, dn_ref, nx) for nx in (False, True)]
            q4 = jnp.concatenate([qs[nx] * keep[h] for h in range(2) for nx in range(2)], axis=0)
            do4 = jnp.concatenate([dos[nx] * keep[h] for h in range(2) for nx in range(2)], axis=0)
            return dict(bi=bi, rows=rows, q4=q4, do4=do4, s4=_dot(q4, kp, NT), dp4=_dot(do4, vp, NT), lts=lts,
                        dts=dts, valids=(own, nxt if inner else nxt_edge))

        def softmax_bwd(u):
            prs, dss = [], []
            for c in range(4):
                h, nx = divmod(c, 2)
                cr = slice(c * BLOCK, (c + 1) * BLOCK)
                pr = jnp.where(u['valids'][nx], jnp.exp(u['s4'][cr] - _col(u['lts'][nx], h)), 0.0)
                prs.append(pr.astype(BF16))
                dss.append((pr * (u['dp4'][cr] - _col(u['dts'][nx], h))).astype(BF16))
            u.update(pr4=jnp.concatenate(prs, axis=0), ds4=jnp.concatenate(dss, axis=0))

        def combine(u):
            rows = u['rows']
            dv_t = _dot(u['pr4'], u['do4'], TN)
            dk_t = _dot(u['ds4'], u['q4'], TN)
            if u['bi'] > 0:
                dk_t = dk_t + dk_ref[rows, :]
                dv_t = dv_t + dv_ref[rows, :]
            dk_ref[rows, :] = dk_t
            dv_ref[rows, :] = dv_t

        units = list(_units())
        for first in range(0, len(units), ATT_GROUP):
            pair = [scores(u) for u in units[first:first + ATT_GROUP]]
            for u in pair:
                softmax_bwd(u)
            for u in pair:
                combine(u)

    cur = pl.BlockSpec((ATT_T, 128), lambda n, p: (n, p))
    nxt_spec = pl.BlockSpec((ATT_T, 128), lambda n, p: (jnp.minimum(n + 1, nt - 1), p))
    kcur, _ = _att_specs(d // 128)
    vcur, _ = _att_specs(2 * (d // 128))
    return pl.pallas_call(
        body, name=name, grid=(nt, d // 128),
        in_specs=[kcur, vcur, cur, nxt_spec, cur, nxt_spec, cur, nxt_spec, cur, nxt_spec], out_specs=[cur, cur],
        out_shape=[jax.ShapeDtypeStruct((s, d), F32)] * 2,
        compiler_params=_params(("parallel", "parallel")))(qkv, qkv, qkv, qkv, do, do, lse, lse, delta, delta)


def adamw(parts_list, w, m, v, *, name):
    nk = len(parts_list)
    npart, rk, c = parts_list[0].shape
    r = rk * nk
    assert w.shape == (r, c)
    tr = next(t for t in range(rk, 0, -8) if rk % t == 0 and (t * c * 4 <= 1024 * 1024 or t == 8))
    nbk = rk // tr

    def body(*refs):
        p_refs = refs[:nk]
        w_ref, m_ref, v_ref, g_ref, d_ref, nm_ref, nv_ref = refs[nk:]
        i = pl.program_id(0)
        g = None
        for kk, p_ref in enumerate(p_refs):
            gk = p_ref[0].astype(F32)
            for j in range(1, npart):
                gk = gk + p_ref[j].astype(F32)
            g = gk if g is None else jnp.where(i >= kk * nbk, gk, g)
        m2 = B1 * m_ref[...] + (1.0 - B1) * g
        v2 = B2 * v_ref[...] + (1.0 - B2) * (g * g)
        m_hat = m2 / (1.0 - B1 ** STEP)
        v_hat = v2 / (1.0 - B2 ** STEP)
        g_ref[...] = g
        d_ref[...] = -LR * (m_hat / (jnp.sqrt(v_hat) + ADAM_EPS) + WD * w_ref[...])
        nm_ref[...] = m2
        nv_ref[...] = v2

    blk = pl.BlockSpec((tr, c), lambda i: (i, 0))
    pspec = lambda kk: pl.BlockSpec((npart, tr, c), lambda i: (0, jnp.clip(i - kk * nbk, 0, nbk - 1), 0))
    return pl.pallas_call(
        body, name=name, grid=(r // tr,),
        in_specs=[pspec(kk) for kk in range(nk)] + [blk, blk, blk],
        out_specs=[blk] * 4, out_shape=[jax.ShapeDtypeStruct((r, c), F32)] * 4,
        compiler_params=_params(("parallel",)))(*parts_list, w, m, v)


def _my_index():
    return 4 * lax.axis_index("x") + 2 * lax.axis_index("y") + lax.axis_index("c")


def _peer_of(kk):
    x, y, c = lax.axis_index("x"), lax.axis_index("y"), lax.axis_index("c")
    return x ^ (kk >> 2), y ^ ((kk >> 1) & 1), c ^ (kk & 1)


def _peer_copy(t, kk, scatter, ins, lands, send_sems, recv_sems):
    px, py, pc = _peer_of(kk)
    me = _my_index()
    src = ins[t].at[4 * px + 2 * py + pc] if scatter[t] else ins[t]
    return pltpu.make_async_remote_copy(
        src_ref=src, dst_ref=lands[t].at[me], send_sem=send_sems.at[t * N_DEV + kk],
        recv_sem=recv_sems.at[t * N_DEV + kk], device_id=(px, py, pc), device_id_type=pl.DeviceIdType.MESH)


def _own_copy(t, scatter, ins, lands, own_sems):
    me = _my_index()
    return pltpu.make_async_copy(ins[t].at[me] if scatter[t] else ins[t], lands[t].at[me], own_sems.at[t])


_HBM = pl.BlockSpec(memory_space=pltpu.HBM)
_SEM = pl.BlockSpec(memory_space=pltpu.SEMAPHORE)
_EFFECT = pltpu.SideEffectType.DATAFLOW_SIDE_EFFECTING


def exchange_start(arrays, scatter, *, name):
    nt = len(arrays)
    land_shapes = [a.shape if scatter[t] else (N_DEV,) + a.shape for t, a in enumerate(arrays)]

    def body(*refs):
        ins, lands = refs[:nt], refs[nt:2 * nt]
        send_sems, recv_sems, own_sems = refs[2 * nt:2 * nt + 3]
        token = refs[-1]
        for kk in range(1, N_DEV):
            for t in range(nt):
                _peer_copy(t, kk, scatter, ins, lands, send_sems, recv_sems).start()
        for t in range(nt):
            _own_copy(t, scatter, ins, lands, own_sems).start()
        token[...] = jnp.zeros_like(token)

    sems = pltpu.SemaphoreType.DMA((nt * N_DEV,))
    outs = pl.pallas_call(
        body, name=name,
        out_shape=(sems, sems, pltpu.SemaphoreType.DMA((nt,)), *[pltpu.HBM(a.shape, a.dtype) for a in arrays],
                   *[pltpu.HBM(shp, a.dtype) for shp, a in zip(land_shapes, arrays)],
                   jax.ShapeDtypeStruct((8, 128), F32)),
        in_specs=[_HBM] * (2 * nt),
        out_specs=(_SEM, _SEM, _SEM, *[_HBM] * (2 * nt), pl.BlockSpec(memory_space=pltpu.VMEM)),
        input_output_aliases={i: 3 + i for i in range(2 * nt)},
        compiler_params=pltpu.CompilerParams(has_side_effects=_EFFECT),
    )(*[pltpu.with_memory_space_constraint(a, pltpu.HBM) for a in arrays],
      *[pltpu.with_memory_space_constraint(lax.empty(shp, a.dtype), pltpu.HBM) for shp, a in zip(land_shapes, arrays)])
    return (outs[:3], outs[3:3 + nt], outs[3 + nt:3 + 2 * nt], scatter), outs[-1]


def exchange_wait(handle, after, *, name):
    sems, thru, lands, scatter = handle
    nt = len(thru)

    def body(*refs):
        ins, lnd = refs[:nt], refs[nt:2 * nt]
        s_sems, r_sems, o_sems = refs[2 * nt:2 * nt + 3]
        for kk in range(1, N_DEV):
            for t in range(nt):
                cp = _peer_copy(t, kk, scatter, ins, lnd, s_sems, r_sems)
                cp.wait_send()
                cp.wait_recv()
        for t in range(nt):
            _own_copy(t, scatter, ins, lnd, o_sems).wait()

    outs = pl.pallas_call(
        body, name=name,
        out_shape=(*[pltpu.HBM(a.shape, a.dtype) for a in thru], *[pltpu.HBM(a.shape, a.dtype) for a in lands]),
        in_specs=[_HBM] * (2 * nt) + [_SEM, _SEM, _SEM, pl.BlockSpec(memory_space=pl.ANY)],
        out_specs=tuple([_HBM] * (2 * nt)),
        input_output_aliases={i: i for i in range(2 * nt)},
        compiler_params=pltpu.CompilerParams(has_side_effects=_EFFECT),
    )(*thru, *lands, *sems, after)
    return outs[nt:]


def _cols_from_shards(g):
    g = jnp.moveaxis(g, 0, -2)
    return g.reshape(g.shape[:-2] + (g.shape[-2] * g.shape[-1],))


def _cols_to_shards(w, nshards=N_DEV):
    w = w.reshape(w.shape[:-1] + (nshards, w.shape[-1] // nshards))
    return jnp.moveaxis(w, -2, 0)


def _half_shards(halves):
    return jnp.concatenate([_cols_to_shards(h[None], N_DEV // 2) for h in halves], axis=0)


def _ffn_fwd(x, h, w_up, cw, cb, get_w_down, down, tag):
    act, up_g, up_v, conv_g, conv_v = ffn_up_mid(h, w_up[0], w_up[1], cw, cb, name=f"{tag}_up_mid")
    w_down = get_w_down(act)
    return down(act, w_down, x), (h, up_g, up_v, conv_g, conv_v, act), w_down


def _ffn_bwd(dx, dxb, x, saved, g, w_up, cw, w_down, tag):
    h, up_g, up_v, conv_g, conv_v, act = saved
    dact = matmul(dxb, w_down, tb=True, name=f"{tag}_ddown")
    d_w_down = matmul_ta(act, dxb, name=f"{tag}_gdown")
    dug, duv, dcw, dcb = ffn_mid_bwd(dact, up_g, up_v, conv_g, conv_v, cw, name=f"{tag}_dmid")
    d_w_up = (matmul_ta(h, dug, name=f"{tag}_gup_g"), matmul_ta(h, duv, name=f"{tag}_gup_v"))
    dh = matmul(dug, w_up[0], tb=True, name=f"{tag}_dup_g")
    dx2, dxb2, dg = matmul_norm_bwd(duv, w_up[1], x, g, dx, res=dh, name=f"{tag}_dup_v_dnorm")
    return dx2, dxb2, dict(norm_g=dg, w_up=d_w_up, conv_w=dcw, conv_b=dcb, w_down=d_w_down)


def local_step(x0, tgt, a, weights, grads_out):
    s, d = x0.shape
    aw = a['even_v_ln_g'].shape[-1]
    causal = jnp.tril(jnp.ones((CHUNK, CHUNK), dtype=bool))
    wm = jnp.where(causal, a['even_w_s'][0], 0.0).astype(BF16)
    wmt = jnp.swapaxes(wm, 1, 2)
    bm = jnp.repeat(a['even_b_s'][0].T, HEAD, axis=1)
    sel = (jnp.arange(aw)[:, None] // HEAD == jnp.arange(128)[None, :]).astype(BF16)
    cos, sin = rope_tables(s)
    ffn_g, ffn_cb = a['ffn_norm_g'], a['ffn_conv_b']

    w0 = weights(0, None)
    w_in, conv_w, odd_g, ffn_cw = w0['w_in'], w0['conv_w'], w0['odd_g'], w0['ffn_cw']
    h0 = rms_fwd(x0, w0['even_g'], name="even_norm")
    z = matmul(h0, w_in, bias=a['even_b_in'], name="even_in")
    ycat, hc = even_mid_fwd(z, a['even_v_ln_g'], a['even_v_ln_b'], wm, bm, conv_w, a['even_conv_b'],
                            a['even_conv_ln_g'], a['even_conv_ln_b'], name="even_mid")
    w1 = weights(1, ycat)
    x1, h1 = matmul(ycat, w1['w_out'], res=x0, norm_g=ffn_g[0:1], name="even_out")
    (x2, h2), ffn0, w_down0 = _ffn_fwd(
        x1, h1, w1['w_up'], ffn_cw[0], ffn_cb[0:1], lambda act: weights(2, act)['w_down'],
        lambda act, w_down, x: matmul(act, w_down, res=x, norm_g=odd_g, name="ffn0_down"), "ffn0")
    w2 = weights(3, h2)
    qkv = qkv_rope(h2, w2['w_qkv'], cos, sin, name="odd_qkv_rope")
    o, lse = attn_fwd(qkv, name="attn_fwd")
    x3, h3 = matmul(o, w2['w_o'], res=x2, norm_g=ffn_g[1:2], name="odd_out")
    w3 = weights(4, x3)
    final_g = a['final_norm_g'].reshape(1, -1)
    (loss_t, dx, dxb, d_final_g), ffn1, _ = _ffn_fwd(
        x3, h3, w3['w_up'], ffn_cw[1], ffn_cb[1:2], lambda act: w3['w_down'],
        lambda act, w_down, x: final_loss_bwd(act, w_down, x, final_g, tgt, name="ffn1_down_loss"), "ffn1")

    dx, dxb, g1 = _ffn_bwd(dx, dxb, x3, ffn1, ffn_g[1:2], w3['w_up'], ffn_cw[1], w3['w_down'], "ffn1")
    dep = grads_out(0, dict(w_up=g1['w_up'], w_down=g1['w_down']))
    do = matmul(dxb, w2['w_o'], tb=True, dep=dep, name="odd_dout")
    d_w_o = matmul_ta(o, dxb, name="odd_gout")
    delta = attn_delta(do, o, name="attn_delta")
    dq = attn_dq(qkv, do, lse, delta, name="attn_dq")
    dk, dv = attn_dkv(qkv, do, lse, delta, name="attn_dkv")
    dqkv = rope_bwd(dq, dk, dv, cos, sin, name="rope_bwd")
    d_w_qkv = matmul_ta(h2, dqkv, name="odd_gqkv")
    dep = grads_out(1, dict(w_qkv=d_w_qkv, w_o=d_w_o))
    dx, dxb, d_odd_g = matmul_norm_bwd(dqkv, w2['w_qkv'], x2, odd_g, dx, dep=dep, name="odd_dqkv_dnorm")
    dx, dxb, g0 = _ffn_bwd(dx, dxb, x1, ffn0, ffn_g[0:1], w1['w_up'], ffn_cw[0], w_down0, "ffn0")
    dep = grads_out(2, dict(w_up=g0['w_up'], w_down=g0['w_down']))
    d_w_out = matmul_ta(ycat, dxb, dep=dep, name="even_gout")
    dep = grads_out(3, dict(w_out=d_w_out))
    dycat = matmul(dxb, w1['w_out'], tb=True, dep=dep, name="even_dout")
    (dza, dhc, dba, dvg, dvb, dwm, dbs, dcg, dcbeta, dcb) = even_mid_bwd_rows(
        dycat, z, hc, a['even_v_ln_g'], a['even_v_ln_b'], wm, wmt, bm, sel, a['even_conv_ln_g'],
        a['even_conv_ln_b'], name="even_dmid_rows")
    dzb, dbb, dcw = even_conv_bwd(dhc, z, conv_w, name="even_dmid_conv")
    nh = a['even_w_s'].shape[1]
    small_grads = {
        'even_b_in': jnp.concatenate([dba, dbb], axis=1), 'even_v_ln_g': dvg,
        'even_v_ln_b': dvb, 'even_w_s': jnp.where(causal, dwm, 0.0)[None], 'even_b_s': dbs[:, :nh].T[None],
        'even_conv_w': dcw[None], 'even_conv_b': dcb, 'even_conv_ln_g': dcg, 'even_conv_ln_b': dcbeta,
        'odd_norm_g': d_odd_g, 'ffn_norm_g': jnp.concatenate([g0['norm_g'], g1['norm_g']], axis=0),
        'ffn_conv_w': jnp.stack([g0['conv_w'], g1['conv_w']]),
        'ffn_conv_b': jnp.concatenate([g0['conv_b'], g1['conv_b']], axis=0),
        'final_norm_g': d_final_g.reshape(-1),
    }
    dep = grads_out(5, dict(small=small_grads))
    d_w_in = (matmul_ta(h0, dza, dep=dep, name="even_gin_a"), matmul_ta(h0, dzb, name="even_gin_b"))
    dep = grads_out(4, dict(w_in=d_w_in))
    dh0 = matmul(dza, w_in, tb=True, bk=0, dep=dep, name="even_din_a")
    grad_x, _, d_even_g = matmul_norm_bwd(dzb, w_in, x0, w0['even_g'], dx, bk=1, res=dh0, name="even_din_b_dnorm")
    last = {'even_norm_g': d_even_g}
    grads_out(6, dict(small=last))
    return loss_t, grad_x, {**last, **small_grads}


BIG = ['even_w_in', 'even_w_out', 'odd_w_qkv', 'odd_w_o', 'ffn_w_up', 'ffn_w_down']


def _as_tiles(flat, dtype=F32):
    return jnp.pad(flat, (0, (-flat.size) % 2048)).reshape(-1, 128).astype(dtype)


def kernel(*args):
    a = dict(zip(NAMES + ['loss_target'] + ['m_' + n for n in WEIGHTS] + ['v_' + n for n in WEIGHTS], args))
    x0 = a['x'][0]
    tgt = a['loss_target'][0]
    s, d = x0.shape
    me = _my_index()
    bf = lambda t: t.astype(BF16)

    small_local = _as_tiles(jnp.concatenate([a['even_conv_w'].reshape(-1), a['odd_norm_g'].reshape(-1),
                                             a['ffn_conv_w'].reshape(-1)]))
    stage_arrays = [
        [bf(a['even_w_in']), small_local],
        [bf(a['even_w_out']), bf(a['ffn_w_up'][0:1])],
        [bf(a['ffn_w_down'][0:1])],
        [bf(a['odd_w_qkv']), bf(a['odd_w_o'])],
        [bf(a['ffn_w_up'][1:2]), bf(a['ffn_w_down'][1:2])],
    ]
    started = [exchange_start(arrs, [False] * len(arrs), name=f"gather{i}_start") for i, arrs in enumerate(stage_arrays)]
    order = sum(tok[0, 0] for _, tok in started)

    def weights(stage, after):
        handle, tok = started[stage]
        full = exchange_wait(handle, tok if after is None else after, name=f"gather{stage}_wait")
        rows = lambda g: jnp.moveaxis(g, 0, 1).reshape(-1, d)
        halves = lambda g: (_cols_from_shards(g[:N_DEV // 2])[0], _cols_from_shards(g[N_DEV // 2:])[0])
        if stage == 0:
            gs = full[1].reshape(N_DEV, -1)
            n_cw, n_og, n_fw = a['even_conv_w'].size, a['odd_norm_g'].size, a['ffn_conv_w'].size
            return dict(
                w_in=_cols_from_shards(full[0])[0], even_g=a['even_norm_g'] + order,
                conv_w=_cols_from_shards(gs[:, :n_cw].reshape((N_DEV,) + a['even_conv_w'].shape))[0],
                odd_g=gs[:, n_cw:n_cw + n_og].reshape(1, -1),
                ffn_cw=_cols_from_shards(gs[:, n_cw + n_og:n_cw + n_og + n_fw].reshape((N_DEV,) + a['ffn_conv_w'].shape)))
        if stage == 1:
            return dict(w_out=rows(full[0]), w_up=halves(full[1]))
        if stage == 2:
            return dict(w_down=rows(full[0]))
        if stage == 3:
            return dict(w_qkv=_cols_from_shards(full[0])[0], w_o=rows(full[1]))
        return dict(w_up=halves(full[0]), w_down=rows(full[1]))

    sent = {}

    def grads_out(stage, g):
        to_rows = lambda w: w.reshape(N_DEV, 1, -1, d)
        if stage in (0, 2):
            pieces, scatter = [_half_shards(g['w_up']), to_rows(g['w_down'])], [True, True]
        elif stage == 1:
            pieces, scatter = [_cols_to_shards(g['w_qkv'][None]), to_rows(g['w_o'])], [True, True]
        elif stage == 3:
            pieces, scatter = [to_rows(g['w_out'])], [True]
        elif stage == 4:
            pieces, scatter = [_half_shards(g['w_in'])], [True]
        else:
            small = jnp.concatenate([g['small'][n].reshape(-1) for n in g['small']])
            pieces, scatter = [_as_tiles(small, BF16)], [False]
        sent[stage], tok = exchange_start(pieces, scatter, name=f"grads{stage}_start")
        return tok

    loss_t, grad_x, small_grads = local_step(x0, tgt, a, weights, grads_out)
    loss = lax.psum(loss_t[0, 0], ("x", "y", "c"))
    received = {stage: exchange_wait(handle, grad_x, name=f"grads{stage}_wait") for stage, handle in sent.items()}

    results = {}
    big_parts = {'even_w_in': [received[4][0]], 'even_w_out': [received[3][0]], 'odd_w_qkv': [received[1][0]],
                 'odd_w_o': [received[1][1]], 'ffn_w_up': [received[2][0], received[0][0]],
                 'ffn_w_down': [received[2][1], received[0][1]]}
    for n in BIG:
        shp = a[n].shape
        flat = lambda t: t.reshape(-1, shp[-1])
        outs = adamw([p.reshape(N_DEV, -1, shp[-1]) for p in big_parts[n]], flat(a[n]), flat(a['m_' + n]),
                     flat(a['v_' + n]), name=f"adamw_{n}")
        results[n] = [t.reshape(shp) for t in outs]

    small_names = list(small_grads)
    n_small = sum(small_grads[n].size for n in small_names)
    n_last = small_grads[small_names[0]].size
    rs = jnp.concatenate([received[6][0].reshape(N_DEV, -1)[:, :n_last],
                          received[5][0].reshape(N_DEV, -1)[:, :n_small - n_last]], axis=1)
    parts, offs = [], 0
    for n in small_names:
        full = small_grads[n].shape
        piece = rs[:, offs:offs + small_grads[n].size].reshape((N_DEV,) + full)
        offs += small_grads[n].size
        shp = a[n].shape
        if shp != full:
            width = shp[-1]
            piece = lax.dynamic_slice_in_dim(piece, me * width, width, axis=piece.ndim - 1)
        parts.append(piece.reshape(N_DEV, -1))
    parts = jnp.concatenate(parts, axis=1)
    pad = (-parts.shape[1]) % 2048
    cat = lambda pre: _as_tiles(jnp.concatenate([a[pre + n].reshape(-1) for n in small_names]))
    outs = adamw([jnp.pad(parts, ((0, 0), (0, pad))).reshape(N_DEV, -1, 128)], cat(''), cat('m_'), cat('v_'),
                 name="adamw_small")
    offs = 0
    for n in small_names:
        size = a[n].size
        results[n] = [t.reshape(-1)[offs:offs + size].reshape(a[n].shape) for t in outs]
        offs += size

    out = [loss, grad_x[None]]
    for i in range(4):
        out += [results[n][i] for n in WEIGHTS]
    return tuple(out)
```
